```python
import math
import jax
import jax.numpy as jnp
from jax import lax
import numpy as np

D_MODEL = 1024
BATCH = 1
SEQ = 16384
DEPTH = 1
DEC_BATCH = 8
DEC_SEQ = 32
PAST_LEN = 1024

CHUNK = 64
MIX_WIDTH = D_MODEL
DN_HEADS = 4
DN_KEY_DIM = 128
DN_VALUE_DIM = 128
CONV_W = 4
CONV_CH = DN_HEADS * (2 * DN_KEY_DIM + DN_VALUE_DIM)
DELTA_BLOCK = CHUNK
SB_HEADS = 8
SB_HEAD_DIM = 64
SB_QUERY_BLOCK = 128
N_GROUPS = 4
EXPERTS_PER_GROUP = 8
N_EXPERTS = N_GROUPS * EXPERTS_PER_GROUP
TOP_K = 2
D_EXPERT = D_MODEL // 2
MOE_BLOCK = 128
EPS = 1e-6

OFF_Q_A = 0
OFF_K_A = DN_HEADS * DN_KEY_DIM
OFF_V_A = 2 * DN_HEADS * DN_KEY_DIM
OFF_Z_A = CONV_CH
OFF_B_A = OFF_Z_A + DN_HEADS * DN_VALUE_DIM
OFF_G_A = OFF_B_A + DN_HEADS
OFF_Q_B = OFF_G_A + DN_HEADS
OFF_K_B = OFF_Q_B + SB_HEADS * SB_HEAD_DIM
OFF_V_B = OFF_K_B + SB_HEADS * SB_HEAD_DIM
IN_COLS = OFF_V_B + SB_HEADS * SB_HEAD_DIM
OUT_IN = DN_HEADS * DN_VALUE_DIM + SB_HEADS * SB_HEAD_DIM

kernel_name = 'hybrid_streaming_encoder_step'


def rms_norm(x, gain):
    xf = x.astype(jnp.float32)
    y = xf * lax.rsqrt(jnp.mean(xf * xf, axis=-1, keepdims=True) + EPS)
    return (y * gain.astype(jnp.float32)).astype(x.dtype)


def l2_normalize(x):
    xf = x.astype(jnp.float32)
    return xf * lax.rsqrt(jnp.sum(xf * xf, axis=-1, keepdims=True) + EPS)


def causal_conv(hist, w, t):
    out = hist[:, 0:t] * w[0]
    for i in range(1, CONV_W):
        out = out + hist[:, i:i + t] * w[i]
    return out


def gated_delta_rule(q, k, v, g, beta, s0):
    f32 = jnp.float32
    b, t, h, _ = q.shape
    dv = v.shape[-1]
    c = DELTA_BLOCK
    pad = (-t) % c
    n = (t + pad) // c

    def to_blocks(a):
        a = jnp.pad(a.astype(f32), [(0, 0), (0, pad)] + [(0, 0)] * (a.ndim - 2))
        a = a.reshape((b, n, c) + a.shape[2:])
        return jnp.moveaxis(a, 3, 1)

    qc, kc, vc, bc = to_blocks(q), to_blocks(k), to_blocks(v), to_blocks(beta)
    gc = lax.cumsum(to_blocks(g), axis=3)
    incl = jnp.tril(jnp.ones((c, c), bool))
    strict = jnp.tril(jnp.ones((c, c), bool), -1)
    decay = jnp.exp(jnp.where(incl, gc[..., :, None] - gc[..., None, :], -jnp.inf))
    kbeta = kc * bc[..., None]
    lower = jnp.where(strict, jnp.einsum('bhnid,bhnjd->bhnij', kbeta, kc) * decay, 0.0)
    rhs = jnp.concatenate([vc * bc[..., None], kbeta * jnp.exp(gc)[..., None]], axis=-1)
    sol = lax.linalg.triangular_solve(lower, rhs, left_side=True, lower=True, unit_diagonal=True)
    u, w = sol[..., :dv], sol[..., dv:]
    attn = jnp.where(incl, jnp.einsum('bhnid,bhnjd->bhnij', qc, kc) * decay, 0.0)
    q_dec = qc * jnp.exp(gc)[..., None]
    k_dec = kc * jnp.exp(gc[..., -1:] - gc)[..., None]
    a_blk = jnp.exp(gc[..., -1])

    def step(s, xs):
        q_i, k_i, u_i, w_i, at_i, a_i = xs
        v_new = u_i - jnp.einsum('bhck,bhkv->bhcv', w_i, s)
        o = jnp.einsum('bhck,bhkv->bhcv', q_i, s) + jnp.einsum('bhcj,bhjv->bhcv', at_i, v_new)
        s = s * a_i[..., None, None] + jnp.einsum('bhck,bhcv->bhkv', k_i, v_new)
        return s, o

    xs = (jnp.moveaxis(q_dec, 2, 0), jnp.moveaxis(k_dec, 2, 0), jnp.moveaxis(u, 2, 0),
          jnp.moveaxis(w, 2, 0), jnp.moveaxis(attn, 2, 0), jnp.moveaxis(a_blk, 2, 0))
    s_fin, o = lax.scan(step, s0.astype(f32), xs)
    o = o.transpose(1, 0, 3, 2, 4).reshape(b, n * c, h, dv)[:, :t]
    return o, s_fin


def sb_block(q, k, v, q_pos, k_pos):
    f32 = jnp.float32
    z = jnp.einsum('bqhd,bkhd->bhqk', q.astype(f32), k.astype(f32)) * (SB_HEAD_DIM ** -0.5)
    earlier = k_pos[None, :] < q_pos[:, None]
    log_keep = jnp.where(earlier, jax.nn.log_sigmoid(-z), 0.0)
    log_w = jax.nn.log_sigmoid(z) + lax.cumsum(log_keep, axis=3, reverse=True) - log_keep
    a = jnp.where(earlier, jnp.exp(log_w), 0.0)
    return jnp.einsum('bhqk,bkhd->bqhd', a, v.astype(f32))


def stick_breaking(q, k, v, q_pos, k_pos):
    b, t, h, d = q.shape
    if t % SB_QUERY_BLOCK != 0:
        return sb_block(q, k, v, q_pos, k_pos)
    n = t // SB_QUERY_BLOCK
    qs = q.reshape(b, n, SB_QUERY_BLOCK, h, d).transpose(1, 0, 2, 3, 4)
    ps = q_pos.reshape(n, SB_QUERY_BLOCK)
    out = lax.map(lambda a: sb_block(a[0], k, v, a[1], k_pos), (qs, ps))
    return out.transpose(1, 0, 2, 3, 4).reshape(b, t, h, d)


def hier_moe(h, p):
    f32 = jnp.float32
    b, t, d = h.shape
    n_tok = b * t
    m = n_tok * TOP_K
    x = h.reshape(n_tok, d)
    logit_g = (x @ p['w_router_group']).astype(f32) + p['b_router_group'].astype(f32)
    grp = jnp.argmax(logit_g, axis=-1)
    p_grp = jnp.take_along_axis(jax.nn.softmax(logit_g, axis=-1), grp[:, None], axis=1)[:, 0]
    logit_e = ((x @ p['w_router_expert']).astype(f32) + p['b_router_expert'].astype(f32)).reshape(
        n_tok, N_GROUPS, EXPERTS_PER_GROUP)
    logit_e = jnp.take_along_axis(logit_e, grp[:, None, None], axis=1)[:, 0]
    top_p, top_i = lax.top_k(jax.nn.softmax(logit_e, axis=-1), TOP_K)
    wts = (p_grp[:, None] * top_p / jnp.sum(top_p, axis=-1, keepdims=True)).reshape(m)
    eid = (grp[:, None] * EXPERTS_PER_GROUP + top_i).reshape(m)
    tok = jnp.arange(m, dtype=jnp.int32) // TOP_K
    order = jnp.argsort(eid)
    e_sorted = eid[order]
    counts = jnp.zeros((N_EXPERTS,), jnp.int32).at[eid].add(1)
    padded = (counts + MOE_BLOCK - 1) // MOE_BLOCK * MOE_BLOCK
    pad_end = jnp.cumsum(padded)
    pad_start = pad_end - padded
    start = jnp.cumsum(counts) - counts
    dest = pad_start[e_sorted] + jnp.arange(m, dtype=jnp.int32) - start[e_sorted]
    n_blocks = -(-m // MOE_BLOCK) + N_EXPERTS
    n_rows = n_blocks * MOE_BLOCK
    row_tok = jnp.full((n_rows,), n_tok, jnp.int32).at[dest].set(tok[order])
    row_w = jnp.zeros((n_rows,), f32).at[dest].set(wts[order])
    blk_e = jnp.minimum(jnp.searchsorted(pad_end, jnp.arange(n_blocks) * MOE_BLOCK, side='right'),
                        N_EXPERTS - 1)
    x_rows = jnp.concatenate([x, jnp.zeros((1, d), x.dtype)], axis=0)[row_tok].reshape(
        n_blocks, MOE_BLOCK, d)
    w_gate, w_up, w_down = p['w_gate'], p['w_up'], p['w_down']

    def expert_block(args):
        xb, e = args
        return (jax.nn.silu(xb @ w_gate[e]) * (xb @ w_up[e])) @ w_down[e]

    y_rows = lax.map(expert_block, (x_rows, blk_e)).reshape(n_rows, d)
    out = jnp.zeros((n_tok + 1, d), f32).at[row_tok].add(y_rows.astype(f32) * row_w[:, None])
    return out[:n_tok].reshape(b, t, d).astype(h.dtype)


def encoder_layer(x, c, conv_hist, s0, k_past, v_past, p):
    f32 = jnp.float32
    b, t, _ = x.shape
    past = 0 if k_past is None else k_past.shape[1]
    mod = jax.nn.silu(c) @ p['w_ada'] + p['b_ada']
    sh_m, sc_m, gt_m, sh_f, sc_f, gt_f = jnp.split(mod[:, None, :], 6, axis=-1)

    h = rms_norm(x, p['g_pre_mix']) * (1 + sc_m) + sh_m
    proj = h @ p['w_in']

    hist = jnp.concatenate([conv_hist.astype(proj.dtype), proj[..., :CONV_CH]], axis=1)
    conv = jax.nn.silu(causal_conv(hist, p['conv_w'], t))
    new_conv = hist[:, -(CONV_W - 1):]
    qa = l2_normalize(conv[..., OFF_Q_A:OFF_K_A].reshape(b, t, DN_HEADS, DN_KEY_DIM)) * (DN_KEY_DIM ** -0.5)
    ka = l2_normalize(conv[..., OFF_K_A:OFF_V_A].reshape(b, t, DN_HEADS, DN_KEY_DIM))
    va = conv[..., OFF_V_A:CONV_CH].reshape(b, t, DN_HEADS, DN_VALUE_DIM)
    za = proj[..., OFF_Z_A:OFF_B_A].reshape(b, t, DN_HEADS, DN_VALUE_DIM)
    beta = jax.nn.sigmoid(proj[..., OFF_B_A:OFF_G_A].astype(f32))
    g = -jnp.exp(p['a_log'].astype(f32)) * jax.nn.softplus(
        proj[..., OFF_G_A:OFF_Q_B].astype(f32) + p['dt_bias'].astype(f32))
    oa, s_new = gated_delta_rule(qa, ka, va, g, beta, s0)
    oa = rms_norm(oa, p['onorm_a']) * jax.nn.silu(za.astype(f32))

    qb = proj[..., OFF_Q_B:OFF_K_B].reshape(b, t, SB_HEADS, SB_HEAD_DIM)
    kb = proj[..., OFF_K_B:OFF_V_B].reshape(b, t, SB_HEADS, SB_HEAD_DIM)
    vb = proj[..., OFF_V_B:IN_COLS].reshape(b, t, SB_HEADS, SB_HEAD_DIM)
    if k_past is None:
        k_all, v_all = kb, vb
    else:
        k_all = jnp.concatenate([k_past.astype(kb.dtype), kb], axis=1)
        v_all = jnp.concatenate([v_past.astype(vb.dtype), vb], axis=1)
    ob = stick_breaking(qb, k_all, v_all, past + jnp.arange(t), jnp.arange(past + t))
    ob = rms_norm(ob, p['onorm_b'])

    merged = jnp.concatenate([oa.reshape(b, t, -1), ob.reshape(b, t, -1)], axis=-1).astype(x.dtype)
    x = x + gt_m * rms_norm(merged @ p['w_out'], p['g_post_mix'])

    h = rms_norm(x, p['g_pre_ffn']) * (1 + sc_f) + sh_f
    x = x + gt_f * rms_norm(hier_moe(h, p), p['g_post_ffn'])
    return x, kb, vb, s_new, new_conv


def setup_inputs(seed: int = 0) -> dict:
    key = jax.random.key(seed)
    ks = jax.random.split(key, 28)
    f32 = jnp.float32
    L, D = DEPTH, D_MODEL

    def normal(k, shape, scale):
        return jax.random.normal(k, shape, f32) * scale

    dt = jnp.exp(jax.random.uniform(ks[17], (L, DN_HEADS), f32, math.log(1e-3), math.log(1e-1)))
    return {
        'x_prompt': normal(ks[0], (BATCH, SEQ, D), 1.0),
        'x_sample': normal(ks[1], (DEC_BATCH, DEC_SEQ, D), 1.0),
        'c_prompt': normal(ks[2], (BATCH, D), 1.0),
        'c_sample': normal(ks[3], (DEC_BATCH, D), 1.0),
        'cache_k': normal(ks[4], (L, DEC_BATCH, PAST_LEN, SB_HEADS, SB_HEAD_DIM), 1.0),
        'cache_v': normal(ks[5], (L, DEC_BATCH, PAST_LEN, SB_HEADS, SB_HEAD_DIM), 1.0),
        'state_delta': normal(ks[6], (L, DEC_BATCH, DN_HEADS, DN_KEY_DIM, DN_VALUE_DIM), 0.5),
        'state_conv': normal(ks[7], (L, DEC_BATCH, CONV_W - 1, CONV_CH), 1.0),
        'w_ada': normal(ks[8], (L, D, 6 * D), 0.5 * D ** -0.5),
        'b_ada': normal(ks[9], (L, 6 * D), 0.01),
        'g_pre_mix': 1.0 + normal(ks[10], (L, D), 0.01),
        'g_post_mix': 1.0 + normal(ks[11], (L, D), 0.01),
        'g_pre_ffn': 1.0 + normal(ks[12], (L, D), 0.01),
        'g_post_ffn': 1.0 + normal(ks[13], (L, D), 0.01),
        'w_in': normal(ks[14], (L, D, IN_COLS), D ** -0.5),
        'conv_w': normal(ks[15], (L, CONV_W, CONV_CH), CONV_W ** -0.5),
        'a_log': jnp.log(jax.random.uniform(ks[16], (L, DN_HEADS), f32, 1.0, 16.0)),
        'dt_bias': dt + jnp.log(-jnp.expm1(-dt)),
        'onorm_a': 1.0 + normal(ks[18], (L, DN_VALUE_DIM), 0.01),
        'onorm_b': 1.0 + normal(ks[19], (L, SB_HEAD_DIM), 0.01),
        'w_out': normal(ks[20], (L, OUT_IN, D), OUT_IN ** -0.5),
        'w_router_group': normal(ks[21], (L, D, N_GROUPS), D ** -0.5),
        'b_router_group': normal(ks[22], (L, N_GROUPS), 0.01),
        'w_router_expert': normal(ks[23], (L, D, N_EXPERTS), D ** -0.5),
        'b_router_expert': normal(ks[24], (L, N_EXPERTS), 0.01),
        'w_gate': normal(ks[25], (L, N_EXPERTS, D, D_EXPERT), D ** -0.5),
        'w_up': normal(ks[26], (L, N_EXPERTS, D, D_EXPERT), D ** -0.5),
        'w_down': normal(ks[27], (L, N_EXPERTS, D_EXPERT, D), D_EXPERT ** -0.5),
    }


def reference(x_prompt, x_sample, c_prompt, c_sample, cache_k, cache_v, state_delta, state_conv,
              w_ada, b_ada, g_pre_mix, g_post_mix, g_pre_ffn, g_post_ffn, w_in, conv_w, a_log,
              dt_bias, onorm_a, onorm_b, w_out, w_router_group, b_router_group, w_router_expert,
              b_router_expert, w_gate, w_up, w_down):
    bp = x_prompt.shape[0]
    y_p, y_s = x_prompt, x_sample
    kp_l, vp_l, sp_l, cp_l, ks_l, vs_l, ss_l, cs_l = [], [], [], [], [], [], [], []
    for l in range(DEPTH):
        p = dict(w_ada=w_ada[l], b_ada=b_ada[l], g_pre_mix=g_pre_mix[l], g_post_mix=g_post_mix[l],
                 g_pre_ffn=g_pre_ffn[l], g_post_ffn=g_post_ffn[l], w_in=w_in[l], conv_w=conv_w[l],
                 a_log=a_log[l], dt_bias=dt_bias[l], onorm_a=onorm_a[l], onorm_b=onorm_b[l],
                 w_out=w_out[l], w_router_group=w_router_group[l], b_router_group=b_router_group[l],
                 w_router_expert=w_router_expert[l], b_router_expert=b_router_expert[l],
                 w_gate=w_gate[l], w_up=w_up[l], w_down=w_down[l])
        y_p, kp, vp, sp, cp = encoder_layer(
            y_p, c_prompt, jnp.zeros((bp, CONV_W - 1, CONV_CH), x_prompt.dtype),
            jnp.zeros((bp, DN_HEADS, DN_KEY_DIM, DN_VALUE_DIM), jnp.float32), None, None, p)
        y_s, k_s, v_s, s_s, c_s = encoder_layer(
            y_s, c_sample, state_conv[l], state_delta[l], cache_k[l], cache_v[l], p)
        kp_l.append(kp); vp_l.append(vp); sp_l.append(sp); cp_l.append(cp)
        ks_l.append(k_s); vs_l.append(v_s); ss_l.append(s_s); cs_l.append(c_s)
    return (y_p, y_s, jnp.stack(kp_l), jnp.stack(vp_l), jnp.stack(sp_l), jnp.stack(cp_l),
            jnp.stack(ks_l), jnp.stack(vs_l), jnp.stack(ss_l), jnp.stack(cs_l))
```

```python
import functools
import math

import jax
import jax.numpy as jnp
from jax import lax
from jax.experimental import pallas as pl
from jax.experimental.pallas import tpu as pltpu

F32 = jnp.float32
BF16 = jnp.bfloat16

D_MODEL = 1024
DN_HEADS = 4
DN_DIM = 128
CONV_W = 4
CONV_CH = DN_HEADS * 3 * DN_DIM
DELTA_BLOCK = 64
SB_HEADS = 8
SB_DIM = 64
SB_WIDTH = SB_HEADS * SB_DIM
N_GROUPS = 4
EXPERTS_PER_GROUP = 8
N_EXPERTS = N_GROUPS * EXPERTS_PER_GROUP
D_EXPERT = D_MODEL // 2
MOE_BLOCK = 128
EPS = 1e-6

LANES = 128
KEY_TILE = 128
EXP_ZERO_BELOW = -104.0
VMEM_LIMIT = 56 * 1024 * 1024


def _cparams(sem):
    return pltpu.CompilerParams(dimension_semantics=sem, vmem_limit_bytes=VMEM_LIMIT)


def _split(a):
    hi = a.astype(BF16)
    lo = (a - hi.astype(F32)).astype(BF16)
    return hi, lo


def _dot(a, b, dims=(((1,), (0,)), ((), ()))):
    return lax.dot_general(a, b, dims, preferred_element_type=F32)


def _dot3(a, b, dims=(((1,), (0,)), ((), ()))):
    ah, al = _split(a)
    bh, bl = _split(b)
    return _dot(ah, bh, dims) + (_dot(al, bh, dims) + _dot(ah, bl, dims))


_NT = (((1,), (1,)), ((), ()))
_TN = (((0,), (0,)), ((), ()))


def _silu(x):
    return x * jax.nn.sigmoid(x)


def _softplus(x):
    return jnp.maximum(x, 0.0) + jnp.log(1.0 + jnp.exp(-jnp.abs(x)))


def _ada_kernel(c_ref, w_ref, b_ref, o_ref):
    s = _silu(c_ref[...]).astype(BF16)
    o_ref[...] = _dot(s, w_ref[...].astype(BF16)) + b_ref[...]


def _ada(c_all, w_ada, b_ada):
    rows = c_all.shape[0]
    n = w_ada.shape[1]
    tn = 1024
    return pl.pallas_call(
        _ada_kernel,
        grid=(n // tn,),
        in_specs=[pl.BlockSpec((rows, D_MODEL), lambda j: (0, 0)),
                  pl.BlockSpec((D_MODEL, tn), lambda j: (0, j)),
                  pl.BlockSpec((1, tn), lambda j: (0, j))],
        out_specs=pl.BlockSpec((rows, tn), lambda j: (0, j)),
        out_shape=jax.ShapeDtypeStruct((rows, n), F32),
        compiler_params=_cparams(("arbitrary",)),
        name="ada",
    )(c_all, w_ada, b_ada.reshape(1, n))


def _rms(x, gain):
    return x * lax.rsqrt(jnp.mean(x * x, axis=-1, keepdims=True) + EPS) * gain


def _proj_kernel(x_ref, mod_ref, g_ref, wm_ref, wbh_ref, wbl_ref,
                 a_ref, z_ref, bg_ref, q_ref, k_ref, v_ref, k16_ref, v16_ref):
    mod = mod_ref[0]
    h = _rms(x_ref[...], g_ref[...]) * (1.0 + mod[1:2]) + mod[0:1]
    hh, hl = _split(h)
    p = _dot(hh, wm_ref[...])
    a_ref[...] = p[:, 0:CONV_CH]
    z_ref[...] = p[:, CONV_CH:CONV_CH + 512]
    o = CONV_CH + 512
    q_ref[...] = (p[:, o:o + 512] * (SB_DIM ** -0.5)).astype(BF16)
    k = p[:, o + 512:o + 1024]
    v = p[:, o + 1024:o + 1536]
    k_ref[...] = k
    v_ref[...] = v
    k16_ref[...] = k.astype(BF16)
    v16_ref[...] = v.astype(BF16)
    wbh = wbh_ref[...]
    bg_ref[...] = _dot(hh, wbh) + (_dot(hl, wbh) + _dot(hh, wbl_ref[...]))


def _proj(x2d, mod8, g_pre, w_main, wb_hi, wb_lo, tm, seq_rows, mod_row0):
    n = x2d.shape[0]
    nm = w_main.shape[1]
    row = lambda i: (i, 0)
    const = lambda i: (0, 0)
    modmap = lambda i: (mod_row0 + (i * tm) // seq_rows, 0, 0)
    outs = [(CONV_CH, F32), (512, F32), (LANES, F32), (512, BF16), (512, F32), (512, F32),
            (512, BF16), (512, BF16)]
    return pl.pallas_call(
        _proj_kernel,
        grid=(n // tm,),
        in_specs=[pl.BlockSpec((tm, D_MODEL), row),
                  pl.BlockSpec((1, 8, D_MODEL), modmap),
                  pl.BlockSpec((1, D_MODEL), const),
                  pl.BlockSpec((D_MODEL, nm), const),
                  pl.BlockSpec((D_MODEL, LANES), const),
                  pl.BlockSpec((D_MODEL, LANES), const)],
        out_specs=[pl.BlockSpec((tm, w), row) for w, _ in outs],
        out_shape=[jax.ShapeDtypeStruct((n, w), dt) for w, dt in outs],
        compiler_params=_cparams(("arbitrary",)),
        name="proj",
    )(x2d, mod8, g_pre, w_main, wb_hi, wb_lo)


def _delta_kernel(a_ref, z_ref, bg_ref, hist0_ref, s0_ref, cw_ref, alog_ref, dtb_ref, on_ref,
                  o_ref, sfin_ref, hist_sc, s_sc, *, chunk, n_chunks):
    t_idx = pl.program_id(1)
    tt = chunk * n_chunks

    @pl.when(t_idx == 0)
    def _():
        hist_sc[...] = hist0_ref[0]
        s_sc[...] = s0_ref[0]

    x = a_ref[0]
    xx = jnp.concatenate([hist_sc[...], x], axis=0)
    cw = cw_ref[...]
    conv = x * cw[CONV_W - 1:CONV_W]
    for s in range(1, CONV_W):
        conv = conv + pltpu.roll(xx, s, 0)[8:] * cw[CONV_W - 1 - s:CONV_W - s]
    conv = _silu(conv)
    hist_sc[...] = x[tt - 8:tt]

    bg = bg_ref[0]
    lane = lax.broadcasted_iota(jnp.int32, (1, LANES), 1)
    g_lane = (lane >= DN_HEADS) & (lane < 2 * DN_HEADS)
    neg_a = jnp.where(g_lane, -jnp.exp(alog_ref[...]), 0.0)
    beta_all = jax.nn.sigmoid(bg)
    g_all = neg_a * _softplus(bg + dtb_ref[...])

    ri = lax.broadcasted_iota(jnp.int32, (chunk, chunk), 0)
    ci = lax.broadcasted_iota(jnp.int32, (chunk, chunk), 1)
    incl = ri >= ci
    strict = ri > ci
    tri = incl.astype(F32)
    eye = (ri == ci).astype(F32)
    ones_cc = jnp.ones((chunk, chunk), F32)
    onorm = on_ref[...]

    for c in range(n_chunks):
        r0 = c * chunk
        gc_all = _dot3(tri, g_all[r0:r0 + chunk])
        diag = jnp.concatenate([eye * gc_all[:, DN_HEADS + h:DN_HEADS + h + 1] for h in range(DN_HEADS)],
                               axis=1)
        gc_rows = _dot3(ones_cc, diag)
        for h in range(DN_HEADS):
            lo = h * DN_DIM
            q = conv[r0:r0 + chunk, lo:lo + DN_DIM]
            k = conv[r0:r0 + chunk, 512 + lo:512 + lo + DN_DIM]
            v = conv[r0:r0 + chunk, 1024 + lo:1024 + lo + DN_DIM]
            q = q * lax.rsqrt(jnp.sum(q * q, axis=-1, keepdims=True) + EPS) * (DN_DIM ** -0.5)
            k = k * lax.rsqrt(jnp.sum(k * k, axis=-1, keepdims=True) + EPS)
            beta = beta_all[r0:r0 + chunk, h:h + 1]
            gcol = gc_all[:, DN_HEADS + h:DN_HEADS + h + 1]
            grow = gc_rows[:, h * chunk:(h + 1) * chunk]
            glast = gcol[chunk - 1:chunk, :]
            decay = jnp.where(incl, jnp.exp(jnp.minimum(gcol - grow, 0.0)), 0.0)
            kbeta = k * beta
            lower = jnp.where(strict, _dot3(kbeta, k, _NT) * decay, 0.0)
            eg = jnp.exp(gcol)
            sol = jnp.concatenate([v * beta, kbeta * eg], axis=1)
            sol = sol - _dot3(lower, sol)
            lp = lower
            p = 2
            while p < chunk:
                lp = _dot3(lp, lp)
                sol = sol + _dot3(lp, sol)
                p *= 2
            u = sol[:, :DN_DIM]
            w = sol[:, DN_DIM:]
            attn = jnp.where(incl, _dot3(q, k, _NT) * decay, 0.0)
            q_dec = q * eg
            k_dec = k * jnp.exp(glast - gcol)
            s = s_sc[h]
            v_new = u - _dot3(w, s)
            o = _dot3(q_dec, s) + _dot3(attn, v_new)
            s_sc[h] = s * jnp.exp(glast) + _dot3(k_dec, v_new, _TN)
            zg = z_ref[0, r0:r0 + chunk, lo:lo + DN_DIM]
            o = _rms(o, onorm) * _silu(zg)
            o_ref[0, r0:r0 + chunk, lo:lo + DN_DIM] = o.astype(BF16)

    @pl.when(t_idx == pl.num_programs(1) - 1)
    def _():
        sfin_ref[0] = s_sc[...]


def _delta(a_in, z, bg, hist8, s0, conv_w8, alog_row, dtb_row, onorm_a, chunk, n_chunks):
    b, t, _ = a_in.shape
    tt = chunk * n_chunks
    tile = lambda bi, ti: (bi, ti, 0)
    per_b3 = lambda bi, ti: (bi, 0, 0)
    per_b4 = lambda bi, ti: (bi, 0, 0, 0)
    const = lambda bi, ti: (0, 0)
    kern = functools.partial(_delta_kernel, chunk=chunk, n_chunks=n_chunks)
    return pl.pallas_call(
        kern,
        grid=(b, t // tt),
        in_specs=[pl.BlockSpec((1, tt, CONV_CH), tile),
                  pl.BlockSpec((1, tt, 512), tile),
                  pl.BlockSpec((1, tt, LANES), tile),
                  pl.BlockSpec((1, 8, CONV_CH), per_b3),
                  pl.BlockSpec((1, DN_HEADS, DN_DIM, DN_DIM), per_b4),
                  pl.BlockSpec((8, CONV_CH), const),
                  pl.BlockSpec((1, LANES), const),
                  pl.BlockSpec((1, LANES), const),
                  pl.BlockSpec((1, DN_DIM), const)],
        out_specs=[pl.BlockSpec((1, tt, 512), tile),
                   pl.BlockSpec((1, DN_HEADS, DN_DIM, DN_DIM), per_b4)],
        out_shape=[jax.ShapeDtypeStruct((b, t, 512), BF16),
                   jax.ShapeDtypeStruct((b, DN_HEADS, DN_DIM, DN_DIM), F32)],
        scratch_shapes=[pltpu.VMEM((8, CONV_CH), F32),
                        pltpu.VMEM((DN_HEADS, DN_DIM, DN_DIM), F32)],
        compiler_params=_cparams(("arbitrary", "arbitrary")),
        name="delta",
    )(a_in, z, bg, hist8, s0, conv_w8, alog_row, dtb_row, onorm_a)


def _sb_kernel(q_ref, k0_ref, k1_ref, k2_ref, v0_ref, v1_ref, v2_ref, k_hbm, v_hbm, on_ref,
               o_ref, kbuf, vbuf, qsel, acc, carry, sem, *, bq, n_pad, q_off):
    b = pl.program_id(0)
    i = pl.program_id(1)
    qend = q_off + (i + 1) * bq
    last_tile = qend // KEY_TILE - 1
    n_tiles = last_tile + 1

    acc[...] = jnp.zeros_like(acc)
    carry[...] = jnp.zeros_like(carry)
    half_lane = lax.broadcasted_iota(jnp.int32, (bq, LANES), 1) < SB_DIM
    for p in range(SB_HEADS // 2):
        qf = q_ref[0, :, p * LANES:(p + 1) * LANES].astype(F32)
        qsel[2 * p] = jnp.where(half_lane, qf, 0.0).astype(BF16)
        qsel[2 * p + 1] = jnp.where(half_lane, 0.0, qf).astype(BF16)

    qpos =q_off + i * bq + lax.broadcasted_iota(jnp.int32, (bq, KEY_TILE), 0)
    col = lax.broadcasted_iota(jnp.int32, (bq, KEY_TILE), 1)
    lane = lax.broadcasted_iota(jnp.int32, (1, LANES), 1)
    low_half = lane < SB_DIM
    rj = lax.broadcasted_iota(jnp.int32, (KEY_TILE, 2 * KEY_TILE), 0)
    cj = lax.broadcasted_iota(jnp.int32, (KEY_TILE, 2 * KEY_TILE), 1)
    suffix = ((rj > cj) | (cj >= KEY_TILE)).astype(BF16)

    def load_tile(t):
        for kk, (kr, vr) in enumerate(((k0_ref, v0_ref), (k1_ref, v1_ref), (k2_ref, v2_ref))):
            @pl.when(t == kk)
            def _():
                for p in range(SB_HEADS // 2):
                    kbuf[p] = kr[0, :, p * LANES:(p + 1) * LANES]
                    vbuf[p] = vr[0, :, p * LANES:(p + 1) * LANES]

        @pl.when(t >= 3)
        def _():
            start = pl.multiple_of((last_tile - t) * KEY_TILE, KEY_TILE)
            copies = []
            for p in range(SB_HEADS // 2):
                for src, dst in ((k_hbm, kbuf), (v_hbm, vbuf)):
                    cp = pltpu.make_async_copy(
                        src.at[b, pl.ds(start, KEY_TILE), pl.ds(p * LANES, LANES)], dst.at[p], sem)
                    cp.start()
                    copies.append(cp)
            for cp in copies:
                cp.wait()

    def body(state):
        t, _ = state
        load_tile(t)
        kpos = (last_tile - t) * KEY_TILE + col
        mask = (kpos < qpos) & (kpos >= n_pad)
        worst = jnp.full((bq, KEY_TILE), -jnp.inf, F32)
        for p in range(SB_HEADS // 2):
            kp = kbuf[p]
            vp = vbuf[p]
            for hh in range(2):
                head = 2 * p + hh
                sel = low_half if hh == 0 else jnp.logical_not(low_half)
                z = _dot(qsel[head], kp, _NT)
                sp = _softplus(z)
                log_keep = jnp.where(mask, -sp, 0.0)
                lk_hi, lk_lo = _split(log_keep)
                cs = _dot(lk_hi, suffix) + _dot(lk_lo, suffix)
                c_old = carry[head]
                log_w = (z - sp) + cs[:, :KEY_TILE] + c_old
                a = jnp.where(mask, jnp.exp(log_w), 0.0)
                pv = _dot(a.astype(BF16), vp)
                acc[p] = acc[p] + jnp.where(sel, pv, 0.0)
                c_new = c_old + cs[:, KEY_TILE:]
                carry[head] = c_new
                worst = jnp.maximum(worst, c_new)
        done = (jnp.max(worst) < EXP_ZERO_BELOW).astype(jnp.int32)
        return t + 1, done

    def cond(state):
        t, done = state
        return (t < n_tiles) & (done == 0)

    lax.while_loop(cond, body, (jnp.int32(0), jnp.int32(0)))

    onb = on_ref[...]
    for p in range(SB_HEADS // 2):
        o = acc[p]
        sq = o * o
        s_lo = jnp.sum(jnp.where(low_half, sq, 0.0), axis=-1, keepdims=True)
        s_hi = jnp.sum(jnp.where(low_half, 0.0, sq), axis=-1, keepdims=True)
        ms = jnp.where(low_half, s_lo, s_hi) * (1.0 / SB_DIM)
        o_ref[0, :, p * LANES:(p + 1) * LANES] = (o * lax.rsqrt(ms + EPS) * onb).astype(BF16)


def _sb_attn(q16, k16p, v16p, onorm_b2, bq, n_pad):
    b, tq, _ = q16.shape
    tkp = k16p.shape[1]
    q_off = tkp - tq
    assert tkp % KEY_TILE == 0 and KEY_TILE % bq == 0 and tq % bq == 0 and q_off % bq == 0

    def kmap(back):
        def f(bi, i):
            last = (q_off + (i + 1) * bq) // KEY_TILE - 1
            return (bi, jnp.maximum(last - back, 0), 0)
        return f

    qmap = lambda bi, i: (bi, i, 0)
    kern = functools.partial(_sb_kernel, bq=bq, n_pad=n_pad, q_off=q_off)
    kspec = [pl.BlockSpec((1, KEY_TILE, SB_WIDTH), kmap(back)) for back in range(3)]
    return pl.pallas_call(
        kern,
        grid=(b, tq // bq),
        in_specs=[pl.BlockSpec((1, bq, SB_WIDTH), qmap)] + kspec + kspec
                 + [pl.BlockSpec(memory_space=pl.ANY), pl.BlockSpec(memory_space=pl.ANY),
                    pl.BlockSpec((1, LANES), lambda bi, i: (0, 0))],
        out_specs=pl.BlockSpec((1, bq, SB_WIDTH), qmap),
        out_shape=jax.ShapeDtypeStruct((b, tq, SB_WIDTH), BF16),
        scratch_shapes=[pltpu.VMEM((SB_HEADS // 2, KEY_TILE, LANES), BF16),
                        pltpu.VMEM((SB_HEADS // 2, KEY_TILE, LANES), BF16),
                        pltpu.VMEM((SB_HEADS, bq, LANES), BF16),
                        pltpu.VMEM((SB_HEADS // 2, bq, LANES), F32),
                        pltpu.VMEM((SB_HEADS, bq, KEY_TILE), F32),
                        pltpu.SemaphoreType.DMA(())],
        compiler_params=_cparams(("arbitrary", "arbitrary")),
        name="sb_attn",
    )(q16, k16p, k16p, k16p, v16p, v16p, v16p, k16p, v16p, onorm_b2)


def _post_kernel(oa_ref, ob_ref, x_ref, mod_ref, gpm_ref, gpf_ref, wo_ref, wrh_ref, wrl_ref, br_ref,
                 x1_ref, h2_ref, route_ref):
    mod = mod_ref[0]
    mix = _dot(oa_ref[...], wo_ref[0:512, :]) + _dot(ob_ref[...], wo_ref[512:1024, :])
    x1 = x_ref[...] + mod[2:3] * _rms(mix, gpm_ref[...])
    x1_ref[...] = x1
    h2 = _rms(x1, gpf_ref[...]) * (1.0 + mod[4:5]) + mod[3:4]
    h2_ref[...] = h2
    hh, hl = _split(h2)
    wrh = wrh_ref[...]
    logits = _dot(hh, wrh) + (_dot(hl, wrh) + _dot(hh, wrl_ref[...])) + br_ref[...]
    lane = lax.broadcasted_iota(jnp.int32, logits.shape, 1).astype(F32)
    neg = -jnp.inf
    nl = float(LANES)
    lg = jnp.where(lane < N_GROUPS, logits, neg)
    gmax = jnp.max(lg, axis=-1, keepdims=True)
    grp = jnp.min(jnp.where(lg == gmax, lane, nl), axis=-1, keepdims=True)
    p_grp = 1.0 / jnp.sum(jnp.exp(lg - gmax), axis=-1, keepdims=True)
    first = N_GROUPS + grp * EXPERTS_PER_GROUP
    le = jnp.where((lane >= first) & (lane < first + EXPERTS_PER_GROUP), logits, neg)
    emax = jnp.max(le, axis=-1, keepdims=True)
    i1 = jnp.min(jnp.where(le == emax, lane, nl), axis=-1, keepdims=True)
    esum = jnp.sum(jnp.exp(le - emax), axis=-1, keepdims=True)
    le2 = jnp.where(lane == i1, neg, le)
    e2max = jnp.max(le2, axis=-1, keepdims=True)
    i2 = jnp.min(jnp.where(le2 == e2max, lane, nl), axis=-1, keepdims=True)
    p1 = 1.0 / esum
    p2 = jnp.exp(e2max - emax) / esum
    w1 = p_grp * p1 / (p1 + p2)
    w2 = p_grp * p2 / (p1 + p2)
    out = jnp.where(lane == 0.0, i1 - N_GROUPS, 0.0)
    out = jnp.where(lane == 1.0, i2 - N_GROUPS, out)
    out = jnp.where(lane == 2.0, w1, out)
    out = jnp.where(lane == 3.0, w2, out)
    route_ref[...] = out


def _post(oa16, ob16, x2d, mod8, g_post_mix, g_pre_ffn, w_out16, wr_hi, wr_lo, b_r, tm, seq_rows, mod_row0):
    n = x2d.shape[0]
    row = lambda i: (i, 0)
    const = lambda i: (0, 0)
    modmap = lambda i: (mod_row0 + (i * tm) // seq_rows, 0, 0)
    return pl.pallas_call(
        _post_kernel,
        grid=(n // tm,),
        in_specs=[pl.BlockSpec((tm, 512), row),
                  pl.BlockSpec((tm, 512), row),
                  pl.BlockSpec((tm, D_MODEL), row),
                  pl.BlockSpec((1, 8, D_MODEL), modmap),
                  pl.BlockSpec((1, D_MODEL), const),
                  pl.BlockSpec((1, D_MODEL), const),
                  pl.BlockSpec((D_MODEL, D_MODEL), const),
                  pl.BlockSpec((D_MODEL, LANES), const),
                  pl.BlockSpec((D_MODEL, LANES), const),
                  pl.BlockSpec((1, LANES), const)],
        out_specs=[pl.BlockSpec((tm, D_MODEL), row),
                   pl.BlockSpec((tm, D_MODEL), row),
                   pl.BlockSpec((tm, LANES), row)],
        out_shape=[jax.ShapeDtypeStruct((n, D_MODEL), F32),
                   jax.ShapeDtypeStruct((n, D_MODEL), F32),
                   jax.ShapeDtypeStruct((n, LANES), F32)],
        compiler_params=_cparams(("arbitrary",)),
        name="post",
    )(oa16, ob16, x2d, mod8, g_post_mix, g_pre_ffn, w_out16, wr_hi, wr_lo, b_r)


def _moe_kernel(blk_e_ref, nvalid_ref, src_ref, dst_ref, roww_ref, wg_ref, wu_ref, wd_ref, x_hbm,
                y_hbm, xbuf, ybuf, gsem, ssem):
    i = pl.program_id(0)
    nv = nvalid_ref[i]

    def gather_row(r):
        return pltpu.make_async_copy(x_hbm.at[pl.ds(src_ref[0, 0, r], 1), :], xbuf.at[pl.ds(r, 1), :], gsem)

    def scatter_row(r):
        return pltpu.make_async_copy(ybuf.at[pl.ds(r, 1), :], y_hbm.at[pl.ds(dst_ref[0, 0, r], 1), :], ssem)

    def move_rows(row_copy, whole_copy):
        @pl.when(nv == MOE_BLOCK)
        def _():
            for r in range(MOE_BLOCK):
                row_copy(r).start()
            whole_copy.wait()

        @pl.when(nv < MOE_BLOCK)
        def _():
            def start(r, c):
                row_copy(r).start()
                return c

            def wait(r, c):
                row_copy(r).wait()
                return c

            lax.fori_loop(0, nv, start, 0)
            lax.fori_loop(0, nv, wait, 0)

    @pl.when(i == 0)
    def _():
        xbuf[...] = jnp.zeros_like(xbuf)

    @pl.when(nv > 0)
    def _():
        move_rows(gather_row, pltpu.make_async_copy(x_hbm.at[pl.ds(0, MOE_BLOCK), :], xbuf, gsem))
        xb = xbuf[...].astype(BF16)
        g = _dot(xb, wg_ref[0])
        u = _dot(xb, wu_ref[0])
        hmid = (_silu(g) * u).astype(BF16)
        ybuf[...] = _dot(hmid, wd_ref[0]) * roww_ref[...]
        move_rows(scatter_row, pltpu.make_async_copy(ybuf, y_hbm.at[pl.ds(0, MOE_BLOCK), :], ssem))


def _moe(blk_e, nvalid, row_src, row_dst, row_w, wg16, wu16, wd16, x_rows, n_out_rows):
    n_blocks = blk_e.shape[0]
    wmap = lambda i, be, nu: (be[i], 0, 0)
    grid_spec = pltpu.PrefetchScalarGridSpec(
        num_scalar_prefetch=2,
        grid=(n_blocks,),
        in_specs=[pl.BlockSpec((1, 1, MOE_BLOCK), lambda i, be, nu: (i, 0, 0), memory_space=pltpu.SMEM),
                  pl.BlockSpec((1, 1, MOE_BLOCK), lambda i, be, nu: (i, 0, 0), memory_space=pltpu.SMEM),
                  pl.BlockSpec((MOE_BLOCK, 1), lambda i, be, nu: (i, 0)),
                  pl.BlockSpec((1, D_MODEL, D_EXPERT), wmap),
                  pl.BlockSpec((1, D_MODEL, D_EXPERT), wmap),
                  pl.BlockSpec((1, D_EXPERT, D_MODEL), wmap),
                  pl.BlockSpec(memory_space=pl.ANY)],
        out_specs=pl.BlockSpec(memory_space=pl.ANY),
        scratch_shapes=[pltpu.VMEM((MOE_BLOCK, D_MODEL), F32),
                        pltpu.VMEM((MOE_BLOCK, D_MODEL), F32),
                        pltpu.SemaphoreType.DMA(()),
                        pltpu.SemaphoreType.DMA(())])
    return pl.pallas_call(
        _moe_kernel,
        grid_spec=grid_spec,
        out_shape=jax.ShapeDtypeStruct((n_out_rows, D_MODEL), F32),
        compiler_params=_cparams(("arbitrary",)),
        name="moe",
    )(blk_e, nvalid, row_src.reshape(n_blocks, 1, MOE_BLOCK), row_dst.reshape(n_blocks, 1, MOE_BLOCK),
      row_w.reshape(n_blocks * MOE_BLOCK, 1), wg16, wu16, wd16, x_rows)


def _combine_kernel(y0_ref, y1_ref, x1_ref, mod_ref, g_ref, o_ref):
    mod = mod_ref[0]
    moe = y0_ref[0] + y1_ref[0]
    o_ref[...] = x1_ref[...] + mod[5:6] * _rms(moe, g_ref[...])


def _combine(y_slots, x1, mod8, g_post_ffn, tm, seq_rows, mod_row0, row0):
    n = x1.shape[0]
    blk0 = row0 // tm
    row = lambda i: (i, 0)
    modmap = lambda i: (mod_row0 + (i * tm) // seq_rows, 0, 0)
    return pl.pallas_call(
        _combine_kernel,
        grid=(n // tm,),
        in_specs=[pl.BlockSpec((1, tm, D_MODEL), lambda i: (0, blk0 + i, 0)),
                  pl.BlockSpec((1, tm, D_MODEL), lambda i: (1, blk0 + i, 0)),
                  pl.BlockSpec((tm, D_MODEL), row),
                  pl.BlockSpec((1, 8, D_MODEL), modmap),
                  pl.BlockSpec((1, D_MODEL), lambda i: (0, 0))],
        out_specs=pl.BlockSpec((tm, D_MODEL), row),
        out_shape=jax.ShapeDtypeStruct((n, D_MODEL), F32),
        compiler_params=_cparams(("arbitrary",)),
        name="combine",
    )(y_slots, y_slots, x1, mod8, g_post_ffn)


def _dispatch_plan(route, n_tok, slot_rows):
    m = 2 * n_tok
    eid = route[:, 0:2].astype(jnp.int32).reshape(m)
    wts = route[:, 2:4].reshape(m)
    flat = jnp.arange(m, dtype=jnp.int32)
    order = jnp.argsort(eid)
    e_sorted = eid[order]
    counts = jnp.zeros((N_EXPERTS,), jnp.int32).at[eid].add(1)
    padded = (counts + MOE_BLOCK - 1) // MOE_BLOCK * MOE_BLOCK
    pad_end = jnp.cumsum(padded)
    pad_start = pad_end - padded
    start = jnp.cumsum(counts) - counts
    dest = pad_start[e_sorted] + flat - start[e_sorted]
    n_blocks = -(-m // MOE_BLOCK) + N_EXPERTS
    n_rows = n_blocks * MOE_BLOCK
    src_tok = order // 2
    src_slot = order % 2
    row_src = jnp.zeros((n_rows,), jnp.int32).at[dest].set(src_tok)
    row_dst = jnp.zeros((n_rows,), jnp.int32).at[dest].set(src_slot * slot_rows + src_tok)
    row_w = jnp.zeros((n_rows,), F32).at[dest].set(wts[order])
    blk_start = jnp.arange(n_blocks, dtype=jnp.int32) * MOE_BLOCK
    blk_e = jnp.minimum(jnp.searchsorted(pad_end, blk_start, side='right'), N_EXPERTS - 1).astype(jnp.int32)
    nvalid = jnp.clip(counts[blk_e] - (blk_start - pad_start[blk_e]), 0, MOE_BLOCK).astype(jnp.int32)
    return blk_e, nvalid, row_src, row_dst, row_w


def _layer(x_p, x_s, c_p, c_s, k_past, v_past, s0_s, conv_s, p):
    bp, tp, d = x_p.shape
    bs, ts, _ = x_s.shape
    n_p, n_s = bp * tp, bs * ts
    n_tok = n_p + n_s

    n_seq = bp + bs
    c_all = jnp.zeros((16, d), F32).at[:n_seq].set(jnp.concatenate([c_p, c_s], axis=0))
    mod = _ada(c_all, p['w_ada'], p['b_ada'])
    mod8 = jnp.pad(mod.reshape(16, 6, d), ((0, 0), (0, 2), (0, 0)))

    w_in = p['w_in']
    o_z, o_b, o_q = CONV_CH, CONV_CH + 512, CONV_CH + 512 + 2 * DN_HEADS
    w_main = jnp.concatenate([w_in[:, :o_b], w_in[:, o_q:]], axis=1).astype(BF16)
    wb = jnp.pad(w_in[:, o_b:o_q], ((0, 0), (0, LANES - 2 * DN_HEADS)))
    wb_hi = wb.astype(BF16)
    wb_lo = (wb - wb_hi.astype(F32)).astype(BF16)
    g_pre_mix = p['g_pre_mix'].reshape(1, d)

    conv_w8 = jnp.pad(p['conv_w'], ((0, 8 - CONV_W), (0, 0)))
    pad_g = lambda a: jnp.pad(a.reshape(1, DN_HEADS), ((0, 0), (DN_HEADS, LANES - 2 * DN_HEADS)))
    alog_row, dtb_row = pad_g(p['a_log']), pad_g(p['dt_bias'])
    onorm_a = p['onorm_a'].reshape(1, DN_DIM)
    onorm_b2 = jnp.tile(p['onorm_b'].reshape(1, SB_DIM), (1, 2))

    w_out16 = p['w_out'].astype(BF16)
    wr = jnp.pad(jnp.concatenate([p['w_router_group'], p['w_router_expert']], axis=1),
                 ((0, 0), (0, LANES - N_GROUPS - N_EXPERTS)))
    wr_hi = wr.astype(BF16)
    wr_lo = (wr - wr_hi.astype(F32)).astype(BF16)
    b_r = jnp.pad(jnp.concatenate([p['b_router_group'], p['b_router_expert']]).reshape(1, -1),
                  ((0, 0), (0, LANES - N_GROUPS - N_EXPERTS)))
    g_post_mix = p['g_post_mix'].reshape(1, d)
    g_pre_ffn = p['g_pre_ffn'].reshape(1, d)
    g_post_ffn = p['g_post_ffn'].reshape(1, d)

    def mixer(x, tm, seq_rows, mod_row0, hist8, s0, k_old, v_old, chunk, n_chunks, bq):
        b, t, _ = x.shape
        x2d = x.reshape(b * t, d)
        a_in, z, bg, q16, kb, vb, k16, v16 = _proj(x2d, mod8, g_pre_mix, w_main, wb_hi, wb_lo,
                                                    tm, seq_rows, mod_row0)
        r3 = lambda a: a.reshape(b, t, a.shape[-1])
        oa16, s_new = _delta(r3(a_in), r3(z), r3(bg), hist8, s0, conv_w8, alog_row, dtb_row, onorm_a,
                             chunk, n_chunks)
        k16, v16 = r3(k16), r3(v16)
        if k_old is not None:
            k16 = jnp.concatenate([k_old.reshape(b, -1, SB_WIDTH).astype(BF16), k16], axis=1)
            v16 = jnp.concatenate([v_old.reshape(b, -1, SB_WIDTH).astype(BF16), v16], axis=1)
        n_pad = (-k16.shape[1]) % KEY_TILE
        k16 = jnp.pad(k16, ((0, 0), (n_pad, 0), (0, 0)))
        v16 = jnp.pad(v16, ((0, 0), (n_pad, 0), (0, 0)))
        ob16 = _sb_attn(r3(q16), k16, v16, onorm_b2, bq, n_pad)
        x1, h2, route = _post(oa16.reshape(b * t, 512), ob16.reshape(b * t, 512), x2d, mod8,
                              g_post_mix, g_pre_ffn, w_out16, wr_hi, wr_lo, b_r, tm, seq_rows, mod_row0)
        new_conv = r3(a_in)[:, t - (CONV_W - 1):, :]
        return (x1, h2, route, kb.reshape(b, t, SB_HEADS, SB_DIM), vb.reshape(b, t, SB_HEADS, SB_DIM),
                s_new, new_conv)

    zero_hist = jnp.zeros((bp, 8, CONV_CH), F32)
    zero_s = jnp.zeros((bp, DN_HEADS, DN_DIM, DN_DIM), F32)
    hist_s = jnp.pad(conv_s, ((0, 0), (8 - (CONV_W - 1), 0), (0, 0)))
    tm_p = min(256, tp)
    nc_p = max(1, min(4, tp // DELTA_BLOCK))
    x1p, h2p, rp, kp, vp, sp, cp = mixer(x_p, tm_p, tp, 0, zero_hist, zero_s, None, None,
                                          min(DELTA_BLOCK, tp), nc_p, min(KEY_TILE, tp))
    x1s, h2s, rs, ks, vs, ss, cs = mixer(x_s, ts, ts, bp, hist_s, s0_s, k_past, v_past,
                                          min(DELTA_BLOCK, ts), max(1, ts // DELTA_BLOCK), min(KEY_TILE, ts))

    h2_all = jnp.concatenate([h2p, h2s], axis=0)
    route = jnp.concatenate([rp, rs], axis=0)
    blk_e, nvalid, row_src, row_dst, row_w = _dispatch_plan(route, n_tok, n_tok)
    y_rows = _moe(blk_e, nvalid, row_src, row_dst, row_w, p['w_gate'].astype(BF16),
                  p['w_up'].astype(BF16), p['w_down'].astype(BF16), h2_all, 2 * n_tok)
    y_slots = y_rows.reshape(2, n_tok, d)
    y_p = _combine(y_slots, x1p, mod8, g_post_ffn, tm_p, tp, 0, 0).reshape(bp, tp, d)
    y_s = _combine(y_slots, x1s, mod8, g_post_ffn, ts, ts, bp, n_p).reshape(bs, ts, d)
    return y_p, y_s, kp, vp, sp, cp, ks, vs, ss, cs


def kernel(x_prompt, x_sample, c_prompt, c_sample, cache_k, cache_v, state_delta, state_conv, w_ada, b_ada, g_pre_mix, g_post_mix, g_pre_ffn, g_post_ffn, w_in, conv_w, a_log, dt_bias, onorm_a, onorm_b, w_out, w_router_group, b_router_group, w_router_expert, b_router_expert, w_gate, w_up, w_down):
    depth = w_in.shape[0]
    y_p, y_s = x_prompt, x_sample
    outs = [[] for _ in range(8)]
    for l in range(depth):
        p = dict(w_ada=w_ada[l], b_ada=b_ada[l], g_pre_mix=g_pre_mix[l], g_post_mix=g_post_mix[l],
                 g_pre_ffn=g_pre_ffn[l], g_post_ffn=g_post_ffn[l], w_in=w_in[l], conv_w=conv_w[l],
                 a_log=a_log[l], dt_bias=dt_bias[l], onorm_a=onorm_a[l], onorm_b=onorm_b[l],
                 w_out=w_out[l], w_router_group=w_router_group[l], b_router_group=b_router_group[l],
                 w_router_expert=w_router_expert[l], b_router_expert=b_router_expert[l],
                 w_gate=w_gate[l], w_up=w_up[l], w_down=w_down[l])
        res = _layer(y_p, y_s, c_prompt, c_sample, cache_k[l], cache_v[l], state_delta[l], state_conv[l], p)
        y_p, y_s = res[0], res[1]
        for lst, r in zip(outs, res[2:]):
            lst.append(r)
    return (y_p, y_s) + tuple(jnp.stack(o) for o in outs)
```

```python
import functools
import math

import jax
import jax.numpy as jnp
from jax import lax
from jax.experimental import pallas as pl
from jax.experimental.pallas import tpu as pltpu

F32 = jnp.float32
BF16 = jnp.bfloat16

D_MODEL = 1024
DN_HEADS = 4
DN_DIM = 128
CONV_W = 4
CONV_CH = DN_HEADS * 3 * DN_DIM
DELTA_BLOCK = 64
SB_HEADS = 8
SB_DIM = 64
SB_WIDTH = SB_HEADS * SB_DIM
N_GROUPS = 4
EXPERTS_PER_GROUP = 8
N_EXPERTS = N_GROUPS * EXPERTS_PER_GROUP
D_EXPERT = D_MODEL // 2
MOE_BLOCK = 128
EPS = 1e-6

LANES = 128
KEY_TILE = 128
EXP_ZERO_BELOW = -104.0
VMEM_LIMIT = 56 * 1024 * 1024


def _cparams(sem):
    return pltpu.CompilerParams(dimension_semantics=sem, vmem_limit_bytes=VMEM_LIMIT)


def _split(a):
    hi = a.astype(BF16)
    lo = (a - hi.astype(F32)).astype(BF16)
    return hi, lo


def _dot(a, b, dims=(((1,), (0,)), ((), ()))):
    return lax.dot_general(a, b, dims, preferred_element_type=F32)


def _dot3(a, b, dims=(((1,), (0,)), ((), ()))):
    ah, al = _split(a)
    bh, bl = _split(b)
    return _dot(ah, bh, dims) + (_dot(al, bh, dims) + _dot(ah, bl, dims))


_NT = (((1,), (1,)), ((), ()))
_TN = (((0,), (0,)), ((), ()))


def _silu(x):
    return x * jax.nn.sigmoid(x)


def _softplus(x):
    return jnp.maximum(x, 0.0) + jnp.log(1.0 + jnp.exp(-jnp.abs(x)))


def _ada_kernel(c_ref, w_ref, b_ref, o_ref):
    s = _silu(c_ref[...]).astype(BF16)
    o_ref[...] = _dot(s, w_ref[...].astype(BF16)) + b_ref[...]


def _ada(c_all, w_ada, b_ada):
    rows = c_all.shape[0]
    n = w_ada.shape[1]
    tn = 1024
    return pl.pallas_call(
        _ada_kernel,
        grid=(n // tn,),
        in_specs=[pl.BlockSpec((rows, D_MODEL), lambda j: (0, 0)),
                  pl.BlockSpec((D_MODEL, tn), lambda j: (0, j)),
                  pl.BlockSpec((1, tn), lambda j: (0, j))],
        out_specs=pl.BlockSpec((rows, tn), lambda j: (0, j)),
        out_shape=jax.ShapeDtypeStruct((rows, n), F32),
        compiler_params=_cparams(("arbitrary",)),
        name="ada",
    )(c_all, w_ada, b_ada.reshape(1, n))


def _rms(x, gain):
    return x * lax.rsqrt(jnp.mean(x * x, axis=-1, keepdims=True) + EPS) * gain


def _proj_kernel(x_ref, mod_ref, g_ref, wm_ref, wbh_ref, wbl_ref,
                 a_ref, z_ref, bg_ref, q_ref, k_ref, v_ref, k16_ref, v16_ref):
    mod = mod_ref[0]
    h = _rms(x_ref[...], g_ref[...]) * (1.0 + mod[1:2]) + mod[0:1]
    hh, hl = _split(h)
    p = _dot(hh, wm_ref[...])
    a_ref[...] = p[:, 0:CONV_CH]
    z_ref[...] = p[:, CONV_CH:CONV_CH + 512]
    o = CONV_CH + 512
    q_ref[...] = (p[:, o:o + 512] * (SB_DIM ** -0.5)).astype(BF16)
    k = p[:, o + 512:o + 1024]
    v = p[:, o + 1024:o + 1536]
    k_ref[...] = k
    v_ref[...] = v
    k16_ref[...] = k.astype(BF16)
    v16_ref[...] = v.astype(BF16)
    wbh = wbh_ref[...]
    bg_ref[...] = _dot(hh, wbh) + (_dot(hl, wbh) + _dot(hh, wbl_ref[...]))


def _proj(x2d, mod8, g_pre, w_main, wb_hi, wb_lo, tm, seq_rows, mod_row0):
    n = x2d.shape[0]
    nm = w_main.shape[1]
    row = lambda i: (i, 0)
    const = lambda i: (0, 0)
    modmap = lambda i: (mod_row0 + (i * tm) // seq_rows, 0, 0)
    outs = [(CONV_CH, F32), (512, F32), (LANES, F32), (512, BF16), (512, F32), (512, F32),
            (512, BF16), (512, BF16)]
    return pl.pallas_call(
        _proj_kernel,
        grid=(n // tm,),
        in_specs=[pl.BlockSpec((tm, D_MODEL), row),
                  pl.BlockSpec((1, 8, D_MODEL), modmap),
                  pl.BlockSpec((1, D_MODEL), const),
                  pl.BlockSpec((D_MODEL, nm), const),
                  pl.BlockSpec((D_MODEL, LANES), const),
                  pl.BlockSpec((D_MODEL, LANES), const)],
        out_specs=[pl.BlockSpec((tm, w), row) for w, _ in outs],
        out_shape=[jax.ShapeDtypeStruct((n, w), dt) for w, dt in outs],
        compiler_params=_cparams(("arbitrary",)),
        name="proj",
    )(x2d, mod8, g_pre, w_main, wb_hi, wb_lo)


def _delta_kernel(a_ref, z_ref, bg_ref, hist0_ref, s0_ref, cw_ref, alog_ref, dtb_ref, on_ref,
                  o_ref, sfin_ref, hist_sc, s_sc, *, chunk, n_chunks):
    t_idx = pl.program_id(1)
    tt = chunk * n_chunks

    @pl.when(t_idx == 0)
    def _():
        hist_sc[...] = hist0_ref[0]
        s_sc[...] = s0_ref[0]

    x = a_ref[0]
    xx = jnp.concatenate([hist_sc[...], x], axis=0)
    cw = cw_ref[...]
    conv = x * cw[CONV_W - 1:CONV_W]
    for s in range(1, CONV_W):
        conv = conv + pltpu.roll(xx, s, 0)[8:] * cw[CONV_W - 1 - s:CONV_W - s]
    conv = _silu(conv)
    hist_sc[...] = x[tt - 8:tt]

    bg = bg_ref[0]
    lane = lax.broadcasted_iota(jnp.int32, (1, LANES), 1)
    g_lane = (lane >= DN_HEADS) & (lane < 2 * DN_HEADS)
    neg_a = jnp.where(g_lane, -jnp.exp(alog_ref[...]), 0.0)
    beta_all = jax.nn.sigmoid(bg)
    g_all = neg_a * _softplus(bg + dtb_ref[...])

    ri = lax.broadcasted_iota(jnp.int32, (chunk, chunk), 0)
    ci = lax.broadcasted_iota(jnp.int32, (chunk, chunk), 1)
    incl = ri >= ci
    strict = ri > ci
    tri = incl.astype(F32)
    eye = (ri == ci).astype(F32)
    ones_cc = jnp.ones((chunk, chunk), F32)
    onorm = on_ref[...]

    chunks = range(n_chunks)
    units = [(c, h) for c in chunks for h in range(DN_HEADS)]
    rows = lambda c: slice(c * chunk, (c + 1) * chunk)

    gc_all = [_dot3(tri, g_all[rows(c)]) for c in chunks]
    diag = [jnp.concatenate([eye * gc_all[c][:, DN_HEADS + h:DN_HEADS + h + 1] for h in range(DN_HEADS)], axis=1)
            for c in chunks]
    gc_rows = [_dot3(ones_cc, diag[c]) for c in chunks]

    q, k, v, beta, gcol, glast, eg, decay, kbeta = {}, {}, {}, {}, {}, {}, {}, {}, {}
    for (c, h) in units:
        lo = h * DN_DIM
        qq = conv[rows(c), lo:lo + DN_DIM]
        kk = conv[rows(c), 512 + lo:512 + lo + DN_DIM]
        u_ = (c, h)
        q[u_] = qq * lax.rsqrt(jnp.sum(qq * qq, axis=-1, keepdims=True) + EPS) * (DN_DIM ** -0.5)
        k[u_] = kk * lax.rsqrt(jnp.sum(kk * kk, axis=-1, keepdims=True) + EPS)
        v[u_] = conv[rows(c), 1024 + lo:1024 + lo + DN_DIM]
        beta[u_] = beta_all[rows(c), h:h + 1]
        gcol[u_] = gc_all[c][:, DN_HEADS + h:DN_HEADS + h + 1]
        glast[u_] = gcol[u_][chunk - 1:chunk, :]
        eg[u_] = jnp.exp(gcol[u_])
        grow = gc_rows[c][:, h * chunk:(h + 1) * chunk]
        decay[u_] = jnp.where(incl, jnp.exp(jnp.minimum(gcol[u_] - grow, 0.0)), 0.0)
        kbeta[u_] = k[u_] * beta[u_]

    k16 = {u_: k[u_].astype(BF16) for u_ in units}
    kq = {u_: _dot(jnp.concatenate([kbeta[u_], q[u_]], axis=0).astype(BF16), k16[u_], _NT) for u_ in units}
    lower = {u_: jnp.where(strict, kq[u_][:chunk] * decay[u_], 0.0) for u_ in units}
    attn16 = {u_: jnp.where(incl, kq[u_][chunk:] * decay[u_], 0.0).astype(BF16) for u_ in units}

    sol = {u_: jnp.concatenate([v[u_] * beta[u_], kbeta[u_] * eg[u_]], axis=1) for u_ in units}
    lp = lower
    p = 1
    while p < chunk:
        lsp = {u_: _split(lp[u_]) for u_ in units}
        ssp = {u_: _split(sol[u_]) for u_ in units}
        stack = {u_: jnp.concatenate(lsp[u_], axis=0) for u_ in units}
        p1 = {u_: _dot(stack[u_], ssp[u_][0]) for u_ in units}
        p2 = {u_: _dot(lsp[u_][0], ssp[u_][1]) for u_ in units}
        upd = {u_: p1[u_][:chunk] + (p1[u_][chunk:] + p2[u_]) for u_ in units}
        sol = {u_: (sol[u_] - upd[u_]) if p == 1 else (sol[u_] + upd[u_]) for u_ in units}
        if 2 * p < chunk:
            q1 = {u_: _dot(stack[u_], lsp[u_][0]) for u_ in units}
            q2 = {u_: _dot(lsp[u_][0], lsp[u_][1]) for u_ in units}
            lp = {u_: q1[u_][:chunk] + (q1[u_][chunk:] + q2[u_]) for u_ in units}
        p *= 2

    usol = {u_: sol[u_][:, :DN_DIM] for u_ in units}
    wq16 = {u_: jnp.concatenate([sol[u_][:, DN_DIM:], q[u_] * eg[u_]], axis=0).astype(BF16) for u_ in units}
    kd16 = {u_: (k[u_] * jnp.exp(glast[u_] - gcol[u_])).astype(BF16) for u_ in units}

    for c in chunks:
        hs = range(DN_HEADS)
        s_old = [s_sc[h] for h in hs]
        ws = [_dot(wq16[(c, h)], s_old[h].astype(BF16)) for h in hs]
        v_new = [(usol[(c, h)] - ws[h][:chunk]).astype(BF16) for h in hs]
        o_in = [_dot(attn16[(c, h)], v_new[h]) for h in hs]
        ds = [_dot(kd16[(c, h)], v_new[h], _TN) for h in hs]
        for h in hs:
            lo = h * DN_DIM
            s_sc[h] = s_old[h] * jnp.exp(glast[(c, h)]) + ds[h]
            o = ws[h][chunk:] + o_in[h]
            zg = z_ref[0, rows(c), lo:lo + DN_DIM]
            o_ref[0, rows(c), lo:lo + DN_DIM] = (_rms(o, onorm) * _silu(zg)).astype(BF16)

    @pl.when(t_idx == pl.num_programs(1) - 1)
    def _():
        sfin_ref[0] = s_sc[...]


def _delta(a_in, z, bg, hist8, s0, conv_w8, alog_row, dtb_row, onorm_a, chunk, n_chunks):
    b, t, _ = a_in.shape
    tt = chunk * n_chunks
    tile = lambda bi, ti: (bi, ti, 0)
    per_b3 = lambda bi, ti: (bi, 0, 0)
    per_b4 = lambda bi, ti: (bi, 0, 0, 0)
    const = lambda bi, ti: (0, 0)
    kern = functools.partial(_delta_kernel, chunk=chunk, n_chunks=n_chunks)
    return pl.pallas_call(
        kern,
        grid=(b, t // tt),
        in_specs=[pl.BlockSpec((1, tt, CONV_CH), tile),
                  pl.BlockSpec((1, tt, 512), tile),
                  pl.BlockSpec((1, tt, LANES), tile),
                  pl.BlockSpec((1, 8, CONV_CH), per_b3),
                  pl.BlockSpec((1, DN_HEADS, DN_DIM, DN_DIM), per_b4),
                  pl.BlockSpec((8, CONV_CH), const),
                  pl.BlockSpec((1, LANES), const),
                  pl.BlockSpec((1, LANES), const),
                  pl.BlockSpec((1, DN_DIM), const)],
        out_specs=[pl.BlockSpec((1, tt, 512), tile),
                   pl.BlockSpec((1, DN_HEADS, DN_DIM, DN_DIM), per_b4)],
        out_shape=[jax.ShapeDtypeStruct((b, t, 512), BF16),
                   jax.ShapeDtypeStruct((b, DN_HEADS, DN_DIM, DN_DIM), F32)],
        scratch_shapes=[pltpu.VMEM((8, CONV_CH), F32),
                        pltpu.VMEM((DN_HEADS, DN_DIM, DN_DIM), F32)],
        compiler_params=_cparams(("arbitrary", "arbitrary")),
        name="delta",
    )(a_in, z, bg, hist8, s0, conv_w8, alog_row, dtb_row, onorm_a)


def _sb_kernel(q_ref, k0_ref, k1_ref, k2_ref, v0_ref, v1_ref, v2_ref, k_hbm, v_hbm, on_ref,
               o_ref, kbuf, vbuf, qsel, acc, carry, sem, *, bq, n_pad, q_off):
    b = pl.program_id(0)
    i = pl.program_id(1)
    qend = q_off + (i + 1) * bq
    last_tile = qend // KEY_TILE - 1
    n_tiles = last_tile + 1

    acc[...] = jnp.zeros_like(acc)
    carry[...] = jnp.zeros_like(carry)
    half_lane = lax.broadcasted_iota(jnp.int32, (bq, LANES), 1) < SB_DIM
    for p in range(SB_HEADS // 2):
        qf = q_ref[0, :, p * LANES:(p + 1) * LANES].astype(F32)
        qsel[2 * p] = jnp.where(half_lane, qf, 0.0).astype(BF16)
        qsel[2 * p + 1] = jnp.where(half_lane, 0.0, qf).astype(BF16)

    qpos =q_off + i * bq + lax.broadcasted_iota(jnp.int32, (bq, KEY_TILE), 0)
    col = lax.broadcasted_iota(jnp.int32, (bq, KEY_TILE), 1)
    lane = lax.broadcasted_iota(jnp.int32, (1, LANES), 1)
    low_half = lane < SB_DIM
    rj = lax.broadcasted_iota(jnp.int32, (KEY_TILE, 2 * KEY_TILE), 0)
    cj = lax.broadcasted_iota(jnp.int32, (KEY_TILE, 2 * KEY_TILE), 1)
    suffix = ((rj > cj) | (cj >= KEY_TILE)).astype(BF16)

    def load_tile(t):
        for kk, (kr, vr) in enumerate(((k0_ref, v0_ref), (k1_ref, v1_ref), (k2_ref, v2_ref))):
            @pl.when(t == kk)
            def _():
                for p in range(SB_HEADS // 2):
                    kbuf[p] = kr[0, :, p * LANES:(p + 1) * LANES]
                    vbuf[p] = vr[0, :, p * LANES:(p + 1) * LANES]

        @pl.when(t >= 3)
        def _():
            start = pl.multiple_of((last_tile - t) * KEY_TILE, KEY_TILE)
            copies = []
            for p in range(SB_HEADS // 2):
                for src, dst in ((k_hbm, kbuf), (v_hbm, vbuf)):
                    cp = pltpu.make_async_copy(
                        src.at[b, pl.ds(start, KEY_TILE), pl.ds(p * LANES, LANES)], dst.at[p], sem)
                    cp.start()
                    copies.append(cp)
            for cp in copies:
                cp.wait()

    def body(state):
        t, _ = state
        load_tile(t)
        kpos = (last_tile - t) * KEY_TILE + col
        mask = (kpos < qpos) & (kpos >= n_pad)
        heads = range(SB_HEADS)
        z = [_dot(qsel[h], kbuf[h // 2], _NT) for h in heads]
        sp = [_softplus(z[h]) for h in heads]
        pieces = []
        for h in heads:
            pieces.extend(_split(jnp.where(mask, -sp[h], 0.0)))
        cs_all = _dot(jnp.concatenate(pieces, axis=0), suffix)
        cs = [cs_all[2 * h * bq:(2 * h + 1) * bq] + cs_all[(2 * h + 1) * bq:(2 * h + 2) * bq] for h in heads]
        c_old = [carry[h] for h in heads]
        a = [jnp.where(mask, jnp.exp((z[h] - sp[h]) + cs[h][:, :KEY_TILE] + c_old[h]), 0.0).astype(BF16)
             for h in heads]
        pv = [_dot(a[h], vbuf[h // 2]) for h in heads]
        for p in range(SB_HEADS // 2):
            acc[p] = acc[p] + jnp.where(low_half, pv[2 * p], pv[2 * p + 1])
        worst = jnp.full((bq, KEY_TILE), -jnp.inf, F32)
        for h in heads:
            c_new = c_old[h] + cs[h][:, KEY_TILE:]
            carry[h] = c_new
            worst = jnp.maximum(worst, c_new)
        done = (jnp.max(worst) < EXP_ZERO_BELOW).astype(jnp.int32)
        return t + 1, done

    def cond(state):
        t, done = state
        return (t < n_tiles) & (done == 0)

    lax.while_loop(cond, body, (jnp.int32(0), jnp.int32(0)))

    onb = on_ref[...]
    for p in range(SB_HEADS // 2):
        o = acc[p]
        sq = o * o
        s_lo = jnp.sum(jnp.where(low_half, sq, 0.0), axis=-1, keepdims=True)
        s_hi = jnp.sum(jnp.where(low_half, 0.0, sq), axis=-1, keepdims=True)
        ms = jnp.where(low_half, s_lo, s_hi) * (1.0 / SB_DIM)
        o_ref[0, :, p * LANES:(p + 1) * LANES] = (o * lax.rsqrt(ms + EPS) * onb).astype(BF16)


def _sb_attn(q16, k16p, v16p, onorm_b2, bq, n_pad):
    b, tq, _ = q16.shape
    tkp = k16p.shape[1]
    q_off = tkp - tq
    assert tkp % KEY_TILE == 0 and KEY_TILE % bq == 0 and tq % bq == 0 and q_off % bq == 0

    def kmap(back):
        def f(bi, i):
            last = (q_off + (i + 1) * bq) // KEY_TILE - 1
            return (bi, jnp.maximum(last - back, 0), 0)
        return f

    qmap = lambda bi, i: (bi, i, 0)
    kern = functools.partial(_sb_kernel, bq=bq, n_pad=n_pad, q_off=q_off)
    kspec = [pl.BlockSpec((1, KEY_TILE, SB_WIDTH), kmap(back)) for back in range(3)]
    return pl.pallas_call(
        kern,
        grid=(b, tq // bq),
        in_specs=[pl.BlockSpec((1, bq, SB_WIDTH), qmap)] + kspec + kspec
                 + [pl.BlockSpec(memory_space=pl.ANY), pl.BlockSpec(memory_space=pl.ANY),
                    pl.BlockSpec((1, LANES), lambda bi, i: (0, 0))],
        out_specs=pl.BlockSpec((1, bq, SB_WIDTH), qmap),
        out_shape=jax.ShapeDtypeStruct((b, tq, SB_WIDTH), BF16),
        scratch_shapes=[pltpu.VMEM((SB_HEADS // 2, KEY_TILE, LANES), BF16),
                        pltpu.VMEM((SB_HEADS // 2, KEY_TILE, LANES), BF16),
                        pltpu.VMEM((SB_HEADS, bq, LANES), BF16),
                        pltpu.VMEM((SB_HEADS // 2, bq, LANES), F32),
                        pltpu.VMEM((SB_HEADS, bq, KEY_TILE), F32),
                        pltpu.SemaphoreType.DMA(())],
        compiler_params=_cparams(("arbitrary", "arbitrary")),
        name="sb_attn",
    )(q16, k16p, k16p, k16p, v16p, v16p, v16p, k16p, v16p, onorm_b2)


def _post_kernel(oa_ref, ob_ref, x_ref, mod_ref, gpm_ref, gpf_ref, wo_ref, wrh_ref, wrl_ref, br_ref,
                 x1_ref, h2_ref, route_ref):
    mod = mod_ref[0]
    mix = _dot(oa_ref[...], wo_ref[0:512, :]) + _dot(ob_ref[...], wo_ref[512:1024, :])
    x1 = x_ref[...] + mod[2:3] * _rms(mix, gpm_ref[...])
    x1_ref[...] = x1
    h2 = _rms(x1, gpf_ref[...]) * (1.0 + mod[4:5]) + mod[3:4]
    h2_ref[...] = h2
    hh, hl = _split(h2)
    wrh = wrh_ref[...]
    logits = _dot(hh, wrh) + (_dot(hl, wrh) + _dot(hh, wrl_ref[...])) + br_ref[...]
    lane = lax.broadcasted_iota(jnp.int32, logits.shape, 1).astype(F32)
    neg = -jnp.inf
    nl = float(LANES)
    lg = jnp.where(lane < N_GROUPS, logits, neg)
    gmax = jnp.max(lg, axis=-1, keepdims=True)
    grp = jnp.min(jnp.where(lg == gmax, lane, nl), axis=-1, keepdims=True)
    p_grp = 1.0 / jnp.sum(jnp.exp(lg - gmax), axis=-1, keepdims=True)
    first = N_GROUPS + grp * EXPERTS_PER_GROUP
    le = jnp.where((lane >= first) & (lane < first + EXPERTS_PER_GROUP), logits, neg)
    emax = jnp.max(le, axis=-1, keepdims=True)
    i1 = jnp.min(jnp.where(le == emax, lane, nl), axis=-1, keepdims=True)
    esum = jnp.sum(jnp.exp(le - emax), axis=-1, keepdims=True)
    le2 = jnp.where(lane == i1, neg, le)
    e2max = jnp.max(le2, axis=-1, keepdims=True)
    i2 = jnp.min(jnp.where(le2 == e2max, lane, nl), axis=-1, keepdims=True)
    p1 = 1.0 / esum
    p2 = jnp.exp(e2max - emax) / esum
    w1 = p_grp * p1 / (p1 + p2)
    w2 = p_grp * p2 / (p1 + p2)
    out = jnp.where(lane == 0.0, i1 - N_GROUPS, 0.0)
    out = jnp.where(lane == 1.0, i2 - N_GROUPS, out)
    out = jnp.where(lane == 2.0, w1, out)
    out = jnp.where(lane == 3.0, w2, out)
    route_ref[...] = out


def _post(oa16, ob16, x2d, mod8, g_post_mix, g_pre_ffn, w_out16, wr_hi, wr_lo, b_r, tm, seq_rows, mod_row0):
    n = x2d.shape[0]
    row = lambda i: (i, 0)
    const = lambda i: (0, 0)
    modmap = lambda i: (mod_row0 + (i * tm) // seq_rows, 0, 0)
    return pl.pallas_call(
        _post_kernel,
        grid=(n // tm,),
        in_specs=[pl.BlockSpec((tm, 512), row),
                  pl.BlockSpec((tm, 512), row),
                  pl.BlockSpec((tm, D_MODEL), row),
                  pl.BlockSpec((1, 8, D_MODEL), modmap),
                  pl.BlockSpec((1, D_MODEL), const),
                  pl.BlockSpec((1, D_MODEL), const),
                  pl.BlockSpec((D_MODEL, D_MODEL), const),
                  pl.BlockSpec((D_MODEL, LANES), const),
                  pl.BlockSpec((D_MODEL, LANES), const),
                  pl.BlockSpec((1, LANES), const)],
        out_specs=[pl.BlockSpec((tm, D_MODEL), row),
                   pl.BlockSpec((tm, D_MODEL), row),
                   pl.BlockSpec((tm, LANES), row)],
        out_shape=[jax.ShapeDtypeStruct((n, D_MODEL), F32),
                   jax.ShapeDtypeStruct((n, D_MODEL), F32),
                   jax.ShapeDtypeStruct((n, LANES), F32)],
        compiler_params=_cparams(("arbitrary",)),
        name="post",
    )(oa16, ob16, x2d, mod8, g_post_mix, g_pre_ffn, w_out16, wr_hi, wr_lo, b_r)


def _moe_kernel(blk_e_ref, nvalid_ref, src_ref, dst_ref, roww_ref, wg_ref, wu_ref, wd_ref, x_hbm,
                y_hbm, xbuf, ybuf, gsem, ssem):
    i = pl.program_id(0)
    nv = nvalid_ref[i]

    def gather_row(r):
        return pltpu.make_async_copy(x_hbm.at[pl.ds(src_ref[0, 0, r], 1), :], xbuf.at[pl.ds(r, 1), :], gsem)

    def scatter_row(r):
        return pltpu.make_async_copy(ybuf.at[pl.ds(r, 1), :], y_hbm.at[pl.ds(dst_ref[0, 0, r], 1), :], ssem)

    def move_rows(row_copy, whole_copy):
        @pl.when(nv == MOE_BLOCK)
        def _():
            for r in range(MOE_BLOCK):
                row_copy(r).start()
            whole_copy.wait()

        @pl.when(nv < MOE_BLOCK)
        def _():
            def start(r, c):
                row_copy(r).start()
                return c

            def wait(r, c):
                row_copy(r).wait()
                return c

            lax.fori_loop(0, nv, start, 0)
            lax.fori_loop(0, nv, wait, 0)

    @pl.when(i == 0)
    def _():
        xbuf[...] = jnp.zeros_like(xbuf)

    @pl.when(nv > 0)
    def _():
        move_rows(gather_row, pltpu.make_async_copy(x_hbm.at[pl.ds(0, MOE_BLOCK), :], xbuf, gsem))
        xb = xbuf[...].astype(BF16)
        g = _dot(xb, wg_ref[0])
        u = _dot(xb, wu_ref[0])
        hmid = (_silu(g) * u).astype(BF16)
        ybuf[...] = _dot(hmid, wd_ref[0]) * roww_ref[...]
        move_rows(scatter_row, pltpu.make_async_copy(ybuf, y_hbm.at[pl.ds(0, MOE_BLOCK), :], ssem))


def _moe(blk_e, nvalid, row_src, row_dst, row_w, wg16, wu16, wd16, x_rows, n_out_rows):
    n_blocks = blk_e.shape[0]
    wmap = lambda i, be, nu: (be[i], 0, 0)
    grid_spec = pltpu.PrefetchScalarGridSpec(
        num_scalar_prefetch=2,
        grid=(n_blocks,),
        in_specs=[pl.BlockSpec((1, 1, MOE_BLOCK), lambda i, be, nu: (i, 0, 0), memory_space=pltpu.SMEM),
                  pl.BlockSpec((1, 1, MOE_BLOCK), lambda i, be, nu: (i, 0, 0), memory_space=pltpu.SMEM),
                  pl.BlockSpec((MOE_BLOCK, 1), lambda i, be, nu: (i, 0)),
                  pl.BlockSpec((1, D_MODEL, D_EXPERT), wmap),
                  pl.BlockSpec((1, D_MODEL, D_EXPERT), wmap),
                  pl.BlockSpec((1, D_EXPERT, D_MODEL), wmap),
                  pl.BlockSpec(memory_space=pl.ANY)],
        out_specs=pl.BlockSpec(memory_space=pl.ANY),
        scratch_shapes=[pltpu.VMEM((MOE_BLOCK, D_MODEL), F32),
                        pltpu.VMEM((MOE_BLOCK, D_MODEL), F32),
                        pltpu.SemaphoreType.DMA(()),
                        pltpu.SemaphoreType.DMA(())])
    return pl.pallas_call(
        _moe_kernel,
        grid_spec=grid_spec,
        out_shape=jax.ShapeDtypeStruct((n_out_rows, D_MODEL), F32),
        compiler_params=_cparams(("arbitrary",)),
        name="moe",
    )(blk_e, nvalid, row_src.reshape(n_blocks, 1, MOE_BLOCK), row_dst.reshape(n_blocks, 1, MOE_BLOCK),
      row_w.reshape(n_blocks * MOE_BLOCK, 1), wg16, wu16, wd16, x_rows)


def _combine_kernel(y0_ref, y1_ref, x1_ref, mod_ref, g_ref, o_ref):
    mod = mod_ref[0]
    moe = y0_ref[0] + y1_ref[0]
    o_ref[...] = x1_ref[...] + mod[5:6] * _rms(moe, g_ref[...])


def _combine(y_slots, x1, mod8, g_post_ffn, tm, seq_rows, mod_row0, row0):
    n = x1.shape[0]
    blk0 = row0 // tm
    row = lambda i: (i, 0)
    modmap = lambda i: (mod_row0 + (i * tm) // seq_rows, 0, 0)
    return pl.pallas_call(
        _combine_kernel,
        grid=(n // tm,),
        in_specs=[pl.BlockSpec((1, tm, D_MODEL), lambda i: (0, blk0 + i, 0)),
                  pl.BlockSpec((1, tm, D_MODEL), lambda i: (1, blk0 + i, 0)),
                  pl.BlockSpec((tm, D_MODEL), row),
                  pl.BlockSpec((1, 8, D_MODEL), modmap),
                  pl.BlockSpec((1, D_MODEL), lambda i: (0, 0))],
        out_specs=pl.BlockSpec((tm, D_MODEL), row),
        out_shape=jax.ShapeDtypeStruct((n, D_MODEL), F32),
        compiler_params=_cparams(("arbitrary",)),
        name="combine",
    )(y_slots, y_slots, x1, mod8, g_post_ffn)


def _dispatch_plan(route, n_tok, slot_rows):
    m = 2 * n_tok
    eid = route[:, 0:2].astype(jnp.int32).reshape(m)
    wts = route[:, 2:4].reshape(m)
    flat = jnp.arange(m, dtype=jnp.int32)
    order = jnp.argsort(eid)
    e_sorted = eid[order]
    counts = jnp.zeros((N_EXPERTS,), jnp.int32).at[eid].add(1)
    padded = (counts + MOE_BLOCK - 1) // MOE_BLOCK * MOE_BLOCK
    pad_end = jnp.cumsum(padded)
    pad_start = pad_end - padded
    start = jnp.cumsum(counts) - counts
    dest = pad_start[e_sorted] + flat - start[e_sorted]
    n_blocks = -(-m // MOE_BLOCK) + N_EXPERTS
    n_rows = n_blocks * MOE_BLOCK
    src_tok = order // 2
    src_slot = order % 2
    row_src = jnp.zeros((n_rows,), jnp.int32).at[dest].set(src_tok)
    row_dst = jnp.zeros((n_rows,), jnp.int32).at[dest].set(src_slot * slot_rows + src_tok)
    row_w = jnp.zeros((n_rows,), F32).at[dest].set(wts[order])
    blk_start = jnp.arange(n_blocks, dtype=jnp.int32) * MOE_BLOCK
    blk_e = jnp.minimum(jnp.searchsorted(pad_end, blk_start, side='right'), N_EXPERTS - 1).astype(jnp.int32)
    nvalid = jnp.clip(counts[blk_e] - (blk_start - pad_start[blk_e]), 0, MOE_BLOCK).astype(jnp.int32)
    return blk_e, nvalid, row_src, row_dst, row_w


def _layer(x_p, x_s, c_p, c_s, k_past, v_past, s0_s, conv_s, p):
    bp, tp, d = x_p.shape
    bs, ts, _ = x_s.shape
    n_p, n_s = bp * tp, bs * ts
    n_tok = n_p + n_s

    n_seq = bp + bs
    c_all = jnp.zeros((16, d), F32).at[:n_seq].set(jnp.concatenate([c_p, c_s], axis=0))
    mod = _ada(c_all, p['w_ada'], p['b_ada'])
    mod8 = jnp.pad(mod.reshape(16, 6, d), ((0, 0), (0, 2), (0, 0)))

    w_in = p['w_in']
    o_z, o_b, o_q = CONV_CH, CONV_CH + 512, CONV_CH + 512 + 2 * DN_HEADS
    w_main = jnp.concatenate([w_in[:, :o_b], w_in[:, o_q:]], axis=1).astype(BF16)
    wb = jnp.pad(w_in[:, o_b:o_q], ((0, 0), (0, LANES - 2 * DN_HEADS)))
    wb_hi = wb.astype(BF16)
    wb_lo = (wb - wb_hi.astype(F32)).astype(BF16)
    g_pre_mix = p['g_pre_mix'].reshape(1, d)

    conv_w8 = jnp.pad(p['conv_w'], ((0, 8 - CONV_W), (0, 0)))
    pad_g = lambda a: jnp.pad(a.reshape(1, DN_HEADS), ((0, 0), (DN_HEADS, LANES - 2 * DN_HEADS)))
    alog_row, dtb_row = pad_g(p['a_log']), pad_g(p['dt_bias'])
    onorm_a = p['onorm_a'].reshape(1, DN_DIM)
    onorm_b2 = jnp.tile(p['onorm_b'].reshape(1, SB_DIM), (1, 2))

    w_out16 = p['w_out'].astype(BF16)
    wr = jnp.pad(jnp.concatenate([p['w_router_group'], p['w_router_expert']], axis=1),
                 ((0, 0), (0, LANES - N_GROUPS - N_EXPERTS)))
    wr_hi = wr.astype(BF16)
    wr_lo = (wr - wr_hi.astype(F32)).astype(BF16)
    b_r = jnp.pad(jnp.concatenate([p['b_router_group'], p['b_router_expert']]).reshape(1, -1),
                  ((0, 0), (0, LANES - N_GROUPS - N_EXPERTS)))
    g_post_mix = p['g_post_mix'].reshape(1, d)
    g_pre_ffn = p['g_pre_ffn'].reshape(1, d)
    g_post_ffn = p['g_post_ffn'].reshape(1, d)

    def mixer(x, tm, seq_rows, mod_row0, hist8, s0, k_old, v_old, chunk, n_chunks, bq):
        b, t, _ = x.shape
        x2d = x.reshape(b * t, d)
        a_in, z, bg, q16, kb, vb, k16, v16 = _proj(x2d, mod8, g_pre_mix, w_main, wb_hi, wb_lo,
                                                    tm, seq_rows, mod_row0)
        r3 = lambda a: a.reshape(b, t, a.shape[-1])
        oa16, s_new = _delta(r3(a_in), r3(z), r3(bg), hist8, s0, conv_w8, alog_row, dtb_row, onorm_a,
                             chunk, n_chunks)
        k16, v16 = r3(k16), r3(v16)
        if k_old is not None:
            k16 = jnp.concatenate([k_old.reshape(b, -1, SB_WIDTH).astype(BF16), k16], axis=1)
            v16 = jnp.concatenate([v_old.reshape(b, -1, SB_WIDTH).astype(BF16), v16], axis=1)
        n_pad = (-k16.shape[1]) % KEY_TILE
        k16 = jnp.pad(k16, ((0, 0), (n_pad, 0), (0, 0)))
        v16 = jnp.pad(v16, ((0, 0), (n_pad, 0), (0, 0)))
        ob16 = _sb_attn(r3(q16), k16, v16, onorm_b2, bq, n_pad)
        x1, h2, route = _post(oa16.reshape(b * t, 512), ob16.reshape(b * t, 512), x2d, mod8,
                              g_post_mix, g_pre_ffn, w_out16, wr_hi, wr_lo, b_r, tm, seq_rows, mod_row0)
        new_conv = r3(a_in)[:, t - (CONV_W - 1):, :]
        return (x1, h2, route, kb.reshape(b, t, SB_HEADS, SB_DIM), vb.reshape(b, t, SB_HEADS, SB_DIM),
                s_new, new_conv)

    zero_hist = jnp.zeros((bp, 8, CONV_CH), F32)
    zero_s = jnp.zeros((bp, DN_HEADS, DN_DIM, DN_DIM), F32)
    hist_s = jnp.pad(conv_s, ((0, 0), (8 - (CONV_W - 1), 0), (0, 0)))
    tm_p = min(256, tp)
    nc_p = max(1, min(4, tp // DELTA_BLOCK))
    x1p, h2p, rp, kp, vp, sp, cp = mixer(x_p, tm_p, tp, 0, zero_hist, zero_s, None, None,
                                          min(DELTA_BLOCK, tp), nc_p, min(KEY_TILE, tp))
    x1s, h2s, rs, ks, vs, ss, cs = mixer(x_s, ts, ts, bp, hist_s, s0_s, k_past, v_past,
                                          min(DELTA_BLOCK, ts), max(1, ts // DELTA_BLOCK), min(KEY_TILE, ts))

    h2_all = jnp.concatenate([h2p, h2s], axis=0)
    route = jnp.concatenate([rp, rs], axis=0)
    blk_e, nvalid, row_src, row_dst, row_w = _dispatch_plan(route, n_tok, n_tok)
    y_rows = _moe(blk_e, nvalid, row_src, row_dst, row_w, p['w_gate'].astype(BF16),
                  p['w_up'].astype(BF16), p['w_down'].astype(BF16), h2_all, 2 * n_tok)
    y_slots = y_rows.reshape(2, n_tok, d)
    y_p = _combine(y_slots, x1p, mod8, g_post_ffn, tm_p, tp, 0, 0).reshape(bp, tp, d)
    y_s = _combine(y_slots, x1s, mod8, g_post_ffn, ts, ts, bp, n_p).reshape(bs, ts, d)
    return y_p, y_s, kp, vp, sp, cp, ks, vs, ss, cs


def kernel(x_prompt, x_sample, c_prompt, c_sample, cache_k, cache_v, state_delta, state_conv, w_ada, b_ada, g_pre_mix, g_post_mix, g_pre_ffn, g_post_ffn, w_in, conv_w, a_log, dt_bias, onorm_a, onorm_b, w_out, w_router_group, b_router_group, w_router_expert, b_router_expert, w_gate, w_up, w_down):
    depth = w_in.shape[0]
    y_p, y_s = x_prompt, x_sample
    outs = [[] for _ in range(8)]
    for l in range(depth):
        p = dict(w_ada=w_ada[l], b_ada=b_ada[l], g_pre_mix=g_pre_mix[l], g_post_mix=g_post_mix[l],
                 g_pre_ffn=g_pre_ffn[l], g_post_ffn=g_post_ffn[l], w_in=w_in[l], conv_w=conv_w[l],
                 a_log=a_log[l], dt_bias=dt_bias[l], onorm_a=onorm_a[l], onorm_b=onorm_b[l],
                 w_out=w_out[l], w_router_group=w_router_group[l], b_router_group=b_router_group[l],
                 w_router_expert=w_router_expert[l], b_router_expert=b_router_expert[l],
                 w_gate=w_gate[l], w_up=w_up[l], w_down=w_down[l])
        res = _layer(y_p, y_s, c_prompt, c_sample, cache_k[l], cache_v[l], state_delta[l], state_conv[l], p)
        y_p, y_s = res[0], res[1]
        for lst, r in zip(outs, res[2:]):
            lst.append(r)
    return (y_p, y_s) + tuple(jnp.stack(o) for o in outs)
```

```python
import functools
import math

import jax
import jax.numpy as jnp
from jax import lax
from jax.experimental import pallas as pl
from jax.experimental.pallas import tpu as pltpu

F32 = jnp.float32
BF16 = jnp.bfloat16

D_MODEL = 1024
DN_HEADS = 4
DN_DIM = 128
CONV_W = 4
CONV_CH = DN_HEADS * 3 * DN_DIM
DELTA_BLOCK = 64
SB_HEADS = 8
SB_DIM = 64
SB_WIDTH = SB_HEADS * SB_DIM
N_GROUPS = 4
EXPERTS_PER_GROUP = 8
N_EXPERTS = N_GROUPS * EXPERTS_PER_GROUP
D_EXPERT = D_MODEL // 2
MOE_BLOCK = 128
EPS = 1e-6

LANES = 128
KEY_TILE = 128
EXP_ZERO_BELOW = -104.0
VMEM_LIMIT = 56 * 1024 * 1024


def _cparams(sem):
    return pltpu.CompilerParams(dimension_semantics=sem, vmem_limit_bytes=VMEM_LIMIT)


def _split(a):
    hi = a.astype(BF16)
    lo = (a - hi.astype(F32)).astype(BF16)
    return hi, lo


def _dot(a, b, dims=(((1,), (0,)), ((), ()))):
    return lax.dot_general(a, b, dims, preferred_element_type=F32)


def _dot3(a, b, dims=(((1,), (0,)), ((), ()))):
    ah, al = _split(a)
    bh, bl = _split(b)
    return _dot(ah, bh, dims) + (_dot(al, bh, dims) + _dot(ah, bl, dims))


_NT = (((1,), (1,)), ((), ()))
_TN = (((0,), (0,)), ((), ()))


def _silu(x):
    return x * jax.nn.sigmoid(x)


def _softplus(x):
    return jnp.maximum(x, 0.0) + jnp.log(1.0 + jnp.exp(-jnp.abs(x)))


def _ada_kernel(c_ref, w_ref, b_ref, o_ref):
    s = _silu(c_ref[...]).astype(BF16)
    o_ref[...] = _dot(s, w_ref[...].astype(BF16)) + b_ref[...]


def _ada(c_all, w_ada, b_ada):
    rows = c_all.shape[0]
    n = w_ada.shape[1]
    tn = 1024
    return pl.pallas_call(
        _ada_kernel,
        grid=(n // tn,),
        in_specs=[pl.BlockSpec((rows, D_MODEL), lambda j: (0, 0)),
                  pl.BlockSpec((D_MODEL, tn), lambda j: (0, j)),
                  pl.BlockSpec((1, tn), lambda j: (0, j))],
        out_specs=pl.BlockSpec((rows, tn), lambda j: (0, j)),
        out_shape=jax.ShapeDtypeStruct((rows, n), F32),
        compiler_params=_cparams(("arbitrary",)),
        name="ada",
    )(c_all, w_ada, b_ada.reshape(1, n))


def _rms(x, gain):
    return x * lax.rsqrt(jnp.mean(x * x, axis=-1, keepdims=True) + EPS) * gain


def _proj_kernel(x_ref, mod_ref, g_ref, wm_ref, wbh_ref, wbl_ref,
                 a_ref, z_ref, bg_ref, q_ref, k_ref, v_ref, k16_ref, v16_ref):
    mod = mod_ref[0]
    h = _rms(x_ref[...], g_ref[...]) * (1.0 + mod[1:2]) + mod[0:1]
    hh, hl = _split(h)
    p = _dot(hh, wm_ref[...])
    a_ref[...] = p[:, 0:CONV_CH]
    z_ref[...] = p[:, CONV_CH:CONV_CH + 512]
    o = CONV_CH + 512
    q_ref[...] = (p[:, o:o + 512] * (SB_DIM ** -0.5)).astype(BF16)
    k = p[:, o + 512:o + 1024]
    v = p[:, o + 1024:o + 1536]
    k_ref[...] = k
    v_ref[...] = v
    k16_ref[...] = k.astype(BF16)
    v16_ref[...] = v.astype(BF16)
    wbh = wbh_ref[...]
    bg_ref[...] = _dot(hh, wbh) + (_dot(hl, wbh) + _dot(hh, wbl_ref[...]))


def _proj(x2d, mod8, g_pre, w_main, wb_hi, wb_lo, tm, seq_rows, mod_row0):
    n = x2d.shape[0]
    nm = w_main.shape[1]
    row = lambda i: (i, 0)
    const = lambda i: (0, 0)
    modmap = lambda i: (mod_row0 + (i * tm) // seq_rows, 0, 0)
    outs = [(CONV_CH, F32), (512, F32), (LANES, F32), (512, BF16), (512, F32), (512, F32),
            (512, BF16), (512, BF16)]
    return pl.pallas_call(
        _proj_kernel,
        grid=(n // tm,),
        in_specs=[pl.BlockSpec((tm, D_MODEL), row),
                  pl.BlockSpec((1, 8, D_MODEL), modmap),
                  pl.BlockSpec((1, D_MODEL), const),
                  pl.BlockSpec((D_MODEL, nm), const),
                  pl.BlockSpec((D_MODEL, LANES), const),
                  pl.BlockSpec((D_MODEL, LANES), const)],
        out_specs=[pl.BlockSpec((tm, w), row) for w, _ in outs],
        out_shape=[jax.ShapeDtypeStruct((n, w), dt) for w, dt in outs],
        compiler_params=_cparams(("arbitrary",)),
        name="proj",
    )(x2d, mod8, g_pre, w_main, wb_hi, wb_lo)


def _delta_kernel(a_ref, z_ref, bg_ref, hist0_ref, s0_ref, cw_ref, alog_ref, dtb_ref, on_ref,
                  o_ref, sfin_ref, hist_sc, s_sc, *, chunk, n_chunks):
    t_idx = pl.program_id(1)
    tt = chunk * n_chunks

    @pl.when(t_idx == 0)
    def _():
        hist_sc[...] = hist0_ref[0]
        s_sc[...] = s0_ref[0]

    x = a_ref[0]
    xx = jnp.concatenate([hist_sc[...], x], axis=0)
    cw = cw_ref[...]
    conv = x * cw[CONV_W - 1:CONV_W]
    for s in range(1, CONV_W):
        conv = conv + pltpu.roll(xx, s, 0)[8:] * cw[CONV_W - 1 - s:CONV_W - s]
    conv = _silu(conv)
    hist_sc[...] = x[tt - 8:tt]

    bg = bg_ref[0]
    lane = lax.broadcasted_iota(jnp.int32, (1, LANES), 1)
    g_lane = (lane >= DN_HEADS) & (lane < 2 * DN_HEADS)
    neg_a = jnp.where(g_lane, -jnp.exp(alog_ref[...]), 0.0)
    beta_all = jax.nn.sigmoid(bg)
    g_all = neg_a * _softplus(bg + dtb_ref[...])

    ri = lax.broadcasted_iota(jnp.int32, (chunk, chunk), 0)
    ci = lax.broadcasted_iota(jnp.int32, (chunk, chunk), 1)
    incl = ri >= ci
    strict = ri > ci
    tri = incl.astype(F32)
    eye = (ri == ci).astype(F32)
    ones_cc = jnp.ones((chunk, chunk), F32)
    onorm = on_ref[...]

    chunks = range(n_chunks)
    units = [(c, h) for c in chunks for h in range(DN_HEADS)]
    rows = lambda c: slice(c * chunk, (c + 1) * chunk)

    gc_all = [_dot3(tri, g_all[rows(c)]) for c in chunks]
    diag = [jnp.concatenate([eye * gc_all[c][:, DN_HEADS + h:DN_HEADS + h + 1] for h in range(DN_HEADS)], axis=1)
            for c in chunks]
    gc_rows = [_dot3(ones_cc, diag[c]) for c in chunks]

    q, k, v, beta, gcol, glast, eg, decay, kbeta = {}, {}, {}, {}, {}, {}, {}, {}, {}
    for (c, h) in units:
        lo = h * DN_DIM
        qq = conv[rows(c), lo:lo + DN_DIM]
        kk = conv[rows(c), 512 + lo:512 + lo + DN_DIM]
        u_ = (c, h)
        q[u_] = qq * lax.rsqrt(jnp.sum(qq * qq, axis=-1, keepdims=True) + EPS) * (DN_DIM ** -0.5)
        k[u_] = kk * lax.rsqrt(jnp.sum(kk * kk, axis=-1, keepdims=True) + EPS)
        v[u_] = conv[rows(c), 1024 + lo:1024 + lo + DN_DIM]
        beta[u_] = beta_all[rows(c), h:h + 1]
        gcol[u_] = gc_all[c][:, DN_HEADS + h:DN_HEADS + h + 1]
        glast[u_] = gcol[u_][chunk - 1:chunk, :]
        eg[u_] = jnp.exp(gcol[u_])
        grow = gc_rows[c][:, h * chunk:(h + 1) * chunk]
        decay[u_] = jnp.where(incl, jnp.exp(jnp.minimum(gcol[u_] - grow, 0.0)), 0.0)
        kbeta[u_] = k[u_] * beta[u_]

    k16 = {u_: k[u_].astype(BF16) for u_ in units}
    kq = {u_: _dot(jnp.concatenate([kbeta[u_], q[u_]], axis=0).astype(BF16), k16[u_], _NT) for u_ in units}
    lower = {u_: jnp.where(strict, kq[u_][:chunk] * decay[u_], 0.0) for u_ in units}
    attn16 = {u_: jnp.where(incl, kq[u_][chunk:] * decay[u_], 0.0).astype(BF16) for u_ in units}

    sol = {u_: jnp.concatenate([v[u_] * beta[u_], kbeta[u_] * eg[u_]], axis=1) for u_ in units}
    lp = lower
    p = 1
    while p < chunk:
        lsp = {u_: _split(lp[u_]) for u_ in units}
        ssp = {u_: _split(sol[u_]) for u_ in units}
        stack = {u_: jnp.concatenate(lsp[u_], axis=0) for u_ in units}
        p1 = {u_: _dot(stack[u_], ssp[u_][0]) for u_ in units}
        p2 = {u_: _dot(lsp[u_][0], ssp[u_][1]) for u_ in units}
        upd = {u_: p1[u_][:chunk] + (p1[u_][chunk:] + p2[u_]) for u_ in units}
        sol = {u_: (sol[u_] - upd[u_]) if p == 1 else (sol[u_] + upd[u_]) for u_ in units}
        if 2 * p < chunk:
            q1 = {u_: _dot(stack[u_], lsp[u_][0]) for u_ in units}
            q2 = {u_: _dot(lsp[u_][0], lsp[u_][1]) for u_ in units}
            lp = {u_: q1[u_][:chunk] + (q1[u_][chunk:] + q2[u_]) for u_ in units}
        p *= 2

    usol = {u_: sol[u_][:, :DN_DIM] for u_ in units}
    wq16 = {u_: jnp.concatenate([sol[u_][:, DN_DIM:], q[u_] * eg[u_]], axis=0).astype(BF16) for u_ in units}
    kd16 = {u_: (k[u_] * jnp.exp(glast[u_] - gcol[u_])).astype(BF16) for u_ in units}

    for c in chunks:
        hs = range(DN_HEADS)
        s_old = [s_sc[h] for h in hs]
        ws = [_dot(wq16[(c, h)], s_old[h].astype(BF16)) for h in hs]
        v_new = [(usol[(c, h)] - ws[h][:chunk]).astype(BF16) for h in hs]
        o_in = [_dot(attn16[(c, h)], v_new[h]) for h in hs]
        ds = [_dot(kd16[(c, h)], v_new[h], _TN) for h in hs]
        for h in hs:
            lo = h * DN_DIM
            s_sc[h] = s_old[h] * jnp.exp(glast[(c, h)]) + ds[h]
            o = ws[h][chunk:] + o_in[h]
            zg = z_ref[0, rows(c), lo:lo + DN_DIM]
            o_ref[0, rows(c), lo:lo + DN_DIM] = (_rms(o, onorm) * _silu(zg)).astype(BF16)

    @pl.when(t_idx == pl.num_programs(1) - 1)
    def _():
        sfin_ref[0] = s_sc[...]


def _delta(a_in, z, bg, hist8, s0, conv_w8, alog_row, dtb_row, onorm_a, chunk, n_chunks):
    b, t, _ = a_in.shape
    tt = chunk * n_chunks
    tile = lambda bi, ti: (bi, ti, 0)
    per_b3 = lambda bi, ti: (bi, 0, 0)
    per_b4 = lambda bi, ti: (bi, 0, 0, 0)
    const = lambda bi, ti: (0, 0)
    kern = functools.partial(_delta_kernel, chunk=chunk, n_chunks=n_chunks)
    return pl.pallas_call(
        kern,
        grid=(b, t // tt),
        in_specs=[pl.BlockSpec((1, tt, CONV_CH), tile),
                  pl.BlockSpec((1, tt, 512), tile),
                  pl.BlockSpec((1, tt, LANES), tile),
                  pl.BlockSpec((1, 8, CONV_CH), per_b3),
                  pl.BlockSpec((1, DN_HEADS, DN_DIM, DN_DIM), per_b4),
                  pl.BlockSpec((8, CONV_CH), const),
                  pl.BlockSpec((1, LANES), const),
                  pl.BlockSpec((1, LANES), const),
                  pl.BlockSpec((1, DN_DIM), const)],
        out_specs=[pl.BlockSpec((1, tt, 512), tile),
                   pl.BlockSpec((1, DN_HEADS, DN_DIM, DN_DIM), per_b4)],
        out_shape=[jax.ShapeDtypeStruct((b, t, 512), BF16),
                   jax.ShapeDtypeStruct((b, DN_HEADS, DN_DIM, DN_DIM), F32)],
        scratch_shapes=[pltpu.VMEM((8, CONV_CH), F32),
                        pltpu.VMEM((DN_HEADS, DN_DIM, DN_DIM), F32)],
        compiler_params=_cparams(("arbitrary", "arbitrary")),
        name="delta",
    )(a_in, z, bg, hist8, s0, conv_w8, alog_row, dtb_row, onorm_a)


def _sb_kernel(q_ref, k0_ref, k1_ref, k2_ref, v0_ref, v1_ref, v2_ref, k_hbm, v_hbm, on_ref,
               o_ref, kbuf, vbuf, qsel, acc, carry, sem, *, bq, n_pad, q_off):
    b = pl.program_id(0)
    i = pl.program_id(1)
    qend = q_off + (i + 1) * bq
    last_tile = qend // KEY_TILE - 1
    n_tiles = last_tile + 1

    acc[...] = jnp.zeros_like(acc)
    carry[...] = jnp.zeros_like(carry)
    half_lane = lax.broadcasted_iota(jnp.int32, (bq, LANES), 1) < SB_DIM
    for p in range(SB_HEADS // 2):
        qf = q_ref[0, :, p * LANES:(p + 1) * LANES].astype(F32)
        qsel[2 * p] = jnp.where(half_lane, qf, 0.0).astype(BF16)
        qsel[2 * p + 1] = jnp.where(half_lane, 0.0, qf).astype(BF16)

    qpos =q_off + i * bq + lax.broadcasted_iota(jnp.int32, (bq, KEY_TILE), 0)
    col = lax.broadcasted_iota(jnp.int32, (bq, KEY_TILE), 1)
    lane = lax.broadcasted_iota(jnp.int32, (1, LANES), 1)
    low_half = lane < SB_DIM
    rj = lax.broadcasted_iota(jnp.int32, (KEY_TILE, 2 * KEY_TILE), 0)
    cj = lax.broadcasted_iota(jnp.int32, (KEY_TILE, 2 * KEY_TILE), 1)
    suffix = ((rj > cj) | (cj >= KEY_TILE)).astype(BF16)

    def load_tile(t):
        for kk, (kr, vr) in enumerate(((k0_ref, v0_ref), (k1_ref, v1_ref), (k2_ref, v2_ref))):
            @pl.when(t == kk)
            def _():
                for p in range(SB_HEADS // 2):
                    kbuf[p] = kr[0, :, p * LANES:(p + 1) * LANES]
                    vbuf[p] = vr[0, :, p * LANES:(p + 1) * LANES]

        @pl.when(t >= 3)
        def _():
            start = pl.multiple_of((last_tile - t) * KEY_TILE, KEY_TILE)
            copies = []
            for p in range(SB_HEADS // 2):
                for src, dst in ((k_hbm, kbuf), (v_hbm, vbuf)):
                    cp = pltpu.make_async_copy(
                        src.at[b, pl.ds(start, KEY_TILE), pl.ds(p * LANES, LANES)], dst.at[p], sem)
                    cp.start()
                    copies.append(cp)
            for cp in copies:
                cp.wait()

    def body(state):
        t, _ = state
        load_tile(t)
        kpos = (last_tile - t) * KEY_TILE + col
        mask = (kpos < qpos) & (kpos >= n_pad)
        heads = range(SB_HEADS)
        z = [_dot(qsel[h], kbuf[h // 2], _NT) for h in heads]
        sp = [_softplus(z[h]) for h in heads]
        pieces = []
        for h in heads:
            pieces.extend(_split(jnp.where(mask, -sp[h], 0.0)))
        cs_all = _dot(jnp.concatenate(pieces, axis=0), suffix)
        cs = [cs_all[2 * h * bq:(2 * h + 1) * bq] + cs_all[(2 * h + 1) * bq:(2 * h + 2) * bq] for h in heads]
        c_old = [carry[h] for h in heads]
        a = [jnp.where(mask, jnp.exp((z[h] - sp[h]) + cs[h][:, :KEY_TILE] + c_old[h]), 0.0).astype(BF16)
             for h in heads]
        pv = [_dot(a[h], vbuf[h // 2]) for h in heads]
        for p in range(SB_HEADS // 2):
            acc[p] = acc[p] + jnp.where(low_half, pv[2 * p], pv[2 * p + 1])
        worst = jnp.full((bq, KEY_TILE), -jnp.inf, F32)
        for h in heads:
            c_new = c_old[h] + cs[h][:, KEY_TILE:]
            carry[h] = c_new
            worst = jnp.maximum(worst, c_new)
        done = (jnp.max(worst) < EXP_ZERO_BELOW).astype(jnp.int32)
        return t + 1, done

    def cond(state):
        t, done = state
        return (t < n_tiles) & (done == 0)

    lax.while_loop(cond, body, (jnp.int32(0), jnp.int32(0)))

    onb = on_ref[...]
    for p in range(SB_HEADS // 2):
        o = acc[p]
        sq = o * o
        s_lo = jnp.sum(jnp.where(low_half, sq, 0.0), axis=-1, keepdims=True)
        s_hi = jnp.sum(jnp.where(low_half, 0.0, sq), axis=-1, keepdims=True)
        ms = jnp.where(low_half, s_lo, s_hi) * (1.0 / SB_DIM)
        o_ref[0, :, p * LANES:(p + 1) * LANES] = (o * lax.rsqrt(ms + EPS) * onb).astype(BF16)


def _sb_attn(q16, k16p, v16p, onorm_b2, bq, n_pad):
    b, tq, _ = q16.shape
    tkp = k16p.shape[1]
    q_off = tkp - tq
    assert tkp % KEY_TILE == 0 and KEY_TILE % bq == 0 and tq % bq == 0 and q_off % bq == 0

    def kmap(back):
        def f(bi, i):
            last = (q_off + (i + 1) * bq) // KEY_TILE - 1
            return (bi, jnp.maximum(last - back, 0), 0)
        return f

    qmap = lambda bi, i: (bi, i, 0)
    kern = functools.partial(_sb_kernel, bq=bq, n_pad=n_pad, q_off=q_off)
    kspec = [pl.BlockSpec((1, KEY_TILE, SB_WIDTH), kmap(back)) for back in range(3)]
    return pl.pallas_call(
        kern,
        grid=(b, tq // bq),
        in_specs=[pl.BlockSpec((1, bq, SB_WIDTH), qmap)] + kspec + kspec
                 + [pl.BlockSpec(memory_space=pl.ANY), pl.BlockSpec(memory_space=pl.ANY),
                    pl.BlockSpec((1, LANES), lambda bi, i: (0, 0))],
        out_specs=pl.BlockSpec((1, bq, SB_WIDTH), qmap),
        out_shape=jax.ShapeDtypeStruct((b, tq, SB_WIDTH), BF16),
        scratch_shapes=[pltpu.VMEM((SB_HEADS // 2, KEY_TILE, LANES), BF16),
                        pltpu.VMEM((SB_HEADS // 2, KEY_TILE, LANES), BF16),
                        pltpu.VMEM((SB_HEADS, bq, LANES), BF16),
                        pltpu.VMEM((SB_HEADS // 2, bq, LANES), F32),
                        pltpu.VMEM((SB_HEADS, bq, KEY_TILE), F32),
                        pltpu.SemaphoreType.DMA(())],
        compiler_params=_cparams(("arbitrary", "arbitrary")),
        name="sb_attn",
    )(q16, k16p, k16p, k16p, v16p, v16p, v16p, k16p, v16p, onorm_b2)


def _post_kernel(oa_ref, ob_ref, x_ref, mod_ref, gpm_ref, gpf_ref, wo_ref, wrh_ref, wrl_ref, br_ref, cnt0_ref,
                 x1_ref, h2_ref, route_ref, cnt_ref):
    @pl.when(pl.program_id(0) == 0)
    def _():
        cnt_ref[...] = cnt0_ref[...]

    mod = mod_ref[0]
    mix = _dot(oa_ref[...], wo_ref[0:512, :]) + _dot(ob_ref[...], wo_ref[512:1024, :])
    x1 = x_ref[...] + mod[2:3] * _rms(mix, gpm_ref[...])
    x1_ref[...] = x1
    h2 = _rms(x1, gpf_ref[...]) * (1.0 + mod[4:5]) + mod[3:4]
    h2_ref[...] = h2
    hh, hl = _split(h2)
    wrh = wrh_ref[...]
    logits = _dot(hh, wrh) + (_dot(hl, wrh) + _dot(hh, wrl_ref[...])) + br_ref[...]
    lane = lax.broadcasted_iota(jnp.int32, logits.shape, 1).astype(F32)
    neg = -jnp.inf
    nl = float(LANES)
    lg = jnp.where(lane < N_GROUPS, logits, neg)
    gmax = jnp.max(lg, axis=-1, keepdims=True)
    grp = jnp.min(jnp.where(lg == gmax, lane, nl), axis=-1, keepdims=True)
    p_grp = 1.0 / jnp.sum(jnp.exp(lg - gmax), axis=-1, keepdims=True)
    first = N_GROUPS + grp * EXPERTS_PER_GROUP
    le = jnp.where((lane >= first) & (lane < first + EXPERTS_PER_GROUP), logits, neg)
    emax = jnp.max(le, axis=-1, keepdims=True)
    i1 = jnp.min(jnp.where(le == emax, lane, nl), axis=-1, keepdims=True)
    esum = jnp.sum(jnp.exp(le - emax), axis=-1, keepdims=True)
    le2 = jnp.where(lane == i1, neg, le)
    e2max = jnp.max(le2, axis=-1, keepdims=True)
    i2 = jnp.min(jnp.where(le2 == e2max, lane, nl), axis=-1, keepdims=True)
    p1 = 1.0 / esum
    p2 = jnp.exp(e2max - emax) / esum
    w1 = p_grp * p1 / (p1 + p2)
    w2 = p_grp * p2 / (p1 + p2)
    e1 = i1 - N_GROUPS
    e2 = i2 - N_GROUPS
    hot1 = (lane == e1).astype(F32)
    hot2 = (lane == e2).astype(F32)
    both = hot1 + hot2
    tm = logits.shape[0]
    ti = lax.broadcasted_iota(jnp.int32, (tm, tm), 0)
    tj = lax.broadcasted_iota(jnp.int32, (tm, tm), 1)
    earlier = _dot((ti > tj).astype(BF16), both.astype(BF16)) + cnt_ref[...]
    rank1 = jnp.sum(hot1 * earlier, axis=-1, keepdims=True)
    rank2 = jnp.sum(hot2 * (earlier + hot1), axis=-1, keepdims=True)
    cnt_ref[...] = cnt_ref[...] + jnp.sum(both, axis=0, keepdims=True)
    out = jnp.where(lane == 0.0, e1, 0.0)
    out = jnp.where(lane == 1.0, e2, out)
    out = jnp.where(lane == 2.0, w1, out)
    out = jnp.where(lane == 3.0, w2, out)
    out = jnp.where(lane == 4.0, rank1, out)
    out = jnp.where(lane == 5.0, rank2, out)
    route_ref[...] = out


def _post(oa16, ob16, x2d, mod8, g_post_mix, g_pre_ffn, w_out16, wr_hi, wr_lo, b_r, cnt0, tm, seq_rows,
          mod_row0):
    n = x2d.shape[0]
    row = lambda i: (i, 0)
    const = lambda i: (0, 0)
    modmap = lambda i: (mod_row0 + (i * tm) // seq_rows, 0, 0)
    return pl.pallas_call(
        _post_kernel,
        grid=(n // tm,),
        in_specs=[pl.BlockSpec((tm, 512), row),
                  pl.BlockSpec((tm, 512), row),
                  pl.BlockSpec((tm, D_MODEL), row),
                  pl.BlockSpec((1, 8, D_MODEL), modmap),
                  pl.BlockSpec((1, D_MODEL), const),
                  pl.BlockSpec((1, D_MODEL), const),
                  pl.BlockSpec((D_MODEL, D_MODEL), const),
                  pl.BlockSpec((D_MODEL, LANES), const),
                  pl.BlockSpec((D_MODEL, LANES), const),
                  pl.BlockSpec((1, LANES), const),
                  pl.BlockSpec((1, LANES), const)],
        out_specs=[pl.BlockSpec((tm, D_MODEL), row),
                   pl.BlockSpec((tm, D_MODEL), row),
                   pl.BlockSpec((tm, LANES), row),
                   pl.BlockSpec((1, LANES), const)],
        out_shape=[jax.ShapeDtypeStruct((n, D_MODEL), F32),
                   jax.ShapeDtypeStruct((n, D_MODEL), F32),
                   jax.ShapeDtypeStruct((n, LANES), F32),
                   jax.ShapeDtypeStruct((1, LANES), F32)],
        compiler_params=_cparams(("arbitrary",)),
        name="post",
    )(oa16, ob16, x2d, mod8, g_post_mix, g_pre_ffn, w_out16, wr_hi, wr_lo, b_r, cnt0)


def _dispatch_kernel(seg_ref, dp_ref, ds_ref, hp_ref, hs_ref, xs_hbm, zbuf, sem, *, n_blocks):
    i = pl.program_id(0)
    last_step = pl.num_programs(0) - 1

    @pl.when(i == 0)
    def _():
        zbuf[...] = jnp.zeros_like(zbuf)

        def zero_block(row0):
            return pltpu.make_async_copy(zbuf, xs_hbm.at[pl.ds(pl.multiple_of(row0, MOE_BLOCK), MOE_BLOCK), :], sem)

        for e in range(N_EXPERTS):
            @pl.when(seg_ref[e] > 0)
            def _():
                zero_block(seg_ref[N_EXPERTS + e] - MOE_BLOCK).start()
        for e in range(N_EXPERTS):
            @pl.when(seg_ref[e] > 0)
            def _():
                zero_block(seg_ref[N_EXPERTS + e] - MOE_BLOCK).wait()

        used = seg_ref[2 * N_EXPERTS - 1] // MOE_BLOCK

        def fill(b, c):
            cp = zero_block(b * MOE_BLOCK)
            cp.start()
            cp.wait()
            return c

        lax.fori_loop(used, n_blocks, fill, 0)

    def scatter(h_ref, dest_ref):
        rows = h_ref.shape[0]
        for t in range(rows):
            for slot in range(2):
                pltpu.make_async_copy(h_ref.at[pl.ds(t, 1), :],
                                      xs_hbm.at[pl.ds(dest_ref[0, 0, 2 * t + slot], 1), :], sem).start()
        for slot in range(2):
            pltpu.make_async_copy(h_ref, xs_hbm.at[pl.ds(0, rows), :], sem).wait()

    @pl.when(i < last_step)
    def _():
        scatter(hp_ref, dp_ref)

    @pl.when(i == last_step)
    def _():
        scatter(hs_ref, ds_ref)


def _dispatch(seg, dest_p, dest_s, h2p, h2s, n_blocks, tm):
    n_p, n_s = h2p.shape[0], h2s.shape[0]
    steps_p = n_p // tm
    pmap3 = lambda i, sg: (jnp.minimum(i, steps_p - 1), 0, 0)
    pmap2 = lambda i, sg: (jnp.minimum(i, steps_p - 1), 0)
    grid_spec = pltpu.PrefetchScalarGridSpec(
        num_scalar_prefetch=1,
        grid=(steps_p + 1,),
        in_specs=[pl.BlockSpec((1, 1, 2 * tm), pmap3, memory_space=pltpu.SMEM),
                  pl.BlockSpec((1, 1, 2 * n_s), lambda i, sg: (0, 0, 0), memory_space=pltpu.SMEM),
                  pl.BlockSpec((tm, D_MODEL), pmap2),
                  pl.BlockSpec((n_s, D_MODEL), lambda i, sg: (0, 0))],
        out_specs=pl.BlockSpec(memory_space=pl.ANY),
        scratch_shapes=[pltpu.VMEM((MOE_BLOCK, D_MODEL), F32), pltpu.SemaphoreType.DMA(())])
    return pl.pallas_call(
        functools.partial(_dispatch_kernel, n_blocks=n_blocks),
        grid_spec=grid_spec,
        out_shape=jax.ShapeDtypeStruct((n_blocks * MOE_BLOCK, D_MODEL), F32),
        compiler_params=_cparams(("arbitrary",)),
        name="dispatch",
    )(seg, dest_p.reshape(steps_p, 1, 2 * tm), dest_s.reshape(1, 1, 2 * n_s), h2p, h2s)


def _moe_kernel(blk_e_ref, nvalid_ref, x_ref, wg_ref, wu_ref, wd_ref, y_ref, wg16, wu16, wd16):
    i = pl.program_id(0)
    e = blk_e_ref[i]
    e_prev = blk_e_ref[jnp.maximum(i - 1, 0)]

    @pl.when((i == 0) | (e != e_prev))
    def _():
        wg16[...] = wg_ref[0].astype(BF16)
        wu16[...] = wu_ref[0].astype(BF16)
        wd16[...] = wd_ref[0].astype(BF16)

    @pl.when(nvalid_ref[i] > 0)
    def _():
        xb = x_ref[...].astype(BF16)
        g = _dot(xb, wg16[...])
        u = _dot(xb, wu16[...])
        hmid = (_silu(g) * u).astype(BF16)
        y_ref[...] = _dot(hmid, wd16[...])

    @pl.when(nvalid_ref[i] == 0)
    def _():
        y_ref[...] = jnp.zeros_like(y_ref)


def _moe(blk_e, nvalid, x_sorted, w_gate, w_up, w_down):
    n_blocks = blk_e.shape[0]
    wmap = lambda i, be, nv: (be[i], 0, 0)
    xmap = lambda i, be, nv: (jnp.where(nv[i] > 0, i, 0), 0)
    grid_spec = pltpu.PrefetchScalarGridSpec(
        num_scalar_prefetch=2,
        grid=(n_blocks,),
        in_specs=[pl.BlockSpec((MOE_BLOCK, D_MODEL), xmap),
                  pl.BlockSpec((1, D_MODEL, D_EXPERT), wmap),
                  pl.BlockSpec((1, D_MODEL, D_EXPERT), wmap),
                  pl.BlockSpec((1, D_EXPERT, D_MODEL), wmap)],
        out_specs=pl.BlockSpec((MOE_BLOCK, D_MODEL), lambda i, be, nv: (i, 0)),
        scratch_shapes=[pltpu.VMEM((D_MODEL, D_EXPERT), BF16),
                        pltpu.VMEM((D_MODEL, D_EXPERT), BF16),
                        pltpu.VMEM((D_EXPERT, D_MODEL), BF16)])
    return pl.pallas_call(
        _moe_kernel,
        grid_spec=grid_spec,
        out_shape=jax.ShapeDtypeStruct((n_blocks * MOE_BLOCK, D_MODEL), F32),
        compiler_params=_cparams(("arbitrary",)),
        name="moe",
    )(blk_e, nvalid, x_sorted, w_gate, w_up, w_down)


def _combine_kernel(dest_ref, route_ref, x1_ref, mod_ref, g_ref, y_hbm, o_ref, ybuf, sem):
    tm = x1_ref.shape[0]
    for t in range(tm):
        for slot in range(2):
            pltpu.make_async_copy(y_hbm.at[pl.ds(dest_ref[0, 0, 2 * t + slot], 1), :],
                                  ybuf.at[slot, pl.ds(t, 1), :], sem).start()
    for slot in range(2):
        pltpu.make_async_copy(y_hbm.at[pl.ds(0, tm), :], ybuf.at[slot], sem).wait()
    mod = mod_ref[0]
    route = route_ref[...]
    moe = ybuf[0] * route[:, 2:3] + ybuf[1] * route[:, 3:4]
    o_ref[...] = x1_ref[...] + mod[5:6] * _rms(moe, g_ref[...])


def _combine(dest, route, y_sorted, x1, mod8, g_post_ffn, tm, seq_rows, mod_row0):
    n = x1.shape[0]
    row = lambda i: (i, 0)
    modmap = lambda i: (mod_row0 + (i * tm) // seq_rows, 0, 0)
    return pl.pallas_call(
        _combine_kernel,
        grid=(n // tm,),
        in_specs=[pl.BlockSpec((1, 1, 2 * tm), lambda i: (i, 0, 0), memory_space=pltpu.SMEM),
                  pl.BlockSpec((tm, LANES), row),
                  pl.BlockSpec((tm, D_MODEL), row),
                  pl.BlockSpec((1, 8, D_MODEL), modmap),
                  pl.BlockSpec((1, D_MODEL), lambda i: (0, 0)),
                  pl.BlockSpec(memory_space=pl.ANY)],
        out_specs=pl.BlockSpec((tm, D_MODEL), row),
        out_shape=jax.ShapeDtypeStruct((n, D_MODEL), F32),
        scratch_shapes=[pltpu.VMEM((2, tm, D_MODEL), F32), pltpu.SemaphoreType.DMA(())],
        compiler_params=_cparams(("arbitrary",)),
        name="combine",
    )(dest.reshape(n // tm, 1, 2 * tm), route, x1, mod8, g_post_ffn, y_sorted)


def _segment_plan(counts_f, n_blocks):
    counts = counts_f[0, :N_EXPERTS].astype(jnp.int32)
    padded = (counts + MOE_BLOCK - 1) // MOE_BLOCK * MOE_BLOCK
    pad_end = jnp.cumsum(padded)
    pad_start = pad_end - padded
    blk_start = jnp.arange(n_blocks, dtype=jnp.int32) * MOE_BLOCK
    blk_e = jnp.minimum(jnp.sum((pad_end[None, :] <= blk_start[:, None]).astype(jnp.int32), axis=1),
                        N_EXPERTS - 1)
    onehot = blk_e[:, None] == jnp.arange(N_EXPERTS, dtype=jnp.int32)[None, :]
    c_blk = jnp.sum(jnp.where(onehot, counts[None, :], 0), axis=1)
    s_blk = jnp.sum(jnp.where(onehot, pad_start[None, :], 0), axis=1)
    nvalid = jnp.clip(c_blk - (blk_start - s_blk), 0, MOE_BLOCK).astype(jnp.int32)
    seg = jnp.concatenate([counts, pad_end]).astype(jnp.int32)
    return blk_e.astype(jnp.int32), nvalid, seg, pad_start


def _token_rows(route, pad_start):
    eid = route[:, 0:2].astype(jnp.int32)
    rank = route[:, 4:6].astype(jnp.int32)
    onehot = eid[:, :, None] == jnp.arange(N_EXPERTS, dtype=jnp.int32)[None, None, :]
    return rank + jnp.sum(jnp.where(onehot, pad_start[None, None, :], 0), axis=2)


def _layer(x_p, x_s, c_p, c_s, k_past, v_past, s0_s, conv_s, p):
    bp, tp, d = x_p.shape
    bs, ts, _ = x_s.shape
    n_p, n_s = bp * tp, bs * ts
    n_tok = n_p + n_s

    n_seq = bp + bs
    c_all = jnp.zeros((16, d), F32).at[:n_seq].set(jnp.concatenate([c_p, c_s], axis=0))
    mod = _ada(c_all, p['w_ada'], p['b_ada'])
    mod8 = jnp.pad(mod.reshape(16, 6, d), ((0, 0), (0, 2), (0, 0)))

    w_in = p['w_in']
    o_z, o_b, o_q = CONV_CH, CONV_CH + 512, CONV_CH + 512 + 2 * DN_HEADS
    w_main = jnp.concatenate([w_in[:, :o_b], w_in[:, o_q:]], axis=1).astype(BF16)
    wb = jnp.pad(w_in[:, o_b:o_q], ((0, 0), (0, LANES - 2 * DN_HEADS)))
    wb_hi = wb.astype(BF16)
    wb_lo = (wb - wb_hi.astype(F32)).astype(BF16)
    g_pre_mix = p['g_pre_mix'].reshape(1, d)

    conv_w8 = jnp.pad(p['conv_w'], ((0, 8 - CONV_W), (0, 0)))
    pad_g = lambda a: jnp.pad(a.reshape(1, DN_HEADS), ((0, 0), (DN_HEADS, LANES - 2 * DN_HEADS)))
    alog_row, dtb_row = pad_g(p['a_log']), pad_g(p['dt_bias'])
    onorm_a = p['onorm_a'].reshape(1, DN_DIM)
    onorm_b2 = jnp.tile(p['onorm_b'].reshape(1, SB_DIM), (1, 2))

    w_out16 = p['w_out'].astype(BF16)
    wr = jnp.pad(jnp.concatenate([p['w_router_group'], p['w_router_expert']], axis=1),
                 ((0, 0), (0, LANES - N_GROUPS - N_EXPERTS)))
    wr_hi = wr.astype(BF16)
    wr_lo = (wr - wr_hi.astype(F32)).astype(BF16)
    b_r = jnp.pad(jnp.concatenate([p['b_router_group'], p['b_router_expert']]).reshape(1, -1),
                  ((0, 0), (0, LANES - N_GROUPS - N_EXPERTS)))
    g_post_mix = p['g_post_mix'].reshape(1, d)
    g_pre_ffn = p['g_pre_ffn'].reshape(1, d)
    g_post_ffn = p['g_post_ffn'].reshape(1, d)

    def mixer(x, tm, seq_rows, mod_row0, hist8, s0, k_old, v_old, chunk, n_chunks, bq, cnt0):
        b, t, _ = x.shape
        x2d = x.reshape(b * t, d)
        a_in, z, bg, q16, kb, vb, k16, v16 = _proj(x2d, mod8, g_pre_mix, w_main, wb_hi, wb_lo,
                                                    tm, seq_rows, mod_row0)
        r3 = lambda a: a.reshape(b, t, a.shape[-1])
        oa16, s_new = _delta(r3(a_in), r3(z), r3(bg), hist8, s0, conv_w8, alog_row, dtb_row, onorm_a,
                             chunk, n_chunks)
        k16, v16 = r3(k16), r3(v16)
        if k_old is not None:
            k16 = jnp.concatenate([k_old.reshape(b, -1, SB_WIDTH).astype(BF16), k16], axis=1)
            v16 = jnp.concatenate([v_old.reshape(b, -1, SB_WIDTH).astype(BF16), v16], axis=1)
        n_pad = (-k16.shape[1]) % KEY_TILE
        k16 = jnp.pad(k16, ((0, 0), (n_pad, 0), (0, 0)))
        v16 = jnp.pad(v16, ((0, 0), (n_pad, 0), (0, 0)))
        ob16 = _sb_attn(r3(q16), k16, v16, onorm_b2, bq, n_pad)
        x1, h2, route, cnt = _post(oa16.reshape(b * t, 512), ob16.reshape(b * t, 512), x2d, mod8,
                                   g_post_mix, g_pre_ffn, w_out16, wr_hi, wr_lo, b_r, cnt0, tm, seq_rows,
                                   mod_row0)
        new_conv = r3(a_in)[:, t - (CONV_W - 1):, :]
        return (x1, h2, route, cnt, kb.reshape(b, t, SB_HEADS, SB_DIM), vb.reshape(b, t, SB_HEADS, SB_DIM),
                s_new, new_conv)

    zero_hist = jnp.zeros((bp, 8, CONV_CH), F32)
    zero_s = jnp.zeros((bp, DN_HEADS, DN_DIM, DN_DIM), F32)
    hist_s = jnp.pad(conv_s, ((0, 0), (8 - (CONV_W - 1), 0), (0, 0)))
    tm_p = min(256, tp)
    nc_p = max(1, min(4, tp // DELTA_BLOCK))
    x1p, h2p, rp, cnt_p, kp, vp, sp, cp = mixer(x_p, tm_p, tp, 0, zero_hist, zero_s, None, None,
                                                 min(DELTA_BLOCK, tp), nc_p, min(KEY_TILE, tp),
                                                 jnp.zeros((1, LANES), F32))
    x1s, h2s, rs, cnt, ks, vs, ss, cs = mixer(x_s, ts, ts, bp, hist_s, s0_s, k_past, v_past,
                                               min(DELTA_BLOCK, ts), max(1, ts // DELTA_BLOCK),
                                               min(KEY_TILE, ts), cnt_p)

    n_blocks = -(-2 * n_tok // MOE_BLOCK) + N_EXPERTS
    blk_e, nvalid, seg, pad_start = _segment_plan(cnt, n_blocks)
    dest_p = _token_rows(rp, pad_start)
    dest_s = _token_rows(rs, pad_start)
    x_sorted = _dispatch(seg, dest_p, dest_s, h2p, h2s, n_blocks, tm_p)
    y_sorted = _moe(blk_e, nvalid, x_sorted, p['w_gate'], p['w_up'], p['w_down'])
    y_p = _combine(dest_p, rp, y_sorted, x1p, mod8, g_post_ffn, tm_p, tp, 0).reshape(bp, tp, d)
    y_s = _combine(dest_s, rs, y_sorted, x1s, mod8, g_post_ffn, ts, ts, bp).reshape(bs, ts, d)
    return y_p, y_s, kp, vp, sp, cp, ks, vs, ss, cs


def kernel(x_prompt, x_sample, c_prompt, c_sample, cache_k, cache_v, state_delta, state_conv, w_ada, b_ada, g_pre_mix, g_post_mix, g_pre_ffn, g_post_ffn, w_in, conv_w, a_log, dt_bias, onorm_a, onorm_b, w_out, w_router_group, b_router_group, w_router_expert, b_router_expert, w_gate, w_up, w_down):
    depth = w_in.shape[0]
    y_p, y_s = x_prompt, x_sample
    outs = [[] for _ in range(8)]
    for l in range(depth):
        p = dict(w_ada=w_ada[l], b_ada=b_ada[l], g_pre_mix=g_pre_mix[l], g_post_mix=g_post_mix[l],
                 g_pre_ffn=g_pre_ffn[l], g_post_ffn=g_post_ffn[l], w_in=w_in[l], conv_w=conv_w[l],
                 a_log=a_log[l], dt_bias=dt_bias[l], onorm_a=onorm_a[l], onorm_b=onorm_b[l],
                 w_out=w_out[l], w_router_group=w_router_group[l], b_router_group=b_router_group[l],
                 w_router_expert=w_router_expert[l], b_router_expert=b_router_expert[l],
                 w_gate=w_gate[l], w_up=w_up[l], w_down=w_down[l])
        res = _layer(y_p, y_s, c_prompt, c_sample, cache_k[l], cache_v[l], state_delta[l], state_conv[l], p)
        y_p, y_s = res[0], res[1]
        for lst, r in zip(outs, res[2:]):
            lst.append(r)
    return (y_p, y_s) + tuple(jnp.stack(o) for o in outs)
```

```python
import functools
import math

import jax
import jax.numpy as jnp
from jax import lax
from jax.experimental import pallas as pl
from jax.experimental.pallas import tpu as pltpu

F32 = jnp.float32
BF16 = jnp.bfloat16

D_MODEL = 1024
DN_HEADS = 4
DN_DIM = 128
CONV_W = 4
CONV_CH = DN_HEADS * 3 * DN_DIM
DELTA_BLOCK = 64
SB_HEADS = 8
SB_DIM = 64
SB_WIDTH = SB_HEADS * SB_DIM
N_GROUPS = 4
EXPERTS_PER_GROUP = 8
N_EXPERTS = N_GROUPS * EXPERTS_PER_GROUP
D_EXPERT = D_MODEL // 2
MOE_BLOCK = 256
EPS = 1e-6

LANES = 128
KEY_TILE = 128
EXP_ZERO_BELOW = -104.0
VMEM_LIMIT = 56 * 1024 * 1024


def _cparams(sem):
    return pltpu.CompilerParams(dimension_semantics=sem, vmem_limit_bytes=VMEM_LIMIT)


def _split(a):
    hi = a.astype(BF16)
    lo = (a - hi.astype(F32)).astype(BF16)
    return hi, lo


def _dot(a, b, dims=(((1,), (0,)), ((), ()))):
    return lax.dot_general(a, b, dims, preferred_element_type=F32)


def _dot3(a, b, dims=(((1,), (0,)), ((), ()))):
    ah, al = _split(a)
    bh, bl = _split(b)
    return _dot(ah, bh, dims) + (_dot(al, bh, dims) + _dot(ah, bl, dims))


_NT = (((1,), (1,)), ((), ()))
_TN = (((0,), (0,)), ((), ()))


def _silu(x):
    return x * jax.nn.sigmoid(x)


def _softplus(x):
    return jnp.maximum(x, 0.0) + jnp.log(1.0 + jnp.exp(-jnp.abs(x)))


def _ada_kernel(c_ref, w_ref, b_ref, o_ref):
    s = _silu(c_ref[...]).astype(BF16)
    o_ref[...] = _dot(s, w_ref[...].astype(BF16)) + b_ref[...]


def _ada(c_all, w_ada, b_ada):
    rows = c_all.shape[0]
    n = w_ada.shape[1]
    tn = 1024
    return pl.pallas_call(
        _ada_kernel,
        grid=(n // tn,),
        in_specs=[pl.BlockSpec((rows, D_MODEL), lambda j: (0, 0)),
                  pl.BlockSpec((D_MODEL, tn), lambda j: (0, j)),
                  pl.BlockSpec((1, tn), lambda j: (0, j))],
        out_specs=pl.BlockSpec((rows, tn), lambda j: (0, j)),
        out_shape=jax.ShapeDtypeStruct((rows, n), F32),
        compiler_params=_cparams(("arbitrary",)),
        name="ada",
    )(c_all, w_ada, b_ada.reshape(1, n))


def _rms(x, gain):
    return x * lax.rsqrt(jnp.mean(x * x, axis=-1, keepdims=True) + EPS) * gain


def _proj_kernel(x_ref, mod_ref, g_ref, wm_ref, wbh_ref, wbl_ref,
                 a_ref, z_ref, bg_ref, q_ref, k_ref, v_ref, k16_ref, v16_ref):
    mod = mod_ref[0]
    h = _rms(x_ref[...], g_ref[...]) * (1.0 + mod[1:2]) + mod[0:1]
    hh, hl = _split(h)
    p = _dot(hh, wm_ref[...])
    a_ref[...] = p[:, 0:CONV_CH]
    z_ref[...] = p[:, CONV_CH:CONV_CH + 512]
    o = CONV_CH + 512
    q_ref[...] = (p[:, o:o + 512] * (SB_DIM ** -0.5)).astype(BF16)
    k = p[:, o + 512:o + 1024]
    v = p[:, o + 1024:o + 1536]
    k_ref[...] = k
    v_ref[...] = v
    k16_ref[...] = k.astype(BF16)
    v16_ref[...] = v.astype(BF16)
    wbh = wbh_ref[...]
    bg_ref[...] = _dot(hh, wbh) + (_dot(hl, wbh) + _dot(hh, wbl_ref[...]))


def _proj(x2d, mod8, g_pre, w_main, wb_hi, wb_lo, tm, seq_rows, mod_row0):
    n = x2d.shape[0]
    nm = w_main.shape[1]
    row = lambda i: (i, 0)
    const = lambda i: (0, 0)
    modmap = lambda i: (mod_row0 + (i * tm) // seq_rows, 0, 0)
    outs = [(CONV_CH, F32), (512, F32), (LANES, F32), (512, BF16), (512, F32), (512, F32),
            (512, BF16), (512, BF16)]
    return pl.pallas_call(
        _proj_kernel,
        grid=(n // tm,),
        in_specs=[pl.BlockSpec((tm, D_MODEL), row),
                  pl.BlockSpec((1, 8, D_MODEL), modmap),
                  pl.BlockSpec((1, D_MODEL), const),
                  pl.BlockSpec((D_MODEL, nm), const),
                  pl.BlockSpec((D_MODEL, LANES), const),
                  pl.BlockSpec((D_MODEL, LANES), const)],
        out_specs=[pl.BlockSpec((tm, w), row) for w, _ in outs],
        out_shape=[jax.ShapeDtypeStruct((n, w), dt) for w, dt in outs],
        compiler_params=_cparams(("arbitrary",)),
        name="proj",
    )(x2d, mod8, g_pre, w_main, wb_hi, wb_lo)


def _delta_kernel(a_ref, z_ref, bg_ref, hist0_ref, s0_ref, cw_ref, alog_ref, dtb_ref, on_ref,
                  o_ref, sfin_ref, hist_sc, s_sc, *, chunk, n_chunks):
    t_idx = pl.program_id(1)
    tt = chunk * n_chunks

    @pl.when(t_idx == 0)
    def _():
        hist_sc[...] = hist0_ref[0]
        s_sc[...] = s0_ref[0]

    x = a_ref[0]
    xx = jnp.concatenate([hist_sc[...], x], axis=0)
    cw = cw_ref[...]
    conv = x * cw[CONV_W - 1:CONV_W]
    for s in range(1, CONV_W):
        conv = conv + pltpu.roll(xx, s, 0)[8:] * cw[CONV_W - 1 - s:CONV_W - s]
    conv = _silu(conv)
    hist_sc[...] = x[tt - 8:tt]

    bg = bg_ref[0]
    lane = lax.broadcasted_iota(jnp.int32, (1, LANES), 1)
    g_lane = (lane >= DN_HEADS) & (lane < 2 * DN_HEADS)
    neg_a = jnp.where(g_lane, -jnp.exp(alog_ref[...]), 0.0)
    beta_all = jax.nn.sigmoid(bg)
    g_all = neg_a * _softplus(bg + dtb_ref[...])

    ri = lax.broadcasted_iota(jnp.int32, (chunk, chunk), 0)
    ci = lax.broadcasted_iota(jnp.int32, (chunk, chunk), 1)
    incl = ri >= ci
    strict = ri > ci
    tri = incl.astype(F32)
    eye = (ri == ci).astype(F32)
    ones_cc = jnp.ones((chunk, chunk), F32)
    onorm = on_ref[...]

    chunks = range(n_chunks)
    units = [(c, h) for c in chunks for h in range(DN_HEADS)]
    rows = lambda c: slice(c * chunk, (c + 1) * chunk)

    gc_all = [_dot3(tri, g_all[rows(c)]) for c in chunks]
    diag = [jnp.concatenate([eye * gc_all[c][:, DN_HEADS + h:DN_HEADS + h + 1] for h in range(DN_HEADS)], axis=1)
            for c in chunks]
    gc_rows = [_dot3(ones_cc, diag[c]) for c in chunks]

    q, k, v, beta, gcol, glast, eg, decay, kbeta = {}, {}, {}, {}, {}, {}, {}, {}, {}
    for (c, h) in units:
        lo = h * DN_DIM
        qq = conv[rows(c), lo:lo + DN_DIM]
        kk = conv[rows(c), 512 + lo:512 + lo + DN_DIM]
        u_ = (c, h)
        q[u_] = qq * lax.rsqrt(jnp.sum(qq * qq, axis=-1, keepdims=True) + EPS) * (DN_DIM ** -0.5)
        k[u_] = kk * lax.rsqrt(jnp.sum(kk * kk, axis=-1, keepdims=True) + EPS)
        v[u_] = conv[rows(c), 1024 + lo:1024 + lo + DN_DIM]
        beta[u_] = beta_all[rows(c), h:h + 1]
        gcol[u_] = gc_all[c][:, DN_HEADS + h:DN_HEADS + h + 1]
        glast[u_] = gcol[u_][chunk - 1:chunk, :]
        eg[u_] = jnp.exp(gcol[u_])
        grow = gc_rows[c][:, h * chunk:(h + 1) * chunk]
        decay[u_] = jnp.where(incl, jnp.exp(jnp.minimum(gcol[u_] - grow, 0.0)), 0.0)
        kbeta[u_] = k[u_] * beta[u_]

    k16 = {u_: k[u_].astype(BF16) for u_ in units}
    kq = {u_: _dot(jnp.concatenate([kbeta[u_], q[u_]], axis=0).astype(BF16), k16[u_], _NT) for u_ in units}
    lower = {u_: jnp.where(strict, kq[u_][:chunk] * decay[u_], 0.0) for u_ in units}
    attn16 = {u_: jnp.where(incl, kq[u_][chunk:] * decay[u_], 0.0).astype(BF16) for u_ in units}

    sol = {u_: jnp.concatenate([v[u_] * beta[u_], kbeta[u_] * eg[u_]], axis=1) for u_ in units}
    lp = lower
    p = 1
    while p < chunk:
        lsp = {u_: _split(lp[u_]) for u_ in units}
        ssp = {u_: _split(sol[u_]) for u_ in units}
        stack = {u_: jnp.concatenate(lsp[u_], axis=0) for u_ in units}
        p1 = {u_: _dot(stack[u_], ssp[u_][0]) for u_ in units}
        p2 = {u_: _dot(lsp[u_][0], ssp[u_][1]) for u_ in units}
        upd = {u_: p1[u_][:chunk] + (p1[u_][chunk:] + p2[u_]) for u_ in units}
        sol = {u_: (sol[u_] - upd[u_]) if p == 1 else (sol[u_] + upd[u_]) for u_ in units}
        if 2 * p < chunk:
            q1 = {u_: _dot(stack[u_], lsp[u_][0]) for u_ in units}
            q2 = {u_: _dot(lsp[u_][0], lsp[u_][1]) for u_ in units}
            lp = {u_: q1[u_][:chunk] + (q1[u_][chunk:] + q2[u_]) for u_ in units}
        p *= 2

    usol = {u_: sol[u_][:, :DN_DIM] for u_ in units}
    wq16 = {u_: jnp.concatenate([sol[u_][:, DN_DIM:], q[u_] * eg[u_]], axis=0).astype(BF16) for u_ in units}
    kd16 = {u_: (k[u_] * jnp.exp(glast[u_] - gcol[u_])).astype(BF16) for u_ in units}

    for c in chunks:
        hs = range(DN_HEADS)
        s_old = [s_sc[h] for h in hs]
        ws = [_dot(wq16[(c, h)], s_old[h].astype(BF16)) for h in hs]
        v_new = [(usol[(c, h)] - ws[h][:chunk]).astype(BF16) for h in hs]
        o_in = [_dot(attn16[(c, h)], v_new[h]) for h in hs]
        ds = [_dot(kd16[(c, h)], v_new[h], _TN) for h in hs]
        for h in hs:
            lo = h * DN_DIM
            s_sc[h] = s_old[h] * jnp.exp(glast[(c, h)]) + ds[h]
            o = ws[h][chunk:] + o_in[h]
            zg = z_ref[0, rows(c), lo:lo + DN_DIM]
            o_ref[0, rows(c), lo:lo + DN_DIM] = (_rms(o, onorm) * _silu(zg)).astype(BF16)

    @pl.when(t_idx == pl.num_programs(1) - 1)
    def _():
        sfin_ref[0] = s_sc[...]


def _delta(a_in, z, bg, hist8, s0, conv_w8, alog_row, dtb_row, onorm_a, chunk, n_chunks):
    b, t, _ = a_in.shape
    tt = chunk * n_chunks
    tile = lambda bi, ti: (bi, ti, 0)
    per_b3 = lambda bi, ti: (bi, 0, 0)
    per_b4 = lambda bi, ti: (bi, 0, 0, 0)
    const = lambda bi, ti: (0, 0)
    kern = functools.partial(_delta_kernel, chunk=chunk, n_chunks=n_chunks)
    return pl.pallas_call(
        kern,
        grid=(b, t // tt),
        in_specs=[pl.BlockSpec((1, tt, CONV_CH), tile),
                  pl.BlockSpec((1, tt, 512), tile),
                  pl.BlockSpec((1, tt, LANES), tile),
                  pl.BlockSpec((1, 8, CONV_CH), per_b3),
                  pl.BlockSpec((1, DN_HEADS, DN_DIM, DN_DIM), per_b4),
                  pl.BlockSpec((8, CONV_CH), const),
                  pl.BlockSpec((1, LANES), const),
                  pl.BlockSpec((1, LANES), const),
                  pl.BlockSpec((1, DN_DIM), const)],
        out_specs=[pl.BlockSpec((1, tt, 512), tile),
                   pl.BlockSpec((1, DN_HEADS, DN_DIM, DN_DIM), per_b4)],
        out_shape=[jax.ShapeDtypeStruct((b, t, 512), BF16),
                   jax.ShapeDtypeStruct((b, DN_HEADS, DN_DIM, DN_DIM), F32)],
        scratch_shapes=[pltpu.VMEM((8, CONV_CH), F32),
                        pltpu.VMEM((DN_HEADS, DN_DIM, DN_DIM), F32)],
        compiler_params=_cparams(("arbitrary", "arbitrary")),
        name="delta",
    )(a_in, z, bg, hist8, s0, conv_w8, alog_row, dtb_row, onorm_a)


def _sb_kernel(q_ref, k0_ref, k1_ref, k2_ref, v0_ref, v1_ref, v2_ref, k_hbm, v_hbm, on_ref,
               o_ref, kbuf, vbuf, qsel, acc, carry, sem, *, bq, n_pad, q_off):
    b = pl.program_id(0)
    i = pl.program_id(1)
    qend = q_off + (i + 1) * bq
    last_tile = qend // KEY_TILE - 1
    n_tiles = last_tile + 1

    acc[...] = jnp.zeros_like(acc)
    carry[...] = jnp.zeros_like(carry)
    half_lane = lax.broadcasted_iota(jnp.int32, (bq, LANES), 1) < SB_DIM
    for p in range(SB_HEADS // 2):
        qf = q_ref[0, :, p * LANES:(p + 1) * LANES].astype(F32)
        qsel[2 * p] = jnp.where(half_lane, qf, 0.0).astype(BF16)
        qsel[2 * p + 1] = jnp.where(half_lane, 0.0, qf).astype(BF16)

    qpos =q_off + i * bq + lax.broadcasted_iota(jnp.int32, (bq, KEY_TILE), 0)
    col = lax.broadcasted_iota(jnp.int32, (bq, KEY_TILE), 1)
    lane = lax.broadcasted_iota(jnp.int32, (1, LANES), 1)
    low_half = lane < SB_DIM
    rj = lax.broadcasted_iota(jnp.int32, (KEY_TILE, 2 * KEY_TILE), 0)
    cj = lax.broadcasted_iota(jnp.int32, (KEY_TILE, 2 * KEY_TILE), 1)
    suffix = ((rj > cj) | (cj >= KEY_TILE)).astype(BF16)

    def load_tile(t):
        for kk, (kr, vr) in enumerate(((k0_ref, v0_ref), (k1_ref, v1_ref), (k2_ref, v2_ref))):
            @pl.when(t == kk)
            def _():
                for p in range(SB_HEADS // 2):
                    kbuf[p] = kr[0, :, p * LANES:(p + 1) * LANES]
                    vbuf[p] = vr[0, :, p * LANES:(p + 1) * LANES]

        @pl.when(t >= 3)
        def _():
            start = pl.multiple_of((last_tile - t) * KEY_TILE, KEY_TILE)
            copies = []
            for p in range(SB_HEADS // 2):
                for src, dst in ((k_hbm, kbuf), (v_hbm, vbuf)):
                    cp = pltpu.make_async_copy(
                        src.at[b, pl.ds(start, KEY_TILE), pl.ds(p * LANES, LANES)], dst.at[p], sem)
                    cp.start()
                    copies.append(cp)
            for cp in copies:
                cp.wait()

    def body(state):
        t, _ = state
        load_tile(t)
        kpos = (last_tile - t) * KEY_TILE + col
        mask = (kpos < qpos) & (kpos >= n_pad)
        heads = range(SB_HEADS)
        z = [_dot(qsel[h], kbuf[h // 2], _NT) for h in heads]
        sp = [_softplus(z[h]) for h in heads]
        pieces = []
        for h in heads:
            pieces.extend(_split(jnp.where(mask, -sp[h], 0.0)))
        cs_all = _dot(jnp.concatenate(pieces, axis=0), suffix)
        cs = [cs_all[2 * h * bq:(2 * h + 1) * bq] + cs_all[(2 * h + 1) * bq:(2 * h + 2) * bq] for h in heads]
        c_old = [carry[h] for h in heads]
        a = [jnp.where(mask, jnp.exp((z[h] - sp[h]) + cs[h][:, :KEY_TILE] + c_old[h]), 0.0).astype(BF16)
             for h in heads]
        pv = [_dot(a[h], vbuf[h // 2]) for h in heads]
        for p in range(SB_HEADS // 2):
            acc[p] = acc[p] + jnp.where(low_half, pv[2 * p], pv[2 * p + 1])
        worst = jnp.full((bq, KEY_TILE), -jnp.inf, F32)
        for h in heads:
            c_new = c_old[h] + cs[h][:, KEY_TILE:]
            carry[h] = c_new
            worst = jnp.maximum(worst, c_new)
        done = (jnp.max(worst) < EXP_ZERO_BELOW).astype(jnp.int32)
        return t + 1, done

    def cond(state):
        t, done = state
        return (t < n_tiles) & (done == 0)

    lax.while_loop(cond, body, (jnp.int32(0), jnp.int32(0)))

    onb = on_ref[...]
    for p in range(SB_HEADS // 2):
        o = acc[p]
        sq = o * o
        s_lo = jnp.sum(jnp.where(low_half, sq, 0.0), axis=-1, keepdims=True)
        s_hi = jnp.sum(jnp.where(low_half, 0.0, sq), axis=-1, keepdims=True)
        ms = jnp.where(low_half, s_lo, s_hi) * (1.0 / SB_DIM)
        o_ref[0, :, p * LANES:(p + 1) * LANES] = (o * lax.rsqrt(ms + EPS) * onb).astype(BF16)


def _sb_attn(q16, k16p, v16p, onorm_b2, bq, n_pad):
    b, tq, _ = q16.shape
    tkp = k16p.shape[1]
    q_off = tkp - tq
    assert tkp % KEY_TILE == 0 and KEY_TILE % bq == 0 and tq % bq == 0 and q_off % bq == 0

    def kmap(back):
        def f(bi, i):
            last = (q_off + (i + 1) * bq) // KEY_TILE - 1
            return (bi, jnp.maximum(last - back, 0), 0)
        return f

    qmap = lambda bi, i: (bi, i, 0)
    kern = functools.partial(_sb_kernel, bq=bq, n_pad=n_pad, q_off=q_off)
    kspec = [pl.BlockSpec((1, KEY_TILE, SB_WIDTH), kmap(back)) for back in range(3)]
    return pl.pallas_call(
        kern,
        grid=(b, tq // bq),
        in_specs=[pl.BlockSpec((1, bq, SB_WIDTH), qmap)] + kspec + kspec
                 + [pl.BlockSpec(memory_space=pl.ANY), pl.BlockSpec(memory_space=pl.ANY),
                    pl.BlockSpec((1, LANES), lambda bi, i: (0, 0))],
        out_specs=pl.BlockSpec((1, bq, SB_WIDTH), qmap),
        out_shape=jax.ShapeDtypeStruct((b, tq, SB_WIDTH), BF16),
        scratch_shapes=[pltpu.VMEM((SB_HEADS // 2, KEY_TILE, LANES), BF16),
                        pltpu.VMEM((SB_HEADS // 2, KEY_TILE, LANES), BF16),
                        pltpu.VMEM((SB_HEADS, bq, LANES), BF16),
                        pltpu.VMEM((SB_HEADS // 2, bq, LANES), F32),
                        pltpu.VMEM((SB_HEADS, bq, KEY_TILE), F32),
                        pltpu.SemaphoreType.DMA(())],
        compiler_params=_cparams(("arbitrary", "arbitrary")),
        name="sb_attn",
    )(q16, k16p, k16p, k16p, v16p, v16p, v16p, k16p, v16p, onorm_b2)


def _post_kernel(oa_ref, ob_ref, x_ref, mod_ref, gpm_ref, gpf_ref, wo_ref, wrh_ref, wrl_ref, br_ref, cnt0_ref,
                 x1_ref, h2_ref, route_ref, cnt_ref):
    @pl.when(pl.program_id(0) == 0)
    def _():
        cnt_ref[...] = cnt0_ref[...]

    mod = mod_ref[0]
    mix = _dot(oa_ref[...], wo_ref[0:512, :]) + _dot(ob_ref[...], wo_ref[512:1024, :])
    x1 = x_ref[...] + mod[2:3] * _rms(mix, gpm_ref[...])
    x1_ref[...] = x1
    h2 = _rms(x1, gpf_ref[...]) * (1.0 + mod[4:5]) + mod[3:4]
    h2_ref[...] = h2
    hh, hl = _split(h2)
    wrh = wrh_ref[...]
    logits = _dot(hh, wrh) + (_dot(hl, wrh) + _dot(hh, wrl_ref[...])) + br_ref[...]
    lane = lax.broadcasted_iota(jnp.int32, logits.shape, 1).astype(F32)
    neg = -jnp.inf
    nl = float(LANES)
    lg = jnp.where(lane < N_GROUPS, logits, neg)
    gmax = jnp.max(lg, axis=-1, keepdims=True)
    grp = jnp.min(jnp.where(lg == gmax, lane, nl), axis=-1, keepdims=True)
    p_grp = 1.0 / jnp.sum(jnp.exp(lg - gmax), axis=-1, keepdims=True)
    first = N_GROUPS + grp * EXPERTS_PER_GROUP
    le = jnp.where((lane >= first) & (lane < first + EXPERTS_PER_GROUP), logits, neg)
    emax = jnp.max(le, axis=-1, keepdims=True)
    i1 = jnp.min(jnp.where(le == emax, lane, nl), axis=-1, keepdims=True)
    esum = jnp.sum(jnp.exp(le - emax), axis=-1, keepdims=True)
    le2 = jnp.where(lane == i1, neg, le)
    e2max = jnp.max(le2, axis=-1, keepdims=True)
    i2 = jnp.min(jnp.where(le2 == e2max, lane, nl), axis=-1, keepdims=True)
    p1 = 1.0 / esum
    p2 = jnp.exp(e2max - emax) / esum
    w1 = p_grp * p1 / (p1 + p2)
    w2 = p_grp * p2 / (p1 + p2)
    e1 = i1 - N_GROUPS
    e2 = i2 - N_GROUPS
    hot1 = (lane == e1).astype(F32)
    hot2 = (lane == e2).astype(F32)
    both = hot1 + hot2
    tm = logits.shape[0]
    ti = lax.broadcasted_iota(jnp.int32, (tm, tm), 0)
    tj = lax.broadcasted_iota(jnp.int32, (tm, tm), 1)
    earlier = _dot((ti > tj).astype(BF16), both.astype(BF16)) + cnt_ref[...]
    rank1 = jnp.sum(hot1 * earlier, axis=-1, keepdims=True)
    rank2 = jnp.sum(hot2 * (earlier + hot1), axis=-1, keepdims=True)
    cnt_ref[...] = cnt_ref[...] + jnp.sum(both, axis=0, keepdims=True)
    out = jnp.where(lane == 0.0, e1, 0.0)
    out = jnp.where(lane == 1.0, e2, out)
    out = jnp.where(lane == 2.0, w1, out)
    out = jnp.where(lane == 3.0, w2, out)
    out = jnp.where(lane == 4.0, rank1, out)
    out = jnp.where(lane == 5.0, rank2, out)
    route_ref[...] = out


def _post(oa16, ob16, x2d, mod8, g_post_mix, g_pre_ffn, w_out16, wr_hi, wr_lo, b_r, cnt0, tm, seq_rows,
          mod_row0):
    n = x2d.shape[0]
    row = lambda i: (i, 0)
    const = lambda i: (0, 0)
    modmap = lambda i: (mod_row0 + (i * tm) // seq_rows, 0, 0)
    return pl.pallas_call(
        _post_kernel,
        grid=(n // tm,),
        in_specs=[pl.BlockSpec((tm, 512), row),
                  pl.BlockSpec((tm, 512), row),
                  pl.BlockSpec((tm, D_MODEL), row),
                  pl.BlockSpec((1, 8, D_MODEL), modmap),
                  pl.BlockSpec((1, D_MODEL), const),
                  pl.BlockSpec((1, D_MODEL), const),
                  pl.BlockSpec((D_MODEL, D_MODEL), const),
                  pl.BlockSpec((D_MODEL, LANES), const),
                  pl.BlockSpec((D_MODEL, LANES), const),
                  pl.BlockSpec((1, LANES), const),
                  pl.BlockSpec((1, LANES), const)],
        out_specs=[pl.BlockSpec((tm, D_MODEL), row),
                   pl.BlockSpec((tm, D_MODEL), row),
                   pl.BlockSpec((tm, LANES), row),
                   pl.BlockSpec((1, LANES), const)],
        out_shape=[jax.ShapeDtypeStruct((n, D_MODEL), F32),
                   jax.ShapeDtypeStruct((n, D_MODEL), F32),
                   jax.ShapeDtypeStruct((n, LANES), F32),
                   jax.ShapeDtypeStruct((1, LANES), F32)],
        compiler_params=_cparams(("arbitrary",)),
        name="post",
    )(oa16, ob16, x2d, mod8, g_post_mix, g_pre_ffn, w_out16, wr_hi, wr_lo, b_r, cnt0)


def _dispatch_kernel(seg_ref, dp_ref, ds_ref, hp_ref, hs_ref, xs_hbm, zbuf, sem, *, n_blocks):
    i = pl.program_id(0)
    last_step = pl.num_programs(0) - 1

    @pl.when(i == 0)
    def _():
        zbuf[...] = jnp.zeros_like(zbuf)

        def zero_block(row0):
            return pltpu.make_async_copy(zbuf, xs_hbm.at[pl.ds(pl.multiple_of(row0, MOE_BLOCK), MOE_BLOCK), :], sem)

        for e in range(N_EXPERTS):
            @pl.when(seg_ref[e] > 0)
            def _():
                zero_block(seg_ref[N_EXPERTS + e] - MOE_BLOCK).start()
        for e in range(N_EXPERTS):
            @pl.when(seg_ref[e] > 0)
            def _():
                zero_block(seg_ref[N_EXPERTS + e] - MOE_BLOCK).wait()

        used = seg_ref[2 * N_EXPERTS - 1] // MOE_BLOCK

        def fill(b, c):
            cp = zero_block(b * MOE_BLOCK)
            cp.start()
            cp.wait()
            return c

        lax.fori_loop(used, n_blocks, fill, 0)

    def scatter(h_ref, dest_ref):
        rows = h_ref.shape[0]
        for t in range(rows):
            for slot in range(2):
                pltpu.make_async_copy(h_ref.at[pl.ds(t, 1), :],
                                      xs_hbm.at[pl.ds(dest_ref[0, 0, 2 * t + slot], 1), :],
                                      sem).start(priority=slot)
        for slot in range(2):
            pltpu.make_async_copy(h_ref, xs_hbm.at[pl.ds(0, rows), :], sem).wait()

    @pl.when(i < last_step)
    def _():
        scatter(hp_ref, dp_ref)

    @pl.when(i == last_step)
    def _():
        scatter(hs_ref, ds_ref)


def _dispatch(seg, dest_p, dest_s, h2p, h2s, n_blocks, tm):
    n_p, n_s = h2p.shape[0], h2s.shape[0]
    steps_p = n_p // tm
    pmap3 = lambda i, sg: (jnp.minimum(i, steps_p - 1), 0, 0)
    pmap2 = lambda i, sg: (jnp.minimum(i, steps_p - 1), 0)
    grid_spec = pltpu.PrefetchScalarGridSpec(
        num_scalar_prefetch=1,
        grid=(steps_p + 1,),
        in_specs=[pl.BlockSpec((1, 1, 2 * tm), pmap3, memory_space=pltpu.SMEM),
                  pl.BlockSpec((1, 1, 2 * n_s), lambda i, sg: (0, 0, 0), memory_space=pltpu.SMEM),
                  pl.BlockSpec((tm, D_MODEL), pmap2),
                  pl.BlockSpec((n_s, D_MODEL), lambda i, sg: (0, 0))],
        out_specs=pl.BlockSpec(memory_space=pl.ANY),
        scratch_shapes=[pltpu.VMEM((MOE_BLOCK, D_MODEL), F32), pltpu.SemaphoreType.DMA(())])
    return pl.pallas_call(
        functools.partial(_dispatch_kernel, n_blocks=n_blocks),
        grid_spec=grid_spec,
        out_shape=jax.ShapeDtypeStruct((n_blocks * MOE_BLOCK, D_MODEL), F32),
        compiler_params=_cparams(("arbitrary",)),
        name="dispatch",
    )(seg, dest_p.reshape(steps_p, 1, 2 * tm), dest_s.reshape(1, 1, 2 * n_s), h2p, h2s)


def _moe_kernel(blk_e_ref, nvalid_ref, x_ref, wg_ref, wu_ref, wd_ref, y_ref, wg16, wu16, wd16):
    i = pl.program_id(0)
    e = blk_e_ref[i]
    e_prev = blk_e_ref[jnp.maximum(i - 1, 0)]

    @pl.when((i == 0) | (e != e_prev))
    def _():
        wg16[...] = wg_ref[0].astype(BF16)
        wu16[...] = wu_ref[0].astype(BF16)
        wd16[...] = wd_ref[0].astype(BF16)

    @pl.when(nvalid_ref[i] > 0)
    def _():
        xb = x_ref[...].astype(BF16)
        g = _dot(xb, wg16[...])
        u = _dot(xb, wu16[...])
        hmid = (_silu(g) * u).astype(BF16)
        y_ref[...] = _dot(hmid, wd16[...])

    @pl.when(nvalid_ref[i] == 0)
    def _():
        y_ref[...] = jnp.zeros_like(y_ref)


def _moe(blk_e, nvalid, x_sorted, w_gate, w_up, w_down):
    n_blocks = blk_e.shape[0]
    wmap = lambda i, be, nv: (be[i], 0, 0)
    xmap = lambda i, be, nv: (jnp.where(nv[i] > 0, i, 0), 0)
    grid_spec = pltpu.PrefetchScalarGridSpec(
        num_scalar_prefetch=2,
        grid=(n_blocks,),
        in_specs=[pl.BlockSpec((MOE_BLOCK, D_MODEL), xmap),
                  pl.BlockSpec((1, D_MODEL, D_EXPERT), wmap),
                  pl.BlockSpec((1, D_MODEL, D_EXPERT), wmap),
                  pl.BlockSpec((1, D_EXPERT, D_MODEL), wmap)],
        out_specs=pl.BlockSpec((MOE_BLOCK, D_MODEL), lambda i, be, nv: (i, 0)),
        scratch_shapes=[pltpu.VMEM((D_MODEL, D_EXPERT), BF16),
                        pltpu.VMEM((D_MODEL, D_EXPERT), BF16),
                        pltpu.VMEM((D_EXPERT, D_MODEL), BF16)])
    return pl.pallas_call(
        _moe_kernel,
        grid_spec=grid_spec,
        out_shape=jax.ShapeDtypeStruct((n_blocks * MOE_BLOCK, D_MODEL), F32),
        compiler_params=_cparams(("arbitrary",)),
        name="moe",
    )(blk_e, nvalid, x_sorted, w_gate, w_up, w_down)


def _combine_kernel(dest_ref, route_ref, x1_ref, mod_ref, g_ref, y_hbm, o_ref, ybuf, sem):
    tm = x1_ref.shape[0]
    for t in range(tm):
        for slot in range(2):
            pltpu.make_async_copy(y_hbm.at[pl.ds(dest_ref[0, 0, 2 * t + slot], 1), :],
                                  ybuf.at[slot, pl.ds(t, 1), :], sem).start(priority=slot)
    for slot in range(2):
        pltpu.make_async_copy(y_hbm.at[pl.ds(0, tm), :], ybuf.at[slot], sem).wait()
    mod = mod_ref[0]
    route = route_ref[...]
    moe = ybuf[0] * route[:, 2:3] + ybuf[1] * route[:, 3:4]
    o_ref[...] = x1_ref[...] + mod[5:6] * _rms(moe, g_ref[...])


def _combine(dest, route, y_sorted, x1, mod8, g_post_ffn, tm, seq_rows, mod_row0):
    n = x1.shape[0]
    row = lambda i: (i, 0)
    modmap = lambda i: (mod_row0 + (i * tm) // seq_rows, 0, 0)
    return pl.pallas_call(
        _combine_kernel,
        grid=(n // tm,),
        in_specs=[pl.BlockSpec((1, 1, 2 * tm), lambda i: (i, 0, 0), memory_space=pltpu.SMEM),
                  pl.BlockSpec((tm, LANES), row),
                  pl.BlockSpec((tm, D_MODEL), row),
                  pl.BlockSpec((1, 8, D_MODEL), modmap),
                  pl.BlockSpec((1, D_MODEL), lambda i: (0, 0)),
                  pl.BlockSpec(memory_space=pl.ANY)],
        out_specs=pl.BlockSpec((tm, D_MODEL), row),
        out_shape=jax.ShapeDtypeStruct((n, D_MODEL), F32),
        scratch_shapes=[pltpu.VMEM((2, tm, D_MODEL), F32), pltpu.SemaphoreType.DMA(())],
        compiler_params=_cparams(("arbitrary",)),
        name="combine",
    )(dest.reshape(n // tm, 1, 2 * tm), route, x1, mod8, g_post_ffn, y_sorted)


def _segment_plan(counts_f, n_blocks):
    counts = counts_f[0, :N_EXPERTS].astype(jnp.int32)
    padded = (counts + MOE_BLOCK - 1) // MOE_BLOCK * MOE_BLOCK
    pad_end = jnp.cumsum(padded)
    pad_start = pad_end - padded
    blk_start = jnp.arange(n_blocks, dtype=jnp.int32) * MOE_BLOCK
    blk_e = jnp.minimum(jnp.sum((pad_end[None, :] <= blk_start[:, None]).astype(jnp.int32), axis=1),
                        N_EXPERTS - 1)
    onehot = blk_e[:, None] == jnp.arange(N_EXPERTS, dtype=jnp.int32)[None, :]
    c_blk = jnp.sum(jnp.where(onehot, counts[None, :], 0), axis=1)
    s_blk = jnp.sum(jnp.where(onehot, pad_start[None, :], 0), axis=1)
    nvalid = jnp.clip(c_blk - (blk_start - s_blk), 0, MOE_BLOCK).astype(jnp.int32)
    seg = jnp.concatenate([counts, pad_end]).astype(jnp.int32)
    return blk_e.astype(jnp.int32), nvalid, seg, pad_start


def _token_rows(route, pad_start):
    eid = route[:, 0:2].astype(jnp.int32)
    rank = route[:, 4:6].astype(jnp.int32)
    onehot = eid[:, :, None] == jnp.arange(N_EXPERTS, dtype=jnp.int32)[None, None, :]
    return rank + jnp.sum(jnp.where(onehot, pad_start[None, None, :], 0), axis=2)


def _layer(x_p, x_s, c_p, c_s, k_past, v_past, s0_s, conv_s, p):
    bp, tp, d = x_p.shape
    bs, ts, _ = x_s.shape
    n_p, n_s = bp * tp, bs * ts
    n_tok = n_p + n_s

    n_seq = bp + bs
    c_all = jnp.zeros((16, d), F32).at[:n_seq].set(jnp.concatenate([c_p, c_s], axis=0))
    mod = _ada(c_all, p['w_ada'], p['b_ada'])
    mod8 = jnp.pad(mod.reshape(16, 6, d), ((0, 0), (0, 2), (0, 0)))

    w_in = p['w_in']
    o_z, o_b, o_q = CONV_CH, CONV_CH + 512, CONV_CH + 512 + 2 * DN_HEADS
    w_main = jnp.concatenate([w_in[:, :o_b], w_in[:, o_q:]], axis=1).astype(BF16)
    wb = jnp.pad(w_in[:, o_b:o_q], ((0, 0), (0, LANES - 2 * DN_HEADS)))
    wb_hi = wb.astype(BF16)
    wb_lo = (wb - wb_hi.astype(F32)).astype(BF16)
    g_pre_mix = p['g_pre_mix'].reshape(1, d)

    conv_w8 = jnp.pad(p['conv_w'], ((0, 8 - CONV_W), (0, 0)))
    pad_g = lambda a: jnp.pad(a.reshape(1, DN_HEADS), ((0, 0), (DN_HEADS, LANES - 2 * DN_HEADS)))
    alog_row, dtb_row = pad_g(p['a_log']), pad_g(p['dt_bias'])
    onorm_a = p['onorm_a'].reshape(1, DN_DIM)
    onorm_b2 = jnp.tile(p['onorm_b'].reshape(1, SB_DIM), (1, 2))

    w_out16 = p['w_out'].astype(BF16)
    wr = jnp.pad(jnp.concatenate([p['w_router_group'], p['w_router_expert']], axis=1),
                 ((0, 0), (0, LANES - N_GROUPS - N_EXPERTS)))
    wr_hi = wr.astype(BF16)
    wr_lo = (wr - wr_hi.astype(F32)).astype(BF16)
    b_r = jnp.pad(jnp.concatenate([p['b_router_group'], p['b_router_expert']]).reshape(1, -1),
                  ((0, 0), (0, LANES - N_GROUPS - N_EXPERTS)))
    g_post_mix = p['g_post_mix'].reshape(1, d)
    g_pre_ffn = p['g_pre_ffn'].reshape(1, d)
    g_post_ffn = p['g_post_ffn'].reshape(1, d)

    def mixer(x, tm, seq_rows, mod_row0, hist8, s0, k_old, v_old, chunk, n_chunks, bq, cnt0):
        b, t, _ = x.shape
        x2d = x.reshape(b * t, d)
        a_in, z, bg, q16, kb, vb, k16, v16 = _proj(x2d, mod8, g_pre_mix, w_main, wb_hi, wb_lo,
                                                    tm, seq_rows, mod_row0)
        r3 = lambda a: a.reshape(b, t, a.shape[-1])
        oa16, s_new = _delta(r3(a_in), r3(z), r3(bg), hist8, s0, conv_w8, alog_row, dtb_row, onorm_a,
                             chunk, n_chunks)
        k16, v16 = r3(k16), r3(v16)
        if k_old is not None:
            k16 = jnp.concatenate([k_old.reshape(b, -1, SB_WIDTH).astype(BF16), k16], axis=1)
            v16 = jnp.concatenate([v_old.reshape(b, -1, SB_WIDTH).astype(BF16), v16], axis=1)
        n_pad = (-k16.shape[1]) % KEY_TILE
        k16 = jnp.pad(k16, ((0, 0), (n_pad, 0), (0, 0)))
        v16 = jnp.pad(v16, ((0, 0), (n_pad, 0), (0, 0)))
        ob16 = _sb_attn(r3(q16), k16, v16, onorm_b2, bq, n_pad)
        x1, h2, route, cnt = _post(oa16.reshape(b * t, 512), ob16.reshape(b * t, 512), x2d, mod8,
                                   g_post_mix, g_pre_ffn, w_out16, wr_hi, wr_lo, b_r, cnt0, tm, seq_rows,
                                   mod_row0)
        new_conv = r3(a_in)[:, t - (CONV_W - 1):, :]
        return (x1, h2, route, cnt, kb.reshape(b, t, SB_HEADS, SB_DIM), vb.reshape(b, t, SB_HEADS, SB_DIM),
                s_new, new_conv)

    zero_hist = jnp.zeros((bp, 8, CONV_CH), F32)
    zero_s = jnp.zeros((bp, DN_HEADS, DN_DIM, DN_DIM), F32)
    hist_s = jnp.pad(conv_s, ((0, 0), (8 - (CONV_W - 1), 0), (0, 0)))
    tm_p = min(256, tp)
    nc_p = max(1, min(4, tp // DELTA_BLOCK))
    x1p, h2p, rp, cnt_p, kp, vp, sp, cp = mixer(x_p, tm_p, tp, 0, zero_hist, zero_s, None, None,
                                                 min(DELTA_BLOCK, tp), nc_p, min(KEY_TILE, tp),
                                                 jnp.zeros((1, LANES), F32))
    x1s, h2s, rs, cnt, ks, vs, ss, cs = mixer(x_s, ts, ts, bp, hist_s, s0_s, k_past, v_past,
                                               min(DELTA_BLOCK, ts), max(1, ts // DELTA_BLOCK),
                                               min(KEY_TILE, ts), cnt_p)

    n_blocks = -(-2 * n_tok // MOE_BLOCK) + N_EXPERTS
    blk_e, nvalid, seg, pad_start = _segment_plan(cnt, n_blocks)
    dest_p = _token_rows(rp, pad_start)
    dest_s = _token_rows(rs, pad_start)
    x_sorted = _dispatch(seg, dest_p, dest_s, h2p, h2s, n_blocks, tm_p)
    y_sorted = _moe(blk_e, nvalid, x_sorted, p['w_gate'], p['w_up'], p['w_down'])
    y_p = _combine(dest_p, rp, y_sorted, x1p, mod8, g_post_ffn, tm_p, tp, 0).reshape(bp, tp, d)
    y_s = _combine(dest_s, rs, y_sorted, x1s, mod8, g_post_ffn, ts, ts, bp).reshape(bs, ts, d)
    return y_p, y_s, kp, vp, sp, cp, ks, vs, ss, cs


def kernel(x_prompt, x_sample, c_prompt, c_sample, cache_k, cache_v, state_delta, state_conv, w_ada, b_ada, g_pre_mix, g_post_mix, g_pre_ffn, g_post_ffn, w_in, conv_w, a_log, dt_bias, onorm_a, onorm_b, w_out, w_router_group, b_router_group, w_router_expert, b_router_expert, w_gate, w_up, w_down):
    depth = w_in.shape[0]
    y_p, y_s = x_prompt, x_sample
    outs = [[] for _ in range(8)]
    for l in range(depth):
        p = dict(w_ada=w_ada[l], b_ada=b_ada[l], g_pre_mix=g_pre_mix[l], g_post_mix=g_post_mix[l],
                 g_pre_ffn=g_pre_ffn[l], g_post_ffn=g_post_ffn[l], w_in=w_in[l], conv_w=conv_w[l],
                 a_log=a_log[l], dt_bias=dt_bias[l], onorm_a=onorm_a[l], onorm_b=onorm_b[l],
                 w_out=w_out[l], w_router_group=w_router_group[l], b_router_group=b_router_group[l],
                 w_router_expert=w_router_expert[l], b_router_expert=b_router_expert[l],
                 w_gate=w_gate[l], w_up=w_up[l], w_down=w_down[l])
        res = _layer(y_p, y_s, c_prompt, c_sample, cache_k[l], cache_v[l], state_delta[l], state_conv[l], p)
        y_p, y_s = res[0], res[1]
        for lst, r in zip(outs, res[2:]):
            lst.append(r)
    return (y_p, y_s) + tuple(jnp.stack(o) for o in outs)
```

```python
import functools
import math

import jax
import jax.numpy as jnp
from jax import lax
from jax.experimental import pallas as pl
from jax.experimental.pallas import tpu as pltpu

F32 = jnp.float32
BF16 = jnp.bfloat16

D_MODEL = 1024
DN_HEADS = 4
DN_DIM = 128
CONV_W = 4
CONV_CH = DN_HEADS * 3 * DN_DIM
DELTA_BLOCK = 64
SB_HEADS = 8
SB_DIM = 64
SB_WIDTH = SB_HEADS * SB_DIM
N_GROUPS = 4
EXPERTS_PER_GROUP = 8
N_EXPERTS = N_GROUPS * EXPERTS_PER_GROUP
D_EXPERT = D_MODEL // 2
MOE_BLOCK = 256
EPS = 1e-6

LANES = 128
KEY_TILE = 128
ATTN_WINDOW = 2 * KEY_TILE
EXP_ZERO_BELOW = -104.0
VMEM_LIMIT = 56 * 1024 * 1024


def _cparams(sem):
    return pltpu.CompilerParams(dimension_semantics=sem, vmem_limit_bytes=VMEM_LIMIT)


def _split(a):
    hi = a.astype(BF16)
    lo = (a - hi.astype(F32)).astype(BF16)
    return hi, lo


def _dot(a, b, dims=(((1,), (0,)), ((), ()))):
    return lax.dot_general(a, b, dims, preferred_element_type=F32)


def _dot3(a, b, dims=(((1,), (0,)), ((), ()))):
    ah, al = _split(a)
    bh, bl = _split(b)
    return _dot(ah, bh, dims) + (_dot(al, bh, dims) + _dot(ah, bl, dims))


_NT = (((1,), (1,)), ((), ()))
_TN = (((0,), (0,)), ((), ()))


def _silu(x):
    return x * jax.nn.sigmoid(x)


def _softplus(x):
    return jnp.maximum(x, 0.0) + jnp.log(1.0 + jnp.exp(-jnp.abs(x)))


def _ada_kernel(c_ref, w_ref, b_ref, o_ref):
    s = _silu(c_ref[...]).astype(BF16)
    o_ref[...] = _dot(s, w_ref[...].astype(BF16)) + b_ref[...]


def _ada(c_all, w_ada, b_ada):
    rows = c_all.shape[0]
    n = w_ada.shape[1]
    tn = 1024
    return pl.pallas_call(
        _ada_kernel,
        grid=(n // tn,),
        in_specs=[pl.BlockSpec((rows, D_MODEL), lambda j: (0, 0)),
                  pl.BlockSpec((D_MODEL, tn), lambda j: (0, j)),
                  pl.BlockSpec((1, tn), lambda j: (0, j))],
        out_specs=pl.BlockSpec((rows, tn), lambda j: (0, j)),
        out_shape=jax.ShapeDtypeStruct((rows, n), F32),
        compiler_params=_cparams(("arbitrary",)),
        name="ada",
    )(c_all, w_ada, b_ada.reshape(1, n))


def _rms(x, gain):
    return x * lax.rsqrt(jnp.mean(x * x, axis=-1, keepdims=True) + EPS) * gain


def _proj_kernel(x_ref, mod_ref, g_ref, wm_ref, wbh_ref, wbl_ref,
                 a_ref, z_ref, bg_ref, q_ref, k_ref, v_ref, k16_ref, v16_ref):
    mod = mod_ref[0]
    h = _rms(x_ref[...], g_ref[...]) * (1.0 + mod[1:2]) + mod[0:1]
    hh, hl = _split(h)
    p = _dot(hh, wm_ref[...])
    a_ref[...] = p[:, 0:CONV_CH]
    z_ref[...] = p[:, CONV_CH:CONV_CH + 512]
    o = CONV_CH + 512
    q_ref[...] = (p[:, o:o + 512] * (SB_DIM ** -0.5)).astype(BF16)
    k = p[:, o + 512:o + 1024]
    v = p[:, o + 1024:o + 1536]
    k_ref[...] = k
    v_ref[...] = v
    k16_ref[...] = k.astype(BF16)
    v16_ref[...] = v.astype(BF16)
    wbh = wbh_ref[...]
    bg_ref[...] = _dot(hh, wbh) + (_dot(hl, wbh) + _dot(hh, wbl_ref[...]))


def _proj(x2d, mod8, g_pre, w_main, wb_hi, wb_lo, tm, seq_rows, mod_row0):
    n = x2d.shape[0]
    nm = w_main.shape[1]
    row = lambda i: (i, 0)
    const = lambda i: (0, 0)
    modmap = lambda i: (mod_row0 + (i * tm) // seq_rows, 0, 0)
    outs = [(CONV_CH, F32), (512, F32), (LANES, F32), (512, BF16), (512, F32), (512, F32),
            (512, BF16), (512, BF16)]
    return pl.pallas_call(
        _proj_kernel,
        grid=(n // tm,),
        in_specs=[pl.BlockSpec((tm, D_MODEL), row),
                  pl.BlockSpec((1, 8, D_MODEL), modmap),
                  pl.BlockSpec((1, D_MODEL), const),
                  pl.BlockSpec((D_MODEL, nm), const),
                  pl.BlockSpec((D_MODEL, LANES), const),
                  pl.BlockSpec((D_MODEL, LANES), const)],
        out_specs=[pl.BlockSpec((tm, w), row) for w, _ in outs],
        out_shape=[jax.ShapeDtypeStruct((n, w), dt) for w, dt in outs],
        compiler_params=_cparams(("arbitrary",)),
        name="proj",
    )(x2d, mod8, g_pre, w_main, wb_hi, wb_lo)


def _delta_kernel(a_ref, z_ref, bg_ref, hist0_ref, s0_ref, cw_ref, alog_ref, dtb_ref, on_ref,
                  o_ref, sfin_ref, hist_sc, s_sc, *, chunk, n_chunks):
    t_idx = pl.program_id(1)
    tt = chunk * n_chunks

    @pl.when(t_idx == 0)
    def _():
        hist_sc[...] = hist0_ref[0]
        s_sc[...] = s0_ref[0]

    x = a_ref[0]
    xx = jnp.concatenate([hist_sc[...], x], axis=0)
    cw = cw_ref[...]
    conv = x * cw[CONV_W - 1:CONV_W]
    for s in range(1, CONV_W):
        conv = conv + pltpu.roll(xx, s, 0)[8:] * cw[CONV_W - 1 - s:CONV_W - s]
    conv = _silu(conv)
    hist_sc[...] = x[tt - 8:tt]

    bg = bg_ref[0]
    lane = lax.broadcasted_iota(jnp.int32, (1, LANES), 1)
    g_lane = (lane >= DN_HEADS) & (lane < 2 * DN_HEADS)
    neg_a = jnp.where(g_lane, -jnp.exp(alog_ref[...]), 0.0)
    beta_all = jax.nn.sigmoid(bg)
    g_all = neg_a * _softplus(bg + dtb_ref[...])

    ri = lax.broadcasted_iota(jnp.int32, (chunk, chunk), 0)
    ci = lax.broadcasted_iota(jnp.int32, (chunk, chunk), 1)
    incl = ri >= ci
    strict = ri > ci
    tri = incl.astype(F32)
    eye = (ri == ci).astype(F32)
    ones_cc = jnp.ones((chunk, chunk), F32)
    onorm = on_ref[...]

    chunks = range(n_chunks)
    units = [(c, h) for c in chunks for h in range(DN_HEADS)]
    rows = lambda c: slice(c * chunk, (c + 1) * chunk)

    gc_all = [_dot3(tri, g_all[rows(c)]) for c in chunks]
    diag = [jnp.concatenate([eye * gc_all[c][:, DN_HEADS + h:DN_HEADS + h + 1] for h in range(DN_HEADS)], axis=1)
            for c in chunks]
    gc_rows = [_dot3(ones_cc, diag[c]) for c in chunks]

    q, k, v, beta, gcol, glast, eg, decay, kbeta = {}, {}, {}, {}, {}, {}, {}, {}, {}
    for (c, h) in units:
        lo = h * DN_DIM
        qq = conv[rows(c), lo:lo + DN_DIM]
        kk = conv[rows(c), 512 + lo:512 + lo + DN_DIM]
        u_ = (c, h)
        q[u_] = qq * lax.rsqrt(jnp.sum(qq * qq, axis=-1, keepdims=True) + EPS) * (DN_DIM ** -0.5)
        k[u_] = kk * lax.rsqrt(jnp.sum(kk * kk, axis=-1, keepdims=True) + EPS)
        v[u_] = conv[rows(c), 1024 + lo:1024 + lo + DN_DIM]
        beta[u_] = beta_all[rows(c), h:h + 1]
        gcol[u_] = gc_all[c][:, DN_HEADS + h:DN_HEADS + h + 1]
        glast[u_] = gcol[u_][chunk - 1:chunk, :]
        eg[u_] = jnp.exp(gcol[u_])
        grow = gc_rows[c][:, h * chunk:(h + 1) * chunk]
        decay[u_] = jnp.where(incl, jnp.exp(jnp.minimum(gcol[u_] - grow, 0.0)), 0.0)
        kbeta[u_] = k[u_] * beta[u_]

    k16 = {u_: k[u_].astype(BF16) for u_ in units}
    kq = {u_: _dot(jnp.concatenate([kbeta[u_], q[u_]], axis=0).astype(BF16), k16[u_], _NT) for u_ in units}
    lower = {u_: jnp.where(strict, kq[u_][:chunk] * decay[u_], 0.0) for u_ in units}
    attn16 = {u_: jnp.where(incl, kq[u_][chunk:] * decay[u_], 0.0).astype(BF16) for u_ in units}

    sol = {u_: jnp.concatenate([v[u_] * beta[u_], kbeta[u_] * eg[u_]], axis=1) for u_ in units}
    lp = lower
    p = 1
    while p < chunk:
        lsp = {u_: _split(lp[u_]) for u_ in units}
        ssp = {u_: _split(sol[u_]) for u_ in units}
        stack = {u_: jnp.concatenate(lsp[u_], axis=0) for u_ in units}
        p1 = {u_: _dot(stack[u_], ssp[u_][0]) for u_ in units}
        p2 = {u_: _dot(lsp[u_][0], ssp[u_][1]) for u_ in units}
        upd = {u_: p1[u_][:chunk] + (p1[u_][chunk:] + p2[u_]) for u_ in units}
        sol = {u_: (sol[u_] - upd[u_]) if p == 1 else (sol[u_] + upd[u_]) for u_ in units}
        if 2 * p < chunk:
            q1 = {u_: _dot(stack[u_], lsp[u_][0]) for u_ in units}
            q2 = {u_: _dot(lsp[u_][0], lsp[u_][1]) for u_ in units}
            lp = {u_: q1[u_][:chunk] + (q1[u_][chunk:] + q2[u_]) for u_ in units}
        p *= 2

    usol = {u_: sol[u_][:, :DN_DIM] for u_ in units}
    wq16 = {u_: jnp.concatenate([sol[u_][:, DN_DIM:], q[u_] * eg[u_]], axis=0).astype(BF16) for u_ in units}
    kd16 = {u_: (k[u_] * jnp.exp(glast[u_] - gcol[u_])).astype(BF16) for u_ in units}

    for c in chunks:
        hs = range(DN_HEADS)
        s_old = [s_sc[h] for h in hs]
        ws = [_dot(wq16[(c, h)], s_old[h].astype(BF16)) for h in hs]
        v_new = [(usol[(c, h)] - ws[h][:chunk]).astype(BF16) for h in hs]
        o_in = [_dot(attn16[(c, h)], v_new[h]) for h in hs]
        ds = [_dot(kd16[(c, h)], v_new[h], _TN) for h in hs]
        for h in hs:
            lo = h * DN_DIM
            s_sc[h] = s_old[h] * jnp.exp(glast[(c, h)]) + ds[h]
            o = ws[h][chunk:] + o_in[h]
            zg = z_ref[0, rows(c), lo:lo + DN_DIM]
            o_ref[0, rows(c), lo:lo + DN_DIM] = (_rms(o, onorm) * _silu(zg)).astype(BF16)

    @pl.when(t_idx == pl.num_programs(1) - 1)
    def _():
        sfin_ref[0] = s_sc[...]


def _delta(a_in, z, bg, hist8, s0, conv_w8, alog_row, dtb_row, onorm_a, chunk, n_chunks):
    b, t, _ = a_in.shape
    tt = chunk * n_chunks
    tile = lambda bi, ti: (bi, ti, 0)
    per_b3 = lambda bi, ti: (bi, 0, 0)
    per_b4 = lambda bi, ti: (bi, 0, 0, 0)
    const = lambda bi, ti: (0, 0)
    kern = functools.partial(_delta_kernel, chunk=chunk, n_chunks=n_chunks)
    return pl.pallas_call(
        kern,
        grid=(b, t // tt),
        in_specs=[pl.BlockSpec((1, tt, CONV_CH), tile),
                  pl.BlockSpec((1, tt, 512), tile),
                  pl.BlockSpec((1, tt, LANES), tile),
                  pl.BlockSpec((1, 8, CONV_CH), per_b3),
                  pl.BlockSpec((1, DN_HEADS, DN_DIM, DN_DIM), per_b4),
                  pl.BlockSpec((8, CONV_CH), const),
                  pl.BlockSpec((1, LANES), const),
                  pl.BlockSpec((1, LANES), const),
                  pl.BlockSpec((1, DN_DIM), const)],
        out_specs=[pl.BlockSpec((1, tt, 512), tile),
                   pl.BlockSpec((1, DN_HEADS, DN_DIM, DN_DIM), per_b4)],
        out_shape=[jax.ShapeDtypeStruct((b, t, 512), BF16),
                   jax.ShapeDtypeStruct((b, DN_HEADS, DN_DIM, DN_DIM), F32)],
        scratch_shapes=[pltpu.VMEM((8, CONV_CH), F32),
                        pltpu.VMEM((DN_HEADS, DN_DIM, DN_DIM), F32)],
        compiler_params=_cparams(("arbitrary", "arbitrary")),
        name="delta",
    )(a_in, z, bg, hist8, s0, conv_w8, alog_row, dtb_row, onorm_a)


def _sb_kernel(q_ref, k0_ref, k1_ref, k2_ref, v0_ref, v1_ref, v2_ref, k_hbm, v_hbm, on_ref,
               o_ref, kbuf, vbuf, qsel, acc, carry, sem, *, bq, n_sub, n_pad, q_off):
    b = pl.program_id(0)
    i = pl.program_id(1)
    g = bq * n_sub
    qend_step = q_off + (i + 1) * g
    win_start = qend_step - 3 * KEY_TILE
    n_pairs = SB_HEADS // 2
    heads = range(SB_HEADS)

    half_lane = lax.broadcasted_iota(jnp.int32, (bq, LANES), 1) < SB_DIM
    rj = lax.broadcasted_iota(jnp.int32, (2 * KEY_TILE, 2 * KEY_TILE), 0) % KEY_TILE
    cj = lax.broadcasted_iota(jnp.int32, (2 * KEY_TILE, 2 * KEY_TILE), 1)
    suffix2 = ((rj > cj) | (cj >= KEY_TILE)).astype(BF16)

    kwin = (k0_ref, k1_ref, k2_ref)
    vwin = (v0_ref, v1_ref, v2_ref)

    def window(refs, off, p):
        parts = []
        for blk in range(3):
            lo, hi = max(off, blk * KEY_TILE), min(off + ATTN_WINDOW, (blk + 1) * KEY_TILE)
            if lo < hi:
                parts.append(refs[2 - blk][0, lo - blk * KEY_TILE:hi - blk * KEY_TILE, p * LANES:(p + 1) * LANES])
        return jnp.concatenate(parts, axis=0)

    row = lax.broadcasted_iota(jnp.int32, (bq, ATTN_WINDOW), 0)
    col = lax.broadcasted_iota(jnp.int32, (bq, ATTN_WINDOW), 1)
    causal = col < row + (ATTN_WINDOW - bq)
    subs = range(n_sub)
    offs = [3 * KEY_TILE - g + (s + 1) * bq - ATTN_WINDOW for s in subs]
    mask = [causal & (col >= n_pad - (win_start + offs[s])) for s in subs]
    zs, sp = {}, {}
    for s in subs:
        for p in range(n_pairs):
            qf = q_ref[0, s * bq:(s + 1) * bq, p * LANES:(p + 1) * LANES].astype(F32)
            qq = jnp.concatenate([jnp.where(half_lane, qf, 0.0), jnp.where(half_lane, 0.0, qf)], axis=0)
            zz = _dot(qq.astype(BF16), window(kwin, offs[s], p), _NT)
            zs[(s, 2 * p)], zs[(s, 2 * p + 1)] = zz[:bq], zz[bq:]
    pieces = []
    for s in subs:
        for h in heads:
            sp[(s, h)] = _softplus(zs[(s, h)])
            hi, lo = _split(jnp.where(mask[s], -sp[(s, h)], 0.0))
            pieces.append(jnp.concatenate([hi[:, KEY_TILE:], lo[:, KEY_TILE:]], axis=1))
            pieces.append(jnp.concatenate([hi[:, :KEY_TILE], lo[:, :KEY_TILE]], axis=1))
    cs_all = _dot(jnp.concatenate(pieces, axis=0), suffix2)
    done = []
    for s in subs:
        worst = jnp.full((bq, KEY_TILE), -jnp.inf, F32)
        a = {}
        for h in heads:
            r0 = ((s * SB_HEADS + h) * 2) * bq
            cs_new = cs_all[r0:r0 + bq]
            cs_old = cs_all[r0 + bq:r0 + 2 * bq]
            tot_new = cs_new[:, KEY_TILE:]
            within = jnp.concatenate([cs_old[:, :KEY_TILE] + tot_new, cs_new[:, :KEY_TILE]], axis=1)
            log_w = (zs[(s, h)] - sp[(s, h)]) + within
            a[h] = jnp.where(mask[s], jnp.exp(log_w), 0.0).astype(BF16)
            c_new = tot_new + cs_old[:, KEY_TILE:]
            carry[s, h] = c_new
            worst = jnp.maximum(worst, c_new)
        for p in range(n_pairs):
            pv = _dot(jnp.concatenate([a[2 * p], a[2 * p + 1]], axis=0), window(vwin, offs[s], p))
            acc[s, p] = jnp.where(half_lane, pv[:bq], pv[bq:])
        done.append((jnp.max(worst) < EXP_ZERO_BELOW).astype(jnp.int32))

    rj1 = lax.broadcasted_iota(jnp.int32, (KEY_TILE, 2 * KEY_TILE), 0)
    cj1 = lax.broadcasted_iota(jnp.int32, (KEY_TILE, 2 * KEY_TILE), 1)
    suffix1 = ((rj1 > cj1) | (cj1 >= KEY_TILE)).astype(BF16)
    col1 = lax.broadcasted_iota(jnp.int32, (bq, KEY_TILE), 1)
    for s in subs:
        swept_from = win_start + offs[s]

        @pl.when((done[s] == 0) & (swept_from > n_pad))
        def _():
            for p in range(n_pairs):
                qf = q_ref[0, s * bq:(s + 1) * bq, p * LANES:(p + 1) * LANES].astype(F32)
                qsel[2 * p] = jnp.where(half_lane, qf, 0.0).astype(BF16)
                qsel[2 * p + 1] = jnp.where(half_lane, 0.0, qf).astype(BF16)

            def body(state):
                upper, _ = state
                start = jnp.maximum(upper - KEY_TILE, 0)
                copies = []
                for p in range(n_pairs):
                    for src, dst in ((k_hbm, kbuf), (v_hbm, vbuf)):
                        cp = pltpu.make_async_copy(
                            src.at[b, pl.ds(pl.multiple_of(start, 16), KEY_TILE), pl.ds(p * LANES, LANES)],
                            dst.at[p], sem)
                        cp.start()
                        copies.append(cp)
                for cp in copies:
                    cp.wait()
                kpos = start + col1
                m1 = (kpos < upper) & (kpos >= n_pad)
                zz = [_dot(qsel[h], kbuf[h // 2], _NT) for h in heads]
                spp = [_softplus(zz[h]) for h in heads]
                pcs = []
                for h in heads:
                    pcs.extend(_split(jnp.where(m1, -spp[h], 0.0)))
                cs1 = _dot(jnp.concatenate(pcs, axis=0), suffix1)
                worst = jnp.full((bq, KEY_TILE), -jnp.inf, F32)
                aa = []
                for h in heads:
                    c = cs1[2 * h * bq:(2 * h + 1) * bq] + cs1[(2 * h + 1) * bq:(2 * h + 2) * bq]
                    c_old = carry[s, h]
                    aa.append(jnp.where(m1, jnp.exp((zz[h] - spp[h]) + c[:, :KEY_TILE] + c_old), 0.0).astype(BF16))
                    c_new = c_old + c[:, KEY_TILE:]
                    carry[s, h] = c_new
                    worst = jnp.maximum(worst, c_new)
                for p in range(n_pairs):
                    pv0 = _dot(aa[2 * p], vbuf[p])
                    pv1 = _dot(aa[2 * p + 1], vbuf[p])
                    acc[s, p] = acc[s, p] + jnp.where(half_lane, pv0, pv1)
                return start, (jnp.max(worst) < EXP_ZERO_BELOW).astype(jnp.int32)

            lax.while_loop(lambda st: (st[0] > n_pad) & (st[1] == 0), body, (swept_from, jnp.int32(0)))

    onb = on_ref[...]
    for s in subs:
        for p in range(n_pairs):
            o = acc[s, p]
            sq = o * o
            s_lo = jnp.sum(jnp.where(half_lane, sq, 0.0), axis=-1, keepdims=True)
            s_hi = jnp.sum(jnp.where(half_lane, 0.0, sq), axis=-1, keepdims=True)
            ms = jnp.where(half_lane, s_lo, s_hi) * (1.0 / SB_DIM)
            o_ref[0, s * bq:(s + 1) * bq, p * LANES:(p + 1) * LANES] = (o * lax.rsqrt(ms + EPS) * onb).astype(BF16)


def _sb_attn(q16, k16p, v16p, onorm_b2, bq, n_sub, n_pad):
    b, tq, _ = q16.shape
    tkp = k16p.shape[1]
    q_off = tkp - tq
    g = bq * n_sub
    assert tq % g == 0 and g <= KEY_TILE and bq % 16 == 0
    assert all((q_off + (i + 1) * g) % KEY_TILE == 0 for i in range(tq // g))

    def kmap(back):
        def f(bi, i):
            last = (q_off + (i + 1) * g) // KEY_TILE - 1
            return (bi, jnp.maximum(last - back, 0), 0)
        return f

    qmap = lambda bi, i: (bi, i, 0)
    kern = functools.partial(_sb_kernel, bq=bq, n_sub=n_sub, n_pad=n_pad, q_off=q_off)
    kspec = [pl.BlockSpec((1, KEY_TILE, SB_WIDTH), kmap(back)) for back in range(3)]
    return pl.pallas_call(
        kern,
        grid=(b, tq // g),
        in_specs=[pl.BlockSpec((1, g, SB_WIDTH), qmap)] + kspec + kspec
                 + [pl.BlockSpec(memory_space=pl.ANY), pl.BlockSpec(memory_space=pl.ANY),
                    pl.BlockSpec((1, LANES), lambda bi, i: (0, 0))],
        out_specs=pl.BlockSpec((1, g, SB_WIDTH), qmap),
        out_shape=jax.ShapeDtypeStruct((b, tq, SB_WIDTH), BF16),
        scratch_shapes=[pltpu.VMEM((SB_HEADS // 2, KEY_TILE, LANES), BF16),
                        pltpu.VMEM((SB_HEADS // 2, KEY_TILE, LANES), BF16),
                        pltpu.VMEM((SB_HEADS, bq, LANES), BF16),
                        pltpu.VMEM((n_sub, SB_HEADS // 2, bq, LANES), F32),
                        pltpu.VMEM((n_sub, SB_HEADS, bq, KEY_TILE), F32),
                        pltpu.SemaphoreType.DMA(())],
        compiler_params=_cparams(("arbitrary", "arbitrary")),
        name="sb_attn",
    )(q16, k16p, k16p, k16p, v16p, v16p, v16p, k16p, v16p, onorm_b2)


def _post_kernel(oa_ref, ob_ref, x_ref, mod_ref, gpm_ref, gpf_ref, wo_ref, wrh_ref, wrl_ref, br_ref, cnt0_ref,
                 x1_ref, h2_ref, route_ref, cnt_ref):
    @pl.when(pl.program_id(0) == 0)
    def _():
        cnt_ref[...] = cnt0_ref[...]

    mod = mod_ref[0]
    mix = _dot(oa_ref[...], wo_ref[0:512, :]) + _dot(ob_ref[...], wo_ref[512:1024, :])
    x1 = x_ref[...] + mod[2:3] * _rms(mix, gpm_ref[...])
    x1_ref[...] = x1
    h2 = _rms(x1, gpf_ref[...]) * (1.0 + mod[4:5]) + mod[3:4]
    h2_ref[...] = h2
    hh, hl = _split(h2)
    wrh = wrh_ref[...]
    logits = _dot(hh, wrh) + (_dot(hl, wrh) + _dot(hh, wrl_ref[...])) + br_ref[...]
    lane = lax.broadcasted_iota(jnp.int32, logits.shape, 1).astype(F32)
    neg = -jnp.inf
    nl = float(LANES)
    lg = jnp.where(lane < N_GROUPS, logits, neg)
    gmax = jnp.max(lg, axis=-1, keepdims=True)
    grp = jnp.min(jnp.where(lg == gmax, lane, nl), axis=-1, keepdims=True)
    p_grp = 1.0 / jnp.sum(jnp.exp(lg - gmax), axis=-1, keepdims=True)
    first = N_GROUPS + grp * EXPERTS_PER_GROUP
    le = jnp.where((lane >= first) & (lane < first + EXPERTS_PER_GROUP), logits, neg)
    emax = jnp.max(le, axis=-1, keepdims=True)
    i1 = jnp.min(jnp.where(le == emax, lane, nl), axis=-1, keepdims=True)
    esum = jnp.sum(jnp.exp(le - emax), axis=-1, keepdims=True)
    le2 = jnp.where(lane == i1, neg, le)
    e2max = jnp.max(le2, axis=-1, keepdims=True)
    i2 = jnp.min(jnp.where(le2 == e2max, lane, nl), axis=-1, keepdims=True)
    p1 = 1.0 / esum
    p2 = jnp.exp(e2max - emax) / esum
    w1 = p_grp * p1 / (p1 + p2)
    w2 = p_grp * p2 / (p1 + p2)
    e1 = i1 - N_GROUPS
    e2 = i2 - N_GROUPS
    hot1 = (lane == e1).astype(F32)
    hot2 = (lane == e2).astype(F32)
    both = hot1 + hot2
    tm = logits.shape[0]
    ti = lax.broadcasted_iota(jnp.int32, (tm, tm), 0)
    tj = lax.broadcasted_iota(jnp.int32, (tm, tm), 1)
    earlier = _dot((ti > tj).astype(BF16), both.astype(BF16)) + cnt_ref[...]
    rank1 = jnp.sum(hot1 * earlier, axis=-1, keepdims=True)
    rank2 = jnp.sum(hot2 * (earlier + hot1), axis=-1, keepdims=True)
    cnt_ref[...] = cnt_ref[...] + jnp.sum(both, axis=0, keepdims=True)
    out = jnp.where(lane == 0.0, e1, 0.0)
    out = jnp.where(lane == 1.0, e2, out)
    out = jnp.where(lane == 2.0, w1, out)
    out = jnp.where(lane == 3.0, w2, out)
    out = jnp.where(lane == 4.0, rank1, out)
    out = jnp.where(lane == 5.0, rank2, out)
    route_ref[...] = out


def _post(oa16, ob16, x2d, mod8, g_post_mix, g_pre_ffn, w_out16, wr_hi, wr_lo, b_r, cnt0, tm, seq_rows,
          mod_row0):
    n = x2d.shape[0]
    row = lambda i: (i, 0)
    const = lambda i: (0, 0)
    modmap = lambda i: (mod_row0 + (i * tm) // seq_rows, 0, 0)
    return pl.pallas_call(
        _post_kernel,
        grid=(n // tm,),
        in_specs=[pl.BlockSpec((tm, 512), row),
                  pl.BlockSpec((tm, 512), row),
                  pl.BlockSpec((tm, D_MODEL), row),
                  pl.BlockSpec((1, 8, D_MODEL), modmap),
                  pl.BlockSpec((1, D_MODEL), const),
                  pl.BlockSpec((1, D_MODEL), const),
                  pl.BlockSpec((D_MODEL, D_MODEL), const),
                  pl.BlockSpec((D_MODEL, LANES), const),
                  pl.BlockSpec((D_MODEL, LANES), const),
                  pl.BlockSpec((1, LANES), const),
                  pl.BlockSpec((1, LANES), const)],
        out_specs=[pl.BlockSpec((tm, D_MODEL), row),
                   pl.BlockSpec((tm, D_MODEL), row),
                   pl.BlockSpec((tm, LANES), row),
                   pl.BlockSpec((1, LANES), const)],
        out_shape=[jax.ShapeDtypeStruct((n, D_MODEL), F32),
                   jax.ShapeDtypeStruct((n, D_MODEL), F32),
                   jax.ShapeDtypeStruct((n, LANES), F32),
                   jax.ShapeDtypeStruct((1, LANES), F32)],
        compiler_params=_cparams(("arbitrary",)),
        name="post",
    )(oa16, ob16, x2d, mod8, g_post_mix, g_pre_ffn, w_out16, wr_hi, wr_lo, b_r, cnt0)


def _dispatch_kernel(seg_ref, dp_ref, ds_ref, hp_ref, hs_ref, xs_hbm, zbuf, sem, *, n_blocks):
    i = pl.program_id(0)
    last_step = pl.num_programs(0) - 1

    @pl.when(i == 0)
    def _():
        zbuf[...] = jnp.zeros_like(zbuf)

        def zero_block(row0):
            return pltpu.make_async_copy(zbuf, xs_hbm.at[pl.ds(pl.multiple_of(row0, MOE_BLOCK), MOE_BLOCK), :], sem)

        for e in range(N_EXPERTS):
            @pl.when(seg_ref[e] > 0)
            def _():
                zero_block(seg_ref[N_EXPERTS + e] - MOE_BLOCK).start()
        for e in range(N_EXPERTS):
            @pl.when(seg_ref[e] > 0)
            def _():
                zero_block(seg_ref[N_EXPERTS + e] - MOE_BLOCK).wait()

        used = seg_ref[2 * N_EXPERTS - 1] // MOE_BLOCK

        def fill(b, c):
            cp = zero_block(b * MOE_BLOCK)
            cp.start()
            cp.wait()
            return c

        lax.fori_loop(used, n_blocks, fill, 0)

    def scatter(h_ref, dest_ref):
        rows = h_ref.shape[0]
        for t in range(rows):
            for slot in range(2):
                pltpu.make_async_copy(h_ref.at[pl.ds(t, 1), :],
                                      xs_hbm.at[pl.ds(dest_ref[0, 0, 2 * t + slot], 1), :],
                                      sem).start(priority=slot)
        for slot in range(2):
            pltpu.make_async_copy(h_ref, xs_hbm.at[pl.ds(0, rows), :], sem).wait()

    @pl.when(i < last_step)
    def _():
        scatter(hp_ref, dp_ref)

    @pl.when(i == last_step)
    def _():
        scatter(hs_ref, ds_ref)


def _dispatch(seg, dest_p, dest_s, h2p, h2s, n_blocks, tm):
    n_p, n_s = h2p.shape[0], h2s.shape[0]
    steps_p = n_p // tm
    pmap3 = lambda i, sg: (jnp.minimum(i, steps_p - 1), 0, 0)
    pmap2 = lambda i, sg: (jnp.minimum(i, steps_p - 1), 0)
    grid_spec = pltpu.PrefetchScalarGridSpec(
        num_scalar_prefetch=1,
        grid=(steps_p + 1,),
        in_specs=[pl.BlockSpec((1, 1, 2 * tm), pmap3, memory_space=pltpu.SMEM),
                  pl.BlockSpec((1, 1, 2 * n_s), lambda i, sg: (0, 0, 0), memory_space=pltpu.SMEM),
                  pl.BlockSpec((tm, D_MODEL), pmap2),
                  pl.BlockSpec((n_s, D_MODEL), lambda i, sg: (0, 0))],
        out_specs=pl.BlockSpec(memory_space=pl.ANY),
        scratch_shapes=[pltpu.VMEM((MOE_BLOCK, D_MODEL), F32), pltpu.SemaphoreType.DMA(())])
    return pl.pallas_call(
        functools.partial(_dispatch_kernel, n_blocks=n_blocks),
        grid_spec=grid_spec,
        out_shape=jax.ShapeDtypeStruct((n_blocks * MOE_BLOCK, D_MODEL), F32),
        compiler_params=_cparams(("arbitrary",)),
        name="dispatch",
    )(seg, dest_p.reshape(steps_p, 1, 2 * tm), dest_s.reshape(1, 1, 2 * n_s), h2p, h2s)


def _moe_kernel(blk_e_ref, nvalid_ref, x_ref, wg_ref, wu_ref, wd_ref, y_ref, wg16, wu16, wd16):
    i = pl.program_id(0)
    e = blk_e_ref[i]
    e_prev = blk_e_ref[jnp.maximum(i - 1, 0)]

    @pl.when((i == 0) | (e != e_prev))
    def _():
        wg16[...] = wg_ref[0].astype(BF16)
        wu16[...] = wu_ref[0].astype(BF16)
        wd16[...] = wd_ref[0].astype(BF16)

    @pl.when(nvalid_ref[i] > 0)
    def _():
        xb = x_ref[...].astype(BF16)
        g = _dot(xb, wg16[...])
        u = _dot(xb, wu16[...])
        hmid = (_silu(g) * u).astype(BF16)
        y_ref[...] = _dot(hmid, wd16[...])

    @pl.when(nvalid_ref[i] == 0)
    def _():
        y_ref[...] = jnp.zeros_like(y_ref)


def _moe(blk_e, nvalid, x_sorted, w_gate, w_up, w_down):
    n_blocks = blk_e.shape[0]
    wmap = lambda i, be, nv: (be[i], 0, 0)
    xmap = lambda i, be, nv: (jnp.where(nv[i] > 0, i, 0), 0)
    grid_spec = pltpu.PrefetchScalarGridSpec(
        num_scalar_prefetch=2,
        grid=(n_blocks,),
        in_specs=[pl.BlockSpec((MOE_BLOCK, D_MODEL), xmap),
                  pl.BlockSpec((1, D_MODEL, D_EXPERT), wmap),
                  pl.BlockSpec((1, D_MODEL, D_EXPERT), wmap),
                  pl.BlockSpec((1, D_EXPERT, D_MODEL), wmap)],
        out_specs=pl.BlockSpec((MOE_BLOCK, D_MODEL), lambda i, be, nv: (i, 0)),
        scratch_shapes=[pltpu.VMEM((D_MODEL, D_EXPERT), BF16),
                        pltpu.VMEM((D_MODEL, D_EXPERT), BF16),
                        pltpu.VMEM((D_EXPERT, D_MODEL), BF16)])
    return pl.pallas_call(
        _moe_kernel,
        grid_spec=grid_spec,
        out_shape=jax.ShapeDtypeStruct((n_blocks * MOE_BLOCK, D_MODEL), F32),
        compiler_params=_cparams(("arbitrary",)),
        name="moe",
    )(blk_e, nvalid, x_sorted, w_gate, w_up, w_down)


def _combine_kernel(dest_ref, route_ref, x1_ref, mod_ref, g_ref, y_hbm, o_ref, ybuf, sem):
    tm = x1_ref.shape[0]
    for t in range(tm):
        for slot in range(2):
            pltpu.make_async_copy(y_hbm.at[pl.ds(dest_ref[0, 0, 2 * t + slot], 1), :],
                                  ybuf.at[slot, pl.ds(t, 1), :], sem).start(priority=slot)
    for slot in range(2):
        pltpu.make_async_copy(y_hbm.at[pl.ds(0, tm), :], ybuf.at[slot], sem).wait()
    mod = mod_ref[0]
    route = route_ref[...]
    moe = ybuf[0] * route[:, 2:3] + ybuf[1] * route[:, 3:4]
    o_ref[...] = x1_ref[...] + mod[5:6] * _rms(moe, g_ref[...])


def _combine(dest, route, y_sorted, x1, mod8, g_post_ffn, tm, seq_rows, mod_row0):
    n = x1.shape[0]
    row = lambda i: (i, 0)
    modmap = lambda i: (mod_row0 + (i * tm) // seq_rows, 0, 0)
    return pl.pallas_call(
        _combine_kernel,
        grid=(n // tm,),
        in_specs=[pl.BlockSpec((1, 1, 2 * tm), lambda i: (i, 0, 0), memory_space=pltpu.SMEM),
                  pl.BlockSpec((tm, LANES), row),
                  pl.BlockSpec((tm, D_MODEL), row),
                  pl.BlockSpec((1, 8, D_MODEL), modmap),
                  pl.BlockSpec((1, D_MODEL), lambda i: (0, 0)),
                  pl.BlockSpec(memory_space=pl.ANY)],
        out_specs=pl.BlockSpec((tm, D_MODEL), row),
        out_shape=jax.ShapeDtypeStruct((n, D_MODEL), F32),
        scratch_shapes=[pltpu.VMEM((2, tm, D_MODEL), F32), pltpu.SemaphoreType.DMA(())],
        compiler_params=_cparams(("arbitrary",)),
        name="combine",
    )(dest.reshape(n // tm, 1, 2 * tm), route, x1, mod8, g_post_ffn, y_sorted)


def _segment_plan(counts_f, n_blocks):
    counts = counts_f[0, :N_EXPERTS].astype(jnp.int32)
    padded = (counts + MOE_BLOCK - 1) // MOE_BLOCK * MOE_BLOCK
    pad_end = jnp.cumsum(padded)
    pad_start = pad_end - padded
    blk_start = jnp.arange(n_blocks, dtype=jnp.int32) * MOE_BLOCK
    blk_e = jnp.minimum(jnp.sum((pad_end[None, :] <= blk_start[:, None]).astype(jnp.int32), axis=1),
                        N_EXPERTS - 1)
    onehot = blk_e[:, None] == jnp.arange(N_EXPERTS, dtype=jnp.int32)[None, :]
    c_blk = jnp.sum(jnp.where(onehot, counts[None, :], 0), axis=1)
    s_blk = jnp.sum(jnp.where(onehot, pad_start[None, :], 0), axis=1)
    nvalid = jnp.clip(c_blk - (blk_start - s_blk), 0, MOE_BLOCK).astype(jnp.int32)
    seg = jnp.concatenate([counts, pad_end]).astype(jnp.int32)
    return blk_e.astype(jnp.int32), nvalid, seg, pad_start


def _token_rows(route, pad_start):
    eid = route[:, 0:2].astype(jnp.int32)
    rank = route[:, 4:6].astype(jnp.int32)
    onehot = eid[:, :, None] == jnp.arange(N_EXPERTS, dtype=jnp.int32)[None, None, :]
    return rank + jnp.sum(jnp.where(onehot, pad_start[None, None, :], 0), axis=2)


def _layer(x_p, x_s, c_p, c_s, k_past, v_past, s0_s, conv_s, p):
    bp, tp, d = x_p.shape
    bs, ts, _ = x_s.shape
    n_p, n_s = bp * tp, bs * ts
    n_tok = n_p + n_s

    n_seq = bp + bs
    c_all = jnp.zeros((16, d), F32).at[:n_seq].set(jnp.concatenate([c_p, c_s], axis=0))
    mod = _ada(c_all, p['w_ada'], p['b_ada'])
    mod8 = jnp.pad(mod.reshape(16, 6, d), ((0, 0), (0, 2), (0, 0)))

    w_in = p['w_in']
    o_z, o_b, o_q = CONV_CH, CONV_CH + 512, CONV_CH + 512 + 2 * DN_HEADS
    w_main = jnp.concatenate([w_in[:, :o_b], w_in[:, o_q:]], axis=1).astype(BF16)
    wb = jnp.pad(w_in[:, o_b:o_q], ((0, 0), (0, LANES - 2 * DN_HEADS)))
    wb_hi = wb.astype(BF16)
    wb_lo = (wb - wb_hi.astype(F32)).astype(BF16)
    g_pre_mix = p['g_pre_mix'].reshape(1, d)

    conv_w8 = jnp.pad(p['conv_w'], ((0, 8 - CONV_W), (0, 0)))
    pad_g = lambda a: jnp.pad(a.reshape(1, DN_HEADS), ((0, 0), (DN_HEADS, LANES - 2 * DN_HEADS)))
    alog_row, dtb_row = pad_g(p['a_log']), pad_g(p['dt_bias'])
    onorm_a = p['onorm_a'].reshape(1, DN_DIM)
    onorm_b2 = jnp.tile(p['onorm_b'].reshape(1, SB_DIM), (1, 2))

    w_out16 = p['w_out'].astype(BF16)
    wr = jnp.pad(jnp.concatenate([p['w_router_group'], p['w_router_expert']], axis=1),
                 ((0, 0), (0, LANES - N_GROUPS - N_EXPERTS)))
    wr_hi = wr.astype(BF16)
    wr_lo = (wr - wr_hi.astype(F32)).astype(BF16)
    b_r = jnp.pad(jnp.concatenate([p['b_router_group'], p['b_router_expert']]).reshape(1, -1),
                  ((0, 0), (0, LANES - N_GROUPS - N_EXPERTS)))
    g_post_mix = p['g_post_mix'].reshape(1, d)
    g_pre_ffn = p['g_pre_ffn'].reshape(1, d)
    g_post_ffn = p['g_post_ffn'].reshape(1, d)

    def mixer(x, tm, seq_rows, mod_row0, hist8, s0, k_old, v_old, chunk, n_chunks, bq, n_sub, cnt0):
        b, t, _ = x.shape
        x2d = x.reshape(b * t, d)
        a_in, z, bg, q16, kb, vb, k16, v16 = _proj(x2d, mod8, g_pre_mix, w_main, wb_hi, wb_lo,
                                                    tm, seq_rows, mod_row0)
        r3 = lambda a: a.reshape(b, t, a.shape[-1])
        oa16, s_new = _delta(r3(a_in), r3(z), r3(bg), hist8, s0, conv_w8, alog_row, dtb_row, onorm_a,
                             chunk, n_chunks)
        k16, v16 = r3(k16), r3(v16)
        if k_old is not None:
            k16 = jnp.concatenate([k_old.reshape(b, -1, SB_WIDTH).astype(BF16), k16], axis=1)
            v16 = jnp.concatenate([v_old.reshape(b, -1, SB_WIDTH).astype(BF16), v16], axis=1)
        n_pad = (-k16.shape[1]) % KEY_TILE
        k16 = jnp.pad(k16, ((0, 0), (n_pad, 0), (0, 0)))
        v16 = jnp.pad(v16, ((0, 0), (n_pad, 0), (0, 0)))
        ob16 = _sb_attn(r3(q16), k16, v16, onorm_b2, bq, n_sub, n_pad)
        x1, h2, route, cnt = _post(oa16.reshape(b * t, 512), ob16.reshape(b * t, 512), x2d, mod8,
                                   g_post_mix, g_pre_ffn, w_out16, wr_hi, wr_lo, b_r, cnt0, tm, seq_rows,
                                   mod_row0)
        new_conv = r3(a_in)[:, t - (CONV_W - 1):, :]
        return (x1, h2, route, cnt, kb.reshape(b, t, SB_HEADS, SB_DIM), vb.reshape(b, t, SB_HEADS, SB_DIM),
                s_new, new_conv)

    zero_hist = jnp.zeros((bp, 8, CONV_CH), F32)
    zero_s = jnp.zeros((bp, DN_HEADS, DN_DIM, DN_DIM), F32)
    hist_s = jnp.pad(conv_s, ((0, 0), (8 - (CONV_W - 1), 0), (0, 0)))
    tm_p = min(256, tp)
    nc_p = max(1, min(4, tp // DELTA_BLOCK))
    x1p, h2p, rp, cnt_p, kp, vp, sp, cp = mixer(x_p, tm_p, tp, 0, zero_hist, zero_s, None, None,
                                                 min(DELTA_BLOCK, tp), nc_p, min(KEY_TILE // 2, tp), 2,
                                                 jnp.zeros((1, LANES), F32))
    x1s, h2s, rs, cnt, ks, vs, ss, cs = mixer(x_s, ts, ts, bp, hist_s, s0_s, k_past, v_past,
                                               min(DELTA_BLOCK, ts), max(1, ts // DELTA_BLOCK),
                                               min(KEY_TILE, ts), 1, cnt_p)

    n_blocks = -(-2 * n_tok // MOE_BLOCK) + N_EXPERTS
    blk_e, nvalid, seg, pad_start = _segment_plan(cnt, n_blocks)
    dest_p = _token_rows(rp, pad_start)
    dest_s = _token_rows(rs, pad_start)
    x_sorted = _dispatch(seg, dest_p, dest_s, h2p, h2s, n_blocks, tm_p)
    y_sorted = _moe(blk_e, nvalid, x_sorted, p['w_gate'], p['w_up'], p['w_down'])
    y_p = _combine(dest_p, rp, y_sorted, x1p, mod8, g_post_ffn, tm_p, tp, 0).reshape(bp, tp, d)
    y_s = _combine(dest_s, rs, y_sorted, x1s, mod8, g_post_ffn, ts, ts, bp).reshape(bs, ts, d)
    return y_p, y_s, kp, vp, sp, cp, ks, vs, ss, cs


def kernel(x_prompt, x_sample, c_prompt, c_sample, cache_k, cache_v, state_delta, state_conv, w_ada, b_ada, g_pre_mix, g_post_mix, g_pre_ffn, g_post_ffn, w_in, conv_w, a_log, dt_bias, onorm_a, onorm_b, w_out, w_router_group, b_router_group, w_router_expert, b_router_expert, w_gate, w_up, w_down):
    depth = w_in.shape[0]
    y_p, y_s = x_prompt, x_sample
    outs = [[] for _ in range(8)]
    for l in range(depth):
        p = dict(w_ada=w_ada[l], b_ada=b_ada[l], g_pre_mix=g_pre_mix[l], g_post_mix=g_post_mix[l],
                 g_pre_ffn=g_pre_ffn[l], g_post_ffn=g_post_ffn[l], w_in=w_in[l], conv_w=conv_w[l],
                 a_log=a_log[l], dt_bias=dt_bias[l], onorm_a=onorm_a[l], onorm_b=onorm_b[l],
                 w_out=w_out[l], w_router_group=w_router_group[l], b_router_group=b_router_group[l],
                 w_router_expert=w_router_expert[l], b_router_expert=b_router_expert[l],
                 w_gate=w_gate[l], w_up=w_up[l], w_down=w_down[l])
        res = _layer(y_p, y_s, c_prompt, c_sample, cache_k[l], cache_v[l], state_delta[l], state_conv[l], p)
        y_p, y_s = res[0], res[1]
        for lst, r in zip(outs, res[2:]):
            lst.append(r)
    return (y_p, y_s) + tuple(jnp.stack(o) for o in outs)
```

```python
import functools
import math

import jax
import jax.numpy as jnp
from jax import lax
from jax.experimental import pallas as pl
from jax.experimental.pallas import tpu as pltpu

F32 = jnp.float32
BF16 = jnp.bfloat16

D_MODEL = 1024
DN_HEADS = 4
DN_DIM = 128
CONV_W = 4
CONV_CH = DN_HEADS * 3 * DN_DIM
DELTA_BLOCK = 64
SB_HEADS = 8
SB_DIM = 64
SB_WIDTH = SB_HEADS * SB_DIM
N_GROUPS = 4
EXPERTS_PER_GROUP = 8
N_EXPERTS = N_GROUPS * EXPERTS_PER_GROUP
D_EXPERT = D_MODEL // 2
MOE_BLOCK = 256
EPS = 1e-6

LANES = 128
KEY_TILE = 128
ATTN_WINDOW = 3 * KEY_TILE
WINDOW_BLOCKS = 4
EXP_ZERO_BELOW = -104.0
VMEM_LIMIT = 56 * 1024 * 1024


def _cparams(sem):
    return pltpu.CompilerParams(dimension_semantics=sem, vmem_limit_bytes=VMEM_LIMIT)


def _split(a):
    hi = a.astype(BF16)
    lo = (a - hi.astype(F32)).astype(BF16)
    return hi, lo


def _dot(a, b, dims=(((1,), (0,)), ((), ()))):
    return lax.dot_general(a, b, dims, preferred_element_type=F32)


def _dot3(a, b, dims=(((1,), (0,)), ((), ()))):
    ah, al = _split(a)
    bh, bl = _split(b)
    return _dot(ah, bh, dims) + (_dot(al, bh, dims) + _dot(ah, bl, dims))


_NT = (((1,), (1,)), ((), ()))
_TN = (((0,), (0,)), ((), ()))


def _silu(x):
    return x * jax.nn.sigmoid(x)


def _softplus(x):
    return jnp.maximum(x, 0.0) + jnp.log(1.0 + jnp.exp(-jnp.abs(x)))


def _ada_kernel(c_ref, w_ref, b_ref, o_ref):
    s = _silu(c_ref[...]).astype(BF16)
    o_ref[...] = _dot(s, w_ref[...].astype(BF16)) + b_ref[...]


def _ada(c_all, w_ada, b_ada):
    rows = c_all.shape[0]
    n = w_ada.shape[1]
    tn = 1024
    return pl.pallas_call(
        _ada_kernel,
        grid=(n // tn,),
        in_specs=[pl.BlockSpec((rows, D_MODEL), lambda j: (0, 0)),
                  pl.BlockSpec((D_MODEL, tn), lambda j: (0, j)),
                  pl.BlockSpec((1, tn), lambda j: (0, j))],
        out_specs=pl.BlockSpec((rows, tn), lambda j: (0, j)),
        out_shape=jax.ShapeDtypeStruct((rows, n), F32),
        compiler_params=_cparams(("arbitrary",)),
        name="ada",
    )(c_all, w_ada, b_ada.reshape(1, n))


def _rms(x, gain):
    return x * lax.rsqrt(jnp.mean(x * x, axis=-1, keepdims=True) + EPS) * gain


def _proj_kernel(x_ref, mod_ref, g_ref, wm_ref, wbh_ref, wbl_ref,
                 a_ref, z_ref, bg_ref, q_ref, k_ref, v_ref, k16_ref, v16_ref):
    mod = mod_ref[0]
    h = _rms(x_ref[...], g_ref[...]) * (1.0 + mod[1:2]) + mod[0:1]
    hh, hl = _split(h)
    p = _dot(hh, wm_ref[...])
    a_ref[...] = p[:, 0:CONV_CH]
    z_ref[...] = p[:, CONV_CH:CONV_CH + 512]
    o = CONV_CH + 512
    q_ref[...] = (p[:, o:o + 512] * (SB_DIM ** -0.5)).astype(BF16)
    k = p[:, o + 512:o + 1024]
    v = p[:, o + 1024:o + 1536]
    k_ref[...] = k
    v_ref[...] = v
    k16_ref[...] = k.astype(BF16)
    v16_ref[...] = v.astype(BF16)
    wbh = wbh_ref[...]
    bg_ref[...] = _dot(hh, wbh) + (_dot(hl, wbh) + _dot(hh, wbl_ref[...]))


def _proj(x2d, mod8, g_pre, w_main, wb_hi, wb_lo, tm, seq_rows, mod_row0):
    n = x2d.shape[0]
    nm = w_main.shape[1]
    row = lambda i: (i, 0)
    const = lambda i: (0, 0)
    modmap = lambda i: (mod_row0 + (i * tm) // seq_rows, 0, 0)
    outs = [(CONV_CH, F32), (512, F32), (LANES, F32), (512, BF16), (512, F32), (512, F32),
            (512, BF16), (512, BF16)]
    return pl.pallas_call(
        _proj_kernel,
        grid=(n // tm,),
        in_specs=[pl.BlockSpec((tm, D_MODEL), row),
                  pl.BlockSpec((1, 8, D_MODEL), modmap),
                  pl.BlockSpec((1, D_MODEL), const),
                  pl.BlockSpec((D_MODEL, nm), const),
                  pl.BlockSpec((D_MODEL, LANES), const),
                  pl.BlockSpec((D_MODEL, LANES), const)],
        out_specs=[pl.BlockSpec((tm, w), row) for w, _ in outs],
        out_shape=[jax.ShapeDtypeStruct((n, w), dt) for w, dt in outs],
        compiler_params=_cparams(("arbitrary",)),
        name="proj",
    )(x2d, mod8, g_pre, w_main, wb_hi, wb_lo)


def _delta_kernel(a_ref, z_ref, bg_ref, hist0_ref, s0_ref, cw_ref, alog_ref, dtb_ref, on_ref,
                  o_ref, sfin_ref, hist_sc, s_sc, *, chunk, n_chunks):
    t_idx = pl.program_id(1)
    tt = chunk * n_chunks

    @pl.when(t_idx == 0)
    def _():
        hist_sc[...] = hist0_ref[0]
        s_sc[...] = s0_ref[0]

    x = a_ref[0]
    xx = jnp.concatenate([hist_sc[...], x], axis=0)
    cw = cw_ref[...]
    conv = x * cw[CONV_W - 1:CONV_W]
    for s in range(1, CONV_W):
        conv = conv + pltpu.roll(xx, s, 0)[8:] * cw[CONV_W - 1 - s:CONV_W - s]
    conv = _silu(conv)
    hist_sc[...] = x[tt - 8:tt]

    bg = bg_ref[0]
    lane = lax.broadcasted_iota(jnp.int32, (1, LANES), 1)
    g_lane = (lane >= DN_HEADS) & (lane < 2 * DN_HEADS)
    neg_a = jnp.where(g_lane, -jnp.exp(alog_ref[...]), 0.0)
    beta_all = jax.nn.sigmoid(bg)
    g_all = neg_a * _softplus(bg + dtb_ref[...])

    ri = lax.broadcasted_iota(jnp.int32, (chunk, chunk), 0)
    ci = lax.broadcasted_iota(jnp.int32, (chunk, chunk), 1)
    incl = ri >= ci
    strict = ri > ci
    tri = incl.astype(F32)
    eye = (ri == ci).astype(F32)
    ones_cc = jnp.ones((chunk, chunk), F32)
    onorm = on_ref[...]

    chunks = range(n_chunks)
    units = [(c, h) for c in chunks for h in range(DN_HEADS)]
    rows = lambda c: slice(c * chunk, (c + 1) * chunk)

    gc_all = [_dot3(tri, g_all[rows(c)]) for c in chunks]
    diag = [jnp.concatenate([eye * gc_all[c][:, DN_HEADS + h:DN_HEADS + h + 1] for h in range(DN_HEADS)], axis=1)
            for c in chunks]
    gc_rows = [_dot3(ones_cc, diag[c]) for c in chunks]

    q, k, v, beta, gcol, glast, eg, decay, kbeta = {}, {}, {}, {}, {}, {}, {}, {}, {}
    for (c, h) in units:
        lo = h * DN_DIM
        qq = conv[rows(c), lo:lo + DN_DIM]
        kk = conv[rows(c), 512 + lo:512 + lo + DN_DIM]
        u_ = (c, h)
        q[u_] = qq * lax.rsqrt(jnp.sum(qq * qq, axis=-1, keepdims=True) + EPS) * (DN_DIM ** -0.5)
        k[u_] = kk * lax.rsqrt(jnp.sum(kk * kk, axis=-1, keepdims=True) + EPS)
        v[u_] = conv[rows(c), 1024 + lo:1024 + lo + DN_DIM]
        beta[u_] = beta_all[rows(c), h:h + 1]
        gcol[u_] = gc_all[c][:, DN_HEADS + h:DN_HEADS + h + 1]
        glast[u_] = gcol[u_][chunk - 1:chunk, :]
        eg[u_] = jnp.exp(gcol[u_])
        grow = gc_rows[c][:, h * chunk:(h + 1) * chunk]
        decay[u_] = jnp.where(incl, jnp.exp(jnp.minimum(gcol[u_] - grow, 0.0)), 0.0)
        kbeta[u_] = k[u_] * beta[u_]

    k16 = {u_: k[u_].astype(BF16) for u_ in units}
    kq = {u_: _dot(jnp.concatenate([kbeta[u_], q[u_]], axis=0).astype(BF16), k16[u_], _NT) for u_ in units}
    lower = {u_: jnp.where(strict, kq[u_][:chunk] * decay[u_], 0.0) for u_ in units}
    attn16 = {u_: jnp.where(incl, kq[u_][chunk:] * decay[u_], 0.0).astype(BF16) for u_ in units}

    sol = {u_: jnp.concatenate([v[u_] * beta[u_], kbeta[u_] * eg[u_]], axis=1) for u_ in units}
    lp = lower
    p = 1
    while p < chunk:
        lsp = {u_: _split(lp[u_]) for u_ in units}
        ssp = {u_: _split(sol[u_]) for u_ in units}
        stack = {u_: jnp.concatenate(lsp[u_], axis=0) for u_ in units}
        p1 = {u_: _dot(stack[u_], ssp[u_][0]) for u_ in units}
        p2 = {u_: _dot(lsp[u_][0], ssp[u_][1]) for u_ in units}
        upd = {u_: p1[u_][:chunk] + (p1[u_][chunk:] + p2[u_]) for u_ in units}
        sol = {u_: (sol[u_] - upd[u_]) if p == 1 else (sol[u_] + upd[u_]) for u_ in units}
        if 2 * p < chunk:
            q1 = {u_: _dot(stack[u_], lsp[u_][0]) for u_ in units}
            q2 = {u_: _dot(lsp[u_][0], lsp[u_][1]) for u_ in units}
            lp = {u_: q1[u_][:chunk] + (q1[u_][chunk:] + q2[u_]) for u_ in units}
        p *= 2

    usol = {u_: sol[u_][:, :DN_DIM] for u_ in units}
    wq16 = {u_: jnp.concatenate([sol[u_][:, DN_DIM:], q[u_] * eg[u_]], axis=0).astype(BF16) for u_ in units}
    kd16 = {u_: (k[u_] * jnp.exp(glast[u_] - gcol[u_])).astype(BF16) for u_ in units}

    for c in chunks:
        hs = range(DN_HEADS)
        s_old = [s_sc[h] for h in hs]
        ws = [_dot(wq16[(c, h)], s_old[h].astype(BF16)) for h in hs]
        v_new = [(usol[(c, h)] - ws[h][:chunk]).astype(BF16) for h in hs]
        o_in = [_dot(attn16[(c, h)], v_new[h]) for h in hs]
        ds = [_dot(kd16[(c, h)], v_new[h], _TN) for h in hs]
        for h in hs:
            lo = h * DN_DIM
            s_sc[h] = s_old[h] * jnp.exp(glast[(c, h)]) + ds[h]
            o = ws[h][chunk:] + o_in[h]
            zg = z_ref[0, rows(c), lo:lo + DN_DIM]
            o_ref[0, rows(c), lo:lo + DN_DIM] = (_rms(o, onorm) * _silu(zg)).astype(BF16)

    @pl.when(t_idx == pl.num_programs(1) - 1)
    def _():
        sfin_ref[0] = s_sc[...]


def _delta(a_in, z, bg, hist8, s0, conv_w8, alog_row, dtb_row, onorm_a, chunk, n_chunks):
    b, t, _ = a_in.shape
    tt = chunk * n_chunks
    tile = lambda bi, ti: (bi, ti, 0)
    per_b3 = lambda bi, ti: (bi, 0, 0)
    per_b4 = lambda bi, ti: (bi, 0, 0, 0)
    const = lambda bi, ti: (0, 0)
    kern = functools.partial(_delta_kernel, chunk=chunk, n_chunks=n_chunks)
    return pl.pallas_call(
        kern,
        grid=(b, t // tt),
        in_specs=[pl.BlockSpec((1, tt, CONV_CH), tile),
                  pl.BlockSpec((1, tt, 512), tile),
                  pl.BlockSpec((1, tt, LANES), tile),
                  pl.BlockSpec((1, 8, CONV_CH), per_b3),
                  pl.BlockSpec((1, DN_HEADS, DN_DIM, DN_DIM), per_b4),
                  pl.BlockSpec((8, CONV_CH), const),
                  pl.BlockSpec((1, LANES), const),
                  pl.BlockSpec((1, LANES), const),
                  pl.BlockSpec((1, DN_DIM), const)],
        out_specs=[pl.BlockSpec((1, tt, 512), tile),
                   pl.BlockSpec((1, DN_HEADS, DN_DIM, DN_DIM), per_b4)],
        out_shape=[jax.ShapeDtypeStruct((b, t, 512), BF16),
                   jax.ShapeDtypeStruct((b, DN_HEADS, DN_DIM, DN_DIM), F32)],
        scratch_shapes=[pltpu.VMEM((8, CONV_CH), F32),
                        pltpu.VMEM((DN_HEADS, DN_DIM, DN_DIM), F32)],
        compiler_params=_cparams(("arbitrary", "arbitrary")),
        name="delta",
    )(a_in, z, bg, hist8, s0, conv_w8, alog_row, dtb_row, onorm_a)


def _sb_kernel(q_ref, *refs, bq, n_sub, n_pad, q_off):
    kwin = refs[:WINDOW_BLOCKS]
    vwin = refs[WINDOW_BLOCKS:2 * WINDOW_BLOCKS]
    k_hbm, v_hbm, on_ref, o_ref, kbuf, vbuf, qsel, acc, carry, sem = refs[2 * WINDOW_BLOCKS:]
    b = pl.program_id(0)
    i = pl.program_id(1)
    g = bq * n_sub
    qend_step = q_off + (i + 1) * g
    win_start = qend_step - WINDOW_BLOCKS * KEY_TILE
    n_pairs = SB_HEADS // 2
    heads = range(SB_HEADS)

    half_lane = lax.broadcasted_iota(jnp.int32, (bq, LANES), 1) < SB_DIM
    rj = lax.broadcasted_iota(jnp.int32, (2 * KEY_TILE, 2 * KEY_TILE), 0) % KEY_TILE
    cj = lax.broadcasted_iota(jnp.int32, (2 * KEY_TILE, 2 * KEY_TILE), 1)
    suffix2 = ((rj > cj) | (cj >= KEY_TILE)).astype(BF16)

    def window(blocks, off, p):
        parts = []
        for blk in range(WINDOW_BLOCKS):
            lo, hi = max(off, blk * KEY_TILE), min(off + ATTN_WINDOW, (blk + 1) * KEY_TILE)
            if lo < hi:
                ref = blocks[WINDOW_BLOCKS - 1 - blk]
                parts.append(ref[0, lo - blk * KEY_TILE:hi - blk * KEY_TILE, p * LANES:(p + 1) * LANES])
        return jnp.concatenate(parts, axis=0)

    n_t = ATTN_WINDOW // KEY_TILE
    row = lax.broadcasted_iota(jnp.int32, (bq, ATTN_WINDOW), 0)
    col = lax.broadcasted_iota(jnp.int32, (bq, ATTN_WINDOW), 1)
    causal = col < row + (ATTN_WINDOW - bq)
    subs = range(n_sub)
    offs = [WINDOW_BLOCKS * KEY_TILE - g + (s + 1) * bq - ATTN_WINDOW for s in subs]
    done = []
    for s in subs:
        mask = causal & (col >= n_pad - (win_start + offs[s]))
        zs = {}
        for p in range(n_pairs):
            qf = q_ref[0, s * bq:(s + 1) * bq, p * LANES:(p + 1) * LANES].astype(F32)
            qq = jnp.concatenate([jnp.where(half_lane, qf, 0.0), jnp.where(half_lane, 0.0, qf)], axis=0)
            zz = _dot(qq.astype(BF16), window(kwin, offs[s], p), _NT)
            zs[2 * p], zs[2 * p + 1] = zz[:bq], zz[bq:]
        sp = {h: _softplus(zs[h]) for h in heads}
        pieces = []
        for h in heads:
            hi, lo = _split(jnp.where(mask, -sp[h], 0.0))
            for j in range(n_t):
                c0 = ATTN_WINDOW - (j + 1) * KEY_TILE
                pieces.append(jnp.concatenate([hi[:, c0:c0 + KEY_TILE], lo[:, c0:c0 + KEY_TILE]], axis=1))
        cs_all = _dot(jnp.concatenate(pieces, axis=0), suffix2)
        worst = jnp.full((bq, KEY_TILE), -jnp.inf, F32)
        a = {}
        for h in heads:
            run = None
            cols = []
            for j in range(n_t):
                r0 = (h * n_t + j) * bq
                cs = cs_all[r0:r0 + bq]
                cols.append(cs[:, :KEY_TILE] if run is None else cs[:, :KEY_TILE] + run)
                run = cs[:, KEY_TILE:] if run is None else run + cs[:, KEY_TILE:]
            within = jnp.concatenate(cols[::-1], axis=1)
            a[h] = jnp.where(mask, jnp.exp((zs[h] - sp[h]) + within), 0.0).astype(BF16)
            carry[s, h] = run
            worst = jnp.maximum(worst, run)
        for p in range(n_pairs):
            pv = _dot(jnp.concatenate([a[2 * p], a[2 * p + 1]], axis=0), window(vwin, offs[s], p))
            acc[s, p] = jnp.where(half_lane, pv[:bq], pv[bq:])
        done.append((jnp.max(worst) < EXP_ZERO_BELOW).astype(jnp.int32))

    rj1 = lax.broadcasted_iota(jnp.int32, (KEY_TILE, 2 * KEY_TILE), 0)
    cj1 = lax.broadcasted_iota(jnp.int32, (KEY_TILE, 2 * KEY_TILE), 1)
    suffix1 = ((rj1 > cj1) | (cj1 >= KEY_TILE)).astype(BF16)
    col1 = lax.broadcasted_iota(jnp.int32, (bq, KEY_TILE), 1)
    for s in subs:
        swept_from = win_start + offs[s]

        @pl.when((done[s] == 0) & (swept_from > n_pad))
        def _():
            for p in range(n_pairs):
                qf = q_ref[0, s * bq:(s + 1) * bq, p * LANES:(p + 1) * LANES].astype(F32)
                qsel[2 * p] = jnp.where(half_lane, qf, 0.0).astype(BF16)
                qsel[2 * p + 1] = jnp.where(half_lane, 0.0, qf).astype(BF16)

            def body(state):
                upper, _ = state
                start = jnp.maximum(upper - KEY_TILE, 0)
                copies = []
                for p in range(n_pairs):
                    for src, dst in ((k_hbm, kbuf), (v_hbm, vbuf)):
                        cp = pltpu.make_async_copy(
                            src.at[b, pl.ds(pl.multiple_of(start, 16), KEY_TILE), pl.ds(p * LANES, LANES)],
                            dst.at[p], sem)
                        cp.start()
                        copies.append(cp)
                for cp in copies:
                    cp.wait()
                kpos = start + col1
                m1 = (kpos < upper) & (kpos >= n_pad)
                zz = [_dot(qsel[h], kbuf[h // 2], _NT) for h in heads]
                spp = [_softplus(zz[h]) for h in heads]
                pcs = []
                for h in heads:
                    pcs.extend(_split(jnp.where(m1, -spp[h], 0.0)))
                cs1 = _dot(jnp.concatenate(pcs, axis=0), suffix1)
                worst = jnp.full((bq, KEY_TILE), -jnp.inf, F32)
                aa = []
                for h in heads:
                    c = cs1[2 * h * bq:(2 * h + 1) * bq] + cs1[(2 * h + 1) * bq:(2 * h + 2) * bq]
                    c_old = carry[s, h]
                    aa.append(jnp.where(m1, jnp.exp((zz[h] - spp[h]) + c[:, :KEY_TILE] + c_old), 0.0).astype(BF16))
                    c_new = c_old + c[:, KEY_TILE:]
                    carry[s, h] = c_new
                    worst = jnp.maximum(worst, c_new)
                for p in range(n_pairs):
                    pv0 = _dot(aa[2 * p], vbuf[p])
                    pv1 = _dot(aa[2 * p + 1], vbuf[p])
                    acc[s, p] = acc[s, p] + jnp.where(half_lane, pv0, pv1)
                return start, (jnp.max(worst) < EXP_ZERO_BELOW).astype(jnp.int32)

            lax.while_loop(lambda st: (st[0] > n_pad) & (st[1] == 0), body, (swept_from, jnp.int32(0)))

    onb = on_ref[...]
    for s in subs:
        for p in range(n_pairs):
            o = acc[s, p]
            sq = o * o
            s_lo = jnp.sum(jnp.where(half_lane, sq, 0.0), axis=-1, keepdims=True)
            s_hi = jnp.sum(jnp.where(half_lane, 0.0, sq), axis=-1, keepdims=True)
            ms = jnp.where(half_lane, s_lo, s_hi) * (1.0 / SB_DIM)
            o_ref[0, s * bq:(s + 1) * bq, p * LANES:(p + 1) * LANES] = (o * lax.rsqrt(ms + EPS) * onb).astype(BF16)


def _sb_attn(q16, k16p, v16p, onorm_b2, bq, n_sub, n_pad):
    b, tq, _ = q16.shape
    tkp = k16p.shape[1]
    q_off = tkp - tq
    g = bq * n_sub
    assert tq % g == 0 and g <= KEY_TILE and bq % 16 == 0
    assert all((q_off + (i + 1) * g) % KEY_TILE == 0 for i in range(tq // g))

    def kmap(back):
        def f(bi, i):
            last = (q_off + (i + 1) * g) // KEY_TILE - 1
            return (bi, jnp.maximum(last - back, 0), 0)
        return f

    qmap = lambda bi, i: (bi, i, 0)
    kern = functools.partial(_sb_kernel, bq=bq, n_sub=n_sub, n_pad=n_pad, q_off=q_off)
    kspec = [pl.BlockSpec((1, KEY_TILE, SB_WIDTH), kmap(back)) for back in range(WINDOW_BLOCKS)]
    return pl.pallas_call(
        kern,
        grid=(b, tq // g),
        in_specs=[pl.BlockSpec((1, g, SB_WIDTH), qmap)] + kspec + kspec
                 + [pl.BlockSpec(memory_space=pl.ANY), pl.BlockSpec(memory_space=pl.ANY),
                    pl.BlockSpec((1, LANES), lambda bi, i: (0, 0))],
        out_specs=pl.BlockSpec((1, g, SB_WIDTH), qmap),
        out_shape=jax.ShapeDtypeStruct((b, tq, SB_WIDTH), BF16),
        scratch_shapes=[pltpu.VMEM((SB_HEADS // 2, KEY_TILE, LANES), BF16),
                        pltpu.VMEM((SB_HEADS // 2, KEY_TILE, LANES), BF16),
                        pltpu.VMEM((SB_HEADS, bq, LANES), BF16),
                        pltpu.VMEM((n_sub, SB_HEADS // 2, bq, LANES), F32),
                        pltpu.VMEM((n_sub, SB_HEADS, bq, KEY_TILE), F32),
                        pltpu.SemaphoreType.DMA(())],
        compiler_params=_cparams(("arbitrary", "arbitrary")),
        name="sb_attn",
    )(q16, *([k16p] * WINDOW_BLOCKS), *([v16p] * WINDOW_BLOCKS), k16p, v16p, onorm_b2)


def _post_kernel(oa_ref, ob_ref, x_ref, mod_ref, gpm_ref, gpf_ref, wo_ref, wrh_ref, wrl_ref, br_ref, cnt0_ref,
                 x1_ref, h2_ref, route_ref, cnt_ref):
    @pl.when(pl.program_id(0) == 0)
    def _():
        cnt_ref[...] = cnt0_ref[...]

    mod = mod_ref[0]
    mix = _dot(oa_ref[...], wo_ref[0:512, :]) + _dot(ob_ref[...], wo_ref[512:1024, :])
    x1 = x_ref[...] + mod[2:3] * _rms(mix, gpm_ref[...])
    x1_ref[...] = x1
    h2 = _rms(x1, gpf_ref[...]) * (1.0 + mod[4:5]) + mod[3:4]
    h2_ref[...] = h2
    hh, hl = _split(h2)
    wrh = wrh_ref[...]
    logits = _dot(hh, wrh) + (_dot(hl, wrh) + _dot(hh, wrl_ref[...])) + br_ref[...]
    lane = lax.broadcasted_iota(jnp.int32, logits.shape, 1).astype(F32)
    neg = -jnp.inf
    nl = float(LANES)
    lg = jnp.where(lane < N_GROUPS, logits, neg)
    gmax = jnp.max(lg, axis=-1, keepdims=True)
    grp = jnp.min(jnp.where(lg == gmax, lane, nl), axis=-1, keepdims=True)
    p_grp = 1.0 / jnp.sum(jnp.exp(lg - gmax), axis=-1, keepdims=True)
    first = N_GROUPS + grp * EXPERTS_PER_GROUP
    le = jnp.where((lane >= first) & (lane < first + EXPERTS_PER_GROUP), logits, neg)
    emax = jnp.max(le, axis=-1, keepdims=True)
    i1 = jnp.min(jnp.where(le == emax, lane, nl), axis=-1, keepdims=True)
    esum = jnp.sum(jnp.exp(le - emax), axis=-1, keepdims=True)
    le2 = jnp.where(lane == i1, neg, le)
    e2max = jnp.max(le2, axis=-1, keepdims=True)
    i2 = jnp.min(jnp.where(le2 == e2max, lane, nl), axis=-1, keepdims=True)
    p1 = 1.0 / esum
    p2 = jnp.exp(e2max - emax) / esum
    w1 = p_grp * p1 / (p1 + p2)
    w2 = p_grp * p2 / (p1 + p2)
    e1 = i1 - N_GROUPS
    e2 = i2 - N_GROUPS
    hot1 = (lane == e1).astype(F32)
    hot2 = (lane == e2).astype(F32)
    both = hot1 + hot2
    tm = logits.shape[0]
    ti = lax.broadcasted_iota(jnp.int32, (tm, tm), 0)
    tj = lax.broadcasted_iota(jnp.int32, (tm, tm), 1)
    earlier = _dot((ti > tj).astype(BF16), both.astype(BF16)) + cnt_ref[...]
    rank1 = jnp.sum(hot1 * earlier, axis=-1, keepdims=True)
    rank2 = jnp.sum(hot2 * (earlier + hot1), axis=-1, keepdims=True)
    cnt_ref[...] = cnt_ref[...] + jnp.sum(both, axis=0, keepdims=True)
    out = jnp.where(lane == 0.0, e1, 0.0)
    out = jnp.where(lane == 1.0, e2, out)
    out = jnp.where(lane == 2.0, w1, out)
    out = jnp.where(lane == 3.0, w2, out)
    out = jnp.where(lane == 4.0, rank1, out)
    out = jnp.where(lane == 5.0, rank2, out)
    route_ref[...] = out


def _post(oa16, ob16, x2d, mod8, g_post_mix, g_pre_ffn, w_out16, wr_hi, wr_lo, b_r, cnt0, tm, seq_rows,
          mod_row0):
    n = x2d.shape[0]
    row = lambda i: (i, 0)
    const = lambda i: (0, 0)
    modmap = lambda i: (mod_row0 + (i * tm) // seq_rows, 0, 0)
    return pl.pallas_call(
        _post_kernel,
        grid=(n // tm,),
        in_specs=[pl.BlockSpec((tm, 512), row),
                  pl.BlockSpec((tm, 512), row),
                  pl.BlockSpec((tm, D_MODEL), row),
                  pl.BlockSpec((1, 8, D_MODEL), modmap),
                  pl.BlockSpec((1, D_MODEL), const),
                  pl.BlockSpec((1, D_MODEL), const),
                  pl.BlockSpec((D_MODEL, D_MODEL), const),
                  pl.BlockSpec((D_MODEL, LANES), const),
                  pl.BlockSpec((D_MODEL, LANES), const),
                  pl.BlockSpec((1, LANES), const),
                  pl.BlockSpec((1, LANES), const)],
        out_specs=[pl.BlockSpec((tm, D_MODEL), row),
                   pl.BlockSpec((tm, D_MODEL), row),
                   pl.BlockSpec((tm, LANES), row),
                   pl.BlockSpec((1, LANES), const)],
        out_shape=[jax.ShapeDtypeStruct((n, D_MODEL), F32),
                   jax.ShapeDtypeStruct((n, D_MODEL), F32),
                   jax.ShapeDtypeStruct((n, LANES), F32),
                   jax.ShapeDtypeStruct((1, LANES), F32)],
        compiler_params=_cparams(("arbitrary",)),
        name="post",
    )(oa16, ob16, x2d, mod8, g_post_mix, g_pre_ffn, w_out16, wr_hi, wr_lo, b_r, cnt0)


def _dispatch_kernel(seg_ref, dp_ref, ds_ref, hp_ref, hs_ref, xs_hbm, zbuf, sem, *, n_blocks):
    i = pl.program_id(0)
    last_step = pl.num_programs(0) - 1

    @pl.when(i == 0)
    def _():
        zbuf[...] = jnp.zeros_like(zbuf)

        def zero_block(row0):
            return pltpu.make_async_copy(zbuf, xs_hbm.at[pl.ds(pl.multiple_of(row0, MOE_BLOCK), MOE_BLOCK), :], sem)

        for e in range(N_EXPERTS):
            @pl.when(seg_ref[e] > 0)
            def _():
                zero_block(seg_ref[N_EXPERTS + e] - MOE_BLOCK).start()
        for e in range(N_EXPERTS):
            @pl.when(seg_ref[e] > 0)
            def _():
                zero_block(seg_ref[N_EXPERTS + e] - MOE_BLOCK).wait()

        used = seg_ref[2 * N_EXPERTS - 1] // MOE_BLOCK

        def fill(b, c):
            cp = zero_block(b * MOE_BLOCK)
            cp.start()
            cp.wait()
            return c

        lax.fori_loop(used, n_blocks, fill, 0)

    def scatter(h_ref, dest_ref):
        rows = h_ref.shape[0]
        for t in range(rows):
            for slot in range(2):
                pltpu.make_async_copy(h_ref.at[pl.ds(t, 1), :],
                                      xs_hbm.at[pl.ds(dest_ref[0, 0, 2 * t + slot], 1), :],
                                      sem).start(priority=slot)
        for slot in range(2):
            pltpu.make_async_copy(h_ref, xs_hbm.at[pl.ds(0, rows), :], sem).wait()

    @pl.when(i < last_step)
    def _():
        scatter(hp_ref, dp_ref)

    @pl.when(i == last_step)
    def _():
        scatter(hs_ref, ds_ref)


def _dispatch(seg, dest_p, dest_s, h2p, h2s, n_blocks, tm):
    n_p, n_s = h2p.shape[0], h2s.shape[0]
    steps_p = n_p // tm
    pmap3 = lambda i, sg: (jnp.minimum(i, steps_p - 1), 0, 0)
    pmap2 = lambda i, sg: (jnp.minimum(i, steps_p - 1), 0)
    grid_spec = pltpu.PrefetchScalarGridSpec(
        num_scalar_prefetch=1,
        grid=(steps_p + 1,),
        in_specs=[pl.BlockSpec((1, 1, 2 * tm), pmap3, memory_space=pltpu.SMEM),
                  pl.BlockSpec((1, 1, 2 * n_s), lambda i, sg: (0, 0, 0), memory_space=pltpu.SMEM),
                  pl.BlockSpec((tm, D_MODEL), pmap2),
                  pl.BlockSpec((n_s, D_MODEL), lambda i, sg: (0, 0))],
        out_specs=pl.BlockSpec(memory_space=pl.ANY),
        scratch_shapes=[pltpu.VMEM((MOE_BLOCK, D_MODEL), F32), pltpu.SemaphoreType.DMA(())])
    return pl.pallas_call(
        functools.partial(_dispatch_kernel, n_blocks=n_blocks),
        grid_spec=grid_spec,
        out_shape=jax.ShapeDtypeStruct((n_blocks * MOE_BLOCK, D_MODEL), F32),
        compiler_params=_cparams(("arbitrary",)),
        name="dispatch",
    )(seg, dest_p.reshape(steps_p, 1, 2 * tm), dest_s.reshape(1, 1, 2 * n_s), h2p, h2s)


def _moe_kernel(blk_e_ref, nvalid_ref, x_ref, wg_ref, wu_ref, wd_ref, y_ref, wg16, wu16, wd16):
    i = pl.program_id(0)
    e = blk_e_ref[i]
    e_prev = blk_e_ref[jnp.maximum(i - 1, 0)]

    @pl.when((i == 0) | (e != e_prev))
    def _():
        wg16[...] = wg_ref[0].astype(BF16)
        wu16[...] = wu_ref[0].astype(BF16)
        wd16[...] = wd_ref[0].astype(BF16)

    @pl.when(nvalid_ref[i] > 0)
    def _():
        xb = x_ref[...].astype(BF16)
        g = _dot(xb, wg16[...])
        u = _dot(xb, wu16[...])
        hmid = (_silu(g) * u).astype(BF16)
        y_ref[...] = _dot(hmid, wd16[...])

    @pl.when(nvalid_ref[i] == 0)
    def _():
        y_ref[...] = jnp.zeros_like(y_ref)


def _moe(blk_e, nvalid, x_sorted, w_gate, w_up, w_down):
    n_blocks = blk_e.shape[0]
    wmap = lambda i, be, nv: (be[i], 0, 0)
    xmap = lambda i, be, nv: (jnp.where(nv[i] > 0, i, 0), 0)
    grid_spec = pltpu.PrefetchScalarGridSpec(
        num_scalar_prefetch=2,
        grid=(n_blocks,),
        in_specs=[pl.BlockSpec((MOE_BLOCK, D_MODEL), xmap),
                  pl.BlockSpec((1, D_MODEL, D_EXPERT), wmap),
                  pl.BlockSpec((1, D_MODEL, D_EXPERT), wmap),
                  pl.BlockSpec((1, D_EXPERT, D_MODEL), wmap)],
        out_specs=pl.BlockSpec((MOE_BLOCK, D_MODEL), lambda i, be, nv: (i, 0)),
        scratch_shapes=[pltpu.VMEM((D_MODEL, D_EXPERT), BF16),
                        pltpu.VMEM((D_MODEL, D_EXPERT), BF16),
                        pltpu.VMEM((D_EXPERT, D_MODEL), BF16)])
    return pl.pallas_call(
        _moe_kernel,
        grid_spec=grid_spec,
        out_shape=jax.ShapeDtypeStruct((n_blocks * MOE_BLOCK, D_MODEL), F32),
        compiler_params=_cparams(("arbitrary",)),
        name="moe",
    )(blk_e, nvalid, x_sorted, w_gate, w_up, w_down)


def _combine_kernel(dest_ref, route_ref, x1_ref, mod_ref, g_ref, y_hbm, o_ref, ybuf, sem):
    tm = x1_ref.shape[0]
    for t in range(tm):
        for slot in range(2):
            pltpu.make_async_copy(y_hbm.at[pl.ds(dest_ref[0, 0, 2 * t + slot], 1), :],
                                  ybuf.at[slot, pl.ds(t, 1), :], sem).start(priority=slot)
    for slot in range(2):
        pltpu.make_async_copy(y_hbm.at[pl.ds(0, tm), :], ybuf.at[slot], sem).wait()
    mod = mod_ref[0]
    route = route_ref[...]
    moe = ybuf[0] * route[:, 2:3] + ybuf[1] * route[:, 3:4]
    o_ref[...] = x1_ref[...] + mod[5:6] * _rms(moe, g_ref[...])


def _combine(dest, route, y_sorted, x1, mod8, g_post_ffn, tm, seq_rows, mod_row0):
    n = x1.shape[0]
    row = lambda i: (i, 0)
    modmap = lambda i: (mod_row0 + (i * tm) // seq_rows, 0, 0)
    return pl.pallas_call(
        _combine_kernel,
        grid=(n // tm,),
        in_specs=[pl.BlockSpec((1, 1, 2 * tm), lambda i: (i, 0, 0), memory_space=pltpu.SMEM),
                  pl.BlockSpec((tm, LANES), row),
                  pl.BlockSpec((tm, D_MODEL), row),
                  pl.BlockSpec((1, 8, D_MODEL), modmap),
                  pl.BlockSpec((1, D_MODEL), lambda i: (0, 0)),
                  pl.BlockSpec(memory_space=pl.ANY)],
        out_specs=pl.BlockSpec((tm, D_MODEL), row),
        out_shape=jax.ShapeDtypeStruct((n, D_MODEL), F32),
        scratch_shapes=[pltpu.VMEM((2, tm, D_MODEL), F32), pltpu.SemaphoreType.DMA(())],
        compiler_params=_cparams(("arbitrary",)),
        name="combine",
    )(dest.reshape(n // tm, 1, 2 * tm), route, x1, mod8, g_post_ffn, y_sorted)


def _segment_plan(counts_f, n_blocks):
    counts = counts_f[0, :N_EXPERTS].astype(jnp.int32)
    padded = (counts + MOE_BLOCK - 1) // MOE_BLOCK * MOE_BLOCK
    pad_end = jnp.cumsum(padded)
    pad_start = pad_end - padded
    blk_start = jnp.arange(n_blocks, dtype=jnp.int32) * MOE_BLOCK
    blk_e = jnp.minimum(jnp.sum((pad_end[None, :] <= blk_start[:, None]).astype(jnp.int32), axis=1),
                        N_EXPERTS - 1)
    onehot = blk_e[:, None] == jnp.arange(N_EXPERTS, dtype=jnp.int32)[None, :]
    c_blk = jnp.sum(jnp.where(onehot, counts[None, :], 0), axis=1)
    s_blk = jnp.sum(jnp.where(onehot, pad_start[None, :], 0), axis=1)
    nvalid = jnp.clip(c_blk - (blk_start - s_blk), 0, MOE_BLOCK).astype(jnp.int32)
    seg = jnp.concatenate([counts, pad_end]).astype(jnp.int32)
    return blk_e.astype(jnp.int32), nvalid, seg, pad_start


def _token_rows(route, pad_start):
    eid = route[:, 0:2].astype(jnp.int32)
    rank = route[:, 4:6].astype(jnp.int32)
    onehot = eid[:, :, None] == jnp.arange(N_EXPERTS, dtype=jnp.int32)[None, None, :]
    return rank + jnp.sum(jnp.where(onehot, pad_start[None, None, :], 0), axis=2)


def _layer(x_p, x_s, c_p, c_s, k_past, v_past, s0_s, conv_s, p):
    bp, tp, d = x_p.shape
    bs, ts, _ = x_s.shape
    n_p, n_s = bp * tp, bs * ts
    n_tok = n_p + n_s

    n_seq = bp + bs
    c_all = jnp.zeros((16, d), F32).at[:n_seq].set(jnp.concatenate([c_p, c_s], axis=0))
    mod = _ada(c_all, p['w_ada'], p['b_ada'])
    mod8 = jnp.pad(mod.reshape(16, 6, d), ((0, 0), (0, 2), (0, 0)))

    w_in = p['w_in']
    o_z, o_b, o_q = CONV_CH, CONV_CH + 512, CONV_CH + 512 + 2 * DN_HEADS
    w_main = jnp.concatenate([w_in[:, :o_b], w_in[:, o_q:]], axis=1).astype(BF16)
    wb = jnp.pad(w_in[:, o_b:o_q], ((0, 0), (0, LANES - 2 * DN_HEADS)))
    wb_hi = wb.astype(BF16)
    wb_lo = (wb - wb_hi.astype(F32)).astype(BF16)
    g_pre_mix = p['g_pre_mix'].reshape(1, d)

    conv_w8 = jnp.pad(p['conv_w'], ((0, 8 - CONV_W), (0, 0)))
    pad_g = lambda a: jnp.pad(a.reshape(1, DN_HEADS), ((0, 0), (DN_HEADS, LANES - 2 * DN_HEADS)))
    alog_row, dtb_row = pad_g(p['a_log']), pad_g(p['dt_bias'])
    onorm_a = p['onorm_a'].reshape(1, DN_DIM)
    onorm_b2 = jnp.tile(p['onorm_b'].reshape(1, SB_DIM), (1, 2))

    w_out16 = p['w_out'].astype(BF16)
    wr = jnp.pad(jnp.concatenate([p['w_router_group'], p['w_router_expert']], axis=1),
                 ((0, 0), (0, LANES - N_GROUPS - N_EXPERTS)))
    wr_hi = wr.astype(BF16)
    wr_lo = (wr - wr_hi.astype(F32)).astype(BF16)
    b_r = jnp.pad(jnp.concatenate([p['b_router_group'], p['b_router_expert']]).reshape(1, -1),
                  ((0, 0), (0, LANES - N_GROUPS - N_EXPERTS)))
    g_post_mix = p['g_post_mix'].reshape(1, d)
    g_pre_ffn = p['g_pre_ffn'].reshape(1, d)
    g_post_ffn = p['g_post_ffn'].reshape(1, d)

    def mixer(x, tm, seq_rows, mod_row0, hist8, s0, k_old, v_old, chunk, n_chunks, bq, n_sub, cnt0):
        b, t, _ = x.shape
        x2d = x.reshape(b * t, d)
        a_in, z, bg, q16, kb, vb, k16, v16 = _proj(x2d, mod8, g_pre_mix, w_main, wb_hi, wb_lo,
                                                    tm, seq_rows, mod_row0)
        r3 = lambda a: a.reshape(b, t, a.shape[-1])
        oa16, s_new = _delta(r3(a_in), r3(z), r3(bg), hist8, s0, conv_w8, alog_row, dtb_row, onorm_a,
                             chunk, n_chunks)
        k16, v16 = r3(k16), r3(v16)
        if k_old is not None:
            k16 = jnp.concatenate([k_old.reshape(b, -1, SB_WIDTH).astype(BF16), k16], axis=1)
            v16 = jnp.concatenate([v_old.reshape(b, -1, SB_WIDTH).astype(BF16), v16], axis=1)
        n_pad = (-k16.shape[1]) % KEY_TILE
        k16 = jnp.pad(k16, ((0, 0), (n_pad, 0), (0, 0)))
        v16 = jnp.pad(v16, ((0, 0), (n_pad, 0), (0, 0)))
        ob16 = _sb_attn(r3(q16), k16, v16, onorm_b2, bq, n_sub, n_pad)
        x1, h2, route, cnt = _post(oa16.reshape(b * t, 512), ob16.reshape(b * t, 512), x2d, mod8,
                                   g_post_mix, g_pre_ffn, w_out16, wr_hi, wr_lo, b_r, cnt0, tm, seq_rows,
                                   mod_row0)
        new_conv = r3(a_in)[:, t - (CONV_W - 1):, :]
        return (x1, h2, route, cnt, kb.reshape(b, t, SB_HEADS, SB_DIM), vb.reshape(b, t, SB_HEADS, SB_DIM),
                s_new, new_conv)

    zero_hist = jnp.zeros((bp, 8, CONV_CH), F32)
    zero_s = jnp.zeros((bp, DN_HEADS, DN_DIM, DN_DIM), F32)
    hist_s = jnp.pad(conv_s, ((0, 0), (8 - (CONV_W - 1), 0), (0, 0)))
    tm_p = min(256, tp)
    nc_p = max(1, min(4, tp // DELTA_BLOCK))
    x1p, h2p, rp, cnt_p, kp, vp, sp, cp = mixer(x_p, tm_p, tp, 0, zero_hist, zero_s, None, None,
                                                 min(DELTA_BLOCK, tp), nc_p, min(KEY_TILE // 2, tp), 2,
                                                 jnp.zeros((1, LANES), F32))
    x1s, h2s, rs, cnt, ks, vs, ss, cs = mixer(x_s, ts, ts, bp, hist_s, s0_s, k_past, v_past,
                                               min(DELTA_BLOCK, ts), max(1, ts // DELTA_BLOCK),
                                               min(KEY_TILE, ts), 1, cnt_p)

    n_blocks = -(-2 * n_tok // MOE_BLOCK) + N_EXPERTS
    blk_e, nvalid, seg, pad_start = _segment_plan(cnt, n_blocks)
    dest_p = _token_rows(rp, pad_start)
    dest_s = _token_rows(rs, pad_start)
    x_sorted = _dispatch(seg, dest_p, dest_s, h2p, h2s, n_blocks, tm_p)
    y_sorted = _moe(blk_e, nvalid, x_sorted, p['w_gate'], p['w_up'], p['w_down'])
    y_p = _combine(dest_p, rp, y_sorted, x1p, mod8, g_post_ffn, tm_p, tp, 0).reshape(bp, tp, d)
    y_s = _combine(dest_s, rs, y_sorted, x1s, mod8, g_post_ffn, ts, ts, bp).reshape(bs, ts, d)
    return y_p, y_s, kp, vp, sp, cp, ks, vs, ss, cs


def kernel(x_prompt, x_sample, c_prompt, c_sample, cache_k, cache_v, state_delta, state_conv, w_ada, b_ada, g_pre_mix, g_post_mix, g_pre_ffn, g_post_ffn, w_in, conv_w, a_log, dt_bias, onorm_a, onorm_b, w_out, w_router_group, b_router_group, w_router_expert, b_router_expert, w_gate, w_up, w_down):
    depth = w_in.shape[0]
    y_p, y_s = x_prompt, x_sample
    outs = [[] for _ in range(8)]
    for l in range(depth):
        p = dict(w_ada=w_ada[l], b_ada=b_ada[l], g_pre_mix=g_pre_mix[l], g_post_mix=g_post_mix[l],
                 g_pre_ffn=g_pre_ffn[l], g_post_ffn=g_post_ffn[l], w_in=w_in[l], conv_w=conv_w[l],
                 a_log=a_log[l], dt_bias=dt_bias[l], onorm_a=onorm_a[l], onorm_b=onorm_b[l],
                 w_out=w_out[l], w_router_group=w_router_group[l], b_router_group=b_router_group[l],
                 w_router_expert=w_router_expert[l], b_router_expert=b_router_expert[l],
                 w_gate=w_gate[l], w_up=w_up[l], w_down=w_down[l])
        res = _layer(y_p, y_s, c_prompt, c_sample, cache_k[l], cache_v[l], state_delta[l], state_conv[l], p)
        y_p, y_s = res[0], res[1]
        for lst, r in zip(outs, res[2:]):
            lst.append(r)
    return (y_p, y_s) + tuple(jnp.stack(o) for o in outs)
```

```python
import functools
import math

import jax
import jax.numpy as jnp
from jax import lax
from jax.experimental import pallas as pl
from jax.experimental.pallas import tpu as pltpu

F32 = jnp.float32
BF16 = jnp.bfloat16

D_MODEL = 1024
DN_HEADS = 4
DN_DIM = 128
CONV_W = 4
CONV_CH = DN_HEADS * 3 * DN_DIM
DELTA_BLOCK = 64
SB_HEADS = 8
SB_DIM = 64
SB_WIDTH = SB_HEADS * SB_DIM
N_GROUPS = 4
EXPERTS_PER_GROUP = 8
N_EXPERTS = N_GROUPS * EXPERTS_PER_GROUP
D_EXPERT = D_MODEL // 2
MOE_BLOCK = 256
EPS = 1e-6

LANES = 128
KEY_TILE = 128
ATTN_WINDOW = 3 * KEY_TILE
WINDOW_BLOCKS = 4
EXP_ZERO_BELOW = -104.0
VMEM_LIMIT = 56 * 1024 * 1024


def _cparams(sem):
    return pltpu.CompilerParams(dimension_semantics=sem, vmem_limit_bytes=VMEM_LIMIT)


def _split(a):
    hi = a.astype(BF16)
    lo = (a - hi.astype(F32)).astype(BF16)
    return hi, lo


def _dot(a, b, dims=(((1,), (0,)), ((), ()))):
    return lax.dot_general(a, b, dims, preferred_element_type=F32)


def _dot_small_int_lhs(a, b):
    a16 = a.astype(BF16)
    return _dot(jnp.concatenate([a16, a16], axis=1), jnp.concatenate(_split(b), axis=0))


_NT = (((1,), (1,)), ((), ()))
_TN = (((0,), (0,)), ((), ()))


def _silu(x):
    return x * jax.nn.sigmoid(x)


def _softplus(x):
    return jnp.maximum(x, 0.0) + jnp.log(1.0 + jnp.exp(-jnp.abs(x)))


def _ada_kernel(c_ref, w_ref, b_ref, o_ref):
    s = _silu(c_ref[...]).astype(BF16)
    o_ref[...] = _dot(s, w_ref[...].astype(BF16)) + b_ref[...]


def _ada(c_all, w_ada, b_ada):
    rows = c_all.shape[0]
    n = w_ada.shape[1]
    tn = 1024
    return pl.pallas_call(
        _ada_kernel,
        grid=(n // tn,),
        in_specs=[pl.BlockSpec((rows, D_MODEL), lambda j: (0, 0)),
                  pl.BlockSpec((D_MODEL, tn), lambda j: (0, j)),
                  pl.BlockSpec((1, tn), lambda j: (0, j))],
        out_specs=pl.BlockSpec((rows, tn), lambda j: (0, j)),
        out_shape=jax.ShapeDtypeStruct((rows, n), F32),
        compiler_params=_cparams(("arbitrary",)),
        name="ada",
    )(c_all, w_ada, b_ada.reshape(1, n))


def _rms(x, gain):
    return x * lax.rsqrt(jnp.mean(x * x, axis=-1, keepdims=True) + EPS) * gain


def _per_seq(y, mod, fn):
    n_seq = mod.shape[0]
    ys = y.reshape(n_seq, y.shape[0] // n_seq, y.shape[1])
    return fn(ys, mod).reshape(y.shape)


def _mod_spec(tm, seq_rows, mod_row0):
    if tm >= seq_rows:
        n_seq = tm // seq_rows
        assert tm % seq_rows == 0 and mod_row0 % n_seq == 0
        return pl.BlockSpec((n_seq, 8, D_MODEL), lambda i: (mod_row0 // n_seq + i, 0, 0))
    assert seq_rows % tm == 0
    return pl.BlockSpec((1, 8, D_MODEL), lambda i: (mod_row0 + (i * tm) // seq_rows, 0, 0))


def _proj_kernel(x_ref, mod_ref, g_ref, wm_ref, wbh_ref, wbl_ref,
                 a_ref, z_ref, bg_ref, q_ref, k_ref, v_ref, k16_ref, v16_ref):
    h = _per_seq(_rms(x_ref[...], g_ref[...]), mod_ref[...], lambda y, m: y * (1.0 + m[:, 1:2]) + m[:, 0:1])
    hh, hl = _split(h)
    p = _dot(hh, wm_ref[...])
    a_ref[...] = p[:, 0:CONV_CH]
    z_ref[...] = p[:, CONV_CH:CONV_CH + 512]
    o = CONV_CH + 512
    q_ref[...] = (p[:, o:o + 512] * (SB_DIM ** -0.5)).astype(BF16)
    k = p[:, o + 512:o + 1024]
    v = p[:, o + 1024:o + 1536]
    k_ref[...] = k
    v_ref[...] = v
    k16_ref[...] = k.astype(BF16)
    v16_ref[...] = v.astype(BF16)
    wbh = wbh_ref[...]
    bg_ref[...] = _dot(hh, wbh) + (_dot(hl, wbh) + _dot(hh, wbl_ref[...]))


def _proj(x2d, mod8, g_pre, w_main, wb_hi, wb_lo, tm, seq_rows, mod_row0):
    n = x2d.shape[0]
    nm = w_main.shape[1]
    row = lambda i: (i, 0)
    const = lambda i: (0, 0)
    outs = [(CONV_CH, F32), (512, F32), (LANES, F32), (512, BF16), (512, F32), (512, F32),
            (512, BF16), (512, BF16)]
    return pl.pallas_call(
        _proj_kernel,
        grid=(n // tm,),
        in_specs=[pl.BlockSpec((tm, D_MODEL), row),
                  _mod_spec(tm, seq_rows, mod_row0),
                  pl.BlockSpec((1, D_MODEL), const),
                  pl.BlockSpec((D_MODEL, nm), const),
                  pl.BlockSpec((D_MODEL, LANES), const),
                  pl.BlockSpec((D_MODEL, LANES), const)],
        out_specs=[pl.BlockSpec((tm, w), row) for w, _ in outs],
        out_shape=[jax.ShapeDtypeStruct((n, w), dt) for w, dt in outs],
        compiler_params=_cparams(("arbitrary",)),
        name="proj",
    )(x2d, mod8, g_pre, w_main, wb_hi, wb_lo)


def _delta_kernel(a_ref, z_ref, bg_ref, hist0_ref, s0_ref, cw_ref, alog_ref, dtb_ref, on_ref,
                  o_ref, sfin_ref, hist_sc, s_sc, *, chunk, n_chunks):
    t_idx = pl.program_id(1)
    tt = chunk * n_chunks

    @pl.when(t_idx == 0)
    def _():
        hist_sc[...] = hist0_ref[0]
        s_sc[...] = s0_ref[0]

    x = a_ref[0]
    xx = jnp.concatenate([hist_sc[...], x], axis=0)
    cw = cw_ref[...]
    conv = x * cw[CONV_W - 1:CONV_W]
    for s in range(1, CONV_W):
        conv = conv + pltpu.roll(xx, s, 0)[8:] * cw[CONV_W - 1 - s:CONV_W - s]
    conv = _silu(conv)
    hist_sc[...] = x[tt - 8:tt]

    bg = bg_ref[0]
    lane = lax.broadcasted_iota(jnp.int32, (1, LANES), 1)
    g_lane = (lane >= DN_HEADS) & (lane < 2 * DN_HEADS)
    neg_a = jnp.where(g_lane, -jnp.exp(alog_ref[...]), 0.0)
    beta_all = jax.nn.sigmoid(bg)
    g_all = neg_a * _softplus(bg + dtb_ref[...])

    ri = lax.broadcasted_iota(jnp.int32, (chunk, chunk), 0)
    ci = lax.broadcasted_iota(jnp.int32, (chunk, chunk), 1)
    incl = ri >= ci
    strict = ri > ci
    tri = incl.astype(F32)
    eye = (ri == ci).astype(F32)
    ones_cc = jnp.ones((chunk, chunk), F32)
    onorm = on_ref[...]

    chunks = range(n_chunks)
    units = [(c, h) for c in chunks for h in range(DN_HEADS)]
    rows = lambda c: slice(c * chunk, (c + 1) * chunk)

    gc_all = [_dot_small_int_lhs(tri, g_all[rows(c)]) for c in chunks]
    diag = [jnp.concatenate([eye * gc_all[c][:, DN_HEADS + h:DN_HEADS + h + 1] for h in range(DN_HEADS)], axis=1)
            for c in chunks]
    gc_rows = [_dot_small_int_lhs(ones_cc, diag[c]) for c in chunks]

    q, k, v, beta, gcol, glast, eg, decay, kbeta = {}, {}, {}, {}, {}, {}, {}, {}, {}
    for (c, h) in units:
        lo = h * DN_DIM
        qq = conv[rows(c), lo:lo + DN_DIM]
        kk = conv[rows(c), 512 + lo:512 + lo + DN_DIM]
        u_ = (c, h)
        q[u_] = qq * lax.rsqrt(jnp.sum(qq * qq, axis=-1, keepdims=True) + EPS) * (DN_DIM ** -0.5)
        k[u_] = kk * lax.rsqrt(jnp.sum(kk * kk, axis=-1, keepdims=True) + EPS)
        v[u_] = conv[rows(c), 1024 + lo:1024 + lo + DN_DIM]
        beta[u_] = beta_all[rows(c), h:h + 1]
        gcol[u_] = gc_all[c][:, DN_HEADS + h:DN_HEADS + h + 1]
        glast[u_] = gcol[u_][chunk - 1:chunk, :]
        eg[u_] = jnp.exp(gcol[u_])
        grow = gc_rows[c][:, h * chunk:(h + 1) * chunk]
        decay[u_] = jnp.where(incl, jnp.exp(jnp.minimum(gcol[u_] - grow, 0.0)), 0.0)
        kbeta[u_] = k[u_] * beta[u_]

    k16 = {u_: k[u_].astype(BF16) for u_ in units}
    kq = {u_: _dot(jnp.concatenate([kbeta[u_], q[u_]], axis=0).astype(BF16), k16[u_], _NT) for u_ in units}
    lower = {u_: jnp.where(strict, kq[u_][:chunk] * decay[u_], 0.0) for u_ in units}
    attn16 = {u_: jnp.where(incl, kq[u_][chunk:] * decay[u_], 0.0).astype(BF16) for u_ in units}

    sol = {u_: jnp.concatenate([v[u_] * beta[u_], kbeta[u_] * eg[u_]], axis=1) for u_ in units}
    lp = lower
    p = 1
    while p < chunk:
        lsp = {u_: _split(lp[u_]) for u_ in units}
        ssp = {u_: _split(sol[u_]) for u_ in units}
        lcat = {u_: jnp.concatenate([lsp[u_][0], lsp[u_][1], lsp[u_][0]], axis=1) for u_ in units}
        upd = {u_: _dot(lcat[u_], jnp.concatenate([ssp[u_][0], ssp[u_][0], ssp[u_][1]], axis=0)) for u_ in units}
        sol = {u_: (sol[u_] - upd[u_]) if p == 1 else (sol[u_] + upd[u_]) for u_ in units}
        if 2 * p < chunk:
            lp = {u_: _dot(lcat[u_], jnp.concatenate([lsp[u_][0], lsp[u_][0], lsp[u_][1]], axis=0))
                  for u_ in units}
        p *= 2

    usol = {u_: sol[u_][:, :DN_DIM] for u_ in units}
    wq16 = {u_: jnp.concatenate([sol[u_][:, DN_DIM:], q[u_] * eg[u_]], axis=0).astype(BF16) for u_ in units}
    kd16 = {u_: (k[u_] * jnp.exp(glast[u_] - gcol[u_])).astype(BF16) for u_ in units}

    for c in chunks:
        hs = range(DN_HEADS)
        s_old = [s_sc[h] for h in hs]
        ws = [_dot(wq16[(c, h)], s_old[h].astype(BF16)) for h in hs]
        v_new = [(usol[(c, h)] - ws[h][:chunk]).astype(BF16) for h in hs]
        o_in = [_dot(attn16[(c, h)], v_new[h]) for h in hs]
        ds = [_dot(kd16[(c, h)], v_new[h], _TN) for h in hs]
        for h in hs:
            lo = h * DN_DIM
            s_sc[h] = s_old[h] * jnp.exp(glast[(c, h)]) + ds[h]
            o = ws[h][chunk:] + o_in[h]
            zg = z_ref[0, rows(c), lo:lo + DN_DIM]
            o_ref[0, rows(c), lo:lo + DN_DIM] = (_rms(o, onorm) * _silu(zg)).astype(BF16)

    @pl.when(t_idx == pl.num_programs(1) - 1)
    def _():
        sfin_ref[0] = s_sc[...]


def _delta(a_in, z, bg, hist8, s0, conv_w8, alog_row, dtb_row, onorm_a, chunk, n_chunks):
    b, t, _ = a_in.shape
    tt = chunk * n_chunks
    tile = lambda bi, ti: (bi, ti, 0)
    per_b3 = lambda bi, ti: (bi, 0, 0)
    per_b4 = lambda bi, ti: (bi, 0, 0, 0)
    const = lambda bi, ti: (0, 0)
    kern = functools.partial(_delta_kernel, chunk=chunk, n_chunks=n_chunks)
    return pl.pallas_call(
        kern,
        grid=(b, t // tt),
        in_specs=[pl.BlockSpec((1, tt, CONV_CH), tile),
                  pl.BlockSpec((1, tt, 512), tile),
                  pl.BlockSpec((1, tt, LANES), tile),
                  pl.BlockSpec((1, 8, CONV_CH), per_b3),
                  pl.BlockSpec((1, DN_HEADS, DN_DIM, DN_DIM), per_b4),
                  pl.BlockSpec((8, CONV_CH), const),
                  pl.BlockSpec((1, LANES), const),
                  pl.BlockSpec((1, LANES), const),
                  pl.BlockSpec((1, DN_DIM), const)],
        out_specs=[pl.BlockSpec((1, tt, 512), tile),
                   pl.BlockSpec((1, DN_HEADS, DN_DIM, DN_DIM), per_b4)],
        out_shape=[jax.ShapeDtypeStruct((b, t, 512), BF16),
                   jax.ShapeDtypeStruct((b, DN_HEADS, DN_DIM, DN_DIM), F32)],
        scratch_shapes=[pltpu.VMEM((8, CONV_CH), F32),
                        pltpu.VMEM((DN_HEADS, DN_DIM, DN_DIM), F32)],
        compiler_params=_cparams(("arbitrary", "arbitrary")),
        name="delta",
    )(a_in, z, bg, hist8, s0, conv_w8, alog_row, dtb_row, onorm_a)


def _sb_kernel(q_ref, *refs, bq, n_sub, n_pad, q_off):
    kwin = refs[:WINDOW_BLOCKS]
    vwin = refs[WINDOW_BLOCKS:2 * WINDOW_BLOCKS]
    k_hbm, v_hbm, on_ref, o_ref, kbuf, vbuf, qsel, acc, carry, sem = refs[2 * WINDOW_BLOCKS:]
    b = pl.program_id(0)
    i = pl.program_id(1)
    g = bq * n_sub
    qend_step = q_off + (i + 1) * g
    win_start = qend_step - WINDOW_BLOCKS * KEY_TILE
    n_pairs = SB_HEADS // 2
    heads = range(SB_HEADS)

    half_lane = lax.broadcasted_iota(jnp.int32, (bq, LANES), 1) < SB_DIM
    rj = lax.broadcasted_iota(jnp.int32, (2 * KEY_TILE, 2 * KEY_TILE), 0) % KEY_TILE
    cj = lax.broadcasted_iota(jnp.int32, (2 * KEY_TILE, 2 * KEY_TILE), 1)
    suffix2 = ((rj > cj) | (cj >= KEY_TILE)).astype(BF16)

    def window(blocks, off, p):
        parts = []
        for blk in range(WINDOW_BLOCKS):
            lo, hi = max(off, blk * KEY_TILE), min(off + ATTN_WINDOW, (blk + 1) * KEY_TILE)
            if lo < hi:
                ref = blocks[WINDOW_BLOCKS - 1 - blk]
                parts.append(ref[0, lo - blk * KEY_TILE:hi - blk * KEY_TILE, p * LANES:(p + 1) * LANES])
        return jnp.concatenate(parts, axis=0)

    n_t = ATTN_WINDOW // KEY_TILE
    row = lax.broadcasted_iota(jnp.int32, (bq, ATTN_WINDOW), 0)
    col = lax.broadcasted_iota(jnp.int32, (bq, ATTN_WINDOW), 1)
    causal = col < row + (ATTN_WINDOW - bq)
    subs = range(n_sub)
    offs = [WINDOW_BLOCKS * KEY_TILE - g + (s + 1) * bq - ATTN_WINDOW for s in subs]
    done = []
    for s in subs:
        mask = causal & (col >= n_pad - (win_start + offs[s]))
        zs = {}
        for p in range(n_pairs):
            qf = q_ref[0, s * bq:(s + 1) * bq, p * LANES:(p + 1) * LANES].astype(F32)
            qq = jnp.concatenate([jnp.where(half_lane, qf, 0.0), jnp.where(half_lane, 0.0, qf)], axis=0)
            zz = _dot(qq.astype(BF16), window(kwin, offs[s], p), _NT)
            zs[2 * p], zs[2 * p + 1] = zz[:bq], zz[bq:]
        sp = {h: _softplus(zs[h]) for h in heads}
        pieces = []
        for h in heads:
            hi, lo = _split(jnp.where(mask, -sp[h], 0.0))
            for j in range(n_t):
                c0 = ATTN_WINDOW - (j + 1) * KEY_TILE
                pieces.append(jnp.concatenate([hi[:, c0:c0 + KEY_TILE], lo[:, c0:c0 + KEY_TILE]], axis=1))
        cs_all = _dot(jnp.concatenate(pieces, axis=0), suffix2)
        worst = jnp.full((bq, KEY_TILE), -jnp.inf, F32)
        a = {}
        for h in heads:
            run = None
            cols = []
            for j in range(n_t):
                r0 = (h * n_t + j) * bq
                cs = cs_all[r0:r0 + bq]
                cols.append(cs[:, :KEY_TILE] if run is None else cs[:, :KEY_TILE] + run)
                run = cs[:, KEY_TILE:] if run is None else run + cs[:, KEY_TILE:]
            within = jnp.concatenate(cols[::-1], axis=1)
            a[h] = jnp.where(mask, jnp.exp((zs[h] - sp[h]) + within), 0.0).astype(BF16)
            carry[s, h] = run
            worst = jnp.maximum(worst, run)
        for p in range(n_pairs):
            pv = _dot(jnp.concatenate([a[2 * p], a[2 * p + 1]], axis=0), window(vwin, offs[s], p))
            acc[s, p] = jnp.where(half_lane, pv[:bq], pv[bq:])
        done.append((jnp.max(worst) < EXP_ZERO_BELOW).astype(jnp.int32))

    rj1 = lax.broadcasted_iota(jnp.int32, (KEY_TILE, 2 * KEY_TILE), 0)
    cj1 = lax.broadcasted_iota(jnp.int32, (KEY_TILE, 2 * KEY_TILE), 1)
    suffix1 = ((rj1 > cj1) | (cj1 >= KEY_TILE)).astype(BF16)
    col1 = lax.broadcasted_iota(jnp.int32, (bq, KEY_TILE), 1)
    for s in subs:
        swept_from = win_start + offs[s]

        @pl.when((done[s] == 0) & (swept_from > n_pad))
        def _():
            for p in range(n_pairs):
                qf = q_ref[0, s * bq:(s + 1) * bq, p * LANES:(p + 1) * LANES].astype(F32)
                qsel[2 * p] = jnp.where(half_lane, qf, 0.0).astype(BF16)
                qsel[2 * p + 1] = jnp.where(half_lane, 0.0, qf).astype(BF16)

            def body(state):
                upper, _ = state
                start = jnp.maximum(upper - KEY_TILE, 0)
                copies = []
                for p in range(n_pairs):
                    for src, dst in ((k_hbm, kbuf), (v_hbm, vbuf)):
                        cp = pltpu.make_async_copy(
                            src.at[b, pl.ds(pl.multiple_of(start, 16), KEY_TILE), pl.ds(p * LANES, LANES)],
                            dst.at[p], sem)
                        cp.start()
                        copies.append(cp)
                for cp in copies:
                    cp.wait()
                kpos = start + col1
                m1 = (kpos < upper) & (kpos >= n_pad)
                zz = [_dot(qsel[h], kbuf[h // 2], _NT) for h in heads]
                spp = [_softplus(zz[h]) for h in heads]
                pcs = []
                for h in heads:
                    pcs.extend(_split(jnp.where(m1, -spp[h], 0.0)))
                cs1 = _dot(jnp.concatenate(pcs, axis=0), suffix1)
                worst = jnp.full((bq, KEY_TILE), -jnp.inf, F32)
                aa = []
                for h in heads:
                    c = cs1[2 * h * bq:(2 * h + 1) * bq] + cs1[(2 * h + 1) * bq:(2 * h + 2) * bq]
                    c_old = carry[s, h]
                    aa.append(jnp.where(m1, jnp.exp((zz[h] - spp[h]) + c[:, :KEY_TILE] + c_old), 0.0).astype(BF16))
                    c_new = c_old + c[:, KEY_TILE:]
                    carry[s, h] = c_new
                    worst = jnp.maximum(worst, c_new)
                for p in range(n_pairs):
                    pv0 = _dot(aa[2 * p], vbuf[p])
                    pv1 = _dot(aa[2 * p + 1], vbuf[p])
                    acc[s, p] = acc[s, p] + jnp.where(half_lane, pv0, pv1)
                return start, (jnp.max(worst) < EXP_ZERO_BELOW).astype(jnp.int32)

            lax.while_loop(lambda st: (st[0] > n_pad) & (st[1] == 0), body, (swept_from, jnp.int32(0)))

    onb = on_ref[...]
    for s in subs:
        for p in range(n_pairs):
            o = acc[s, p]
            sq = o * o
            s_lo = jnp.sum(jnp.where(half_lane, sq, 0.0), axis=-1, keepdims=True)
            s_hi = jnp.sum(jnp.where(half_lane, 0.0, sq), axis=-1, keepdims=True)
            ms = jnp.where(half_lane, s_lo, s_hi) * (1.0 / SB_DIM)
            o_ref[0, s * bq:(s + 1) * bq, p * LANES:(p + 1) * LANES] = (o * lax.rsqrt(ms + EPS) * onb).astype(BF16)


def _sb_attn(q16, k16p, v16p, onorm_b2, bq, n_sub, n_pad):
    b, tq, _ = q16.shape
    tkp = k16p.shape[1]
    q_off = tkp - tq
    g = bq * n_sub
    assert tq % g == 0 and g <= KEY_TILE and bq % 16 == 0
    assert all((q_off + (i + 1) * g) % KEY_TILE == 0 for i in range(tq // g))

    def kmap(back):
        def f(bi, i):
            last = (q_off + (i + 1) * g) // KEY_TILE - 1
            return (bi, jnp.maximum(last - back, 0), 0)
        return f

    qmap = lambda bi, i: (bi, i, 0)
    kern = functools.partial(_sb_kernel, bq=bq, n_sub=n_sub, n_pad=n_pad, q_off=q_off)
    kspec = [pl.BlockSpec((1, KEY_TILE, SB_WIDTH), kmap(back)) for back in range(WINDOW_BLOCKS)]
    return pl.pallas_call(
        kern,
        grid=(b, tq // g),
        in_specs=[pl.BlockSpec((1, g, SB_WIDTH), qmap)] + kspec + kspec
                 + [pl.BlockSpec(memory_space=pl.ANY), pl.BlockSpec(memory_space=pl.ANY),
                    pl.BlockSpec((1, LANES), lambda bi, i: (0, 0))],
        out_specs=pl.BlockSpec((1, g, SB_WIDTH), qmap),
        out_shape=jax.ShapeDtypeStruct((b, tq, SB_WIDTH), BF16),
        scratch_shapes=[pltpu.VMEM((SB_HEADS // 2, KEY_TILE, LANES), BF16),
                        pltpu.VMEM((SB_HEADS // 2, KEY_TILE, LANES), BF16),
                        pltpu.VMEM((SB_HEADS, bq, LANES), BF16),
                        pltpu.VMEM((n_sub, SB_HEADS // 2, bq, LANES), F32),
                        pltpu.VMEM((n_sub, SB_HEADS, bq, KEY_TILE), F32),
                        pltpu.SemaphoreType.DMA(())],
        compiler_params=_cparams(("arbitrary", "arbitrary")),
        name="sb_attn",
    )(q16, *([k16p] * WINDOW_BLOCKS), *([v16p] * WINDOW_BLOCKS), k16p, v16p, onorm_b2)


def _post_kernel(oa_ref, ob_ref, x_ref, mod_ref, gpm_ref, gpf_ref, wo_ref, wrh_ref, wrl_ref, br_ref, cnt0_ref,
                 x1_ref, h2_ref, route_ref, cnt_ref):
    @pl.when(pl.program_id(0) == 0)
    def _():
        cnt_ref[...] = cnt0_ref[...]

    mod = mod_ref[...]
    mix = _dot(oa_ref[...], wo_ref[0:512, :]) + _dot(ob_ref[...], wo_ref[512:1024, :])
    x1 = x_ref[...] + _per_seq(_rms(mix, gpm_ref[...]), mod, lambda y, m: y * m[:, 2:3])
    x1_ref[...] = x1
    h2 = _per_seq(_rms(x1, gpf_ref[...]), mod, lambda y, m: y * (1.0 + m[:, 4:5]) + m[:, 3:4])
    h2_ref[...] = h2
    hh, hl = _split(h2)
    wrh = wrh_ref[...]
    logits = _dot(hh, wrh) + (_dot(hl, wrh) + _dot(hh, wrl_ref[...])) + br_ref[...]
    lane = lax.broadcasted_iota(jnp.int32, logits.shape, 1).astype(F32)
    neg = -jnp.inf
    nl = float(LANES)
    lg = jnp.where(lane < N_GROUPS, logits, neg)
    gmax = jnp.max(lg, axis=-1, keepdims=True)
    grp = jnp.min(jnp.where(lg == gmax, lane, nl), axis=-1, keepdims=True)
    p_grp = 1.0 / jnp.sum(jnp.exp(lg - gmax), axis=-1, keepdims=True)
    first = N_GROUPS + grp * EXPERTS_PER_GROUP
    le = jnp.where((lane >= first) & (lane < first + EXPERTS_PER_GROUP), logits, neg)
    emax = jnp.max(le, axis=-1, keepdims=True)
    i1 = jnp.min(jnp.where(le == emax, lane, nl), axis=-1, keepdims=True)
    esum = jnp.sum(jnp.exp(le - emax), axis=-1, keepdims=True)
    le2 = jnp.where(lane == i1, neg, le)
    e2max = jnp.max(le2, axis=-1, keepdims=True)
    i2 = jnp.min(jnp.where(le2 == e2max, lane, nl), axis=-1, keepdims=True)
    p1 = 1.0 / esum
    p2 = jnp.exp(e2max - emax) / esum
    w1 = p_grp * p1 / (p1 + p2)
    w2 = p_grp * p2 / (p1 + p2)
    e1 = i1 - N_GROUPS
    e2 = i2 - N_GROUPS
    hot1 = (lane == e1).astype(F32)
    hot2 = (lane == e2).astype(F32)
    both = hot1 + hot2
    tm = logits.shape[0]
    ti = lax.broadcasted_iota(jnp.int32, (tm, tm), 0)
    tj = lax.broadcasted_iota(jnp.int32, (tm, tm), 1)
    earlier = _dot((ti > tj).astype(BF16), both.astype(BF16)) + cnt_ref[...]
    rank1 = jnp.sum(hot1 * earlier, axis=-1, keepdims=True)
    rank2 = jnp.sum(hot2 * (earlier + hot1), axis=-1, keepdims=True)
    cnt_ref[...] = cnt_ref[...] + jnp.sum(both, axis=0, keepdims=True)
    out = jnp.where(lane == 0.0, e1, 0.0)
    out = jnp.where(lane == 1.0, e2, out)
    out = jnp.where(lane == 2.0, w1, out)
    out = jnp.where(lane == 3.0, w2, out)
    out = jnp.where(lane == 4.0, rank1, out)
    out = jnp.where(lane == 5.0, rank2, out)
    route_ref[...] = out


def _post(oa16, ob16, x2d, mod8, g_post_mix, g_pre_ffn, w_out16, wr_hi, wr_lo, b_r, cnt0, tm, seq_rows,
          mod_row0):
    n = x2d.shape[0]
    row = lambda i: (i, 0)
    const = lambda i: (0, 0)
    return pl.pallas_call(
        _post_kernel,
        grid=(n // tm,),
        in_specs=[pl.BlockSpec((tm, 512), row),
                  pl.BlockSpec((tm, 512), row),
                  pl.BlockSpec((tm, D_MODEL), row),
                  _mod_spec(tm, seq_rows, mod_row0),
                  pl.BlockSpec((1, D_MODEL), const),
                  pl.BlockSpec((1, D_MODEL), const),
                  pl.BlockSpec((D_MODEL, D_MODEL), const),
                  pl.BlockSpec((D_MODEL, LANES), const),
                  pl.BlockSpec((D_MODEL, LANES), const),
                  pl.BlockSpec((1, LANES), const),
                  pl.BlockSpec((1, LANES), const)],
        out_specs=[pl.BlockSpec((tm, D_MODEL), row),
                   pl.BlockSpec((tm, D_MODEL), row),
                   pl.BlockSpec((tm, LANES), row),
                   pl.BlockSpec((1, LANES), const)],
        out_shape=[jax.ShapeDtypeStruct((n, D_MODEL), F32),
                   jax.ShapeDtypeStruct((n, D_MODEL), F32),
                   jax.ShapeDtypeStruct((n, LANES), F32),
                   jax.ShapeDtypeStruct((1, LANES), F32)],
        compiler_params=_cparams(("arbitrary",)),
        name="post",
    )(oa16, ob16, x2d, mod8, g_post_mix, g_pre_ffn, w_out16, wr_hi, wr_lo, b_r, cnt0)


def _dispatch_kernel(seg_ref, dp_ref, ds_ref, hp_ref, hs_ref, xs_hbm, zbuf, sem, *, n_blocks):
    i = pl.program_id(0)
    last_step = pl.num_programs(0) - 1

    @pl.when(i == 0)
    def _():
        zbuf[...] = jnp.zeros_like(zbuf)

        def zero_block(row0):
            return pltpu.make_async_copy(zbuf, xs_hbm.at[pl.ds(pl.multiple_of(row0, MOE_BLOCK), MOE_BLOCK), :], sem)

        for e in range(N_EXPERTS):
            @pl.when(seg_ref[e] > 0)
            def _():
                zero_block(seg_ref[N_EXPERTS + e] - MOE_BLOCK).start()
        for e in range(N_EXPERTS):
            @pl.when(seg_ref[e] > 0)
            def _():
                zero_block(seg_ref[N_EXPERTS + e] - MOE_BLOCK).wait()

        used = seg_ref[2 * N_EXPERTS - 1] // MOE_BLOCK

        def fill(b, c):
            cp = zero_block(b * MOE_BLOCK)
            cp.start()
            cp.wait()
            return c

        lax.fori_loop(used, n_blocks, fill, 0)

    def scatter(h_ref, dest_ref):
        rows = h_ref.shape[0]
        for t in range(rows):
            for slot in range(2):
                pltpu.make_async_copy(h_ref.at[pl.ds(t, 1), :],
                                      xs_hbm.at[pl.ds(dest_ref[0, 0, 2 * t + slot], 1), :],
                                      sem).start(priority=slot)
        for slot in range(2):
            pltpu.make_async_copy(h_ref, xs_hbm.at[pl.ds(0, rows), :], sem).wait()

    @pl.when(i < last_step)
    def _():
        scatter(hp_ref, dp_ref)

    @pl.when(i == last_step)
    def _():
        scatter(hs_ref, ds_ref)


def _dispatch(seg, dest_p, dest_s, h2p, h2s, n_blocks, tm):
    n_p, n_s = h2p.shape[0], h2s.shape[0]
    steps_p = n_p // tm
    pmap3 = lambda i, sg: (jnp.minimum(i, steps_p - 1), 0, 0)
    pmap2 = lambda i, sg: (jnp.minimum(i, steps_p - 1), 0)
    grid_spec = pltpu.PrefetchScalarGridSpec(
        num_scalar_prefetch=1,
        grid=(steps_p + 1,),
        in_specs=[pl.BlockSpec((1, 1, 2 * tm), pmap3, memory_space=pltpu.SMEM),
                  pl.BlockSpec((1, 1, 2 * n_s), lambda i, sg: (0, 0, 0), memory_space=pltpu.SMEM),
                  pl.BlockSpec((tm, D_MODEL), pmap2),
                  pl.BlockSpec((n_s, D_MODEL), lambda i, sg: (0, 0))],
        out_specs=pl.BlockSpec(memory_space=pl.ANY),
        scratch_shapes=[pltpu.VMEM((MOE_BLOCK, D_MODEL), F32), pltpu.SemaphoreType.DMA(())])
    return pl.pallas_call(
        functools.partial(_dispatch_kernel, n_blocks=n_blocks),
        grid_spec=grid_spec,
        out_shape=jax.ShapeDtypeStruct((n_blocks * MOE_BLOCK, D_MODEL), F32),
        compiler_params=_cparams(("arbitrary",)),
        name="dispatch",
    )(seg, dest_p.reshape(steps_p, 1, 2 * tm), dest_s.reshape(1, 1, 2 * n_s), h2p, h2s)


def _moe_kernel(blk_e_ref, nvalid_ref, x_ref, wg_ref, wu_ref, wd_ref, y_ref, wg16, wu16, wd16):
    i = pl.program_id(0)
    e = blk_e_ref[i]
    e_prev = blk_e_ref[jnp.maximum(i - 1, 0)]

    @pl.when((i == 0) | (e != e_prev))
    def _():
        wg16[...] = wg_ref[0].astype(BF16)
        wu16[...] = wu_ref[0].astype(BF16)
        wd16[...] = wd_ref[0].astype(BF16)

    @pl.when(nvalid_ref[i] > 0)
    def _():
        xb = x_ref[...].astype(BF16)
        g = _dot(xb, wg16[...])
        u = _dot(xb, wu16[...])
        hmid = (_silu(g) * u).astype(BF16)
        y_ref[...] = _dot(hmid, wd16[...])

    @pl.when(nvalid_ref[i] == 0)
    def _():
        y_ref[...] = jnp.zeros_like(y_ref)


def _moe(blk_e, nvalid, x_sorted, w_gate, w_up, w_down):
    n_blocks = blk_e.shape[0]
    wmap = lambda i, be, nv: (be[i], 0, 0)
    xmap = lambda i, be, nv: (jnp.where(nv[i] > 0, i, 0), 0)
    grid_spec = pltpu.PrefetchScalarGridSpec(
        num_scalar_prefetch=2,
        grid=(n_blocks,),
        in_specs=[pl.BlockSpec((MOE_BLOCK, D_MODEL), xmap),
                  pl.BlockSpec((1, D_MODEL, D_EXPERT), wmap),
                  pl.BlockSpec((1, D_MODEL, D_EXPERT), wmap),
                  pl.BlockSpec((1, D_EXPERT, D_MODEL), wmap)],
        out_specs=pl.BlockSpec((MOE_BLOCK, D_MODEL), lambda i, be, nv: (i, 0)),
        scratch_shapes=[pltpu.VMEM((D_MODEL, D_EXPERT), BF16),
                        pltpu.VMEM((D_MODEL, D_EXPERT), BF16),
                        pltpu.VMEM((D_EXPERT, D_MODEL), BF16)])
    return pl.pallas_call(
        _moe_kernel,
        grid_spec=grid_spec,
        out_shape=jax.ShapeDtypeStruct((n_blocks * MOE_BLOCK, D_MODEL), F32),
        compiler_params=_cparams(("arbitrary",)),
        name="moe",
    )(blk_e, nvalid, x_sorted, w_gate, w_up, w_down)


def _combine_kernel(dest_ref, route_ref, x1_ref, mod_ref, g_ref, y_hbm, o_ref, ybuf, sem):
    tm = x1_ref.shape[0]
    for t in range(tm):
        for slot in range(2):
            pltpu.make_async_copy(y_hbm.at[pl.ds(dest_ref[0, 0, 2 * t + slot], 1), :],
                                  ybuf.at[slot, pl.ds(t, 1), :], sem).start(priority=slot)
    for slot in range(2):
        pltpu.make_async_copy(y_hbm.at[pl.ds(0, tm), :], ybuf.at[slot], sem).wait()
    route = route_ref[...]
    moe = ybuf[0] * route[:, 2:3] + ybuf[1] * route[:, 3:4]
    o_ref[...] = x1_ref[...] + _per_seq(_rms(moe, g_ref[...]), mod_ref[...], lambda y, m: y * m[:, 5:6])


def _combine(dest, route, y_sorted, x1, mod8, g_post_ffn, tm, seq_rows, mod_row0):
    n = x1.shape[0]
    row = lambda i: (i, 0)
    return pl.pallas_call(
        _combine_kernel,
        grid=(n // tm,),
        in_specs=[pl.BlockSpec((1, 1, 2 * tm), lambda i: (i, 0, 0), memory_space=pltpu.SMEM),
                  pl.BlockSpec((tm, LANES), row),
                  pl.BlockSpec((tm, D_MODEL), row),
                  _mod_spec(tm, seq_rows, mod_row0),
                  pl.BlockSpec((1, D_MODEL), lambda i: (0, 0)),
                  pl.BlockSpec(memory_space=pl.ANY)],
        out_specs=pl.BlockSpec((tm, D_MODEL), row),
        out_shape=jax.ShapeDtypeStruct((n, D_MODEL), F32),
        scratch_shapes=[pltpu.VMEM((2, tm, D_MODEL), F32), pltpu.SemaphoreType.DMA(())],
        compiler_params=_cparams(("arbitrary",)),
        name="combine",
    )(dest.reshape(n // tm, 1, 2 * tm), route, x1, mod8, g_post_ffn, y_sorted)


def _segment_plan(counts_f, n_blocks):
    counts = counts_f[0, :N_EXPERTS].astype(jnp.int32)
    padded = (counts + MOE_BLOCK - 1) // MOE_BLOCK * MOE_BLOCK
    pad_end = jnp.cumsum(padded)
    pad_start = pad_end - padded
    blk_start = jnp.arange(n_blocks, dtype=jnp.int32) * MOE_BLOCK
    blk_e = jnp.minimum(jnp.sum((pad_end[None, :] <= blk_start[:, None]).astype(jnp.int32), axis=1),
                        N_EXPERTS - 1)
    onehot = blk_e[:, None] == jnp.arange(N_EXPERTS, dtype=jnp.int32)[None, :]
    c_blk = jnp.sum(jnp.where(onehot, counts[None, :], 0), axis=1)
    s_blk = jnp.sum(jnp.where(onehot, pad_start[None, :], 0), axis=1)
    nvalid = jnp.clip(c_blk - (blk_start - s_blk), 0, MOE_BLOCK).astype(jnp.int32)
    seg = jnp.concatenate([counts, pad_end]).astype(jnp.int32)
    return blk_e.astype(jnp.int32), nvalid, seg, pad_start


def _token_rows(route, pad_start):
    eid = route[:, 0:2].astype(jnp.int32)
    rank = route[:, 4:6].astype(jnp.int32)
    onehot = eid[:, :, None] == jnp.arange(N_EXPERTS, dtype=jnp.int32)[None, None, :]
    return rank + jnp.sum(jnp.where(onehot, pad_start[None, None, :], 0), axis=2)


def _layer(x_p, x_s, c_p, c_s, k_past, v_past, s0_s, conv_s, p):
    bp, tp, d = x_p.shape
    bs, ts, _ = x_s.shape
    n_p, n_s = bp * tp, bs * ts
    n_tok = n_p + n_s

    n_seq = bp + bs
    c_all = jnp.zeros((16, d), F32).at[:n_seq].set(jnp.concatenate([c_s, c_p], axis=0))
    mod = _ada(c_all, p['w_ada'], p['b_ada'])
    mod8 = jnp.pad(mod.reshape(16, 6, d), ((0, 0), (0, 2), (0, 0)))

    w_in = p['w_in']
    o_z, o_b, o_q = CONV_CH, CONV_CH + 512, CONV_CH + 512 + 2 * DN_HEADS
    w_main = jnp.concatenate([w_in[:, :o_b], w_in[:, o_q:]], axis=1).astype(BF16)
    wb = jnp.pad(w_in[:, o_b:o_q], ((0, 0), (0, LANES - 2 * DN_HEADS)))
    wb_hi = wb.astype(BF16)
    wb_lo = (wb - wb_hi.astype(F32)).astype(BF16)
    g_pre_mix = p['g_pre_mix'].reshape(1, d)

    conv_w8 = jnp.pad(p['conv_w'], ((0, 8 - CONV_W), (0, 0)))
    pad_g = lambda a: jnp.pad(a.reshape(1, DN_HEADS), ((0, 0), (DN_HEADS, LANES - 2 * DN_HEADS)))
    alog_row, dtb_row = pad_g(p['a_log']), pad_g(p['dt_bias'])
    onorm_a = p['onorm_a'].reshape(1, DN_DIM)
    onorm_b2 = jnp.tile(p['onorm_b'].reshape(1, SB_DIM), (1, 2))

    w_out16 = p['w_out'].astype(BF16)
    wr = jnp.pad(jnp.concatenate([p['w_router_group'], p['w_router_expert']], axis=1),
                 ((0, 0), (0, LANES - N_GROUPS - N_EXPERTS)))
    wr_hi = wr.astype(BF16)
    wr_lo = (wr - wr_hi.astype(F32)).astype(BF16)
    b_r = jnp.pad(jnp.concatenate([p['b_router_group'], p['b_router_expert']]).reshape(1, -1),
                  ((0, 0), (0, LANES - N_GROUPS - N_EXPERTS)))
    g_post_mix = p['g_post_mix'].reshape(1, d)
    g_pre_ffn = p['g_pre_ffn'].reshape(1, d)
    g_post_ffn = p['g_post_ffn'].reshape(1, d)

    def mixer(x, tm, seq_rows, mod_row0, hist8, s0, k_old, v_old, chunk, n_chunks, bq, n_sub, cnt0):
        b, t, _ = x.shape
        x2d = x.reshape(b * t, d)
        a_in, z, bg, q16, kb, vb, k16, v16 = _proj(x2d, mod8, g_pre_mix, w_main, wb_hi, wb_lo,
                                                    tm, seq_rows, mod_row0)
        r3 = lambda a: a.reshape(b, t, a.shape[-1])
        oa16, s_new = _delta(r3(a_in), r3(z), r3(bg), hist8, s0, conv_w8, alog_row, dtb_row, onorm_a,
                             chunk, n_chunks)
        k16, v16 = r3(k16), r3(v16)
        if k_old is not None:
            k16 = jnp.concatenate([k_old.reshape(b, -1, SB_WIDTH).astype(BF16), k16], axis=1)
            v16 = jnp.concatenate([v_old.reshape(b, -1, SB_WIDTH).astype(BF16), v16], axis=1)
        n_pad = (-k16.shape[1]) % KEY_TILE
        k16 = jnp.pad(k16, ((0, 0), (n_pad, 0), (0, 0)))
        v16 = jnp.pad(v16, ((0, 0), (n_pad, 0), (0, 0)))
        ob16 = _sb_attn(r3(q16), k16, v16, onorm_b2, bq, n_sub, n_pad)
        x1, h2, route, cnt = _post(oa16.reshape(b * t, 512), ob16.reshape(b * t, 512), x2d, mod8,
                                   g_post_mix, g_pre_ffn, w_out16, wr_hi, wr_lo, b_r, cnt0, tm, seq_rows,
                                   mod_row0)
        new_conv = r3(a_in)[:, t - (CONV_W - 1):, :]
        return (x1, h2, route, cnt, kb.reshape(b, t, SB_HEADS, SB_DIM), vb.reshape(b, t, SB_HEADS, SB_DIM),
                s_new, new_conv)

    zero_hist = jnp.zeros((bp, 8, CONV_CH), F32)
    zero_s = jnp.zeros((bp, DN_HEADS, DN_DIM, DN_DIM), F32)
    hist_s = jnp.pad(conv_s, ((0, 0), (8 - (CONV_W - 1), 0), (0, 0)))
    tm_p = min(256, tp)
    nc_p = max(1, min(4, tp // DELTA_BLOCK))
    x1p, h2p, rp, cnt_p, kp, vp, sp, cp = mixer(x_p, tm_p, tp, bs, zero_hist, zero_s, None, None,
                                                 min(DELTA_BLOCK, tp), nc_p, min(KEY_TILE // 2, tp), 2,
                                                 jnp.zeros((1, LANES), F32))
    x1s, h2s, rs, cnt, ks, vs, ss, cs = mixer(x_s, n_s, ts, 0, hist_s, s0_s, k_past, v_past,
                                               min(DELTA_BLOCK, ts), max(1, ts // DELTA_BLOCK),
                                               min(KEY_TILE, ts), 1, cnt_p)

    n_blocks = -(-2 * n_tok // MOE_BLOCK) + N_EXPERTS
    blk_e, nvalid, seg, pad_start = _segment_plan(cnt, n_blocks)
    dest_p = _token_rows(rp, pad_start)
    dest_s = _token_rows(rs, pad_start)
    x_sorted = _dispatch(seg, dest_p, dest_s, h2p, h2s, n_blocks, tm_p)
    y_sorted = _moe(blk_e, nvalid, x_sorted, p['w_gate'], p['w_up'], p['w_down'])
    y_p = _combine(dest_p, rp, y_sorted, x1p, mod8, g_post_ffn, tm_p, tp, bs).reshape(bp, tp, d)
    y_s = _combine(dest_s, rs, y_sorted, x1s, mod8, g_post_ffn, n_s, ts, 0).reshape(bs, ts, d)
    return y_p, y_s, kp, vp, sp, cp, ks, vs, ss, cs


def kernel(x_prompt, x_sample, c_prompt, c_sample, cache_k, cache_v, state_delta, state_conv, w_ada, b_ada, g_pre_mix, g_post_mix, g_pre_ffn, g_post_ffn, w_in, conv_w, a_log, dt_bias, onorm_a, onorm_b, w_out, w_router_group, b_router_group, w_router_expert, b_router_expert, w_gate, w_up, w_down):
    depth = w_in.shape[0]
    y_p, y_s = x_prompt, x_sample
    outs = [[] for _ in range(8)]
    for l in range(depth):
        p = dict(w_ada=w_ada[l], b_ada=b_ada[l], g_pre_mix=g_pre_mix[l], g_post_mix=g_post_mix[l],
                 g_pre_ffn=g_pre_ffn[l], g_post_ffn=g_post_ffn[l], w_in=w_in[l], conv_w=conv_w[l],
                 a_log=a_log[l], dt_bias=dt_bias[l], onorm_a=onorm_a[l], onorm_b=onorm_b[l],
                 w_out=w_out[l], w_router_group=w_router_group[l], b_router_group=b_router_group[l],
                 w_router_expert=w_router_expert[l], b_router_expert=b_router_expert[l],
                 w_gate=w_gate[l], w_up=w_up[l], w_down=w_down[l])
        res = _layer(y_p, y_s, c_prompt, c_sample, cache_k[l], cache_v[l], state_delta[l], state_conv[l], p)
        y_p, y_s = res[0], res[1]
        for lst, r in zip(outs, res[2:]):
            lst.append(r)
    return (y_p, y_s) + tuple(jnp.stack(o) for o in outs)
```

```python
import functools
import math

import jax
import jax.numpy as jnp
from jax import lax
from jax.experimental import pallas as pl
from jax.experimental.pallas import tpu as pltpu

F32 = jnp.float32
BF16 = jnp.bfloat16

D_MODEL = 1024
DN_HEADS = 4
DN_DIM = 128
CONV_W = 4
CONV_CH = DN_HEADS * 3 * DN_DIM
DELTA_BLOCK = 64
SB_HEADS = 8
SB_DIM = 64
SB_WIDTH = SB_HEADS * SB_DIM
N_GROUPS = 4
EXPERTS_PER_GROUP = 8
N_EXPERTS = N_GROUPS * EXPERTS_PER_GROUP
D_EXPERT = D_MODEL // 2
MOE_BLOCK = 256
EPS = 1e-6

LANES = 128
KEY_TILE = 128
ATTN_WINDOW = 3 * KEY_TILE
WINDOW_BLOCKS = 4
EXP_ZERO_BELOW = -104.0
VMEM_LIMIT = 56 * 1024 * 1024
POST_TILE_FACTOR = 2


def _cparams(sem):
    return pltpu.CompilerParams(dimension_semantics=sem, vmem_limit_bytes=VMEM_LIMIT)


def _split(a):
    hi = a.astype(BF16)
    lo = (a - hi.astype(F32)).astype(BF16)
    return hi, lo


def _dot(a, b, dims=(((1,), (0,)), ((), ()))):
    return lax.dot_general(a, b, dims, preferred_element_type=F32)


def _dot_small_int_lhs(a, b):
    a16 = a.astype(BF16)
    return _dot(jnp.concatenate([a16, a16], axis=1), jnp.concatenate(_split(b), axis=0))


_NT = (((1,), (1,)), ((), ()))
_TN = (((0,), (0,)), ((), ()))


def _silu(x):
    return x * jax.nn.sigmoid(x)


def _softplus(x):
    return jnp.maximum(x, 0.0) + jnp.log(1.0 + jnp.exp(-jnp.abs(x)))


def _ada_kernel(c_ref, w_ref, b_ref, o_ref):
    s = _silu(c_ref[...]).astype(BF16)
    o_ref[...] = _dot(s, w_ref[...].astype(BF16)) + b_ref[...]


def _ada(c_all, w_ada, b_ada):
    rows = c_all.shape[0]
    n = w_ada.shape[1]
    tn = 1024
    return pl.pallas_call(
        _ada_kernel,
        grid=(n // tn,),
        in_specs=[pl.BlockSpec((rows, D_MODEL), lambda j: (0, 0)),
                  pl.BlockSpec((D_MODEL, tn), lambda j: (0, j)),
                  pl.BlockSpec((1, tn), lambda j: (0, j))],
        out_specs=pl.BlockSpec((rows, tn), lambda j: (0, j)),
        out_shape=jax.ShapeDtypeStruct((rows, n), F32),
        compiler_params=_cparams(("arbitrary",)),
        name="ada",
    )(c_all, w_ada, b_ada.reshape(1, n))


def _rms(x, gain):
    return x * lax.rsqrt(jnp.mean(x * x, axis=-1, keepdims=True) + EPS) * gain


def _per_seq(y, mod, fn):
    n_seq = mod.shape[0]
    ys = y.reshape(n_seq, y.shape[0] // n_seq, y.shape[1])
    return fn(ys, mod).reshape(y.shape)


def _mod_spec(tm, seq_rows, mod_row0):
    if tm >= seq_rows:
        n_seq = tm // seq_rows
        assert tm % seq_rows == 0 and mod_row0 % n_seq == 0
        return pl.BlockSpec((n_seq, 8, D_MODEL), lambda i: (mod_row0 // n_seq + i, 0, 0))
    assert seq_rows % tm == 0
    return pl.BlockSpec((1, 8, D_MODEL), lambda i: (mod_row0 + (i * tm) // seq_rows, 0, 0))


def _proj_kernel(x_ref, mod_ref, g_ref, wm_ref, wbh_ref,
                 a_ref, z_ref, bg_ref, q_ref, k_ref, v_ref, k16_ref, v16_ref):
    h = _per_seq(_rms(x_ref[...], g_ref[...]), mod_ref[...], lambda y, m: y * (1.0 + m[:, 1:2]) + m[:, 0:1])
    hh, hl = _split(h)
    p = _dot(hh, wm_ref[...])
    a_ref[...] = p[:, 0:CONV_CH]
    z_ref[...] = p[:, CONV_CH:CONV_CH + 512]
    o = CONV_CH + 512
    q_ref[...] = (p[:, o:o + 512] * (SB_DIM ** -0.5)).astype(BF16)
    k = p[:, o + 512:o + 1024]
    v = p[:, o + 1024:o + 1536]
    k_ref[...] = k
    v_ref[...] = v
    k16_ref[...] = k.astype(BF16)
    v16_ref[...] = v.astype(BF16)
    bg_ref[...] = p[:, o + 1536:o + 1536 + LANES] + (_dot(hl, wbh_ref[...]) + p[:, o + 1536 + LANES:])


def _proj(x2d, mod8, g_pre, w_main, wb_hi, tm, seq_rows, mod_row0):
    n = x2d.shape[0]
    nm = w_main.shape[1]
    row = lambda i: (i, 0)
    const = lambda i: (0, 0)
    outs = [(CONV_CH, F32), (512, F32), (LANES, F32), (512, BF16), (512, F32), (512, F32),
            (512, BF16), (512, BF16)]
    return pl.pallas_call(
        _proj_kernel,
        grid=(n // tm,),
        in_specs=[pl.BlockSpec((tm, D_MODEL), row),
                  _mod_spec(tm, seq_rows, mod_row0),
                  pl.BlockSpec((1, D_MODEL), const),
                  pl.BlockSpec((D_MODEL, nm), const),
                  pl.BlockSpec((D_MODEL, LANES), const)],
        out_specs=[pl.BlockSpec((tm, w), row) for w, _ in outs],
        out_shape=[jax.ShapeDtypeStruct((n, w), dt) for w, dt in outs],
        compiler_params=_cparams(("arbitrary",)),
        name="proj",
    )(x2d, mod8, g_pre, w_main, wb_hi)


def _delta_kernel(a_ref, z_ref, bg_ref, hist0_ref, s0_ref, cw_ref, alog_ref, dtb_ref, on_ref,
                  o_ref, sfin_ref, hist_sc, s_sc, *, chunk, n_chunks):
    t_idx = pl.program_id(1)
    tt = chunk * n_chunks

    @pl.when(t_idx == 0)
    def _():
        hist_sc[...] = hist0_ref[0]
        s_sc[...] = s0_ref[0]

    x = a_ref[0]
    xx = jnp.concatenate([hist_sc[...], x], axis=0)
    cw = cw_ref[...]
    conv = x * cw[CONV_W - 1:CONV_W]
    for s in range(1, CONV_W):
        conv = conv + pltpu.roll(xx, s, 0)[8:] * cw[CONV_W - 1 - s:CONV_W - s]
    conv = _silu(conv)
    hist_sc[...] = x[tt - 8:tt]

    bg = bg_ref[0]
    lane = lax.broadcasted_iota(jnp.int32, (1, LANES), 1)
    g_lane = (lane >= DN_HEADS) & (lane < 2 * DN_HEADS)
    neg_a = jnp.where(g_lane, -jnp.exp(alog_ref[...]), 0.0)
    beta_all = jax.nn.sigmoid(bg)
    g_all = neg_a * _softplus(bg + dtb_ref[...])

    ri = lax.broadcasted_iota(jnp.int32, (chunk, chunk), 0)
    ci = lax.broadcasted_iota(jnp.int32, (chunk, chunk), 1)
    incl = ri >= ci
    strict = ri > ci
    tri = incl.astype(F32)
    eye = (ri == ci).astype(F32)
    ones_cc = jnp.ones((chunk, chunk), F32)
    onorm = on_ref[...]

    chunks = range(n_chunks)
    units = [(c, h) for c in chunks for h in range(DN_HEADS)]
    rows = lambda c: slice(c * chunk, (c + 1) * chunk)

    gc_all = [_dot_small_int_lhs(tri, g_all[rows(c)]) for c in chunks]
    diag = [jnp.concatenate([eye * gc_all[c][:, DN_HEADS + h:DN_HEADS + h + 1] for h in range(DN_HEADS)], axis=1)
            for c in chunks]
    gc_rows = [_dot_small_int_lhs(ones_cc, diag[c]) for c in chunks]

    q, k, v, beta, gcol, glast, eg, decay, kbeta = {}, {}, {}, {}, {}, {}, {}, {}, {}
    for (c, h) in units:
        lo = h * DN_DIM
        qq = conv[rows(c), lo:lo + DN_DIM]
        kk = conv[rows(c), 512 + lo:512 + lo + DN_DIM]
        u_ = (c, h)
        q[u_] = qq * lax.rsqrt(jnp.sum(qq * qq, axis=-1, keepdims=True) + EPS) * (DN_DIM ** -0.5)
        k[u_] = kk * lax.rsqrt(jnp.sum(kk * kk, axis=-1, keepdims=True) + EPS)
        v[u_] = conv[rows(c), 1024 + lo:1024 + lo + DN_DIM]
        beta[u_] = beta_all[rows(c), h:h + 1]
        gcol[u_] = gc_all[c][:, DN_HEADS + h:DN_HEADS + h + 1]
        glast[u_] = gcol[u_][chunk - 1:chunk, :]
        eg[u_] = jnp.exp(gcol[u_])
        grow = gc_rows[c][:, h * chunk:(h + 1) * chunk]
        decay[u_] = jnp.where(incl, jnp.exp(jnp.minimum(gcol[u_] - grow, 0.0)), 0.0)
        kbeta[u_] = k[u_] * beta[u_]

    k16 = {u_: k[u_].astype(BF16) for u_ in units}
    kq = {u_: _dot(jnp.concatenate([kbeta[u_], q[u_]], axis=0).astype(BF16), k16[u_], _NT) for u_ in units}
    lower = {u_: jnp.where(strict, kq[u_][:chunk] * decay[u_], 0.0) for u_ in units}
    attn16 = {u_: jnp.where(incl, kq[u_][chunk:] * decay[u_], 0.0).astype(BF16) for u_ in units}

    sol = {u_: jnp.concatenate([v[u_] * beta[u_], kbeta[u_] * eg[u_]], axis=1) for u_ in units}
    lp = lower
    p = 1
    while p < chunk:
        lsp = {u_: _split(lp[u_]) for u_ in units}
        ssp = {u_: _split(sol[u_]) for u_ in units}
        lcat = {u_: jnp.concatenate([lsp[u_][0], lsp[u_][1], lsp[u_][0]], axis=1) for u_ in units}
        upd = {u_: _dot(lcat[u_], jnp.concatenate([ssp[u_][0], ssp[u_][0], ssp[u_][1]], axis=0)) for u_ in units}
        sol = {u_: (sol[u_] - upd[u_]) if p == 1 else (sol[u_] + upd[u_]) for u_ in units}
        if 2 * p < chunk:
            lp = {u_: _dot(lcat[u_], jnp.concatenate([lsp[u_][0], lsp[u_][0], lsp[u_][1]], axis=0))
                  for u_ in units}
        p *= 2

    usol = {u_: sol[u_][:, :DN_DIM] for u_ in units}
    wq16 = {u_: jnp.concatenate([sol[u_][:, DN_DIM:], q[u_] * eg[u_]], axis=0).astype(BF16) for u_ in units}
    kd16 = {u_: (k[u_] * jnp.exp(glast[u_] - gcol[u_])).astype(BF16) for u_ in units}

    for c in chunks:
        hs = range(DN_HEADS)
        s_old = [s_sc[h] for h in hs]
        ws = [_dot(wq16[(c, h)], s_old[h].astype(BF16)) for h in hs]
        v_new = [(usol[(c, h)] - ws[h][:chunk]).astype(BF16) for h in hs]
        o_in = [_dot(attn16[(c, h)], v_new[h]) for h in hs]
        ds = [_dot(kd16[(c, h)], v_new[h], _TN) for h in hs]
        for h in hs:
            lo = h * DN_DIM
            s_sc[h] = s_old[h] * jnp.exp(glast[(c, h)]) + ds[h]
            o = ws[h][chunk:] + o_in[h]
            zg = z_ref[0, rows(c), lo:lo + DN_DIM]
            o_ref[0, rows(c), lo:lo + DN_DIM] = (_rms(o, onorm) * _silu(zg)).astype(BF16)

    @pl.when(t_idx == pl.num_programs(1) - 1)
    def _():
        sfin_ref[0] = s_sc[...]


def _delta(a_in, z, bg, hist8, s0, conv_w8, alog_row, dtb_row, onorm_a, chunk, n_chunks):
    b, t, _ = a_in.shape
    tt = chunk * n_chunks
    tile = lambda bi, ti: (bi, ti, 0)
    per_b3 = lambda bi, ti: (bi, 0, 0)
    per_b4 = lambda bi, ti: (bi, 0, 0, 0)
    const = lambda bi, ti: (0, 0)
    kern = functools.partial(_delta_kernel, chunk=chunk, n_chunks=n_chunks)
    return pl.pallas_call(
        kern,
        grid=(b, t // tt),
        in_specs=[pl.BlockSpec((1, tt, CONV_CH), tile),
                  pl.BlockSpec((1, tt, 512), tile),
                  pl.BlockSpec((1, tt, LANES), tile),
                  pl.BlockSpec((1, 8, CONV_CH), per_b3),
                  pl.BlockSpec((1, DN_HEADS, DN_DIM, DN_DIM), per_b4),
                  pl.BlockSpec((8, CONV_CH), const),
                  pl.BlockSpec((1, LANES), const),
                  pl.BlockSpec((1, LANES), const),
                  pl.BlockSpec((1, DN_DIM), const)],
        out_specs=[pl.BlockSpec((1, tt, 512), tile),
                   pl.BlockSpec((1, DN_HEADS, DN_DIM, DN_DIM), per_b4)],
        out_shape=[jax.ShapeDtypeStruct((b, t, 512), BF16),
                   jax.ShapeDtypeStruct((b, DN_HEADS, DN_DIM, DN_DIM), F32)],
        scratch_shapes=[pltpu.VMEM((8, CONV_CH), F32),
                        pltpu.VMEM((DN_HEADS, DN_DIM, DN_DIM), F32)],
        compiler_params=_cparams(("arbitrary", "arbitrary")),
        name="delta",
    )(a_in, z, bg, hist8, s0, conv_w8, alog_row, dtb_row, onorm_a)


def _sb_kernel(q_ref, *refs, bq, n_sub, n_pad, q_off):
    kwin = refs[:WINDOW_BLOCKS]
    vwin = refs[WINDOW_BLOCKS:2 * WINDOW_BLOCKS]
    k_hbm, v_hbm, on_ref, o_ref, kbuf, vbuf, qsel, acc, carry, sem = refs[2 * WINDOW_BLOCKS:]
    b = pl.program_id(0)
    i = pl.program_id(1)
    g = bq * n_sub
    qend_step = q_off + (i + 1) * g
    win_start = qend_step - WINDOW_BLOCKS * KEY_TILE
    n_pairs = SB_HEADS // 2
    heads = range(SB_HEADS)

    half_lane = lax.broadcasted_iota(jnp.int32, (bq, LANES), 1) < SB_DIM
    rj = lax.broadcasted_iota(jnp.int32, (2 * KEY_TILE, 2 * KEY_TILE), 0) % KEY_TILE
    cj = lax.broadcasted_iota(jnp.int32, (2 * KEY_TILE, 2 * KEY_TILE), 1)
    suffix2 = ((rj > cj) | (cj >= KEY_TILE)).astype(BF16)

    def window(blocks, off, p):
        parts = []
        for blk in range(WINDOW_BLOCKS):
            lo, hi = max(off, blk * KEY_TILE), min(off + ATTN_WINDOW, (blk + 1) * KEY_TILE)
            if lo < hi:
                ref = blocks[WINDOW_BLOCKS - 1 - blk]
                parts.append(ref[0, lo - blk * KEY_TILE:hi - blk * KEY_TILE, p * LANES:(p + 1) * LANES])
        return jnp.concatenate(parts, axis=0)

    n_t = ATTN_WINDOW // KEY_TILE
    row = lax.broadcasted_iota(jnp.int32, (bq, ATTN_WINDOW), 0)
    col = lax.broadcasted_iota(jnp.int32, (bq, ATTN_WINDOW), 1)
    causal = col < row + (ATTN_WINDOW - bq)
    subs = range(n_sub)
    offs = [WINDOW_BLOCKS * KEY_TILE - g + (s + 1) * bq - ATTN_WINDOW for s in subs]
    done = []
    for s in subs:
        mask = causal & (col >= n_pad - (win_start + offs[s]))
        zs = {}
        for p in range(n_pairs):
            qf = q_ref[0, s * bq:(s + 1) * bq, p * LANES:(p + 1) * LANES].astype(F32)
            qq = jnp.concatenate([jnp.where(half_lane, qf, 0.0), jnp.where(half_lane, 0.0, qf)], axis=0)
            zz = _dot(qq.astype(BF16), window(kwin, offs[s], p), _NT)
            zs[2 * p], zs[2 * p + 1] = zz[:bq], zz[bq:]
        sp = {h: _softplus(zs[h]) for h in heads}
        pieces = []
        for h in heads:
            hi, lo = _split(jnp.where(mask, -sp[h], 0.0))
            for j in range(n_t):
                c0 = ATTN_WINDOW - (j + 1) * KEY_TILE
                pieces.append(jnp.concatenate([hi[:, c0:c0 + KEY_TILE], lo[:, c0:c0 + KEY_TILE]], axis=1))
        cs_all = _dot(jnp.concatenate(pieces, axis=0), suffix2)
        worst = jnp.full((bq, KEY_TILE), -jnp.inf, F32)
        a = {}
        for h in heads:
            run = None
            cols = []
            for j in range(n_t):
                r0 = (h * n_t + j) * bq
                cs = cs_all[r0:r0 + bq]
                cols.append(cs[:, :KEY_TILE] if run is None else cs[:, :KEY_TILE] + run)
                run = cs[:, KEY_TILE:] if run is None else run + cs[:, KEY_TILE:]
            within = jnp.concatenate(cols[::-1], axis=1)
            a[h] = jnp.where(mask, jnp.exp((zs[h] - sp[h]) + within), 0.0).astype(BF16)
            carry[s, h] = run
            worst = jnp.maximum(worst, run)
        for p in range(n_pairs):
            pv = _dot(jnp.concatenate([a[2 * p], a[2 * p + 1]], axis=0), window(vwin, offs[s], p))
            acc[s, p] = jnp.where(half_lane, pv[:bq], pv[bq:])
        done.append((jnp.max(worst) < EXP_ZERO_BELOW).astype(jnp.int32))

    rj1 = lax.broadcasted_iota(jnp.int32, (KEY_TILE, 2 * KEY_TILE), 0)
    cj1 = lax.broadcasted_iota(jnp.int32, (KEY_TILE, 2 * KEY_TILE), 1)
    suffix1 = ((rj1 > cj1) | (cj1 >= KEY_TILE)).astype(BF16)
    col1 = lax.broadcasted_iota(jnp.int32, (bq, KEY_TILE), 1)
    for s in subs:
        swept_from = win_start + offs[s]

        @pl.when((done[s] == 0) & (swept_from > n_pad))
        def _():
            for p in range(n_pairs):
                qf = q_ref[0, s * bq:(s + 1) * bq, p * LANES:(p + 1) * LANES].astype(F32)
                qsel[2 * p] = jnp.where(half_lane, qf, 0.0).astype(BF16)
                qsel[2 * p + 1] = jnp.where(half_lane, 0.0, qf).astype(BF16)

            def body(state):
                upper, _ = state
                start = jnp.maximum(upper - KEY_TILE, 0)
                copies = []
                for p in range(n_pairs):
                    for src, dst in ((k_hbm, kbuf), (v_hbm, vbuf)):
                        cp = pltpu.make_async_copy(
                            src.at[b, pl.ds(pl.multiple_of(start, 16), KEY_TILE), pl.ds(p * LANES, LANES)],
                            dst.at[p], sem)
                        cp.start()
                        copies.append(cp)
                for cp in copies:
                    cp.wait()
                kpos = start + col1
                m1 = (kpos < upper) & (kpos >= n_pad)
                zz = [_dot(qsel[h], kbuf[h // 2], _NT) for h in heads]
                spp = [_softplus(zz[h]) for h in heads]
                pcs = []
                for h in heads:
                    pcs.extend(_split(jnp.where(m1, -spp[h], 0.0)))
                cs1 = _dot(jnp.concatenate(pcs, axis=0), suffix1)
                worst = jnp.full((bq, KEY_TILE), -jnp.inf, F32)
                aa = []
                for h in heads:
                    c = cs1[2 * h * bq:(2 * h + 1) * bq] + cs1[(2 * h + 1) * bq:(2 * h + 2) * bq]
                    c_old = carry[s, h]
                    aa.append(jnp.where(m1, jnp.exp((zz[h] - spp[h]) + c[:, :KEY_TILE] + c_old), 0.0).astype(BF16))
                    c_new = c_old + c[:, KEY_TILE:]
                    carry[s, h] = c_new
                    worst = jnp.maximum(worst, c_new)
                for p in range(n_pairs):
                    pv0 = _dot(aa[2 * p], vbuf[p])
                    pv1 = _dot(aa[2 * p + 1], vbuf[p])
                    acc[s, p] = acc[s, p] + jnp.where(half_lane, pv0, pv1)
                return start, (jnp.max(worst) < EXP_ZERO_BELOW).astype(jnp.int32)

            lax.while_loop(lambda st: (st[0] > n_pad) & (st[1] == 0), body, (swept_from, jnp.int32(0)))

    onb = on_ref[...]
    for s in subs:
        for p in range(n_pairs):
            o = acc[s, p]
            sq = o * o
            s_lo = jnp.sum(jnp.where(half_lane, sq, 0.0), axis=-1, keepdims=True)
            s_hi = jnp.sum(jnp.where(half_lane, 0.0, sq), axis=-1, keepdims=True)
            ms = jnp.where(half_lane, s_lo, s_hi) * (1.0 / SB_DIM)
            o_ref[0, s * bq:(s + 1) * bq, p * LANES:(p + 1) * LANES] = (o * lax.rsqrt(ms + EPS) * onb).astype(BF16)


def _sb_attn(q16, k16p, v16p, onorm_b2, bq, n_sub, n_pad):
    b, tq, _ = q16.shape
    tkp = k16p.shape[1]
    q_off = tkp - tq
    g = bq * n_sub
    assert tq % g == 0 and g <= KEY_TILE and bq % 16 == 0
    assert all((q_off + (i + 1) * g) % KEY_TILE == 0 for i in range(tq // g))

    def kmap(back):
        def f(bi, i):
            last = (q_off + (i + 1) * g) // KEY_TILE - 1
            return (bi, jnp.maximum(last - back, 0), 0)
        return f

    qmap = lambda bi, i: (bi, i, 0)
    kern = functools.partial(_sb_kernel, bq=bq, n_sub=n_sub, n_pad=n_pad, q_off=q_off)
    kspec = [pl.BlockSpec((1, KEY_TILE, SB_WIDTH), kmap(back)) for back in range(WINDOW_BLOCKS)]
    return pl.pallas_call(
        kern,
        grid=(b, tq // g),
        in_specs=[pl.BlockSpec((1, g, SB_WIDTH), qmap)] + kspec + kspec
                 + [pl.BlockSpec(memory_space=pl.ANY), pl.BlockSpec(memory_space=pl.ANY),
                    pl.BlockSpec((1, LANES), lambda bi, i: (0, 0))],
        out_specs=pl.BlockSpec((1, g, SB_WIDTH), qmap),
        out_shape=jax.ShapeDtypeStruct((b, tq, SB_WIDTH), BF16),
        scratch_shapes=[pltpu.VMEM((SB_HEADS // 2, KEY_TILE, LANES), BF16),
                        pltpu.VMEM((SB_HEADS // 2, KEY_TILE, LANES), BF16),
                        pltpu.VMEM((SB_HEADS, bq, LANES), BF16),
                        pltpu.VMEM((n_sub, SB_HEADS // 2, bq, LANES), F32),
                        pltpu.VMEM((n_sub, SB_HEADS, bq, KEY_TILE), F32),
                        pltpu.SemaphoreType.DMA(())],
        compiler_params=_cparams(("arbitrary", "arbitrary")),
        name="sb_attn",
    )(q16, *([k16p] * WINDOW_BLOCKS), *([v16p] * WINDOW_BLOCKS), k16p, v16p, onorm_b2)


def _post_kernel(oa_ref, ob_ref, x_ref, mod_ref, gpm_ref, gpf_ref, wo_ref, wrh_ref, wrl_ref, br_ref, cnt0_ref,
                 x1_ref, h2_ref, route_ref, cnt_ref):
    @pl.when(pl.program_id(0) == 0)
    def _():
        cnt_ref[...] = cnt0_ref[...]

    mod = mod_ref[...]
    mix = _dot(oa_ref[...], wo_ref[0:512, :]) + _dot(ob_ref[...], wo_ref[512:1024, :])
    x1 = x_ref[...] + _per_seq(_rms(mix, gpm_ref[...]), mod, lambda y, m: y * m[:, 2:3])
    x1_ref[...] = x1
    h2 = _per_seq(_rms(x1, gpf_ref[...]), mod, lambda y, m: y * (1.0 + m[:, 4:5]) + m[:, 3:4])
    h2_ref[...] = h2
    hh, hl = _split(h2)
    wrh = wrh_ref[...]
    logits = _dot(hh, wrh) + (_dot(hl, wrh) + _dot(hh, wrl_ref[...])) + br_ref[...]
    lane = lax.broadcasted_iota(jnp.int32, logits.shape, 1).astype(F32)
    neg = -jnp.inf
    nl = float(LANES)
    lg = jnp.where(lane < N_GROUPS, logits, neg)
    gmax = jnp.max(lg, axis=-1, keepdims=True)
    grp = jnp.min(jnp.where(lg == gmax, lane, nl), axis=-1, keepdims=True)
    p_grp = 1.0 / jnp.sum(jnp.exp(lg - gmax), axis=-1, keepdims=True)
    first = N_GROUPS + grp * EXPERTS_PER_GROUP
    le = jnp.where((lane >= first) & (lane < first + EXPERTS_PER_GROUP), logits, neg)
    emax = jnp.max(le, axis=-1, keepdims=True)
    i1 = jnp.min(jnp.where(le == emax, lane, nl), axis=-1, keepdims=True)
    esum = jnp.sum(jnp.exp(le - emax), axis=-1, keepdims=True)
    le2 = jnp.where(lane == i1, neg, le)
    e2max = jnp.max(le2, axis=-1, keepdims=True)
    i2 = jnp.min(jnp.where(le2 == e2max, lane, nl), axis=-1, keepdims=True)
    p1 = 1.0 / esum
    p2 = jnp.exp(e2max - emax) / esum
    w1 = p_grp * p1 / (p1 + p2)
    w2 = p_grp * p2 / (p1 + p2)
    e1 = i1 - N_GROUPS
    e2 = i2 - N_GROUPS
    hot1 = (lane == e1).astype(F32)
    hot2 = (lane == e2).astype(F32)
    both = hot1 + hot2
    tm = logits.shape[0]
    ti = lax.broadcasted_iota(jnp.int32, (tm, tm), 0)
    tj = lax.broadcasted_iota(jnp.int32, (tm, tm), 1)
    earlier = _dot((ti > tj).astype(BF16), both.astype(BF16)) + cnt_ref[...]
    rank1 = jnp.sum(hot1 * earlier, axis=-1, keepdims=True)
    rank2 = jnp.sum(hot2 * (earlier + hot1), axis=-1, keepdims=True)
    cnt_ref[...] = cnt_ref[...] + jnp.sum(both, axis=0, keepdims=True)
    out = jnp.where(lane == 0.0, e1, 0.0)
    out = jnp.where(lane == 1.0, e2, out)
    out = jnp.where(lane == 2.0, w1, out)
    out = jnp.where(lane == 3.0, w2, out)
    out = jnp.where(lane == 4.0, rank1, out)
    out = jnp.where(lane == 5.0, rank2, out)
    route_ref[...] = out


def _post(oa16, ob16, x2d, mod8, g_post_mix, g_pre_ffn, w_out16, wr_hi, wr_lo, b_r, cnt0, tm, seq_rows,
          mod_row0):
    n = x2d.shape[0]
    row = lambda i: (i, 0)
    const = lambda i: (0, 0)
    return pl.pallas_call(
        _post_kernel,
        grid=(n // tm,),
        in_specs=[pl.BlockSpec((tm, 512), row),
                  pl.BlockSpec((tm, 512), row),
                  pl.BlockSpec((tm, D_MODEL), row),
                  _mod_spec(tm, seq_rows, mod_row0),
                  pl.BlockSpec((1, D_MODEL), const),
                  pl.BlockSpec((1, D_MODEL), const),
                  pl.BlockSpec((D_MODEL, D_MODEL), const),
                  pl.BlockSpec((D_MODEL, LANES), const),
                  pl.BlockSpec((D_MODEL, LANES), const),
                  pl.BlockSpec((1, LANES), const),
                  pl.BlockSpec((1, LANES), const)],
        out_specs=[pl.BlockSpec((tm, D_MODEL), row),
                   pl.BlockSpec((tm, D_MODEL), row),
                   pl.BlockSpec((tm, LANES), row),
                   pl.BlockSpec((1, LANES), const)],
        out_shape=[jax.ShapeDtypeStruct((n, D_MODEL), F32),
                   jax.ShapeDtypeStruct((n, D_MODEL), F32),
                   jax.ShapeDtypeStruct((n, LANES), F32),
                   jax.ShapeDtypeStruct((1, LANES), F32)],
        compiler_params=_cparams(("arbitrary",)),
        name="post",
    )(oa16, ob16, x2d, mod8, g_post_mix, g_pre_ffn, w_out16, wr_hi, wr_lo, b_r, cnt0)


def _dispatch_kernel(seg_ref, dp_ref, ds_ref, hp_ref, hs_ref, xs_hbm, zbuf, sem, *, n_blocks):
    i = pl.program_id(0)
    last_step = pl.num_programs(0) - 1

    @pl.when(i == 0)
    def _():
        zbuf[...] = jnp.zeros_like(zbuf)

        def zero_block(row0):
            return pltpu.make_async_copy(zbuf, xs_hbm.at[pl.ds(pl.multiple_of(row0, MOE_BLOCK), MOE_BLOCK), :], sem)

        for e in range(N_EXPERTS):
            @pl.when(seg_ref[e] > 0)
            def _():
                zero_block(seg_ref[N_EXPERTS + e] - MOE_BLOCK).start()
        for e in range(N_EXPERTS):
            @pl.when(seg_ref[e] > 0)
            def _():
                zero_block(seg_ref[N_EXPERTS + e] - MOE_BLOCK).wait()

        used = seg_ref[2 * N_EXPERTS - 1] // MOE_BLOCK

        def fill(b, c):
            cp = zero_block(b * MOE_BLOCK)
            cp.start()
            cp.wait()
            return c

        lax.fori_loop(used, n_blocks, fill, 0)

    def scatter(h_ref, dest_ref):
        rows = h_ref.shape[0]
        for t in range(rows):
            for slot in range(2):
                pltpu.make_async_copy(h_ref.at[pl.ds(t, 1), :],
                                      xs_hbm.at[pl.ds(dest_ref[0, 0, 2 * t + slot], 1), :],
                                      sem).start(priority=slot)
        for slot in range(2):
            pltpu.make_async_copy(h_ref, xs_hbm.at[pl.ds(0, rows), :], sem).wait()

    @pl.when(i < last_step)
    def _():
        scatter(hp_ref, dp_ref)

    @pl.when(i == last_step)
    def _():
        scatter(hs_ref, ds_ref)


def _dispatch(seg, dest_p, dest_s, h2p, h2s, n_blocks, tm):
    n_p, n_s = h2p.shape[0], h2s.shape[0]
    steps_p = n_p // tm
    pmap3 = lambda i, sg: (jnp.minimum(i, steps_p - 1), 0, 0)
    pmap2 = lambda i, sg: (jnp.minimum(i, steps_p - 1), 0)
    grid_spec = pltpu.PrefetchScalarGridSpec(
        num_scalar_prefetch=1,
        grid=(steps_p + 1,),
        in_specs=[pl.BlockSpec((1, 1, 2 * tm), pmap3, memory_space=pltpu.SMEM),
                  pl.BlockSpec((1, 1, 2 * n_s), lambda i, sg: (0, 0, 0), memory_space=pltpu.SMEM),
                  pl.BlockSpec((tm, D_MODEL), pmap2),
                  pl.BlockSpec((n_s, D_MODEL), lambda i, sg: (0, 0))],
        out_specs=pl.BlockSpec(memory_space=pl.ANY),
        scratch_shapes=[pltpu.VMEM((MOE_BLOCK, D_MODEL), F32), pltpu.SemaphoreType.DMA(())])
    return pl.pallas_call(
        functools.partial(_dispatch_kernel, n_blocks=n_blocks),
        grid_spec=grid_spec,
        out_shape=jax.ShapeDtypeStruct((n_blocks * MOE_BLOCK, D_MODEL), F32),
        compiler_params=_cparams(("arbitrary",)),
        name="dispatch",
    )(seg, dest_p.reshape(steps_p, 1, 2 * tm), dest_s.reshape(1, 1, 2 * n_s), h2p, h2s)


def _moe_kernel(blk_e_ref, nvalid_ref, x_ref, wg_ref, wu_ref, wd_ref, y_ref, wg16, wu16, wd16):
    i = pl.program_id(0)
    e = blk_e_ref[i]
    e_prev = blk_e_ref[jnp.maximum(i - 1, 0)]

    @pl.when((i == 0) | (e != e_prev))
    def _():
        wg16[...] = wg_ref[0].astype(BF16)
        wu16[...] = wu_ref[0].astype(BF16)
        wd16[...] = wd_ref[0].astype(BF16)

    @pl.when(nvalid_ref[i] > 0)
    def _():
        xb = x_ref[...].astype(BF16)
        g = _dot(xb, wg16[...])
        u = _dot(xb, wu16[...])
        hmid = (_silu(g) * u).astype(BF16)
        y_ref[...] = _dot(hmid, wd16[...])

    @pl.when(nvalid_ref[i] == 0)
    def _():
        y_ref[...] = jnp.zeros_like(y_ref)


def _moe(blk_e, nvalid, x_sorted, w_gate, w_up, w_down):
    n_blocks = blk_e.shape[0]
    wmap = lambda i, be, nv: (be[i], 0, 0)
    xmap = lambda i, be, nv: (jnp.where(nv[i] > 0, i, 0), 0)
    grid_spec = pltpu.PrefetchScalarGridSpec(
        num_scalar_prefetch=2,
        grid=(n_blocks,),
        in_specs=[pl.BlockSpec((MOE_BLOCK, D_MODEL), xmap),
                  pl.BlockSpec((1, D_MODEL, D_EXPERT), wmap),
                  pl.BlockSpec((1, D_MODEL, D_EXPERT), wmap),
                  pl.BlockSpec((1, D_EXPERT, D_MODEL), wmap)],
        out_specs=pl.BlockSpec((MOE_BLOCK, D_MODEL), lambda i, be, nv: (i, 0)),
        scratch_shapes=[pltpu.VMEM((D_MODEL, D_EXPERT), BF16),
                        pltpu.VMEM((D_MODEL, D_EXPERT), BF16),
                        pltpu.VMEM((D_EXPERT, D_MODEL), BF16)])
    return pl.pallas_call(
        _moe_kernel,
        grid_spec=grid_spec,
        out_shape=jax.ShapeDtypeStruct((n_blocks * MOE_BLOCK, D_MODEL), F32),
        compiler_params=_cparams(("arbitrary",)),
        name="moe",
    )(blk_e, nvalid, x_sorted, w_gate, w_up, w_down)


def _combine_kernel(dcur_ref, dnext_ref, route_ref, x1_ref, mod_ref, g_ref, y_hbm, o_ref, ybuf, sems):
    i = pl.program_id(0)
    last = pl.num_programs(0) - 1
    tm = x1_ref.shape[0]

    def gather(dest_ref, buf):
        for t in range(tm):
            for slot in range(2):
                pltpu.make_async_copy(y_hbm.at[pl.ds(dest_ref[0, 0, 2 * t + slot], 1), :],
                                      ybuf.at[buf, slot, pl.ds(t, 1), :], sems.at[buf]).start(priority=slot)

    @pl.when(i == 0)
    def _():
        gather(dcur_ref, 0)

    for buf in range(2):
        @pl.when(i % 2 == buf)
        def _():
            @pl.when(i < last)
            def _():
                gather(dnext_ref, 1 - buf)

            for slot in range(2):
                pltpu.make_async_copy(y_hbm.at[pl.ds(0, tm), :], ybuf.at[buf, slot], sems.at[buf]).wait()
            route = route_ref[...]
            moe = ybuf[buf, 0] * route[:, 2:3] + ybuf[buf, 1] * route[:, 3:4]
            o_ref[...] = x1_ref[...] + _per_seq(_rms(moe, g_ref[...]), mod_ref[...], lambda y, m: y * m[:, 5:6])


def _combine(dest, route, y_sorted, x1, mod8, g_post_ffn, tm, seq_rows, mod_row0):
    n = x1.shape[0]
    steps = n // tm
    row = lambda i: (i, 0)
    dest3 = dest.reshape(steps, 1, 2 * tm)
    return pl.pallas_call(
        _combine_kernel,
        grid=(steps,),
        in_specs=[pl.BlockSpec((1, 1, 2 * tm), lambda i: (i, 0, 0), memory_space=pltpu.SMEM),
                  pl.BlockSpec((1, 1, 2 * tm), lambda i: (jnp.minimum(i + 1, steps - 1), 0, 0),
                               memory_space=pltpu.SMEM),
                  pl.BlockSpec((tm, LANES), row),
                  pl.BlockSpec((tm, D_MODEL), row),
                  _mod_spec(tm, seq_rows, mod_row0),
                  pl.BlockSpec((1, D_MODEL), lambda i: (0, 0)),
                  pl.BlockSpec(memory_space=pl.ANY)],
        out_specs=pl.BlockSpec((tm, D_MODEL), row),
        out_shape=jax.ShapeDtypeStruct((n, D_MODEL), F32),
        scratch_shapes=[pltpu.VMEM((2, 2, tm, D_MODEL), F32), pltpu.SemaphoreType.DMA((2,))],
        compiler_params=_cparams(("arbitrary",)),
        name="combine",
    )(dest3, dest3, route, x1, mod8, g_post_ffn, y_sorted)


def _segment_plan(counts_f, n_blocks):
    counts = counts_f[0, :N_EXPERTS].astype(jnp.int32)
    padded = (counts + MOE_BLOCK - 1) // MOE_BLOCK * MOE_BLOCK
    pad_end = jnp.cumsum(padded)
    pad_start = pad_end - padded
    blk_start = jnp.arange(n_blocks, dtype=jnp.int32) * MOE_BLOCK
    blk_e = jnp.minimum(jnp.sum((pad_end[None, :] <= blk_start[:, None]).astype(jnp.int32), axis=1),
                        N_EXPERTS - 1)
    onehot = blk_e[:, None] == jnp.arange(N_EXPERTS, dtype=jnp.int32)[None, :]
    c_blk = jnp.sum(jnp.where(onehot, counts[None, :], 0), axis=1)
    s_blk = jnp.sum(jnp.where(onehot, pad_start[None, :], 0), axis=1)
    nvalid = jnp.clip(c_blk - (blk_start - s_blk), 0, MOE_BLOCK).astype(jnp.int32)
    seg = jnp.concatenate([counts, pad_end]).astype(jnp.int32)
    return blk_e.astype(jnp.int32), nvalid, seg, pad_start


def _token_rows(route, pad_start):
    eid = route[:, 0:2].astype(jnp.int32)
    rank = route[:, 4:6].astype(jnp.int32)
    onehot = eid[:, :, None] == jnp.arange(N_EXPERTS, dtype=jnp.int32)[None, None, :]
    return rank + jnp.sum(jnp.where(onehot, pad_start[None, None, :], 0), axis=2)


def _layer(x_p, x_s, c_p, c_s, k_past, v_past, s0_s, conv_s, p):
    bp, tp, d = x_p.shape
    bs, ts, _ = x_s.shape
    n_p, n_s = bp * tp, bs * ts
    n_tok = n_p + n_s

    n_seq = bp + bs
    c_all = jnp.zeros((16, d), F32).at[:n_seq].set(jnp.concatenate([c_s, c_p], axis=0))
    mod = _ada(c_all, p['w_ada'], p['b_ada'])
    mod8 = jnp.pad(mod.reshape(16, 6, d), ((0, 0), (0, 2), (0, 0)))

    w_in = p['w_in']
    o_z, o_b, o_q = CONV_CH, CONV_CH + 512, CONV_CH + 512 + 2 * DN_HEADS
    wb = jnp.pad(w_in[:, o_b:o_q], ((0, 0), (0, LANES - 2 * DN_HEADS)))
    wb_hi = wb.astype(BF16)
    wb_lo = (wb - wb_hi.astype(F32)).astype(BF16)
    w_main = jnp.concatenate([w_in[:, :o_b].astype(BF16), w_in[:, o_q:].astype(BF16), wb_hi, wb_lo], axis=1)
    g_pre_mix = p['g_pre_mix'].reshape(1, d)

    conv_w8 = jnp.pad(p['conv_w'], ((0, 8 - CONV_W), (0, 0)))
    pad_g = lambda a: jnp.pad(a.reshape(1, DN_HEADS), ((0, 0), (DN_HEADS, LANES - 2 * DN_HEADS)))
    alog_row, dtb_row = pad_g(p['a_log']), pad_g(p['dt_bias'])
    onorm_a = p['onorm_a'].reshape(1, DN_DIM)
    onorm_b2 = jnp.tile(p['onorm_b'].reshape(1, SB_DIM), (1, 2))

    w_out16 = p['w_out'].astype(BF16)
    wr = jnp.pad(jnp.concatenate([p['w_router_group'], p['w_router_expert']], axis=1),
                 ((0, 0), (0, LANES - N_GROUPS - N_EXPERTS)))
    wr_hi = wr.astype(BF16)
    wr_lo = (wr - wr_hi.astype(F32)).astype(BF16)
    b_r = jnp.pad(jnp.concatenate([p['b_router_group'], p['b_router_expert']]).reshape(1, -1),
                  ((0, 0), (0, LANES - N_GROUPS - N_EXPERTS)))
    g_post_mix = p['g_post_mix'].reshape(1, d)
    g_pre_ffn = p['g_pre_ffn'].reshape(1, d)
    g_post_ffn = p['g_post_ffn'].reshape(1, d)

    def mixer(x, tm, tm_post, seq_rows, mod_row0, hist8, s0, k_old, v_old, chunk, n_chunks, bq, n_sub, cnt0):
        b, t, _ = x.shape
        x2d = x.reshape(b * t, d)
        a_in, z, bg, q16, kb, vb, k16, v16 = _proj(x2d, mod8, g_pre_mix, w_main, wb_hi, tm, seq_rows, mod_row0)
        r3 = lambda a: a.reshape(b, t, a.shape[-1])
        oa16, s_new = _delta(r3(a_in), r3(z), r3(bg), hist8, s0, conv_w8, alog_row, dtb_row, onorm_a,
                             chunk, n_chunks)
        k16, v16 = r3(k16), r3(v16)
        if k_old is not None:
            k16 = jnp.concatenate([k_old.reshape(b, -1, SB_WIDTH).astype(BF16), k16], axis=1)
            v16 = jnp.concatenate([v_old.reshape(b, -1, SB_WIDTH).astype(BF16), v16], axis=1)
        n_pad = (-k16.shape[1]) % KEY_TILE
        k16 = jnp.pad(k16, ((0, 0), (n_pad, 0), (0, 0)))
        v16 = jnp.pad(v16, ((0, 0), (n_pad, 0), (0, 0)))
        ob16 = _sb_attn(r3(q16), k16, v16, onorm_b2, bq, n_sub, n_pad)
        x1, h2, route, cnt = _post(oa16.reshape(b * t, 512), ob16.reshape(b * t, 512), x2d, mod8,
                                   g_post_mix, g_pre_ffn, w_out16, wr_hi, wr_lo, b_r, cnt0, tm_post, seq_rows,
                                   mod_row0)
        new_conv = r3(a_in)[:, t - (CONV_W - 1):, :]
        return (x1, h2, route, cnt, kb.reshape(b, t, SB_HEADS, SB_DIM), vb.reshape(b, t, SB_HEADS, SB_DIM),
                s_new, new_conv)

    zero_hist = jnp.zeros((bp, 8, CONV_CH), F32)
    zero_s = jnp.zeros((bp, DN_HEADS, DN_DIM, DN_DIM), F32)
    hist_s = jnp.pad(conv_s, ((0, 0), (8 - (CONV_W - 1), 0), (0, 0)))
    tm_p = min(256, tp)
    tm_post = tm_p * POST_TILE_FACTOR if n_p % (tm_p * POST_TILE_FACTOR) == 0 else tm_p
    nc_p = max(1, min(4, tp // DELTA_BLOCK))
    x1p, h2p, rp, cnt_p, kp, vp, sp, cp = mixer(x_p, tm_p, tm_post, tp, bs, zero_hist, zero_s, None, None,
                                                 min(DELTA_BLOCK, tp), nc_p, min(KEY_TILE // 2, tp), 2,
                                                 jnp.zeros((1, LANES), F32))
    x1s, h2s, rs, cnt, ks, vs, ss, cs = mixer(x_s, n_s, n_s, ts, 0, hist_s, s0_s, k_past, v_past,
                                               min(DELTA_BLOCK, ts), max(1, ts // DELTA_BLOCK),
                                               min(KEY_TILE, ts), 1, cnt_p)

    n_blocks = -(-2 * n_tok // MOE_BLOCK) + N_EXPERTS
    blk_e, nvalid, seg, pad_start = _segment_plan(cnt, n_blocks)
    dest_p = _token_rows(rp, pad_start)
    dest_s = _token_rows(rs, pad_start)
    x_sorted = _dispatch(seg, dest_p, dest_s, h2p, h2s, n_blocks, tm_p)
    y_sorted = _moe(blk_e, nvalid, x_sorted, p['w_gate'], p['w_up'], p['w_down'])
    y_p = _combine(dest_p, rp, y_sorted, x1p, mod8, g_post_ffn, tm_p, tp, bs).reshape(bp, tp, d)
    y_s = _combine(dest_s, rs, y_sorted, x1s, mod8, g_post_ffn, n_s, ts, 0).reshape(bs, ts, d)
    return y_p, y_s, kp, vp, sp, cp, ks, vs, ss, cs


def kernel(x_prompt, x_sample, c_prompt, c_sample, cache_k, cache_v, state_delta, state_conv, w_ada, b_ada, g_pre_mix, g_post_mix, g_pre_ffn, g_post_ffn, w_in, conv_w, a_log, dt_bias, onorm_a, onorm_b, w_out, w_router_group, b_router_group, w_router_expert, b_router_expert, w_gate, w_up, w_down):
    depth = w_in.shape[0]
    y_p, y_s = x_prompt, x_sample
    outs = [[] for _ in range(8)]
    for l in range(depth):
        p = dict(w_ada=w_ada[l], b_ada=b_ada[l], g_pre_mix=g_pre_mix[l], g_post_mix=g_post_mix[l],
                 g_pre_ffn=g_pre_ffn[l], g_post_ffn=g_post_ffn[l], w_in=w_in[l], conv_w=conv_w[l],
                 a_log=a_log[l], dt_bias=dt_bias[l], onorm_a=onorm_a[l], onorm_b=onorm_b[l],
                 w_out=w_out[l], w_router_group=w_router_group[l], b_router_group=b_router_group[l],
                 w_router_expert=w_router_expert[l], b_router_expert=b_router_expert[l],
                 w_gate=w_gate[l], w_up=w_up[l], w_down=w_down[l])
        res = _layer(y_p, y_s, c_prompt, c_sample, cache_k[l], cache_v[l], state_delta[l], state_conv[l], p)
        y_p, y_s = res[0], res[1]
        for lst, r in zip(outs, res[2:]):
            lst.append(r)
    return (y_p, y_s) + tuple(jnp.stack(o) for o in outs)
```

```python
import functools
import math

import jax
import jax.numpy as jnp
from jax import lax
from jax.experimental import pallas as pl
from jax.experimental.pallas import tpu as pltpu

F32 = jnp.float32
BF16 = jnp.bfloat16

D_MODEL = 1024
DN_HEADS = 4
DN_DIM = 128
CONV_W = 4
CONV_CH = DN_HEADS * 3 * DN_DIM
DELTA_BLOCK = 64
SB_HEADS = 8
SB_DIM = 64
SB_WIDTH = SB_HEADS * SB_DIM
N_GROUPS = 4
EXPERTS_PER_GROUP = 8
N_EXPERTS = N_GROUPS * EXPERTS_PER_GROUP
D_EXPERT = D_MODEL // 2
MOE_BLOCK = 256
EPS = 1e-6

LANES = 128
KEY_TILE = 128
ATTN_WINDOW = 3 * KEY_TILE
WINDOW_BLOCKS = 4
EXP_ZERO_BELOW = -104.0
VMEM_LIMIT = 56 * 1024 * 1024
POST_TILE_FACTOR = 2


def _cparams(sem):
    return pltpu.CompilerParams(dimension_semantics=sem, vmem_limit_bytes=VMEM_LIMIT)


def _split(a):
    hi = a.astype(BF16)
    lo = (a - hi.astype(F32)).astype(BF16)
    return hi, lo


def _dot(a, b, dims=(((1,), (0,)), ((), ()))):
    return lax.dot_general(a, b, dims, preferred_element_type=F32)


def _dot_small_int_lhs(a, b):
    a16 = a.astype(BF16)
    return _dot(jnp.concatenate([a16, a16], axis=1), jnp.concatenate(_split(b), axis=0))


_NT = (((1,), (1,)), ((), ()))
_TN = (((0,), (0,)), ((), ()))


def _silu(x):
    return x * jax.nn.sigmoid(x)


def _softplus(x):
    return jnp.maximum(x, 0.0) + jnp.log(1.0 + jnp.exp(-jnp.abs(x)))


def _ada_kernel(c_ref, w_ref, b_ref, o_ref):
    s = _silu(c_ref[...]).astype(BF16)
    o_ref[...] = _dot(s, w_ref[...].astype(BF16)) + b_ref[...]


def _ada(c_all, w_ada, b_ada):
    rows = c_all.shape[0]
    n = w_ada.shape[1]
    tn = 1024
    return pl.pallas_call(
        _ada_kernel,
        grid=(n // tn,),
        in_specs=[pl.BlockSpec((rows, D_MODEL), lambda j: (0, 0)),
                  pl.BlockSpec((D_MODEL, tn), lambda j: (0, j)),
                  pl.BlockSpec((1, tn), lambda j: (0, j))],
        out_specs=pl.BlockSpec((rows, tn), lambda j: (0, j)),
        out_shape=jax.ShapeDtypeStruct((rows, n), F32),
        compiler_params=_cparams(("arbitrary",)),
        name="ada",
    )(c_all, w_ada, b_ada.reshape(1, n))


def _rms(x, gain):
    return x * lax.rsqrt(jnp.mean(x * x, axis=-1, keepdims=True) + EPS) * gain


def _per_seq(y, mod, fn):
    n_seq = mod.shape[0]
    ys = y.reshape(n_seq, y.shape[0] // n_seq, y.shape[1])
    return fn(ys, mod).reshape(y.shape)


def _mod_spec(tm, seq_rows, mod_row0):
    if tm >= seq_rows:
        n_seq = tm // seq_rows
        assert tm % seq_rows == 0 and mod_row0 % n_seq == 0
        return pl.BlockSpec((n_seq, 8, D_MODEL), lambda i: (mod_row0 // n_seq + i, 0, 0))
    assert seq_rows % tm == 0
    return pl.BlockSpec((1, 8, D_MODEL), lambda i: (mod_row0 + (i * tm) // seq_rows, 0, 0))


def _proj_kernel(x_ref, mod_ref, g_ref, wm_ref, wbh_ref,
                 a_ref, z_ref, bg_ref, q_ref, k_ref, v_ref, k16_ref, v16_ref):
    h = _per_seq(_rms(x_ref[...], g_ref[...]), mod_ref[...], lambda y, m: y * (1.0 + m[:, 1:2]) + m[:, 0:1])
    hh, hl = _split(h)
    p = _dot(hh, wm_ref[...])
    a_ref[...] = p[:, 0:CONV_CH]
    z_ref[...] = p[:, CONV_CH:CONV_CH + 512]
    o = CONV_CH + 512
    q_ref[...] = (p[:, o:o + 512] * (SB_DIM ** -0.5)).astype(BF16)
    k = p[:, o + 512:o + 1024]
    v = p[:, o + 1024:o + 1536]
    k_ref[...] = k
    v_ref[...] = v
    k16_ref[...] = k.astype(BF16)
    v16_ref[...] = v.astype(BF16)
    bg_ref[...] = p[:, o + 1536:o + 1536 + LANES] + (_dot(hl, wbh_ref[...]) + p[:, o + 1536 + LANES:])


def _proj(x2d, mod8, g_pre, w_main, wb_hi, tm, seq_rows, mod_row0):
    n = x2d.shape[0]
    nm = w_main.shape[1]
    row = lambda i: (i, 0)
    const = lambda i: (0, 0)
    outs = [(CONV_CH, F32), (512, F32), (LANES, F32), (512, BF16), (512, F32), (512, F32),
            (512, BF16), (512, BF16)]
    return pl.pallas_call(
        _proj_kernel,
        grid=(n // tm,),
        in_specs=[pl.BlockSpec((tm, D_MODEL), row),
                  _mod_spec(tm, seq_rows, mod_row0),
                  pl.BlockSpec((1, D_MODEL), const),
                  pl.BlockSpec((D_MODEL, nm), const),
                  pl.BlockSpec((D_MODEL, LANES), const)],
        out_specs=[pl.BlockSpec((tm, w), row) for w, _ in outs],
        out_shape=[jax.ShapeDtypeStruct((n, w), dt) for w, dt in outs],
        compiler_params=_cparams(("arbitrary",)),
        name="proj",
    )(x2d, mod8, g_pre, w_main, wb_hi)


def _delta_kernel(a_ref, z_ref, bg_ref, hist0_ref, s0_ref, cw_ref, alog_ref, dtb_ref, on_ref,
                  o_ref, sfin_ref, hist_sc, s_sc, *, chunk, n_chunks):
    t_idx = pl.program_id(1)
    tt = chunk * n_chunks

    @pl.when(t_idx == 0)
    def _():
        hist_sc[...] = hist0_ref[0]
        s_sc[...] = s0_ref[0]

    x = a_ref[0]
    xx = jnp.concatenate([hist_sc[...], x], axis=0)
    cw = cw_ref[...]
    conv = x * cw[CONV_W - 1:CONV_W]
    for s in range(1, CONV_W):
        conv = conv + pltpu.roll(xx, s, 0)[8:] * cw[CONV_W - 1 - s:CONV_W - s]
    conv = _silu(conv)
    hist_sc[...] = x[tt - 8:tt]

    bg = bg_ref[0]
    lane = lax.broadcasted_iota(jnp.int32, (1, LANES), 1)
    g_lane = (lane >= DN_HEADS) & (lane < 2 * DN_HEADS)
    neg_a = jnp.where(g_lane, -jnp.exp(alog_ref[...]), 0.0)
    beta_all = jax.nn.sigmoid(bg)
    g_all = neg_a * _softplus(bg + dtb_ref[...])

    ri = lax.broadcasted_iota(jnp.int32, (chunk, chunk), 0)
    ci = lax.broadcasted_iota(jnp.int32, (chunk, chunk), 1)
    incl = ri >= ci
    strict = ri > ci
    tri = incl.astype(F32)
    eye = (ri == ci).astype(F32)
    ones_cc = jnp.ones((chunk, chunk), F32)
    onorm = on_ref[...]

    chunks = range(n_chunks)
    units = [(c, h) for c in chunks for h in range(DN_HEADS)]
    rows = lambda c: slice(c * chunk, (c + 1) * chunk)

    gc_all = [_dot_small_int_lhs(tri, g_all[rows(c)]) for c in chunks]
    diag = [jnp.concatenate([eye * gc_all[c][:, DN_HEADS + h:DN_HEADS + h + 1] for h in range(DN_HEADS)], axis=1)
            for c in chunks]
    gc_rows = [_dot_small_int_lhs(ones_cc, diag[c]) for c in chunks]

    q, k, v, beta, gcol, glast, eg, decay, kbeta = {}, {}, {}, {}, {}, {}, {}, {}, {}
    for (c, h) in units:
        lo = h * DN_DIM
        qq = conv[rows(c), lo:lo + DN_DIM]
        kk = conv[rows(c), 512 + lo:512 + lo + DN_DIM]
        u_ = (c, h)
        q[u_] = qq * lax.rsqrt(jnp.sum(qq * qq, axis=-1, keepdims=True) + EPS) * (DN_DIM ** -0.5)
        k[u_] = kk * lax.rsqrt(jnp.sum(kk * kk, axis=-1, keepdims=True) + EPS)
        v[u_] = conv[rows(c), 1024 + lo:1024 + lo + DN_DIM]
        beta[u_] = beta_all[rows(c), h:h + 1]
        gcol[u_] = gc_all[c][:, DN_HEADS + h:DN_HEADS + h + 1]
        glast[u_] = gcol[u_][chunk - 1:chunk, :]
        eg[u_] = jnp.exp(gcol[u_])
        grow = gc_rows[c][:, h * chunk:(h + 1) * chunk]
        decay[u_] = jnp.where(incl, jnp.exp(jnp.minimum(gcol[u_] - grow, 0.0)), 0.0)
        kbeta[u_] = k[u_] * beta[u_]

    k16 = {u_: k[u_].astype(BF16) for u_ in units}
    kq = {u_: _dot(jnp.concatenate([kbeta[u_], q[u_]], axis=0).astype(BF16), k16[u_], _NT) for u_ in units}
    lower = {u_: jnp.where(strict, kq[u_][:chunk] * decay[u_], 0.0) for u_ in units}
    attn16 = {u_: jnp.where(incl, kq[u_][chunk:] * decay[u_], 0.0).astype(BF16) for u_ in units}

    sol = {u_: jnp.concatenate([v[u_] * beta[u_], kbeta[u_] * eg[u_]], axis=1) for u_ in units}
    lp = lower
    p = 1
    while p < chunk:
        lsp = {u_: _split(lp[u_]) for u_ in units}
        ssp = {u_: _split(sol[u_]) for u_ in units}
        lcat = {u_: jnp.concatenate([lsp[u_][0], lsp[u_][1], lsp[u_][0]], axis=1) for u_ in units}
        upd = {u_: _dot(lcat[u_], jnp.concatenate([ssp[u_][0], ssp[u_][0], ssp[u_][1]], axis=0)) for u_ in units}
        sol = {u_: (sol[u_] - upd[u_]) if p == 1 else (sol[u_] + upd[u_]) for u_ in units}
        if 2 * p < chunk:
            lp = {u_: _dot(lcat[u_], jnp.concatenate([lsp[u_][0], lsp[u_][0], lsp[u_][1]], axis=0))
                  for u_ in units}
        p *= 2

    usol = {u_: sol[u_][:, :DN_DIM] for u_ in units}
    wq16 = {u_: jnp.concatenate([sol[u_][:, DN_DIM:], q[u_] * eg[u_]], axis=0).astype(BF16) for u_ in units}
    kd16 = {u_: (k[u_] * jnp.exp(glast[u_] - gcol[u_])).astype(BF16) for u_ in units}

    for c in chunks:
        hs = range(DN_HEADS)
        s_old = [s_sc[h] for h in hs]
        ws = [_dot(wq16[(c, h)], s_old[h].astype(BF16)) for h in hs]
        v_new = [(usol[(c, h)] - ws[h][:chunk]).astype(BF16) for h in hs]
        o_in = [_dot(attn16[(c, h)], v_new[h]) for h in hs]
        ds = [_dot(kd16[(c, h)], v_new[h], _TN) for h in hs]
        for h in hs:
            lo = h * DN_DIM
            s_sc[h] = s_old[h] * jnp.exp(glast[(c, h)]) + ds[h]
            o = ws[h][chunk:] + o_in[h]
            zg = z_ref[0, rows(c), lo:lo + DN_DIM]
            o_ref[0, rows(c), lo:lo + DN_DIM] = (_rms(o, onorm) * _silu(zg)).astype(BF16)

    @pl.when(t_idx == pl.num_programs(1) - 1)
    def _():
        sfin_ref[0] = s_sc[...]


def _delta(a_in, z, bg, hist8, s0, conv_w8, alog_row, dtb_row, onorm_a, chunk, n_chunks):
    b, t, _ = a_in.shape
    tt = chunk * n_chunks
    tile = lambda bi, ti: (bi, ti, 0)
    per_b3 = lambda bi, ti: (bi, 0, 0)
    per_b4 = lambda bi, ti: (bi, 0, 0, 0)
    const = lambda bi, ti: (0, 0)
    kern = functools.partial(_delta_kernel, chunk=chunk, n_chunks=n_chunks)
    return pl.pallas_call(
        kern,
        grid=(b, t // tt),
        in_specs=[pl.BlockSpec((1, tt, CONV_CH), tile),
                  pl.BlockSpec((1, tt, 512), tile),
                  pl.BlockSpec((1, tt, LANES), tile),
                  pl.BlockSpec((1, 8, CONV_CH), per_b3),
                  pl.BlockSpec((1, DN_HEADS, DN_DIM, DN_DIM), per_b4),
                  pl.BlockSpec((8, CONV_CH), const),
                  pl.BlockSpec((1, LANES), const),
                  pl.BlockSpec((1, LANES), const),
                  pl.BlockSpec((1, DN_DIM), const)],
        out_specs=[pl.BlockSpec((1, tt, 512), tile),
                   pl.BlockSpec((1, DN_HEADS, DN_DIM, DN_DIM), per_b4)],
        out_shape=[jax.ShapeDtypeStruct((b, t, 512), BF16),
                   jax.ShapeDtypeStruct((b, DN_HEADS, DN_DIM, DN_DIM), F32)],
        scratch_shapes=[pltpu.VMEM((8, CONV_CH), F32),
                        pltpu.VMEM((DN_HEADS, DN_DIM, DN_DIM), F32)],
        compiler_params=_cparams(("arbitrary", "arbitrary")),
        name="delta",
    )(a_in, z, bg, hist8, s0, conv_w8, alog_row, dtb_row, onorm_a)


def _sb_kernel(q_ref, *refs, bq, n_sub, n_pad, q_off):
    kwin = refs[:WINDOW_BLOCKS]
    vwin = refs[WINDOW_BLOCKS:2 * WINDOW_BLOCKS]
    k_hbm, v_hbm, on_ref, o_ref, kbuf, vbuf, qsel, acc, carry, sem = refs[2 * WINDOW_BLOCKS:]
    b = pl.program_id(0)
    i = pl.program_id(1)
    g = bq * n_sub
    qend_step = q_off + (i + 1) * g
    win_start = qend_step - WINDOW_BLOCKS * KEY_TILE
    n_pairs = SB_HEADS // 2
    heads = range(SB_HEADS)

    half_lane = lax.broadcasted_iota(jnp.int32, (bq, LANES), 1) < SB_DIM
    rj = lax.broadcasted_iota(jnp.int32, (2 * KEY_TILE, 2 * KEY_TILE), 0) % KEY_TILE
    cj = lax.broadcasted_iota(jnp.int32, (2 * KEY_TILE, 2 * KEY_TILE), 1)
    suffix2 = ((rj > cj) | (cj >= KEY_TILE)).astype(BF16)

    def window(blocks, off, p):
        parts = []
        for blk in range(WINDOW_BLOCKS):
            lo, hi = max(off, blk * KEY_TILE), min(off + ATTN_WINDOW, (blk + 1) * KEY_TILE)
            if lo < hi:
                ref = blocks[WINDOW_BLOCKS - 1 - blk]
                parts.append(ref[0, lo - blk * KEY_TILE:hi - blk * KEY_TILE, p * LANES:(p + 1) * LANES])
        return jnp.concatenate(parts, axis=0)

    n_t = ATTN_WINDOW // KEY_TILE
    row = lax.broadcasted_iota(jnp.int32, (bq, ATTN_WINDOW), 0)
    col = lax.broadcasted_iota(jnp.int32, (bq, ATTN_WINDOW), 1)
    causal = col < row + (ATTN_WINDOW - bq)
    subs = range(n_sub)
    offs = [WINDOW_BLOCKS * KEY_TILE - g + (s + 1) * bq - ATTN_WINDOW for s in subs]
    done = []
    for s in subs:
        mask = causal & (col >= n_pad - (win_start + offs[s]))
        zs = {}
        for p in range(n_pairs):
            qf = q_ref[0, s * bq:(s + 1) * bq, p * LANES:(p + 1) * LANES].astype(F32)
            qq = jnp.concatenate([jnp.where(half_lane, qf, 0.0), jnp.where(half_lane, 0.0, qf)], axis=0)
            zz = _dot(qq.astype(BF16), window(kwin, offs[s], p), _NT)
            zs[2 * p], zs[2 * p + 1] = zz[:bq], zz[bq:]
        sp = {h: _softplus(zs[h]) for h in heads}
        pieces = []
        for h in heads:
            hi, lo = _split(jnp.where(mask, -sp[h], 0.0))
            for j in range(n_t):
                c0 = ATTN_WINDOW - (j + 1) * KEY_TILE
                pieces.append(jnp.concatenate([hi[:, c0:c0 + KEY_TILE], lo[:, c0:c0 + KEY_TILE]], axis=1))
        cs_all = _dot(jnp.concatenate(pieces, axis=0), suffix2)
        worst = jnp.full((bq, KEY_TILE), -jnp.inf, F32)
        a = {}
        for h in heads:
            run = None
            cols = []
            for j in range(n_t):
                r0 = (h * n_t + j) * bq
                cs = cs_all[r0:r0 + bq]
                cols.append(cs[:, :KEY_TILE] if run is None else cs[:, :KEY_TILE] + run)
                run = cs[:, KEY_TILE:] if run is None else run + cs[:, KEY_TILE:]
            within = jnp.concatenate(cols[::-1], axis=1)
            a[h] = jnp.where(mask, jnp.exp((zs[h] - sp[h]) + within), 0.0).astype(BF16)
            carry[s, h] = run
            worst = jnp.maximum(worst, run)
        for p in range(n_pairs):
            pv = _dot(jnp.concatenate([a[2 * p], a[2 * p + 1]], axis=0), window(vwin, offs[s], p))
            acc[s, p] = jnp.where(half_lane, pv[:bq], pv[bq:])
        done.append((jnp.max(worst) < EXP_ZERO_BELOW).astype(jnp.int32))

    rj1 = lax.broadcasted_iota(jnp.int32, (KEY_TILE, 2 * KEY_TILE), 0)
    cj1 = lax.broadcasted_iota(jnp.int32, (KEY_TILE, 2 * KEY_TILE), 1)
    suffix1 = ((rj1 > cj1) | (cj1 >= KEY_TILE)).astype(BF16)
    col1 = lax.broadcasted_iota(jnp.int32, (bq, KEY_TILE), 1)
    for s in subs:
        swept_from = win_start + offs[s]

        @pl.when((done[s] == 0) & (swept_from > n_pad))
        def _():
            for p in range(n_pairs):
                qf = q_ref[0, s * bq:(s + 1) * bq, p * LANES:(p + 1) * LANES].astype(F32)
                qsel[2 * p] = jnp.where(half_lane, qf, 0.0).astype(BF16)
                qsel[2 * p + 1] = jnp.where(half_lane, 0.0, qf).astype(BF16)

            def body(state):
                upper, _ = state
                start = jnp.maximum(upper - KEY_TILE, 0)
                copies = []
                for p in range(n_pairs):
                    for src, dst in ((k_hbm, kbuf), (v_hbm, vbuf)):
                        cp = pltpu.make_async_copy(
                            src.at[b, pl.ds(pl.multiple_of(start, 16), KEY_TILE), pl.ds(p * LANES, LANES)],
                            dst.at[p], sem)
                        cp.start()
                        copies.append(cp)
                for cp in copies:
                    cp.wait()
                kpos = start + col1
                m1 = (kpos < upper) & (kpos >= n_pad)
                zz = [_dot(qsel[h], kbuf[h // 2], _NT) for h in heads]
                spp = [_softplus(zz[h]) for h in heads]
                pcs = []
                for h in heads:
                    pcs.extend(_split(jnp.where(m1, -spp[h], 0.0)))
                cs1 = _dot(jnp.concatenate(pcs, axis=0), suffix1)
                worst = jnp.full((bq, KEY_TILE), -jnp.inf, F32)
                aa = []
                for h in heads:
                    c = cs1[2 * h * bq:(2 * h + 1) * bq] + cs1[(2 * h + 1) * bq:(2 * h + 2) * bq]
                    c_old = carry[s, h]
                    aa.append(jnp.where(m1, jnp.exp((zz[h] - spp[h]) + c[:, :KEY_TILE] + c_old), 0.0).astype(BF16))
                    c_new = c_old + c[:, KEY_TILE:]
                    carry[s, h] = c_new
                    worst = jnp.maximum(worst, c_new)
                for p in range(n_pairs):
                    pv0 = _dot(aa[2 * p], vbuf[p])
                    pv1 = _dot(aa[2 * p + 1], vbuf[p])
                    acc[s, p] = acc[s, p] + jnp.where(half_lane, pv0, pv1)
                return start, (jnp.max(worst) < EXP_ZERO_BELOW).astype(jnp.int32)

            lax.while_loop(lambda st: (st[0] > n_pad) & (st[1] == 0), body, (swept_from, jnp.int32(0)))

    onb = on_ref[...]
    for s in subs:
        for p in range(n_pairs):
            o = acc[s, p]
            sq = o * o
            s_lo = jnp.sum(jnp.where(half_lane, sq, 0.0), axis=-1, keepdims=True)
            s_hi = jnp.sum(jnp.where(half_lane, 0.0, sq), axis=-1, keepdims=True)
            ms = jnp.where(half_lane, s_lo, s_hi) * (1.0 / SB_DIM)
            o_ref[0, s * bq:(s + 1) * bq, p * LANES:(p + 1) * LANES] = (o * lax.rsqrt(ms + EPS) * onb).astype(BF16)


def _sb_attn(q16, k16p, v16p, onorm_b2, bq, n_sub, n_pad):
    b, tq, _ = q16.shape
    tkp = k16p.shape[1]
    q_off = tkp - tq
    g = bq * n_sub
    assert tq % g == 0 and g <= KEY_TILE and bq % 16 == 0
    assert all((q_off + (i + 1) * g) % KEY_TILE == 0 for i in range(tq // g))

    def kmap(back):
        def f(bi, i):
            last = (q_off + (i + 1) * g) // KEY_TILE - 1
            return (bi, jnp.maximum(last - back, 0), 0)
        return f

    qmap = lambda bi, i: (bi, i, 0)
    kern = functools.partial(_sb_kernel, bq=bq, n_sub=n_sub, n_pad=n_pad, q_off=q_off)
    kspec = [pl.BlockSpec((1, KEY_TILE, SB_WIDTH), kmap(back)) for back in range(WINDOW_BLOCKS)]
    return pl.pallas_call(
        kern,
        grid=(b, tq // g),
        in_specs=[pl.BlockSpec((1, g, SB_WIDTH), qmap)] + kspec + kspec
                 + [pl.BlockSpec(memory_space=pl.ANY), pl.BlockSpec(memory_space=pl.ANY),
                    pl.BlockSpec((1, LANES), lambda bi, i: (0, 0))],
        out_specs=pl.BlockSpec((1, g, SB_WIDTH), qmap),
        out_shape=jax.ShapeDtypeStruct((b, tq, SB_WIDTH), BF16),
        scratch_shapes=[pltpu.VMEM((SB_HEADS // 2, KEY_TILE, LANES), BF16),
                        pltpu.VMEM((SB_HEADS // 2, KEY_TILE, LANES), BF16),
                        pltpu.VMEM((SB_HEADS, bq, LANES), BF16),
                        pltpu.VMEM((n_sub, SB_HEADS // 2, bq, LANES), F32),
                        pltpu.VMEM((n_sub, SB_HEADS, bq, KEY_TILE), F32),
                        pltpu.SemaphoreType.DMA(())],
        compiler_params=_cparams(("arbitrary", "arbitrary")),
        name="sb_attn",
    )(q16, *([k16p] * WINDOW_BLOCKS), *([v16p] * WINDOW_BLOCKS), k16p, v16p, onorm_b2)


def _post_kernel(oa_ref, ob_ref, x_ref, mod_ref, gpm_ref, gpf_ref, wo_ref, wrh_ref, wrl_ref, br_ref, cnt0_ref,
                 x1_ref, h2_ref, route_ref, cnt_ref):
    @pl.when(pl.program_id(0) == 0)
    def _():
        cnt_ref[...] = cnt0_ref[...]

    mod = mod_ref[...]
    mix = _dot(oa_ref[...], wo_ref[0:512, :]) + _dot(ob_ref[...], wo_ref[512:1024, :])
    x1 = x_ref[...] + _per_seq(_rms(mix, gpm_ref[...]), mod, lambda y, m: y * m[:, 2:3])
    x1_ref[...] = x1
    h2 = _per_seq(_rms(x1, gpf_ref[...]), mod, lambda y, m: y * (1.0 + m[:, 4:5]) + m[:, 3:4])
    h2_ref[...] = h2
    hh, hl = _split(h2)
    wrh = wrh_ref[...]
    logits = _dot(hh, wrh) + (_dot(hl, wrh) + _dot(hh, wrl_ref[...])) + br_ref[...]
    lane = lax.broadcasted_iota(jnp.int32, logits.shape, 1).astype(F32)
    neg = -jnp.inf
    nl = float(LANES)
    lg = jnp.where(lane < N_GROUPS, logits, neg)
    gmax = jnp.max(lg, axis=-1, keepdims=True)
    grp = jnp.min(jnp.where(lg == gmax, lane, nl), axis=-1, keepdims=True)
    p_grp = 1.0 / jnp.sum(jnp.exp(lg - gmax), axis=-1, keepdims=True)
    first = N_GROUPS + grp * EXPERTS_PER_GROUP
    le = jnp.where((lane >= first) & (lane < first + EXPERTS_PER_GROUP), logits, neg)
    emax = jnp.max(le, axis=-1, keepdims=True)
    i1 = jnp.min(jnp.where(le == emax, lane, nl), axis=-1, keepdims=True)
    esum = jnp.sum(jnp.exp(le - emax), axis=-1, keepdims=True)
    le2 = jnp.where(lane == i1, neg, le)
    e2max = jnp.max(le2, axis=-1, keepdims=True)
    i2 = jnp.min(jnp.where(le2 == e2max, lane, nl), axis=-1, keepdims=True)
    p1 = 1.0 / esum
    p2 = jnp.exp(e2max - emax) / esum
    w1 = p_grp * p1 / (p1 + p2)
    w2 = p_grp * p2 / (p1 + p2)
    e1 = i1 - N_GROUPS
    e2 = i2 - N_GROUPS
    hot1 = (lane == e1).astype(F32)
    hot2 = (lane == e2).astype(F32)
    both = hot1 + hot2
    tm = logits.shape[0]
    ti = lax.broadcasted_iota(jnp.int32, (tm, tm), 0)
    tj = lax.broadcasted_iota(jnp.int32, (tm, tm), 1)
    earlier = _dot((ti > tj).astype(BF16), both.astype(BF16)) + cnt_ref[...]
    rank1 = jnp.sum(hot1 * earlier, axis=-1, keepdims=True)
    rank2 = jnp.sum(hot2 * (earlier + hot1), axis=-1, keepdims=True)
    cnt_ref[...] = cnt_ref[...] + jnp.sum(both, axis=0, keepdims=True)
    out = jnp.where(lane == 0.0, e1, 0.0)
    out = jnp.where(lane == 1.0, e2, out)
    out = jnp.where(lane == 2.0, w1, out)
    out = jnp.where(lane == 3.0, w2, out)
    out = jnp.where(lane == 4.0, rank1, out)
    out = jnp.where(lane == 5.0, rank2, out)
    route_ref[...] = out


def _post(oa16, ob16, x2d, mod8, g_post_mix, g_pre_ffn, w_out16, wr_hi, wr_lo, b_r, cnt0, tm, seq_rows,
          mod_row0):
    n = x2d.shape[0]
    row = lambda i: (i, 0)
    const = lambda i: (0, 0)
    return pl.pallas_call(
        _post_kernel,
        grid=(n // tm,),
        in_specs=[pl.BlockSpec((tm, 512), row),
                  pl.BlockSpec((tm, 512), row),
                  pl.BlockSpec((tm, D_MODEL), row),
                  _mod_spec(tm, seq_rows, mod_row0),
                  pl.BlockSpec((1, D_MODEL), const),
                  pl.BlockSpec((1, D_MODEL), const),
                  pl.BlockSpec((D_MODEL, D_MODEL), const),
                  pl.BlockSpec((D_MODEL, LANES), const),
                  pl.BlockSpec((D_MODEL, LANES), const),
                  pl.BlockSpec((1, LANES), const),
                  pl.BlockSpec((1, LANES), const)],
        out_specs=[pl.BlockSpec((tm, D_MODEL), row),
                   pl.BlockSpec((tm, D_MODEL), row),
                   pl.BlockSpec((tm, LANES), row),
                   pl.BlockSpec((1, LANES), const)],
        out_shape=[jax.ShapeDtypeStruct((n, D_MODEL), F32),
                   jax.ShapeDtypeStruct((n, D_MODEL), F32),
                   jax.ShapeDtypeStruct((n, LANES), F32),
                   jax.ShapeDtypeStruct((1, LANES), F32)],
        compiler_params=_cparams(("arbitrary",)),
        name="post",
    )(oa16, ob16, x2d, mod8, g_post_mix, g_pre_ffn, w_out16, wr_hi, wr_lo, b_r, cnt0)


def _dispatch_kernel(seg_ref, dp_ref, ds_ref, hp_ref, hs_ref, xs_hbm, zbuf, stage, sem, stage_sems, *,
                     n_blocks, n_prompt_steps):
    i = pl.program_id(0)
    last_step = pl.num_programs(0) - 1

    @pl.when(i == 0)
    def _():
        zbuf[...] = jnp.zeros_like(zbuf)

        def zero_block(row0):
            return pltpu.make_async_copy(zbuf, xs_hbm.at[pl.ds(pl.multiple_of(row0, MOE_BLOCK), MOE_BLOCK), :], sem)

        for e in range(N_EXPERTS):
            @pl.when(seg_ref[e] > 0)
            def _():
                zero_block(seg_ref[N_EXPERTS + e] - MOE_BLOCK).start()
        for e in range(N_EXPERTS):
            @pl.when(seg_ref[e] > 0)
            def _():
                zero_block(seg_ref[N_EXPERTS + e] - MOE_BLOCK).wait()

        used = seg_ref[2 * N_EXPERTS - 1] // MOE_BLOCK

        def fill(b, c):
            cp = zero_block(b * MOE_BLOCK)
            cp.start()
            cp.wait()
            return c

        lax.fori_loop(used, n_blocks, fill, 0)

    def scatter(h_ref, dest_ref, sem_):
        for t in range(h_ref.shape[0]):
            for slot in range(2):
                pltpu.make_async_copy(h_ref.at[pl.ds(t, 1), :],
                                      xs_hbm.at[pl.ds(dest_ref[0, 0, 2 * t + slot], 1), :],
                                      sem_).start(priority=slot)

    def drain(h_ref, sem_):
        for slot in range(2):
            pltpu.make_async_copy(h_ref, xs_hbm.at[pl.ds(0, h_ref.shape[0]), :], sem_).wait()

    for buf in range(2):
        @pl.when((i < last_step) & (i % 2 == buf))
        def _():
            @pl.when(i >= 2)
            def _():
                drain(stage.at[buf], stage_sems.at[buf])
            stage[buf] = hp_ref[...]
            scatter(stage.at[buf], dp_ref, stage_sems.at[buf])

    @pl.when(i == last_step)
    def _():
        scatter(hs_ref, ds_ref, sem)
        drain(hs_ref, sem)
        for buf in range(min(2, n_prompt_steps)):
            drain(stage.at[buf], stage_sems.at[buf])


def _dispatch(seg, dest_p, dest_s, h2p, h2s, n_blocks, tm):
    n_p, n_s = h2p.shape[0], h2s.shape[0]
    steps_p = n_p // tm
    pmap3 = lambda i, sg: (jnp.minimum(i, steps_p - 1), 0, 0)
    pmap2 = lambda i, sg: (jnp.minimum(i, steps_p - 1), 0)
    grid_spec = pltpu.PrefetchScalarGridSpec(
        num_scalar_prefetch=1,
        grid=(steps_p + 1,),
        in_specs=[pl.BlockSpec((1, 1, 2 * tm), pmap3, memory_space=pltpu.SMEM),
                  pl.BlockSpec((1, 1, 2 * n_s), lambda i, sg: (0, 0, 0), memory_space=pltpu.SMEM),
                  pl.BlockSpec((tm, D_MODEL), pmap2),
                  pl.BlockSpec((n_s, D_MODEL), lambda i, sg: (0, 0))],
        out_specs=pl.BlockSpec(memory_space=pl.ANY),
        scratch_shapes=[pltpu.VMEM((MOE_BLOCK, D_MODEL), F32), pltpu.VMEM((2, tm, D_MODEL), F32),
                        pltpu.SemaphoreType.DMA(()), pltpu.SemaphoreType.DMA((2,))])
    return pl.pallas_call(
        functools.partial(_dispatch_kernel, n_blocks=n_blocks, n_prompt_steps=steps_p),
        grid_spec=grid_spec,
        out_shape=jax.ShapeDtypeStruct((n_blocks * MOE_BLOCK, D_MODEL), F32),
        compiler_params=_cparams(("arbitrary",)),
        name="dispatch",
    )(seg, dest_p.reshape(steps_p, 1, 2 * tm), dest_s.reshape(1, 1, 2 * n_s), h2p, h2s)


def _moe_kernel(blk_e_ref, nvalid_ref, x_ref, wg_ref, wu_ref, wd_ref, y_ref, wg16, wu16, wd16):
    i = pl.program_id(0)
    e = blk_e_ref[i]
    e_prev = blk_e_ref[jnp.maximum(i - 1, 0)]

    @pl.when((i == 0) | (e != e_prev))
    def _():
        wg16[...] = wg_ref[0].astype(BF16)
        wu16[...] = wu_ref[0].astype(BF16)
        wd16[...] = wd_ref[0].astype(BF16)

    @pl.when(nvalid_ref[i] > 0)
    def _():
        xb = x_ref[...].astype(BF16)
        g = _dot(xb, wg16[...])
        u = _dot(xb, wu16[...])
        hmid = (_silu(g) * u).astype(BF16)
        y_ref[...] = _dot(hmid, wd16[...])

    @pl.when(nvalid_ref[i] == 0)
    def _():
        y_ref[...] = jnp.zeros_like(y_ref)


def _moe(blk_e, nvalid, x_sorted, w_gate, w_up, w_down):
    n_blocks = blk_e.shape[0]
    wmap = lambda i, be, nv: (be[i], 0, 0)
    xmap = lambda i, be, nv: (jnp.where(nv[i] > 0, i, 0), 0)
    grid_spec = pltpu.PrefetchScalarGridSpec(
        num_scalar_prefetch=2,
        grid=(n_blocks,),
        in_specs=[pl.BlockSpec((MOE_BLOCK, D_MODEL), xmap),
                  pl.BlockSpec((1, D_MODEL, D_EXPERT), wmap),
                  pl.BlockSpec((1, D_MODEL, D_EXPERT), wmap),
                  pl.BlockSpec((1, D_EXPERT, D_MODEL), wmap)],
        out_specs=pl.BlockSpec((MOE_BLOCK, D_MODEL), lambda i, be, nv: (i, 0)),
        scratch_shapes=[pltpu.VMEM((D_MODEL, D_EXPERT), BF16),
                        pltpu.VMEM((D_MODEL, D_EXPERT), BF16),
                        pltpu.VMEM((D_EXPERT, D_MODEL), BF16)])
    return pl.pallas_call(
        _moe_kernel,
        grid_spec=grid_spec,
        out_shape=jax.ShapeDtypeStruct((n_blocks * MOE_BLOCK, D_MODEL), F32),
        compiler_params=_cparams(("arbitrary",)),
        name="moe",
    )(blk_e, nvalid, x_sorted, w_gate, w_up, w_down)


def _combine_kernel(dcur_ref, dnext_ref, route_ref, x1_ref, mod_ref, g_ref, y_hbm, o_ref, ybuf, sems):
    i = pl.program_id(0)
    last = pl.num_programs(0) - 1
    tm = x1_ref.shape[0]

    def gather(dest_ref, buf):
        for t in range(tm):
            for slot in range(2):
                pltpu.make_async_copy(y_hbm.at[pl.ds(dest_ref[0, 0, 2 * t + slot], 1), :],
                                      ybuf.at[buf, slot, pl.ds(t, 1), :], sems.at[buf]).start(priority=slot)

    @pl.when(i == 0)
    def _():
        gather(dcur_ref, 0)

    for buf in range(2):
        @pl.when(i % 2 == buf)
        def _():
            @pl.when(i < last)
            def _():
                gather(dnext_ref, 1 - buf)

            for slot in range(2):
                pltpu.make_async_copy(y_hbm.at[pl.ds(0, tm), :], ybuf.at[buf, slot], sems.at[buf]).wait()
            route = route_ref[...]
            moe = ybuf[buf, 0] * route[:, 2:3] + ybuf[buf, 1] * route[:, 3:4]
            o_ref[...] = x1_ref[...] + _per_seq(_rms(moe, g_ref[...]), mod_ref[...], lambda y, m: y * m[:, 5:6])


def _combine(dest, route, y_sorted, x1, mod8, g_post_ffn, tm, seq_rows, mod_row0):
    n = x1.shape[0]
    steps = n // tm
    row = lambda i: (i, 0)
    dest3 = dest.reshape(steps, 1, 2 * tm)
    return pl.pallas_call(
        _combine_kernel,
        grid=(steps,),
        in_specs=[pl.BlockSpec((1, 1, 2 * tm), lambda i: (i, 0, 0), memory_space=pltpu.SMEM),
                  pl.BlockSpec((1, 1, 2 * tm), lambda i: (jnp.minimum(i + 1, steps - 1), 0, 0),
                               memory_space=pltpu.SMEM),
                  pl.BlockSpec((tm, LANES), row),
                  pl.BlockSpec((tm, D_MODEL), row),
                  _mod_spec(tm, seq_rows, mod_row0),
                  pl.BlockSpec((1, D_MODEL), lambda i: (0, 0)),
                  pl.BlockSpec(memory_space=pl.ANY)],
        out_specs=pl.BlockSpec((tm, D_MODEL), row),
        out_shape=jax.ShapeDtypeStruct((n, D_MODEL), F32),
        scratch_shapes=[pltpu.VMEM((2, 2, tm, D_MODEL), F32), pltpu.SemaphoreType.DMA((2,))],
        compiler_params=_cparams(("arbitrary",)),
        name="combine",
    )(dest3, dest3, route, x1, mod8, g_post_ffn, y_sorted)


def _segment_plan(counts_f, n_blocks):
    counts = counts_f[0, :N_EXPERTS].astype(jnp.int32)
    padded = (counts + MOE_BLOCK - 1) // MOE_BLOCK * MOE_BLOCK
    pad_end = jnp.cumsum(padded)
    pad_start = pad_end - padded
    blk_start = jnp.arange(n_blocks, dtype=jnp.int32) * MOE_BLOCK
    blk_e = jnp.minimum(jnp.sum((pad_end[None, :] <= blk_start[:, None]).astype(jnp.int32), axis=1),
                        N_EXPERTS - 1)
    onehot = blk_e[:, None] == jnp.arange(N_EXPERTS, dtype=jnp.int32)[None, :]
    c_blk = jnp.sum(jnp.where(onehot, counts[None, :], 0), axis=1)
    s_blk = jnp.sum(jnp.where(onehot, pad_start[None, :], 0), axis=1)
    nvalid = jnp.clip(c_blk - (blk_start - s_blk), 0, MOE_BLOCK).astype(jnp.int32)
    seg = jnp.concatenate([counts, pad_end]).astype(jnp.int32)
    return blk_e.astype(jnp.int32), nvalid, seg, pad_start


def _token_rows(route, pad_start):
    eid = route[:, 0:2].astype(jnp.int32)
    rank = route[:, 4:6].astype(jnp.int32)
    onehot = eid[:, :, None] == jnp.arange(N_EXPERTS, dtype=jnp.int32)[None, None, :]
    return rank + jnp.sum(jnp.where(onehot, pad_start[None, None, :], 0), axis=2)


def _layer(x_p, x_s, c_p, c_s, k_past, v_past, s0_s, conv_s, p):
    bp, tp, d = x_p.shape
    bs, ts, _ = x_s.shape
    n_p, n_s = bp * tp, bs * ts
    n_tok = n_p + n_s

    n_seq = bp + bs
    c_all = jnp.zeros((16, d), F32).at[:n_seq].set(jnp.concatenate([c_s, c_p], axis=0))
    mod = _ada(c_all, p['w_ada'], p['b_ada'])
    mod8 = jnp.pad(mod.reshape(16, 6, d), ((0, 0), (0, 2), (0, 0)))

    w_in = p['w_in']
    o_z, o_b, o_q = CONV_CH, CONV_CH + 512, CONV_CH + 512 + 2 * DN_HEADS
    wb = jnp.pad(w_in[:, o_b:o_q], ((0, 0), (0, LANES - 2 * DN_HEADS)))
    wb_hi = wb.astype(BF16)
    wb_lo = (wb - wb_hi.astype(F32)).astype(BF16)
    w_main = jnp.concatenate([w_in[:, :o_b].astype(BF16), w_in[:, o_q:].astype(BF16), wb_hi, wb_lo], axis=1)
    g_pre_mix = p['g_pre_mix'].reshape(1, d)

    conv_w8 = jnp.pad(p['conv_w'], ((0, 8 - CONV_W), (0, 0)))
    pad_g = lambda a: jnp.pad(a.reshape(1, DN_HEADS), ((0, 0), (DN_HEADS, LANES - 2 * DN_HEADS)))
    alog_row, dtb_row = pad_g(p['a_log']), pad_g(p['dt_bias'])
    onorm_a = p['onorm_a'].reshape(1, DN_DIM)
    onorm_b2 = jnp.tile(p['onorm_b'].reshape(1, SB_DIM), (1, 2))

    w_out16 = p['w_out'].astype(BF16)
    wr = jnp.pad(jnp.concatenate([p['w_router_group'], p['w_router_expert']], axis=1),
                 ((0, 0), (0, LANES - N_GROUPS - N_EXPERTS)))
    wr_hi = wr.astype(BF16)
    wr_lo = (wr - wr_hi.astype(F32)).astype(BF16)
    b_r = jnp.pad(jnp.concatenate([p['b_router_group'], p['b_router_expert']]).reshape(1, -1),
                  ((0, 0), (0, LANES - N_GROUPS - N_EXPERTS)))
    g_post_mix = p['g_post_mix'].reshape(1, d)
    g_pre_ffn = p['g_pre_ffn'].reshape(1, d)
    g_post_ffn = p['g_post_ffn'].reshape(1, d)

    def mixer(x, tm, tm_post, seq_rows, mod_row0, hist8, s0, k_old, v_old, chunk, n_chunks, bq, n_sub, cnt0):
        b, t, _ = x.shape
        x2d = x.reshape(b * t, d)
        a_in, z, bg, q16, kb, vb, k16, v16 = _proj(x2d, mod8, g_pre_mix, w_main, wb_hi, tm, seq_rows, mod_row0)
        r3 = lambda a: a.reshape(b, t, a.shape[-1])
        oa16, s_new = _delta(r3(a_in), r3(z), r3(bg), hist8, s0, conv_w8, alog_row, dtb_row, onorm_a,
                             chunk, n_chunks)
        k16, v16 = r3(k16), r3(v16)
        if k_old is not None:
            k16 = jnp.concatenate([k_old.reshape(b, -1, SB_WIDTH).astype(BF16), k16], axis=1)
            v16 = jnp.concatenate([v_old.reshape(b, -1, SB_WIDTH).astype(BF16), v16], axis=1)
        n_pad = (-k16.shape[1]) % KEY_TILE
        k16 = jnp.pad(k16, ((0, 0), (n_pad, 0), (0, 0)))
        v16 = jnp.pad(v16, ((0, 0), (n_pad, 0), (0, 0)))
        ob16 = _sb_attn(r3(q16), k16, v16, onorm_b2, bq, n_sub, n_pad)
        x1, h2, route, cnt = _post(oa16.reshape(b * t, 512), ob16.reshape(b * t, 512), x2d, mod8,
                                   g_post_mix, g_pre_ffn, w_out16, wr_hi, wr_lo, b_r, cnt0, tm_post, seq_rows,
                                   mod_row0)
        new_conv = r3(a_in)[:, t - (CONV_W - 1):, :]
        return (x1, h2, route, cnt, kb.reshape(b, t, SB_HEADS, SB_DIM), vb.reshape(b, t, SB_HEADS, SB_DIM),
                s_new, new_conv)

    zero_hist = jnp.zeros((bp, 8, CONV_CH), F32)
    zero_s = jnp.zeros((bp, DN_HEADS, DN_DIM, DN_DIM), F32)
    hist_s = jnp.pad(conv_s, ((0, 0), (8 - (CONV_W - 1), 0), (0, 0)))
    tm_p = min(256, tp)
    tm_post = tm_p * POST_TILE_FACTOR if n_p % (tm_p * POST_TILE_FACTOR) == 0 else tm_p
    nc_p = max(1, min(4, tp // DELTA_BLOCK))
    x1p, h2p, rp, cnt_p, kp, vp, sp, cp = mixer(x_p, tm_p, tm_post, tp, bs, zero_hist, zero_s, None, None,
                                                 min(DELTA_BLOCK, tp), nc_p, min(KEY_TILE // 2, tp), 2,
                                                 jnp.zeros((1, LANES), F32))
    x1s, h2s, rs, cnt, ks, vs, ss, cs = mixer(x_s, n_s, n_s, ts, 0, hist_s, s0_s, k_past, v_past,
                                               min(DELTA_BLOCK, ts), max(1, ts // DELTA_BLOCK),
                                               min(KEY_TILE, ts), 1, cnt_p)

    n_blocks = -(-2 * n_tok // MOE_BLOCK) + N_EXPERTS
    blk_e, nvalid, seg, pad_start = _segment_plan(cnt, n_blocks)
    dest_p = _token_rows(rp, pad_start)
    dest_s = _token_rows(rs, pad_start)
    x_sorted = _dispatch(seg, dest_p, dest_s, h2p, h2s, n_blocks, tm_p)
    y_sorted = _moe(blk_e, nvalid, x_sorted, p['w_gate'], p['w_up'], p['w_down'])
    y_p = _combine(dest_p, rp, y_sorted, x1p, mod8, g_post_ffn, tm_p, tp, bs).reshape(bp, tp, d)
    y_s = _combine(dest_s, rs, y_sorted, x1s, mod8, g_post_ffn, n_s, ts, 0).reshape(bs, ts, d)
    return y_p, y_s, kp, vp, sp, cp, ks, vs, ss, cs


def kernel(x_prompt, x_sample, c_prompt, c_sample, cache_k, cache_v, state_delta, state_conv, w_ada, b_ada, g_pre_mix, g_post_mix, g_pre_ffn, g_post_ffn, w_in, conv_w, a_log, dt_bias, onorm_a, onorm_b, w_out, w_router_group, b_router_group, w_router_expert, b_router_expert, w_gate, w_up, w_down):
    depth = w_in.shape[0]
    y_p, y_s = x_prompt, x_sample
    outs = [[] for _ in range(8)]
    for l in range(depth):
        p = dict(w_ada=w_ada[l], b_ada=b_ada[l], g_pre_mix=g_pre_mix[l], g_post_mix=g_post_mix[l],
                 g_pre_ffn=g_pre_ffn[l], g_post_ffn=g_post_ffn[l], w_in=w_in[l], conv_w=conv_w[l],
                 a_log=a_log[l], dt_bias=dt_bias[l], onorm_a=onorm_a[l], onorm_b=onorm_b[l],
                 w_out=w_out[l], w_router_group=w_router_group[l], b_router_group=b_router_group[l],
                 w_router_expert=w_router_expert[l], b_router_expert=b_router_expert[l],
                 w_gate=w_gate[l], w_up=w_up[l], w_down=w_down[l])
        res = _layer(y_p, y_s, c_prompt, c_sample, cache_k[l], cache_v[l], state_delta[l], state_conv[l], p)
        y_p, y_s = res[0], res[1]
        for lst, r in zip(outs, res[2:]):
            lst.append(r)
    return (y_p, y_s) + tuple(jnp.stack(o) for o in outs)
```

```python
import functools
import math

import jax
import jax.numpy as jnp
from jax import lax
from jax.experimental import pallas as pl
from jax.experimental.pallas import tpu as pltpu

F32 = jnp.float32
BF16 = jnp.bfloat16

D_MODEL = 1024
DN_HEADS = 4
DN_DIM = 128
CONV_W = 4
CONV_CH = DN_HEADS * 3 * DN_DIM
DELTA_BLOCK = 64
SB_HEADS = 8
SB_DIM = 64
SB_WIDTH = SB_HEADS * SB_DIM
N_GROUPS = 4
EXPERTS_PER_GROUP = 8
N_EXPERTS = N_GROUPS * EXPERTS_PER_GROUP
D_EXPERT = D_MODEL // 2
MOE_BLOCK = 256
EPS = 1e-6

LANES = 128
KEY_TILE = 128
ATTN_WINDOW = 3 * KEY_TILE
WINDOW_BLOCKS = 4
EXP_ZERO_BELOW = -104.0
VMEM_LIMIT = 56 * 1024 * 1024
POST_TILE_FACTOR = 2


def _cparams(sem):
    return pltpu.CompilerParams(dimension_semantics=sem, vmem_limit_bytes=VMEM_LIMIT)


def _split(a):
    hi = a.astype(BF16)
    lo = (a - hi.astype(F32)).astype(BF16)
    return hi, lo


def _dot(a, b, dims=(((1,), (0,)), ((), ()))):
    return lax.dot_general(a, b, dims, preferred_element_type=F32)


def _dot_small_int_lhs(a, b):
    a16 = a.astype(BF16)
    return _dot(jnp.concatenate([a16, a16], axis=1), jnp.concatenate(_split(b), axis=0))


_NT = (((1,), (1,)), ((), ()))
_TN = (((0,), (0,)), ((), ()))


def _silu(x):
    return x * jax.nn.sigmoid(x)


SOFTPLUS_LINEAR_ABOVE = 80.0


def _softplus(x):
    return jnp.where(x > SOFTPLUS_LINEAR_ABOVE, x, jnp.log(1.0 + jnp.exp(jnp.minimum(x, SOFTPLUS_LINEAR_ABOVE))))


def _ada_kernel(c_ref, w_ref, b_ref, o_ref):
    s = _silu(c_ref[...]).astype(BF16)
    o_ref[...] = _dot(s, w_ref[...].astype(BF16)) + b_ref[...]


def _ada(c_all, w_ada, b_ada):
    rows = c_all.shape[0]
    n = w_ada.shape[1]
    tn = 1024
    return pl.pallas_call(
        _ada_kernel,
        grid=(n // tn,),
        in_specs=[pl.BlockSpec((rows, D_MODEL), lambda j: (0, 0)),
                  pl.BlockSpec((D_MODEL, tn), lambda j: (0, j)),
                  pl.BlockSpec((1, tn), lambda j: (0, j))],
        out_specs=pl.BlockSpec((rows, tn), lambda j: (0, j)),
        out_shape=jax.ShapeDtypeStruct((rows, n), F32),
        compiler_params=_cparams(("arbitrary",)),
        name="ada",
    )(c_all, w_ada, b_ada.reshape(1, n))


def _rms(x, gain):
    return x * lax.rsqrt(jnp.mean(x * x, axis=-1, keepdims=True) + EPS) * gain


def _per_seq(y, mod, fn):
    n_seq = mod.shape[0]
    ys = y.reshape(n_seq, y.shape[0] // n_seq, y.shape[1])
    return fn(ys, mod).reshape(y.shape)


def _mod_spec(tm, seq_rows, mod_row0):
    if tm >= seq_rows:
        n_seq = tm // seq_rows
        assert tm % seq_rows == 0 and mod_row0 % n_seq == 0
        return pl.BlockSpec((n_seq, 8, D_MODEL), lambda i: (mod_row0 // n_seq + i, 0, 0))
    assert seq_rows % tm == 0
    return pl.BlockSpec((1, 8, D_MODEL), lambda i: (mod_row0 + (i * tm) // seq_rows, 0, 0))


def _proj_kernel(x_ref, mod_ref, g_ref, wm_ref, wbh_ref,
                 a_ref, z_ref, bg_ref, q_ref, k_ref, v_ref, k16_ref, v16_ref):
    h = _per_seq(_rms(x_ref[...], g_ref[...]), mod_ref[...], lambda y, m: y * (1.0 + m[:, 1:2]) + m[:, 0:1])
    hh, hl = _split(h)
    p = _dot(hh, wm_ref[...])
    a_ref[...] = p[:, 0:CONV_CH]
    z_ref[...] = p[:, CONV_CH:CONV_CH + 512]
    o = CONV_CH + 512
    q_ref[...] = (p[:, o:o + 512] * (SB_DIM ** -0.5)).astype(BF16)
    k = p[:, o + 512:o + 1024]
    v = p[:, o + 1024:o + 1536]
    k_ref[...] = k
    v_ref[...] = v
    k16_ref[...] = k.astype(BF16)
    v16_ref[...] = v.astype(BF16)
    bg_ref[...] = p[:, o + 1536:o + 1536 + LANES] + (_dot(hl, wbh_ref[...]) + p[:, o + 1536 + LANES:])


def _proj(x2d, mod8, g_pre, w_main, wb_hi, tm, seq_rows, mod_row0):
    n = x2d.shape[0]
    nm = w_main.shape[1]
    row = lambda i: (i, 0)
    const = lambda i: (0, 0)
    outs = [(CONV_CH, F32), (512, F32), (LANES, F32), (512, BF16), (512, F32), (512, F32),
            (512, BF16), (512, BF16)]
    return pl.pallas_call(
        _proj_kernel,
        grid=(n // tm,),
        in_specs=[pl.BlockSpec((tm, D_MODEL), row),
                  _mod_spec(tm, seq_rows, mod_row0),
                  pl.BlockSpec((1, D_MODEL), const),
                  pl.BlockSpec((D_MODEL, nm), const),
                  pl.BlockSpec((D_MODEL, LANES), const)],
        out_specs=[pl.BlockSpec((tm, w), row) for w, _ in outs],
        out_shape=[jax.ShapeDtypeStruct((n, w), dt) for w, dt in outs],
        compiler_params=_cparams(("arbitrary",)),
        name="proj",
    )(x2d, mod8, g_pre, w_main, wb_hi)


def _delta_kernel(a_ref, z_ref, bg_ref, hist0_ref, s0_ref, cw_ref, alog_ref, dtb_ref, on_ref,
                  o_ref, sfin_ref, hist_sc, s_sc, *, chunk, n_chunks):
    t_idx = pl.program_id(1)
    tt = chunk * n_chunks

    @pl.when(t_idx == 0)
    def _():
        hist_sc[...] = hist0_ref[0]
        s_sc[...] = s0_ref[0]

    x = a_ref[0]
    xx = jnp.concatenate([hist_sc[...], x], axis=0)
    cw = cw_ref[...]
    conv = x * cw[CONV_W - 1:CONV_W]
    for s in range(1, CONV_W):
        conv = conv + pltpu.roll(xx, s, 0)[8:] * cw[CONV_W - 1 - s:CONV_W - s]
    conv = _silu(conv)
    hist_sc[...] = x[tt - 8:tt]

    bg = bg_ref[0]
    lane = lax.broadcasted_iota(jnp.int32, (1, LANES), 1)
    g_lane = (lane >= DN_HEADS) & (lane < 2 * DN_HEADS)
    neg_a = jnp.where(g_lane, -jnp.exp(alog_ref[...]), 0.0)
    beta_all = jax.nn.sigmoid(bg)
    g_all = neg_a * _softplus(bg + dtb_ref[...])

    ri = lax.broadcasted_iota(jnp.int32, (chunk, chunk), 0)
    ci = lax.broadcasted_iota(jnp.int32, (chunk, chunk), 1)
    incl = ri >= ci
    strict = ri > ci
    tri = incl.astype(F32)
    eye = (ri == ci).astype(F32)
    ones_cc = jnp.ones((chunk, chunk), F32)
    onorm = on_ref[...]

    chunks = range(n_chunks)
    units = [(c, h) for c in chunks for h in range(DN_HEADS)]
    rows = lambda c: slice(c * chunk, (c + 1) * chunk)

    gc_all = [_dot_small_int_lhs(tri, g_all[rows(c)]) for c in chunks]
    diag = [jnp.concatenate([eye * gc_all[c][:, DN_HEADS + h:DN_HEADS + h + 1] for h in range(DN_HEADS)], axis=1)
            for c in chunks]
    gc_rows = [_dot_small_int_lhs(ones_cc, diag[c]) for c in chunks]

    q, k, v, beta, gcol, glast, eg, decay, kbeta = {}, {}, {}, {}, {}, {}, {}, {}, {}
    for (c, h) in units:
        lo = h * DN_DIM
        qq = conv[rows(c), lo:lo + DN_DIM]
        kk = conv[rows(c), 512 + lo:512 + lo + DN_DIM]
        u_ = (c, h)
        q[u_] = qq * lax.rsqrt(jnp.sum(qq * qq, axis=-1, keepdims=True) + EPS) * (DN_DIM ** -0.5)
        k[u_] = kk * lax.rsqrt(jnp.sum(kk * kk, axis=-1, keepdims=True) + EPS)
        v[u_] = conv[rows(c), 1024 + lo:1024 + lo + DN_DIM]
        beta[u_] = beta_all[rows(c), h:h + 1]
        gcol[u_] = gc_all[c][:, DN_HEADS + h:DN_HEADS + h + 1]
        glast[u_] = gcol[u_][chunk - 1:chunk, :]
        eg[u_] = jnp.exp(gcol[u_])
        grow = gc_rows[c][:, h * chunk:(h + 1) * chunk]
        decay[u_] = jnp.where(incl, jnp.exp(jnp.minimum(gcol[u_] - grow, 0.0)), 0.0)
        kbeta[u_] = k[u_] * beta[u_]

    k16 = {u_: k[u_].astype(BF16) for u_ in units}
    kq = {u_: _dot(jnp.concatenate([kbeta[u_], q[u_]], axis=0).astype(BF16), k16[u_], _NT) for u_ in units}
    lower = {u_: jnp.where(strict, kq[u_][:chunk] * decay[u_], 0.0) for u_ in units}
    attn16 = {u_: jnp.where(incl, kq[u_][chunk:] * decay[u_], 0.0).astype(BF16) for u_ in units}

    sol = {u_: jnp.concatenate([v[u_] * beta[u_], kbeta[u_] * eg[u_]], axis=1) for u_ in units}
    lp = lower
    p = 1
    while p < chunk:
        lsp = {u_: _split(lp[u_]) for u_ in units}
        ssp = {u_: _split(sol[u_]) for u_ in units}
        lcat = {u_: jnp.concatenate([lsp[u_][0], lsp[u_][1], lsp[u_][0]], axis=1) for u_ in units}
        upd = {u_: _dot(lcat[u_], jnp.concatenate([ssp[u_][0], ssp[u_][0], ssp[u_][1]], axis=0)) for u_ in units}
        sol = {u_: (sol[u_] - upd[u_]) if p == 1 else (sol[u_] + upd[u_]) for u_ in units}
        if 2 * p < chunk:
            lp = {u_: _dot(lcat[u_], jnp.concatenate([lsp[u_][0], lsp[u_][0], lsp[u_][1]], axis=0))
                  for u_ in units}
        p *= 2

    usol = {u_: sol[u_][:, :DN_DIM] for u_ in units}
    wq16 = {u_: jnp.concatenate([sol[u_][:, DN_DIM:], q[u_] * eg[u_]], axis=0).astype(BF16) for u_ in units}
    kd16 = {u_: (k[u_] * jnp.exp(glast[u_] - gcol[u_])).astype(BF16) for u_ in units}

    for c in chunks:
        hs = range(DN_HEADS)
        s_old = [s_sc[h] for h in hs]
        ws = [_dot(wq16[(c, h)], s_old[h].astype(BF16)) for h in hs]
        v_new = [(usol[(c, h)] - ws[h][:chunk]).astype(BF16) for h in hs]
        o_in = [_dot(attn16[(c, h)], v_new[h]) for h in hs]
        ds = [_dot(kd16[(c, h)], v_new[h], _TN) for h in hs]
        for h in hs:
            lo = h * DN_DIM
            s_sc[h] = s_old[h] * jnp.exp(glast[(c, h)]) + ds[h]
            o = ws[h][chunk:] + o_in[h]
            zg = z_ref[0, rows(c), lo:lo + DN_DIM]
            o_ref[0, rows(c), lo:lo + DN_DIM] = (_rms(o, onorm) * _silu(zg)).astype(BF16)

    @pl.when(t_idx == pl.num_programs(1) - 1)
    def _():
        sfin_ref[0] = s_sc[...]


def _delta(a_in, z, bg, hist8, s0, conv_w8, alog_row, dtb_row, onorm_a, chunk, n_chunks):
    b, t, _ = a_in.shape
    tt = chunk * n_chunks
    tile = lambda bi, ti: (bi, ti, 0)
    per_b3 = lambda bi, ti: (bi, 0, 0)
    per_b4 = lambda bi, ti: (bi, 0, 0, 0)
    const = lambda bi, ti: (0, 0)
    kern = functools.partial(_delta_kernel, chunk=chunk, n_chunks=n_chunks)
    return pl.pallas_call(
        kern,
        grid=(b, t // tt),
        in_specs=[pl.BlockSpec((1, tt, CONV_CH), tile),
                  pl.BlockSpec((1, tt, 512), tile),
                  pl.BlockSpec((1, tt, LANES), tile),
                  pl.BlockSpec((1, 8, CONV_CH), per_b3),
                  pl.BlockSpec((1, DN_HEADS, DN_DIM, DN_DIM), per_b4),
                  pl.BlockSpec((8, CONV_CH), const),
                  pl.BlockSpec((1, LANES), const),
                  pl.BlockSpec((1, LANES), const),
                  pl.BlockSpec((1, DN_DIM), const)],
        out_specs=[pl.BlockSpec((1, tt, 512), tile),
                   pl.BlockSpec((1, DN_HEADS, DN_DIM, DN_DIM), per_b4)],
        out_shape=[jax.ShapeDtypeStruct((b, t, 512), BF16),
                   jax.ShapeDtypeStruct((b, DN_HEADS, DN_DIM, DN_DIM), F32)],
        scratch_shapes=[pltpu.VMEM((8, CONV_CH), F32),
                        pltpu.VMEM((DN_HEADS, DN_DIM, DN_DIM), F32)],
        compiler_params=_cparams(("arbitrary", "arbitrary")),
        name="delta",
    )(a_in, z, bg, hist8, s0, conv_w8, alog_row, dtb_row, onorm_a)


def _sb_kernel(q_ref, *refs, bq, n_sub, n_pad, q_off):
    kwin = refs[:WINDOW_BLOCKS]
    vwin = refs[WINDOW_BLOCKS:2 * WINDOW_BLOCKS]
    k_hbm, v_hbm, on_ref, o_ref, kbuf, vbuf, qsel, acc, carry, sem = refs[2 * WINDOW_BLOCKS:]
    b = pl.program_id(0)
    i = pl.program_id(1)
    g = bq * n_sub
    qend_step = q_off + (i + 1) * g
    win_start = qend_step - WINDOW_BLOCKS * KEY_TILE
    n_pairs = SB_HEADS // 2
    heads = range(SB_HEADS)

    half_lane = lax.broadcasted_iota(jnp.int32, (bq, LANES), 1) < SB_DIM
    rj = lax.broadcasted_iota(jnp.int32, (2 * KEY_TILE, 2 * KEY_TILE), 0) % KEY_TILE
    cj = lax.broadcasted_iota(jnp.int32, (2 * KEY_TILE, 2 * KEY_TILE), 1)
    suffix2 = ((rj > cj) | (cj >= KEY_TILE)).astype(BF16)

    def window(blocks, off, p):
        parts = []
        for blk in range(WINDOW_BLOCKS):
            lo, hi = max(off, blk * KEY_TILE), min(off + ATTN_WINDOW, (blk + 1) * KEY_TILE)
            if lo < hi:
                ref = blocks[WINDOW_BLOCKS - 1 - blk]
                parts.append(ref[0, lo - blk * KEY_TILE:hi - blk * KEY_TILE, p * LANES:(p + 1) * LANES])
        return jnp.concatenate(parts, axis=0)

    n_t = ATTN_WINDOW // KEY_TILE
    row = lax.broadcasted_iota(jnp.int32, (bq, ATTN_WINDOW), 0)
    col = lax.broadcasted_iota(jnp.int32, (bq, ATTN_WINDOW), 1)
    causal = col < row + (ATTN_WINDOW - bq)
    subs = range(n_sub)
    offs = [WINDOW_BLOCKS * KEY_TILE - g + (s + 1) * bq - ATTN_WINDOW for s in subs]
    mask = [causal & (col >= n_pad - (win_start + offs[s])) for s in subs]
    zs, sp, cs_all = {}, {}, {}
    for s in subs:
        for p in range(n_pairs):
            qf = q_ref[0, s * bq:(s + 1) * bq, p * LANES:(p + 1) * LANES].astype(F32)
            qq = jnp.concatenate([jnp.where(half_lane, qf, 0.0), jnp.where(half_lane, 0.0, qf)], axis=0)
            zz = _dot(qq.astype(BF16), window(kwin, offs[s], p), _NT)
            zs[(s, 2 * p)], zs[(s, 2 * p + 1)] = zz[:bq], zz[bq:]
    for s in subs:
        pieces = []
        for h in heads:
            sp[(s, h)] = _softplus(zs[(s, h)])
            hi, lo = _split(jnp.where(mask[s], -sp[(s, h)], 0.0))
            for j in range(n_t):
                c0 = ATTN_WINDOW - (j + 1) * KEY_TILE
                pieces.append(jnp.concatenate([hi[:, c0:c0 + KEY_TILE], lo[:, c0:c0 + KEY_TILE]], axis=1))
        cs_all[s] = _dot(jnp.concatenate(pieces, axis=0), suffix2)
    done = []
    for s in subs:
        worst = jnp.full((bq, KEY_TILE), -jnp.inf, F32)
        a = {}
        for h in heads:
            run = None
            cols = []
            for j in range(n_t):
                r0 = (h * n_t + j) * bq
                cs = cs_all[s][r0:r0 + bq]
                cols.append(cs[:, :KEY_TILE] if run is None else cs[:, :KEY_TILE] + run)
                run = cs[:, KEY_TILE:] if run is None else run + cs[:, KEY_TILE:]
            within = jnp.concatenate(cols[::-1], axis=1)
            a[h] = jnp.where(mask[s], jnp.exp((zs[(s, h)] - sp[(s, h)]) + within), 0.0).astype(BF16)
            carry[s, h] = run
            worst = jnp.maximum(worst, run)
        for p in range(n_pairs):
            pv = _dot(jnp.concatenate([a[2 * p], a[2 * p + 1]], axis=0), window(vwin, offs[s], p))
            acc[s, p] = jnp.where(half_lane, pv[:bq], pv[bq:])
        done.append((jnp.max(worst) < EXP_ZERO_BELOW).astype(jnp.int32))

    rj1 = lax.broadcasted_iota(jnp.int32, (KEY_TILE, 2 * KEY_TILE), 0)
    cj1 = lax.broadcasted_iota(jnp.int32, (KEY_TILE, 2 * KEY_TILE), 1)
    suffix1 = ((rj1 > cj1) | (cj1 >= KEY_TILE)).astype(BF16)
    col1 = lax.broadcasted_iota(jnp.int32, (bq, KEY_TILE), 1)
    for s in subs:
        swept_from = win_start + offs[s]

        @pl.when((done[s] == 0) & (swept_from > n_pad))
        def _():
            for p in range(n_pairs):
                qf = q_ref[0, s * bq:(s + 1) * bq, p * LANES:(p + 1) * LANES].astype(F32)
                qsel[2 * p] = jnp.where(half_lane, qf, 0.0).astype(BF16)
                qsel[2 * p + 1] = jnp.where(half_lane, 0.0, qf).astype(BF16)

            def body(state):
                upper, _ = state
                start = jnp.maximum(upper - KEY_TILE, 0)
                copies = []
                for p in range(n_pairs):
                    for src, dst in ((k_hbm, kbuf), (v_hbm, vbuf)):
                        cp = pltpu.make_async_copy(
                            src.at[b, pl.ds(pl.multiple_of(start, 16), KEY_TILE), pl.ds(p * LANES, LANES)],
                            dst.at[p], sem)
                        cp.start()
                        copies.append(cp)
                for cp in copies:
                    cp.wait()
                kpos = start + col1
                m1 = (kpos < upper) & (kpos >= n_pad)
                zz = [_dot(qsel[h], kbuf[h // 2], _NT) for h in heads]
                spp = [_softplus(zz[h]) for h in heads]
                pcs = []
                for h in heads:
                    pcs.extend(_split(jnp.where(m1, -spp[h], 0.0)))
                cs1 = _dot(jnp.concatenate(pcs, axis=0), suffix1)
                worst = jnp.full((bq, KEY_TILE), -jnp.inf, F32)
                aa = []
                for h in heads:
                    c = cs1[2 * h * bq:(2 * h + 1) * bq] + cs1[(2 * h + 1) * bq:(2 * h + 2) * bq]
                    c_old = carry[s, h]
                    aa.append(jnp.where(m1, jnp.exp((zz[h] - spp[h]) + c[:, :KEY_TILE] + c_old), 0.0).astype(BF16))
                    c_new = c_old + c[:, KEY_TILE:]
                    carry[s, h] = c_new
                    worst = jnp.maximum(worst, c_new)
                for p in range(n_pairs):
                    pv0 = _dot(aa[2 * p], vbuf[p])
                    pv1 = _dot(aa[2 * p + 1], vbuf[p])
                    acc[s, p] = acc[s, p] + jnp.where(half_lane, pv0, pv1)
                return start, (jnp.max(worst) < EXP_ZERO_BELOW).astype(jnp.int32)

            lax.while_loop(lambda st: (st[0] > n_pad) & (st[1] == 0), body, (swept_from, jnp.int32(0)))

    onb = on_ref[...]
    for s in subs:
        for p in range(n_pairs):
            o = acc[s, p]
            sq = o * o
            s_lo = jnp.sum(jnp.where(half_lane, sq, 0.0), axis=-1, keepdims=True)
            s_hi = jnp.sum(jnp.where(half_lane, 0.0, sq), axis=-1, keepdims=True)
            ms = jnp.where(half_lane, s_lo, s_hi) * (1.0 / SB_DIM)
            o_ref[0, s * bq:(s + 1) * bq, p * LANES:(p + 1) * LANES] = (o * lax.rsqrt(ms + EPS) * onb).astype(BF16)


def _sb_attn(q16, k16p, v16p, onorm_b2, bq, n_sub, n_pad):
    b, tq, _ = q16.shape
    tkp = k16p.shape[1]
    q_off = tkp - tq
    g = bq * n_sub
    assert tq % g == 0 and g <= KEY_TILE and bq % 16 == 0
    assert all((q_off + (i + 1) * g) % KEY_TILE == 0 for i in range(tq // g))

    def kmap(back):
        def f(bi, i):
            last = (q_off + (i + 1) * g) // KEY_TILE - 1
            return (bi, jnp.maximum(last - back, 0), 0)
        return f

    qmap = lambda bi, i: (bi, i, 0)
    kern = functools.partial(_sb_kernel, bq=bq, n_sub=n_sub, n_pad=n_pad, q_off=q_off)
    kspec = [pl.BlockSpec((1, KEY_TILE, SB_WIDTH), kmap(back)) for back in range(WINDOW_BLOCKS)]
    return pl.pallas_call(
        kern,
        grid=(b, tq // g),
        in_specs=[pl.BlockSpec((1, g, SB_WIDTH), qmap)] + kspec + kspec
                 + [pl.BlockSpec(memory_space=pl.ANY), pl.BlockSpec(memory_space=pl.ANY),
                    pl.BlockSpec((1, LANES), lambda bi, i: (0, 0))],
        out_specs=pl.BlockSpec((1, g, SB_WIDTH), qmap),
        out_shape=jax.ShapeDtypeStruct((b, tq, SB_WIDTH), BF16),
        scratch_shapes=[pltpu.VMEM((SB_HEADS // 2, KEY_TILE, LANES), BF16),
                        pltpu.VMEM((SB_HEADS // 2, KEY_TILE, LANES), BF16),
                        pltpu.VMEM((SB_HEADS, bq, LANES), BF16),
                        pltpu.VMEM((n_sub, SB_HEADS // 2, bq, LANES), F32),
                        pltpu.VMEM((n_sub, SB_HEADS, bq, KEY_TILE), F32),
                        pltpu.SemaphoreType.DMA(())],
        compiler_params=_cparams(("arbitrary", "arbitrary")),
        name="sb_attn",
    )(q16, *([k16p] * WINDOW_BLOCKS), *([v16p] * WINDOW_BLOCKS), k16p, v16p, onorm_b2)


def _post_kernel(oa_ref, ob_ref, x_ref, mod_ref, gpm_ref, gpf_ref, wo_ref, wrh_ref, wrl_ref, br_ref, cnt0_ref,
                 x1_ref, h2_ref, route_ref, cnt_ref):
    @pl.when(pl.program_id(0) == 0)
    def _():
        cnt_ref[...] = cnt0_ref[...]

    mod = mod_ref[...]
    mix = _dot(oa_ref[...], wo_ref[0:512, :]) + _dot(ob_ref[...], wo_ref[512:1024, :])
    x1 = x_ref[...] + _per_seq(_rms(mix, gpm_ref[...]), mod, lambda y, m: y * m[:, 2:3])
    x1_ref[...] = x1
    h2 = _per_seq(_rms(x1, gpf_ref[...]), mod, lambda y, m: y * (1.0 + m[:, 4:5]) + m[:, 3:4])
    h2_ref[...] = h2
    hh, hl = _split(h2)
    wrh = wrh_ref[...]
    logits = _dot(hh, wrh) + (_dot(hl, wrh) + _dot(hh, wrl_ref[...])) + br_ref[...]
    lane = lax.broadcasted_iota(jnp.int32, logits.shape, 1).astype(F32)
    neg = -jnp.inf
    nl = float(LANES)
    lg = jnp.where(lane < N_GROUPS, logits, neg)
    gmax = jnp.max(lg, axis=-1, keepdims=True)
    grp = jnp.min(jnp.where(lg == gmax, lane, nl), axis=-1, keepdims=True)
    p_grp = 1.0 / jnp.sum(jnp.exp(lg - gmax), axis=-1, keepdims=True)
    first = N_GROUPS + grp * EXPERTS_PER_GROUP
    le = jnp.where((lane >= first) & (lane < first + EXPERTS_PER_GROUP), logits, neg)
    emax = jnp.max(le, axis=-1, keepdims=True)
    i1 = jnp.min(jnp.where(le == emax, lane, nl), axis=-1, keepdims=True)
    esum = jnp.sum(jnp.exp(le - emax), axis=-1, keepdims=True)
    le2 = jnp.where(lane == i1, neg, le)
    e2max = jnp.max(le2, axis=-1, keepdims=True)
    i2 = jnp.min(jnp.where(le2 == e2max, lane, nl), axis=-1, keepdims=True)
    p1 = 1.0 / esum
    p2 = jnp.exp(e2max - emax) / esum
    w1 = p_grp * p1 / (p1 + p2)
    w2 = p_grp * p2 / (p1 + p2)
    e1 = i1 - N_GROUPS
    e2 = i2 - N_GROUPS
    hot1 = (lane == e1).astype(F32)
    hot2 = (lane == e2).astype(F32)
    both = hot1 + hot2
    tm = logits.shape[0]
    ti = lax.broadcasted_iota(jnp.int32, (tm, tm), 0)
    tj = lax.broadcasted_iota(jnp.int32, (tm, tm), 1)
    earlier = _dot((ti > tj).astype(BF16), both.astype(BF16)) + cnt_ref[...]
    rank1 = jnp.sum(hot1 * earlier, axis=-1, keepdims=True)
    rank2 = jnp.sum(hot2 * (earlier + hot1), axis=-1, keepdims=True)
    cnt_ref[...] = cnt_ref[...] + jnp.sum(both, axis=0, keepdims=True)
    out = jnp.where(lane == 0.0, e1, 0.0)
    out = jnp.where(lane == 1.0, e2, out)
    out = jnp.where(lane == 2.0, w1, out)
    out = jnp.where(lane == 3.0, w2, out)
    out = jnp.where(lane == 4.0, rank1, out)
    out = jnp.where(lane == 5.0, rank2, out)
    route_ref[...] = out


def _post(oa16, ob16, x2d, mod8, g_post_mix, g_pre_ffn, w_out16, wr_hi, wr_lo, b_r, cnt0, tm, seq_rows,
          mod_row0):
    n = x2d.shape[0]
    row = lambda i: (i, 0)
    const = lambda i: (0, 0)
    return pl.pallas_call(
        _post_kernel,
        grid=(n // tm,),
        in_specs=[pl.BlockSpec((tm, 512), row),
                  pl.BlockSpec((tm, 512), row),
                  pl.BlockSpec((tm, D_MODEL), row),
                  _mod_spec(tm, seq_rows, mod_row0),
                  pl.BlockSpec((1, D_MODEL), const),
                  pl.BlockSpec((1, D_MODEL), const),
                  pl.BlockSpec((D_MODEL, D_MODEL), const),
                  pl.BlockSpec((D_MODEL, LANES), const),
                  pl.BlockSpec((D_MODEL, LANES), const),
                  pl.BlockSpec((1, LANES), const),
                  pl.BlockSpec((1, LANES), const)],
        out_specs=[pl.BlockSpec((tm, D_MODEL), row),
                   pl.BlockSpec((tm, D_MODEL), row),
                   pl.BlockSpec((tm, LANES), row),
                   pl.BlockSpec((1, LANES), const)],
        out_shape=[jax.ShapeDtypeStruct((n, D_MODEL), F32),
                   jax.ShapeDtypeStruct((n, D_MODEL), F32),
                   jax.ShapeDtypeStruct((n, LANES), F32),
                   jax.ShapeDtypeStruct((1, LANES), F32)],
        compiler_params=_cparams(("arbitrary",)),
        name="post",
    )(oa16, ob16, x2d, mod8, g_post_mix, g_pre_ffn, w_out16, wr_hi, wr_lo, b_r, cnt0)


def _dispatch_kernel(seg_ref, dp_ref, ds_ref, hp_ref, hs_ref, xs_hbm, zbuf, stage, sem, stage_sems, *,
                     n_blocks, n_prompt_steps):
    i = pl.program_id(0)
    last_step = pl.num_programs(0) - 1

    @pl.when(i == 0)
    def _():
        zbuf[...] = jnp.zeros_like(zbuf)

        def zero_block(row0):
            return pltpu.make_async_copy(zbuf, xs_hbm.at[pl.ds(pl.multiple_of(row0, MOE_BLOCK), MOE_BLOCK), :], sem)

        for e in range(N_EXPERTS):
            @pl.when(seg_ref[e] > 0)
            def _():
                zero_block(seg_ref[N_EXPERTS + e] - MOE_BLOCK).start()
        for e in range(N_EXPERTS):
            @pl.when(seg_ref[e] > 0)
            def _():
                zero_block(seg_ref[N_EXPERTS + e] - MOE_BLOCK).wait()

        used = seg_ref[2 * N_EXPERTS - 1] // MOE_BLOCK

        def fill(b, c):
            cp = zero_block(b * MOE_BLOCK)
            cp.start()
            cp.wait()
            return c

        lax.fori_loop(used, n_blocks, fill, 0)

    def scatter(h_ref, dest_ref, sem_):
        for t in range(h_ref.shape[0]):
            for slot in range(2):
                pltpu.make_async_copy(h_ref.at[pl.ds(t, 1), :],
                                      xs_hbm.at[pl.ds(dest_ref[0, 0, 2 * t + slot], 1), :],
                                      sem_).start(priority=slot)

    def drain(h_ref, sem_):
        for slot in range(2):
            pltpu.make_async_copy(h_ref, xs_hbm.at[pl.ds(0, h_ref.shape[0]), :], sem_).wait()

    for buf in range(2):
        @pl.when((i < last_step) & (i % 2 == buf))
        def _():
            @pl.when(i >= 2)
            def _():
                drain(stage.at[buf], stage_sems.at[buf])
            stage[buf] = hp_ref[...]
            scatter(stage.at[buf], dp_ref, stage_sems.at[buf])

    @pl.when(i == last_step)
    def _():
        scatter(hs_ref, ds_ref, sem)
        drain(hs_ref, sem)
        for buf in range(min(2, n_prompt_steps)):
            drain(stage.at[buf], stage_sems.at[buf])


def _dispatch(seg, dest_p, dest_s, h2p, h2s, n_blocks, tm):
    n_p, n_s = h2p.shape[0], h2s.shape[0]
    steps_p = n_p // tm
    pmap3 = lambda i, sg: (jnp.minimum(i, steps_p - 1), 0, 0)
    pmap2 = lambda i, sg: (jnp.minimum(i, steps_p - 1), 0)
    grid_spec = pltpu.PrefetchScalarGridSpec(
        num_scalar_prefetch=1,
        grid=(steps_p + 1,),
        in_specs=[pl.BlockSpec((1, 1, 2 * tm), pmap3, memory_space=pltpu.SMEM),
                  pl.BlockSpec((1, 1, 2 * n_s), lambda i, sg: (0, 0, 0), memory_space=pltpu.SMEM),
                  pl.BlockSpec((tm, D_MODEL), pmap2),
                  pl.BlockSpec((n_s, D_MODEL), lambda i, sg: (0, 0))],
        out_specs=pl.BlockSpec(memory_space=pl.ANY),
        scratch_shapes=[pltpu.VMEM((MOE_BLOCK, D_MODEL), F32), pltpu.VMEM((2, tm, D_MODEL), F32),
                        pltpu.SemaphoreType.DMA(()), pltpu.SemaphoreType.DMA((2,))])
    return pl.pallas_call(
        functools.partial(_dispatch_kernel, n_blocks=n_blocks, n_prompt_steps=steps_p),
        grid_spec=grid_spec,
        out_shape=jax.ShapeDtypeStruct((n_blocks * MOE_BLOCK, D_MODEL), F32),
        compiler_params=_cparams(("arbitrary",)),
        name="dispatch",
    )(seg, dest_p.reshape(steps_p, 1, 2 * tm), dest_s.reshape(1, 1, 2 * n_s), h2p, h2s)


def _moe_kernel(blk_e_ref, nvalid_ref, x_ref, wg_ref, wu_ref, wd_ref, y_ref, wg16, wu16, wd16):
    i = pl.program_id(0)
    e = blk_e_ref[i]
    e_prev = blk_e_ref[jnp.maximum(i - 1, 0)]

    @pl.when((i == 0) | (e != e_prev))
    def _():
        wg16[...] = wg_ref[0].astype(BF16)
        wu16[...] = wu_ref[0].astype(BF16)
        wd16[...] = wd_ref[0].astype(BF16)

    @pl.when(nvalid_ref[i] > 0)
    def _():
        xb = x_ref[...].astype(BF16)
        g = _dot(xb, wg16[...])
        u = _dot(xb, wu16[...])
        hmid = (_silu(g) * u).astype(BF16)
        y_ref[...] = _dot(hmid, wd16[...])

    @pl.when(nvalid_ref[i] == 0)
    def _():
        y_ref[...] = jnp.zeros_like(y_ref)


def _moe(blk_e, nvalid, x_sorted, w_gate, w_up, w_down):
    n_blocks = blk_e.shape[0]
    wmap = lambda i, be, nv: (be[i], 0, 0)
    xmap = lambda i, be, nv: (jnp.where(nv[i] > 0, i, 0), 0)
    grid_spec = pltpu.PrefetchScalarGridSpec(
        num_scalar_prefetch=2,
        grid=(n_blocks,),
        in_specs=[pl.BlockSpec((MOE_BLOCK, D_MODEL), xmap),
                  pl.BlockSpec((1, D_MODEL, D_EXPERT), wmap),
                  pl.BlockSpec((1, D_MODEL, D_EXPERT), wmap),
                  pl.BlockSpec((1, D_EXPERT, D_MODEL), wmap)],
        out_specs=pl.BlockSpec((MOE_BLOCK, D_MODEL), lambda i, be, nv: (i, 0)),
        scratch_shapes=[pltpu.VMEM((D_MODEL, D_EXPERT), BF16),
                        pltpu.VMEM((D_MODEL, D_EXPERT), BF16),
                        pltpu.VMEM((D_EXPERT, D_MODEL), BF16)])
    return pl.pallas_call(
        _moe_kernel,
        grid_spec=grid_spec,
        out_shape=jax.ShapeDtypeStruct((n_blocks * MOE_BLOCK, D_MODEL), F32),
        compiler_params=_cparams(("arbitrary",)),
        name="moe",
    )(blk_e, nvalid, x_sorted, w_gate, w_up, w_down)


def _combine_kernel(dcur_ref, dnext_ref, route_ref, x1_ref, mod_ref, g_ref, y_hbm, o_ref, ybuf, sems):
    i = pl.program_id(0)
    last = pl.num_programs(0) - 1
    tm = x1_ref.shape[0]

    def gather(dest_ref, buf):
        for t in range(tm):
            for slot in range(2):
                pltpu.make_async_copy(y_hbm.at[pl.ds(dest_ref[0, 0, 2 * t + slot], 1), :],
                                      ybuf.at[buf, slot, pl.ds(t, 1), :], sems.at[buf]).start(priority=slot)

    @pl.when(i == 0)
    def _():
        gather(dcur_ref, 0)

    for buf in range(2):
        @pl.when(i % 2 == buf)
        def _():
            @pl.when(i < last)
            def _():
                gather(dnext_ref, 1 - buf)

            for slot in range(2):
                pltpu.make_async_copy(y_hbm.at[pl.ds(0, tm), :], ybuf.at[buf, slot], sems.at[buf]).wait()
            route = route_ref[...]
            moe = ybuf[buf, 0] * route[:, 2:3] + ybuf[buf, 1] * route[:, 3:4]
            o_ref[...] = x1_ref[...] + _per_seq(_rms(moe, g_ref[...]), mod_ref[...], lambda y, m: y * m[:, 5:6])


def _combine(dest, route, y_sorted, x1, mod8, g_post_ffn, tm, seq_rows, mod_row0):
    n = x1.shape[0]
    steps = n // tm
    row = lambda i: (i, 0)
    dest3 = dest.reshape(steps, 1, 2 * tm)
    return pl.pallas_call(
        _combine_kernel,
        grid=(steps,),
        in_specs=[pl.BlockSpec((1, 1, 2 * tm), lambda i: (i, 0, 0), memory_space=pltpu.SMEM),
                  pl.BlockSpec((1, 1, 2 * tm), lambda i: (jnp.minimum(i + 1, steps - 1), 0, 0),
                               memory_space=pltpu.SMEM),
                  pl.BlockSpec((tm, LANES), row),
                  pl.BlockSpec((tm, D_MODEL), row),
                  _mod_spec(tm, seq_rows, mod_row0),
                  pl.BlockSpec((1, D_MODEL), lambda i: (0, 0)),
                  pl.BlockSpec(memory_space=pl.ANY)],
        out_specs=pl.BlockSpec((tm, D_MODEL), row),
        out_shape=jax.ShapeDtypeStruct((n, D_MODEL), F32),
        scratch_shapes=[pltpu.VMEM((2, 2, tm, D_MODEL), F32), pltpu.SemaphoreType.DMA((2,))],
        compiler_params=_cparams(("arbitrary",)),
        name="combine",
    )(dest3, dest3, route, x1, mod8, g_post_ffn, y_sorted)


def _segment_plan(counts_f, n_blocks):
    counts = counts_f[0, :N_EXPERTS].astype(jnp.int32)
    padded = (counts + MOE_BLOCK - 1) // MOE_BLOCK * MOE_BLOCK
    pad_end = jnp.cumsum(padded)
    pad_start = pad_end - padded
    blk_start = jnp.arange(n_blocks, dtype=jnp.int32) * MOE_BLOCK
    blk_e = jnp.minimum(jnp.sum((pad_end[None, :] <= blk_start[:, None]).astype(jnp.int32), axis=1),
                        N_EXPERTS - 1)
    onehot = blk_e[:, None] == jnp.arange(N_EXPERTS, dtype=jnp.int32)[None, :]
    c_blk = jnp.sum(jnp.where(onehot, counts[None, :], 0), axis=1)
    s_blk = jnp.sum(jnp.where(onehot, pad_start[None, :], 0), axis=1)
    nvalid = jnp.clip(c_blk - (blk_start - s_blk), 0, MOE_BLOCK).astype(jnp.int32)
    seg = jnp.concatenate([counts, pad_end]).astype(jnp.int32)
    return blk_e.astype(jnp.int32), nvalid, seg, pad_start


def _token_rows(route, pad_start):
    eid = route[:, 0:2].astype(jnp.int32)
    rank = route[:, 4:6].astype(jnp.int32)
    onehot = eid[:, :, None] == jnp.arange(N_EXPERTS, dtype=jnp.int32)[None, None, :]
    return rank + jnp.sum(jnp.where(onehot, pad_start[None, None, :], 0), axis=2)


def _layer(x_p, x_s, c_p, c_s, k_past, v_past, s0_s, conv_s, p):
    bp, tp, d = x_p.shape
    bs, ts, _ = x_s.shape
    n_p, n_s = bp * tp, bs * ts
    n_tok = n_p + n_s

    n_seq = bp + bs
    c_all = jnp.zeros((16, d), F32).at[:n_seq].set(jnp.concatenate([c_s, c_p], axis=0))
    mod = _ada(c_all, p['w_ada'], p['b_ada'])
    mod8 = jnp.pad(mod.reshape(16, 6, d), ((0, 0), (0, 2), (0, 0)))

    w_in = p['w_in']
    o_z, o_b, o_q = CONV_CH, CONV_CH + 512, CONV_CH + 512 + 2 * DN_HEADS
    wb = jnp.pad(w_in[:, o_b:o_q], ((0, 0), (0, LANES - 2 * DN_HEADS)))
    wb_hi = wb.astype(BF16)
    wb_lo = (wb - wb_hi.astype(F32)).astype(BF16)
    w_main = jnp.concatenate([w_in[:, :o_b].astype(BF16), w_in[:, o_q:].astype(BF16), wb_hi, wb_lo], axis=1)
    g_pre_mix = p['g_pre_mix'].reshape(1, d)

    conv_w8 = jnp.pad(p['conv_w'], ((0, 8 - CONV_W), (0, 0)))
    pad_g = lambda a: jnp.pad(a.reshape(1, DN_HEADS), ((0, 0), (DN_HEADS, LANES - 2 * DN_HEADS)))
    alog_row, dtb_row = pad_g(p['a_log']), pad_g(p['dt_bias'])
    onorm_a = p['onorm_a'].reshape(1, DN_DIM)
    onorm_b2 = jnp.tile(p['onorm_b'].reshape(1, SB_DIM), (1, 2))

    w_out16 = p['w_out'].astype(BF16)
    wr = jnp.pad(jnp.concatenate([p['w_router_group'], p['w_router_expert']], axis=1),
                 ((0, 0), (0, LANES - N_GROUPS - N_EXPERTS)))
    wr_hi = wr.astype(BF16)
    wr_lo = (wr - wr_hi.astype(F32)).astype(BF16)
    b_r = jnp.pad(jnp.concatenate([p['b_router_group'], p['b_router_expert']]).reshape(1, -1),
                  ((0, 0), (0, LANES - N_GROUPS - N_EXPERTS)))
    g_post_mix = p['g_post_mix'].reshape(1, d)
    g_pre_ffn = p['g_pre_ffn'].reshape(1, d)
    g_post_ffn = p['g_post_ffn'].reshape(1, d)

    def mixer(x, tm, tm_post, seq_rows, mod_row0, hist8, s0, k_old, v_old, chunk, n_chunks, bq, n_sub, cnt0):
        b, t, _ = x.shape
        x2d = x.reshape(b * t, d)
        a_in, z, bg, q16, kb, vb, k16, v16 = _proj(x2d, mod8, g_pre_mix, w_main, wb_hi, tm, seq_rows, mod_row0)
        r3 = lambda a: a.reshape(b, t, a.shape[-1])
        oa16, s_new = _delta(r3(a_in), r3(z), r3(bg), hist8, s0, conv_w8, alog_row, dtb_row, onorm_a,
                             chunk, n_chunks)
        k16, v16 = r3(k16), r3(v16)
        if k_old is not None:
            k16 = jnp.concatenate([k_old.reshape(b, -1, SB_WIDTH).astype(BF16), k16], axis=1)
            v16 = jnp.concatenate([v_old.reshape(b, -1, SB_WIDTH).astype(BF16), v16], axis=1)
        n_pad = (-k16.shape[1]) % KEY_TILE
        k16 = jnp.pad(k16, ((0, 0), (n_pad, 0), (0, 0)))
        v16 = jnp.pad(v16, ((0, 0), (n_pad, 0), (0, 0)))
        ob16 = _sb_attn(r3(q16), k16, v16, onorm_b2, bq, n_sub, n_pad)
        x1, h2, route, cnt = _post(oa16.reshape(b * t, 512), ob16.reshape(b * t, 512), x2d, mod8,
                                   g_post_mix, g_pre_ffn, w_out16, wr_hi, wr_lo, b_r, cnt0, tm_post, seq_rows,
                                   mod_row0)
        new_conv = r3(a_in)[:, t - (CONV_W - 1):, :]
        return (x1, h2, route, cnt, kb.reshape(b, t, SB_HEADS, SB_DIM), vb.reshape(b, t, SB_HEADS, SB_DIM),
                s_new, new_conv)

    zero_hist = jnp.zeros((bp, 8, CONV_CH), F32)
    zero_s = jnp.zeros((bp, DN_HEADS, DN_DIM, DN_DIM), F32)
    hist_s = jnp.pad(conv_s, ((0, 0), (8 - (CONV_W - 1), 0), (0, 0)))
    tm_p = min(256, tp)
    tm_post = tm_p * POST_TILE_FACTOR if n_p % (tm_p * POST_TILE_FACTOR) == 0 else tm_p
    nc_p = max(1, min(4, tp // DELTA_BLOCK))
    x1p, h2p, rp, cnt_p, kp, vp, sp, cp = mixer(x_p, tm_p, tm_post, tp, bs, zero_hist, zero_s, None, None,
                                                 min(DELTA_BLOCK, tp), nc_p, min(KEY_TILE // 2, tp), 2,
                                                 jnp.zeros((1, LANES), F32))
    x1s, h2s, rs, cnt, ks, vs, ss, cs = mixer(x_s, n_s, n_s, ts, 0, hist_s, s0_s, k_past, v_past,
                                               min(DELTA_BLOCK, ts), max(1, ts // DELTA_BLOCK),
                                               min(KEY_TILE, ts), 1, cnt_p)

    n_blocks = -(-2 * n_tok // MOE_BLOCK) + N_EXPERTS
    blk_e, nvalid, seg, pad_start = _segment_plan(cnt, n_blocks)
    dest_p = _token_rows(rp, pad_start)
    dest_s = _token_rows(rs, pad_start)
    x_sorted = _dispatch(seg, dest_p, dest_s, h2p, h2s, n_blocks, tm_p)
    y_sorted = _moe(blk_e, nvalid, x_sorted, p['w_gate'], p['w_up'], p['w_down'])
    y_p = _combine(dest_p, rp, y_sorted, x1p, mod8, g_post_ffn, tm_p, tp, bs).reshape(bp, tp, d)
    y_s = _combine(dest_s, rs, y_sorted, x1s, mod8, g_post_ffn, n_s, ts, 0).reshape(bs, ts, d)
    return y_p, y_s, kp, vp, sp, cp, ks, vs, ss, cs


def kernel(x_prompt, x_sample, c_prompt, c_sample, cache_k, cache_v, state_delta, state_conv, w_ada, b_ada, g_pre_mix, g_post_mix, g_pre_ffn, g_post_ffn, w_in, conv_w, a_log, dt_bias, onorm_a, onorm_b, w_out, w_router_group, b_router_group, w_router_expert, b_router_expert, w_gate, w_up, w_down):
    depth = w_in.shape[0]
    y_p, y_s = x_prompt, x_sample
    outs = [[] for _ in range(8)]
    for l in range(depth):
        p = dict(w_ada=w_ada[l], b_ada=b_ada[l], g_pre_mix=g_pre_mix[l], g_post_mix=g_post_mix[l],
                 g_pre_ffn=g_pre_ffn[l], g_post_ffn=g_post_ffn[l], w_in=w_in[l], conv_w=conv_w[l],
                 a_log=a_log[l], dt_bias=dt_bias[l], onorm_a=onorm_a[l], onorm_b=onorm_b[l],
                 w_out=w_out[l], w_router_group=w_router_group[l], b_router_group=b_router_group[l],
                 w_router_expert=w_router_expert[l], b_router_expert=b_router_expert[l],
                 w_gate=w_gate[l], w_up=w_up[l], w_down=w_down[l])
        res = _layer(y_p, y_s, c_prompt, c_sample, cache_k[l], cache_v[l], state_delta[l], state_conv[l], p)
        y_p, y_s = res[0], res[1]
        for lst, r in zip(outs, res[2:]):
            lst.append(r)
    return (y_p, y_s) + tuple(jnp.stack(o) for o in outs)
```

```python
import functools
import math

import jax
import jax.numpy as jnp
from jax import lax
from jax.experimental import pallas as pl
from jax.experimental.pallas import tpu as pltpu

F32 = jnp.float32
BF16 = jnp.bfloat16

D_MODEL = 1024
DN_HEADS = 4
DN_DIM = 128
CONV_W = 4
CONV_CH = DN_HEADS * 3 * DN_DIM
DELTA_BLOCK = 64
SB_HEADS = 8
SB_DIM = 64
SB_WIDTH = SB_HEADS * SB_DIM
N_GROUPS = 4
EXPERTS_PER_GROUP = 8
N_EXPERTS = N_GROUPS * EXPERTS_PER_GROUP
D_EXPERT = D_MODEL // 2
MOE_BLOCK = 256
EPS = 1e-6

LANES = 128
KEY_TILE = 128
ATTN_WINDOW = 3 * KEY_TILE
WINDOW_BLOCKS = 4
EXP_ZERO_BELOW = -104.0
VMEM_LIMIT = 56 * 1024 * 1024
POST_TILE_FACTOR = 2


def _cparams(sem):
    return pltpu.CompilerParams(dimension_semantics=sem, vmem_limit_bytes=VMEM_LIMIT)


def _split(a):
    hi = a.astype(BF16)
    lo = (a - hi.astype(F32)).astype(BF16)
    return hi, lo


def _dot(a, b, dims=(((1,), (0,)), ((), ()))):
    return lax.dot_general(a, b, dims, preferred_element_type=F32)


def _dot_small_int_lhs(a, b):
    a16 = a.astype(BF16)
    return _dot(jnp.concatenate([a16, a16], axis=1), jnp.concatenate(_split(b), axis=0))


_NT = (((1,), (1,)), ((), ()))
_TN = (((0,), (0,)), ((), ()))


def _silu(x):
    return x * jax.nn.sigmoid(x)


SOFTPLUS_LINEAR_ABOVE = 80.0


def _softplus(x):
    return jnp.where(x > SOFTPLUS_LINEAR_ABOVE, x, jnp.log(1.0 + jnp.exp(jnp.minimum(x, SOFTPLUS_LINEAR_ABOVE))))


def _ada_kernel(c_ref, w_ref, b_ref, o_ref):
    s = _silu(c_ref[...]).astype(BF16)
    o_ref[...] = _dot(s, w_ref[...].astype(BF16)) + b_ref[...]


def _ada(c_all, w_ada, b_ada):
    rows = c_all.shape[0]
    n = w_ada.shape[1]
    tn = 1024
    return pl.pallas_call(
        _ada_kernel,
        grid=(n // tn,),
        in_specs=[pl.BlockSpec((rows, D_MODEL), lambda j: (0, 0)),
                  pl.BlockSpec((D_MODEL, tn), lambda j: (0, j)),
                  pl.BlockSpec((1, tn), lambda j: (0, j))],
        out_specs=pl.BlockSpec((rows, tn), lambda j: (0, j)),
        out_shape=jax.ShapeDtypeStruct((rows, n), F32),
        compiler_params=_cparams(("arbitrary",)),
        name="ada",
    )(c_all, w_ada, b_ada.reshape(1, n))


def _rms(x, gain):
    return x * lax.rsqrt(jnp.mean(x * x, axis=-1, keepdims=True) + EPS) * gain


def _per_seq(y, mod, fn):
    n_seq = mod.shape[0]
    ys = y.reshape(n_seq, y.shape[0] // n_seq, y.shape[1])
    return fn(ys, mod).reshape(y.shape)


def _mod_spec(tm, seq_rows, mod_row0):
    if tm >= seq_rows:
        n_seq = tm // seq_rows
        assert tm % seq_rows == 0 and mod_row0 % n_seq == 0
        return pl.BlockSpec((n_seq, 8, D_MODEL), lambda i: (mod_row0 // n_seq + i, 0, 0))
    assert seq_rows % tm == 0
    return pl.BlockSpec((1, 8, D_MODEL), lambda i: (mod_row0 + (i * tm) // seq_rows, 0, 0))


def _proj_kernel(x_ref, mod_ref, g_ref, wm_ref, wbh_ref,
                 a_ref, z_ref, bg_ref, q_ref, k_ref, v_ref, k16_ref, v16_ref):
    h = _per_seq(_rms(x_ref[...], g_ref[...]), mod_ref[...], lambda y, m: y * (1.0 + m[:, 1:2]) + m[:, 0:1])
    hh, hl = _split(h)
    p = _dot(hh, wm_ref[...])
    a_ref[...] = p[:, 0:CONV_CH]
    z_ref[...] = p[:, CONV_CH:CONV_CH + 512]
    o = CONV_CH + 512
    q_ref[...] = (p[:, o:o + 512] * (SB_DIM ** -0.5)).astype(BF16)
    k = p[:, o + 512:o + 1024]
    v = p[:, o + 1024:o + 1536]
    k_ref[...] = k
    v_ref[...] = v
    k16_ref[...] = k.astype(BF16)
    v16_ref[...] = v.astype(BF16)
    bg_ref[...] = p[:, o + 1536:o + 1536 + LANES] + (_dot(hl, wbh_ref[...]) + p[:, o + 1536 + LANES:])


def _proj(x2d, mod8, g_pre, w_main, wb_hi, tm, seq_rows, mod_row0):
    n = x2d.shape[0]
    nm = w_main.shape[1]
    row = lambda i: (i, 0)
    const = lambda i: (0, 0)
    outs = [(CONV_CH, F32), (512, F32), (LANES, F32), (512, BF16), (512, F32), (512, F32),
            (512, BF16), (512, BF16)]
    return pl.pallas_call(
        _proj_kernel,
        grid=(n // tm,),
        in_specs=[pl.BlockSpec((tm, D_MODEL), row),
                  _mod_spec(tm, seq_rows, mod_row0),
                  pl.BlockSpec((1, D_MODEL), const),
                  pl.BlockSpec((D_MODEL, nm), const),
                  pl.BlockSpec((D_MODEL, LANES), const)],
        out_specs=[pl.BlockSpec((tm, w), row) for w, _ in outs],
        out_shape=[jax.ShapeDtypeStruct((n, w), dt) for w, dt in outs],
        compiler_params=_cparams(("arbitrary",)),
        name="proj",
    )(x2d, mod8, g_pre, w_main, wb_hi)


def _delta_kernel(a_ref, z_ref, bg_ref, hist0_ref, s0_ref, cw_ref, alog_ref, dtb_ref, on_ref,
                  o_ref, sfin_ref, hist_sc, s_sc, *, chunk, n_chunks):
    t_idx = pl.program_id(1)
    tt = chunk * n_chunks

    @pl.when(t_idx == 0)
    def _():
        hist_sc[...] = hist0_ref[0]
        s_sc[...] = s0_ref[0]

    x = a_ref[0]
    xx = jnp.concatenate([hist_sc[...], x], axis=0)
    cw = cw_ref[...]
    conv = x * cw[CONV_W - 1:CONV_W]
    for s in range(1, CONV_W):
        conv = conv + pltpu.roll(xx, s, 0)[8:] * cw[CONV_W - 1 - s:CONV_W - s]
    conv = _silu(conv)
    hist_sc[...] = x[tt - 8:tt]

    bg = bg_ref[0]
    lane = lax.broadcasted_iota(jnp.int32, (1, LANES), 1)
    g_lane = (lane >= DN_HEADS) & (lane < 2 * DN_HEADS)
    neg_a = jnp.where(g_lane, -jnp.exp(alog_ref[...]), 0.0)
    beta_all = jax.nn.sigmoid(bg)
    g_all = neg_a * _softplus(bg + dtb_ref[...])

    ri = lax.broadcasted_iota(jnp.int32, (chunk, chunk), 0)
    ci = lax.broadcasted_iota(jnp.int32, (chunk, chunk), 1)
    incl = ri >= ci
    strict = ri > ci
    tri = incl.astype(F32)
    eye = (ri == ci).astype(F32)
    ones_cc = jnp.ones((chunk, chunk), F32)
    onorm = on_ref[...]

    chunks = range(n_chunks)
    units = [(c, h) for c in chunks for h in range(DN_HEADS)]
    rows = lambda c: slice(c * chunk, (c + 1) * chunk)

    gc_all = [_dot_small_int_lhs(tri, g_all[rows(c)]) for c in chunks]
    diag = [jnp.concatenate([eye * gc_all[c][:, DN_HEADS + h:DN_HEADS + h + 1] for h in range(DN_HEADS)], axis=1)
            for c in chunks]
    gc_rows = [_dot_small_int_lhs(ones_cc, diag[c]) for c in chunks]

    q, k, v, beta, gcol, glast, eg, decay, kbeta = {}, {}, {}, {}, {}, {}, {}, {}, {}
    for (c, h) in units:
        lo = h * DN_DIM
        qq = conv[rows(c), lo:lo + DN_DIM]
        kk = conv[rows(c), 512 + lo:512 + lo + DN_DIM]
        u_ = (c, h)
        q[u_] = qq * lax.rsqrt(jnp.sum(qq * qq, axis=-1, keepdims=True) + EPS) * (DN_DIM ** -0.5)
        k[u_] = kk * lax.rsqrt(jnp.sum(kk * kk, axis=-1, keepdims=True) + EPS)
        v[u_] = conv[rows(c), 1024 + lo:1024 + lo + DN_DIM]
        beta[u_] = beta_all[rows(c), h:h + 1]
        gcol[u_] = gc_all[c][:, DN_HEADS + h:DN_HEADS + h + 1]
        glast[u_] = gcol[u_][chunk - 1:chunk, :]
        eg[u_] = jnp.exp(gcol[u_])
        grow = gc_rows[c][:, h * chunk:(h + 1) * chunk]
        decay[u_] = jnp.where(incl, jnp.exp(jnp.minimum(gcol[u_] - grow, 0.0)), 0.0)
        kbeta[u_] = k[u_] * beta[u_]

    k16 = {u_: k[u_].astype(BF16) for u_ in units}
    kq = {u_: _dot(jnp.concatenate([kbeta[u_], q[u_]], axis=0).astype(BF16), k16[u_], _NT) for u_ in units}
    lower = {u_: jnp.where(strict, kq[u_][:chunk] * decay[u_], 0.0) for u_ in units}
    attn16 = {u_: jnp.where(incl, kq[u_][chunk:] * decay[u_], 0.0).astype(BF16) for u_ in units}

    sol = {u_: jnp.concatenate([v[u_] * beta[u_], kbeta[u_] * eg[u_]], axis=1) for u_ in units}
    lp = lower
    p = 1
    while p < chunk:
        lsp = {u_: _split(lp[u_]) for u_ in units}
        ssp = {u_: _split(sol[u_]) for u_ in units}
        lcat = {u_: jnp.concatenate([lsp[u_][0], lsp[u_][1], lsp[u_][0]], axis=1) for u_ in units}
        upd = {u_: _dot(lcat[u_], jnp.concatenate([ssp[u_][0], ssp[u_][0], ssp[u_][1]], axis=0)) for u_ in units}
        sol = {u_: (sol[u_] - upd[u_]) if p == 1 else (sol[u_] + upd[u_]) for u_ in units}
        if 2 * p < chunk:
            lp = {u_: _dot(lcat[u_], jnp.concatenate([lsp[u_][0], lsp[u_][0], lsp[u_][1]], axis=0))
                  for u_ in units}
        p *= 2

    usol = {u_: sol[u_][:, :DN_DIM] for u_ in units}
    wq16 = {u_: jnp.concatenate([sol[u_][:, DN_DIM:], q[u_] * eg[u_]], axis=0).astype(BF16) for u_ in units}
    kd16 = {u_: (k[u_] * jnp.exp(glast[u_] - gcol[u_])).astype(BF16) for u_ in units}

    for c in chunks:
        hs = range(DN_HEADS)
        s_old = [s_sc[h] for h in hs]
        ws = [_dot(wq16[(c, h)], s_old[h].astype(BF16)) for h in hs]
        v_new = [(usol[(c, h)] - ws[h][:chunk]).astype(BF16) for h in hs]
        o_in = [_dot(attn16[(c, h)], v_new[h]) for h in hs]
        ds = [_dot(kd16[(c, h)], v_new[h], _TN) for h in hs]
        for h in hs:
            lo = h * DN_DIM
            s_sc[h] = s_old[h] * jnp.exp(glast[(c, h)]) + ds[h]
            o = ws[h][chunk:] + o_in[h]
            zg = z_ref[0, rows(c), lo:lo + DN_DIM]
            o_ref[0, rows(c), lo:lo + DN_DIM] = (_rms(o, onorm) * _silu(zg)).astype(BF16)

    @pl.when(t_idx == pl.num_programs(1) - 1)
    def _():
        sfin_ref[0] = s_sc[...]


def _delta(a_in, z, bg, hist8, s0, conv_w8, alog_row, dtb_row, onorm_a, chunk, n_chunks):
    b, t, _ = a_in.shape
    tt = chunk * n_chunks
    tile = lambda bi, ti: (bi, ti, 0)
    per_b3 = lambda bi, ti: (bi, 0, 0)
    per_b4 = lambda bi, ti: (bi, 0, 0, 0)
    const = lambda bi, ti: (0, 0)
    kern = functools.partial(_delta_kernel, chunk=chunk, n_chunks=n_chunks)
    return pl.pallas_call(
        kern,
        grid=(b, t // tt),
        in_specs=[pl.BlockSpec((1, tt, CONV_CH), tile),
                  pl.BlockSpec((1, tt, 512), tile),
                  pl.BlockSpec((1, tt, LANES), tile),
                  pl.BlockSpec((1, 8, CONV_CH), per_b3),
                  pl.BlockSpec((1, DN_HEADS, DN_DIM, DN_DIM), per_b4),
                  pl.BlockSpec((8, CONV_CH), const),
                  pl.BlockSpec((1, LANES), const),
                  pl.BlockSpec((1, LANES), const),
                  pl.BlockSpec((1, DN_DIM), const)],
        out_specs=[pl.BlockSpec((1, tt, 512), tile),
                   pl.BlockSpec((1, DN_HEADS, DN_DIM, DN_DIM), per_b4)],
        out_shape=[jax.ShapeDtypeStruct((b, t, 512), BF16),
                   jax.ShapeDtypeStruct((b, DN_HEADS, DN_DIM, DN_DIM), F32)],
        scratch_shapes=[pltpu.VMEM((8, CONV_CH), F32),
                        pltpu.VMEM((DN_HEADS, DN_DIM, DN_DIM), F32)],
        compiler_params=_cparams(("arbitrary", "arbitrary")),
        name="delta",
    )(a_in, z, bg, hist8, s0, conv_w8, alog_row, dtb_row, onorm_a)


def _sb_kernel(q_ref, *refs, bq, n_sub, n_pad, q_off):
    kwin = refs[:WINDOW_BLOCKS]
    vwin = refs[WINDOW_BLOCKS:2 * WINDOW_BLOCKS]
    k_hbm, v_hbm, on_ref, o_ref, kbuf, vbuf, qsel, acc, carry, sem = refs[2 * WINDOW_BLOCKS:]
    b = pl.program_id(0)
    i = pl.program_id(1)
    g = bq * n_sub
    qend_step = q_off + (i + 1) * g
    win_start = qend_step - WINDOW_BLOCKS * KEY_TILE
    n_pairs = SB_HEADS // 2
    heads = range(SB_HEADS)

    half_lane = lax.broadcasted_iota(jnp.int32, (bq, LANES), 1) < SB_DIM
    rj = lax.broadcasted_iota(jnp.int32, (2 * KEY_TILE, 2 * KEY_TILE), 0) % KEY_TILE
    cj = lax.broadcasted_iota(jnp.int32, (2 * KEY_TILE, 2 * KEY_TILE), 1)
    suffix2 = ((rj > cj) | (cj >= KEY_TILE)).astype(BF16)

    def window(blocks, off, p):
        parts = []
        for blk in range(WINDOW_BLOCKS):
            lo, hi = max(off, blk * KEY_TILE), min(off + ATTN_WINDOW, (blk + 1) * KEY_TILE)
            if lo < hi:
                ref = blocks[WINDOW_BLOCKS - 1 - blk]
                parts.append(ref[0, lo - blk * KEY_TILE:hi - blk * KEY_TILE, p * LANES:(p + 1) * LANES])
        return jnp.concatenate(parts, axis=0)

    n_t = ATTN_WINDOW // KEY_TILE
    row = lax.broadcasted_iota(jnp.int32, (bq, ATTN_WINDOW), 0)
    col = lax.broadcasted_iota(jnp.int32, (bq, ATTN_WINDOW), 1)
    causal = col < row + (ATTN_WINDOW - bq)
    subs = range(n_sub)
    offs = [WINDOW_BLOCKS * KEY_TILE - g + (s + 1) * bq - ATTN_WINDOW for s in subs]
    mask = [causal & (col >= n_pad - (win_start + offs[s])) for s in subs]
    zs, sp, cs_all = {}, {}, {}
    for s in subs:
        for p in range(n_pairs):
            qf = q_ref[0, s * bq:(s + 1) * bq, p * LANES:(p + 1) * LANES].astype(F32)
            qq = jnp.concatenate([jnp.where(half_lane, qf, 0.0), jnp.where(half_lane, 0.0, qf)], axis=0)
            zz = _dot(qq.astype(BF16), window(kwin, offs[s], p), _NT)
            zs[(s, 2 * p)], zs[(s, 2 * p + 1)] = zz[:bq], zz[bq:]
    for s in subs:
        pieces = []
        for h in heads:
            sp[(s, h)] = _softplus(zs[(s, h)])
            hi, lo = _split(jnp.where(mask[s], -sp[(s, h)], 0.0))
            for j in range(n_t):
                c0 = ATTN_WINDOW - (j + 1) * KEY_TILE
                pieces.append(jnp.concatenate([hi[:, c0:c0 + KEY_TILE], lo[:, c0:c0 + KEY_TILE]], axis=1))
        cs_all[s] = _dot(jnp.concatenate(pieces, axis=0), suffix2)
    done = []
    for s in subs:
        worst = jnp.full((bq, KEY_TILE), -jnp.inf, F32)
        a = {}
        for h in heads:
            run = None
            cols = []
            for j in range(n_t):
                r0 = (h * n_t + j) * bq
                cs = cs_all[s][r0:r0 + bq]
                cols.append(cs[:, :KEY_TILE] if run is None else cs[:, :KEY_TILE] + run)
                run = cs[:, KEY_TILE:] if run is None else run + cs[:, KEY_TILE:]
            within = jnp.concatenate(cols[::-1], axis=1)
            a[h] = jnp.where(mask[s], jnp.exp((zs[(s, h)] - sp[(s, h)]) + within), 0.0).astype(BF16)
            carry[s, h] = run
            worst = jnp.maximum(worst, run)
        for p in range(n_pairs):
            pv = _dot(jnp.concatenate([a[2 * p], a[2 * p + 1]], axis=0), window(vwin, offs[s], p))
            acc[s, p] = jnp.where(half_lane, pv[:bq], pv[bq:])
        done.append((jnp.max(worst) < EXP_ZERO_BELOW).astype(jnp.int32))

    rj1 = lax.broadcasted_iota(jnp.int32, (KEY_TILE, 2 * KEY_TILE), 0)
    cj1 = lax.broadcasted_iota(jnp.int32, (KEY_TILE, 2 * KEY_TILE), 1)
    suffix1 = ((rj1 > cj1) | (cj1 >= KEY_TILE)).astype(BF16)
    col1 = lax.broadcasted_iota(jnp.int32, (bq, KEY_TILE), 1)
    for s in subs:
        swept_from = win_start + offs[s]

        @pl.when((done[s] == 0) & (swept_from > n_pad))
        def _():
            for p in range(n_pairs):
                qf = q_ref[0, s * bq:(s + 1) * bq, p * LANES:(p + 1) * LANES].astype(F32)
                qsel[2 * p] = jnp.where(half_lane, qf, 0.0).astype(BF16)
                qsel[2 * p + 1] = jnp.where(half_lane, 0.0, qf).astype(BF16)

            def body(state):
                upper, _ = state
                start = jnp.maximum(upper - KEY_TILE, 0)
                copies = []
                for p in range(n_pairs):
                    for src, dst in ((k_hbm, kbuf), (v_hbm, vbuf)):
                        cp = pltpu.make_async_copy(
                            src.at[b, pl.ds(pl.multiple_of(start, 16), KEY_TILE), pl.ds(p * LANES, LANES)],
                            dst.at[p], sem)
                        cp.start()
                        copies.append(cp)
                for cp in copies:
                    cp.wait()
                kpos = start + col1
                m1 = (kpos < upper) & (kpos >= n_pad)
                zz = [_dot(qsel[h], kbuf[h // 2], _NT) for h in heads]
                spp = [_softplus(zz[h]) for h in heads]
                pcs = []
                for h in heads:
                    pcs.extend(_split(jnp.where(m1, -spp[h], 0.0)))
                cs1 = _dot(jnp.concatenate(pcs, axis=0), suffix1)
                worst = jnp.full((bq, KEY_TILE), -jnp.inf, F32)
                aa = []
                for h in heads:
                    c = cs1[2 * h * bq:(2 * h + 1) * bq] + cs1[(2 * h + 1) * bq:(2 * h + 2) * bq]
                    c_old = carry[s, h]
                    aa.append(jnp.where(m1, jnp.exp((zz[h] - spp[h]) + c[:, :KEY_TILE] + c_old), 0.0).astype(BF16))
                    c_new = c_old + c[:, KEY_TILE:]
                    carry[s, h] = c_new
                    worst = jnp.maximum(worst, c_new)
                for p in range(n_pairs):
                    pv0 = _dot(aa[2 * p], vbuf[p])
                    pv1 = _dot(aa[2 * p + 1], vbuf[p])
                    acc[s, p] = acc[s, p] + jnp.where(half_lane, pv0, pv1)
                return start, (jnp.max(worst) < EXP_ZERO_BELOW).astype(jnp.int32)

            lax.while_loop(lambda st: (st[0] > n_pad) & (st[1] == 0), body, (swept_from, jnp.int32(0)))

    onb = on_ref[...]
    for s in subs:
        for p in range(n_pairs):
            o = acc[s, p]
            sq = o * o
            s_lo = jnp.sum(jnp.where(half_lane, sq, 0.0), axis=-1, keepdims=True)
            s_hi = jnp.sum(jnp.where(half_lane, 0.0, sq), axis=-1, keepdims=True)
            ms = jnp.where(half_lane, s_lo, s_hi) * (1.0 / SB_DIM)
            o_ref[0, s * bq:(s + 1) * bq, p * LANES:(p + 1) * LANES] = (o * lax.rsqrt(ms + EPS) * onb).astype(BF16)


def _sb_attn(q16, k16p, v16p, onorm_b2, bq, n_sub, n_pad):
    b, tq, _ = q16.shape
    tkp = k16p.shape[1]
    q_off = tkp - tq
    g = bq * n_sub
    assert tq % g == 0 and g <= KEY_TILE and bq % 16 == 0
    assert all((q_off + (i + 1) * g) % KEY_TILE == 0 for i in range(tq // g))

    def kmap(back):
        def f(bi, i):
            last = (q_off + (i + 1) * g) // KEY_TILE - 1
            return (bi, jnp.maximum(last - back, 0), 0)
        return f

    qmap = lambda bi, i: (bi, i, 0)
    kern = functools.partial(_sb_kernel, bq=bq, n_sub=n_sub, n_pad=n_pad, q_off=q_off)
    kspec = [pl.BlockSpec((1, KEY_TILE, SB_WIDTH), kmap(back)) for back in range(WINDOW_BLOCKS)]
    return pl.pallas_call(
        kern,
        grid=(b, tq // g),
        in_specs=[pl.BlockSpec((1, g, SB_WIDTH), qmap)] + kspec + kspec
                 + [pl.BlockSpec(memory_space=pl.ANY), pl.BlockSpec(memory_space=pl.ANY),
                    pl.BlockSpec((1, LANES), lambda bi, i: (0, 0))],
        out_specs=pl.BlockSpec((1, g, SB_WIDTH), qmap),
        out_shape=jax.ShapeDtypeStruct((b, tq, SB_WIDTH), BF16),
        scratch_shapes=[pltpu.VMEM((SB_HEADS // 2, KEY_TILE, LANES), BF16),
                        pltpu.VMEM((SB_HEADS // 2, KEY_TILE, LANES), BF16),
                        pltpu.VMEM((SB_HEADS, bq, LANES), BF16),
                        pltpu.VMEM((n_sub, SB_HEADS // 2, bq, LANES), F32),
                        pltpu.VMEM((n_sub, SB_HEADS, bq, KEY_TILE), F32),
                        pltpu.SemaphoreType.DMA(())],
        compiler_params=_cparams(("arbitrary", "arbitrary")),
        name="sb_attn",
    )(q16, *([k16p] * WINDOW_BLOCKS), *([v16p] * WINDOW_BLOCKS), k16p, v16p, onorm_b2)


def _post_kernel(oa_ref, ob_ref, x_ref, mod_ref, gpm_ref, gpf_ref, wo_ref, wrh_ref, wrl_ref, br_ref, cnt0_ref,
                 x1_ref, h2_ref, route_ref, cnt_ref):
    @pl.when(pl.program_id(0) == 0)
    def _():
        cnt_ref[...] = cnt0_ref[...]

    mod = mod_ref[...]
    mix = _dot(oa_ref[...], wo_ref[0:512, :]) + _dot(ob_ref[...], wo_ref[512:1024, :])
    x1 = x_ref[...] + _per_seq(_rms(mix, gpm_ref[...]), mod, lambda y, m: y * m[:, 2:3])
    x1_ref[...] = x1
    h2 = _per_seq(_rms(x1, gpf_ref[...]), mod, lambda y, m: y * (1.0 + m[:, 4:5]) + m[:, 3:4])
    h2_ref[...] = h2
    hh, hl = _split(h2)
    wrh = wrh_ref[...]
    logits = _dot(hh, wrh) + (_dot(hl, wrh) + _dot(hh, wrl_ref[...])) + br_ref[...]
    lane = lax.broadcasted_iota(jnp.int32, logits.shape, 1).astype(F32)
    neg = -jnp.inf
    nl = float(LANES)
    lg = jnp.where(lane < N_GROUPS, logits, neg)
    gmax = jnp.max(lg, axis=-1, keepdims=True)
    grp = jnp.min(jnp.where(lg == gmax, lane, nl), axis=-1, keepdims=True)
    p_grp = 1.0 / jnp.sum(jnp.exp(lg - gmax), axis=-1, keepdims=True)
    first = N_GROUPS + grp * EXPERTS_PER_GROUP
    le = jnp.where((lane >= first) & (lane < first + EXPERTS_PER_GROUP), logits, neg)
    emax = jnp.max(le, axis=-1, keepdims=True)
    i1 = jnp.min(jnp.where(le == emax, lane, nl), axis=-1, keepdims=True)
    esum = jnp.sum(jnp.exp(le - emax), axis=-1, keepdims=True)
    le2 = jnp.where(lane == i1, neg, le)
    e2max = jnp.max(le2, axis=-1, keepdims=True)
    i2 = jnp.min(jnp.where(le2 == e2max, lane, nl), axis=-1, keepdims=True)
    p1 = 1.0 / esum
    p2 = jnp.exp(e2max - emax) / esum
    w1 = p_grp * p1 / (p1 + p2)
    w2 = p_grp * p2 / (p1 + p2)
    e1 = i1 - N_GROUPS
    e2 = i2 - N_GROUPS
    hot1 = (lane == e1).astype(F32)
    hot2 = (lane == e2).astype(F32)
    both = hot1 + hot2
    tm = logits.shape[0]
    ti = lax.broadcasted_iota(jnp.int32, (tm, tm), 0)
    tj = lax.broadcasted_iota(jnp.int32, (tm, tm), 1)
    earlier = _dot((ti > tj).astype(BF16), both.astype(BF16)) + cnt_ref[...]
    rank1 = jnp.sum(hot1 * earlier, axis=-1, keepdims=True)
    rank2 = jnp.sum(hot2 * (earlier + hot1), axis=-1, keepdims=True)
    cnt_ref[...] = cnt_ref[...] + jnp.sum(both, axis=0, keepdims=True)
    out = jnp.where(lane == 0.0, e1, 0.0)
    out = jnp.where(lane == 1.0, e2, out)
    out = jnp.where(lane == 2.0, w1, out)
    out = jnp.where(lane == 3.0, w2, out)
    out = jnp.where(lane == 4.0, rank1, out)
    out = jnp.where(lane == 5.0, rank2, out)
    route_ref[...] = out


def _post(oa16, ob16, x2d, mod8, g_post_mix, g_pre_ffn, w_out16, wr_hi, wr_lo, b_r, cnt0, tm, seq_rows,
          mod_row0):
    n = x2d.shape[0]
    row = lambda i: (i, 0)
    const = lambda i: (0, 0)
    return pl.pallas_call(
        _post_kernel,
        grid=(n // tm,),
        in_specs=[pl.BlockSpec((tm, 512), row),
                  pl.BlockSpec((tm, 512), row),
                  pl.BlockSpec((tm, D_MODEL), row),
                  _mod_spec(tm, seq_rows, mod_row0),
                  pl.BlockSpec((1, D_MODEL), const),
                  pl.BlockSpec((1, D_MODEL), const),
                  pl.BlockSpec((D_MODEL, D_MODEL), const),
                  pl.BlockSpec((D_MODEL, LANES), const),
                  pl.BlockSpec((D_MODEL, LANES), const),
                  pl.BlockSpec((1, LANES), const),
                  pl.BlockSpec((1, LANES), const)],
        out_specs=[pl.BlockSpec((tm, D_MODEL), row),
                   pl.BlockSpec((tm, D_MODEL), row),
                   pl.BlockSpec((tm, LANES), row),
                   pl.BlockSpec((1, LANES), const)],
        out_shape=[jax.ShapeDtypeStruct((n, D_MODEL), F32),
                   jax.ShapeDtypeStruct((n, D_MODEL), F32),
                   jax.ShapeDtypeStruct((n, LANES), F32),
                   jax.ShapeDtypeStruct((1, LANES), F32)],
        compiler_params=_cparams(("arbitrary",)),
        name="post",
    )(oa16, ob16, x2d, mod8, g_post_mix, g_pre_ffn, w_out16, wr_hi, wr_lo, b_r, cnt0)


def _dispatch_kernel(seg_ref, dp_ref, ds_ref, hp_ref, hs_ref, xs_hbm, zbuf, stage, sem, stage_sems, *,
                     n_blocks, n_prompt_steps):
    i = pl.program_id(0)
    last_step = pl.num_programs(0) - 1

    @pl.when(i == 0)
    def _():
        zbuf[...] = jnp.zeros_like(zbuf)

        def zero_block(row0):
            return pltpu.make_async_copy(zbuf, xs_hbm.at[pl.ds(pl.multiple_of(row0, MOE_BLOCK), MOE_BLOCK), :], sem)

        for e in range(N_EXPERTS):
            @pl.when(seg_ref[e] > 0)
            def _():
                zero_block(seg_ref[N_EXPERTS + e] - MOE_BLOCK).start()
        for e in range(N_EXPERTS):
            @pl.when(seg_ref[e] > 0)
            def _():
                zero_block(seg_ref[N_EXPERTS + e] - MOE_BLOCK).wait()

        used = seg_ref[2 * N_EXPERTS - 1] // MOE_BLOCK

        def fill(b, c):
            cp = zero_block(b * MOE_BLOCK)
            cp.start()
            cp.wait()
            return c

        lax.fori_loop(used, n_blocks, fill, 0)

    def scatter(h_ref, dest_ref, sem_):
        for t in range(h_ref.shape[0]):
            for slot in range(2):
                pltpu.make_async_copy(h_ref.at[pl.ds(t, 1), :],
                                      xs_hbm.at[pl.ds(dest_ref[0, 0, 2 * t + slot], 1), :],
                                      sem_).start(priority=slot)

    def drain(h_ref, sem_):
        for slot in range(2):
            pltpu.make_async_copy(h_ref, xs_hbm.at[pl.ds(0, h_ref.shape[0]), :], sem_).wait()

    for buf in range(2):
        @pl.when((i < last_step) & (i % 2 == buf))
        def _():
            @pl.when(i >= 2)
            def _():
                drain(stage.at[buf], stage_sems.at[buf])
            stage[buf] = hp_ref[...]
            scatter(stage.at[buf], dp_ref, stage_sems.at[buf])

    @pl.when(i == last_step)
    def _():
        scatter(hs_ref, ds_ref, sem)
        drain(hs_ref, sem)
        for buf in range(min(2, n_prompt_steps)):
            drain(stage.at[buf], stage_sems.at[buf])


def _dispatch(seg, dest_p, dest_s, h2p, h2s, n_blocks, tm):
    n_p, n_s = h2p.shape[0], h2s.shape[0]
    steps_p = n_p // tm
    pmap3 = lambda i, sg: (jnp.minimum(i, steps_p - 1), 0, 0)
    pmap2 = lambda i, sg: (jnp.minimum(i, steps_p - 1), 0)
    grid_spec = pltpu.PrefetchScalarGridSpec(
        num_scalar_prefetch=1,
        grid=(steps_p + 1,),
        in_specs=[pl.BlockSpec((1, 1, 2 * tm), pmap3, memory_space=pltpu.SMEM),
                  pl.BlockSpec((1, 1, 2 * n_s), lambda i, sg: (0, 0, 0), memory_space=pltpu.SMEM),
                  pl.BlockSpec((tm, D_MODEL), pmap2),
                  pl.BlockSpec((n_s, D_MODEL), lambda i, sg: (0, 0))],
        out_specs=pl.BlockSpec(memory_space=pl.ANY),
        scratch_shapes=[pltpu.VMEM((MOE_BLOCK, D_MODEL), F32), pltpu.VMEM((2, tm, D_MODEL), F32),
                        pltpu.SemaphoreType.DMA(()), pltpu.SemaphoreType.DMA((2,))])
    return pl.pallas_call(
        functools.partial(_dispatch_kernel, n_blocks=n_blocks, n_prompt_steps=steps_p),
        grid_spec=grid_spec,
        out_shape=jax.ShapeDtypeStruct((n_blocks * MOE_BLOCK, D_MODEL), F32),
        compiler_params=_cparams(("arbitrary",)),
        name="dispatch",
    )(seg, dest_p.reshape(steps_p, 1, 2 * tm), dest_s.reshape(1, 1, 2 * n_s), h2p, h2s)


def _moe_kernel(seg_ref, wg_ref, wu_ref, wd_ref, x_hbm, y_hbm, wg16, wu16, wd16, xbuf, ybuf, xsem, ysem, *,
                n_blocks):
    e = pl.program_id(0)
    end = seg_ref[N_EXPERTS + e]
    start = jnp.where(e == 0, 0, seg_ref[N_EXPERTS + jnp.maximum(e - 1, 0)])
    nb = (end - start) // MOE_BLOCK

    def rows(j):
        return pl.ds(pl.multiple_of(start + j * MOE_BLOCK, MOE_BLOCK), MOE_BLOCK)

    def x_copy(j, slot):
        return pltpu.make_async_copy(x_hbm.at[rows(j), :], xbuf.at[slot], xsem.at[slot])

    def y_copy(j, slot):
        return pltpu.make_async_copy(ybuf.at[slot], y_hbm.at[rows(j), :], ysem.at[slot])

    @pl.when(nb > 0)
    def _():
        x_copy(0, 0).start()
        wg16[...] = wg_ref[0].astype(BF16)
        wu16[...] = wu_ref[0].astype(BF16)
        wd16[...] = wd_ref[0].astype(BF16)

        def block(j, c):
            slot = lax.rem(j, 2)

            @pl.when(j + 1 < nb)
            def _():
                x_copy(j + 1, 1 - slot).start()

            x_copy(j, slot).wait()

            @pl.when(j >= 2)
            def _():
                y_copy(j - 2, slot).wait()

            xb = xbuf[slot].astype(BF16)
            g = _dot(xb, wg16[...])
            u = _dot(xb, wu16[...])
            hmid = (_silu(g) * u).astype(BF16)
            ybuf[slot] = _dot(hmid, wd16[...])
            y_copy(j, slot).start()
            return c

        lax.fori_loop(0, nb, block, 0)

        @pl.when(nb >= 2)
        def _():
            y_copy(nb - 2, lax.rem(nb, 2)).wait()
        y_copy(nb - 1, lax.rem(nb - 1, 2)).wait()

    @pl.when(e == N_EXPERTS - 1)
    def _():
        ybuf[0] = jnp.zeros_like(ybuf[0])

        def fill(b, c):
            cp = pltpu.make_async_copy(ybuf.at[0], y_hbm.at[pl.ds(pl.multiple_of(b * MOE_BLOCK, MOE_BLOCK),
                                                                    MOE_BLOCK), :], ysem.at[0])
            cp.start()
            cp.wait()
            return c

        lax.fori_loop(end // MOE_BLOCK, n_blocks, fill, 0)


def _moe(seg, x_sorted, w_gate, w_up, w_down):
    n_blocks = x_sorted.shape[0] // MOE_BLOCK
    wmap = lambda e, sg: (e, 0, 0)
    grid_spec = pltpu.PrefetchScalarGridSpec(
        num_scalar_prefetch=1,
        grid=(N_EXPERTS,),
        in_specs=[pl.BlockSpec((1, D_MODEL, D_EXPERT), wmap),
                  pl.BlockSpec((1, D_MODEL, D_EXPERT), wmap),
                  pl.BlockSpec((1, D_EXPERT, D_MODEL), wmap),
                  pl.BlockSpec(memory_space=pl.ANY)],
        out_specs=pl.BlockSpec(memory_space=pl.ANY),
        scratch_shapes=[pltpu.VMEM((D_MODEL, D_EXPERT), BF16),
                        pltpu.VMEM((D_MODEL, D_EXPERT), BF16),
                        pltpu.VMEM((D_EXPERT, D_MODEL), BF16),
                        pltpu.VMEM((2, MOE_BLOCK, D_MODEL), F32),
                        pltpu.VMEM((2, MOE_BLOCK, D_MODEL), F32),
                        pltpu.SemaphoreType.DMA((2,)),
                        pltpu.SemaphoreType.DMA((2,))])
    return pl.pallas_call(
        functools.partial(_moe_kernel, n_blocks=n_blocks),
        grid_spec=grid_spec,
        out_shape=jax.ShapeDtypeStruct((n_blocks * MOE_BLOCK, D_MODEL), F32),
        compiler_params=_cparams(("arbitrary",)),
        name="moe",
    )(seg, w_gate, w_up, w_down, x_sorted)


def _combine_kernel(dcur_ref, dnext_ref, route_ref, x1_ref, mod_ref, g_ref, y_hbm, o_ref, ybuf, sems):
    i = pl.program_id(0)
    last = pl.num_programs(0) - 1
    tm = x1_ref.shape[0]

    def gather(dest_ref, buf):
        for t in range(tm):
            for slot in range(2):
                pltpu.make_async_copy(y_hbm.at[pl.ds(dest_ref[0, 0, 2 * t + slot], 1), :],
                                      ybuf.at[buf, slot, pl.ds(t, 1), :], sems.at[buf]).start(priority=slot)

    @pl.when(i == 0)
    def _():
        gather(dcur_ref, 0)

    for buf in range(2):
        @pl.when(i % 2 == buf)
        def _():
            @pl.when(i < last)
            def _():
                gather(dnext_ref, 1 - buf)

            for slot in range(2):
                pltpu.make_async_copy(y_hbm.at[pl.ds(0, tm), :], ybuf.at[buf, slot], sems.at[buf]).wait()
            route = route_ref[...]
            moe = ybuf[buf, 0] * route[:, 2:3] + ybuf[buf, 1] * route[:, 3:4]
            o_ref[...] = x1_ref[...] + _per_seq(_rms(moe, g_ref[...]), mod_ref[...], lambda y, m: y * m[:, 5:6])


def _combine(dest, route, y_sorted, x1, mod8, g_post_ffn, tm, seq_rows, mod_row0):
    n = x1.shape[0]
    steps = n // tm
    row = lambda i: (i, 0)
    dest3 = dest.reshape(steps, 1, 2 * tm)
    return pl.pallas_call(
        _combine_kernel,
        grid=(steps,),
        in_specs=[pl.BlockSpec((1, 1, 2 * tm), lambda i: (i, 0, 0), memory_space=pltpu.SMEM),
                  pl.BlockSpec((1, 1, 2 * tm), lambda i: (jnp.minimum(i + 1, steps - 1), 0, 0),
                               memory_space=pltpu.SMEM),
                  pl.BlockSpec((tm, LANES), row),
                  pl.BlockSpec((tm, D_MODEL), row),
                  _mod_spec(tm, seq_rows, mod_row0),
                  pl.BlockSpec((1, D_MODEL), lambda i: (0, 0)),
                  pl.BlockSpec(memory_space=pl.ANY)],
        out_specs=pl.BlockSpec((tm, D_MODEL), row),
        out_shape=jax.ShapeDtypeStruct((n, D_MODEL), F32),
        scratch_shapes=[pltpu.VMEM((2, 2, tm, D_MODEL), F32), pltpu.SemaphoreType.DMA((2,))],
        compiler_params=_cparams(("arbitrary",)),
        name="combine",
    )(dest3, dest3, route, x1, mod8, g_post_ffn, y_sorted)


def _segment_plan(counts_f):
    counts = counts_f[0, :N_EXPERTS].astype(jnp.int32)
    padded = (counts + MOE_BLOCK - 1) // MOE_BLOCK * MOE_BLOCK
    pad_end = jnp.cumsum(padded)
    seg = jnp.concatenate([counts, pad_end]).astype(jnp.int32)
    return seg, pad_end - padded


def _token_rows(route, pad_start):
    eid = route[:, 0:2].astype(jnp.int32)
    rank = route[:, 4:6].astype(jnp.int32)
    onehot = eid[:, :, None] == jnp.arange(N_EXPERTS, dtype=jnp.int32)[None, None, :]
    return rank + jnp.sum(jnp.where(onehot, pad_start[None, None, :], 0), axis=2)


def _layer(x_p, x_s, c_p, c_s, k_past, v_past, s0_s, conv_s, p):
    bp, tp, d = x_p.shape
    bs, ts, _ = x_s.shape
    n_p, n_s = bp * tp, bs * ts
    n_tok = n_p + n_s

    n_seq = bp + bs
    c_all = jnp.zeros((16, d), F32).at[:n_seq].set(jnp.concatenate([c_s, c_p], axis=0))
    mod = _ada(c_all, p['w_ada'], p['b_ada'])
    mod8 = jnp.pad(mod.reshape(16, 6, d), ((0, 0), (0, 2), (0, 0)))

    w_in = p['w_in']
    o_z, o_b, o_q = CONV_CH, CONV_CH + 512, CONV_CH + 512 + 2 * DN_HEADS
    wb = jnp.pad(w_in[:, o_b:o_q], ((0, 0), (0, LANES - 2 * DN_HEADS)))
    wb_hi = wb.astype(BF16)
    wb_lo = (wb - wb_hi.astype(F32)).astype(BF16)
    w_main = jnp.concatenate([w_in[:, :o_b].astype(BF16), w_in[:, o_q:].astype(BF16), wb_hi, wb_lo], axis=1)
    g_pre_mix = p['g_pre_mix'].reshape(1, d)

    conv_w8 = jnp.pad(p['conv_w'], ((0, 8 - CONV_W), (0, 0)))
    pad_g = lambda a: jnp.pad(a.reshape(1, DN_HEADS), ((0, 0), (DN_HEADS, LANES - 2 * DN_HEADS)))
    alog_row, dtb_row = pad_g(p['a_log']), pad_g(p['dt_bias'])
    onorm_a = p['onorm_a'].reshape(1, DN_DIM)
    onorm_b2 = jnp.tile(p['onorm_b'].reshape(1, SB_DIM), (1, 2))

    w_out16 = p['w_out'].astype(BF16)
    wr = jnp.pad(jnp.concatenate([p['w_router_group'], p['w_router_expert']], axis=1),
                 ((0, 0), (0, LANES - N_GROUPS - N_EXPERTS)))
    wr_hi = wr.astype(BF16)
    wr_lo = (wr - wr_hi.astype(F32)).astype(BF16)
    b_r = jnp.pad(jnp.concatenate([p['b_router_group'], p['b_router_expert']]).reshape(1, -1),
                  ((0, 0), (0, LANES - N_GROUPS - N_EXPERTS)))
    g_post_mix = p['g_post_mix'].reshape(1, d)
    g_pre_ffn = p['g_pre_ffn'].reshape(1, d)
    g_post_ffn = p['g_post_ffn'].reshape(1, d)

    def mixer(x, tm, tm_post, seq_rows, mod_row0, hist8, s0, k_old, v_old, chunk, n_chunks, bq, n_sub, cnt0):
        b, t, _ = x.shape
        x2d = x.reshape(b * t, d)
        a_in, z, bg, q16, kb, vb, k16, v16 = _proj(x2d, mod8, g_pre_mix, w_main, wb_hi, tm, seq_rows, mod_row0)
        r3 = lambda a: a.reshape(b, t, a.shape[-1])
        oa16, s_new = _delta(r3(a_in), r3(z), r3(bg), hist8, s0, conv_w8, alog_row, dtb_row, onorm_a,
                             chunk, n_chunks)
        k16, v16 = r3(k16), r3(v16)
        if k_old is not None:
            k16 = jnp.concatenate([k_old.reshape(b, -1, SB_WIDTH).astype(BF16), k16], axis=1)
            v16 = jnp.concatenate([v_old.reshape(b, -1, SB_WIDTH).astype(BF16), v16], axis=1)
        n_pad = (-k16.shape[1]) % KEY_TILE
        k16 = jnp.pad(k16, ((0, 0), (n_pad, 0), (0, 0)))
        v16 = jnp.pad(v16, ((0, 0), (n_pad, 0), (0, 0)))
        ob16 = _sb_attn(r3(q16), k16, v16, onorm_b2, bq, n_sub, n_pad)
        x1, h2, route, cnt = _post(oa16.reshape(b * t, 512), ob16.reshape(b * t, 512), x2d, mod8,
                                   g_post_mix, g_pre_ffn, w_out16, wr_hi, wr_lo, b_r, cnt0, tm_post, seq_rows,
                                   mod_row0)
        new_conv = r3(a_in)[:, t - (CONV_W - 1):, :]
        return (x1, h2, route, cnt, kb.reshape(b, t, SB_HEADS, SB_DIM), vb.reshape(b, t, SB_HEADS, SB_DIM),
                s_new, new_conv)

    zero_hist = jnp.zeros((bp, 8, CONV_CH), F32)
    zero_s = jnp.zeros((bp, DN_HEADS, DN_DIM, DN_DIM), F32)
    hist_s = jnp.pad(conv_s, ((0, 0), (8 - (CONV_W - 1), 0), (0, 0)))
    tm_p = min(256, tp)
    tm_post = tm_p * POST_TILE_FACTOR if n_p % (tm_p * POST_TILE_FACTOR) == 0 else tm_p
    nc_p = max(1, min(4, tp // DELTA_BLOCK))
    x1p, h2p, rp, cnt_p, kp, vp, sp, cp = mixer(x_p, tm_p, tm_post, tp, bs, zero_hist, zero_s, None, None,
                                                 min(DELTA_BLOCK, tp), nc_p, min(KEY_TILE // 2, tp), 2,
                                                 jnp.zeros((1, LANES), F32))
    x1s, h2s, rs, cnt, ks, vs, ss, cs = mixer(x_s, n_s, n_s, ts, 0, hist_s, s0_s, k_past, v_past,
                                               min(DELTA_BLOCK, ts), max(1, ts // DELTA_BLOCK),
                                               min(KEY_TILE, ts), 1, cnt_p)

    n_blocks = -(-2 * n_tok // MOE_BLOCK) + N_EXPERTS
    seg, pad_start = _segment_plan(cnt)
    dest_p = _token_rows(rp, pad_start)
    dest_s = _token_rows(rs, pad_start)
    x_sorted = _dispatch(seg, dest_p, dest_s, h2p, h2s, n_blocks, tm_p)
    y_sorted = _moe(seg, x_sorted, p['w_gate'], p['w_up'], p['w_down'])
    y_p = _combine(dest_p, rp, y_sorted, x1p, mod8, g_post_ffn, tm_p, tp, bs).reshape(bp, tp, d)
    y_s = _combine(dest_s, rs, y_sorted, x1s, mod8, g_post_ffn, n_s, ts, 0).reshape(bs, ts, d)
    return y_p, y_s, kp, vp, sp, cp, ks, vs, ss, cs


def kernel(x_prompt, x_sample, c_prompt, c_sample, cache_k, cache_v, state_delta, state_conv, w_ada, b_ada, g_pre_mix, g_post_mix, g_pre_ffn, g_post_ffn, w_in, conv_w, a_log, dt_bias, onorm_a, onorm_b, w_out, w_router_group, b_router_group, w_router_expert, b_router_expert, w_gate, w_up, w_down):
    depth = w_in.shape[0]
    y_p, y_s = x_prompt, x_sample
    outs = [[] for _ in range(8)]
    for l in range(depth):
        p = dict(w_ada=w_ada[l], b_ada=b_ada[l], g_pre_mix=g_pre_mix[l], g_post_mix=g_post_mix[l],
                 g_pre_ffn=g_pre_ffn[l], g_post_ffn=g_post_ffn[l], w_in=w_in[l], conv_w=conv_w[l],
                 a_log=a_log[l], dt_bias=dt_bias[l], onorm_a=onorm_a[l], onorm_b=onorm_b[l],
                 w_out=w_out[l], w_router_group=w_router_group[l], b_router_group=b_router_group[l],
                 w_router_expert=w_router_expert[l], b_router_expert=b_router_expert[l],
                 w_gate=w_gate[l], w_up=w_up[l], w_down=w_down[l])
        res = _layer(y_p, y_s, c_prompt, c_sample, cache_k[l], cache_v[l], state_delta[l], state_conv[l], p)
        y_p, y_s = res[0], res[1]
        for lst, r in zip(outs, res[2:]):
            lst.append(r)
    return (y_p, y_s) + tuple(jnp.stack(o) for o in outs)
```

```python
import functools
import math

import jax
import jax.numpy as jnp
from jax import lax
from jax.experimental import pallas as pl
from jax.experimental.pallas import tpu as pltpu

F32 = jnp.float32
BF16 = jnp.bfloat16

D_MODEL = 1024
DN_HEADS = 4
DN_DIM = 128
CONV_W = 4
CONV_CH = DN_HEADS * 3 * DN_DIM
DELTA_BLOCK = 64
SB_HEADS = 8
SB_DIM = 64
SB_WIDTH = SB_HEADS * SB_DIM
N_GROUPS = 4
EXPERTS_PER_GROUP = 8
N_EXPERTS = N_GROUPS * EXPERTS_PER_GROUP
D_EXPERT = D_MODEL // 2
MOE_BLOCK = 256
EPS = 1e-6

LANES = 128
KEY_TILE = 128
ATTN_WINDOW = 3 * KEY_TILE
WINDOW_BLOCKS = 4
EXP_ZERO_BELOW = -104.0
VMEM_LIMIT = 56 * 1024 * 1024
POST_TILE_FACTOR = 2


def _cparams(sem):
    return pltpu.CompilerParams(dimension_semantics=sem, vmem_limit_bytes=VMEM_LIMIT)


def _split(a):
    hi = a.astype(BF16)
    lo = (a - hi.astype(F32)).astype(BF16)
    return hi, lo


def _dot(a, b, dims=(((1,), (0,)), ((), ()))):
    return lax.dot_general(a, b, dims, preferred_element_type=F32)


def _dot_small_int_lhs(a, b):
    a16 = a.astype(BF16)
    return _dot(jnp.concatenate([a16, a16], axis=1), jnp.concatenate(_split(b), axis=0))


_NT = (((1,), (1,)), ((), ()))
_TN = (((0,), (0,)), ((), ()))


def _silu(x):
    return x * jax.nn.sigmoid(x)


SOFTPLUS_LINEAR_ABOVE = 80.0


def _softplus(x):
    return jnp.where(x > SOFTPLUS_LINEAR_ABOVE, x, jnp.log(1.0 + jnp.exp(jnp.minimum(x, SOFTPLUS_LINEAR_ABOVE))))


def _ada_kernel(c_ref, w_ref, b_ref, o_ref):
    s = _silu(c_ref[...]).astype(BF16)
    o_ref[...] = _dot(s, w_ref[...].astype(BF16)) + b_ref[...]


def _ada(c_all, w_ada, b_ada):
    rows = c_all.shape[0]
    n = w_ada.shape[1]
    tn = 1024
    return pl.pallas_call(
        _ada_kernel,
        grid=(n // tn,),
        in_specs=[pl.BlockSpec((rows, D_MODEL), lambda j: (0, 0)),
                  pl.BlockSpec((D_MODEL, tn), lambda j: (0, j)),
                  pl.BlockSpec((1, tn), lambda j: (0, j))],
        out_specs=pl.BlockSpec((rows, tn), lambda j: (0, j)),
        out_shape=jax.ShapeDtypeStruct((rows, n), F32),
        compiler_params=_cparams(("arbitrary",)),
        name="ada",
    )(c_all, w_ada, b_ada.reshape(1, n))


def _rms(x, gain):
    return x * lax.rsqrt(jnp.mean(x * x, axis=-1, keepdims=True) + EPS) * gain


def _per_seq(y, mod, fn):
    n_seq = mod.shape[0]
    ys = y.reshape(n_seq, y.shape[0] // n_seq, y.shape[1])
    return fn(ys, mod).reshape(y.shape)


def _mod_spec(tm, seq_rows, mod_row0):
    if tm >= seq_rows:
        n_seq = tm // seq_rows
        assert tm % seq_rows == 0 and mod_row0 % n_seq == 0
        return pl.BlockSpec((n_seq, 8, D_MODEL), lambda i: (mod_row0 // n_seq + i, 0, 0))
    assert seq_rows % tm == 0
    return pl.BlockSpec((1, 8, D_MODEL), lambda i: (mod_row0 + (i * tm) // seq_rows, 0, 0))


def _proj_kernel(x_ref, mod_ref, g_ref, wm_ref, wbh_ref,
                 a_ref, z_ref, bg_ref, q_ref, k_ref, v_ref, k16_ref, v16_ref):
    h = _per_seq(_rms(x_ref[...], g_ref[...]), mod_ref[...], lambda y, m: y * (1.0 + m[:, 1:2]) + m[:, 0:1])
    hh, hl = _split(h)
    p = _dot(hh, wm_ref[...])
    a_ref[...] = p[:, 0:CONV_CH]
    z_ref[...] = p[:, CONV_CH:CONV_CH + 512]
    o = CONV_CH + 512
    q_ref[...] = (p[:, o:o + 512] * (SB_DIM ** -0.5)).astype(BF16)
    k = p[:, o + 512:o + 1024]
    v = p[:, o + 1024:o + 1536]
    k_ref[...] = k
    v_ref[...] = v
    k16_ref[...] = k.astype(BF16)
    v16_ref[...] = v.astype(BF16)
    bg_ref[...] = p[:, o + 1536:o + 1536 + LANES] + (_dot(hl, wbh_ref[...]) + p[:, o + 1536 + LANES:])


def _proj(x2d, mod8, g_pre, w_main, wb_hi, tm, seq_rows, mod_row0):
    n = x2d.shape[0]
    nm = w_main.shape[1]
    row = lambda i: (i, 0)
    const = lambda i: (0, 0)
    outs = [(CONV_CH, F32), (512, F32), (LANES, F32), (512, BF16), (512, F32), (512, F32),
            (512, BF16), (512, BF16)]
    return pl.pallas_call(
        _proj_kernel,
        grid=(n // tm,),
        in_specs=[pl.BlockSpec((tm, D_MODEL), row),
                  _mod_spec(tm, seq_rows, mod_row0),
                  pl.BlockSpec((1, D_MODEL), const),
                  pl.BlockSpec((D_MODEL, nm), const),
                  pl.BlockSpec((D_MODEL, LANES), const)],
        out_specs=[pl.BlockSpec((tm, w), row) for w, _ in outs],
        out_shape=[jax.ShapeDtypeStruct((n, w), dt) for w, dt in outs],
        compiler_params=_cparams(("arbitrary",)),
        name="proj",
    )(x2d, mod8, g_pre, w_main, wb_hi)


def _delta_kernel(a_ref, z_ref, bg_ref, hist0_ref, s0_ref, cw_ref, alog_ref, dtb_ref, on_ref,
                  o_ref, sfin_ref, hist_sc, s_sc, *, chunk, n_chunks):
    t_idx = pl.program_id(1)
    tt = chunk * n_chunks

    @pl.when(t_idx == 0)
    def _():
        hist_sc[...] = hist0_ref[0]
        s_sc[...] = s0_ref[0]

    x = a_ref[0]
    xx = jnp.concatenate([hist_sc[...], x], axis=0)
    cw = cw_ref[...]
    conv = x * cw[CONV_W - 1:CONV_W]
    for s in range(1, CONV_W):
        conv = conv + pltpu.roll(xx, s, 0)[8:] * cw[CONV_W - 1 - s:CONV_W - s]
    conv = _silu(conv)
    hist_sc[...] = x[tt - 8:tt]

    bg = bg_ref[0]
    lane = lax.broadcasted_iota(jnp.int32, (1, LANES), 1)
    g_lane = (lane >= DN_HEADS) & (lane < 2 * DN_HEADS)
    neg_a = jnp.where(g_lane, -jnp.exp(alog_ref[...]), 0.0)
    beta_all = jax.nn.sigmoid(bg)
    g_all = neg_a * _softplus(bg + dtb_ref[...])

    ri = lax.broadcasted_iota(jnp.int32, (chunk, chunk), 0)
    ci = lax.broadcasted_iota(jnp.int32, (chunk, chunk), 1)
    incl = ri >= ci
    strict = ri > ci
    tri = incl.astype(F32)
    eye = (ri == ci).astype(F32)
    ones_cc = jnp.ones((chunk, chunk), F32)
    onorm = on_ref[...]

    chunks = range(n_chunks)
    units = [(c, h) for c in chunks for h in range(DN_HEADS)]
    rows = lambda c: slice(c * chunk, (c + 1) * chunk)

    gc_all = [_dot_small_int_lhs(tri, g_all[rows(c)]) for c in chunks]
    diag = [jnp.concatenate([eye * gc_all[c][:, DN_HEADS + h:DN_HEADS + h + 1] for h in range(DN_HEADS)], axis=1)
            for c in chunks]
    gc_rows = [_dot_small_int_lhs(ones_cc, diag[c]) for c in chunks]

    q, k, v, beta, gcol, glast, eg, decay, kbeta = {}, {}, {}, {}, {}, {}, {}, {}, {}
    for (c, h) in units:
        lo = h * DN_DIM
        qq = conv[rows(c), lo:lo + DN_DIM]
        kk = conv[rows(c), 512 + lo:512 + lo + DN_DIM]
        u_ = (c, h)
        q[u_] = qq * lax.rsqrt(jnp.sum(qq * qq, axis=-1, keepdims=True) + EPS) * (DN_DIM ** -0.5)
        k[u_] = kk * lax.rsqrt(jnp.sum(kk * kk, axis=-1, keepdims=True) + EPS)
        v[u_] = conv[rows(c), 1024 + lo:1024 + lo + DN_DIM]
        beta[u_] = beta_all[rows(c), h:h + 1]
        gcol[u_] = gc_all[c][:, DN_HEADS + h:DN_HEADS + h + 1]
        glast[u_] = gcol[u_][chunk - 1:chunk, :]
        eg[u_] = jnp.exp(gcol[u_])
        grow = gc_rows[c][:, h * chunk:(h + 1) * chunk]
        decay[u_] = jnp.where(incl, jnp.exp(jnp.minimum(gcol[u_] - grow, 0.0)), 0.0)
        kbeta[u_] = k[u_] * beta[u_]

    k16 = {u_: k[u_].astype(BF16) for u_ in units}
    kq = {u_: _dot(jnp.concatenate([kbeta[u_], q[u_]], axis=0).astype(BF16), k16[u_], _NT) for u_ in units}
    lower = {u_: jnp.where(strict, kq[u_][:chunk] * decay[u_], 0.0) for u_ in units}
    attn16 = {u_: jnp.where(incl, kq[u_][chunk:] * decay[u_], 0.0).astype(BF16) for u_ in units}

    sol = {u_: jnp.concatenate([v[u_] * beta[u_], kbeta[u_] * eg[u_]], axis=1) for u_ in units}
    lp = lower
    p = 1
    while p < chunk:
        lsp = {u_: _split(lp[u_]) for u_ in units}
        ssp = {u_: _split(sol[u_]) for u_ in units}
        lcat = {u_: jnp.concatenate([lsp[u_][0], lsp[u_][1], lsp[u_][0]], axis=1) for u_ in units}
        upd = {u_: _dot(lcat[u_], jnp.concatenate([ssp[u_][0], ssp[u_][0], ssp[u_][1]], axis=0)) for u_ in units}
        sol = {u_: (sol[u_] - upd[u_]) if p == 1 else (sol[u_] + upd[u_]) for u_ in units}
        if 2 * p < chunk:
            lp = {u_: _dot(lcat[u_], jnp.concatenate([lsp[u_][0], lsp[u_][0], lsp[u_][1]], axis=0))
                  for u_ in units}
        p *= 2

    usol = {u_: sol[u_][:, :DN_DIM] for u_ in units}
    wq16 = {u_: jnp.concatenate([sol[u_][:, DN_DIM:], q[u_] * eg[u_]], axis=0).astype(BF16) for u_ in units}
    kd16 = {u_: (k[u_] * jnp.exp(glast[u_] - gcol[u_])).astype(BF16) for u_ in units}

    for c in chunks:
        hs = range(DN_HEADS)
        s_old = [s_sc[h] for h in hs]
        ws = [_dot(wq16[(c, h)], s_old[h].astype(BF16)) for h in hs]
        v_new = [(usol[(c, h)] - ws[h][:chunk]).astype(BF16) for h in hs]
        o_in = [_dot(attn16[(c, h)], v_new[h]) for h in hs]
        ds = [_dot(kd16[(c, h)], v_new[h], _TN) for h in hs]
        for h in hs:
            lo = h * DN_DIM
            s_sc[h] = s_old[h] * jnp.exp(glast[(c, h)]) + ds[h]
            o = ws[h][chunk:] + o_in[h]
            zg = z_ref[0, rows(c), lo:lo + DN_DIM]
            o_ref[0, rows(c), lo:lo + DN_DIM] = (_rms(o, onorm) * _silu(zg)).astype(BF16)

    @pl.when(t_idx == pl.num_programs(1) - 1)
    def _():
        sfin_ref[0] = s_sc[...]


def _delta(a_in, z, bg, hist8, s0, conv_w8, alog_row, dtb_row, onorm_a, chunk, n_chunks):
    b, t, _ = a_in.shape
    tt = chunk * n_chunks
    tile = lambda bi, ti: (bi, ti, 0)
    per_b3 = lambda bi, ti: (bi, 0, 0)
    per_b4 = lambda bi, ti: (bi, 0, 0, 0)
    const = lambda bi, ti: (0, 0)
    kern = functools.partial(_delta_kernel, chunk=chunk, n_chunks=n_chunks)
    return pl.pallas_call(
        kern,
        grid=(b, t // tt),
        in_specs=[pl.BlockSpec((1, tt, CONV_CH), tile),
                  pl.BlockSpec((1, tt, 512), tile),
                  pl.BlockSpec((1, tt, LANES), tile),
                  pl.BlockSpec((1, 8, CONV_CH), per_b3),
                  pl.BlockSpec((1, DN_HEADS, DN_DIM, DN_DIM), per_b4),
                  pl.BlockSpec((8, CONV_CH), const),
                  pl.BlockSpec((1, LANES), const),
                  pl.BlockSpec((1, LANES), const),
                  pl.BlockSpec((1, DN_DIM), const)],
        out_specs=[pl.BlockSpec((1, tt, 512), tile),
                   pl.BlockSpec((1, DN_HEADS, DN_DIM, DN_DIM), per_b4)],
        out_shape=[jax.ShapeDtypeStruct((b, t, 512), BF16),
                   jax.ShapeDtypeStruct((b, DN_HEADS, DN_DIM, DN_DIM), F32)],
        scratch_shapes=[pltpu.VMEM((8, CONV_CH), F32),
                        pltpu.VMEM((DN_HEADS, DN_DIM, DN_DIM), F32)],
        compiler_params=_cparams(("arbitrary", "arbitrary")),
        name="delta",
    )(a_in, z, bg, hist8, s0, conv_w8, alog_row, dtb_row, onorm_a)


def _sb_kernel(q_ref, *refs, bq, n_sub, n_pad, q_off):
    kwin = refs[:WINDOW_BLOCKS]
    vwin = refs[WINDOW_BLOCKS:2 * WINDOW_BLOCKS]
    k_hbm, v_hbm, on_ref, o_ref, kbuf, vbuf, qsel, acc, carry, sem = refs[2 * WINDOW_BLOCKS:]
    b = pl.program_id(0)
    i = pl.program_id(1)
    g = bq * n_sub
    qend_step = q_off + (i + 1) * g
    win_start = qend_step - WINDOW_BLOCKS * KEY_TILE
    n_pairs = SB_HEADS // 2
    heads = range(SB_HEADS)

    half_lane = lax.broadcasted_iota(jnp.int32, (bq, LANES), 1) < SB_DIM
    rj = lax.broadcasted_iota(jnp.int32, (2 * KEY_TILE, 2 * KEY_TILE), 0) % KEY_TILE
    cj = lax.broadcasted_iota(jnp.int32, (2 * KEY_TILE, 2 * KEY_TILE), 1)
    suffix2 = ((rj > cj) | (cj >= KEY_TILE)).astype(BF16)

    def window(blocks, off, p):
        parts = []
        for blk in range(WINDOW_BLOCKS):
            lo, hi = max(off, blk * KEY_TILE), min(off + ATTN_WINDOW, (blk + 1) * KEY_TILE)
            if lo < hi:
                ref = blocks[WINDOW_BLOCKS - 1 - blk]
                parts.append(ref[0, lo - blk * KEY_TILE:hi - blk * KEY_TILE, p * LANES:(p + 1) * LANES])
        return jnp.concatenate(parts, axis=0)

    n_t = ATTN_WINDOW // KEY_TILE
    row = lax.broadcasted_iota(jnp.int32, (bq, ATTN_WINDOW), 0)
    col = lax.broadcasted_iota(jnp.int32, (bq, ATTN_WINDOW), 1)
    causal = col < row + (ATTN_WINDOW - bq)
    subs = range(n_sub)
    offs = [WINDOW_BLOCKS * KEY_TILE - g + (s + 1) * bq - ATTN_WINDOW for s in subs]
    mask = [causal & (col >= n_pad - (win_start + offs[s])) for s in subs]
    zs, sp, cs_all = {}, {}, {}
    for s in subs:
        for p in range(n_pairs):
            qf = q_ref[0, s * bq:(s + 1) * bq, p * LANES:(p + 1) * LANES].astype(F32)
            qq = jnp.concatenate([jnp.where(half_lane, qf, 0.0), jnp.where(half_lane, 0.0, qf)], axis=0)
            zz = _dot(qq.astype(BF16), window(kwin, offs[s], p), _NT)
            zs[(s, 2 * p)], zs[(s, 2 * p + 1)] = zz[:bq], zz[bq:]
    for s in subs:
        pieces = []
        for h in heads:
            sp[(s, h)] = _softplus(zs[(s, h)])
            hi, lo = _split(jnp.where(mask[s], -sp[(s, h)], 0.0))
            for j in range(n_t):
                c0 = ATTN_WINDOW - (j + 1) * KEY_TILE
                pieces.append(jnp.concatenate([hi[:, c0:c0 + KEY_TILE], lo[:, c0:c0 + KEY_TILE]], axis=1))
        cs_all[s] = _dot(jnp.concatenate(pieces, axis=0), suffix2)
    done = []
    for s in subs:
        worst = jnp.full((bq, KEY_TILE), -jnp.inf, F32)
        a = {}
        for h in heads:
            run = None
            cols = []
            for j in range(n_t):
                r0 = (h * n_t + j) * bq
                cs = cs_all[s][r0:r0 + bq]
                cols.append(cs[:, :KEY_TILE] if run is None else cs[:, :KEY_TILE] + run)
                run = cs[:, KEY_TILE:] if run is None else run + cs[:, KEY_TILE:]
            within = jnp.concatenate(cols[::-1], axis=1)
            a[h] = jnp.where(mask[s], jnp.exp((zs[(s, h)] - sp[(s, h)]) + within), 0.0).astype(BF16)
            carry[s, h] = run
            worst = jnp.maximum(worst, run)
        for p in range(n_pairs):
            pv = _dot(jnp.concatenate([a[2 * p], a[2 * p + 1]], axis=0), window(vwin, offs[s], p))
            acc[s, p] = jnp.where(half_lane, pv[:bq], pv[bq:])
        done.append((jnp.max(worst) < EXP_ZERO_BELOW).astype(jnp.int32))

    rj1 = lax.broadcasted_iota(jnp.int32, (KEY_TILE, 2 * KEY_TILE), 0)
    cj1 = lax.broadcasted_iota(jnp.int32, (KEY_TILE, 2 * KEY_TILE), 1)
    suffix1 = ((rj1 > cj1) | (cj1 >= KEY_TILE)).astype(BF16)
    col1 = lax.broadcasted_iota(jnp.int32, (bq, KEY_TILE), 1)
    for s in subs:
        swept_from = win_start + offs[s]

        @pl.when((done[s] == 0) & (swept_from > n_pad))
        def _():
            for p in range(n_pairs):
                qf = q_ref[0, s * bq:(s + 1) * bq, p * LANES:(p + 1) * LANES].astype(F32)
                qsel[2 * p] = jnp.where(half_lane, qf, 0.0).astype(BF16)
                qsel[2 * p + 1] = jnp.where(half_lane, 0.0, qf).astype(BF16)

            def body(state):
                upper, _ = state
                start = jnp.maximum(upper - KEY_TILE, 0)
                copies = []
                for p in range(n_pairs):
                    for src, dst in ((k_hbm, kbuf), (v_hbm, vbuf)):
                        cp = pltpu.make_async_copy(
                            src.at[b, pl.ds(pl.multiple_of(start, 16), KEY_TILE), pl.ds(p * LANES, LANES)],
                            dst.at[p], sem)
                        cp.start()
                        copies.append(cp)
                for cp in copies:
                    cp.wait()
                kpos = start + col1
                m1 = (kpos < upper) & (kpos >= n_pad)
                zz = [_dot(qsel[h], kbuf[h // 2], _NT) for h in heads]
                spp = [_softplus(zz[h]) for h in heads]
                pcs = []
                for h in heads:
                    pcs.extend(_split(jnp.where(m1, -spp[h], 0.0)))
                cs1 = _dot(jnp.concatenate(pcs, axis=0), suffix1)
                worst = jnp.full((bq, KEY_TILE), -jnp.inf, F32)
                aa = []
                for h in heads:
                    c = cs1[2 * h * bq:(2 * h + 1) * bq] + cs1[(2 * h + 1) * bq:(2 * h + 2) * bq]
                    c_old = carry[s, h]
                    aa.append(jnp.where(m1, jnp.exp((zz[h] - spp[h]) + c[:, :KEY_TILE] + c_old), 0.0).astype(BF16))
                    c_new = c_old + c[:, KEY_TILE:]
                    carry[s, h] = c_new
                    worst = jnp.maximum(worst, c_new)
                for p in range(n_pairs):
                    pv0 = _dot(aa[2 * p], vbuf[p])
                    pv1 = _dot(aa[2 * p + 1], vbuf[p])
                    acc[s, p] = acc[s, p] + jnp.where(half_lane, pv0, pv1)
                return start, (jnp.max(worst) < EXP_ZERO_BELOW).astype(jnp.int32)

            lax.while_loop(lambda st: (st[0] > n_pad) & (st[1] == 0), body, (swept_from, jnp.int32(0)))

    onb = on_ref[...]
    for s in subs:
        for p in range(n_pairs):
            o = acc[s, p]
            sq = o * o
            s_lo = jnp.sum(jnp.where(half_lane, sq, 0.0), axis=-1, keepdims=True)
            s_hi = jnp.sum(jnp.where(half_lane, 0.0, sq), axis=-1, keepdims=True)
            ms = jnp.where(half_lane, s_lo, s_hi) * (1.0 / SB_DIM)
            o_ref[0, s * bq:(s + 1) * bq, p * LANES:(p + 1) * LANES] = (o * lax.rsqrt(ms + EPS) * onb).astype(BF16)


def _sb_attn(q16, k16p, v16p, onorm_b2, bq, n_sub, n_pad):
    b, tq, _ = q16.shape
    tkp = k16p.shape[1]
    q_off = tkp - tq
    g = bq * n_sub
    assert tq % g == 0 and g <= KEY_TILE and bq % 16 == 0
    assert all((q_off + (i + 1) * g) % KEY_TILE == 0 for i in range(tq // g))

    def kmap(back):
        def f(bi, i):
            last = (q_off + (i + 1) * g) // KEY_TILE - 1
            return (bi, jnp.maximum(last - back, 0), 0)
        return f

    qmap = lambda bi, i: (bi, i, 0)
    kern = functools.partial(_sb_kernel, bq=bq, n_sub=n_sub, n_pad=n_pad, q_off=q_off)
    kspec = [pl.BlockSpec((1, KEY_TILE, SB_WIDTH), kmap(back)) for back in range(WINDOW_BLOCKS)]
    return pl.pallas_call(
        kern,
        grid=(b, tq // g),
        in_specs=[pl.BlockSpec((1, g, SB_WIDTH), qmap)] + kspec + kspec
                 + [pl.BlockSpec(memory_space=pl.ANY), pl.BlockSpec(memory_space=pl.ANY),
                    pl.BlockSpec((1, LANES), lambda bi, i: (0, 0))],
        out_specs=pl.BlockSpec((1, g, SB_WIDTH), qmap),
        out_shape=jax.ShapeDtypeStruct((b, tq, SB_WIDTH), BF16),
        scratch_shapes=[pltpu.VMEM((SB_HEADS // 2, KEY_TILE, LANES), BF16),
                        pltpu.VMEM((SB_HEADS // 2, KEY_TILE, LANES), BF16),
                        pltpu.VMEM((SB_HEADS, bq, LANES), BF16),
                        pltpu.VMEM((n_sub, SB_HEADS // 2, bq, LANES), F32),
                        pltpu.VMEM((n_sub, SB_HEADS, bq, KEY_TILE), F32),
                        pltpu.SemaphoreType.DMA(())],
        compiler_params=_cparams(("arbitrary", "arbitrary")),
        name="sb_attn",
    )(q16, *([k16p] * WINDOW_BLOCKS), *([v16p] * WINDOW_BLOCKS), k16p, v16p, onorm_b2)


def _post_kernel(oa_ref, ob_ref, x_ref, mod_ref, gpm_ref, gpf_ref, wo_ref, wrh_ref, wrl_ref, br_ref, cnt0_ref,
                 x1_ref, h2_ref, route_ref, cnt_ref):
    @pl.when(pl.program_id(0) == 0)
    def _():
        cnt_ref[...] = cnt0_ref[...]

    mod = mod_ref[...]
    mix = _dot(oa_ref[...], wo_ref[0:512, :]) + _dot(ob_ref[...], wo_ref[512:1024, :])
    x1 = x_ref[...] + _per_seq(_rms(mix, gpm_ref[...]), mod, lambda y, m: y * m[:, 2:3])
    x1_ref[...] = x1
    h2 = _per_seq(_rms(x1, gpf_ref[...]), mod, lambda y, m: y * (1.0 + m[:, 4:5]) + m[:, 3:4])
    h2_ref[...] = h2
    hh, hl = _split(h2)
    wrh = wrh_ref[...]
    logits = _dot(hh, wrh) + (_dot(hl, wrh) + _dot(hh, wrl_ref[...])) + br_ref[...]
    lane = lax.broadcasted_iota(jnp.int32, logits.shape, 1).astype(F32)
    neg = -jnp.inf
    nl = float(LANES)
    lg = jnp.where(lane < N_GROUPS, logits, neg)
    gmax = jnp.max(lg, axis=-1, keepdims=True)
    grp = jnp.min(jnp.where(lg == gmax, lane, nl), axis=-1, keepdims=True)
    p_grp = 1.0 / jnp.sum(jnp.exp(lg - gmax), axis=-1, keepdims=True)
    first = N_GROUPS + grp * EXPERTS_PER_GROUP
    le = jnp.where((lane >= first) & (lane < first + EXPERTS_PER_GROUP), logits, neg)
    emax = jnp.max(le, axis=-1, keepdims=True)
    i1 = jnp.min(jnp.where(le == emax, lane, nl), axis=-1, keepdims=True)
    esum = jnp.sum(jnp.exp(le - emax), axis=-1, keepdims=True)
    le2 = jnp.where(lane == i1, neg, le)
    e2max = jnp.max(le2, axis=-1, keepdims=True)
    i2 = jnp.min(jnp.where(le2 == e2max, lane, nl), axis=-1, keepdims=True)
    p1 = 1.0 / esum
    p2 = jnp.exp(e2max - emax) / esum
    w1 = p_grp * p1 / (p1 + p2)
    w2 = p_grp * p2 / (p1 + p2)
    e1 = i1 - N_GROUPS
    e2 = i2 - N_GROUPS
    hot1 = (lane == e1).astype(F32)
    hot2 = (lane == e2).astype(F32)
    both = hot1 + hot2
    tm = logits.shape[0]
    ti = lax.broadcasted_iota(jnp.int32, (tm, tm), 0)
    tj = lax.broadcasted_iota(jnp.int32, (tm, tm), 1)
    earlier = _dot((ti > tj).astype(BF16), both.astype(BF16)) + cnt_ref[...]
    rank1 = jnp.sum(hot1 * earlier, axis=-1, keepdims=True)
    rank2 = jnp.sum(hot2 * (earlier + hot1), axis=-1, keepdims=True)
    cnt_ref[...] = cnt_ref[...] + jnp.sum(both, axis=0, keepdims=True)
    out = jnp.where(lane == 0.0, e1, 0.0)
    out = jnp.where(lane == 1.0, e2, out)
    out = jnp.where(lane == 2.0, w1, out)
    out = jnp.where(lane == 3.0, w2, out)
    out = jnp.where(lane == 4.0, rank1, out)
    out = jnp.where(lane == 5.0, rank2, out)
    route_ref[...] = out


def _post(oa16, ob16, x2d, mod8, g_post_mix, g_pre_ffn, w_out16, wr_hi, wr_lo, b_r, cnt0, tm, seq_rows,
          mod_row0):
    n = x2d.shape[0]
    row = lambda i: (i, 0)
    const = lambda i: (0, 0)
    return pl.pallas_call(
        _post_kernel,
        grid=(n // tm,),
        in_specs=[pl.BlockSpec((tm, 512), row),
                  pl.BlockSpec((tm, 512), row),
                  pl.BlockSpec((tm, D_MODEL), row),
                  _mod_spec(tm, seq_rows, mod_row0),
                  pl.BlockSpec((1, D_MODEL), const),
                  pl.BlockSpec((1, D_MODEL), const),
                  pl.BlockSpec((D_MODEL, D_MODEL), const),
                  pl.BlockSpec((D_MODEL, LANES), const),
                  pl.BlockSpec((D_MODEL, LANES), const),
                  pl.BlockSpec((1, LANES), const),
                  pl.BlockSpec((1, LANES), const)],
        out_specs=[pl.BlockSpec((tm, D_MODEL), row),
                   pl.BlockSpec((tm, D_MODEL), row),
                   pl.BlockSpec((tm, LANES), row),
                   pl.BlockSpec((1, LANES), const)],
        out_shape=[jax.ShapeDtypeStruct((n, D_MODEL), F32),
                   jax.ShapeDtypeStruct((n, D_MODEL), F32),
                   jax.ShapeDtypeStruct((n, LANES), F32),
                   jax.ShapeDtypeStruct((1, LANES), F32)],
        compiler_params=_cparams(("arbitrary",)),
        name="post",
    )(oa16, ob16, x2d, mod8, g_post_mix, g_pre_ffn, w_out16, wr_hi, wr_lo, b_r, cnt0)


def _dispatch_kernel(seg_ref, dp_ref, ds_ref, hp_ref, hs_ref, xs_hbm, zbuf, stage, sem, stage_sems, *,
                     n_blocks, n_prompt_steps):
    i = pl.program_id(0)
    last_step = pl.num_programs(0) - 1

    @pl.when(i == 0)
    def _():
        zbuf[...] = jnp.zeros_like(zbuf)

        def zero_block(row0):
            return pltpu.make_async_copy(zbuf, xs_hbm.at[pl.ds(pl.multiple_of(row0, MOE_BLOCK), MOE_BLOCK), :], sem)

        for e in range(N_EXPERTS):
            @pl.when(seg_ref[e] > 0)
            def _():
                zero_block(seg_ref[N_EXPERTS + e] - MOE_BLOCK).start()
        for e in range(N_EXPERTS):
            @pl.when(seg_ref[e] > 0)
            def _():
                zero_block(seg_ref[N_EXPERTS + e] - MOE_BLOCK).wait()

        used = seg_ref[2 * N_EXPERTS - 1] // MOE_BLOCK

        def fill(b, c):
            cp = zero_block(b * MOE_BLOCK)
            cp.start()
            cp.wait()
            return c

        lax.fori_loop(used, n_blocks, fill, 0)

    def scatter(h_ref, dest_ref, sem_):
        for t in range(h_ref.shape[0]):
            for slot in range(2):
                pltpu.make_async_copy(h_ref.at[pl.ds(t, 1), :],
                                      xs_hbm.at[pl.ds(dest_ref[0, 0, 2 * t + slot], 1), :],
                                      sem_).start(priority=slot)

    def drain(h_ref, sem_):
        for slot in range(2):
            pltpu.make_async_copy(h_ref, xs_hbm.at[pl.ds(0, h_ref.shape[0]), :], sem_).wait()

    for buf in range(2):
        @pl.when((i < last_step) & (i % 2 == buf))
        def _():
            @pl.when(i >= 2)
            def _():
                drain(stage.at[buf], stage_sems.at[buf])
            stage[buf] = hp_ref[...]
            scatter(stage.at[buf], dp_ref, stage_sems.at[buf])

    @pl.when(i == last_step)
    def _():
        scatter(hs_ref, ds_ref, sem)
        drain(hs_ref, sem)
        for buf in range(min(2, n_prompt_steps)):
            drain(stage.at[buf], stage_sems.at[buf])


def _dispatch(seg, dest_p, dest_s, h2p, h2s, n_blocks, tm):
    n_p, n_s = h2p.shape[0], h2s.shape[0]
    steps_p = n_p // tm
    pmap3 = lambda i, sg: (jnp.minimum(i, steps_p - 1), 0, 0)
    pmap2 = lambda i, sg: (jnp.minimum(i, steps_p - 1), 0)
    grid_spec = pltpu.PrefetchScalarGridSpec(
        num_scalar_prefetch=1,
        grid=(steps_p + 1,),
        in_specs=[pl.BlockSpec((1, 1, 2 * tm), pmap3, memory_space=pltpu.SMEM),
                  pl.BlockSpec((1, 1, 2 * n_s), lambda i, sg: (0, 0, 0), memory_space=pltpu.SMEM),
                  pl.BlockSpec((tm, D_MODEL), pmap2),
                  pl.BlockSpec((n_s, D_MODEL), lambda i, sg: (0, 0))],
        out_specs=pl.BlockSpec(memory_space=pl.ANY),
        scratch_shapes=[pltpu.VMEM((MOE_BLOCK, D_MODEL), F32), pltpu.VMEM((2, tm, D_MODEL), F32),
                        pltpu.SemaphoreType.DMA(()), pltpu.SemaphoreType.DMA((2,))])
    return pl.pallas_call(
        functools.partial(_dispatch_kernel, n_blocks=n_blocks, n_prompt_steps=steps_p),
        grid_spec=grid_spec,
        out_shape=jax.ShapeDtypeStruct((n_blocks * MOE_BLOCK, D_MODEL), F32),
        compiler_params=_cparams(("arbitrary",)),
        name="dispatch",
    )(seg, dest_p.reshape(steps_p, 1, 2 * tm), dest_s.reshape(1, 1, 2 * n_s), h2p, h2s)


def _moe_kernel(seg_ref, wg_ref, wu_ref, wd_ref, x_hbm, y_hbm, wg16, wu16, wd16, xbuf, ybuf, xsem, ysem, *,
                n_blocks):
    e = pl.program_id(0)
    end = seg_ref[N_EXPERTS + e]
    start = jnp.where(e == 0, 0, seg_ref[N_EXPERTS + jnp.maximum(e - 1, 0)])
    nb = (end - start) // MOE_BLOCK

    def rows(j):
        return pl.ds(pl.multiple_of(start + j * MOE_BLOCK, MOE_BLOCK), MOE_BLOCK)

    def x_copy(j, slot):
        return pltpu.make_async_copy(x_hbm.at[rows(j), :], xbuf.at[slot], xsem.at[slot])

    def y_copy(j, slot):
        return pltpu.make_async_copy(ybuf.at[slot], y_hbm.at[rows(j), :], ysem.at[slot])

    @pl.when(nb > 0)
    def _():
        x_copy(0, 0).start()
        wg16[...] = wg_ref[0].astype(BF16)
        wu16[...] = wu_ref[0].astype(BF16)
        wd16[...] = wd_ref[0].astype(BF16)

        def block(j, slot):
            @pl.when(j + 1 < nb)
            def _():
                x_copy(j + 1, 1 - slot).start()

            x_copy(j, slot).wait()

            @pl.when(j >= 2)
            def _():
                y_copy(j - 2, slot).wait()

            xb = xbuf[slot].astype(BF16)
            g = _dot(xb, wg16[...])
            u = _dot(xb, wu16[...])
            hmid = (_silu(g) * u).astype(BF16)
            ybuf[slot] = _dot(hmid, wd16[...])
            y_copy(j, slot).start()

        def pair(jj, c):
            for slot in range(2):
                @pl.when(2 * jj + slot < nb)
                def _():
                    block(2 * jj + slot, slot)
            return c

        lax.fori_loop(0, (nb + 1) // 2, pair, 0)

        @pl.when(nb >= 2)
        def _():
            y_copy(nb - 2, lax.rem(nb, 2)).wait()
        y_copy(nb - 1, lax.rem(nb - 1, 2)).wait()

    @pl.when(e == N_EXPERTS - 1)
    def _():
        ybuf[0] = jnp.zeros_like(ybuf[0])

        def fill(b, c):
            cp = pltpu.make_async_copy(ybuf.at[0], y_hbm.at[pl.ds(pl.multiple_of(b * MOE_BLOCK, MOE_BLOCK),
                                                                    MOE_BLOCK), :], ysem.at[0])
            cp.start()
            cp.wait()
            return c

        lax.fori_loop(end // MOE_BLOCK, n_blocks, fill, 0)


def _moe(seg, x_sorted, w_gate, w_up, w_down):
    n_blocks = x_sorted.shape[0] // MOE_BLOCK
    wmap = lambda e, sg: (e, 0, 0)
    grid_spec = pltpu.PrefetchScalarGridSpec(
        num_scalar_prefetch=1,
        grid=(N_EXPERTS,),
        in_specs=[pl.BlockSpec((1, D_MODEL, D_EXPERT), wmap),
                  pl.BlockSpec((1, D_MODEL, D_EXPERT), wmap),
                  pl.BlockSpec((1, D_EXPERT, D_MODEL), wmap),
                  pl.BlockSpec(memory_space=pl.ANY)],
        out_specs=pl.BlockSpec(memory_space=pl.ANY),
        scratch_shapes=[pltpu.VMEM((D_MODEL, D_EXPERT), BF16),
                        pltpu.VMEM((D_MODEL, D_EXPERT), BF16),
                        pltpu.VMEM((D_EXPERT, D_MODEL), BF16),
                        pltpu.VMEM((2, MOE_BLOCK, D_MODEL), F32),
                        pltpu.VMEM((2, MOE_BLOCK, D_MODEL), F32),
                        pltpu.SemaphoreType.DMA((2,)),
                        pltpu.SemaphoreType.DMA((2,))])
    return pl.pallas_call(
        functools.partial(_moe_kernel, n_blocks=n_blocks),
        grid_spec=grid_spec,
        out_shape=jax.ShapeDtypeStruct((n_blocks * MOE_BLOCK, D_MODEL), F32),
        compiler_params=_cparams(("arbitrary",)),
        name="moe",
    )(seg, w_gate, w_up, w_down, x_sorted)


def _combine_kernel(dcur_ref, dnext_ref, route_ref, x1_ref, mod_ref, g_ref, y_hbm, o_ref, ybuf, sems):
    i = pl.program_id(0)
    last = pl.num_programs(0) - 1
    tm = x1_ref.shape[0]

    def gather(dest_ref, buf):
        for t in range(tm):
            for slot in range(2):
                pltpu.make_async_copy(y_hbm.at[pl.ds(dest_ref[0, 0, 2 * t + slot], 1), :],
                                      ybuf.at[buf, slot, pl.ds(t, 1), :], sems.at[buf]).start(priority=slot)

    @pl.when(i == 0)
    def _():
        gather(dcur_ref, 0)

    for buf in range(2):
        @pl.when(i % 2 == buf)
        def _():
            @pl.when(i < last)
            def _():
                gather(dnext_ref, 1 - buf)

            for slot in range(2):
                pltpu.make_async_copy(y_hbm.at[pl.ds(0, tm), :], ybuf.at[buf, slot], sems.at[buf]).wait()
            route = route_ref[...]
            moe = ybuf[buf, 0] * route[:, 2:3] + ybuf[buf, 1] * route[:, 3:4]
            o_ref[...] = x1_ref[...] + _per_seq(_rms(moe, g_ref[...]), mod_ref[...], lambda y, m: y * m[:, 5:6])


def _combine(dest, route, y_sorted, x1, mod8, g_post_ffn, tm, seq_rows, mod_row0):
    n = x1.shape[0]
    steps = n // tm
    row = lambda i: (i, 0)
    dest3 = dest.reshape(steps, 1, 2 * tm)
    return pl.pallas_call(
        _combine_kernel,
        grid=(steps,),
        in_specs=[pl.BlockSpec((1, 1, 2 * tm), lambda i: (i, 0, 0), memory_space=pltpu.SMEM),
                  pl.BlockSpec((1, 1, 2 * tm), lambda i: (jnp.minimum(i + 1, steps - 1), 0, 0),
                               memory_space=pltpu.SMEM),
                  pl.BlockSpec((tm, LANES), row),
                  pl.BlockSpec((tm, D_MODEL), row),
                  _mod_spec(tm, seq_rows, mod_row0),
                  pl.BlockSpec((1, D_MODEL), lambda i: (0, 0)),
                  pl.BlockSpec(memory_space=pl.ANY)],
        out_specs=pl.BlockSpec((tm, D_MODEL), row),
        out_shape=jax.ShapeDtypeStruct((n, D_MODEL), F32),
        scratch_shapes=[pltpu.VMEM((2, 2, tm, D_MODEL), F32), pltpu.SemaphoreType.DMA((2,))],
        compiler_params=_cparams(("arbitrary",)),
        name="combine",
    )(dest3, dest3, route, x1, mod8, g_post_ffn, y_sorted)


def _segment_plan(counts_f):
    counts = counts_f[0, :N_EXPERTS].astype(jnp.int32)
    padded = (counts + MOE_BLOCK - 1) // MOE_BLOCK * MOE_BLOCK
    pad_end = jnp.cumsum(padded)
    seg = jnp.concatenate([counts, pad_end]).astype(jnp.int32)
    return seg, pad_end - padded


def _token_rows(route, pad_start):
    eid = route[:, 0:2].astype(jnp.int32)
    rank = route[:, 4:6].astype(jnp.int32)
    onehot = eid[:, :, None] == jnp.arange(N_EXPERTS, dtype=jnp.int32)[None, None, :]
    return rank + jnp.sum(jnp.where(onehot, pad_start[None, None, :], 0), axis=2)


def _layer(x_p, x_s, c_p, c_s, k_past, v_past, s0_s, conv_s, p):
    bp, tp, d = x_p.shape
    bs, ts, _ = x_s.shape
    n_p, n_s = bp * tp, bs * ts
    n_tok = n_p + n_s

    n_seq = bp + bs
    c_all = jnp.zeros((16, d), F32).at[:n_seq].set(jnp.concatenate([c_s, c_p], axis=0))
    mod = _ada(c_all, p['w_ada'], p['b_ada'])
    mod8 = jnp.pad(mod.reshape(16, 6, d), ((0, 0), (0, 2), (0, 0)))

    w_in = p['w_in']
    o_z, o_b, o_q = CONV_CH, CONV_CH + 512, CONV_CH + 512 + 2 * DN_HEADS
    wb = jnp.pad(w_in[:, o_b:o_q], ((0, 0), (0, LANES - 2 * DN_HEADS)))
    wb_hi = wb.astype(BF16)
    wb_lo = (wb - wb_hi.astype(F32)).astype(BF16)
    w_main = jnp.concatenate([w_in[:, :o_b].astype(BF16), w_in[:, o_q:].astype(BF16), wb_hi, wb_lo], axis=1)
    g_pre_mix = p['g_pre_mix'].reshape(1, d)

    conv_w8 = jnp.pad(p['conv_w'], ((0, 8 - CONV_W), (0, 0)))
    pad_g = lambda a: jnp.pad(a.reshape(1, DN_HEADS), ((0, 0), (DN_HEADS, LANES - 2 * DN_HEADS)))
    alog_row, dtb_row = pad_g(p['a_log']), pad_g(p['dt_bias'])
    onorm_a = p['onorm_a'].reshape(1, DN_DIM)
    onorm_b2 = jnp.tile(p['onorm_b'].reshape(1, SB_DIM), (1, 2))

    w_out16 = p['w_out'].astype(BF16)
    wr = jnp.pad(jnp.concatenate([p['w_router_group'], p['w_router_expert']], axis=1),
                 ((0, 0), (0, LANES - N_GROUPS - N_EXPERTS)))
    wr_hi = wr.astype(BF16)
    wr_lo = (wr - wr_hi.astype(F32)).astype(BF16)
    b_r = jnp.pad(jnp.concatenate([p['b_router_group'], p['b_router_expert']]).reshape(1, -1),
                  ((0, 0), (0, LANES - N_GROUPS - N_EXPERTS)))
    g_post_mix = p['g_post_mix'].reshape(1, d)
    g_pre_ffn = p['g_pre_ffn'].reshape(1, d)
    g_post_ffn = p['g_post_ffn'].reshape(1, d)

    def mixer(x, tm, tm_post, seq_rows, mod_row0, hist8, s0, k_old, v_old, chunk, n_chunks, bq, n_sub, cnt0):
        b, t, _ = x.shape
        x2d = x.reshape(b * t, d)
        a_in, z, bg, q16, kb, vb, k16, v16 = _proj(x2d, mod8, g_pre_mix, w_main, wb_hi, tm, seq_rows, mod_row0)
        r3 = lambda a: a.reshape(b, t, a.shape[-1])
        oa16, s_new = _delta(r3(a_in), r3(z), r3(bg), hist8, s0, conv_w8, alog_row, dtb_row, onorm_a,
                             chunk, n_chunks)
        k16, v16 = r3(k16), r3(v16)
        if k_old is not None:
            k16 = jnp.concatenate([k_old.reshape(b, -1, SB_WIDTH).astype(BF16), k16], axis=1)
            v16 = jnp.concatenate([v_old.reshape(b, -1, SB_WIDTH).astype(BF16), v16], axis=1)
        n_pad = (-k16.shape[1]) % KEY_TILE
        k16 = jnp.pad(k16, ((0, 0), (n_pad, 0), (0, 0)))
        v16 = jnp.pad(v16, ((0, 0), (n_pad, 0), (0, 0)))
        ob16 = _sb_attn(r3(q16), k16, v16, onorm_b2, bq, n_sub, n_pad)
        x1, h2, route, cnt = _post(oa16.reshape(b * t, 512), ob16.reshape(b * t, 512), x2d, mod8,
                                   g_post_mix, g_pre_ffn, w_out16, wr_hi, wr_lo, b_r, cnt0, tm_post, seq_rows,
                                   mod_row0)
        new_conv = r3(a_in)[:, t - (CONV_W - 1):, :]
        return (x1, h2, route, cnt, kb.reshape(b, t, SB_HEADS, SB_DIM), vb.reshape(b, t, SB_HEADS, SB_DIM),
                s_new, new_conv)

    zero_hist = jnp.zeros((bp, 8, CONV_CH), F32)
    zero_s = jnp.zeros((bp, DN_HEADS, DN_DIM, DN_DIM), F32)
    hist_s = jnp.pad(conv_s, ((0, 0), (8 - (CONV_W - 1), 0), (0, 0)))
    tm_p = min(256, tp)
    tm_post = tm_p * POST_TILE_FACTOR if n_p % (tm_p * POST_TILE_FACTOR) == 0 else tm_p
    nc_p = max(1, min(4, tp // DELTA_BLOCK))
    x1p, h2p, rp, cnt_p, kp, vp, sp, cp = mixer(x_p, tm_p, tm_post, tp, bs, zero_hist, zero_s, None, None,
                                                 min(DELTA_BLOCK, tp), nc_p, min(KEY_TILE // 2, tp), 2,
                                                 jnp.zeros((1, LANES), F32))
    x1s, h2s, rs, cnt, ks, vs, ss, cs = mixer(x_s, n_s, n_s, ts, 0, hist_s, s0_s, k_past, v_past,
                                               min(DELTA_BLOCK, ts), max(1, ts // DELTA_BLOCK),
                                               min(KEY_TILE, ts), 1, cnt_p)

    n_blocks = -(-2 * n_tok // MOE_BLOCK) + N_EXPERTS
    seg, pad_start = _segment_plan(cnt)
    dest_p = _token_rows(rp, pad_start)
    dest_s = _token_rows(rs, pad_start)
    x_sorted = _dispatch(seg, dest_p, dest_s, h2p, h2s, n_blocks, tm_p)
    y_sorted = _moe(seg, x_sorted, p['w_gate'], p['w_up'], p['w_down'])
    y_p = _combine(dest_p, rp, y_sorted, x1p, mod8, g_post_ffn, tm_p, tp, bs).reshape(bp, tp, d)
    y_s = _combine(dest_s, rs, y_sorted, x1s, mod8, g_post_ffn, n_s, ts, 0).reshape(bs, ts, d)
    return y_p, y_s, kp, vp, sp, cp, ks, vs, ss, cs


def kernel(x_prompt, x_sample, c_prompt, c_sample, cache_k, cache_v, state_delta, state_conv, w_ada, b_ada, g_pre_mix, g_post_mix, g_pre_ffn, g_post_ffn, w_in, conv_w, a_log, dt_bias, onorm_a, onorm_b, w_out, w_router_group, b_router_group, w_router_expert, b_router_expert, w_gate, w_up, w_down):
    depth = w_in.shape[0]
    y_p, y_s = x_prompt, x_sample
    outs = [[] for _ in range(8)]
    for l in range(depth):
        p = dict(w_ada=w_ada[l], b_ada=b_ada[l], g_pre_mix=g_pre_mix[l], g_post_mix=g_post_mix[l],
                 g_pre_ffn=g_pre_ffn[l], g_post_ffn=g_post_ffn[l], w_in=w_in[l], conv_w=conv_w[l],
                 a_log=a_log[l], dt_bias=dt_bias[l], onorm_a=onorm_a[l], onorm_b=onorm_b[l],
                 w_out=w_out[l], w_router_group=w_router_group[l], b_router_group=b_router_group[l],
                 w_router_expert=w_router_expert[l], b_router_expert=b_router_expert[l],
                 w_gate=w_gate[l], w_up=w_up[l], w_down=w_down[l])
        res = _layer(y_p, y_s, c_prompt, c_sample, cache_k[l], cache_v[l], state_delta[l], state_conv[l], p)
        y_p, y_s = res[0], res[1]
        for lst, r in zip(outs, res[2:]):
            lst.append(r)
    return (y_p, y_s) + tuple(jnp.stack(o) for o in outs)
```

```python
import functools
import math

import jax
import jax.numpy as jnp
from jax import lax
from jax.experimental import pallas as pl
from jax.experimental.pallas import tpu as pltpu

F32 = jnp.float32
BF16 = jnp.bfloat16

D_MODEL = 1024
DN_HEADS = 4
DN_DIM = 128
CONV_W = 4
CONV_CH = DN_HEADS * 3 * DN_DIM
DELTA_BLOCK = 64
SB_HEADS = 8
SB_DIM = 64
SB_WIDTH = SB_HEADS * SB_DIM
N_GROUPS = 4
EXPERTS_PER_GROUP = 8
N_EXPERTS = N_GROUPS * EXPERTS_PER_GROUP
D_EXPERT = D_MODEL // 2
MOE_BLOCK = 256
EPS = 1e-6

LANES = 128
KEY_TILE = 128
ATTN_WINDOW = 3 * KEY_TILE
WINDOW_BLOCKS = 4
EXP_ZERO_BELOW = -104.0
VMEM_LIMIT = 56 * 1024 * 1024
DENSE_TILE = 512
ROW_TILE = 256


def _cparams(sem):
    return pltpu.CompilerParams(dimension_semantics=sem, vmem_limit_bytes=VMEM_LIMIT)


def _split(a):
    hi = a.astype(BF16)
    lo = (a - hi.astype(F32)).astype(BF16)
    return hi, lo


def _dot(a, b, dims=(((1,), (0,)), ((), ()))):
    return lax.dot_general(a, b, dims, preferred_element_type=F32)


def _dot_small_int_lhs(a, b):
    a16 = a.astype(BF16)
    return _dot(jnp.concatenate([a16, a16], axis=1), jnp.concatenate(_split(b), axis=0))


_NT = (((1,), (1,)), ((), ()))
_TN = (((0,), (0,)), ((), ()))


def _silu(x):
    return x * jax.nn.sigmoid(x)


SOFTPLUS_LINEAR_ABOVE = 80.0


def _softplus(x):
    return jnp.where(x > SOFTPLUS_LINEAR_ABOVE, x, jnp.log(1.0 + jnp.exp(jnp.minimum(x, SOFTPLUS_LINEAR_ABOVE))))


def _ada_kernel(c_ref, w_ref, b_ref, o_ref):
    s = _silu(c_ref[...]).astype(BF16)
    o_ref[...] = _dot(s, w_ref[...].astype(BF16)) + b_ref[...]


def _ada(c_all, w_ada, b_ada):
    rows = c_all.shape[0]
    n = w_ada.shape[1]
    tn = 1024
    return pl.pallas_call(
        _ada_kernel,
        grid=(n // tn,),
        in_specs=[pl.BlockSpec((rows, D_MODEL), lambda j: (0, 0)),
                  pl.BlockSpec((D_MODEL, tn), lambda j: (0, j)),
                  pl.BlockSpec((1, tn), lambda j: (0, j))],
        out_specs=pl.BlockSpec((rows, tn), lambda j: (0, j)),
        out_shape=jax.ShapeDtypeStruct((rows, n), F32),
        compiler_params=_cparams(("arbitrary",)),
        name="ada",
    )(c_all, w_ada, b_ada.reshape(1, n))


def _rms(x, gain):
    return x * lax.rsqrt(jnp.mean(x * x, axis=-1, keepdims=True) + EPS) * gain


def _per_seq(y, mod, fn):
    n_seq = mod.shape[0]
    ys = y.reshape(n_seq, y.shape[0] // n_seq, y.shape[1])
    return fn(ys, mod).reshape(y.shape)


def _mod_spec(tm, seq_rows, mod_row0):
    if tm >= seq_rows:
        n_seq = tm // seq_rows
        assert tm % seq_rows == 0 and mod_row0 % n_seq == 0
        return pl.BlockSpec((n_seq, 8, D_MODEL), lambda i: (mod_row0 // n_seq + i, 0, 0))
    assert seq_rows % tm == 0
    return pl.BlockSpec((1, 8, D_MODEL), lambda i: (mod_row0 + (i * tm) // seq_rows, 0, 0))


def _proj_kernel(x_ref, mod_ref, g_ref, wm_ref, wbh_ref,
                 a_ref, z_ref, bg_ref, q_ref, k_ref, v_ref, k16_ref, v16_ref):
    h = _per_seq(_rms(x_ref[...], g_ref[...]), mod_ref[...], lambda y, m: y * (1.0 + m[:, 1:2]) + m[:, 0:1])
    hh, hl = _split(h)
    p = _dot(hh, wm_ref[...])
    a_ref[...] = p[:, 0:CONV_CH]
    z_ref[...] = p[:, CONV_CH:CONV_CH + 512]
    o = CONV_CH + 512
    q_ref[...] = (p[:, o:o + 512] * (SB_DIM ** -0.5)).astype(BF16)
    k = p[:, o + 512:o + 1024]
    v = p[:, o + 1024:o + 1536]
    k_ref[...] = k
    v_ref[...] = v
    k16_ref[...] = k.astype(BF16)
    v16_ref[...] = v.astype(BF16)
    bg_ref[...] = p[:, o + 1536:o + 1536 + LANES] + (_dot(hl, wbh_ref[...]) + p[:, o + 1536 + LANES:])


def _proj(x2d, mod8, g_pre, w_main, wb_hi, tm, seq_rows, mod_row0):
    n = x2d.shape[0]
    nm = w_main.shape[1]
    row = lambda i: (i, 0)
    const = lambda i: (0, 0)
    outs = [(CONV_CH, F32), (512, F32), (LANES, F32), (512, BF16), (512, F32), (512, F32),
            (512, BF16), (512, BF16)]
    return pl.pallas_call(
        _proj_kernel,
        grid=(n // tm,),
        in_specs=[pl.BlockSpec((tm, D_MODEL), row),
                  _mod_spec(tm, seq_rows, mod_row0),
                  pl.BlockSpec((1, D_MODEL), const),
                  pl.BlockSpec((D_MODEL, nm), const),
                  pl.BlockSpec((D_MODEL, LANES), const)],
        out_specs=[pl.BlockSpec((tm, w), row) for w, _ in outs],
        out_shape=[jax.ShapeDtypeStruct((n, w), dt) for w, dt in outs],
        compiler_params=_cparams(("arbitrary",)),
        name="proj",
    )(x2d, mod8, g_pre, w_main, wb_hi)


def _delta_kernel(a_ref, z_ref, bg_ref, hist0_ref, s0_ref, cw_ref, alog_ref, dtb_ref, on_ref,
                  o_ref, sfin_ref, hist_sc, s_sc, *, chunk, n_chunks):
    t_idx = pl.program_id(1)
    tt = chunk * n_chunks

    @pl.when(t_idx == 0)
    def _():
        hist_sc[...] = hist0_ref[0]
        s_sc[...] = s0_ref[0]

    x = a_ref[0]
    xx = jnp.concatenate([hist_sc[...], x], axis=0)
    cw = cw_ref[...]
    conv = x * cw[CONV_W - 1:CONV_W]
    for s in range(1, CONV_W):
        conv = conv + pltpu.roll(xx, s, 0)[8:] * cw[CONV_W - 1 - s:CONV_W - s]
    conv = _silu(conv)
    hist_sc[...] = x[tt - 8:tt]

    bg = bg_ref[0]
    lane = lax.broadcasted_iota(jnp.int32, (1, LANES), 1)
    g_lane = (lane >= DN_HEADS) & (lane < 2 * DN_HEADS)
    neg_a = jnp.where(g_lane, -jnp.exp(alog_ref[...]), 0.0)
    beta_all = jax.nn.sigmoid(bg)
    g_all = neg_a * _softplus(bg + dtb_ref[...])

    ri = lax.broadcasted_iota(jnp.int32, (chunk, chunk), 0)
    ci = lax.broadcasted_iota(jnp.int32, (chunk, chunk), 1)
    incl = ri >= ci
    strict = ri > ci
    tri = incl.astype(F32)
    eye = (ri == ci).astype(F32)
    ones_cc = jnp.ones((chunk, chunk), F32)
    onorm = on_ref[...]

    chunks = range(n_chunks)
    units = [(c, h) for c in chunks for h in range(DN_HEADS)]
    rows = lambda c: slice(c * chunk, (c + 1) * chunk)

    gc_all = [_dot_small_int_lhs(tri, g_all[rows(c)]) for c in chunks]
    diag = [jnp.concatenate([eye * gc_all[c][:, DN_HEADS + h:DN_HEADS + h + 1] for h in range(DN_HEADS)], axis=1)
            for c in chunks]
    gc_rows = [_dot_small_int_lhs(ones_cc, diag[c]) for c in chunks]

    q, k, v, beta, gcol, glast, eg, decay, kbeta = {}, {}, {}, {}, {}, {}, {}, {}, {}
    for (c, h) in units:
        lo = h * DN_DIM
        qq = conv[rows(c), lo:lo + DN_DIM]
        kk = conv[rows(c), 512 + lo:512 + lo + DN_DIM]
        u_ = (c, h)
        q[u_] = qq * lax.rsqrt(jnp.sum(qq * qq, axis=-1, keepdims=True) + EPS) * (DN_DIM ** -0.5)
        k[u_] = kk * lax.rsqrt(jnp.sum(kk * kk, axis=-1, keepdims=True) + EPS)
        v[u_] = conv[rows(c), 1024 + lo:1024 + lo + DN_DIM]
        beta[u_] = beta_all[rows(c), h:h + 1]
        gcol[u_] = gc_all[c][:, DN_HEADS + h:DN_HEADS + h + 1]
        glast[u_] = gcol[u_][chunk - 1:chunk, :]
        eg[u_] = jnp.exp(gcol[u_])
        grow = gc_rows[c][:, h * chunk:(h + 1) * chunk]
        decay[u_] = jnp.where(incl, jnp.exp(jnp.minimum(gcol[u_] - grow, 0.0)), 0.0)
        kbeta[u_] = k[u_] * beta[u_]

    k16 = {u_: k[u_].astype(BF16) for u_ in units}
    kq = {u_: _dot(jnp.concatenate([kbeta[u_], q[u_]], axis=0).astype(BF16), k16[u_], _NT) for u_ in units}
    lower = {u_: jnp.where(strict, kq[u_][:chunk] * decay[u_], 0.0) for u_ in units}
    attn16 = {u_: jnp.where(incl, kq[u_][chunk:] * decay[u_], 0.0).astype(BF16) for u_ in units}

    sol = {u_: jnp.concatenate([v[u_] * beta[u_], kbeta[u_] * eg[u_]], axis=1) for u_ in units}
    lp = lower
    p = 1
    while p < chunk:
        lsp = {u_: _split(lp[u_]) for u_ in units}
        ssp = {u_: _split(sol[u_]) for u_ in units}
        lcat = {u_: jnp.concatenate([lsp[u_][0], lsp[u_][1], lsp[u_][0]], axis=1) for u_ in units}
        upd = {u_: _dot(lcat[u_], jnp.concatenate([ssp[u_][0], ssp[u_][0], ssp[u_][1]], axis=0)) for u_ in units}
        sol = {u_: (sol[u_] - upd[u_]) if p == 1 else (sol[u_] + upd[u_]) for u_ in units}
        if 2 * p < chunk:
            lp = {u_: _dot(lcat[u_], jnp.concatenate([lsp[u_][0], lsp[u_][0], lsp[u_][1]], axis=0))
                  for u_ in units}
        p *= 2

    usol = {u_: sol[u_][:, :DN_DIM] for u_ in units}
    wq16 = {u_: jnp.concatenate([sol[u_][:, DN_DIM:], q[u_] * eg[u_]], axis=0).astype(BF16) for u_ in units}
    kd16 = {u_: (k[u_] * jnp.exp(glast[u_] - gcol[u_])).astype(BF16) for u_ in units}

    for c in chunks:
        hs = range(DN_HEADS)
        s_old = [s_sc[h] for h in hs]
        ws = [_dot(wq16[(c, h)], s_old[h].astype(BF16)) for h in hs]
        v_new = [(usol[(c, h)] - ws[h][:chunk]).astype(BF16) for h in hs]
        o_in = [_dot(attn16[(c, h)], v_new[h]) for h in hs]
        ds = [_dot(kd16[(c, h)], v_new[h], _TN) for h in hs]
        for h in hs:
            lo = h * DN_DIM
            s_sc[h] = s_old[h] * jnp.exp(glast[(c, h)]) + ds[h]
            o = ws[h][chunk:] + o_in[h]
            zg = z_ref[0, rows(c), lo:lo + DN_DIM]
            o_ref[0, rows(c), lo:lo + DN_DIM] = (_rms(o, onorm) * _silu(zg)).astype(BF16)

    @pl.when(t_idx == pl.num_programs(1) - 1)
    def _():
        sfin_ref[0] = s_sc[...]


def _delta(a_in, z, bg, hist8, s0, conv_w8, alog_row, dtb_row, onorm_a, chunk, n_chunks):
    b, t, _ = a_in.shape
    tt = chunk * n_chunks
    tile = lambda bi, ti: (bi, ti, 0)
    per_b3 = lambda bi, ti: (bi, 0, 0)
    per_b4 = lambda bi, ti: (bi, 0, 0, 0)
    const = lambda bi, ti: (0, 0)
    kern = functools.partial(_delta_kernel, chunk=chunk, n_chunks=n_chunks)
    return pl.pallas_call(
        kern,
        grid=(b, t // tt),
        in_specs=[pl.BlockSpec((1, tt, CONV_CH), tile),
                  pl.BlockSpec((1, tt, 512), tile),
                  pl.BlockSpec((1, tt, LANES), tile),
                  pl.BlockSpec((1, 8, CONV_CH), per_b3),
                  pl.BlockSpec((1, DN_HEADS, DN_DIM, DN_DIM), per_b4),
                  pl.BlockSpec((8, CONV_CH), const),
                  pl.BlockSpec((1, LANES), const),
                  pl.BlockSpec((1, LANES), const),
                  pl.BlockSpec((1, DN_DIM), const)],
        out_specs=[pl.BlockSpec((1, tt, 512), tile),
                   pl.BlockSpec((1, DN_HEADS, DN_DIM, DN_DIM), per_b4)],
        out_shape=[jax.ShapeDtypeStruct((b, t, 512), BF16),
                   jax.ShapeDtypeStruct((b, DN_HEADS, DN_DIM, DN_DIM), F32)],
        scratch_shapes=[pltpu.VMEM((8, CONV_CH), F32),
                        pltpu.VMEM((DN_HEADS, DN_DIM, DN_DIM), F32)],
        compiler_params=_cparams(("arbitrary", "arbitrary")),
        name="delta",
    )(a_in, z, bg, hist8, s0, conv_w8, alog_row, dtb_row, onorm_a)


def _sb_kernel(q_ref, *refs, bq, n_sub, n_pad, q_off):
    kwin = refs[:WINDOW_BLOCKS]
    vwin = refs[WINDOW_BLOCKS:2 * WINDOW_BLOCKS]
    k_hbm, v_hbm, on_ref, o_ref, kbuf, vbuf, qsel, acc, carry, sem = refs[2 * WINDOW_BLOCKS:]
    b = pl.program_id(0)
    i = pl.program_id(1)
    g = bq * n_sub
    qend_step = q_off + (i + 1) * g
    win_start = qend_step - WINDOW_BLOCKS * KEY_TILE
    n_pairs = SB_HEADS // 2
    heads = range(SB_HEADS)

    half_lane = lax.broadcasted_iota(jnp.int32, (bq, LANES), 1) < SB_DIM
    rj = lax.broadcasted_iota(jnp.int32, (2 * KEY_TILE, 2 * KEY_TILE), 0) % KEY_TILE
    cj = lax.broadcasted_iota(jnp.int32, (2 * KEY_TILE, 2 * KEY_TILE), 1)
    suffix2 = ((rj > cj) | (cj >= KEY_TILE)).astype(BF16)

    def window(blocks, off, p):
        parts = []
        for blk in range(WINDOW_BLOCKS):
            lo, hi = max(off, blk * KEY_TILE), min(off + ATTN_WINDOW, (blk + 1) * KEY_TILE)
            if lo < hi:
                ref = blocks[WINDOW_BLOCKS - 1 - blk]
                parts.append(ref[0, lo - blk * KEY_TILE:hi - blk * KEY_TILE, p * LANES:(p + 1) * LANES])
        return jnp.concatenate(parts, axis=0)

    n_t = ATTN_WINDOW // KEY_TILE
    row = lax.broadcasted_iota(jnp.int32, (bq, ATTN_WINDOW), 0)
    col = lax.broadcasted_iota(jnp.int32, (bq, ATTN_WINDOW), 1)
    causal = col < row + (ATTN_WINDOW - bq)
    subs = range(n_sub)
    offs = [WINDOW_BLOCKS * KEY_TILE - g + (s + 1) * bq - ATTN_WINDOW for s in subs]
    mask = [causal & (col >= n_pad - (win_start + offs[s])) for s in subs]
    zs, sp, cs_all = {}, {}, {}
    for s in subs:
        for p in range(n_pairs):
            qf = q_ref[0, s * bq:(s + 1) * bq, p * LANES:(p + 1) * LANES].astype(F32)
            qq = jnp.concatenate([jnp.where(half_lane, qf, 0.0), jnp.where(half_lane, 0.0, qf)], axis=0)
            zz = _dot(qq.astype(BF16), window(kwin, offs[s], p), _NT)
            zs[(s, 2 * p)], zs[(s, 2 * p + 1)] = zz[:bq], zz[bq:]
    for s in subs:
        pieces = []
        for h in heads:
            sp[(s, h)] = _softplus(zs[(s, h)])
            hi, lo = _split(jnp.where(mask[s], -sp[(s, h)], 0.0))
            for j in range(n_t):
                c0 = ATTN_WINDOW - (j + 1) * KEY_TILE
                pieces.append(jnp.concatenate([hi[:, c0:c0 + KEY_TILE], lo[:, c0:c0 + KEY_TILE]], axis=1))
        cs_all[s] = _dot(jnp.concatenate(pieces, axis=0), suffix2)
    done = []
    for s in subs:
        worst = jnp.full((bq, KEY_TILE), -jnp.inf, F32)
        a = {}
        for h in heads:
            run = None
            cols = []
            for j in range(n_t):
                r0 = (h * n_t + j) * bq
                cs = cs_all[s][r0:r0 + bq]
                cols.append(cs[:, :KEY_TILE] if run is None else cs[:, :KEY_TILE] + run)
                run = cs[:, KEY_TILE:] if run is None else run + cs[:, KEY_TILE:]
            within = jnp.concatenate(cols[::-1], axis=1)
            a[h] = jnp.where(mask[s], jnp.exp((zs[(s, h)] - sp[(s, h)]) + within), 0.0).astype(BF16)
            carry[s, h] = run
            worst = jnp.maximum(worst, run)
        for p in range(n_pairs):
            pv = _dot(jnp.concatenate([a[2 * p], a[2 * p + 1]], axis=0), window(vwin, offs[s], p))
            acc[s, p] = jnp.where(half_lane, pv[:bq], pv[bq:])
        done.append((jnp.max(worst) < EXP_ZERO_BELOW).astype(jnp.int32))

    rj1 = lax.broadcasted_iota(jnp.int32, (KEY_TILE, 2 * KEY_TILE), 0)
    cj1 = lax.broadcasted_iota(jnp.int32, (KEY_TILE, 2 * KEY_TILE), 1)
    suffix1 = ((rj1 > cj1) | (cj1 >= KEY_TILE)).astype(BF16)
    col1 = lax.broadcasted_iota(jnp.int32, (bq, KEY_TILE), 1)
    for s in subs:
        swept_from = win_start + offs[s]

        @pl.when((done[s] == 0) & (swept_from > n_pad))
        def _():
            for p in range(n_pairs):
                qf = q_ref[0, s * bq:(s + 1) * bq, p * LANES:(p + 1) * LANES].astype(F32)
                qsel[2 * p] = jnp.where(half_lane, qf, 0.0).astype(BF16)
                qsel[2 * p + 1] = jnp.where(half_lane, 0.0, qf).astype(BF16)

            def body(state):
                upper, _ = state
                start = jnp.maximum(upper - KEY_TILE, 0)
                copies = []
                for p in range(n_pairs):
                    for src, dst in ((k_hbm, kbuf), (v_hbm, vbuf)):
                        cp = pltpu.make_async_copy(
                            src.at[b, pl.ds(pl.multiple_of(start, 16), KEY_TILE), pl.ds(p * LANES, LANES)],
                            dst.at[p], sem)
                        cp.start()
                        copies.append(cp)
                for cp in copies:
                    cp.wait()
                kpos = start + col1
                m1 = (kpos < upper) & (kpos >= n_pad)
                zz = [_dot(qsel[h], kbuf[h // 2], _NT) for h in heads]
                spp = [_softplus(zz[h]) for h in heads]
                pcs = []
                for h in heads:
                    pcs.extend(_split(jnp.where(m1, -spp[h], 0.0)))
                cs1 = _dot(jnp.concatenate(pcs, axis=0), suffix1)
                worst = jnp.full((bq, KEY_TILE), -jnp.inf, F32)
                aa = []
                for h in heads:
                    c = cs1[2 * h * bq:(2 * h + 1) * bq] + cs1[(2 * h + 1) * bq:(2 * h + 2) * bq]
                    c_old = carry[s, h]
                    aa.append(jnp.where(m1, jnp.exp((zz[h] - spp[h]) + c[:, :KEY_TILE] + c_old), 0.0).astype(BF16))
                    c_new = c_old + c[:, KEY_TILE:]
                    carry[s, h] = c_new
                    worst = jnp.maximum(worst, c_new)
                for p in range(n_pairs):
                    pv0 = _dot(aa[2 * p], vbuf[p])
                    pv1 = _dot(aa[2 * p + 1], vbuf[p])
                    acc[s, p] = acc[s, p] + jnp.where(half_lane, pv0, pv1)
                return start, (jnp.max(worst) < EXP_ZERO_BELOW).astype(jnp.int32)

            lax.while_loop(lambda st: (st[0] > n_pad) & (st[1] == 0), body, (swept_from, jnp.int32(0)))

    onb = on_ref[...]
    for s in subs:
        for p in range(n_pairs):
            o = acc[s, p]
            sq = o * o
            s_lo = jnp.sum(jnp.where(half_lane, sq, 0.0), axis=-1, keepdims=True)
            s_hi = jnp.sum(jnp.where(half_lane, 0.0, sq), axis=-1, keepdims=True)
            ms = jnp.where(half_lane, s_lo, s_hi) * (1.0 / SB_DIM)
            o_ref[0, s * bq:(s + 1) * bq, p * LANES:(p + 1) * LANES] = (o * lax.rsqrt(ms + EPS) * onb).astype(BF16)


def _sb_attn(q16, k16p, v16p, onorm_b2, bq, n_sub, n_pad):
    b, tq, _ = q16.shape
    tkp = k16p.shape[1]
    q_off = tkp - tq
    g = bq * n_sub
    assert tq % g == 0 and g <= KEY_TILE and bq % 16 == 0
    assert all((q_off + (i + 1) * g) % KEY_TILE == 0 for i in range(tq // g))

    def kmap(back):
        def f(bi, i):
            last = (q_off + (i + 1) * g) // KEY_TILE - 1
            return (bi, jnp.maximum(last - back, 0), 0)
        return f

    qmap = lambda bi, i: (bi, i, 0)
    kern = functools.partial(_sb_kernel, bq=bq, n_sub=n_sub, n_pad=n_pad, q_off=q_off)
    kspec = [pl.BlockSpec((1, KEY_TILE, SB_WIDTH), kmap(back)) for back in range(WINDOW_BLOCKS)]
    return pl.pallas_call(
        kern,
        grid=(b, tq // g),
        in_specs=[pl.BlockSpec((1, g, SB_WIDTH), qmap)] + kspec + kspec
                 + [pl.BlockSpec(memory_space=pl.ANY), pl.BlockSpec(memory_space=pl.ANY),
                    pl.BlockSpec((1, LANES), lambda bi, i: (0, 0))],
        out_specs=pl.BlockSpec((1, g, SB_WIDTH), qmap),
        out_shape=jax.ShapeDtypeStruct((b, tq, SB_WIDTH), BF16),
        scratch_shapes=[pltpu.VMEM((SB_HEADS // 2, KEY_TILE, LANES), BF16),
                        pltpu.VMEM((SB_HEADS // 2, KEY_TILE, LANES), BF16),
                        pltpu.VMEM((SB_HEADS, bq, LANES), BF16),
                        pltpu.VMEM((n_sub, SB_HEADS // 2, bq, LANES), F32),
                        pltpu.VMEM((n_sub, SB_HEADS, bq, KEY_TILE), F32),
                        pltpu.SemaphoreType.DMA(())],
        compiler_params=_cparams(("arbitrary", "arbitrary")),
        name="sb_attn",
    )(q16, *([k16p] * WINDOW_BLOCKS), *([v16p] * WINDOW_BLOCKS), k16p, v16p, onorm_b2)


def _post_kernel(oa_ref, ob_ref, x_ref, mod_ref, gpm_ref, gpf_ref, wo_ref, wrh_ref, wrl_ref, br_ref, cnt0_ref,
                 x1_ref, h2_ref, route_ref, cnt_ref):
    @pl.when(pl.program_id(0) == 0)
    def _():
        cnt_ref[...] = cnt0_ref[...]

    mod = mod_ref[...]
    mix = _dot(oa_ref[...], wo_ref[0:512, :]) + _dot(ob_ref[...], wo_ref[512:1024, :])
    x1 = x_ref[...] + _per_seq(_rms(mix, gpm_ref[...]), mod, lambda y, m: y * m[:, 2:3])
    x1_ref[...] = x1
    h2 = _per_seq(_rms(x1, gpf_ref[...]), mod, lambda y, m: y * (1.0 + m[:, 4:5]) + m[:, 3:4])
    h2_ref[...] = h2
    hh, hl = _split(h2)
    wrh = wrh_ref[...]
    logits = _dot(hh, wrh) + (_dot(hl, wrh) + _dot(hh, wrl_ref[...])) + br_ref[...]
    lane = lax.broadcasted_iota(jnp.int32, logits.shape, 1).astype(F32)
    neg = -jnp.inf
    nl = float(LANES)
    lg = jnp.where(lane < N_GROUPS, logits, neg)
    gmax = jnp.max(lg, axis=-1, keepdims=True)
    grp = jnp.min(jnp.where(lg == gmax, lane, nl), axis=-1, keepdims=True)
    p_grp = 1.0 / jnp.sum(jnp.exp(lg - gmax), axis=-1, keepdims=True)
    first = N_GROUPS + grp * EXPERTS_PER_GROUP
    le = jnp.where((lane >= first) & (lane < first + EXPERTS_PER_GROUP), logits, neg)
    emax = jnp.max(le, axis=-1, keepdims=True)
    i1 = jnp.min(jnp.where(le == emax, lane, nl), axis=-1, keepdims=True)
    esum = jnp.sum(jnp.exp(le - emax), axis=-1, keepdims=True)
    le2 = jnp.where(lane == i1, neg, le)
    e2max = jnp.max(le2, axis=-1, keepdims=True)
    i2 = jnp.min(jnp.where(le2 == e2max, lane, nl), axis=-1, keepdims=True)
    p1 = 1.0 / esum
    p2 = jnp.exp(e2max - emax) / esum
    w1 = p_grp * p1 / (p1 + p2)
    w2 = p_grp * p2 / (p1 + p2)
    e1 = i1 - N_GROUPS
    e2 = i2 - N_GROUPS
    hot1 = (lane == e1).astype(F32)
    hot2 = (lane == e2).astype(F32)
    both = hot1 + hot2
    tm = logits.shape[0]
    ti = lax.broadcasted_iota(jnp.int32, (tm, tm), 0)
    tj = lax.broadcasted_iota(jnp.int32, (tm, tm), 1)
    earlier = _dot((ti > tj).astype(BF16), both.astype(BF16)) + cnt_ref[...]
    rank1 = jnp.sum(hot1 * earlier, axis=-1, keepdims=True)
    rank2 = jnp.sum(hot2 * (earlier + hot1), axis=-1, keepdims=True)
    cnt_ref[...] = cnt_ref[...] + jnp.sum(both, axis=0, keepdims=True)
    out = jnp.where(lane == 0.0, e1, 0.0)
    out = jnp.where(lane == 1.0, e2, out)
    out = jnp.where(lane == 2.0, w1, out)
    out = jnp.where(lane == 3.0, w2, out)
    out = jnp.where(lane == 4.0, rank1, out)
    out = jnp.where(lane == 5.0, rank2, out)
    route_ref[...] = out


def _post(oa16, ob16, x2d, mod8, g_post_mix, g_pre_ffn, w_out16, wr_hi, wr_lo, b_r, cnt0, tm, seq_rows,
          mod_row0):
    n = x2d.shape[0]
    row = lambda i: (i, 0)
    const = lambda i: (0, 0)
    return pl.pallas_call(
        _post_kernel,
        grid=(n // tm,),
        in_specs=[pl.BlockSpec((tm, 512), row),
                  pl.BlockSpec((tm, 512), row),
                  pl.BlockSpec((tm, D_MODEL), row),
                  _mod_spec(tm, seq_rows, mod_row0),
                  pl.BlockSpec((1, D_MODEL), const),
                  pl.BlockSpec((1, D_MODEL), const),
                  pl.BlockSpec((D_MODEL, D_MODEL), const),
                  pl.BlockSpec((D_MODEL, LANES), const),
                  pl.BlockSpec((D_MODEL, LANES), const),
                  pl.BlockSpec((1, LANES), const),
                  pl.BlockSpec((1, LANES), const)],
        out_specs=[pl.BlockSpec((tm, D_MODEL), row),
                   pl.BlockSpec((tm, D_MODEL), row),
                   pl.BlockSpec((tm, LANES), row),
                   pl.BlockSpec((1, LANES), const)],
        out_shape=[jax.ShapeDtypeStruct((n, D_MODEL), F32),
                   jax.ShapeDtypeStruct((n, D_MODEL), F32),
                   jax.ShapeDtypeStruct((n, LANES), F32),
                   jax.ShapeDtypeStruct((1, LANES), F32)],
        compiler_params=_cparams(("arbitrary",)),
        name="post",
    )(oa16, ob16, x2d, mod8, g_post_mix, g_pre_ffn, w_out16, wr_hi, wr_lo, b_r, cnt0)


def _dispatch_kernel(seg_ref, dp_ref, ds_ref, hp_ref, hs_ref, xs_hbm, zbuf, stage, sem, stage_sems, *,
                     n_blocks, n_prompt_steps):
    i = pl.program_id(0)
    last_step = pl.num_programs(0) - 1

    @pl.when(i == 0)
    def _():
        zbuf[...] = jnp.zeros_like(zbuf)

        def zero_block(row0):
            return pltpu.make_async_copy(zbuf, xs_hbm.at[pl.ds(pl.multiple_of(row0, MOE_BLOCK), MOE_BLOCK), :], sem)

        for e in range(N_EXPERTS):
            @pl.when(seg_ref[e] > 0)
            def _():
                zero_block(seg_ref[N_EXPERTS + e] - MOE_BLOCK).start()
        for e in range(N_EXPERTS):
            @pl.when(seg_ref[e] > 0)
            def _():
                zero_block(seg_ref[N_EXPERTS + e] - MOE_BLOCK).wait()

        used = seg_ref[2 * N_EXPERTS - 1] // MOE_BLOCK

        def fill(b, c):
            cp = zero_block(b * MOE_BLOCK)
            cp.start()
            cp.wait()
            return c

        lax.fori_loop(used, n_blocks, fill, 0)

    def scatter(h_ref, dest_ref, sem_):
        for t in range(h_ref.shape[0]):
            for slot in range(2):
                pltpu.make_async_copy(h_ref.at[pl.ds(t, 1), :],
                                      xs_hbm.at[pl.ds(dest_ref[0, 0, 2 * t + slot], 1), :],
                                      sem_).start(priority=slot)

    def drain(h_ref, sem_):
        for slot in range(2):
            pltpu.make_async_copy(h_ref, xs_hbm.at[pl.ds(0, h_ref.shape[0]), :], sem_).wait()

    for buf in range(2):
        @pl.when((i < last_step) & (i % 2 == buf))
        def _():
            @pl.when(i >= 2)
            def _():
                drain(stage.at[buf], stage_sems.at[buf])
            stage[buf] = hp_ref[...]
            scatter(stage.at[buf], dp_ref, stage_sems.at[buf])

    @pl.when(i == last_step)
    def _():
        scatter(hs_ref, ds_ref, sem)
        drain(hs_ref, sem)
        for buf in range(min(2, n_prompt_steps)):
            drain(stage.at[buf], stage_sems.at[buf])


def _dispatch(seg, dest_p, dest_s, h2p, h2s, n_blocks, tm):
    n_p, n_s = h2p.shape[0], h2s.shape[0]
    steps_p = n_p // tm
    pmap3 = lambda i, sg: (jnp.minimum(i, steps_p - 1), 0, 0)
    pmap2 = lambda i, sg: (jnp.minimum(i, steps_p - 1), 0)
    grid_spec = pltpu.PrefetchScalarGridSpec(
        num_scalar_prefetch=1,
        grid=(steps_p + 1,),
        in_specs=[pl.BlockSpec((1, 1, 2 * tm), pmap3, memory_space=pltpu.SMEM),
                  pl.BlockSpec((1, 1, 2 * n_s), lambda i, sg: (0, 0, 0), memory_space=pltpu.SMEM),
                  pl.BlockSpec((tm, D_MODEL), pmap2),
                  pl.BlockSpec((n_s, D_MODEL), lambda i, sg: (0, 0))],
        out_specs=pl.BlockSpec(memory_space=pl.ANY),
        scratch_shapes=[pltpu.VMEM((MOE_BLOCK, D_MODEL), F32), pltpu.VMEM((2, tm, D_MODEL), F32),
                        pltpu.SemaphoreType.DMA(()), pltpu.SemaphoreType.DMA((2,))])
    return pl.pallas_call(
        functools.partial(_dispatch_kernel, n_blocks=n_blocks, n_prompt_steps=steps_p),
        grid_spec=grid_spec,
        out_shape=jax.ShapeDtypeStruct((n_blocks * MOE_BLOCK, D_MODEL), F32),
        compiler_params=_cparams(("arbitrary",)),
        name="dispatch",
    )(seg, dest_p.reshape(steps_p, 1, 2 * tm), dest_s.reshape(1, 1, 2 * n_s), h2p, h2s)


def _moe_kernel(blk_e_ref, nvalid_ref, x_ref, wg_ref, wu_ref, wd_ref, y_ref, wg16, wu16, wd16):
    i = pl.program_id(0)
    e = blk_e_ref[i]
    e_prev = blk_e_ref[jnp.maximum(i - 1, 0)]

    @pl.when((i == 0) | (e != e_prev))
    def _():
        wg16[...] = wg_ref[0].astype(BF16)
        wu16[...] = wu_ref[0].astype(BF16)
        wd16[...] = wd_ref[0].astype(BF16)

    @pl.when(nvalid_ref[i] > 0)
    def _():
        xb = x_ref[...].astype(BF16)
        g = _dot(xb, wg16[...])
        u = _dot(xb, wu16[...])
        hmid = (_silu(g) * u).astype(BF16)
        y_ref[...] = _dot(hmid, wd16[...])

    @pl.when(nvalid_ref[i] == 0)
    def _():
        y_ref[...] = jnp.zeros_like(y_ref)


def _moe(blk_e, nvalid, x_sorted, w_gate, w_up, w_down):
    n_blocks = blk_e.shape[0]
    wmap = lambda i, be, nv: (be[i], 0, 0)
    xmap = lambda i, be, nv: (jnp.where(nv[i] > 0, i, 0), 0)
    grid_spec = pltpu.PrefetchScalarGridSpec(
        num_scalar_prefetch=2,
        grid=(n_blocks,),
        in_specs=[pl.BlockSpec((MOE_BLOCK, D_MODEL), xmap),
                  pl.BlockSpec((1, D_MODEL, D_EXPERT), wmap),
                  pl.BlockSpec((1, D_MODEL, D_EXPERT), wmap),
                  pl.BlockSpec((1, D_EXPERT, D_MODEL), wmap)],
        out_specs=pl.BlockSpec((MOE_BLOCK, D_MODEL), lambda i, be, nv: (i, 0)),
        scratch_shapes=[pltpu.VMEM((D_MODEL, D_EXPERT), BF16),
                        pltpu.VMEM((D_MODEL, D_EXPERT), BF16),
                        pltpu.VMEM((D_EXPERT, D_MODEL), BF16)])
    return pl.pallas_call(
        _moe_kernel,
        grid_spec=grid_spec,
        out_shape=jax.ShapeDtypeStruct((n_blocks * MOE_BLOCK, D_MODEL), F32),
        compiler_params=_cparams(("arbitrary",)),
        name="moe",
    )(blk_e, nvalid, x_sorted, w_gate, w_up, w_down)


def _combine_kernel(dcur_ref, dnext_ref, route_ref, x1_ref, mod_ref, g_ref, y_hbm, o_ref, ybuf, sems):
    i = pl.program_id(0)
    last = pl.num_programs(0) - 1
    tm = x1_ref.shape[0]

    def gather(dest_ref, buf):
        for t in range(tm):
            for slot in range(2):
                pltpu.make_async_copy(y_hbm.at[pl.ds(dest_ref[0, 0, 2 * t + slot], 1), :],
                                      ybuf.at[buf, slot, pl.ds(t, 1), :], sems.at[buf]).start(priority=slot)

    @pl.when(i == 0)
    def _():
        gather(dcur_ref, 0)

    for buf in range(2):
        @pl.when(i % 2 == buf)
        def _():
            @pl.when(i < last)
            def _():
                gather(dnext_ref, 1 - buf)

            for slot in range(2):
                pltpu.make_async_copy(y_hbm.at[pl.ds(0, tm), :], ybuf.at[buf, slot], sems.at[buf]).wait()
            route = route_ref[...]
            moe = ybuf[buf, 0] * route[:, 2:3] + ybuf[buf, 1] * route[:, 3:4]
            o_ref[...] = x1_ref[...] + _per_seq(_rms(moe, g_ref[...]), mod_ref[...], lambda y, m: y * m[:, 5:6])


def _combine(dest, route, y_sorted, x1, mod8, g_post_ffn, tm, seq_rows, mod_row0):
    n = x1.shape[0]
    steps = n // tm
    row = lambda i: (i, 0)
    dest3 = dest.reshape(steps, 1, 2 * tm)
    return pl.pallas_call(
        _combine_kernel,
        grid=(steps,),
        in_specs=[pl.BlockSpec((1, 1, 2 * tm), lambda i: (i, 0, 0), memory_space=pltpu.SMEM),
                  pl.BlockSpec((1, 1, 2 * tm), lambda i: (jnp.minimum(i + 1, steps - 1), 0, 0),
                               memory_space=pltpu.SMEM),
                  pl.BlockSpec((tm, LANES), row),
                  pl.BlockSpec((tm, D_MODEL), row),
                  _mod_spec(tm, seq_rows, mod_row0),
                  pl.BlockSpec((1, D_MODEL), lambda i: (0, 0)),
                  pl.BlockSpec(memory_space=pl.ANY)],
        out_specs=pl.BlockSpec((tm, D_MODEL), row),
        out_shape=jax.ShapeDtypeStruct((n, D_MODEL), F32),
        scratch_shapes=[pltpu.VMEM((2, 2, tm, D_MODEL), F32), pltpu.SemaphoreType.DMA((2,))],
        compiler_params=_cparams(("arbitrary",)),
        name="combine",
    )(dest3, dest3, route, x1, mod8, g_post_ffn, y_sorted)


def _segment_plan(counts_f, n_blocks):
    counts = counts_f[0, :N_EXPERTS].astype(jnp.int32)
    padded = (counts + MOE_BLOCK - 1) // MOE_BLOCK * MOE_BLOCK
    pad_end = jnp.cumsum(padded)
    pad_start = pad_end - padded
    blk_start = jnp.arange(n_blocks, dtype=jnp.int32) * MOE_BLOCK
    blk_e = jnp.minimum(jnp.sum((pad_end[None, :] <= blk_start[:, None]).astype(jnp.int32), axis=1),
                        N_EXPERTS - 1)
    onehot = blk_e[:, None] == jnp.arange(N_EXPERTS, dtype=jnp.int32)[None, :]
    c_blk = jnp.sum(jnp.where(onehot, counts[None, :], 0), axis=1)
    s_blk = jnp.sum(jnp.where(onehot, pad_start[None, :], 0), axis=1)
    nvalid = jnp.clip(c_blk - (blk_start - s_blk), 0, MOE_BLOCK).astype(jnp.int32)
    seg = jnp.concatenate([counts, pad_end]).astype(jnp.int32)
    return blk_e.astype(jnp.int32), nvalid, seg, pad_start


def _token_rows(route, pad_start):
    eid = route[:, 0:2].astype(jnp.int32)
    rank = route[:, 4:6].astype(jnp.int32)
    onehot = eid[:, :, None] == jnp.arange(N_EXPERTS, dtype=jnp.int32)[None, None, :]
    return rank + jnp.sum(jnp.where(onehot, pad_start[None, None, :], 0), axis=2)


def _layer(x_p, x_s, c_p, c_s, k_past, v_past, s0_s, conv_s, p):
    bp, tp, d = x_p.shape
    bs, ts, _ = x_s.shape
    n_p, n_s = bp * tp, bs * ts
    n_tok = n_p + n_s

    n_seq = bp + bs
    c_all = jnp.zeros((16, d), F32).at[:n_seq].set(jnp.concatenate([c_s, c_p], axis=0))
    mod = _ada(c_all, p['w_ada'], p['b_ada'])
    mod8 = jnp.pad(mod.reshape(16, 6, d), ((0, 0), (0, 2), (0, 0)))

    w_in = p['w_in']
    o_z, o_b, o_q = CONV_CH, CONV_CH + 512, CONV_CH + 512 + 2 * DN_HEADS
    wb = jnp.pad(w_in[:, o_b:o_q], ((0, 0), (0, LANES - 2 * DN_HEADS)))
    wb_hi = wb.astype(BF16)
    wb_lo = (wb - wb_hi.astype(F32)).astype(BF16)
    w_main = jnp.concatenate([w_in[:, :o_b].astype(BF16), w_in[:, o_q:].astype(BF16), wb_hi, wb_lo], axis=1)
    g_pre_mix = p['g_pre_mix'].reshape(1, d)

    conv_w8 = jnp.pad(p['conv_w'], ((0, 8 - CONV_W), (0, 0)))
    pad_g = lambda a: jnp.pad(a.reshape(1, DN_HEADS), ((0, 0), (DN_HEADS, LANES - 2 * DN_HEADS)))
    alog_row, dtb_row = pad_g(p['a_log']), pad_g(p['dt_bias'])
    onorm_a = p['onorm_a'].reshape(1, DN_DIM)
    onorm_b2 = jnp.tile(p['onorm_b'].reshape(1, SB_DIM), (1, 2))

    w_out16 = p['w_out'].astype(BF16)
    wr = jnp.pad(jnp.concatenate([p['w_router_group'], p['w_router_expert']], axis=1),
                 ((0, 0), (0, LANES - N_GROUPS - N_EXPERTS)))
    wr_hi = wr.astype(BF16)
    wr_lo = (wr - wr_hi.astype(F32)).astype(BF16)
    b_r = jnp.pad(jnp.concatenate([p['b_router_group'], p['b_router_expert']]).reshape(1, -1),
                  ((0, 0), (0, LANES - N_GROUPS - N_EXPERTS)))
    g_post_mix = p['g_post_mix'].reshape(1, d)
    g_pre_ffn = p['g_pre_ffn'].reshape(1, d)
    g_post_ffn = p['g_post_ffn'].reshape(1, d)

    def mixer(x, tm, seq_rows, mod_row0, hist8, s0, k_old, v_old, chunk, n_chunks, bq, n_sub, cnt0):
        b, t, _ = x.shape
        x2d = x.reshape(b * t, d)
        a_in, z, bg, q16, kb, vb, k16, v16 = _proj(x2d, mod8, g_pre_mix, w_main, wb_hi, tm, seq_rows, mod_row0)
        r3 = lambda a: a.reshape(b, t, a.shape[-1])
        oa16, s_new = _delta(r3(a_in), r3(z), r3(bg), hist8, s0, conv_w8, alog_row, dtb_row, onorm_a,
                             chunk, n_chunks)
        k16, v16 = r3(k16), r3(v16)
        if k_old is not None:
            k16 = jnp.concatenate([k_old.reshape(b, -1, SB_WIDTH).astype(BF16), k16], axis=1)
            v16 = jnp.concatenate([v_old.reshape(b, -1, SB_WIDTH).astype(BF16), v16], axis=1)
        n_pad = (-k16.shape[1]) % KEY_TILE
        k16 = jnp.pad(k16, ((0, 0), (n_pad, 0), (0, 0)))
        v16 = jnp.pad(v16, ((0, 0), (n_pad, 0), (0, 0)))
        ob16 = _sb_attn(r3(q16), k16, v16, onorm_b2, bq, n_sub, n_pad)
        x1, h2, route, cnt = _post(oa16.reshape(b * t, 512), ob16.reshape(b * t, 512), x2d, mod8,
                                   g_post_mix, g_pre_ffn, w_out16, wr_hi, wr_lo, b_r, cnt0, tm, seq_rows,
                                   mod_row0)
        new_conv = r3(a_in)[:, t - (CONV_W - 1):, :]
        return (x1, h2, route, cnt, kb.reshape(b, t, SB_HEADS, SB_DIM), vb.reshape(b, t, SB_HEADS, SB_DIM),
                s_new, new_conv)

    zero_hist = jnp.zeros((bp, 8, CONV_CH), F32)
    zero_s = jnp.zeros((bp, DN_HEADS, DN_DIM, DN_DIM), F32)
    hist_s = jnp.pad(conv_s, ((0, 0), (8 - (CONV_W - 1), 0), (0, 0)))
    tm_p = min(ROW_TILE, tp)
    tm_dense = DENSE_TILE if n_p % DENSE_TILE == 0 else tm_p
    nc_p = max(1, min(8, tp // DELTA_BLOCK))
    x1p, h2p, rp, cnt_p, kp, vp, sp, cp = mixer(x_p, tm_dense, tp, bs, zero_hist, zero_s, None, None,
                                                 min(DELTA_BLOCK, tp), nc_p, min(KEY_TILE // 2, tp), 2,
                                                 jnp.zeros((1, LANES), F32))
    x1s, h2s, rs, cnt, ks, vs, ss, cs = mixer(x_s, n_s, ts, 0, hist_s, s0_s, k_past, v_past,
                                               min(DELTA_BLOCK, ts), max(1, ts // DELTA_BLOCK),
                                               min(KEY_TILE, ts), 1, cnt_p)

    n_blocks = -(-2 * n_tok // MOE_BLOCK) + N_EXPERTS
    blk_e, nvalid, seg, pad_start = _segment_plan(cnt, n_blocks)
    dest_p = _token_rows(rp, pad_start)
    dest_s = _token_rows(rs, pad_start)
    x_sorted = _dispatch(seg, dest_p, dest_s, h2p, h2s, n_blocks, tm_p)
    y_sorted = _moe(blk_e, nvalid, x_sorted, p['w_gate'], p['w_up'], p['w_down'])
    y_p = _combine(dest_p, rp, y_sorted, x1p, mod8, g_post_ffn, tm_p, tp, bs).reshape(bp, tp, d)
    y_s = _combine(dest_s, rs, y_sorted, x1s, mod8, g_post_ffn, n_s, ts, 0).reshape(bs, ts, d)
    return y_p, y_s, kp, vp, sp, cp, ks, vs, ss, cs


def kernel(x_prompt, x_sample, c_prompt, c_sample, cache_k, cache_v, state_delta, state_conv, w_ada, b_ada, g_pre_mix, g_post_mix, g_pre_ffn, g_post_ffn, w_in, conv_w, a_log, dt_bias, onorm_a, onorm_b, w_out, w_router_group, b_router_group, w_router_expert, b_router_expert, w_gate, w_up, w_down):
    depth = w_in.shape[0]
    y_p, y_s = x_prompt, x_sample
    outs = [[] for _ in range(8)]
    for l in range(depth):
        p = dict(w_ada=w_ada[l], b_ada=b_ada[l], g_pre_mix=g_pre_mix[l], g_post_mix=g_post_mix[l],
                 g_pre_ffn=g_pre_ffn[l], g_post_ffn=g_post_ffn[l], w_in=w_in[l], conv_w=conv_w[l],
                 a_log=a_log[l], dt_bias=dt_bias[l], onorm_a=onorm_a[l], onorm_b=onorm_b[l],
                 w_out=w_out[l], w_router_group=w_router_group[l], b_router_group=b_router_group[l],
                 w_router_expert=w_router_expert[l], b_router_expert=b_router_expert[l],
                 w_gate=w_gate[l], w_up=w_up[l], w_down=w_down[l])
        res = _layer(y_p, y_s, c_prompt, c_sample, cache_k[l], cache_v[l], state_delta[l], state_conv[l], p)
        y_p, y_s = res[0], res[1]
        for lst, r in zip(outs, res[2:]):
            lst.append(r)
    return (y_p, y_s) + tuple(jnp.stack(o) for o in outs)
```

```python
import functools
import math

import jax
import jax.numpy as jnp
from jax import lax
from jax.experimental import pallas as pl
from jax.experimental.pallas import tpu as pltpu

F32 = jnp.float32
BF16 = jnp.bfloat16

D_MODEL = 1024
DN_HEADS = 4
DN_DIM = 128
CONV_W = 4
CONV_CH = DN_HEADS * 3 * DN_DIM
DELTA_BLOCK = 64
SB_HEADS = 8
SB_DIM = 64
SB_WIDTH = SB_HEADS * SB_DIM
N_GROUPS = 4
EXPERTS_PER_GROUP = 8
N_EXPERTS = N_GROUPS * EXPERTS_PER_GROUP
D_EXPERT = D_MODEL // 2
MOE_BLOCK = 256
EPS = 1e-6

LANES = 128
KEY_TILE = 128
ATTN_WINDOW = 3 * KEY_TILE
WINDOW_BLOCKS = 4
EXP_ZERO_BELOW = -104.0
VMEM_LIMIT = 56 * 1024 * 1024
DENSE_TILE = 512
ROUTER_TILE = 1024
ROW_TILE = 256


def _cparams(sem):
    return pltpu.CompilerParams(dimension_semantics=sem, vmem_limit_bytes=VMEM_LIMIT)


def _split(a):
    hi = a.astype(BF16)
    lo = (a - hi.astype(F32)).astype(BF16)
    return hi, lo


def _dot(a, b, dims=(((1,), (0,)), ((), ()))):
    return lax.dot_general(a, b, dims, preferred_element_type=F32)


def _dot_small_int_lhs(a, b):
    a16 = a.astype(BF16)
    return _dot(jnp.concatenate([a16, a16], axis=1), jnp.concatenate(_split(b), axis=0))


_NT = (((1,), (1,)), ((), ()))
_TN = (((0,), (0,)), ((), ()))


def _silu(x):
    return x * jax.nn.sigmoid(x)


SOFTPLUS_LINEAR_ABOVE = 80.0


def _softplus(x):
    return jnp.where(x > SOFTPLUS_LINEAR_ABOVE, x, jnp.log(1.0 + jnp.exp(jnp.minimum(x, SOFTPLUS_LINEAR_ABOVE))))


def _ada_kernel(c_ref, w_ref, b_ref, o_ref):
    s = _silu(c_ref[...]).astype(BF16)
    o_ref[...] = _dot(s, w_ref[...].astype(BF16)) + b_ref[...]


def _ada(c_all, w_ada, b_ada):
    rows = c_all.shape[0]
    n = w_ada.shape[1]
    tn = 1024
    return pl.pallas_call(
        _ada_kernel,
        grid=(n // tn,),
        in_specs=[pl.BlockSpec((rows, D_MODEL), lambda j: (0, 0)),
                  pl.BlockSpec((D_MODEL, tn), lambda j: (0, j)),
                  pl.BlockSpec((1, tn), lambda j: (0, j))],
        out_specs=pl.BlockSpec((rows, tn), lambda j: (0, j)),
        out_shape=jax.ShapeDtypeStruct((rows, n), F32),
        compiler_params=_cparams(("arbitrary",)),
        name="ada",
    )(c_all, w_ada, b_ada.reshape(1, n))


def _rms(x, gain):
    return x * lax.rsqrt(jnp.mean(x * x, axis=-1, keepdims=True) + EPS) * gain


def _per_seq(y, mod, fn):
    n_seq = mod.shape[0]
    ys = y.reshape(n_seq, y.shape[0] // n_seq, y.shape[1])
    return fn(ys, mod).reshape(y.shape)


def _mod_spec(tm, seq_rows, mod_row0):
    if tm >= seq_rows:
        n_seq = tm // seq_rows
        assert tm % seq_rows == 0 and mod_row0 % n_seq == 0
        return pl.BlockSpec((n_seq, 8, D_MODEL), lambda i: (mod_row0 // n_seq + i, 0, 0))
    assert seq_rows % tm == 0
    return pl.BlockSpec((1, 8, D_MODEL), lambda i: (mod_row0 + (i * tm) // seq_rows, 0, 0))


def _proj_kernel(x_ref, mod_ref, g_ref, wm_ref, wbh_ref,
                 a_ref, z_ref, bg_ref, q_ref, k_ref, v_ref, k16_ref, v16_ref):
    h = _per_seq(_rms(x_ref[...], g_ref[...]), mod_ref[...], lambda y, m: y * (1.0 + m[:, 1:2]) + m[:, 0:1])
    hh, hl = _split(h)
    p = _dot(hh, wm_ref[...])
    a_ref[...] = p[:, 0:CONV_CH]
    z_ref[...] = p[:, CONV_CH:CONV_CH + 512]
    o = CONV_CH + 512
    q_ref[...] = (p[:, o:o + 512] * (SB_DIM ** -0.5)).astype(BF16)
    k = p[:, o + 512:o + 1024]
    v = p[:, o + 1024:o + 1536]
    k_ref[...] = k
    v_ref[...] = v
    k16_ref[...] = k.astype(BF16)
    v16_ref[...] = v.astype(BF16)
    bg_ref[...] = p[:, o + 1536:o + 1536 + LANES] + (_dot(hl, wbh_ref[...]) + p[:, o + 1536 + LANES:])


def _proj(x2d, mod8, g_pre, w_main, wb_hi, tm, seq_rows, mod_row0):
    n = x2d.shape[0]
    nm = w_main.shape[1]
    row = lambda i: (i, 0)
    const = lambda i: (0, 0)
    outs = [(CONV_CH, F32), (512, F32), (LANES, F32), (512, BF16), (512, F32), (512, F32),
            (512, BF16), (512, BF16)]
    return pl.pallas_call(
        _proj_kernel,
        grid=(n // tm,),
        in_specs=[pl.BlockSpec((tm, D_MODEL), row),
                  _mod_spec(tm, seq_rows, mod_row0),
                  pl.BlockSpec((1, D_MODEL), const),
                  pl.BlockSpec((D_MODEL, nm), const),
                  pl.BlockSpec((D_MODEL, LANES), const)],
        out_specs=[pl.BlockSpec((tm, w), row) for w, _ in outs],
        out_shape=[jax.ShapeDtypeStruct((n, w), dt) for w, dt in outs],
        compiler_params=_cparams(("arbitrary",)),
        name="proj",
    )(x2d, mod8, g_pre, w_main, wb_hi)


def _delta_kernel(a_ref, z_ref, bg_ref, hist0_ref, s0_ref, cw_ref, alog_ref, dtb_ref, on_ref,
                  o_ref, sfin_ref, hist_sc, s_sc, *, chunk, n_chunks):
    t_idx = pl.program_id(1)
    tt = chunk * n_chunks

    @pl.when(t_idx == 0)
    def _():
        hist_sc[...] = hist0_ref[0]
        s_sc[...] = s0_ref[0]

    x = a_ref[0]
    xx = jnp.concatenate([hist_sc[...], x], axis=0)
    cw = cw_ref[...]
    conv = x * cw[CONV_W - 1:CONV_W]
    for s in range(1, CONV_W):
        conv = conv + pltpu.roll(xx, s, 0)[8:] * cw[CONV_W - 1 - s:CONV_W - s]
    conv = _silu(conv)
    hist_sc[...] = x[tt - 8:tt]

    bg = bg_ref[0]
    lane = lax.broadcasted_iota(jnp.int32, (1, LANES), 1)
    g_lane = (lane >= DN_HEADS) & (lane < 2 * DN_HEADS)
    neg_a = jnp.where(g_lane, -jnp.exp(alog_ref[...]), 0.0)
    beta_all = jax.nn.sigmoid(bg)
    g_all = neg_a * _softplus(bg + dtb_ref[...])

    ri = lax.broadcasted_iota(jnp.int32, (chunk, chunk), 0)
    ci = lax.broadcasted_iota(jnp.int32, (chunk, chunk), 1)
    incl = ri >= ci
    strict = ri > ci
    tri = incl.astype(F32)
    eye = (ri == ci).astype(F32)
    ones_cc = jnp.ones((chunk, chunk), F32)
    onorm = on_ref[...]

    chunks = range(n_chunks)
    units = [(c, h) for c in chunks for h in range(DN_HEADS)]
    rows = lambda c: slice(c * chunk, (c + 1) * chunk)

    gc_all = [_dot_small_int_lhs(tri, g_all[rows(c)]) for c in chunks]
    diag = [jnp.concatenate([eye * gc_all[c][:, DN_HEADS + h:DN_HEADS + h + 1] for h in range(DN_HEADS)], axis=1)
            for c in chunks]
    gc_rows = [_dot_small_int_lhs(ones_cc, diag[c]) for c in chunks]

    q, k, v, beta, gcol, glast, eg, decay, kbeta = {}, {}, {}, {}, {}, {}, {}, {}, {}
    for (c, h) in units:
        lo = h * DN_DIM
        qq = conv[rows(c), lo:lo + DN_DIM]
        kk = conv[rows(c), 512 + lo:512 + lo + DN_DIM]
        u_ = (c, h)
        q[u_] = qq * lax.rsqrt(jnp.sum(qq * qq, axis=-1, keepdims=True) + EPS) * (DN_DIM ** -0.5)
        k[u_] = kk * lax.rsqrt(jnp.sum(kk * kk, axis=-1, keepdims=True) + EPS)
        v[u_] = conv[rows(c), 1024 + lo:1024 + lo + DN_DIM]
        beta[u_] = beta_all[rows(c), h:h + 1]
        gcol[u_] = gc_all[c][:, DN_HEADS + h:DN_HEADS + h + 1]
        glast[u_] = gcol[u_][chunk - 1:chunk, :]
        eg[u_] = jnp.exp(gcol[u_])
        grow = gc_rows[c][:, h * chunk:(h + 1) * chunk]
        decay[u_] = jnp.where(incl, jnp.exp(jnp.minimum(gcol[u_] - grow, 0.0)), 0.0)
        kbeta[u_] = k[u_] * beta[u_]

    k16 = {u_: k[u_].astype(BF16) for u_ in units}
    kq = {u_: _dot(jnp.concatenate([kbeta[u_], q[u_]], axis=0).astype(BF16), k16[u_], _NT) for u_ in units}
    lower = {u_: jnp.where(strict, kq[u_][:chunk] * decay[u_], 0.0) for u_ in units}
    attn16 = {u_: jnp.where(incl, kq[u_][chunk:] * decay[u_], 0.0).astype(BF16) for u_ in units}

    sol = {u_: jnp.concatenate([v[u_] * beta[u_], kbeta[u_] * eg[u_]], axis=1) for u_ in units}
    lp = lower
    p = 1
    while p < chunk:
        lsp = {u_: _split(lp[u_]) for u_ in units}
        ssp = {u_: _split(sol[u_]) for u_ in units}
        lcat = {u_: jnp.concatenate([lsp[u_][0], lsp[u_][1], lsp[u_][0]], axis=1) for u_ in units}
        upd = {u_: _dot(lcat[u_], jnp.concatenate([ssp[u_][0], ssp[u_][0], ssp[u_][1]], axis=0)) for u_ in units}
        sol = {u_: (sol[u_] - upd[u_]) if p == 1 else (sol[u_] + upd[u_]) for u_ in units}
        if 2 * p < chunk:
            lp = {u_: _dot(lcat[u_], jnp.concatenate([lsp[u_][0], lsp[u_][0], lsp[u_][1]], axis=0))
                  for u_ in units}
        p *= 2

    usol = {u_: sol[u_][:, :DN_DIM] for u_ in units}
    wq16 = {u_: jnp.concatenate([sol[u_][:, DN_DIM:], q[u_] * eg[u_]], axis=0).astype(BF16) for u_ in units}
    kd16 = {u_: (k[u_] * jnp.exp(glast[u_] - gcol[u_])).astype(BF16) for u_ in units}

    for c in chunks:
        hs = range(DN_HEADS)
        s_old = [s_sc[h] for h in hs]
        ws = [_dot(wq16[(c, h)], s_old[h].astype(BF16)) for h in hs]
        v_new = [(usol[(c, h)] - ws[h][:chunk]).astype(BF16) for h in hs]
        o_in = [_dot(attn16[(c, h)], v_new[h]) for h in hs]
        ds = [_dot(kd16[(c, h)], v_new[h], _TN) for h in hs]
        for h in hs:
            lo = h * DN_DIM
            s_sc[h] = s_old[h] * jnp.exp(glast[(c, h)]) + ds[h]
            o = ws[h][chunk:] + o_in[h]
            zg = z_ref[0, rows(c), lo:lo + DN_DIM]
            o_ref[0, rows(c), lo:lo + DN_DIM] = (_rms(o, onorm) * _silu(zg)).astype(BF16)

    @pl.when(t_idx == pl.num_programs(1) - 1)
    def _():
        sfin_ref[0] = s_sc[...]


def _delta(a_in, z, bg, hist8, s0, conv_w8, alog_row, dtb_row, onorm_a, chunk, n_chunks):
    b, t, _ = a_in.shape
    tt = chunk * n_chunks
    tile = lambda bi, ti: (bi, ti, 0)
    per_b3 = lambda bi, ti: (bi, 0, 0)
    per_b4 = lambda bi, ti: (bi, 0, 0, 0)
    const = lambda bi, ti: (0, 0)
    kern = functools.partial(_delta_kernel, chunk=chunk, n_chunks=n_chunks)
    return pl.pallas_call(
        kern,
        grid=(b, t // tt),
        in_specs=[pl.BlockSpec((1, tt, CONV_CH), tile),
                  pl.BlockSpec((1, tt, 512), tile),
                  pl.BlockSpec((1, tt, LANES), tile),
                  pl.BlockSpec((1, 8, CONV_CH), per_b3),
                  pl.BlockSpec((1, DN_HEADS, DN_DIM, DN_DIM), per_b4),
                  pl.BlockSpec((8, CONV_CH), const),
                  pl.BlockSpec((1, LANES), const),
                  pl.BlockSpec((1, LANES), const),
                  pl.BlockSpec((1, DN_DIM), const)],
        out_specs=[pl.BlockSpec((1, tt, 512), tile),
                   pl.BlockSpec((1, DN_HEADS, DN_DIM, DN_DIM), per_b4)],
        out_shape=[jax.ShapeDtypeStruct((b, t, 512), BF16),
                   jax.ShapeDtypeStruct((b, DN_HEADS, DN_DIM, DN_DIM), F32)],
        scratch_shapes=[pltpu.VMEM((8, CONV_CH), F32),
                        pltpu.VMEM((DN_HEADS, DN_DIM, DN_DIM), F32)],
        compiler_params=_cparams(("arbitrary", "arbitrary")),
        name="delta",
    )(a_in, z, bg, hist8, s0, conv_w8, alog_row, dtb_row, onorm_a)


def _sb_kernel(q_ref, *refs, bq, n_sub, n_pad, q_off):
    kwin = refs[:WINDOW_BLOCKS]
    vwin = refs[WINDOW_BLOCKS:2 * WINDOW_BLOCKS]
    k_hbm, v_hbm, on_ref, o_ref, kbuf, vbuf, qsel, acc, carry, sem = refs[2 * WINDOW_BLOCKS:]
    b = pl.program_id(0)
    i = pl.program_id(1)
    g = bq * n_sub
    qend_step = q_off + (i + 1) * g
    win_start = qend_step - WINDOW_BLOCKS * KEY_TILE
    n_pairs = SB_HEADS // 2
    heads = range(SB_HEADS)

    half_lane = lax.broadcasted_iota(jnp.int32, (bq, LANES), 1) < SB_DIM
    rj = lax.broadcasted_iota(jnp.int32, (2 * KEY_TILE, 2 * KEY_TILE), 0) % KEY_TILE
    cj = lax.broadcasted_iota(jnp.int32, (2 * KEY_TILE, 2 * KEY_TILE), 1)
    suffix2 = ((rj > cj) | (cj >= KEY_TILE)).astype(BF16)

    def window(blocks, off, p):
        parts = []
        for blk in range(WINDOW_BLOCKS):
            lo, hi = max(off, blk * KEY_TILE), min(off + ATTN_WINDOW, (blk + 1) * KEY_TILE)
            if lo < hi:
                ref = blocks[WINDOW_BLOCKS - 1 - blk]
                parts.append(ref[0, lo - blk * KEY_TILE:hi - blk * KEY_TILE, p * LANES:(p + 1) * LANES])
        return jnp.concatenate(parts, axis=0)

    n_t = ATTN_WINDOW // KEY_TILE
    row = lax.broadcasted_iota(jnp.int32, (bq, ATTN_WINDOW), 0)
    col = lax.broadcasted_iota(jnp.int32, (bq, ATTN_WINDOW), 1)
    causal = col < row + (ATTN_WINDOW - bq)
    subs = range(n_sub)
    offs = [WINDOW_BLOCKS * KEY_TILE - g + (s + 1) * bq - ATTN_WINDOW for s in subs]
    mask = [causal & (col >= n_pad - (win_start + offs[s])) for s in subs]
    zs, sp, cs_all = {}, {}, {}
    for s in subs:
        for p in range(n_pairs):
            qf = q_ref[0, s * bq:(s + 1) * bq, p * LANES:(p + 1) * LANES].astype(F32)
            qq = jnp.concatenate([jnp.where(half_lane, qf, 0.0), jnp.where(half_lane, 0.0, qf)], axis=0)
            zz = _dot(qq.astype(BF16), window(kwin, offs[s], p), _NT)
            zs[(s, 2 * p)], zs[(s, 2 * p + 1)] = zz[:bq], zz[bq:]
    for s in subs:
        pieces = []
        for h in heads:
            sp[(s, h)] = _softplus(zs[(s, h)])
            hi, lo = _split(jnp.where(mask[s], -sp[(s, h)], 0.0))
            for j in range(n_t):
                c0 = ATTN_WINDOW - (j + 1) * KEY_TILE
                pieces.append(jnp.concatenate([hi[:, c0:c0 + KEY_TILE], lo[:, c0:c0 + KEY_TILE]], axis=1))
        cs_all[s] = _dot(jnp.concatenate(pieces, axis=0), suffix2)
    done = []
    for s in subs:
        worst = jnp.full((bq, KEY_TILE), -jnp.inf, F32)
        a = {}
        for h in heads:
            run = None
            cols = []
            for j in range(n_t):
                r0 = (h * n_t + j) * bq
                cs = cs_all[s][r0:r0 + bq]
                cols.append(cs[:, :KEY_TILE] if run is None else cs[:, :KEY_TILE] + run)
                run = cs[:, KEY_TILE:] if run is None else run + cs[:, KEY_TILE:]
            within = jnp.concatenate(cols[::-1], axis=1)
            a[h] = jnp.where(mask[s], jnp.exp((zs[(s, h)] - sp[(s, h)]) + within), 0.0).astype(BF16)
            carry[s, h] = run
            worst = jnp.maximum(worst, run)
        for p in range(n_pairs):
            pv = _dot(jnp.concatenate([a[2 * p], a[2 * p + 1]], axis=0), window(vwin, offs[s], p))
            acc[s, p] = jnp.where(half_lane, pv[:bq], pv[bq:])
        done.append((jnp.max(worst) < EXP_ZERO_BELOW).astype(jnp.int32))

    rj1 = lax.broadcasted_iota(jnp.int32, (KEY_TILE, 2 * KEY_TILE), 0)
    cj1 = lax.broadcasted_iota(jnp.int32, (KEY_TILE, 2 * KEY_TILE), 1)
    suffix1 = ((rj1 > cj1) | (cj1 >= KEY_TILE)).astype(BF16)
    col1 = lax.broadcasted_iota(jnp.int32, (bq, KEY_TILE), 1)
    for s in subs:
        swept_from = win_start + offs[s]

        @pl.when((done[s] == 0) & (swept_from > n_pad))
        def _():
            for p in range(n_pairs):
                qf = q_ref[0, s * bq:(s + 1) * bq, p * LANES:(p + 1) * LANES].astype(F32)
                qsel[2 * p] = jnp.where(half_lane, qf, 0.0).astype(BF16)
                qsel[2 * p + 1] = jnp.where(half_lane, 0.0, qf).astype(BF16)

            def body(state):
                upper, _ = state
                start = jnp.maximum(upper - KEY_TILE, 0)
                copies = []
                for p in range(n_pairs):
                    for src, dst in ((k_hbm, kbuf), (v_hbm, vbuf)):
                        cp = pltpu.make_async_copy(
                            src.at[b, pl.ds(pl.multiple_of(start, 16), KEY_TILE), pl.ds(p * LANES, LANES)],
                            dst.at[p], sem)
                        cp.start()
                        copies.append(cp)
                for cp in copies:
                    cp.wait()
                kpos = start + col1
                m1 = (kpos < upper) & (kpos >= n_pad)
                zz = [_dot(qsel[h], kbuf[h // 2], _NT) for h in heads]
                spp = [_softplus(zz[h]) for h in heads]
                pcs = []
                for h in heads:
                    pcs.extend(_split(jnp.where(m1, -spp[h], 0.0)))
                cs1 = _dot(jnp.concatenate(pcs, axis=0), suffix1)
                worst = jnp.full((bq, KEY_TILE), -jnp.inf, F32)
                aa = []
                for h in heads:
                    c = cs1[2 * h * bq:(2 * h + 1) * bq] + cs1[(2 * h + 1) * bq:(2 * h + 2) * bq]
                    c_old = carry[s, h]
                    aa.append(jnp.where(m1, jnp.exp((zz[h] - spp[h]) + c[:, :KEY_TILE] + c_old), 0.0).astype(BF16))
                    c_new = c_old + c[:, KEY_TILE:]
                    carry[s, h] = c_new
                    worst = jnp.maximum(worst, c_new)
                for p in range(n_pairs):
                    pv0 = _dot(aa[2 * p], vbuf[p])
                    pv1 = _dot(aa[2 * p + 1], vbuf[p])
                    acc[s, p] = acc[s, p] + jnp.where(half_lane, pv0, pv1)
                return start, (jnp.max(worst) < EXP_ZERO_BELOW).astype(jnp.int32)

            lax.while_loop(lambda st: (st[0] > n_pad) & (st[1] == 0), body, (swept_from, jnp.int32(0)))

    onb = on_ref[...]
    for s in subs:
        for p in range(n_pairs):
            o = acc[s, p]
            sq = o * o
            s_lo = jnp.sum(jnp.where(half_lane, sq, 0.0), axis=-1, keepdims=True)
            s_hi = jnp.sum(jnp.where(half_lane, 0.0, sq), axis=-1, keepdims=True)
            ms = jnp.where(half_lane, s_lo, s_hi) * (1.0 / SB_DIM)
            o_ref[0, s * bq:(s + 1) * bq, p * LANES:(p + 1) * LANES] = (o * lax.rsqrt(ms + EPS) * onb).astype(BF16)


def _sb_attn(q16, k16p, v16p, onorm_b2, bq, n_sub, n_pad):
    b, tq, _ = q16.shape
    tkp = k16p.shape[1]
    q_off = tkp - tq
    g = bq * n_sub
    assert tq % g == 0 and g <= KEY_TILE and bq % 16 == 0
    assert all((q_off + (i + 1) * g) % KEY_TILE == 0 for i in range(tq // g))

    def kmap(back):
        def f(bi, i):
            last = (q_off + (i + 1) * g) // KEY_TILE - 1
            return (bi, jnp.maximum(last - back, 0), 0)
        return f

    qmap = lambda bi, i: (bi, i, 0)
    kern = functools.partial(_sb_kernel, bq=bq, n_sub=n_sub, n_pad=n_pad, q_off=q_off)
    kspec = [pl.BlockSpec((1, KEY_TILE, SB_WIDTH), kmap(back)) for back in range(WINDOW_BLOCKS)]
    return pl.pallas_call(
        kern,
        grid=(b, tq // g),
        in_specs=[pl.BlockSpec((1, g, SB_WIDTH), qmap)] + kspec + kspec
                 + [pl.BlockSpec(memory_space=pl.ANY), pl.BlockSpec(memory_space=pl.ANY),
                    pl.BlockSpec((1, LANES), lambda bi, i: (0, 0))],
        out_specs=pl.BlockSpec((1, g, SB_WIDTH), qmap),
        out_shape=jax.ShapeDtypeStruct((b, tq, SB_WIDTH), BF16),
        scratch_shapes=[pltpu.VMEM((SB_HEADS // 2, KEY_TILE, LANES), BF16),
                        pltpu.VMEM((SB_HEADS // 2, KEY_TILE, LANES), BF16),
                        pltpu.VMEM((SB_HEADS, bq, LANES), BF16),
                        pltpu.VMEM((n_sub, SB_HEADS // 2, bq, LANES), F32),
                        pltpu.VMEM((n_sub, SB_HEADS, bq, KEY_TILE), F32),
                        pltpu.SemaphoreType.DMA(())],
        compiler_params=_cparams(("arbitrary", "arbitrary")),
        name="sb_attn",
    )(q16, *([k16p] * WINDOW_BLOCKS), *([v16p] * WINDOW_BLOCKS), k16p, v16p, onorm_b2)


def _post_kernel(oa_ref, ob_ref, x_ref, mod_ref, gpm_ref, gpf_ref, wo_ref, wrh_ref, wrl_ref, br_ref, cnt0_ref,
                 x1_ref, h2_ref, route_ref, cnt_ref):
    @pl.when(pl.program_id(0) == 0)
    def _():
        cnt_ref[...] = cnt0_ref[...]

    mod = mod_ref[...]
    mix = _dot(oa_ref[...], wo_ref[0:512, :]) + _dot(ob_ref[...], wo_ref[512:1024, :])
    x1 = x_ref[...] + _per_seq(_rms(mix, gpm_ref[...]), mod, lambda y, m: y * m[:, 2:3])
    x1_ref[...] = x1
    h2 = _per_seq(_rms(x1, gpf_ref[...]), mod, lambda y, m: y * (1.0 + m[:, 4:5]) + m[:, 3:4])
    h2_ref[...] = h2
    hh, hl = _split(h2)
    wrh = wrh_ref[...]
    logits = _dot(hh, wrh) + (_dot(hl, wrh) + _dot(hh, wrl_ref[...])) + br_ref[...]
    lane = lax.broadcasted_iota(jnp.int32, logits.shape, 1).astype(F32)
    neg = -jnp.inf
    nl = float(LANES)
    lg = jnp.where(lane < N_GROUPS, logits, neg)
    gmax = jnp.max(lg, axis=-1, keepdims=True)
    grp = jnp.min(jnp.where(lg == gmax, lane, nl), axis=-1, keepdims=True)
    p_grp = 1.0 / jnp.sum(jnp.exp(lg - gmax), axis=-1, keepdims=True)
    first = N_GROUPS + grp * EXPERTS_PER_GROUP
    le = jnp.where((lane >= first) & (lane < first + EXPERTS_PER_GROUP), logits, neg)
    emax = jnp.max(le, axis=-1, keepdims=True)
    i1 = jnp.min(jnp.where(le == emax, lane, nl), axis=-1, keepdims=True)
    esum = jnp.sum(jnp.exp(le - emax), axis=-1, keepdims=True)
    le2 = jnp.where(lane == i1, neg, le)
    e2max = jnp.max(le2, axis=-1, keepdims=True)
    i2 = jnp.min(jnp.where(le2 == e2max, lane, nl), axis=-1, keepdims=True)
    p1 = 1.0 / esum
    p2 = jnp.exp(e2max - emax) / esum
    w1 = p_grp * p1 / (p1 + p2)
    w2 = p_grp * p2 / (p1 + p2)
    e1 = i1 - N_GROUPS
    e2 = i2 - N_GROUPS
    hot1 = (lane == e1).astype(F32)
    hot2 = (lane == e2).astype(F32)
    both = hot1 + hot2
    tm = logits.shape[0]
    ti = lax.broadcasted_iota(jnp.int32, (tm, tm), 0)
    tj = lax.broadcasted_iota(jnp.int32, (tm, tm), 1)
    earlier = _dot((ti > tj).astype(BF16), both.astype(BF16)) + cnt_ref[...]
    rank1 = jnp.sum(hot1 * earlier, axis=-1, keepdims=True)
    rank2 = jnp.sum(hot2 * (earlier + hot1), axis=-1, keepdims=True)
    cnt_ref[...] = cnt_ref[...] + jnp.sum(both, axis=0, keepdims=True)
    out = jnp.where(lane == 0.0, e1, 0.0)
    out = jnp.where(lane == 1.0, e2, out)
    out = jnp.where(lane == 2.0, w1, out)
    out = jnp.where(lane == 3.0, w2, out)
    out = jnp.where(lane == 4.0, rank1, out)
    out = jnp.where(lane == 5.0, rank2, out)
    route_ref[...] = out


def _post(oa16, ob16, x2d, mod8, g_post_mix, g_pre_ffn, w_out16, wr_hi, wr_lo, b_r, cnt0, tm, seq_rows,
          mod_row0):
    n = x2d.shape[0]
    row = lambda i: (i, 0)
    const = lambda i: (0, 0)
    return pl.pallas_call(
        _post_kernel,
        grid=(n // tm,),
        in_specs=[pl.BlockSpec((tm, 512), row),
                  pl.BlockSpec((tm, 512), row),
                  pl.BlockSpec((tm, D_MODEL), row),
                  _mod_spec(tm, seq_rows, mod_row0),
                  pl.BlockSpec((1, D_MODEL), const),
                  pl.BlockSpec((1, D_MODEL), const),
                  pl.BlockSpec((D_MODEL, D_MODEL), const),
                  pl.BlockSpec((D_MODEL, LANES), const),
                  pl.BlockSpec((D_MODEL, LANES), const),
                  pl.BlockSpec((1, LANES), const),
                  pl.BlockSpec((1, LANES), const)],
        out_specs=[pl.BlockSpec((tm, D_MODEL), row),
                   pl.BlockSpec((tm, D_MODEL), row),
                   pl.BlockSpec((tm, LANES), row),
                   pl.BlockSpec((1, LANES), const)],
        out_shape=[jax.ShapeDtypeStruct((n, D_MODEL), F32),
                   jax.ShapeDtypeStruct((n, D_MODEL), F32),
                   jax.ShapeDtypeStruct((n, LANES), F32),
                   jax.ShapeDtypeStruct((1, LANES), F32)],
        compiler_params=_cparams(("arbitrary",)),
        name="post",
    )(oa16, ob16, x2d, mod8, g_post_mix, g_pre_ffn, w_out16, wr_hi, wr_lo, b_r, cnt0)


def _dispatch_kernel(seg_ref, dp_ref, ds_ref, hp_ref, hs_ref, xs_hbm, zbuf, stage, sem, stage_sems, *,
                     n_blocks, n_prompt_steps):
    i = pl.program_id(0)
    last_step = pl.num_programs(0) - 1

    @pl.when(i == 0)
    def _():
        zbuf[...] = jnp.zeros_like(zbuf)

        def zero_block(row0):
            return pltpu.make_async_copy(zbuf, xs_hbm.at[pl.ds(pl.multiple_of(row0, MOE_BLOCK), MOE_BLOCK), :], sem)

        for e in range(N_EXPERTS):
            @pl.when(seg_ref[e] > 0)
            def _():
                zero_block(seg_ref[N_EXPERTS + e] - MOE_BLOCK).start()
        for e in range(N_EXPERTS):
            @pl.when(seg_ref[e] > 0)
            def _():
                zero_block(seg_ref[N_EXPERTS + e] - MOE_BLOCK).wait()

        used = seg_ref[2 * N_EXPERTS - 1] // MOE_BLOCK

        def fill(b, c):
            cp = zero_block(b * MOE_BLOCK)
            cp.start()
            cp.wait()
            return c

        lax.fori_loop(used, n_blocks, fill, 0)

    def scatter(h_ref, dest_ref, sem_):
        for t in range(h_ref.shape[0]):
            for slot in range(2):
                pltpu.make_async_copy(h_ref.at[pl.ds(t, 1), :],
                                      xs_hbm.at[pl.ds(dest_ref[0, 0, 2 * t + slot], 1), :],
                                      sem_).start(priority=slot)

    def drain(h_ref, sem_):
        for slot in range(2):
            pltpu.make_async_copy(h_ref, xs_hbm.at[pl.ds(0, h_ref.shape[0]), :], sem_).wait()

    for buf in range(2):
        @pl.when((i < last_step) & (i % 2 == buf))
        def _():
            @pl.when(i >= 2)
            def _():
                drain(stage.at[buf], stage_sems.at[buf])
            stage[buf] = hp_ref[...]
            scatter(stage.at[buf], dp_ref, stage_sems.at[buf])

    @pl.when(i == last_step)
    def _():
        scatter(hs_ref, ds_ref, sem)
        drain(hs_ref, sem)
        for buf in range(min(2, n_prompt_steps)):
            drain(stage.at[buf], stage_sems.at[buf])


def _dispatch(seg, dest_p, dest_s, h2p, h2s, n_blocks, tm):
    n_p, n_s = h2p.shape[0], h2s.shape[0]
    steps_p = n_p // tm
    pmap3 = lambda i, sg: (jnp.minimum(i, steps_p - 1), 0, 0)
    pmap2 = lambda i, sg: (jnp.minimum(i, steps_p - 1), 0)
    grid_spec = pltpu.PrefetchScalarGridSpec(
        num_scalar_prefetch=1,
        grid=(steps_p + 1,),
        in_specs=[pl.BlockSpec((1, 1, 2 * tm), pmap3, memory_space=pltpu.SMEM),
                  pl.BlockSpec((1, 1, 2 * n_s), lambda i, sg: (0, 0, 0), memory_space=pltpu.SMEM),
                  pl.BlockSpec((tm, D_MODEL), pmap2),
                  pl.BlockSpec((n_s, D_MODEL), lambda i, sg: (0, 0))],
        out_specs=pl.BlockSpec(memory_space=pl.ANY),
        scratch_shapes=[pltpu.VMEM((MOE_BLOCK, D_MODEL), F32), pltpu.VMEM((2, tm, D_MODEL), F32),
                        pltpu.SemaphoreType.DMA(()), pltpu.SemaphoreType.DMA((2,))])
    return pl.pallas_call(
        functools.partial(_dispatch_kernel, n_blocks=n_blocks, n_prompt_steps=steps_p),
        grid_spec=grid_spec,
        out_shape=jax.ShapeDtypeStruct((n_blocks * MOE_BLOCK, D_MODEL), F32),
        compiler_params=_cparams(("arbitrary",)),
        name="dispatch",
    )(seg, dest_p.reshape(steps_p, 1, 2 * tm), dest_s.reshape(1, 1, 2 * n_s), h2p, h2s)


def _moe_kernel(blk_e_ref, nvalid_ref, x_ref, wg_ref, wu_ref, wd_ref, y_ref, wg16, wu16, wd16):
    i = pl.program_id(0)
    e = blk_e_ref[i]
    e_prev = blk_e_ref[jnp.maximum(i - 1, 0)]

    @pl.when((i == 0) | (e != e_prev))
    def _():
        wg16[...] = wg_ref[0].astype(BF16)
        wu16[...] = wu_ref[0].astype(BF16)
        wd16[...] = wd_ref[0].astype(BF16)

    @pl.when(nvalid_ref[i] > 0)
    def _():
        xb = x_ref[...].astype(BF16)
        g = _dot(xb, wg16[...])
        u = _dot(xb, wu16[...])
        hmid = (_silu(g) * u).astype(BF16)
        y_ref[...] = _dot(hmid, wd16[...])

    @pl.when(nvalid_ref[i] == 0)
    def _():
        y_ref[...] = jnp.zeros_like(y_ref)


def _moe(blk_e, nvalid, x_sorted, w_gate, w_up, w_down):
    n_blocks = blk_e.shape[0]
    wmap = lambda i, be, nv: (be[i], 0, 0)
    xmap = lambda i, be, nv: (jnp.where(nv[i] > 0, i, 0), 0)
    grid_spec = pltpu.PrefetchScalarGridSpec(
        num_scalar_prefetch=2,
        grid=(n_blocks,),
        in_specs=[pl.BlockSpec((MOE_BLOCK, D_MODEL), xmap),
                  pl.BlockSpec((1, D_MODEL, D_EXPERT), wmap),
                  pl.BlockSpec((1, D_MODEL, D_EXPERT), wmap),
                  pl.BlockSpec((1, D_EXPERT, D_MODEL), wmap)],
        out_specs=pl.BlockSpec((MOE_BLOCK, D_MODEL), lambda i, be, nv: (i, 0)),
        scratch_shapes=[pltpu.VMEM((D_MODEL, D_EXPERT), BF16),
                        pltpu.VMEM((D_MODEL, D_EXPERT), BF16),
                        pltpu.VMEM((D_EXPERT, D_MODEL), BF16)])
    return pl.pallas_call(
        _moe_kernel,
        grid_spec=grid_spec,
        out_shape=jax.ShapeDtypeStruct((n_blocks * MOE_BLOCK, D_MODEL), F32),
        compiler_params=_cparams(("arbitrary",)),
        name="moe",
    )(blk_e, nvalid, x_sorted, w_gate, w_up, w_down)


def _combine_kernel(dcur_ref, dnext_ref, route_ref, x1_ref, mod_ref, g_ref, y_hbm, o_ref, ybuf, sems):
    i = pl.program_id(0)
    last = pl.num_programs(0) - 1
    tm = x1_ref.shape[0]

    def gather(dest_ref, buf):
        for t in range(tm):
            for slot in range(2):
                pltpu.make_async_copy(y_hbm.at[pl.ds(dest_ref[0, 0, 2 * t + slot], 1), :],
                                      ybuf.at[buf, slot, pl.ds(t, 1), :], sems.at[buf]).start(priority=slot)

    @pl.when(i == 0)
    def _():
        gather(dcur_ref, 0)

    for buf in range(2):
        @pl.when(i % 2 == buf)
        def _():
            @pl.when(i < last)
            def _():
                gather(dnext_ref, 1 - buf)

            for slot in range(2):
                pltpu.make_async_copy(y_hbm.at[pl.ds(0, tm), :], ybuf.at[buf, slot], sems.at[buf]).wait()
            route = route_ref[...]
            moe = ybuf[buf, 0] * route[:, 2:3] + ybuf[buf, 1] * route[:, 3:4]
            o_ref[...] = x1_ref[...] + _per_seq(_rms(moe, g_ref[...]), mod_ref[...], lambda y, m: y * m[:, 5:6])


def _combine(dest, route, y_sorted, x1, mod8, g_post_ffn, tm, seq_rows, mod_row0):
    n = x1.shape[0]
    steps = n // tm
    row = lambda i: (i, 0)
    dest3 = dest.reshape(steps, 1, 2 * tm)
    return pl.pallas_call(
        _combine_kernel,
        grid=(steps,),
        in_specs=[pl.BlockSpec((1, 1, 2 * tm), lambda i: (i, 0, 0), memory_space=pltpu.SMEM),
                  pl.BlockSpec((1, 1, 2 * tm), lambda i: (jnp.minimum(i + 1, steps - 1), 0, 0),
                               memory_space=pltpu.SMEM),
                  pl.BlockSpec((tm, LANES), row),
                  pl.BlockSpec((tm, D_MODEL), row),
                  _mod_spec(tm, seq_rows, mod_row0),
                  pl.BlockSpec((1, D_MODEL), lambda i: (0, 0)),
                  pl.BlockSpec(memory_space=pl.ANY)],
        out_specs=pl.BlockSpec((tm, D_MODEL), row),
        out_shape=jax.ShapeDtypeStruct((n, D_MODEL), F32),
        scratch_shapes=[pltpu.VMEM((2, 2, tm, D_MODEL), F32), pltpu.SemaphoreType.DMA((2,))],
        compiler_params=_cparams(("arbitrary",)),
        name="combine",
    )(dest3, dest3, route, x1, mod8, g_post_ffn, y_sorted)


def _segment_plan(counts_f, n_blocks):
    counts = counts_f[0, :N_EXPERTS].astype(jnp.int32)
    padded = (counts + MOE_BLOCK - 1) // MOE_BLOCK * MOE_BLOCK
    pad_end = jnp.cumsum(padded)
    pad_start = pad_end - padded
    blk_start = jnp.arange(n_blocks, dtype=jnp.int32) * MOE_BLOCK
    blk_e = jnp.minimum(jnp.sum((pad_end[None, :] <= blk_start[:, None]).astype(jnp.int32), axis=1),
                        N_EXPERTS - 1)
    onehot = blk_e[:, None] == jnp.arange(N_EXPERTS, dtype=jnp.int32)[None, :]
    c_blk = jnp.sum(jnp.where(onehot, counts[None, :], 0), axis=1)
    s_blk = jnp.sum(jnp.where(onehot, pad_start[None, :], 0), axis=1)
    nvalid = jnp.clip(c_blk - (blk_start - s_blk), 0, MOE_BLOCK).astype(jnp.int32)
    seg = jnp.concatenate([counts, pad_end]).astype(jnp.int32)
    return blk_e.astype(jnp.int32), nvalid, seg, pad_start


def _token_rows(route, pad_start):
    eid = route[:, 0:2].astype(jnp.int32)
    rank = route[:, 4:6].astype(jnp.int32)
    onehot = eid[:, :, None] == jnp.arange(N_EXPERTS, dtype=jnp.int32)[None, None, :]
    return rank + jnp.sum(jnp.where(onehot, pad_start[None, None, :], 0), axis=2)


def _layer(x_p, x_s, c_p, c_s, k_past, v_past, s0_s, conv_s, p):
    bp, tp, d = x_p.shape
    bs, ts, _ = x_s.shape
    n_p, n_s = bp * tp, bs * ts
    n_tok = n_p + n_s

    n_seq = bp + bs
    c_all = jnp.zeros((16, d), F32).at[:n_seq].set(jnp.concatenate([c_s, c_p], axis=0))
    mod = _ada(c_all, p['w_ada'], p['b_ada'])
    mod8 = jnp.pad(mod.reshape(16, 6, d), ((0, 0), (0, 2), (0, 0)))

    w_in = p['w_in']
    o_z, o_b, o_q = CONV_CH, CONV_CH + 512, CONV_CH + 512 + 2 * DN_HEADS
    wb = jnp.pad(w_in[:, o_b:o_q], ((0, 0), (0, LANES - 2 * DN_HEADS)))
    wb_hi = wb.astype(BF16)
    wb_lo = (wb - wb_hi.astype(F32)).astype(BF16)
    w_main = jnp.concatenate([w_in[:, :o_b].astype(BF16), w_in[:, o_q:].astype(BF16), wb_hi, wb_lo], axis=1)
    g_pre_mix = p['g_pre_mix'].reshape(1, d)

    conv_w8 = jnp.pad(p['conv_w'], ((0, 8 - CONV_W), (0, 0)))
    pad_g = lambda a: jnp.pad(a.reshape(1, DN_HEADS), ((0, 0), (DN_HEADS, LANES - 2 * DN_HEADS)))
    alog_row, dtb_row = pad_g(p['a_log']), pad_g(p['dt_bias'])
    onorm_a = p['onorm_a'].reshape(1, DN_DIM)
    onorm_b2 = jnp.tile(p['onorm_b'].reshape(1, SB_DIM), (1, 2))

    w_out16 = p['w_out'].astype(BF16)
    wr = jnp.pad(jnp.concatenate([p['w_router_group'], p['w_router_expert']], axis=1),
                 ((0, 0), (0, LANES - N_GROUPS - N_EXPERTS)))
    wr_hi = wr.astype(BF16)
    wr_lo = (wr - wr_hi.astype(F32)).astype(BF16)
    b_r = jnp.pad(jnp.concatenate([p['b_router_group'], p['b_router_expert']]).reshape(1, -1),
                  ((0, 0), (0, LANES - N_GROUPS - N_EXPERTS)))
    g_post_mix = p['g_post_mix'].reshape(1, d)
    g_pre_ffn = p['g_pre_ffn'].reshape(1, d)
    g_post_ffn = p['g_post_ffn'].reshape(1, d)

    def mixer(x, tm, tm_post, seq_rows, mod_row0, hist8, s0, k_old, v_old, chunk, n_chunks, bq, n_sub, cnt0):
        b, t, _ = x.shape
        x2d = x.reshape(b * t, d)
        a_in, z, bg, q16, kb, vb, k16, v16 = _proj(x2d, mod8, g_pre_mix, w_main, wb_hi, tm, seq_rows, mod_row0)
        r3 = lambda a: a.reshape(b, t, a.shape[-1])
        oa16, s_new = _delta(r3(a_in), r3(z), r3(bg), hist8, s0, conv_w8, alog_row, dtb_row, onorm_a,
                             chunk, n_chunks)
        k16, v16 = r3(k16), r3(v16)
        if k_old is not None:
            k16 = jnp.concatenate([k_old.reshape(b, -1, SB_WIDTH).astype(BF16), k16], axis=1)
            v16 = jnp.concatenate([v_old.reshape(b, -1, SB_WIDTH).astype(BF16), v16], axis=1)
        n_pad = (-k16.shape[1]) % KEY_TILE
        k16 = jnp.pad(k16, ((0, 0), (n_pad, 0), (0, 0)))
        v16 = jnp.pad(v16, ((0, 0), (n_pad, 0), (0, 0)))
        ob16 = _sb_attn(r3(q16), k16, v16, onorm_b2, bq, n_sub, n_pad)
        x1, h2, route, cnt = _post(oa16.reshape(b * t, 512), ob16.reshape(b * t, 512), x2d, mod8,
                                   g_post_mix, g_pre_ffn, w_out16, wr_hi, wr_lo, b_r, cnt0, tm_post, seq_rows,
                                   mod_row0)
        new_conv = r3(a_in)[:, t - (CONV_W - 1):, :]
        return (x1, h2, route, cnt, kb.reshape(b, t, SB_HEADS, SB_DIM), vb.reshape(b, t, SB_HEADS, SB_DIM),
                s_new, new_conv)

    zero_hist = jnp.zeros((bp, 8, CONV_CH), F32)
    zero_s = jnp.zeros((bp, DN_HEADS, DN_DIM, DN_DIM), F32)
    hist_s = jnp.pad(conv_s, ((0, 0), (8 - (CONV_W - 1), 0), (0, 0)))
    tm_p = min(ROW_TILE, tp)
    tm_dense = DENSE_TILE if n_p % DENSE_TILE == 0 else tm_p
    tm_router = ROUTER_TILE if n_p % ROUTER_TILE == 0 else tm_dense
    nc_p = max(1, min(8, tp // DELTA_BLOCK))
    x1p, h2p, rp, cnt_p, kp, vp, sp, cp = mixer(x_p, tm_dense, tm_router, tp, bs, zero_hist, zero_s, None, None,
                                                 min(DELTA_BLOCK, tp), nc_p, min(KEY_TILE // 2, tp), 2,
                                                 jnp.zeros((1, LANES), F32))
    x1s, h2s, rs, cnt, ks, vs, ss, cs = mixer(x_s, n_s, n_s, ts, 0, hist_s, s0_s, k_past, v_past,
                                               min(DELTA_BLOCK, ts), max(1, ts // DELTA_BLOCK),
                                               min(KEY_TILE, ts), 1, cnt_p)

    n_blocks = -(-2 * n_tok // MOE_BLOCK) + N_EXPERTS
    blk_e, nvalid, seg, pad_start = _segment_plan(cnt, n_blocks)
    dest_p = _token_rows(rp, pad_start)
    dest_s = _token_rows(rs, pad_start)
    x_sorted = _dispatch(seg, dest_p, dest_s, h2p, h2s, n_blocks, tm_p)
    y_sorted = _moe(blk_e, nvalid, x_sorted, p['w_gate'], p['w_up'], p['w_down'])
    y_p = _combine(dest_p, rp, y_sorted, x1p, mod8, g_post_ffn, tm_p, tp, bs).reshape(bp, tp, d)
    y_s = _combine(dest_s, rs, y_sorted, x1s, mod8, g_post_ffn, n_s, ts, 0).reshape(bs, ts, d)
    return y_p, y_s, kp, vp, sp, cp, ks, vs, ss, cs


def kernel(x_prompt, x_sample, c_prompt, c_sample, cache_k, cache_v, state_delta, state_conv, w_ada, b_ada, g_pre_mix, g_post_mix, g_pre_ffn, g_post_ffn, w_in, conv_w, a_log, dt_bias, onorm_a, onorm_b, w_out, w_router_group, b_router_group, w_router_expert, b_router_expert, w_gate, w_up, w_down):
    depth = w_in.shape[0]
    y_p, y_s = x_prompt, x_sample
    outs = [[] for _ in range(8)]
    for l in range(depth):
        p = dict(w_ada=w_ada[l], b_ada=b_ada[l], g_pre_mix=g_pre_mix[l], g_post_mix=g_post_mix[l],
                 g_pre_ffn=g_pre_ffn[l], g_post_ffn=g_post_ffn[l], w_in=w_in[l], conv_w=conv_w[l],
                 a_log=a_log[l], dt_bias=dt_bias[l], onorm_a=onorm_a[l], onorm_b=onorm_b[l],
                 w_out=w_out[l], w_router_group=w_router_group[l], b_router_group=b_router_group[l],
                 w_router_expert=w_router_expert[l], b_router_expert=b_router_expert[l],
                 w_gate=w_gate[l], w_up=w_up[l], w_down=w_down[l])
        res = _layer(y_p, y_s, c_prompt, c_sample, cache_k[l], cache_v[l], state_delta[l], state_conv[l], p)
        y_p, y_s = res[0], res[1]
        for lst, r in zip(outs, res[2:]):
            lst.append(r)
    return (y_p, y_s) + tuple(jnp.stack(o) for o in outs)
```

```python
import functools

import jax
import jax.numpy as jnp
from jax import lax
from jax.experimental import pallas as pl
from jax.experimental.pallas import tpu as pltpu

F32 = jnp.float32
BF16 = jnp.bfloat16

D_MODEL = 1024
DN_HEADS = 4
DN_DIM = 128
CONV_W = 4
DN_WIDTH = DN_HEADS * DN_DIM
CONV_CH = 3 * DN_WIDTH
DELTA_BLOCK = 64
SB_HEADS = 8
SB_DIM = 64
SB_WIDTH = SB_HEADS * SB_DIM
N_GROUPS = 4
EXPERTS_PER_GROUP = 8
N_EXPERTS = N_GROUPS * EXPERTS_PER_GROUP
D_EXPERT = D_MODEL // 2
MOE_BLOCK = 256
EPS = 1e-6

LANES = 128
KEY_TILE = 128
ATTN_WINDOW = 3 * KEY_TILE
WINDOW_BLOCKS = 4
EXP_ZERO_BELOW = -104.0
VMEM_LIMIT = 56 * 1024 * 1024
DENSE_TILE = 512
ROUTER_TILE = 1024
ROW_TILE = 256


def _cparams(sem):
    return pltpu.CompilerParams(dimension_semantics=sem, vmem_limit_bytes=VMEM_LIMIT)


def _split(a):
    hi = a.astype(BF16)
    lo = (a - hi.astype(F32)).astype(BF16)
    return hi, lo


def _dot(a, b, dims=(((1,), (0,)), ((), ()))):
    return lax.dot_general(a, b, dims, preferred_element_type=F32)


def _dot_small_int_lhs(a, b):
    a16 = a.astype(BF16)
    return _dot(jnp.concatenate([a16, a16], axis=1), jnp.concatenate(_split(b), axis=0))


_NT = (((1,), (1,)), ((), ()))
_TN = (((0,), (0,)), ((), ()))


def _silu(x):
    return x * jax.nn.sigmoid(x)


SOFTPLUS_LINEAR_ABOVE = 80.0


def _softplus(x):
    return jnp.where(x > SOFTPLUS_LINEAR_ABOVE, x, jnp.log(1.0 + jnp.exp(jnp.minimum(x, SOFTPLUS_LINEAR_ABOVE))))


def _ada_kernel(c_ref, w_ref, b_ref, o_ref):
    s = _silu(c_ref[...]).astype(BF16)
    o_ref[...] = _dot(s, w_ref[...].astype(BF16)) + b_ref[...]


def _ada(c_all, w_ada, b_ada):
    rows = c_all.shape[0]
    n = w_ada.shape[1]
    tn = 1024
    return pl.pallas_call(
        _ada_kernel,
        grid=(n // tn,),
        in_specs=[pl.BlockSpec((rows, D_MODEL), lambda j: (0, 0)),
                  pl.BlockSpec((D_MODEL, tn), lambda j: (0, j)),
                  pl.BlockSpec((1, tn), lambda j: (0, j))],
        out_specs=pl.BlockSpec((rows, tn), lambda j: (0, j)),
        out_shape=jax.ShapeDtypeStruct((rows, n), F32),
        compiler_params=_cparams(("arbitrary",)),
        name="ada",
    )(c_all, w_ada, b_ada.reshape(1, n))


def _rms(x, gain):
    return x * lax.rsqrt(jnp.mean(x * x, axis=-1, keepdims=True) + EPS) * gain


def _per_seq(y, mod, fn):
    n_seq = mod.shape[0]
    ys = y.reshape(n_seq, y.shape[0] // n_seq, y.shape[1])
    return fn(ys, mod).reshape(y.shape)


def _mod_spec(tm, seq_rows, mod_row0):
    if tm >= seq_rows:
        n_seq = tm // seq_rows
        assert tm % seq_rows == 0 and mod_row0 % n_seq == 0
        return pl.BlockSpec((n_seq, 8, D_MODEL), lambda i: (mod_row0 // n_seq + i, 0, 0))
    assert seq_rows % tm == 0
    return pl.BlockSpec((1, 8, D_MODEL), lambda i: (mod_row0 + (i * tm) // seq_rows, 0, 0))


def _proj_kernel(x_ref, mod_ref, g_ref, wm_ref, wbh_ref,
                 a_ref, z_ref, bg_ref, q_ref, k_ref, v_ref, k16_ref, v16_ref):
    h = _per_seq(_rms(x_ref[...], g_ref[...]), mod_ref[...], lambda y, m: y * (1.0 + m[:, 1:2]) + m[:, 0:1])
    hh, hl = _split(h)
    p = _dot(hh, wm_ref[...])
    a_ref[...] = p[:, 0:CONV_CH]
    z_ref[...] = p[:, CONV_CH:CONV_CH + DN_WIDTH]
    o = CONV_CH + DN_WIDTH
    q_ref[...] = (p[:, o:o + SB_WIDTH] * (SB_DIM ** -0.5)).astype(BF16)
    k = p[:, o + SB_WIDTH:o + 2 * SB_WIDTH]
    v = p[:, o + 2 * SB_WIDTH:o + 3 * SB_WIDTH]
    k_ref[...] = k
    v_ref[...] = v
    k16_ref[...] = k.astype(BF16)
    v16_ref[...] = v.astype(BF16)
    o += 3 * SB_WIDTH
    bg_ref[...] = p[:, o:o + LANES] + (_dot(hl, wbh_ref[...]) + p[:, o + LANES:])


def _proj(x2d, mod8, g_pre, w_main, wb_hi, tm, seq_rows, mod_row0):
    n = x2d.shape[0]
    nm = w_main.shape[1]
    row = lambda i: (i, 0)
    const = lambda i: (0, 0)
    outs = [(CONV_CH, F32), (DN_WIDTH, F32), (LANES, F32), (SB_WIDTH, BF16), (SB_WIDTH, F32), (SB_WIDTH, F32),
            (SB_WIDTH, BF16), (SB_WIDTH, BF16)]
    return pl.pallas_call(
        _proj_kernel,
        grid=(n // tm,),
        in_specs=[pl.BlockSpec((tm, D_MODEL), row),
                  _mod_spec(tm, seq_rows, mod_row0),
                  pl.BlockSpec((1, D_MODEL), const),
                  pl.BlockSpec((D_MODEL, nm), const),
                  pl.BlockSpec((D_MODEL, LANES), const)],
        out_specs=[pl.BlockSpec((tm, w), row) for w, _ in outs],
        out_shape=[jax.ShapeDtypeStruct((n, w), dt) for w, dt in outs],
        compiler_params=_cparams(("arbitrary",)),
        name="proj",
    )(x2d, mod8, g_pre, w_main, wb_hi)


def _delta_kernel(a_ref, z_ref, bg_ref, hist0_ref, s0_ref, cw_ref, alog_ref, dtb_ref, on_ref,
                  o_ref, sfin_ref, hist_sc, s_sc, *, chunk, n_chunks):
    t_idx = pl.program_id(1)
    tt = chunk * n_chunks

    @pl.when(t_idx == 0)
    def _():
        hist_sc[...] = hist0_ref[0]
        s_sc[...] = s0_ref[0]

    x = a_ref[0]
    xx = jnp.concatenate([hist_sc[...], x], axis=0)
    cw = cw_ref[...]
    conv = x * cw[CONV_W - 1:CONV_W]
    for s in range(1, CONV_W):
        conv = conv + pltpu.roll(xx, s, 0)[8:] * cw[CONV_W - 1 - s:CONV_W - s]
    conv = _silu(conv)
    hist_sc[...] = x[tt - 8:tt]

    bg = bg_ref[0]
    lane = lax.broadcasted_iota(jnp.int32, (1, LANES), 1)
    g_lane = (lane >= DN_HEADS) & (lane < 2 * DN_HEADS)
    neg_a = jnp.where(g_lane, -jnp.exp(alog_ref[...]), 0.0)
    beta_all = jax.nn.sigmoid(bg)
    g_all = neg_a * _softplus(bg + dtb_ref[...])

    ri = lax.broadcasted_iota(jnp.int32, (chunk, chunk), 0)
    ci = lax.broadcasted_iota(jnp.int32, (chunk, chunk), 1)
    incl = ri >= ci
    strict = ri > ci
    tri = incl.astype(F32)
    eye = (ri == ci).astype(F32)
    ones_cc = jnp.ones((chunk, chunk), F32)
    onorm = on_ref[...]

    chunks = range(n_chunks)
    units = [(c, h) for c in chunks for h in range(DN_HEADS)]
    rows = lambda c: slice(c * chunk, (c + 1) * chunk)

    gc_all = [_dot_small_int_lhs(tri, g_all[rows(c)]) for c in chunks]
    diag = [jnp.concatenate([eye * gc_all[c][:, DN_HEADS + h:DN_HEADS + h + 1] for h in range(DN_HEADS)], axis=1)
            for c in chunks]
    gc_rows = [_dot_small_int_lhs(ones_cc, diag[c]) for c in chunks]

    q, k, v, beta, gcol, glast, eg, decay, kbeta = {}, {}, {}, {}, {}, {}, {}, {}, {}
    for (c, h) in units:
        lo = h * DN_DIM
        qq = conv[rows(c), lo:lo + DN_DIM]
        kk = conv[rows(c), DN_WIDTH + lo:DN_WIDTH + lo + DN_DIM]
        u_ = (c, h)
        q[u_] = qq * lax.rsqrt(jnp.sum(qq * qq, axis=-1, keepdims=True) + EPS) * (DN_DIM ** -0.5)
        k[u_] = kk * lax.rsqrt(jnp.sum(kk * kk, axis=-1, keepdims=True) + EPS)
        v[u_] = conv[rows(c), 2 * DN_WIDTH + lo:2 * DN_WIDTH + lo + DN_DIM]
        beta[u_] = beta_all[rows(c), h:h + 1]
        gcol[u_] = gc_all[c][:, DN_HEADS + h:DN_HEADS + h + 1]
        glast[u_] = gcol[u_][chunk - 1:chunk, :]
        eg[u_] = jnp.exp(gcol[u_])
        grow = gc_rows[c][:, h * chunk:(h + 1) * chunk]
        decay[u_] = jnp.where(incl, jnp.exp(jnp.minimum(gcol[u_] - grow, 0.0)), 0.0)
        kbeta[u_] = k[u_] * beta[u_]

    k16 = {u_: k[u_].astype(BF16) for u_ in units}
    kq = {u_: _dot(jnp.concatenate([kbeta[u_], q[u_]], axis=0).astype(BF16), k16[u_], _NT) for u_ in units}
    lower = {u_: jnp.where(strict, kq[u_][:chunk] * decay[u_], 0.0) for u_ in units}
    attn16 = {u_: jnp.where(incl, kq[u_][chunk:] * decay[u_], 0.0).astype(BF16) for u_ in units}

    sol = {u_: jnp.concatenate([v[u_] * beta[u_], kbeta[u_] * eg[u_]], axis=1) for u_ in units}
    lp = lower
    p = 1
    while p < chunk:
        lsp = {u_: _split(lp[u_]) for u_ in units}
        ssp = {u_: _split(sol[u_]) for u_ in units}
        lcat = {u_: jnp.concatenate([lsp[u_][0], lsp[u_][1], lsp[u_][0]], axis=1) for u_ in units}
        upd = {u_: _dot(lcat[u_], jnp.concatenate([ssp[u_][0], ssp[u_][0], ssp[u_][1]], axis=0)) for u_ in units}
        sol = {u_: (sol[u_] - upd[u_]) if p == 1 else (sol[u_] + upd[u_]) for u_ in units}
        if 2 * p < chunk:
            lp = {u_: _dot(lcat[u_], jnp.concatenate([lsp[u_][0], lsp[u_][0], lsp[u_][1]], axis=0))
                  for u_ in units}
        p *= 2

    usol = {u_: sol[u_][:, :DN_DIM] for u_ in units}
    wq16 = {u_: jnp.concatenate([sol[u_][:, DN_DIM:], q[u_] * eg[u_]], axis=0).astype(BF16) for u_ in units}
    kd16 = {u_: (k[u_] * jnp.exp(glast[u_] - gcol[u_])).astype(BF16) for u_ in units}

    for c in chunks:
        hs = range(DN_HEADS)
        s_old = [s_sc[h] for h in hs]
        ws = [_dot(wq16[(c, h)], s_old[h].astype(BF16)) for h in hs]
        v_new = [(usol[(c, h)] - ws[h][:chunk]).astype(BF16) for h in hs]
        o_in = [_dot(attn16[(c, h)], v_new[h]) for h in hs]
        ds = [_dot(kd16[(c, h)], v_new[h], _TN) for h in hs]
        for h in hs:
            lo = h * DN_DIM
            s_sc[h] = s_old[h] * jnp.exp(glast[(c, h)]) + ds[h]
            o = ws[h][chunk:] + o_in[h]
            zg = z_ref[0, rows(c), lo:lo + DN_DIM]
            o_ref[0, rows(c), lo:lo + DN_DIM] = (_rms(o, onorm) * _silu(zg)).astype(BF16)

    @pl.when(t_idx == pl.num_programs(1) - 1)
    def _():
        sfin_ref[0] = s_sc[...]


def _delta(a_in, z, bg, hist8, s0, conv_w8, alog_row, dtb_row, onorm_a, chunk, n_chunks):
    b, t, _ = a_in.shape
    tt = chunk * n_chunks
    tile = lambda bi, ti: (bi, ti, 0)
    per_b3 = lambda bi, ti: (bi, 0, 0)
    per_b4 = lambda bi, ti: (bi, 0, 0, 0)
    const = lambda bi, ti: (0, 0)
    kern = functools.partial(_delta_kernel, chunk=chunk, n_chunks=n_chunks)
    return pl.pallas_call(
        kern,
        grid=(b, t // tt),
        in_specs=[pl.BlockSpec((1, tt, CONV_CH), tile),
                  pl.BlockSpec((1, tt, DN_WIDTH), tile),
                  pl.BlockSpec((1, tt, LANES), tile),
                  pl.BlockSpec((1, 8, CONV_CH), per_b3),
                  pl.BlockSpec((1, DN_HEADS, DN_DIM, DN_DIM), per_b4),
                  pl.BlockSpec((8, CONV_CH), const),
                  pl.BlockSpec((1, LANES), const),
                  pl.BlockSpec((1, LANES), const),
                  pl.BlockSpec((1, DN_DIM), const)],
        out_specs=[pl.BlockSpec((1, tt, DN_WIDTH), tile),
                   pl.BlockSpec((1, DN_HEADS, DN_DIM, DN_DIM), per_b4)],
        out_shape=[jax.ShapeDtypeStruct((b, t, DN_WIDTH), BF16),
                   jax.ShapeDtypeStruct((b, DN_HEADS, DN_DIM, DN_DIM), F32)],
        scratch_shapes=[pltpu.VMEM((8, CONV_CH), F32),
                        pltpu.VMEM((DN_HEADS, DN_DIM, DN_DIM), F32)],
        compiler_params=_cparams(("arbitrary", "arbitrary")),
        name="delta",
    )(a_in, z, bg, hist8, s0, conv_w8, alog_row, dtb_row, onorm_a)


def _sb_kernel(q_ref, *refs, bq, n_sub, n_pad, q_off):
    kwin = refs[:WINDOW_BLOCKS]
    vwin = refs[WINDOW_BLOCKS:2 * WINDOW_BLOCKS]
    k_hbm, v_hbm, on_ref, o_ref, kbuf, vbuf, qsel, acc, carry, sem = refs[2 * WINDOW_BLOCKS:]
    b = pl.program_id(0)
    i = pl.program_id(1)
    g = bq * n_sub
    qend_step = q_off + (i + 1) * g
    win_start = qend_step - WINDOW_BLOCKS * KEY_TILE
    n_pairs = SB_HEADS // 2
    heads = range(SB_HEADS)

    half_lane = lax.broadcasted_iota(jnp.int32, (bq, LANES), 1) < SB_DIM
    rj = lax.broadcasted_iota(jnp.int32, (2 * KEY_TILE, 2 * KEY_TILE), 0) % KEY_TILE
    cj = lax.broadcasted_iota(jnp.int32, (2 * KEY_TILE, 2 * KEY_TILE), 1)
    suffix2 = ((rj > cj) | (cj >= KEY_TILE)).astype(BF16)

    def window(blocks, off, p):
        parts = []
        for blk in range(WINDOW_BLOCKS):
            lo, hi = max(off, blk * KEY_TILE), min(off + ATTN_WINDOW, (blk + 1) * KEY_TILE)
            if lo < hi:
                ref = blocks[WINDOW_BLOCKS - 1 - blk]
                parts.append(ref[0, lo - blk * KEY_TILE:hi - blk * KEY_TILE, p * LANES:(p + 1) * LANES])
        return jnp.concatenate(parts, axis=0)

    n_t = ATTN_WINDOW // KEY_TILE
    row = lax.broadcasted_iota(jnp.int32, (bq, ATTN_WINDOW), 0)
    col = lax.broadcasted_iota(jnp.int32, (bq, ATTN_WINDOW), 1)
    causal = col < row + (ATTN_WINDOW - bq)
    subs = range(n_sub)
    offs = [WINDOW_BLOCKS * KEY_TILE - g + (s + 1) * bq - ATTN_WINDOW for s in subs]
    mask = [causal & (col >= n_pad - (win_start + offs[s])) for s in subs]
    zs, sp, cs_all = {}, {}, {}
    for s in subs:
        for p in range(n_pairs):
            qf = q_ref[0, s * bq:(s + 1) * bq, p * LANES:(p + 1) * LANES].astype(F32)
            qq = jnp.concatenate([jnp.where(half_lane, qf, 0.0), jnp.where(half_lane, 0.0, qf)], axis=0)
            zz = _dot(qq.astype(BF16), window(kwin, offs[s], p), _NT)
            zs[(s, 2 * p)], zs[(s, 2 * p + 1)] = zz[:bq], zz[bq:]
    for s in subs:
        pieces = []
        for h in heads:
            sp[(s, h)] = _softplus(zs[(s, h)])
            hi, lo = _split(jnp.where(mask[s], -sp[(s, h)], 0.0))
            for j in range(n_t):
                c0 = ATTN_WINDOW - (j + 1) * KEY_TILE
                pieces.append(jnp.concatenate([hi[:, c0:c0 + KEY_TILE], lo[:, c0:c0 + KEY_TILE]], axis=1))
        cs_all[s] = _dot(jnp.concatenate(pieces, axis=0), suffix2)
    done = []
    for s in subs:
        worst = jnp.full((bq, KEY_TILE), -jnp.inf, F32)
        a = {}
        for h in heads:
            run = None
            cols = []
            for j in range(n_t):
                r0 = (h * n_t + j) * bq
                cs = cs_all[s][r0:r0 + bq]
                cols.append(cs[:, :KEY_TILE] if run is None else cs[:, :KEY_TILE] + run)
                run = cs[:, KEY_TILE:] if run is None else run + cs[:, KEY_TILE:]
            within = jnp.concatenate(cols[::-1], axis=1)
            a[h] = jnp.where(mask[s], jnp.exp((zs[(s, h)] - sp[(s, h)]) + within), 0.0).astype(BF16)
            carry[s, h] = run
            worst = jnp.maximum(worst, run)
        for p in range(n_pairs):
            pv = _dot(jnp.concatenate([a[2 * p], a[2 * p + 1]], axis=0), window(vwin, offs[s], p))
            acc[s, p] = jnp.where(half_lane, pv[:bq], pv[bq:])
        done.append((jnp.max(worst) < EXP_ZERO_BELOW).astype(jnp.int32))

    rj1 = lax.broadcasted_iota(jnp.int32, (KEY_TILE, 2 * KEY_TILE), 0)
    cj1 = lax.broadcasted_iota(jnp.int32, (KEY_TILE, 2 * KEY_TILE), 1)
    suffix1 = ((rj1 > cj1) | (cj1 >= KEY_TILE)).astype(BF16)
    col1 = lax.broadcasted_iota(jnp.int32, (bq, KEY_TILE), 1)
    for s in subs:
        swept_from = win_start + offs[s]

        @pl.when((done[s] == 0) & (swept_from > n_pad))
        def _():
            for p in range(n_pairs):
                qf = q_ref[0, s * bq:(s + 1) * bq, p * LANES:(p + 1) * LANES].astype(F32)
                qsel[2 * p] = jnp.where(half_lane, qf, 0.0).astype(BF16)
                qsel[2 * p + 1] = jnp.where(half_lane, 0.0, qf).astype(BF16)

            def body(state):
                upper, _ = state
                start = jnp.maximum(upper - KEY_TILE, 0)
                copies = []
                for p in range(n_pairs):
                    for src, dst in ((k_hbm, kbuf), (v_hbm, vbuf)):
                        cp = pltpu.make_async_copy(
                            src.at[b, pl.ds(pl.multiple_of(start, 16), KEY_TILE), pl.ds(p * LANES, LANES)],
                            dst.at[p], sem)
                        cp.start()
                        copies.append(cp)
                for cp in copies:
                    cp.wait()
                kpos = start + col1
                m1 = (kpos < upper) & (kpos >= n_pad)
                zz = [_dot(qsel[h], kbuf[h // 2], _NT) for h in heads]
                spp = [_softplus(zz[h]) for h in heads]
                pcs = []
                for h in heads:
                    pcs.extend(_split(jnp.where(m1, -spp[h], 0.0)))
                cs1 = _dot(jnp.concatenate(pcs, axis=0), suffix1)
                worst = jnp.full((bq, KEY_TILE), -jnp.inf, F32)
                aa = []
                for h in heads:
                    c = cs1[2 * h * bq:(2 * h + 1) * bq] + cs1[(2 * h + 1) * bq:(2 * h + 2) * bq]
                    c_old = carry[s, h]
                    aa.append(jnp.where(m1, jnp.exp((zz[h] - spp[h]) + c[:, :KEY_TILE] + c_old), 0.0).astype(BF16))
                    c_new = c_old + c[:, KEY_TILE:]
                    carry[s, h] = c_new
                    worst = jnp.maximum(worst, c_new)
                for p in range(n_pairs):
                    pv0 = _dot(aa[2 * p], vbuf[p])
                    pv1 = _dot(aa[2 * p + 1], vbuf[p])
                    acc[s, p] = acc[s, p] + jnp.where(half_lane, pv0, pv1)
                return start, (jnp.max(worst) < EXP_ZERO_BELOW).astype(jnp.int32)

            lax.while_loop(lambda st: (st[0] > n_pad) & (st[1] == 0), body, (swept_from, jnp.int32(0)))

    onb = on_ref[...]
    for s in subs:
        for p in range(n_pairs):
            o = acc[s, p]
            sq = o * o
            s_lo = jnp.sum(jnp.where(half_lane, sq, 0.0), axis=-1, keepdims=True)
            s_hi = jnp.sum(jnp.where(half_lane, 0.0, sq), axis=-1, keepdims=True)
            ms = jnp.where(half_lane, s_lo, s_hi) * (1.0 / SB_DIM)
            o_ref[0, s * bq:(s + 1) * bq, p * LANES:(p + 1) * LANES] = (o * lax.rsqrt(ms + EPS) * onb).astype(BF16)


def _sb_attn(q16, k16p, v16p, onorm_b2, bq, n_sub, n_pad):
    b, tq, _ = q16.shape
    tkp = k16p.shape[1]
    q_off = tkp - tq
    g = bq * n_sub
    assert tq % g == 0 and g <= KEY_TILE and bq % 16 == 0
    assert all((q_off + (i + 1) * g) % KEY_TILE == 0 for i in range(tq // g))

    def kmap(back):
        def f(bi, i):
            last = (q_off + (i + 1) * g) // KEY_TILE - 1
            return (bi, jnp.maximum(last - back, 0), 0)
        return f

    qmap = lambda bi, i: (bi, i, 0)
    kern = functools.partial(_sb_kernel, bq=bq, n_sub=n_sub, n_pad=n_pad, q_off=q_off)
    kspec = [pl.BlockSpec((1, KEY_TILE, SB_WIDTH), kmap(back)) for back in range(WINDOW_BLOCKS)]
    return pl.pallas_call(
        kern,
        grid=(b, tq // g),
        in_specs=[pl.BlockSpec((1, g, SB_WIDTH), qmap)] + kspec + kspec
                 + [pl.BlockSpec(memory_space=pl.ANY), pl.BlockSpec(memory_space=pl.ANY),
                    pl.BlockSpec((1, LANES), lambda bi, i: (0, 0))],
        out_specs=pl.BlockSpec((1, g, SB_WIDTH), qmap),
        out_shape=jax.ShapeDtypeStruct((b, tq, SB_WIDTH), BF16),
        scratch_shapes=[pltpu.VMEM((SB_HEADS // 2, KEY_TILE, LANES), BF16),
                        pltpu.VMEM((SB_HEADS // 2, KEY_TILE, LANES), BF16),
                        pltpu.VMEM((SB_HEADS, bq, LANES), BF16),
                        pltpu.VMEM((n_sub, SB_HEADS // 2, bq, LANES), F32),
                        pltpu.VMEM((n_sub, SB_HEADS, bq, KEY_TILE), F32),
                        pltpu.SemaphoreType.DMA(())],
        compiler_params=_cparams(("arbitrary", "arbitrary")),
        name="sb_attn",
    )(q16, *([k16p] * WINDOW_BLOCKS), *([v16p] * WINDOW_BLOCKS), k16p, v16p, onorm_b2)


def _post_kernel(oa_ref, ob_ref, x_ref, mod_ref, gpm_ref, gpf_ref, wo_ref, wrh_ref, wrl_ref, br_ref, cnt0_ref,
                 x1_ref, h2_ref, route_ref, cnt_ref):
    @pl.when(pl.program_id(0) == 0)
    def _():
        cnt_ref[...] = cnt0_ref[...]

    mod = mod_ref[...]
    mix = _dot(oa_ref[...], wo_ref[0:DN_WIDTH, :]) + _dot(ob_ref[...], wo_ref[DN_WIDTH:DN_WIDTH + SB_WIDTH, :])
    x1 = x_ref[...] + _per_seq(_rms(mix, gpm_ref[...]), mod, lambda y, m: y * m[:, 2:3])
    x1_ref[...] = x1
    h2 = _per_seq(_rms(x1, gpf_ref[...]), mod, lambda y, m: y * (1.0 + m[:, 4:5]) + m[:, 3:4])
    h2_ref[...] = h2
    hh, hl = _split(h2)
    wrh = wrh_ref[...]
    logits = _dot(hh, wrh) + (_dot(hl, wrh) + _dot(hh, wrl_ref[...])) + br_ref[...]
    lane = lax.broadcasted_iota(jnp.int32, logits.shape, 1).astype(F32)
    neg = -jnp.inf
    nl = float(LANES)
    lg = jnp.where(lane < N_GROUPS, logits, neg)
    gmax = jnp.max(lg, axis=-1, keepdims=True)
    grp = jnp.min(jnp.where(lg == gmax, lane, nl), axis=-1, keepdims=True)
    p_grp = 1.0 / jnp.sum(jnp.exp(lg - gmax), axis=-1, keepdims=True)
    first = N_GROUPS + grp * EXPERTS_PER_GROUP
    le = jnp.where((lane >= first) & (lane < first + EXPERTS_PER_GROUP), logits, neg)
    emax = jnp.max(le, axis=-1, keepdims=True)
    i1 = jnp.min(jnp.where(le == emax, lane, nl), axis=-1, keepdims=True)
    esum = jnp.sum(jnp.exp(le - emax), axis=-1, keepdims=True)
    le2 = jnp.where(lane == i1, neg, le)
    e2max = jnp.max(le2, axis=-1, keepdims=True)
    i2 = jnp.min(jnp.where(le2 == e2max, lane, nl), axis=-1, keepdims=True)
    p1 = 1.0 / esum
    p2 = jnp.exp(e2max - emax) / esum
    w1 = p_grp * p1 / (p1 + p2)
    w2 = p_grp * p2 / (p1 + p2)
    e1 = i1 - N_GROUPS
    e2 = i2 - N_GROUPS
    hot1 = (lane == e1).astype(F32)
    hot2 = (lane == e2).astype(F32)
    both = hot1 + hot2
    tm = logits.shape[0]
    ti = lax.broadcasted_iota(jnp.int32, (tm, tm), 0)
    tj = lax.broadcasted_iota(jnp.int32, (tm, tm), 1)
    earlier = _dot((ti > tj).astype(BF16), both.astype(BF16)) + cnt_ref[...]
    rank1 = jnp.sum(hot1 * earlier, axis=-1, keepdims=True)
    rank2 = jnp.sum(hot2 * (earlier + hot1), axis=-1, keepdims=True)
    cnt_ref[...] = cnt_ref[...] + jnp.sum(both, axis=0, keepdims=True)
    out = jnp.where(lane == 0.0, e1, 0.0)
    out = jnp.where(lane == 1.0, e2, out)
    out = jnp.where(lane == 2.0, w1, out)
    out = jnp.where(lane == 3.0, w2, out)
    out = jnp.where(lane == 4.0, rank1, out)
    out = jnp.where(lane == 5.0, rank2, out)
    route_ref[...] = out


def _post(oa16, ob16, x2d, mod8, g_post_mix, g_pre_ffn, w_out16, wr_hi, wr_lo, b_r, cnt0, tm, seq_rows,
          mod_row0):
    n = x2d.shape[0]
    row = lambda i: (i, 0)
    const = lambda i: (0, 0)
    return pl.pallas_call(
        _post_kernel,
        grid=(n // tm,),
        in_specs=[pl.BlockSpec((tm, DN_WIDTH), row),
                  pl.BlockSpec((tm, SB_WIDTH), row),
                  pl.BlockSpec((tm, D_MODEL), row),
                  _mod_spec(tm, seq_rows, mod_row0),
                  pl.BlockSpec((1, D_MODEL), const),
                  pl.BlockSpec((1, D_MODEL), const),
                  pl.BlockSpec((D_MODEL, D_MODEL), const),
                  pl.BlockSpec((D_MODEL, LANES), const),
                  pl.BlockSpec((D_MODEL, LANES), const),
                  pl.BlockSpec((1, LANES), const),
                  pl.BlockSpec((1, LANES), const)],
        out_specs=[pl.BlockSpec((tm, D_MODEL), row),
                   pl.BlockSpec((tm, D_MODEL), row),
                   pl.BlockSpec((tm, LANES), row),
                   pl.BlockSpec((1, LANES), const)],
        out_shape=[jax.ShapeDtypeStruct((n, D_MODEL), F32),
                   jax.ShapeDtypeStruct((n, D_MODEL), F32),
                   jax.ShapeDtypeStruct((n, LANES), F32),
                   jax.ShapeDtypeStruct((1, LANES), F32)],
        compiler_params=_cparams(("arbitrary",)),
        name="post",
    )(oa16, ob16, x2d, mod8, g_post_mix, g_pre_ffn, w_out16, wr_hi, wr_lo, b_r, cnt0)


def _dispatch_kernel(seg_ref, dp_ref, ds_ref, hp_ref, hs_ref, xs_hbm, zbuf, stage, sem, stage_sems, *,
                     n_blocks, n_prompt_steps):
    i = pl.program_id(0)
    last_step = pl.num_programs(0) - 1

    @pl.when(i == 0)
    def _():
        zbuf[...] = jnp.zeros_like(zbuf)

        def zero_block(row0):
            return pltpu.make_async_copy(zbuf, xs_hbm.at[pl.ds(pl.multiple_of(row0, MOE_BLOCK), MOE_BLOCK), :], sem)

        for e in range(N_EXPERTS):
            @pl.when(seg_ref[e] > 0)
            def _():
                zero_block(seg_ref[N_EXPERTS + e] - MOE_BLOCK).start()
        for e in range(N_EXPERTS):
            @pl.when(seg_ref[e] > 0)
            def _():
                zero_block(seg_ref[N_EXPERTS + e] - MOE_BLOCK).wait()

        used = seg_ref[2 * N_EXPERTS - 1] // MOE_BLOCK

        def fill(b, c):
            cp = zero_block(b * MOE_BLOCK)
            cp.start()
            cp.wait()
            return c

        lax.fori_loop(used, n_blocks, fill, 0)

    def scatter(h_ref, dest_ref, sem_):
        for t in range(h_ref.shape[0]):
            for slot in range(2):
                pltpu.make_async_copy(h_ref.at[pl.ds(t, 1), :],
                                      xs_hbm.at[pl.ds(dest_ref[0, 0, 2 * t + slot], 1), :],
                                      sem_).start(priority=slot)

    def drain(h_ref, sem_):
        for slot in range(2):
            pltpu.make_async_copy(h_ref, xs_hbm.at[pl.ds(0, h_ref.shape[0]), :], sem_).wait()

    for buf in range(2):
        @pl.when((i < last_step) & (i % 2 == buf))
        def _():
            @pl.when(i >= 2)
            def _():
                drain(stage.at[buf], stage_sems.at[buf])
            stage[buf] = hp_ref[...]
            scatter(stage.at[buf], dp_ref, stage_sems.at[buf])

    @pl.when(i == last_step)
    def _():
        scatter(hs_ref, ds_ref, sem)
        drain(hs_ref, sem)
        for buf in range(min(2, n_prompt_steps)):
            drain(stage.at[buf], stage_sems.at[buf])


def _dispatch(seg, dest_p, dest_s, h2p, h2s, n_blocks, tm):
    n_p, n_s = h2p.shape[0], h2s.shape[0]
    steps_p = n_p // tm
    pmap3 = lambda i, sg: (jnp.minimum(i, steps_p - 1), 0, 0)
    pmap2 = lambda i, sg: (jnp.minimum(i, steps_p - 1), 0)
    grid_spec = pltpu.PrefetchScalarGridSpec(
        num_scalar_prefetch=1,
        grid=(steps_p + 1,),
        in_specs=[pl.BlockSpec((1, 1, 2 * tm), pmap3, memory_space=pltpu.SMEM),
                  pl.BlockSpec((1, 1, 2 * n_s), lambda i, sg: (0, 0, 0), memory_space=pltpu.SMEM),
                  pl.BlockSpec((tm, D_MODEL), pmap2),
                  pl.BlockSpec((n_s, D_MODEL), lambda i, sg: (0, 0))],
        out_specs=pl.BlockSpec(memory_space=pl.ANY),
        scratch_shapes=[pltpu.VMEM((MOE_BLOCK, D_MODEL), F32), pltpu.VMEM((2, tm, D_MODEL), F32),
                        pltpu.SemaphoreType.DMA(()), pltpu.SemaphoreType.DMA((2,))])
    return pl.pallas_call(
        functools.partial(_dispatch_kernel, n_blocks=n_blocks, n_prompt_steps=steps_p),
        grid_spec=grid_spec,
        out_shape=jax.ShapeDtypeStruct((n_blocks * MOE_BLOCK, D_MODEL), F32),
        compiler_params=_cparams(("arbitrary",)),
        name="dispatch",
    )(seg, dest_p.reshape(steps_p, 1, 2 * tm), dest_s.reshape(1, 1, 2 * n_s), h2p, h2s)


def _moe_kernel(blk_e_ref, nvalid_ref, x_ref, wg_ref, wu_ref, wd_ref, y_ref, wg16, wu16, wd16):
    i = pl.program_id(0)
    e = blk_e_ref[i]
    e_prev = blk_e_ref[jnp.maximum(i - 1, 0)]

    @pl.when((i == 0) | (e != e_prev))
    def _():
        wg16[...] = wg_ref[0].astype(BF16)
        wu16[...] = wu_ref[0].astype(BF16)
        wd16[...] = wd_ref[0].astype(BF16)

    @pl.when(nvalid_ref[i] > 0)
    def _():
        xb = x_ref[...].astype(BF16)
        g = _dot(xb, wg16[...])
        u = _dot(xb, wu16[...])
        hmid = (_silu(g) * u).astype(BF16)
        y_ref[...] = _dot(hmid, wd16[...])

    @pl.when(nvalid_ref[i] == 0)
    def _():
        y_ref[...] = jnp.zeros_like(y_ref)


def _moe(blk_e, nvalid, x_sorted, w_gate, w_up, w_down):
    n_blocks = blk_e.shape[0]
    wmap = lambda i, be, nv: (be[i], 0, 0)
    xmap = lambda i, be, nv: (jnp.where(nv[i] > 0, i, 0), 0)
    grid_spec = pltpu.PrefetchScalarGridSpec(
        num_scalar_prefetch=2,
        grid=(n_blocks,),
        in_specs=[pl.BlockSpec((MOE_BLOCK, D_MODEL), xmap),
                  pl.BlockSpec((1, D_MODEL, D_EXPERT), wmap),
                  pl.BlockSpec((1, D_MODEL, D_EXPERT), wmap),
                  pl.BlockSpec((1, D_EXPERT, D_MODEL), wmap)],
        out_specs=pl.BlockSpec((MOE_BLOCK, D_MODEL), lambda i, be, nv: (i, 0)),
        scratch_shapes=[pltpu.VMEM((D_MODEL, D_EXPERT), BF16),
                        pltpu.VMEM((D_MODEL, D_EXPERT), BF16),
                        pltpu.VMEM((D_EXPERT, D_MODEL), BF16)])
    return pl.pallas_call(
        _moe_kernel,
        grid_spec=grid_spec,
        out_shape=jax.ShapeDtypeStruct((n_blocks * MOE_BLOCK, D_MODEL), F32),
        compiler_params=_cparams(("arbitrary",)),
        name="moe",
    )(blk_e, nvalid, x_sorted, w_gate, w_up, w_down)


def _combine_kernel(dcur_ref, dnext_ref, route_ref, x1_ref, mod_ref, g_ref, y_hbm, o_ref, ybuf, sems):
    i = pl.program_id(0)
    last = pl.num_programs(0) - 1
    tm = x1_ref.shape[0]

    def gather(dest_ref, buf):
        for t in range(tm):
            for slot in range(2):
                pltpu.make_async_copy(y_hbm.at[pl.ds(dest_ref[0, 0, 2 * t + slot], 1), :],
                                      ybuf.at[buf, slot, pl.ds(t, 1), :], sems.at[buf]).start(priority=slot)

    @pl.when(i == 0)
    def _():
        gather(dcur_ref, 0)

    for buf in range(2):
        @pl.when(i % 2 == buf)
        def _():
            @pl.when(i < last)
            def _():
                gather(dnext_ref, 1 - buf)

            for slot in range(2):
                pltpu.make_async_copy(y_hbm.at[pl.ds(0, tm), :], ybuf.at[buf, slot], sems.at[buf]).wait()
            route = route_ref[...]
            moe = ybuf[buf, 0] * route[:, 2:3] + ybuf[buf, 1] * route[:, 3:4]
            o_ref[...] = x1_ref[...] + _per_seq(_rms(moe, g_ref[...]), mod_ref[...], lambda y, m: y * m[:, 5:6])


def _combine(dest, route, y_sorted, x1, mod8, g_post_ffn, tm, seq_rows, mod_row0):
    n = x1.shape[0]
    steps = n // tm
    row = lambda i: (i, 0)
    dest3 = dest.reshape(steps, 1, 2 * tm)
    return pl.pallas_call(
        _combine_kernel,
        grid=(steps,),
        in_specs=[pl.BlockSpec((1, 1, 2 * tm), lambda i: (i, 0, 0), memory_space=pltpu.SMEM),
                  pl.BlockSpec((1, 1, 2 * tm), lambda i: (jnp.minimum(i + 1, steps - 1), 0, 0),
                               memory_space=pltpu.SMEM),
                  pl.BlockSpec((tm, LANES), row),
                  pl.BlockSpec((tm, D_MODEL), row),
                  _mod_spec(tm, seq_rows, mod_row0),
                  pl.BlockSpec((1, D_MODEL), lambda i: (0, 0)),
                  pl.BlockSpec(memory_space=pl.ANY)],
        out_specs=pl.BlockSpec((tm, D_MODEL), row),
        out_shape=jax.ShapeDtypeStruct((n, D_MODEL), F32),
        scratch_shapes=[pltpu.VMEM((2, 2, tm, D_MODEL), F32), pltpu.SemaphoreType.DMA((2,))],
        compiler_params=_cparams(("arbitrary",)),
        name="combine",
    )(dest3, dest3, route, x1, mod8, g_post_ffn, y_sorted)


def _segment_plan(counts_f, n_blocks):
    counts = counts_f[0, :N_EXPERTS].astype(jnp.int32)
    padded = (counts + MOE_BLOCK - 1) // MOE_BLOCK * MOE_BLOCK
    pad_end = jnp.cumsum(padded)
    pad_start = pad_end - padded
    blk_start = jnp.arange(n_blocks, dtype=jnp.int32) * MOE_BLOCK
    blk_e = jnp.minimum(jnp.sum((pad_end[None, :] <= blk_start[:, None]).astype(jnp.int32), axis=1),
                        N_EXPERTS - 1)
    onehot = blk_e[:, None] == jnp.arange(N_EXPERTS, dtype=jnp.int32)[None, :]
    c_blk = jnp.sum(jnp.where(onehot, counts[None, :], 0), axis=1)
    s_blk = jnp.sum(jnp.where(onehot, pad_start[None, :], 0), axis=1)
    nvalid = jnp.clip(c_blk - (blk_start - s_blk), 0, MOE_BLOCK).astype(jnp.int32)
    seg = jnp.concatenate([counts, pad_end]).astype(jnp.int32)
    return blk_e.astype(jnp.int32), nvalid, seg, pad_start


def _token_rows(route, pad_start):
    eid = route[:, 0:2].astype(jnp.int32)
    rank = route[:, 4:6].astype(jnp.int32)
    onehot = eid[:, :, None] == jnp.arange(N_EXPERTS, dtype=jnp.int32)[None, None, :]
    return rank + jnp.sum(jnp.where(onehot, pad_start[None, None, :], 0), axis=2)


def _layer(x_p, x_s, c_p, c_s, k_past, v_past, s0_s, conv_s, p):
    bp, tp, d = x_p.shape
    bs, ts, _ = x_s.shape
    n_p, n_s = bp * tp, bs * ts
    n_tok = n_p + n_s

    n_seq = bp + bs
    c_all = jnp.zeros((16, d), F32).at[:n_seq].set(jnp.concatenate([c_s, c_p], axis=0))
    mod = _ada(c_all, p['w_ada'], p['b_ada'])
    mod8 = jnp.pad(mod.reshape(16, 6, d), ((0, 0), (0, 2), (0, 0)))

    w_in = p['w_in']
    o_b = CONV_CH + DN_WIDTH
    o_q = o_b + 2 * DN_HEADS
    wb = jnp.pad(w_in[:, o_b:o_q], ((0, 0), (0, LANES - 2 * DN_HEADS)))
    wb_hi = wb.astype(BF16)
    wb_lo = (wb - wb_hi.astype(F32)).astype(BF16)
    w_main = jnp.concatenate([w_in[:, :o_b].astype(BF16), w_in[:, o_q:].astype(BF16), wb_hi, wb_lo], axis=1)
    g_pre_mix = p['g_pre_mix'].reshape(1, d)

    conv_w8 = jnp.pad(p['conv_w'], ((0, 8 - CONV_W), (0, 0)))
    pad_g = lambda a: jnp.pad(a.reshape(1, DN_HEADS), ((0, 0), (DN_HEADS, LANES - 2 * DN_HEADS)))
    alog_row, dtb_row = pad_g(p['a_log']), pad_g(p['dt_bias'])
    onorm_a = p['onorm_a'].reshape(1, DN_DIM)
    onorm_b2 = jnp.tile(p['onorm_b'].reshape(1, SB_DIM), (1, 2))

    w_out16 = p['w_out'].astype(BF16)
    wr = jnp.pad(jnp.concatenate([p['w_router_group'], p['w_router_expert']], axis=1),
                 ((0, 0), (0, LANES - N_GROUPS - N_EXPERTS)))
    wr_hi = wr.astype(BF16)
    wr_lo = (wr - wr_hi.astype(F32)).astype(BF16)
    b_r = jnp.pad(jnp.concatenate([p['b_router_group'], p['b_router_expert']]).reshape(1, -1),
                  ((0, 0), (0, LANES - N_GROUPS - N_EXPERTS)))
    g_post_mix = p['g_post_mix'].reshape(1, d)
    g_pre_ffn = p['g_pre_ffn'].reshape(1, d)
    g_post_ffn = p['g_post_ffn'].reshape(1, d)

    def mixer(x, tm, tm_post, seq_rows, mod_row0, hist8, s0, k_old, v_old, chunk, n_chunks, bq, n_sub, cnt0):
        b, t, _ = x.shape
        x2d = x.reshape(b * t, d)
        a_in, z, bg, q16, kb, vb, k16, v16 = _proj(x2d, mod8, g_pre_mix, w_main, wb_hi, tm, seq_rows, mod_row0)
        r3 = lambda a: a.reshape(b, t, a.shape[-1])
        oa16, s_new = _delta(r3(a_in), r3(z), r3(bg), hist8, s0, conv_w8, alog_row, dtb_row, onorm_a,
                             chunk, n_chunks)
        k16, v16 = r3(k16), r3(v16)
        if k_old is not None:
            k16 = jnp.concatenate([k_old.reshape(b, -1, SB_WIDTH).astype(BF16), k16], axis=1)
            v16 = jnp.concatenate([v_old.reshape(b, -1, SB_WIDTH).astype(BF16), v16], axis=1)
        n_pad = (-k16.shape[1]) % KEY_TILE
        k16 = jnp.pad(k16, ((0, 0), (n_pad, 0), (0, 0)))
        v16 = jnp.pad(v16, ((0, 0), (n_pad, 0), (0, 0)))
        ob16 = _sb_attn(r3(q16), k16, v16, onorm_b2, bq, n_sub, n_pad)
        x1, h2, route, cnt = _post(oa16.reshape(b * t, DN_WIDTH), ob16.reshape(b * t, SB_WIDTH), x2d, mod8,
                                   g_post_mix, g_pre_ffn, w_out16, wr_hi, wr_lo, b_r, cnt0, tm_post, seq_rows,
                                   mod_row0)
        new_conv = r3(a_in)[:, t - (CONV_W - 1):, :]
        return (x1, h2, route, cnt, kb.reshape(b, t, SB_HEADS, SB_DIM), vb.reshape(b, t, SB_HEADS, SB_DIM),
                s_new, new_conv)

    zero_hist = jnp.zeros((bp, 8, CONV_CH), F32)
    zero_s = jnp.zeros((bp, DN_HEADS, DN_DIM, DN_DIM), F32)
    hist_s = jnp.pad(conv_s, ((0, 0), (8 - (CONV_W - 1), 0), (0, 0)))
    tm_p = min(ROW_TILE, tp)
    tm_dense = DENSE_TILE if n_p % DENSE_TILE == 0 else tm_p
    tm_router = ROUTER_TILE if n_p % ROUTER_TILE == 0 else tm_dense
    nc_p = max(1, min(8, tp // DELTA_BLOCK))
    x1p, h2p, rp, cnt_p, kp, vp, sp, cp = mixer(x_p, tm_dense, tm_router, tp, bs, zero_hist, zero_s, None, None,
                                                 min(DELTA_BLOCK, tp), nc_p, min(KEY_TILE // 2, tp), 2,
                                                 jnp.zeros((1, LANES), F32))
    x1s, h2s, rs, cnt, ks, vs, ss, cs = mixer(x_s, n_s, n_s, ts, 0, hist_s, s0_s, k_past, v_past,
                                               min(DELTA_BLOCK, ts), max(1, ts // DELTA_BLOCK),
                                               min(KEY_TILE, ts), 1, cnt_p)

    n_blocks = -(-2 * n_tok // MOE_BLOCK) + N_EXPERTS
    blk_e, nvalid, seg, pad_start = _segment_plan(cnt, n_blocks)
    dest_p = _token_rows(rp, pad_start)
    dest_s = _token_rows(rs, pad_start)
    x_sorted = _dispatch(seg, dest_p, dest_s, h2p, h2s, n_blocks, tm_p)
    y_sorted = _moe(blk_e, nvalid, x_sorted, p['w_gate'], p['w_up'], p['w_down'])
    y_p = _combine(dest_p, rp, y_sorted, x1p, mod8, g_post_ffn, tm_p, tp, bs).reshape(bp, tp, d)
    y_s = _combine(dest_s, rs, y_sorted, x1s, mod8, g_post_ffn, n_s, ts, 0).reshape(bs, ts, d)
    return y_p, y_s, kp, vp, sp, cp, ks, vs, ss, cs


def kernel(x_prompt, x_sample, c_prompt, c_sample, cache_k, cache_v, state_delta, state_conv, w_ada, b_ada, g_pre_mix, g_post_mix, g_pre_ffn, g_post_ffn, w_in, conv_w, a_log, dt_bias, onorm_a, onorm_b, w_out, w_router_group, b_router_group, w_router_expert, b_router_expert, w_gate, w_up, w_down):
    depth = w_in.shape[0]
    y_p, y_s = x_prompt, x_sample
    outs = [[] for _ in range(8)]
    for l in range(depth):
        p = dict(w_ada=w_ada[l], b_ada=b_ada[l], g_pre_mix=g_pre_mix[l], g_post_mix=g_post_mix[l],
                 g_pre_ffn=g_pre_ffn[l], g_post_ffn=g_post_ffn[l], w_in=w_in[l], conv_w=conv_w[l],
                 a_log=a_log[l], dt_bias=dt_bias[l], onorm_a=onorm_a[l], onorm_b=onorm_b[l],
                 w_out=w_out[l], w_router_group=w_router_group[l], b_router_group=b_router_group[l],
                 w_router_expert=w_router_expert[l], b_router_expert=b_router_expert[l],
                 w_gate=w_gate[l], w_up=w_up[l], w_down=w_down[l])
        res = _layer(y_p, y_s, c_prompt, c_sample, cache_k[l], cache_v[l], state_delta[l], state_conv[l], p)
        y_p, y_s = res[0], res[1]
        for lst, r in zip(outs, res[2:]):
            lst.append(r)
    return (y_p, y_s) + tuple(jnp.stack(o) for o in outs)
```

```python
import functools

import jax
import jax.numpy as jnp
from jax import lax
from jax.experimental import pallas as pl
from jax.experimental.pallas import tpu as pltpu

F32 = jnp.float32
BF16 = jnp.bfloat16

D_MODEL = 1024
DN_HEADS = 4
DN_DIM = 128
CONV_W = 4
DN_WIDTH = DN_HEADS * DN_DIM
CONV_CH = 3 * DN_WIDTH
DELTA_BLOCK = 64
SB_HEADS = 8
SB_DIM = 64
SB_WIDTH = SB_HEADS * SB_DIM
N_GROUPS = 4
EXPERTS_PER_GROUP = 8
N_EXPERTS = N_GROUPS * EXPERTS_PER_GROUP
D_EXPERT = D_MODEL // 2
MOE_BLOCK = 256
EPS = 1e-6

LANES = 128
KEY_TILE = 128
ATTN_WINDOW = 3 * KEY_TILE
WINDOW_BLOCKS = 4
EXP_ZERO_BELOW = -104.0
VMEM_LIMIT = 56 * 1024 * 1024
DENSE_TILE = 512
ROUTER_TILE = 1024
ROW_TILE = 256


def _cparams(sem):
    return pltpu.CompilerParams(dimension_semantics=sem, vmem_limit_bytes=VMEM_LIMIT)


def _split(a):
    hi = a.astype(BF16)
    lo = (a - hi.astype(F32)).astype(BF16)
    return hi, lo


def _dot(a, b, dims=(((1,), (0,)), ((), ()))):
    return lax.dot_general(a, b, dims, preferred_element_type=F32)


def _dot_small_int_lhs(a, b):
    a16 = a.astype(BF16)
    return _dot(jnp.concatenate([a16, a16], axis=1), jnp.concatenate(_split(b), axis=0))


_NT = (((1,), (1,)), ((), ()))
_TN = (((0,), (0,)), ((), ()))


def _silu(x):
    return x * jax.nn.sigmoid(x)


SOFTPLUS_LINEAR_ABOVE = 80.0


def _softplus(x):
    return jnp.where(x > SOFTPLUS_LINEAR_ABOVE, x, jnp.log(1.0 + jnp.exp(jnp.minimum(x, SOFTPLUS_LINEAR_ABOVE))))


def _ada_kernel(c_ref, w_ref, b_ref, o_ref):
    s = _silu(c_ref[...]).astype(BF16)
    o_ref[...] = _dot(s, w_ref[...].astype(BF16)) + b_ref[...]


def _ada(c_all, w_ada, b_ada):
    rows = c_all.shape[0]
    n = w_ada.shape[1]
    tn = 1024
    return pl.pallas_call(
        _ada_kernel,
        grid=(n // tn,),
        in_specs=[pl.BlockSpec((rows, D_MODEL), lambda j: (0, 0)),
                  pl.BlockSpec((D_MODEL, tn), lambda j: (0, j)),
                  pl.BlockSpec((1, tn), lambda j: (0, j))],
        out_specs=pl.BlockSpec((rows, tn), lambda j: (0, j)),
        out_shape=jax.ShapeDtypeStruct((rows, n), F32),
        compiler_params=_cparams(("arbitrary",)),
        name="ada",
    )(c_all, w_ada, b_ada.reshape(1, n))


def _rms(x, gain):
    return x * lax.rsqrt(jnp.mean(x * x, axis=-1, keepdims=True) + EPS) * gain


def _per_seq(y, mod, fn):
    n_seq = mod.shape[0]
    ys = y.reshape(n_seq, y.shape[0] // n_seq, y.shape[1])
    return fn(ys, mod).reshape(y.shape)


def _mod_spec(tm, seq_rows, mod_row0):
    if tm >= seq_rows:
        n_seq = tm // seq_rows
        assert tm % seq_rows == 0 and mod_row0 % n_seq == 0
        return pl.BlockSpec((n_seq, 8, D_MODEL), lambda i: (mod_row0 // n_seq + i, 0, 0))
    assert seq_rows % tm == 0
    return pl.BlockSpec((1, 8, D_MODEL), lambda i: (mod_row0 + (i * tm) // seq_rows, 0, 0))


def _proj_kernel(x_ref, mod_ref, g_ref, wm_ref, wbh_ref,
                 a_ref, z_ref, bg_ref, q_ref, k_ref, v_ref, k16_ref, v16_ref):
    h = _per_seq(_rms(x_ref[...], g_ref[...]), mod_ref[...], lambda y, m: y * (1.0 + m[:, 1:2]) + m[:, 0:1])
    hh, hl = _split(h)
    p = _dot(hh, wm_ref[...])
    a_ref[...] = p[:, 0:CONV_CH]
    z_ref[...] = p[:, CONV_CH:CONV_CH + DN_WIDTH]
    o = CONV_CH + DN_WIDTH
    q_ref[...] = (p[:, o:o + SB_WIDTH] * (SB_DIM ** -0.5)).astype(BF16)
    k = p[:, o + SB_WIDTH:o + 2 * SB_WIDTH]
    v = p[:, o + 2 * SB_WIDTH:o + 3 * SB_WIDTH]
    k_ref[...] = k
    v_ref[...] = v
    k16_ref[...] = k.astype(BF16)
    v16_ref[...] = v.astype(BF16)
    o += 3 * SB_WIDTH
    bg_ref[...] = p[:, o:o + LANES] + (_dot(hl, wbh_ref[...]) + p[:, o + LANES:])


def _proj(x2d, mod8, g_pre, w_main, wb_hi, tm, seq_rows, mod_row0):
    n = x2d.shape[0]
    nm = w_main.shape[1]
    row = lambda i: (i, 0)
    const = lambda i: (0, 0)
    outs = [(CONV_CH, F32), (DN_WIDTH, F32), (LANES, F32), (SB_WIDTH, BF16), (SB_WIDTH, F32), (SB_WIDTH, F32),
            (SB_WIDTH, BF16), (SB_WIDTH, BF16)]
    return pl.pallas_call(
        _proj_kernel,
        grid=(n // tm,),
        in_specs=[pl.BlockSpec((tm, D_MODEL), row),
                  _mod_spec(tm, seq_rows, mod_row0),
                  pl.BlockSpec((1, D_MODEL), const),
                  pl.BlockSpec((D_MODEL, nm), const),
                  pl.BlockSpec((D_MODEL, LANES), const)],
        out_specs=[pl.BlockSpec((tm, w), row) for w, _ in outs],
        out_shape=[jax.ShapeDtypeStruct((n, w), dt) for w, dt in outs],
        compiler_params=_cparams(("arbitrary",)),
        name="proj",
    )(x2d, mod8, g_pre, w_main, wb_hi)


def _delta_kernel(a_ref, z_ref, bg_ref, hist0_ref, s0_ref, cw_ref, alog_ref, dtb_ref, on_ref,
                  o_ref, sfin_ref, hist_sc, s_sc, *, chunk, n_chunks):
    t_idx = pl.program_id(1)
    tt = chunk * n_chunks

    @pl.when(t_idx == 0)
    def _():
        hist_sc[...] = hist0_ref[0]
        s_sc[...] = s0_ref[0]

    x = a_ref[0]
    xx = jnp.concatenate([hist_sc[...], x], axis=0)
    cw = cw_ref[...]
    conv = x * cw[CONV_W - 1:CONV_W]
    for s in range(1, CONV_W):
        conv = conv + pltpu.roll(xx, s, 0)[8:] * cw[CONV_W - 1 - s:CONV_W - s]
    conv = _silu(conv)
    hist_sc[...] = x[tt - 8:tt]

    bg = bg_ref[0]
    lane = lax.broadcasted_iota(jnp.int32, (1, LANES), 1)
    g_lane = (lane >= DN_HEADS) & (lane < 2 * DN_HEADS)
    neg_a = jnp.where(g_lane, -jnp.exp(alog_ref[...]), 0.0)
    beta_all = jax.nn.sigmoid(bg)
    g_all = neg_a * _softplus(bg + dtb_ref[...])

    ri = lax.broadcasted_iota(jnp.int32, (chunk, chunk), 0)
    ci = lax.broadcasted_iota(jnp.int32, (chunk, chunk), 1)
    incl = ri >= ci
    strict = ri > ci
    tri = incl.astype(F32)
    eye = (ri == ci).astype(F32)
    ones_cc = jnp.ones((chunk, chunk), F32)
    onorm = on_ref[...]

    rows = lambda c: slice(c * chunk, (c + 1) * chunk)
    hs = range(DN_HEADS)
    prep = {}

    def prepare(chunk_ids):
        units = [(c, h) for c in chunk_ids for h in hs]
        gc_all = {c: _dot_small_int_lhs(tri, g_all[rows(c)]) for c in chunk_ids}
        yield
        diag = {c: jnp.concatenate([eye * gc_all[c][:, DN_HEADS + h:DN_HEADS + h + 1] for h in hs], axis=1)
                for c in chunk_ids}
        gc_rows = {c: _dot_small_int_lhs(ones_cc, diag[c]) for c in chunk_ids}
        yield
        q, k, v, beta, gcol, glast, eg, decay, kbeta = {}, {}, {}, {}, {}, {}, {}, {}, {}
        for (c, h) in units:
            lo = h * DN_DIM
            qq = conv[rows(c), lo:lo + DN_DIM]
            kk = conv[rows(c), DN_WIDTH + lo:DN_WIDTH + lo + DN_DIM]
            u_ = (c, h)
            q[u_] = qq * lax.rsqrt(jnp.sum(qq * qq, axis=-1, keepdims=True) + EPS) * (DN_DIM ** -0.5)
            k[u_] = kk * lax.rsqrt(jnp.sum(kk * kk, axis=-1, keepdims=True) + EPS)
            v[u_] = conv[rows(c), 2 * DN_WIDTH + lo:2 * DN_WIDTH + lo + DN_DIM]
            beta[u_] = beta_all[rows(c), h:h + 1]
            gcol[u_] = gc_all[c][:, DN_HEADS + h:DN_HEADS + h + 1]
            glast[u_] = gcol[u_][chunk - 1:chunk, :]
            eg[u_] = jnp.exp(gcol[u_])
            grow = gc_rows[c][:, h * chunk:(h + 1) * chunk]
            decay[u_] = jnp.where(incl, jnp.exp(jnp.minimum(gcol[u_] - grow, 0.0)), 0.0)
            kbeta[u_] = k[u_] * beta[u_]

        k16 = {u_: k[u_].astype(BF16) for u_ in units}
        kq = {u_: _dot(jnp.concatenate([kbeta[u_], q[u_]], axis=0).astype(BF16), k16[u_], _NT) for u_ in units}
        yield
        lower = {u_: jnp.where(strict, kq[u_][:chunk] * decay[u_], 0.0) for u_ in units}
        attn16 = {u_: jnp.where(incl, kq[u_][chunk:] * decay[u_], 0.0).astype(BF16) for u_ in units}

        sol = {u_: jnp.concatenate([v[u_] * beta[u_], kbeta[u_] * eg[u_]], axis=1) for u_ in units}
        lp = lower
        p = 1
        while p < chunk:
            lsp = {u_: _split(lp[u_]) for u_ in units}
            ssp = {u_: _split(sol[u_]) for u_ in units}
            lcat = {u_: jnp.concatenate([lsp[u_][0], lsp[u_][1], lsp[u_][0]], axis=1) for u_ in units}
            upd = {u_: _dot(lcat[u_], jnp.concatenate([ssp[u_][0], ssp[u_][0], ssp[u_][1]], axis=0))
                   for u_ in units}
            if 2 * p < chunk:
                lp = {u_: _dot(lcat[u_], jnp.concatenate([lsp[u_][0], lsp[u_][0], lsp[u_][1]], axis=0))
                      for u_ in units}
            yield
            sol = {u_: (sol[u_] - upd[u_]) if p == 1 else (sol[u_] + upd[u_]) for u_ in units}
            p *= 2

        for u_ in units:
            prep[u_] = dict(
                u=sol[u_][:, :DN_DIM],
                wq=jnp.concatenate([sol[u_][:, DN_DIM:], q[u_] * eg[u_]], axis=0).astype(BF16),
                kd=(k[u_] * jnp.exp(glast[u_] - gcol[u_])).astype(BF16),
                attn=attn16[u_], decay_all=jnp.exp(glast[u_]))

    def recur(chunk_ids):
        for c in chunk_ids:
            s_old = [s_sc[h] for h in hs]
            ws = [_dot(prep[(c, h)]['wq'], s_old[h].astype(BF16)) for h in hs]
            yield
            v_new = [(prep[(c, h)]['u'] - ws[h][:chunk]).astype(BF16) for h in hs]
            o_in = [_dot(prep[(c, h)]['attn'], v_new[h]) for h in hs]
            ds = [_dot(prep[(c, h)]['kd'], v_new[h], _TN) for h in hs]
            yield
            for h in hs:
                lo = h * DN_DIM
                s_sc[h] = s_old[h] * prep[(c, h)]['decay_all'] + ds[h]
                o = ws[h][chunk:] + o_in[h]
                zg = z_ref[0, rows(c), lo:lo + DN_DIM]
                o_ref[0, rows(c), lo:lo + DN_DIM] = (_rms(o, onorm) * _silu(zg)).astype(BF16)

    def run(*streams):
        live = list(streams)
        while live:
            for g_ in list(live):
                if next(g_, done_) is done_:
                    live.remove(g_)

    done_ = object()
    ids = list(range(n_chunks))
    first, second = (ids[:n_chunks // 2], ids[n_chunks // 2:]) if n_chunks >= 2 else (ids, [])
    run(prepare(first))
    run(recur(first), prepare(second))
    run(recur(second))

    @pl.when(t_idx == pl.num_programs(1) - 1)
    def _():
        sfin_ref[0] = s_sc[...]


def _delta(a_in, z, bg, hist8, s0, conv_w8, alog_row, dtb_row, onorm_a, chunk, n_chunks):
    b, t, _ = a_in.shape
    tt = chunk * n_chunks
    tile = lambda bi, ti: (bi, ti, 0)
    per_b3 = lambda bi, ti: (bi, 0, 0)
    per_b4 = lambda bi, ti: (bi, 0, 0, 0)
    const = lambda bi, ti: (0, 0)
    kern = functools.partial(_delta_kernel, chunk=chunk, n_chunks=n_chunks)
    return pl.pallas_call(
        kern,
        grid=(b, t // tt),
        in_specs=[pl.BlockSpec((1, tt, CONV_CH), tile),
                  pl.BlockSpec((1, tt, DN_WIDTH), tile),
                  pl.BlockSpec((1, tt, LANES), tile),
                  pl.BlockSpec((1, 8, CONV_CH), per_b3),
                  pl.BlockSpec((1, DN_HEADS, DN_DIM, DN_DIM), per_b4),
                  pl.BlockSpec((8, CONV_CH), const),
                  pl.BlockSpec((1, LANES), const),
                  pl.BlockSpec((1, LANES), const),
                  pl.BlockSpec((1, DN_DIM), const)],
        out_specs=[pl.BlockSpec((1, tt, DN_WIDTH), tile),
                   pl.BlockSpec((1, DN_HEADS, DN_DIM, DN_DIM), per_b4)],
        out_shape=[jax.ShapeDtypeStruct((b, t, DN_WIDTH), BF16),
                   jax.ShapeDtypeStruct((b, DN_HEADS, DN_DIM, DN_DIM), F32)],
        scratch_shapes=[pltpu.VMEM((8, CONV_CH), F32),
                        pltpu.VMEM((DN_HEADS, DN_DIM, DN_DIM), F32)],
        compiler_params=_cparams(("arbitrary", "arbitrary")),
        name="delta",
    )(a_in, z, bg, hist8, s0, conv_w8, alog_row, dtb_row, onorm_a)


def _sb_kernel(q_ref, *refs, bq, n_sub, n_pad, q_off):
    kwin = refs[:WINDOW_BLOCKS]
    vwin = refs[WINDOW_BLOCKS:2 * WINDOW_BLOCKS]
    k_hbm, v_hbm, on_ref, o_ref, kbuf, vbuf, qsel, acc, carry, sem = refs[2 * WINDOW_BLOCKS:]
    b = pl.program_id(0)
    i = pl.program_id(1)
    g = bq * n_sub
    qend_step = q_off + (i + 1) * g
    win_start = qend_step - WINDOW_BLOCKS * KEY_TILE
    n_pairs = SB_HEADS // 2
    heads = range(SB_HEADS)

    half_lane = lax.broadcasted_iota(jnp.int32, (bq, LANES), 1) < SB_DIM
    rj = lax.broadcasted_iota(jnp.int32, (2 * KEY_TILE, 2 * KEY_TILE), 0) % KEY_TILE
    cj = lax.broadcasted_iota(jnp.int32, (2 * KEY_TILE, 2 * KEY_TILE), 1)
    suffix2 = ((rj > cj) | (cj >= KEY_TILE)).astype(BF16)

    def window(blocks, off, p):
        parts = []
        for blk in range(WINDOW_BLOCKS):
            lo, hi = max(off, blk * KEY_TILE), min(off + ATTN_WINDOW, (blk + 1) * KEY_TILE)
            if lo < hi:
                ref = blocks[WINDOW_BLOCKS - 1 - blk]
                parts.append(ref[0, lo - blk * KEY_TILE:hi - blk * KEY_TILE, p * LANES:(p + 1) * LANES])
        return jnp.concatenate(parts, axis=0)

    n_t = ATTN_WINDOW // KEY_TILE
    row = lax.broadcasted_iota(jnp.int32, (bq, ATTN_WINDOW), 0)
    col = lax.broadcasted_iota(jnp.int32, (bq, ATTN_WINDOW), 1)
    causal = col < row + (ATTN_WINDOW - bq)
    subs = range(n_sub)
    offs = [WINDOW_BLOCKS * KEY_TILE - g + (s + 1) * bq - ATTN_WINDOW for s in subs]
    mask = [causal & (col >= n_pad - (win_start + offs[s])) for s in subs]
    zs, sp, cs_all = {}, {}, {}
    for s in subs:
        for p in range(n_pairs):
            qf = q_ref[0, s * bq:(s + 1) * bq, p * LANES:(p + 1) * LANES].astype(F32)
            qq = jnp.concatenate([jnp.where(half_lane, qf, 0.0), jnp.where(half_lane, 0.0, qf)], axis=0)
            zz = _dot(qq.astype(BF16), window(kwin, offs[s], p), _NT)
            zs[(s, 2 * p)], zs[(s, 2 * p + 1)] = zz[:bq], zz[bq:]
    for s in subs:
        pieces = []
        for h in heads:
            sp[(s, h)] = _softplus(zs[(s, h)])
            hi, lo = _split(jnp.where(mask[s], -sp[(s, h)], 0.0))
            for j in range(n_t):
                c0 = ATTN_WINDOW - (j + 1) * KEY_TILE
                pieces.append(jnp.concatenate([hi[:, c0:c0 + KEY_TILE], lo[:, c0:c0 + KEY_TILE]], axis=1))
        cs_all[s] = _dot(jnp.concatenate(pieces, axis=0), suffix2)
    done = []
    for s in subs:
        worst = jnp.full((bq, KEY_TILE), -jnp.inf, F32)
        a = {}
        for h in heads:
            run = None
            cols = []
            for j in range(n_t):
                r0 = (h * n_t + j) * bq
                cs = cs_all[s][r0:r0 + bq]
                cols.append(cs[:, :KEY_TILE] if run is None else cs[:, :KEY_TILE] + run)
                run = cs[:, KEY_TILE:] if run is None else run + cs[:, KEY_TILE:]
            within = jnp.concatenate(cols[::-1], axis=1)
            a[h] = jnp.where(mask[s], jnp.exp((zs[(s, h)] - sp[(s, h)]) + within), 0.0).astype(BF16)
            carry[s, h] = run
            worst = jnp.maximum(worst, run)
        for p in range(n_pairs):
            pv = _dot(jnp.concatenate([a[2 * p], a[2 * p + 1]], axis=0), window(vwin, offs[s], p))
            acc[s, p] = jnp.where(half_lane, pv[:bq], pv[bq:])
        done.append((jnp.max(worst) < EXP_ZERO_BELOW).astype(jnp.int32))

    rj1 = lax.broadcasted_iota(jnp.int32, (KEY_TILE, 2 * KEY_TILE), 0)
    cj1 = lax.broadcasted_iota(jnp.int32, (KEY_TILE, 2 * KEY_TILE), 1)
    suffix1 = ((rj1 > cj1) | (cj1 >= KEY_TILE)).astype(BF16)
    col1 = lax.broadcasted_iota(jnp.int32, (bq, KEY_TILE), 1)
    for s in subs:
        swept_from = win_start + offs[s]

        @pl.when((done[s] == 0) & (swept_from > n_pad))
        def _():
            for p in range(n_pairs):
                qf = q_ref[0, s * bq:(s + 1) * bq, p * LANES:(p + 1) * LANES].astype(F32)
                qsel[2 * p] = jnp.where(half_lane, qf, 0.0).astype(BF16)
                qsel[2 * p + 1] = jnp.where(half_lane, 0.0, qf).astype(BF16)

            def body(state):
                upper, _ = state
                start = jnp.maximum(upper - KEY_TILE, 0)
                copies = []
                for p in range(n_pairs):
                    for src, dst in ((k_hbm, kbuf), (v_hbm, vbuf)):
                        cp = pltpu.make_async_copy(
                            src.at[b, pl.ds(pl.multiple_of(start, 16), KEY_TILE), pl.ds(p * LANES, LANES)],
                            dst.at[p], sem)
                        cp.start()
                        copies.append(cp)
                for cp in copies:
                    cp.wait()
                kpos = start + col1
                m1 = (kpos < upper) & (kpos >= n_pad)
                zz = [_dot(qsel[h], kbuf[h // 2], _NT) for h in heads]
                spp = [_softplus(zz[h]) for h in heads]
                pcs = []
                for h in heads:
                    pcs.extend(_split(jnp.where(m1, -spp[h], 0.0)))
                cs1 = _dot(jnp.concatenate(pcs, axis=0), suffix1)
                worst = jnp.full((bq, KEY_TILE), -jnp.inf, F32)
                aa = []
                for h in heads:
                    c = cs1[2 * h * bq:(2 * h + 1) * bq] + cs1[(2 * h + 1) * bq:(2 * h + 2) * bq]
                    c_old = carry[s, h]
                    aa.append(jnp.where(m1, jnp.exp((zz[h] - spp[h]) + c[:, :KEY_TILE] + c_old), 0.0).astype(BF16))
                    c_new = c_old + c[:, KEY_TILE:]
                    carry[s, h] = c_new
                    worst = jnp.maximum(worst, c_new)
                for p in range(n_pairs):
                    pv0 = _dot(aa[2 * p], vbuf[p])
                    pv1 = _dot(aa[2 * p + 1], vbuf[p])
                    acc[s, p] = acc[s, p] + jnp.where(half_lane, pv0, pv1)
                return start, (jnp.max(worst) < EXP_ZERO_BELOW).astype(jnp.int32)

            lax.while_loop(lambda st: (st[0] > n_pad) & (st[1] == 0), body, (swept_from, jnp.int32(0)))

    onb = on_ref[...]
    for s in subs:
        for p in range(n_pairs):
            o = acc[s, p]
            sq = o * o
            s_lo = jnp.sum(jnp.where(half_lane, sq, 0.0), axis=-1, keepdims=True)
            s_hi = jnp.sum(jnp.where(half_lane, 0.0, sq), axis=-1, keepdims=True)
            ms = jnp.where(half_lane, s_lo, s_hi) * (1.0 / SB_DIM)
            o_ref[0, s * bq:(s + 1) * bq, p * LANES:(p + 1) * LANES] = (o * lax.rsqrt(ms + EPS) * onb).astype(BF16)


def _sb_attn(q16, k16p, v16p, onorm_b2, bq, n_sub, n_pad):
    b, tq, _ = q16.shape
    tkp = k16p.shape[1]
    q_off = tkp - tq
    g = bq * n_sub
    assert tq % g == 0 and g <= KEY_TILE and bq % 16 == 0
    assert all((q_off + (i + 1) * g) % KEY_TILE == 0 for i in range(tq // g))

    def kmap(back):
        def f(bi, i):
            last = (q_off + (i + 1) * g) // KEY_TILE - 1
            return (bi, jnp.maximum(last - back, 0), 0)
        return f

    qmap = lambda bi, i: (bi, i, 0)
    kern = functools.partial(_sb_kernel, bq=bq, n_sub=n_sub, n_pad=n_pad, q_off=q_off)
    kspec = [pl.BlockSpec((1, KEY_TILE, SB_WIDTH), kmap(back)) for back in range(WINDOW_BLOCKS)]
    return pl.pallas_call(
        kern,
        grid=(b, tq // g),
        in_specs=[pl.BlockSpec((1, g, SB_WIDTH), qmap)] + kspec + kspec
                 + [pl.BlockSpec(memory_space=pl.ANY), pl.BlockSpec(memory_space=pl.ANY),
                    pl.BlockSpec((1, LANES), lambda bi, i: (0, 0))],
        out_specs=pl.BlockSpec((1, g, SB_WIDTH), qmap),
        out_shape=jax.ShapeDtypeStruct((b, tq, SB_WIDTH), BF16),
        scratch_shapes=[pltpu.VMEM((SB_HEADS // 2, KEY_TILE, LANES), BF16),
                        pltpu.VMEM((SB_HEADS // 2, KEY_TILE, LANES), BF16),
                        pltpu.VMEM((SB_HEADS, bq, LANES), BF16),
                        pltpu.VMEM((n_sub, SB_HEADS // 2, bq, LANES), F32),
                        pltpu.VMEM((n_sub, SB_HEADS, bq, KEY_TILE), F32),
                        pltpu.SemaphoreType.DMA(())],
        compiler_params=_cparams(("arbitrary", "arbitrary")),
        name="sb_attn",
    )(q16, *([k16p] * WINDOW_BLOCKS), *([v16p] * WINDOW_BLOCKS), k16p, v16p, onorm_b2)


def _post_kernel(oa_ref, ob_ref, x_ref, mod_ref, gpm_ref, gpf_ref, wo_ref, wrh_ref, wrl_ref, br_ref, cnt0_ref,
                 x1_ref, h2_ref, route_ref, cnt_ref):
    @pl.when(pl.program_id(0) == 0)
    def _():
        cnt_ref[...] = cnt0_ref[...]

    mod = mod_ref[...]
    mix = _dot(oa_ref[...], wo_ref[0:DN_WIDTH, :]) + _dot(ob_ref[...], wo_ref[DN_WIDTH:DN_WIDTH + SB_WIDTH, :])
    x1 = x_ref[...] + _per_seq(_rms(mix, gpm_ref[...]), mod, lambda y, m: y * m[:, 2:3])
    x1_ref[...] = x1
    h2 = _per_seq(_rms(x1, gpf_ref[...]), mod, lambda y, m: y * (1.0 + m[:, 4:5]) + m[:, 3:4])
    h2_ref[...] = h2
    hh, hl = _split(h2)
    wrh = wrh_ref[...]
    logits = _dot(hh, wrh) + (_dot(hl, wrh) + _dot(hh, wrl_ref[...])) + br_ref[...]
    lane = lax.broadcasted_iota(jnp.int32, logits.shape, 1).astype(F32)
    neg = -jnp.inf
    nl = float(LANES)
    lg = jnp.where(lane < N_GROUPS, logits, neg)
    gmax = jnp.max(lg, axis=-1, keepdims=True)
    grp = jnp.min(jnp.where(lg == gmax, lane, nl), axis=-1, keepdims=True)
    p_grp = 1.0 / jnp.sum(jnp.exp(lg - gmax), axis=-1, keepdims=True)
    first = N_GROUPS + grp * EXPERTS_PER_GROUP
    le = jnp.where((lane >= first) & (lane < first + EXPERTS_PER_GROUP), logits, neg)
    emax = jnp.max(le, axis=-1, keepdims=True)
    i1 = jnp.min(jnp.where(le == emax, lane, nl), axis=-1, keepdims=True)
    esum = jnp.sum(jnp.exp(le - emax), axis=-1, keepdims=True)
    le2 = jnp.where(lane == i1, neg, le)
    e2max = jnp.max(le2, axis=-1, keepdims=True)
    i2 = jnp.min(jnp.where(le2 == e2max, lane, nl), axis=-1, keepdims=True)
    p1 = 1.0 / esum
    p2 = jnp.exp(e2max - emax) / esum
    w1 = p_grp * p1 / (p1 + p2)
    w2 = p_grp * p2 / (p1 + p2)
    e1 = i1 - N_GROUPS
    e2 = i2 - N_GROUPS
    hot1 = (lane == e1).astype(F32)
    hot2 = (lane == e2).astype(F32)
    both = hot1 + hot2
    tm = logits.shape[0]
    ti = lax.broadcasted_iota(jnp.int32, (tm, tm), 0)
    tj = lax.broadcasted_iota(jnp.int32, (tm, tm), 1)
    earlier = _dot((ti > tj).astype(BF16), both.astype(BF16)) + cnt_ref[...]
    rank1 = jnp.sum(hot1 * earlier, axis=-1, keepdims=True)
    rank2 = jnp.sum(hot2 * (earlier + hot1), axis=-1, keepdims=True)
    cnt_ref[...] = cnt_ref[...] + jnp.sum(both, axis=0, keepdims=True)
    out = jnp.where(lane == 0.0, e1, 0.0)
    out = jnp.where(lane == 1.0, e2, out)
    out = jnp.where(lane == 2.0, w1, out)
    out = jnp.where(lane == 3.0, w2, out)
    out = jnp.where(lane == 4.0, rank1, out)
    out = jnp.where(lane == 5.0, rank2, out)
    route_ref[...] = out


def _post(oa16, ob16, x2d, mod8, g_post_mix, g_pre_ffn, w_out16, wr_hi, wr_lo, b_r, cnt0, tm, seq_rows,
          mod_row0):
    n = x2d.shape[0]
    row = lambda i: (i, 0)
    const = lambda i: (0, 0)
    return pl.pallas_call(
        _post_kernel,
        grid=(n // tm,),
        in_specs=[pl.BlockSpec((tm, DN_WIDTH), row),
                  pl.BlockSpec((tm, SB_WIDTH), row),
                  pl.BlockSpec((tm, D_MODEL), row),
                  _mod_spec(tm, seq_rows, mod_row0),
                  pl.BlockSpec((1, D_MODEL), const),
                  pl.BlockSpec((1, D_MODEL), const),
                  pl.BlockSpec((D_MODEL, D_MODEL), const),
                  pl.BlockSpec((D_MODEL, LANES), const),
                  pl.BlockSpec((D_MODEL, LANES), const),
                  pl.BlockSpec((1, LANES), const),
                  pl.BlockSpec((1, LANES), const)],
        out_specs=[pl.BlockSpec((tm, D_MODEL), row),
                   pl.BlockSpec((tm, D_MODEL), row),
                   pl.BlockSpec((tm, LANES), row),
                   pl.BlockSpec((1, LANES), const)],
        out_shape=[jax.ShapeDtypeStruct((n, D_MODEL), F32),
                   jax.ShapeDtypeStruct((n, D_MODEL), F32),
                   jax.ShapeDtypeStruct((n, LANES), F32),
                   jax.ShapeDtypeStruct((1, LANES), F32)],
        compiler_params=_cparams(("arbitrary",)),
        name="post",
    )(oa16, ob16, x2d, mod8, g_post_mix, g_pre_ffn, w_out16, wr_hi, wr_lo, b_r, cnt0)


def _dispatch_kernel(seg_ref, dp_ref, ds_ref, hp_ref, hs_ref, xs_hbm, zbuf, stage, sem, stage_sems, *,
                     n_blocks, n_prompt_steps):
    i = pl.program_id(0)
    last_step = pl.num_programs(0) - 1

    @pl.when(i == 0)
    def _():
        zbuf[...] = jnp.zeros_like(zbuf)

        def zero_block(row0):
            return pltpu.make_async_copy(zbuf, xs_hbm.at[pl.ds(pl.multiple_of(row0, MOE_BLOCK), MOE_BLOCK), :], sem)

        for e in range(N_EXPERTS):
            @pl.when(seg_ref[e] > 0)
            def _():
                zero_block(seg_ref[N_EXPERTS + e] - MOE_BLOCK).start()
        for e in range(N_EXPERTS):
            @pl.when(seg_ref[e] > 0)
            def _():
                zero_block(seg_ref[N_EXPERTS + e] - MOE_BLOCK).wait()

        used = seg_ref[2 * N_EXPERTS - 1] // MOE_BLOCK

        def fill(b, c):
            cp = zero_block(b * MOE_BLOCK)
            cp.start()
            cp.wait()
            return c

        lax.fori_loop(used, n_blocks, fill, 0)

    def scatter(h_ref, dest_ref, sem_):
        for t in range(h_ref.shape[0]):
            for slot in range(2):
                pltpu.make_async_copy(h_ref.at[pl.ds(t, 1), :],
                                      xs_hbm.at[pl.ds(dest_ref[0, 0, 2 * t + slot], 1), :],
                                      sem_).start(priority=slot)

    def drain(h_ref, sem_):
        for slot in range(2):
            pltpu.make_async_copy(h_ref, xs_hbm.at[pl.ds(0, h_ref.shape[0]), :], sem_).wait()

    for buf in range(2):
        @pl.when((i < last_step) & (i % 2 == buf))
        def _():
            @pl.when(i >= 2)
            def _():
                drain(stage.at[buf], stage_sems.at[buf])
            stage[buf] = hp_ref[...]
            scatter(stage.at[buf], dp_ref, stage_sems.at[buf])

    @pl.when(i == last_step)
    def _():
        scatter(hs_ref, ds_ref, sem)
        drain(hs_ref, sem)
        for buf in range(min(2, n_prompt_steps)):
            drain(stage.at[buf], stage_sems.at[buf])


def _dispatch(seg, dest_p, dest_s, h2p, h2s, n_blocks, tm):
    n_p, n_s = h2p.shape[0], h2s.shape[0]
    steps_p = n_p // tm
    pmap3 = lambda i, sg: (jnp.minimum(i, steps_p - 1), 0, 0)
    pmap2 = lambda i, sg: (jnp.minimum(i, steps_p - 1), 0)
    grid_spec = pltpu.PrefetchScalarGridSpec(
        num_scalar_prefetch=1,
        grid=(steps_p + 1,),
        in_specs=[pl.BlockSpec((1, 1, 2 * tm), pmap3, memory_space=pltpu.SMEM),
                  pl.BlockSpec((1, 1, 2 * n_s), lambda i, sg: (0, 0, 0), memory_space=pltpu.SMEM),
                  pl.BlockSpec((tm, D_MODEL), pmap2),
                  pl.BlockSpec((n_s, D_MODEL), lambda i, sg: (0, 0))],
        out_specs=pl.BlockSpec(memory_space=pl.ANY),
        scratch_shapes=[pltpu.VMEM((MOE_BLOCK, D_MODEL), F32), pltpu.VMEM((2, tm, D_MODEL), F32),
                        pltpu.SemaphoreType.DMA(()), pltpu.SemaphoreType.DMA((2,))])
    return pl.pallas_call(
        functools.partial(_dispatch_kernel, n_blocks=n_blocks, n_prompt_steps=steps_p),
        grid_spec=grid_spec,
        out_shape=jax.ShapeDtypeStruct((n_blocks * MOE_BLOCK, D_MODEL), F32),
        compiler_params=_cparams(("arbitrary",)),
        name="dispatch",
    )(seg, dest_p.reshape(steps_p, 1, 2 * tm), dest_s.reshape(1, 1, 2 * n_s), h2p, h2s)


def _moe_kernel(blk_e_ref, nvalid_ref, x_ref, wg_ref, wu_ref, wd_ref, y_ref, wg16, wu16, wd16):
    i = pl.program_id(0)
    e = blk_e_ref[i]
    e_prev = blk_e_ref[jnp.maximum(i - 1, 0)]

    @pl.when((i == 0) | (e != e_prev))
    def _():
        wg16[...] = wg_ref[0].astype(BF16)
        wu16[...] = wu_ref[0].astype(BF16)
        wd16[...] = wd_ref[0].astype(BF16)

    @pl.when(nvalid_ref[i] > 0)
    def _():
        xb = x_ref[...].astype(BF16)
        g = _dot(xb, wg16[...])
        u = _dot(xb, wu16[...])
        hmid = (_silu(g) * u).astype(BF16)
        y_ref[...] = _dot(hmid, wd16[...])

    @pl.when(nvalid_ref[i] == 0)
    def _():
        y_ref[...] = jnp.zeros_like(y_ref)


def _moe(blk_e, nvalid, x_sorted, w_gate, w_up, w_down):
    n_blocks = blk_e.shape[0]
    wmap = lambda i, be, nv: (be[i], 0, 0)
    xmap = lambda i, be, nv: (jnp.where(nv[i] > 0, i, 0), 0)
    grid_spec = pltpu.PrefetchScalarGridSpec(
        num_scalar_prefetch=2,
        grid=(n_blocks,),
        in_specs=[pl.BlockSpec((MOE_BLOCK, D_MODEL), xmap),
                  pl.BlockSpec((1, D_MODEL, D_EXPERT), wmap),
                  pl.BlockSpec((1, D_MODEL, D_EXPERT), wmap),
                  pl.BlockSpec((1, D_EXPERT, D_MODEL), wmap)],
        out_specs=pl.BlockSpec((MOE_BLOCK, D_MODEL), lambda i, be, nv: (i, 0)),
        scratch_shapes=[pltpu.VMEM((D_MODEL, D_EXPERT), BF16),
                        pltpu.VMEM((D_MODEL, D_EXPERT), BF16),
                        pltpu.VMEM((D_EXPERT, D_MODEL), BF16)])
    return pl.pallas_call(
        _moe_kernel,
        grid_spec=grid_spec,
        out_shape=jax.ShapeDtypeStruct((n_blocks * MOE_BLOCK, D_MODEL), F32),
        compiler_params=_cparams(("arbitrary",)),
        name="moe",
    )(blk_e, nvalid, x_sorted, w_gate, w_up, w_down)


def _combine_kernel(dcur_ref, dnext_ref, route_ref, x1_ref, mod_ref, g_ref, y_hbm, o_ref, ybuf, sems):
    i = pl.program_id(0)
    last = pl.num_programs(0) - 1
    tm = x1_ref.shape[0]

    def gather(dest_ref, buf):
        for t in range(tm):
            for slot in range(2):
                pltpu.make_async_copy(y_hbm.at[pl.ds(dest_ref[0, 0, 2 * t + slot], 1), :],
                                      ybuf.at[buf, slot, pl.ds(t, 1), :], sems.at[buf]).start(priority=slot)

    @pl.when(i == 0)
    def _():
        gather(dcur_ref, 0)

    for buf in range(2):
        @pl.when(i % 2 == buf)
        def _():
            @pl.when(i < last)
            def _():
                gather(dnext_ref, 1 - buf)

            for slot in range(2):
                pltpu.make_async_copy(y_hbm.at[pl.ds(0, tm), :], ybuf.at[buf, slot], sems.at[buf]).wait()
            route = route_ref[...]
            moe = ybuf[buf, 0] * route[:, 2:3] + ybuf[buf, 1] * route[:, 3:4]
            o_ref[...] = x1_ref[...] + _per_seq(_rms(moe, g_ref[...]), mod_ref[...], lambda y, m: y * m[:, 5:6])


def _combine(dest, route, y_sorted, x1, mod8, g_post_ffn, tm, seq_rows, mod_row0):
    n = x1.shape[0]
    steps = n // tm
    row = lambda i: (i, 0)
    dest3 = dest.reshape(steps, 1, 2 * tm)
    return pl.pallas_call(
        _combine_kernel,
        grid=(steps,),
        in_specs=[pl.BlockSpec((1, 1, 2 * tm), lambda i: (i, 0, 0), memory_space=pltpu.SMEM),
                  pl.BlockSpec((1, 1, 2 * tm), lambda i: (jnp.minimum(i + 1, steps - 1), 0, 0),
                               memory_space=pltpu.SMEM),
                  pl.BlockSpec((tm, LANES), row),
                  pl.BlockSpec((tm, D_MODEL), row),
                  _mod_spec(tm, seq_rows, mod_row0),
                  pl.BlockSpec((1, D_MODEL), lambda i: (0, 0)),
                  pl.BlockSpec(memory_space=pl.ANY)],
        out_specs=pl.BlockSpec((tm, D_MODEL), row),
        out_shape=jax.ShapeDtypeStruct((n, D_MODEL), F32),
        scratch_shapes=[pltpu.VMEM((2, 2, tm, D_MODEL), F32), pltpu.SemaphoreType.DMA((2,))],
        compiler_params=_cparams(("arbitrary",)),
        name="combine",
    )(dest3, dest3, route, x1, mod8, g_post_ffn, y_sorted)


def _segment_plan(counts_f, n_blocks):
    counts = counts_f[0, :N_EXPERTS].astype(jnp.int32)
    padded = (counts + MOE_BLOCK - 1) // MOE_BLOCK * MOE_BLOCK
    pad_end = jnp.cumsum(padded)
    pad_start = pad_end - padded
    blk_start = jnp.arange(n_blocks, dtype=jnp.int32) * MOE_BLOCK
    blk_e = jnp.minimum(jnp.sum((pad_end[None, :] <= blk_start[:, None]).astype(jnp.int32), axis=1),
                        N_EXPERTS - 1)
    onehot = blk_e[:, None] == jnp.arange(N_EXPERTS, dtype=jnp.int32)[None, :]
    c_blk = jnp.sum(jnp.where(onehot, counts[None, :], 0), axis=1)
    s_blk = jnp.sum(jnp.where(onehot, pad_start[None, :], 0), axis=1)
    nvalid = jnp.clip(c_blk - (blk_start - s_blk), 0, MOE_BLOCK).astype(jnp.int32)
    seg = jnp.concatenate([counts, pad_end]).astype(jnp.int32)
    return blk_e.astype(jnp.int32), nvalid, seg, pad_start


def _token_rows(route, pad_start):
    eid = route[:, 0:2].astype(jnp.int32)
    rank = route[:, 4:6].astype(jnp.int32)
    onehot = eid[:, :, None] == jnp.arange(N_EXPERTS, dtype=jnp.int32)[None, None, :]
    return rank + jnp.sum(jnp.where(onehot, pad_start[None, None, :], 0), axis=2)


def _layer(x_p, x_s, c_p, c_s, k_past, v_past, s0_s, conv_s, p):
    bp, tp, d = x_p.shape
    bs, ts, _ = x_s.shape
    n_p, n_s = bp * tp, bs * ts
    n_tok = n_p + n_s

    n_seq = bp + bs
    c_all = jnp.zeros((16, d), F32).at[:n_seq].set(jnp.concatenate([c_s, c_p], axis=0))
    mod = _ada(c_all, p['w_ada'], p['b_ada'])
    mod8 = jnp.pad(mod.reshape(16, 6, d), ((0, 0), (0, 2), (0, 0)))

    w_in = p['w_in']
    o_b = CONV_CH + DN_WIDTH
    o_q = o_b + 2 * DN_HEADS
    wb = jnp.pad(w_in[:, o_b:o_q], ((0, 0), (0, LANES - 2 * DN_HEADS)))
    wb_hi = wb.astype(BF16)
    wb_lo = (wb - wb_hi.astype(F32)).astype(BF16)
    w_main = jnp.concatenate([w_in[:, :o_b].astype(BF16), w_in[:, o_q:].astype(BF16), wb_hi, wb_lo], axis=1)
    g_pre_mix = p['g_pre_mix'].reshape(1, d)

    conv_w8 = jnp.pad(p['conv_w'], ((0, 8 - CONV_W), (0, 0)))
    pad_g = lambda a: jnp.pad(a.reshape(1, DN_HEADS), ((0, 0), (DN_HEADS, LANES - 2 * DN_HEADS)))
    alog_row, dtb_row = pad_g(p['a_log']), pad_g(p['dt_bias'])
    onorm_a = p['onorm_a'].reshape(1, DN_DIM)
    onorm_b2 = jnp.tile(p['onorm_b'].reshape(1, SB_DIM), (1, 2))

    w_out16 = p['w_out'].astype(BF16)
    wr = jnp.pad(jnp.concatenate([p['w_router_group'], p['w_router_expert']], axis=1),
                 ((0, 0), (0, LANES - N_GROUPS - N_EXPERTS)))
    wr_hi = wr.astype(BF16)
    wr_lo = (wr - wr_hi.astype(F32)).astype(BF16)
    b_r = jnp.pad(jnp.concatenate([p['b_router_group'], p['b_router_expert']]).reshape(1, -1),
                  ((0, 0), (0, LANES - N_GROUPS - N_EXPERTS)))
    g_post_mix = p['g_post_mix'].reshape(1, d)
    g_pre_ffn = p['g_pre_ffn'].reshape(1, d)
    g_post_ffn = p['g_post_ffn'].reshape(1, d)

    def mixer(x, tm, tm_post, seq_rows, mod_row0, hist8, s0, k_old, v_old, chunk, n_chunks, bq, n_sub, cnt0):
        b, t, _ = x.shape
        x2d = x.reshape(b * t, d)
        a_in, z, bg, q16, kb, vb, k16, v16 = _proj(x2d, mod8, g_pre_mix, w_main, wb_hi, tm, seq_rows, mod_row0)
        r3 = lambda a: a.reshape(b, t, a.shape[-1])
        oa16, s_new = _delta(r3(a_in), r3(z), r3(bg), hist8, s0, conv_w8, alog_row, dtb_row, onorm_a,
                             chunk, n_chunks)
        k16, v16 = r3(k16), r3(v16)
        if k_old is not None:
            k16 = jnp.concatenate([k_old.reshape(b, -1, SB_WIDTH).astype(BF16), k16], axis=1)
            v16 = jnp.concatenate([v_old.reshape(b, -1, SB_WIDTH).astype(BF16), v16], axis=1)
        n_pad = (-k16.shape[1]) % KEY_TILE
        k16 = jnp.pad(k16, ((0, 0), (n_pad, 0), (0, 0)))
        v16 = jnp.pad(v16, ((0, 0), (n_pad, 0), (0, 0)))
        ob16 = _sb_attn(r3(q16), k16, v16, onorm_b2, bq, n_sub, n_pad)
        x1, h2, route, cnt = _post(oa16.reshape(b * t, DN_WIDTH), ob16.reshape(b * t, SB_WIDTH), x2d, mod8,
                                   g_post_mix, g_pre_ffn, w_out16, wr_hi, wr_lo, b_r, cnt0, tm_post, seq_rows,
                                   mod_row0)
        new_conv = r3(a_in)[:, t - (CONV_W - 1):, :]
        return (x1, h2, route, cnt, kb.reshape(b, t, SB_HEADS, SB_DIM), vb.reshape(b, t, SB_HEADS, SB_DIM),
                s_new, new_conv)

    zero_hist = jnp.zeros((bp, 8, CONV_CH), F32)
    zero_s = jnp.zeros((bp, DN_HEADS, DN_DIM, DN_DIM), F32)
    hist_s = jnp.pad(conv_s, ((0, 0), (8 - (CONV_W - 1), 0), (0, 0)))
    tm_p = min(ROW_TILE, tp)
    tm_dense = DENSE_TILE if n_p % DENSE_TILE == 0 else tm_p
    tm_router = ROUTER_TILE if n_p % ROUTER_TILE == 0 else tm_dense
    nc_p = max(1, min(8, tp // DELTA_BLOCK))
    x1p, h2p, rp, cnt_p, kp, vp, sp, cp = mixer(x_p, tm_dense, tm_router, tp, bs, zero_hist, zero_s, None, None,
                                                 min(DELTA_BLOCK, tp), nc_p, min(KEY_TILE // 2, tp), 2,
                                                 jnp.zeros((1, LANES), F32))
    x1s, h2s, rs, cnt, ks, vs, ss, cs = mixer(x_s, n_s, n_s, ts, 0, hist_s, s0_s, k_past, v_past,
                                               min(DELTA_BLOCK, ts), max(1, ts // DELTA_BLOCK),
                                               min(KEY_TILE, ts), 1, cnt_p)

    n_blocks = -(-2 * n_tok // MOE_BLOCK) + N_EXPERTS
    blk_e, nvalid, seg, pad_start = _segment_plan(cnt, n_blocks)
    dest_p = _token_rows(rp, pad_start)
    dest_s = _token_rows(rs, pad_start)
    x_sorted = _dispatch(seg, dest_p, dest_s, h2p, h2s, n_blocks, tm_p)
    y_sorted = _moe(blk_e, nvalid, x_sorted, p['w_gate'], p['w_up'], p['w_down'])
    y_p = _combine(dest_p, rp, y_sorted, x1p, mod8, g_post_ffn, tm_p, tp, bs).reshape(bp, tp, d)
    y_s = _combine(dest_s, rs, y_sorted, x1s, mod8, g_post_ffn, n_s, ts, 0).reshape(bs, ts, d)
    return y_p, y_s, kp, vp, sp, cp, ks, vs, ss, cs


def kernel(x_prompt, x_sample, c_prompt, c_sample, cache_k, cache_v, state_delta, state_conv, w_ada, b_ada, g_pre_mix, g_post_mix, g_pre_ffn, g_post_ffn, w_in, conv_w, a_log, dt_bias, onorm_a, onorm_b, w_out, w_router_group, b_router_group, w_router_expert, b_router_expert, w_gate, w_up, w_down):
    depth = w_in.shape[0]
    y_p, y_s = x_prompt, x_sample
    outs = [[] for _ in range(8)]
    for l in range(depth):
        p = dict(w_ada=w_ada[l], b_ada=b_ada[l], g_pre_mix=g_pre_mix[l], g_post_mix=g_post_mix[l],
                 g_pre_ffn=g_pre_ffn[l], g_post_ffn=g_post_ffn[l], w_in=w_in[l], conv_w=conv_w[l],
                 a_log=a_log[l], dt_bias=dt_bias[l], onorm_a=onorm_a[l], onorm_b=onorm_b[l],
                 w_out=w_out[l], w_router_group=w_router_group[l], b_router_group=b_router_group[l],
                 w_router_expert=w_router_expert[l], b_router_expert=b_router_expert[l],
                 w_gate=w_gate[l], w_up=w_up[l], w_down=w_down[l])
        res = _layer(y_p, y_s, c_prompt, c_sample, cache_k[l], cache_v[l], state_delta[l], state_conv[l], p)
        y_p, y_s = res[0], res[1]
        for lst, r in zip(outs, res[2:]):
            lst.append(r)
    return (y_p, y_s) + tuple(jnp.stack(o) for o in outs)
```

```python
import functools

import jax
import jax.numpy as jnp
from jax import lax
from jax.experimental import pallas as pl
from jax.experimental.pallas import tpu as pltpu

F32 = jnp.float32
BF16 = jnp.bfloat16

D_MODEL = 1024
DN_HEADS = 4
DN_DIM = 128
CONV_W = 4
DN_WIDTH = DN_HEADS * DN_DIM
CONV_CH = 3 * DN_WIDTH
DELTA_BLOCK = 64
SB_HEADS = 8
SB_DIM = 64
SB_WIDTH = SB_HEADS * SB_DIM
N_GROUPS = 4
EXPERTS_PER_GROUP = 8
N_EXPERTS = N_GROUPS * EXPERTS_PER_GROUP
D_EXPERT = D_MODEL // 2
MOE_BLOCK = 256
EPS = 1e-6

LANES = 128
KEY_TILE = 128
ATTN_WINDOW = 3 * KEY_TILE
WINDOW_BLOCKS = 4
EXP_ZERO_BELOW = -104.0
VMEM_LIMIT = 56 * 1024 * 1024
DENSE_TILE = 512
ROUTER_TILE = 1024
ROW_TILE = 256


def _cparams(sem):
    return pltpu.CompilerParams(dimension_semantics=sem, vmem_limit_bytes=VMEM_LIMIT)


def _split(a):
    hi = a.astype(BF16)
    lo = (a - hi.astype(F32)).astype(BF16)
    return hi, lo


def _dot(a, b, dims=(((1,), (0,)), ((), ()))):
    return lax.dot_general(a, b, dims, preferred_element_type=F32)


def _dot_small_int_lhs(a, b):
    a16 = a.astype(BF16)
    return _dot(jnp.concatenate([a16, a16], axis=1), jnp.concatenate(_split(b), axis=0))


_NT = (((1,), (1,)), ((), ()))
_TN = (((0,), (0,)), ((), ()))


def _silu(x):
    return x * jax.nn.sigmoid(x)


SOFTPLUS_LINEAR_ABOVE = 80.0


def _softplus(x):
    return jnp.where(x > SOFTPLUS_LINEAR_ABOVE, x, jnp.log(1.0 + jnp.exp(jnp.minimum(x, SOFTPLUS_LINEAR_ABOVE))))


def _ada_kernel(c_ref, w_ref, b_ref, o_ref):
    s = _silu(c_ref[...]).astype(BF16)
    o_ref[...] = _dot(s, w_ref[...].astype(BF16)) + b_ref[...]


def _ada(c_all, w_ada, b_ada):
    rows = c_all.shape[0]
    n = w_ada.shape[1]
    tn = 1024
    return pl.pallas_call(
        _ada_kernel,
        grid=(n // tn,),
        in_specs=[pl.BlockSpec((rows, D_MODEL), lambda j: (0, 0)),
                  pl.BlockSpec((D_MODEL, tn), lambda j: (0, j)),
                  pl.BlockSpec((1, tn), lambda j: (0, j))],
        out_specs=pl.BlockSpec((rows, tn), lambda j: (0, j)),
        out_shape=jax.ShapeDtypeStruct((rows, n), F32),
        compiler_params=_cparams(("arbitrary",)),
        name="ada",
    )(c_all, w_ada, b_ada.reshape(1, n))


def _rms(x, gain):
    return x * lax.rsqrt(jnp.mean(x * x, axis=-1, keepdims=True) + EPS) * gain


def _per_seq(y, mod, fn):
    n_seq = mod.shape[0]
    ys = y.reshape(n_seq, y.shape[0] // n_seq, y.shape[1])
    return fn(ys, mod).reshape(y.shape)


def _mod_spec(tm, seq_rows, mod_row0):
    if tm >= seq_rows:
        n_seq = tm // seq_rows
        assert tm % seq_rows == 0 and mod_row0 % n_seq == 0
        return pl.BlockSpec((n_seq, 8, D_MODEL), lambda i: (mod_row0 // n_seq + i, 0, 0))
    assert seq_rows % tm == 0
    return pl.BlockSpec((1, 8, D_MODEL), lambda i: (mod_row0 + (i * tm) // seq_rows, 0, 0))


def _proj_kernel(x_ref, mod_ref, g_ref, wm_ref, wbh_ref,
                 a_ref, z_ref, bg_ref, q_ref, k_ref, v_ref, k16_ref, v16_ref):
    h = _per_seq(_rms(x_ref[...], g_ref[...]), mod_ref[...], lambda y, m: y * (1.0 + m[:, 1:2]) + m[:, 0:1])
    hh, hl = _split(h)
    p = _dot(hh, wm_ref[...])
    a_ref[...] = p[:, 0:CONV_CH]
    z_ref[...] = p[:, CONV_CH:CONV_CH + DN_WIDTH]
    o = CONV_CH + DN_WIDTH
    q_ref[...] = (p[:, o:o + SB_WIDTH] * (SB_DIM ** -0.5)).astype(BF16)
    k = p[:, o + SB_WIDTH:o + 2 * SB_WIDTH]
    v = p[:, o + 2 * SB_WIDTH:o + 3 * SB_WIDTH]
    k_ref[...] = k
    v_ref[...] = v
    k16_ref[...] = k.astype(BF16)
    v16_ref[...] = v.astype(BF16)
    o += 3 * SB_WIDTH
    bg_ref[...] = p[:, o:o + LANES] + (_dot(hl, wbh_ref[...]) + p[:, o + LANES:])


def _proj(x2d, mod8, g_pre, w_main, wb_hi, tm, seq_rows, mod_row0):
    n = x2d.shape[0]
    nm = w_main.shape[1]
    row = lambda i: (i, 0)
    const = lambda i: (0, 0)
    outs = [(CONV_CH, F32), (DN_WIDTH, F32), (LANES, F32), (SB_WIDTH, BF16), (SB_WIDTH, F32), (SB_WIDTH, F32),
            (SB_WIDTH, BF16), (SB_WIDTH, BF16)]
    return pl.pallas_call(
        _proj_kernel,
        grid=(n // tm,),
        in_specs=[pl.BlockSpec((tm, D_MODEL), row),
                  _mod_spec(tm, seq_rows, mod_row0),
                  pl.BlockSpec((1, D_MODEL), const),
                  pl.BlockSpec((D_MODEL, nm), const),
                  pl.BlockSpec((D_MODEL, LANES), const)],
        out_specs=[pl.BlockSpec((tm, w), row) for w, _ in outs],
        out_shape=[jax.ShapeDtypeStruct((n, w), dt) for w, dt in outs],
        compiler_params=_cparams(("arbitrary",)),
        name="proj",
    )(x2d, mod8, g_pre, w_main, wb_hi)


def _delta_kernel(a_ref, z_ref, bg_ref, hist0_ref, s0_ref, cw_ref, alog_ref, dtb_ref, on_ref,
                  o_ref, sfin_ref, hist_sc, s_sc, *, chunk, n_chunks):
    t_idx = pl.program_id(1)
    tt = chunk * n_chunks

    @pl.when(t_idx == 0)
    def _():
        hist_sc[...] = hist0_ref[0]
        s_sc[...] = s0_ref[0]

    x = a_ref[0]
    xx = jnp.concatenate([hist_sc[...], x], axis=0)
    cw = cw_ref[...]
    conv = x * cw[CONV_W - 1:CONV_W]
    for s in range(1, CONV_W):
        conv = conv + pltpu.roll(xx, s, 0)[8:] * cw[CONV_W - 1 - s:CONV_W - s]
    conv = _silu(conv)
    hist_sc[...] = x[tt - 8:tt]

    bg = bg_ref[0]
    lane = lax.broadcasted_iota(jnp.int32, (1, LANES), 1)
    g_lane = (lane >= DN_HEADS) & (lane < 2 * DN_HEADS)
    neg_a = jnp.where(g_lane, -jnp.exp(alog_ref[...]), 0.0)
    beta_all = jax.nn.sigmoid(bg)
    g_all = neg_a * _softplus(bg + dtb_ref[...])

    ri = lax.broadcasted_iota(jnp.int32, (chunk, chunk), 0)
    ci = lax.broadcasted_iota(jnp.int32, (chunk, chunk), 1)
    incl = ri >= ci
    strict = ri > ci
    tri = incl.astype(F32)
    eye = (ri == ci).astype(F32)
    ones_cc = jnp.ones((chunk, chunk), F32)
    onorm = on_ref[...]

    rows = lambda c: slice(c * chunk, (c + 1) * chunk)
    hs = range(DN_HEADS)
    prep = {}

    def prepare(chunk_ids):
        units = [(c, h) for c in chunk_ids for h in hs]
        gc_all = {c: _dot_small_int_lhs(tri, g_all[rows(c)]) for c in chunk_ids}
        yield
        diag = {c: jnp.concatenate([eye * gc_all[c][:, DN_HEADS + h:DN_HEADS + h + 1] for h in hs], axis=1)
                for c in chunk_ids}
        gc_rows = {c: _dot_small_int_lhs(ones_cc, diag[c]) for c in chunk_ids}
        yield
        q, k, v, beta, gcol, glast, eg, decay, kbeta = {}, {}, {}, {}, {}, {}, {}, {}, {}
        for (c, h) in units:
            lo = h * DN_DIM
            qq = conv[rows(c), lo:lo + DN_DIM]
            kk = conv[rows(c), DN_WIDTH + lo:DN_WIDTH + lo + DN_DIM]
            u_ = (c, h)
            q[u_] = qq * lax.rsqrt(jnp.sum(qq * qq, axis=-1, keepdims=True) + EPS) * (DN_DIM ** -0.5)
            k[u_] = kk * lax.rsqrt(jnp.sum(kk * kk, axis=-1, keepdims=True) + EPS)
            v[u_] = conv[rows(c), 2 * DN_WIDTH + lo:2 * DN_WIDTH + lo + DN_DIM]
            beta[u_] = beta_all[rows(c), h:h + 1]
            gcol[u_] = gc_all[c][:, DN_HEADS + h:DN_HEADS + h + 1]
            glast[u_] = gcol[u_][chunk - 1:chunk, :]
            eg[u_] = jnp.exp(gcol[u_])
            grow = gc_rows[c][:, h * chunk:(h + 1) * chunk]
            decay[u_] = jnp.where(incl, jnp.exp(jnp.minimum(gcol[u_] - grow, 0.0)), 0.0)
            kbeta[u_] = k[u_] * beta[u_]

        k16 = {u_: k[u_].astype(BF16) for u_ in units}
        kq = {u_: _dot(jnp.concatenate([kbeta[u_], q[u_]], axis=0).astype(BF16), k16[u_], _NT) for u_ in units}
        yield
        lower = {u_: jnp.where(strict, kq[u_][:chunk] * decay[u_], 0.0) for u_ in units}
        attn16 = {u_: jnp.where(incl, kq[u_][chunk:] * decay[u_], 0.0).astype(BF16) for u_ in units}

        sol = {u_: jnp.concatenate([v[u_] * beta[u_], kbeta[u_] * eg[u_]], axis=1) for u_ in units}
        lp = lower
        p = 1
        while p < chunk:
            lsp = {u_: _split(lp[u_]) for u_ in units}
            ssp = {u_: _split(sol[u_]) for u_ in units}
            lcat = {u_: jnp.concatenate([lsp[u_][0], lsp[u_][1], lsp[u_][0]], axis=1) for u_ in units}
            upd = {u_: _dot(lcat[u_], jnp.concatenate([ssp[u_][0], ssp[u_][0], ssp[u_][1]], axis=0))
                   for u_ in units}
            if 2 * p < chunk:
                lp = {u_: _dot(lcat[u_], jnp.concatenate([lsp[u_][0], lsp[u_][0], lsp[u_][1]], axis=0))
                      for u_ in units}
            yield
            sol = {u_: (sol[u_] - upd[u_]) if p == 1 else (sol[u_] + upd[u_]) for u_ in units}
            p *= 2

        for u_ in units:
            prep[u_] = dict(
                u=sol[u_][:, :DN_DIM],
                wq=jnp.concatenate([sol[u_][:, DN_DIM:], q[u_] * eg[u_]], axis=0).astype(BF16),
                kd=(k[u_] * jnp.exp(glast[u_] - gcol[u_])).astype(BF16),
                attn=attn16[u_], decay_all=jnp.exp(glast[u_]))

    def recur(chunk_ids):
        for c in chunk_ids:
            s_old = [s_sc[h] for h in hs]
            ws = [_dot(prep[(c, h)]['wq'], s_old[h].astype(BF16)) for h in hs]
            yield
            v_new = [(prep[(c, h)]['u'] - ws[h][:chunk]).astype(BF16) for h in hs]
            o_in = [_dot(prep[(c, h)]['attn'], v_new[h]) for h in hs]
            ds = [_dot(prep[(c, h)]['kd'], v_new[h], _TN) for h in hs]
            yield
            for h in hs:
                lo = h * DN_DIM
                s_sc[h] = s_old[h] * prep[(c, h)]['decay_all'] + ds[h]
                o = ws[h][chunk:] + o_in[h]
                zg = z_ref[0, rows(c), lo:lo + DN_DIM]
                o_ref[0, rows(c), lo:lo + DN_DIM] = (_rms(o, onorm) * _silu(zg)).astype(BF16)

    def run(*streams):
        live = list(streams)
        while live:
            for g_ in list(live):
                if next(g_, done_) is done_:
                    live.remove(g_)

    done_ = object()
    ids = list(range(n_chunks))
    first, second = (ids[:n_chunks // 2], ids[n_chunks // 2:]) if n_chunks >= 2 else (ids, [])
    run(prepare(first))
    run(recur(first), prepare(second))
    run(recur(second))

    @pl.when(t_idx == pl.num_programs(1) - 1)
    def _():
        sfin_ref[0] = s_sc[...]


def _delta(a_in, z, bg, hist8, s0, conv_w8, alog_row, dtb_row, onorm_a, chunk, n_chunks):
    b, t, _ = a_in.shape
    tt = chunk * n_chunks
    tile = lambda bi, ti: (bi, ti, 0)
    per_b3 = lambda bi, ti: (bi, 0, 0)
    per_b4 = lambda bi, ti: (bi, 0, 0, 0)
    const = lambda bi, ti: (0, 0)
    kern = functools.partial(_delta_kernel, chunk=chunk, n_chunks=n_chunks)
    return pl.pallas_call(
        kern,
        grid=(b, t // tt),
        in_specs=[pl.BlockSpec((1, tt, CONV_CH), tile),
                  pl.BlockSpec((1, tt, DN_WIDTH), tile),
                  pl.BlockSpec((1, tt, LANES), tile),
                  pl.BlockSpec((1, 8, CONV_CH), per_b3),
                  pl.BlockSpec((1, DN_HEADS, DN_DIM, DN_DIM), per_b4),
                  pl.BlockSpec((8, CONV_CH), const),
                  pl.BlockSpec((1, LANES), const),
                  pl.BlockSpec((1, LANES), const),
                  pl.BlockSpec((1, DN_DIM), const)],
        out_specs=[pl.BlockSpec((1, tt, DN_WIDTH), tile),
                   pl.BlockSpec((1, DN_HEADS, DN_DIM, DN_DIM), per_b4)],
        out_shape=[jax.ShapeDtypeStruct((b, t, DN_WIDTH), BF16),
                   jax.ShapeDtypeStruct((b, DN_HEADS, DN_DIM, DN_DIM), F32)],
        scratch_shapes=[pltpu.VMEM((8, CONV_CH), F32),
                        pltpu.VMEM((DN_HEADS, DN_DIM, DN_DIM), F32)],
        compiler_params=_cparams(("arbitrary", "arbitrary")),
        name="delta",
    )(a_in, z, bg, hist8, s0, conv_w8, alog_row, dtb_row, onorm_a)


def _sb_kernel(q_ref, *refs, bq, n_sub, n_pad, q_off):
    kwin = refs[:WINDOW_BLOCKS]
    vwin = refs[WINDOW_BLOCKS:2 * WINDOW_BLOCKS]
    k_hbm, v_hbm, on_ref, o_ref, kbuf, vbuf, qsel, acc, carry, sem = refs[2 * WINDOW_BLOCKS:]
    b = pl.program_id(0)
    i = pl.program_id(1)
    g = bq * n_sub
    qend_step = q_off + (i + 1) * g
    win_start = qend_step - WINDOW_BLOCKS * KEY_TILE
    n_pairs = SB_HEADS // 2
    heads = range(SB_HEADS)

    half_lane = lax.broadcasted_iota(jnp.int32, (bq, LANES), 1) < SB_DIM
    rj = lax.broadcasted_iota(jnp.int32, (2 * KEY_TILE, 2 * KEY_TILE), 0) % KEY_TILE
    cj = lax.broadcasted_iota(jnp.int32, (2 * KEY_TILE, 2 * KEY_TILE), 1)
    suffix2 = ((rj > cj) | (cj >= KEY_TILE)).astype(BF16)

    def window(blocks, off, p):
        parts = []
        for blk in range(WINDOW_BLOCKS):
            lo, hi = max(off, blk * KEY_TILE), min(off + ATTN_WINDOW, (blk + 1) * KEY_TILE)
            if lo < hi:
                ref = blocks[WINDOW_BLOCKS - 1 - blk]
                parts.append(ref[0, lo - blk * KEY_TILE:hi - blk * KEY_TILE, p * LANES:(p + 1) * LANES])
        return jnp.concatenate(parts, axis=0)

    n_t = ATTN_WINDOW // KEY_TILE
    row = lax.broadcasted_iota(jnp.int32, (bq, ATTN_WINDOW), 0)
    col = lax.broadcasted_iota(jnp.int32, (bq, ATTN_WINDOW), 1)
    causal = col < row + (ATTN_WINDOW - bq)
    subs = range(n_sub)
    offs = [WINDOW_BLOCKS * KEY_TILE - g + (s + 1) * bq - ATTN_WINDOW for s in subs]
    mask = [causal & (col >= n_pad - (win_start + offs[s])) for s in subs]
    zs, sp, cs_all = {}, {}, {}
    for s in subs:
        for p in range(n_pairs):
            qf = q_ref[0, s * bq:(s + 1) * bq, p * LANES:(p + 1) * LANES].astype(F32)
            qq = jnp.concatenate([jnp.where(half_lane, qf, 0.0), jnp.where(half_lane, 0.0, qf)], axis=0)
            zz = _dot(qq.astype(BF16), window(kwin, offs[s], p), _NT)
            zs[(s, 2 * p)], zs[(s, 2 * p + 1)] = zz[:bq], zz[bq:]
    for s in subs:
        pieces = []
        for h in heads:
            sp[(s, h)] = _softplus(zs[(s, h)])
            hi, lo = _split(jnp.where(mask[s], -sp[(s, h)], 0.0))
            for j in range(n_t):
                c0 = ATTN_WINDOW - (j + 1) * KEY_TILE
                pieces.append(jnp.concatenate([hi[:, c0:c0 + KEY_TILE], lo[:, c0:c0 + KEY_TILE]], axis=1))
        cs_all[s] = _dot(jnp.concatenate(pieces, axis=0), suffix2)
    done = []
    for s in subs:
        worst = jnp.full((bq, KEY_TILE), -jnp.inf, F32)
        a = {}
        for h in heads:
            run = None
            cols = []
            for j in range(n_t):
                r0 = (h * n_t + j) * bq
                cs = cs_all[s][r0:r0 + bq]
                cols.append(cs[:, :KEY_TILE] if run is None else cs[:, :KEY_TILE] + run)
                run = cs[:, KEY_TILE:] if run is None else run + cs[:, KEY_TILE:]
            within = jnp.concatenate(cols[::-1], axis=1)
            a[h] = jnp.where(mask[s], jnp.exp((zs[(s, h)] - sp[(s, h)]) + within), 0.0).astype(BF16)
            carry[s, h] = run
            worst = jnp.maximum(worst, run)
        for p in range(n_pairs):
            pv = _dot(jnp.concatenate([a[2 * p], a[2 * p + 1]], axis=0), window(vwin, offs[s], p))
            acc[s, p] = jnp.where(half_lane, pv[:bq], pv[bq:])
        done.append((jnp.max(worst) < EXP_ZERO_BELOW).astype(jnp.int32))

    rj1 = lax.broadcasted_iota(jnp.int32, (KEY_TILE, 2 * KEY_TILE), 0)
    cj1 = lax.broadcasted_iota(jnp.int32, (KEY_TILE, 2 * KEY_TILE), 1)
    suffix1 = ((rj1 > cj1) | (cj1 >= KEY_TILE)).astype(BF16)
    col1 = lax.broadcasted_iota(jnp.int32, (bq, KEY_TILE), 1)
    for s in subs:
        swept_from = win_start + offs[s]

        @pl.when((done[s] == 0) & (swept_from > n_pad))
        def _():
            for p in range(n_pairs):
                qf = q_ref[0, s * bq:(s + 1) * bq, p * LANES:(p + 1) * LANES].astype(F32)
                qsel[2 * p] = jnp.where(half_lane, qf, 0.0).astype(BF16)
                qsel[2 * p + 1] = jnp.where(half_lane, 0.0, qf).astype(BF16)

            def body(state):
                upper, _ = state
                start = jnp.maximum(upper - KEY_TILE, 0)
                copies = []
                for p in range(n_pairs):
                    for src, dst in ((k_hbm, kbuf), (v_hbm, vbuf)):
                        cp = pltpu.make_async_copy(
                            src.at[b, pl.ds(pl.multiple_of(start, 16), KEY_TILE), pl.ds(p * LANES, LANES)],
                            dst.at[p], sem)
                        cp.start()
                        copies.append(cp)
                for cp in copies:
                    cp.wait()
                kpos = start + col1
                m1 = (kpos < upper) & (kpos >= n_pad)
                zz = [_dot(qsel[h], kbuf[h // 2], _NT) for h in heads]
                spp = [_softplus(zz[h]) for h in heads]
                pcs = []
                for h in heads:
                    pcs.extend(_split(jnp.where(m1, -spp[h], 0.0)))
                cs1 = _dot(jnp.concatenate(pcs, axis=0), suffix1)
                worst = jnp.full((bq, KEY_TILE), -jnp.inf, F32)
                aa = []
                for h in heads:
                    c = cs1[2 * h * bq:(2 * h + 1) * bq] + cs1[(2 * h + 1) * bq:(2 * h + 2) * bq]
                    c_old = carry[s, h]
                    aa.append(jnp.where(m1, jnp.exp((zz[h] - spp[h]) + c[:, :KEY_TILE] + c_old), 0.0).astype(BF16))
                    c_new = c_old + c[:, KEY_TILE:]
                    carry[s, h] = c_new
                    worst = jnp.maximum(worst, c_new)
                for p in range(n_pairs):
                    pv0 = _dot(aa[2 * p], vbuf[p])
                    pv1 = _dot(aa[2 * p + 1], vbuf[p])
                    acc[s, p] = acc[s, p] + jnp.where(half_lane, pv0, pv1)
                return start, (jnp.max(worst) < EXP_ZERO_BELOW).astype(jnp.int32)

            lax.while_loop(lambda st: (st[0] > n_pad) & (st[1] == 0), body, (swept_from, jnp.int32(0)))

    onb = on_ref[...]
    for s in subs:
        for p in range(n_pairs):
            o = acc[s, p]
            sq = o * o
            s_lo = jnp.sum(jnp.where(half_lane, sq, 0.0), axis=-1, keepdims=True)
            s_hi = jnp.sum(jnp.where(half_lane, 0.0, sq), axis=-1, keepdims=True)
            ms = jnp.where(half_lane, s_lo, s_hi) * (1.0 / SB_DIM)
            o_ref[0, s * bq:(s + 1) * bq, p * LANES:(p + 1) * LANES] = (o * lax.rsqrt(ms + EPS) * onb).astype(BF16)


def _sb_attn(q16, k16p, v16p, onorm_b2, bq, n_sub, n_pad):
    b, tq, _ = q16.shape
    tkp = k16p.shape[1]
    q_off = tkp - tq
    g = bq * n_sub
    assert tq % g == 0 and g <= KEY_TILE and bq % 16 == 0
    assert all((q_off + (i + 1) * g) % KEY_TILE == 0 for i in range(tq // g))

    def kmap(back):
        def f(bi, i):
            last = (q_off + (i + 1) * g) // KEY_TILE - 1
            return (bi, jnp.maximum(last - back, 0), 0)
        return f

    qmap = lambda bi, i: (bi, i, 0)
    kern = functools.partial(_sb_kernel, bq=bq, n_sub=n_sub, n_pad=n_pad, q_off=q_off)
    kspec = [pl.BlockSpec((1, KEY_TILE, SB_WIDTH), kmap(back)) for back in range(WINDOW_BLOCKS)]
    return pl.pallas_call(
        kern,
        grid=(b, tq // g),
        in_specs=[pl.BlockSpec((1, g, SB_WIDTH), qmap)] + kspec + kspec
                 + [pl.BlockSpec(memory_space=pl.ANY), pl.BlockSpec(memory_space=pl.ANY),
                    pl.BlockSpec((1, LANES), lambda bi, i: (0, 0))],
        out_specs=pl.BlockSpec((1, g, SB_WIDTH), qmap),
        out_shape=jax.ShapeDtypeStruct((b, tq, SB_WIDTH), BF16),
        scratch_shapes=[pltpu.VMEM((SB_HEADS // 2, KEY_TILE, LANES), BF16),
                        pltpu.VMEM((SB_HEADS // 2, KEY_TILE, LANES), BF16),
                        pltpu.VMEM((SB_HEADS, bq, LANES), BF16),
                        pltpu.VMEM((n_sub, SB_HEADS // 2, bq, LANES), F32),
                        pltpu.VMEM((n_sub, SB_HEADS, bq, KEY_TILE), F32),
                        pltpu.SemaphoreType.DMA(())],
        compiler_params=_cparams(("arbitrary", "arbitrary")),
        name="sb_attn",
    )(q16, *([k16p] * WINDOW_BLOCKS), *([v16p] * WINDOW_BLOCKS), k16p, v16p, onorm_b2)


def _post_kernel(oa_ref, ob_ref, x_ref, mod_ref, gpm_ref, gpf_ref, wo_ref, wrh_ref, wrl_ref, br_ref, cnt0_ref,
                 x1_ref, h2_ref, route_ref, cnt_ref):
    @pl.when(pl.program_id(0) == 0)
    def _():
        cnt_ref[...] = cnt0_ref[...]

    mod = mod_ref[...]
    mix = _dot(oa_ref[...], wo_ref[0:DN_WIDTH, :]) + _dot(ob_ref[...], wo_ref[DN_WIDTH:DN_WIDTH + SB_WIDTH, :])
    x1 = x_ref[...] + _per_seq(_rms(mix, gpm_ref[...]), mod, lambda y, m: y * m[:, 2:3])
    x1_ref[...] = x1
    h2 = _per_seq(_rms(x1, gpf_ref[...]), mod, lambda y, m: y * (1.0 + m[:, 4:5]) + m[:, 3:4])
    h2_ref[...] = h2
    hh, hl = _split(h2)
    wrh = wrh_ref[...]
    logits = _dot(hh, wrh) + (_dot(hl, wrh) + _dot(hh, wrl_ref[...])) + br_ref[...]
    lane = lax.broadcasted_iota(jnp.int32, logits.shape, 1).astype(F32)
    neg = -jnp.inf
    nl = float(LANES)
    lg = jnp.where(lane < N_GROUPS, logits, neg)
    gmax = jnp.max(lg, axis=-1, keepdims=True)
    grp = jnp.min(jnp.where(lg == gmax, lane, nl), axis=-1, keepdims=True)
    p_grp = 1.0 / jnp.sum(jnp.exp(lg - gmax), axis=-1, keepdims=True)
    first = N_GROUPS + grp * EXPERTS_PER_GROUP
    le = jnp.where((lane >= first) & (lane < first + EXPERTS_PER_GROUP), logits, neg)
    emax = jnp.max(le, axis=-1, keepdims=True)
    i1 = jnp.min(jnp.where(le == emax, lane, nl), axis=-1, keepdims=True)
    esum = jnp.sum(jnp.exp(le - emax), axis=-1, keepdims=True)
    le2 = jnp.where(lane == i1, neg, le)
    e2max = jnp.max(le2, axis=-1, keepdims=True)
    i2 = jnp.min(jnp.where(le2 == e2max, lane, nl), axis=-1, keepdims=True)
    p1 = 1.0 / esum
    p2 = jnp.exp(e2max - emax) / esum
    w1 = p_grp * p1 / (p1 + p2)
    w2 = p_grp * p2 / (p1 + p2)
    e1 = i1 - N_GROUPS
    e2 = i2 - N_GROUPS
    hot1 = (lane == e1).astype(F32)
    hot2 = (lane == e2).astype(F32)
    both = hot1 + hot2
    tm = logits.shape[0]
    ti = lax.broadcasted_iota(jnp.int32, (tm, tm), 0)
    tj = lax.broadcasted_iota(jnp.int32, (tm, tm), 1)
    earlier = _dot((ti > tj).astype(BF16), both.astype(BF16)) + cnt_ref[...]
    rank1 = jnp.sum(hot1 * earlier, axis=-1, keepdims=True)
    rank2 = jnp.sum(hot2 * (earlier + hot1), axis=-1, keepdims=True)
    cnt_ref[...] = cnt_ref[...] + jnp.sum(both, axis=0, keepdims=True)
    out = jnp.where(lane == 0.0, e1, 0.0)
    out = jnp.where(lane == 1.0, e2, out)
    out = jnp.where(lane == 2.0, w1, out)
    out = jnp.where(lane == 3.0, w2, out)
    out = jnp.where(lane == 4.0, rank1, out)
    out = jnp.where(lane == 5.0, rank2, out)
    route_ref[...] = out


def _post(oa16, ob16, x2d, mod8, g_post_mix, g_pre_ffn, w_out16, wr_hi, wr_lo, b_r, cnt0, tm, seq_rows,
          mod_row0):
    n = x2d.shape[0]
    row = lambda i: (i, 0)
    const = lambda i: (0, 0)
    return pl.pallas_call(
        _post_kernel,
        grid=(n // tm,),
        in_specs=[pl.BlockSpec((tm, DN_WIDTH), row),
                  pl.BlockSpec((tm, SB_WIDTH), row),
                  pl.BlockSpec((tm, D_MODEL), row),
                  _mod_spec(tm, seq_rows, mod_row0),
                  pl.BlockSpec((1, D_MODEL), const),
                  pl.BlockSpec((1, D_MODEL), const),
                  pl.BlockSpec((D_MODEL, D_MODEL), const),
                  pl.BlockSpec((D_MODEL, LANES), const),
                  pl.BlockSpec((D_MODEL, LANES), const),
                  pl.BlockSpec((1, LANES), const),
                  pl.BlockSpec((1, LANES), const)],
        out_specs=[pl.BlockSpec((tm, D_MODEL), row),
                   pl.BlockSpec((tm, D_MODEL), row),
                   pl.BlockSpec((tm, LANES), row),
                   pl.BlockSpec((1, LANES), const)],
        out_shape=[jax.ShapeDtypeStruct((n, D_MODEL), F32),
                   jax.ShapeDtypeStruct((n, D_MODEL), F32),
                   jax.ShapeDtypeStruct((n, LANES), F32),
                   jax.ShapeDtypeStruct((1, LANES), F32)],
        compiler_params=_cparams(("arbitrary",)),
        name="post",
    )(oa16, ob16, x2d, mod8, g_post_mix, g_pre_ffn, w_out16, wr_hi, wr_lo, b_r, cnt0)


def _dispatch_kernel(seg_ref, dp_ref, ds_ref, hp_ref, hs_ref, xs_hbm, zbuf, stage, sem, stage_sems, *,
                     n_blocks, n_prompt_steps):
    i = pl.program_id(0)
    last_step = pl.num_programs(0) - 1

    @pl.when(i == 0)
    def _():
        zbuf[...] = jnp.zeros_like(zbuf)

        def zero_block(row0):
            return pltpu.make_async_copy(zbuf, xs_hbm.at[pl.ds(pl.multiple_of(row0, MOE_BLOCK), MOE_BLOCK), :], sem)

        for e in range(N_EXPERTS):
            @pl.when(seg_ref[e] > 0)
            def _():
                zero_block(seg_ref[N_EXPERTS + e] - MOE_BLOCK).start()
        for e in range(N_EXPERTS):
            @pl.when(seg_ref[e] > 0)
            def _():
                zero_block(seg_ref[N_EXPERTS + e] - MOE_BLOCK).wait()

        used = seg_ref[2 * N_EXPERTS - 1] // MOE_BLOCK

        def fill(b, c):
            cp = zero_block(b * MOE_BLOCK)
            cp.start()
            cp.wait()
            return c

        lax.fori_loop(used, n_blocks, fill, 0)

    def scatter(h_ref, dest_ref, sem_):
        for t in range(h_ref.shape[0]):
            for slot in range(2):
                pltpu.make_async_copy(h_ref.at[pl.ds(t, 1), :],
                                      xs_hbm.at[pl.ds(dest_ref[0, 0, 2 * t + slot], 1), :],
                                      sem_).start(priority=slot)

    def drain(h_ref, sem_):
        for slot in range(2):
            pltpu.make_async_copy(h_ref, xs_hbm.at[pl.ds(0, h_ref.shape[0]), :], sem_).wait()

    for buf in range(2):
        @pl.when((i < last_step) & (i % 2 == buf))
        def _():
            @pl.when(i >= 2)
            def _():
                drain(stage.at[buf], stage_sems.at[buf])
            stage[buf] = hp_ref[...]
            scatter(stage.at[buf], dp_ref, stage_sems.at[buf])

    @pl.when(i == last_step)
    def _():
        scatter(hs_ref, ds_ref, sem)
        drain(hs_ref, sem)
        for buf in range(min(2, n_prompt_steps)):
            drain(stage.at[buf], stage_sems.at[buf])


def _dispatch(seg, dest_p, dest_s, h2p, h2s, n_blocks, tm):
    n_p, n_s = h2p.shape[0], h2s.shape[0]
    steps_p = n_p // tm
    pmap3 = lambda i, sg: (jnp.minimum(i, steps_p - 1), 0, 0)
    pmap2 = lambda i, sg: (jnp.minimum(i, steps_p - 1), 0)
    grid_spec = pltpu.PrefetchScalarGridSpec(
        num_scalar_prefetch=1,
        grid=(steps_p + 1,),
        in_specs=[pl.BlockSpec((1, 1, 2 * tm), pmap3, memory_space=pltpu.SMEM),
                  pl.BlockSpec((1, 1, 2 * n_s), lambda i, sg: (0, 0, 0), memory_space=pltpu.SMEM),
                  pl.BlockSpec((tm, D_MODEL), pmap2),
                  pl.BlockSpec((n_s, D_MODEL), lambda i, sg: (0, 0))],
        out_specs=pl.BlockSpec(memory_space=pl.ANY),
        scratch_shapes=[pltpu.VMEM((MOE_BLOCK, D_MODEL), F32), pltpu.VMEM((2, tm, D_MODEL), F32),
                        pltpu.SemaphoreType.DMA(()), pltpu.SemaphoreType.DMA((2,))])
    return pl.pallas_call(
        functools.partial(_dispatch_kernel, n_blocks=n_blocks, n_prompt_steps=steps_p),
        grid_spec=grid_spec,
        out_shape=jax.ShapeDtypeStruct((n_blocks * MOE_BLOCK, D_MODEL), F32),
        compiler_params=_cparams(("arbitrary",)),
        name="dispatch",
    )(seg, dest_p.reshape(steps_p, 1, 2 * tm), dest_s.reshape(1, 1, 2 * n_s), h2p, h2s)


def _moe_kernel(blk_e_ref, nvalid_ref, next_e_ref, wslot_ref, x_ref, wg_hbm, wu_hbm, wd_hbm, y_ref,
                wg16, wu16, wd16, wg32, wu32, wd32, wsem):
    i = pl.program_id(0)
    e = blk_e_ref[i]
    used = nvalid_ref[i] > 0
    first_of_expert = used & ((i == 0) | (e != blk_e_ref[jnp.maximum(i - 1, 0)]))

    def fetch(expert, slot):
        return [pltpu.make_async_copy(src.at[expert], dst.at[slot], wsem.at[slot])
                for src, dst in ((wg_hbm, wg32), (wu_hbm, wu32), (wd_hbm, wd32))]

    for slot in range(2):
        @pl.when(first_of_expert & (wslot_ref[i] == slot))
        def _():
            @pl.when(i == 0)
            def _():
                for cp in fetch(e, slot):
                    cp.start()
            for cp in fetch(e, slot):
                cp.wait()
            wg16[...] = wg32[slot].astype(BF16)
            wu16[...] = wu32[slot].astype(BF16)
            wd16[...] = wd32[slot].astype(BF16)

            @pl.when(next_e_ref[i] < N_EXPERTS)
            def _():
                for cp in fetch(next_e_ref[i], 1 - slot):
                    cp.start()

    @pl.when(used)
    def _():
        xb = x_ref[...].astype(BF16)
        g = _dot(xb, wg16[...])
        u = _dot(xb, wu16[...])
        hmid = (_silu(g) * u).astype(BF16)
        y_ref[...] = _dot(hmid, wd16[...])

    @pl.when(nvalid_ref[i] == 0)
    def _():
        y_ref[...] = jnp.zeros_like(y_ref)


def _moe(blk_e, nvalid, next_e, wslot, x_sorted, w_gate, w_up, w_down):
    n_blocks = blk_e.shape[0]
    xmap = lambda i, be, nv, ne, ws: (jnp.where(nv[i] > 0, i, 0), 0)
    any_space = pl.BlockSpec(memory_space=pl.ANY)
    grid_spec = pltpu.PrefetchScalarGridSpec(
        num_scalar_prefetch=4,
        grid=(n_blocks,),
        in_specs=[pl.BlockSpec((MOE_BLOCK, D_MODEL), xmap), any_space, any_space, any_space],
        out_specs=pl.BlockSpec((MOE_BLOCK, D_MODEL), lambda i, be, nv, ne, ws: (i, 0)),
        scratch_shapes=[pltpu.VMEM((D_MODEL, D_EXPERT), BF16),
                        pltpu.VMEM((D_MODEL, D_EXPERT), BF16),
                        pltpu.VMEM((D_EXPERT, D_MODEL), BF16),
                        pltpu.VMEM((2, D_MODEL, D_EXPERT), F32),
                        pltpu.VMEM((2, D_MODEL, D_EXPERT), F32),
                        pltpu.VMEM((2, D_EXPERT, D_MODEL), F32),
                        pltpu.SemaphoreType.DMA((2,))])
    return pl.pallas_call(
        _moe_kernel,
        grid_spec=grid_spec,
        out_shape=jax.ShapeDtypeStruct((n_blocks * MOE_BLOCK, D_MODEL), F32),
        compiler_params=_cparams(("arbitrary",)),
        name="moe",
    )(blk_e, nvalid, next_e, wslot, x_sorted, w_gate, w_up, w_down)


def _combine_kernel(dcur_ref, dnext_ref, route_ref, x1_ref, mod_ref, g_ref, y_hbm, o_ref, ybuf, sems):
    i = pl.program_id(0)
    last = pl.num_programs(0) - 1
    tm = x1_ref.shape[0]

    def gather(dest_ref, buf):
        for t in range(tm):
            for slot in range(2):
                pltpu.make_async_copy(y_hbm.at[pl.ds(dest_ref[0, 0, 2 * t + slot], 1), :],
                                      ybuf.at[buf, slot, pl.ds(t, 1), :], sems.at[buf]).start(priority=slot)

    @pl.when(i == 0)
    def _():
        gather(dcur_ref, 0)

    for buf in range(2):
        @pl.when(i % 2 == buf)
        def _():
            @pl.when(i < last)
            def _():
                gather(dnext_ref, 1 - buf)

            for slot in range(2):
                pltpu.make_async_copy(y_hbm.at[pl.ds(0, tm), :], ybuf.at[buf, slot], sems.at[buf]).wait()
            route = route_ref[...]
            moe = ybuf[buf, 0] * route[:, 2:3] + ybuf[buf, 1] * route[:, 3:4]
            o_ref[...] = x1_ref[...] + _per_seq(_rms(moe, g_ref[...]), mod_ref[...], lambda y, m: y * m[:, 5:6])


def _combine(dest, route, y_sorted, x1, mod8, g_post_ffn, tm, seq_rows, mod_row0):
    n = x1.shape[0]
    steps = n // tm
    row = lambda i: (i, 0)
    dest3 = dest.reshape(steps, 1, 2 * tm)
    return pl.pallas_call(
        _combine_kernel,
        grid=(steps,),
        in_specs=[pl.BlockSpec((1, 1, 2 * tm), lambda i: (i, 0, 0), memory_space=pltpu.SMEM),
                  pl.BlockSpec((1, 1, 2 * tm), lambda i: (jnp.minimum(i + 1, steps - 1), 0, 0),
                               memory_space=pltpu.SMEM),
                  pl.BlockSpec((tm, LANES), row),
                  pl.BlockSpec((tm, D_MODEL), row),
                  _mod_spec(tm, seq_rows, mod_row0),
                  pl.BlockSpec((1, D_MODEL), lambda i: (0, 0)),
                  pl.BlockSpec(memory_space=pl.ANY)],
        out_specs=pl.BlockSpec((tm, D_MODEL), row),
        out_shape=jax.ShapeDtypeStruct((n, D_MODEL), F32),
        scratch_shapes=[pltpu.VMEM((2, 2, tm, D_MODEL), F32), pltpu.SemaphoreType.DMA((2,))],
        compiler_params=_cparams(("arbitrary",)),
        name="combine",
    )(dest3, dest3, route, x1, mod8, g_post_ffn, y_sorted)


def _segment_plan(counts_f, n_blocks):
    counts = counts_f[0, :N_EXPERTS].astype(jnp.int32)
    padded = (counts + MOE_BLOCK - 1) // MOE_BLOCK * MOE_BLOCK
    pad_end = jnp.cumsum(padded)
    pad_start = pad_end - padded
    blk_start = jnp.arange(n_blocks, dtype=jnp.int32) * MOE_BLOCK
    blk_e = jnp.minimum(jnp.sum((pad_end[None, :] <= blk_start[:, None]).astype(jnp.int32), axis=1),
                        N_EXPERTS - 1)
    onehot = blk_e[:, None] == jnp.arange(N_EXPERTS, dtype=jnp.int32)[None, :]
    c_blk = jnp.sum(jnp.where(onehot, counts[None, :], 0), axis=1)
    s_blk = jnp.sum(jnp.where(onehot, pad_start[None, :], 0), axis=1)
    nvalid = jnp.clip(c_blk - (blk_start - s_blk), 0, MOE_BLOCK).astype(jnp.int32)
    seg = jnp.concatenate([counts, pad_end]).astype(jnp.int32)
    ids = jnp.arange(N_EXPERTS, dtype=jnp.int32)
    later = (ids[None, :] > ids[:, None]) & (counts[None, :] > 0)
    next_e = jnp.min(jnp.where(later, ids[None, :], N_EXPERTS), axis=1)
    order = jnp.cumsum((counts > 0).astype(jnp.int32)) - 1
    pick = lambda table: jnp.sum(jnp.where(onehot, table[None, :], 0), axis=1).astype(jnp.int32)
    return blk_e.astype(jnp.int32), nvalid, pick(next_e), pick(order % 2), seg, pad_start


def _token_rows(route, pad_start):
    eid = route[:, 0:2].astype(jnp.int32)
    rank = route[:, 4:6].astype(jnp.int32)
    onehot = eid[:, :, None] == jnp.arange(N_EXPERTS, dtype=jnp.int32)[None, None, :]
    return rank + jnp.sum(jnp.where(onehot, pad_start[None, None, :], 0), axis=2)


def _layer(x_p, x_s, c_p, c_s, k_past, v_past, s0_s, conv_s, p):
    bp, tp, d = x_p.shape
    bs, ts, _ = x_s.shape
    n_p, n_s = bp * tp, bs * ts
    n_tok = n_p + n_s

    n_seq = bp + bs
    c_all = jnp.zeros((16, d), F32).at[:n_seq].set(jnp.concatenate([c_s, c_p], axis=0))
    mod = _ada(c_all, p['w_ada'], p['b_ada'])
    mod8 = jnp.pad(mod.reshape(16, 6, d), ((0, 0), (0, 2), (0, 0)))

    w_in = p['w_in']
    o_b = CONV_CH + DN_WIDTH
    o_q = o_b + 2 * DN_HEADS
    wb = jnp.pad(w_in[:, o_b:o_q], ((0, 0), (0, LANES - 2 * DN_HEADS)))
    wb_hi = wb.astype(BF16)
    wb_lo = (wb - wb_hi.astype(F32)).astype(BF16)
    w_main = jnp.concatenate([w_in[:, :o_b].astype(BF16), w_in[:, o_q:].astype(BF16), wb_hi, wb_lo], axis=1)
    g_pre_mix = p['g_pre_mix'].reshape(1, d)

    conv_w8 = jnp.pad(p['conv_w'], ((0, 8 - CONV_W), (0, 0)))
    pad_g = lambda a: jnp.pad(a.reshape(1, DN_HEADS), ((0, 0), (DN_HEADS, LANES - 2 * DN_HEADS)))
    alog_row, dtb_row = pad_g(p['a_log']), pad_g(p['dt_bias'])
    onorm_a = p['onorm_a'].reshape(1, DN_DIM)
    onorm_b2 = jnp.tile(p['onorm_b'].reshape(1, SB_DIM), (1, 2))

    w_out16 = p['w_out'].astype(BF16)
    wr = jnp.pad(jnp.concatenate([p['w_router_group'], p['w_router_expert']], axis=1),
                 ((0, 0), (0, LANES - N_GROUPS - N_EXPERTS)))
    wr_hi = wr.astype(BF16)
    wr_lo = (wr - wr_hi.astype(F32)).astype(BF16)
    b_r = jnp.pad(jnp.concatenate([p['b_router_group'], p['b_router_expert']]).reshape(1, -1),
                  ((0, 0), (0, LANES - N_GROUPS - N_EXPERTS)))
    g_post_mix = p['g_post_mix'].reshape(1, d)
    g_pre_ffn = p['g_pre_ffn'].reshape(1, d)
    g_post_ffn = p['g_post_ffn'].reshape(1, d)

    def mixer(x, tm, tm_post, seq_rows, mod_row0, hist8, s0, k_old, v_old, chunk, n_chunks, bq, n_sub, cnt0):
        b, t, _ = x.shape
        x2d = x.reshape(b * t, d)
        a_in, z, bg, q16, kb, vb, k16, v16 = _proj(x2d, mod8, g_pre_mix, w_main, wb_hi, tm, seq_rows, mod_row0)
        r3 = lambda a: a.reshape(b, t, a.shape[-1])
        oa16, s_new = _delta(r3(a_in), r3(z), r3(bg), hist8, s0, conv_w8, alog_row, dtb_row, onorm_a,
                             chunk, n_chunks)
        k16, v16 = r3(k16), r3(v16)
        if k_old is not None:
            k16 = jnp.concatenate([k_old.reshape(b, -1, SB_WIDTH).astype(BF16), k16], axis=1)
            v16 = jnp.concatenate([v_old.reshape(b, -1, SB_WIDTH).astype(BF16), v16], axis=1)
        n_pad = (-k16.shape[1]) % KEY_TILE
        k16 = jnp.pad(k16, ((0, 0), (n_pad, 0), (0, 0)))
        v16 = jnp.pad(v16, ((0, 0), (n_pad, 0), (0, 0)))
        ob16 = _sb_attn(r3(q16), k16, v16, onorm_b2, bq, n_sub, n_pad)
        x1, h2, route, cnt = _post(oa16.reshape(b * t, DN_WIDTH), ob16.reshape(b * t, SB_WIDTH), x2d, mod8,
                                   g_post_mix, g_pre_ffn, w_out16, wr_hi, wr_lo, b_r, cnt0, tm_post, seq_rows,
                                   mod_row0)
        new_conv = r3(a_in)[:, t - (CONV_W - 1):, :]
        return (x1, h2, route, cnt, kb.reshape(b, t, SB_HEADS, SB_DIM), vb.reshape(b, t, SB_HEADS, SB_DIM),
                s_new, new_conv)

    zero_hist = jnp.zeros((bp, 8, CONV_CH), F32)
    zero_s = jnp.zeros((bp, DN_HEADS, DN_DIM, DN_DIM), F32)
    hist_s = jnp.pad(conv_s, ((0, 0), (8 - (CONV_W - 1), 0), (0, 0)))
    tm_p = min(ROW_TILE, tp)
    tm_dense = DENSE_TILE if n_p % DENSE_TILE == 0 else tm_p
    tm_router = ROUTER_TILE if n_p % ROUTER_TILE == 0 else tm_dense
    nc_p = max(1, min(8, tp // DELTA_BLOCK))
    x1p, h2p, rp, cnt_p, kp, vp, sp, cp = mixer(x_p, tm_dense, tm_router, tp, bs, zero_hist, zero_s, None, None,
                                                 min(DELTA_BLOCK, tp), nc_p, min(KEY_TILE // 2, tp), 2,
                                                 jnp.zeros((1, LANES), F32))
    x1s, h2s, rs, cnt, ks, vs, ss, cs = mixer(x_s, n_s, n_s, ts, 0, hist_s, s0_s, k_past, v_past,
                                               min(DELTA_BLOCK, ts), max(1, ts // DELTA_BLOCK),
                                               min(KEY_TILE, ts), 1, cnt_p)

    n_blocks = -(-2 * n_tok // MOE_BLOCK) + N_EXPERTS
    blk_e, nvalid, next_e, wslot, seg, pad_start = _segment_plan(cnt, n_blocks)
    dest_p = _token_rows(rp, pad_start)
    dest_s = _token_rows(rs, pad_start)
    x_sorted = _dispatch(seg, dest_p, dest_s, h2p, h2s, n_blocks, tm_p)
    y_sorted = _moe(blk_e, nvalid, next_e, wslot, x_sorted, p['w_gate'], p['w_up'], p['w_down'])
    y_p = _combine(dest_p, rp, y_sorted, x1p, mod8, g_post_ffn, tm_p, tp, bs).reshape(bp, tp, d)
    y_s = _combine(dest_s, rs, y_sorted, x1s, mod8, g_post_ffn, n_s, ts, 0).reshape(bs, ts, d)
    return y_p, y_s, kp, vp, sp, cp, ks, vs, ss, cs


def kernel(x_prompt, x_sample, c_prompt, c_sample, cache_k, cache_v, state_delta, state_conv, w_ada, b_ada, g_pre_mix, g_post_mix, g_pre_ffn, g_post_ffn, w_in, conv_w, a_log, dt_bias, onorm_a, onorm_b, w_out, w_router_group, b_router_group, w_router_expert, b_router_expert, w_gate, w_up, w_down):
    depth = w_in.shape[0]
    y_p, y_s = x_prompt, x_sample
    outs = [[] for _ in range(8)]
    for l in range(depth):
        p = dict(w_ada=w_ada[l], b_ada=b_ada[l], g_pre_mix=g_pre_mix[l], g_post_mix=g_post_mix[l],
                 g_pre_ffn=g_pre_ffn[l], g_post_ffn=g_post_ffn[l], w_in=w_in[l], conv_w=conv_w[l],
                 a_log=a_log[l], dt_bias=dt_bias[l], onorm_a=onorm_a[l], onorm_b=onorm_b[l],
                 w_out=w_out[l], w_router_group=w_router_group[l], b_router_group=b_router_group[l],
                 w_router_expert=w_router_expert[l], b_router_expert=b_router_expert[l],
                 w_gate=w_gate[l], w_up=w_up[l], w_down=w_down[l])
        res = _layer(y_p, y_s, c_prompt, c_sample, cache_k[l], cache_v[l], state_delta[l], state_conv[l], p)
        y_p, y_s = res[0], res[1]
        for lst, r in zip(outs, res[2:]):
            lst.append(r)
    return (y_p, y_s) + tuple(jnp.stack(o) for o in outs)
```

```python
import functools

import jax
import jax.numpy as jnp
from jax import lax
from jax.experimental import pallas as pl
from jax.experimental.pallas import tpu as pltpu

F32 = jnp.float32
BF16 = jnp.bfloat16

D_MODEL = 1024
DN_HEADS = 4
DN_DIM = 128
CONV_W = 4
DN_WIDTH = DN_HEADS * DN_DIM
CONV_CH = 3 * DN_WIDTH
DELTA_BLOCK = 64
SB_HEADS = 8
SB_DIM = 64
SB_WIDTH = SB_HEADS * SB_DIM
N_GROUPS = 4
EXPERTS_PER_GROUP = 8
N_EXPERTS = N_GROUPS * EXPERTS_PER_GROUP
D_EXPERT = D_MODEL // 2
MOE_BLOCK = 256
EPS = 1e-6

LANES = 128
KEY_TILE = 128
ATTN_WINDOW = 3 * KEY_TILE
WINDOW_BLOCKS = 4
EXP_ZERO_BELOW = -104.0
VMEM_LIMIT = 56 * 1024 * 1024
DENSE_TILE = 512
ROUTER_TILE = 1024
ROW_TILE = 256
RANK_SUBTILE = 256


def _cparams(sem):
    return pltpu.CompilerParams(dimension_semantics=sem, vmem_limit_bytes=VMEM_LIMIT)


def _split(a):
    hi = a.astype(BF16)
    lo = (a - hi.astype(F32)).astype(BF16)
    return hi, lo


def _dot(a, b, dims=(((1,), (0,)), ((), ()))):
    return lax.dot_general(a, b, dims, preferred_element_type=F32)


def _dot_small_int_lhs(a, b):
    a16 = a.astype(BF16)
    return _dot(jnp.concatenate([a16, a16], axis=1), jnp.concatenate(_split(b), axis=0))


_NT = (((1,), (1,)), ((), ()))
_TN = (((0,), (0,)), ((), ()))


def _silu(x):
    return x * jax.nn.sigmoid(x)


SOFTPLUS_LINEAR_ABOVE = 80.0


def _softplus(x):
    return jnp.where(x > SOFTPLUS_LINEAR_ABOVE, x, jnp.log(1.0 + jnp.exp(jnp.minimum(x, SOFTPLUS_LINEAR_ABOVE))))


def _ada_kernel(c_ref, w_ref, b_ref, o_ref):
    s = _silu(c_ref[...]).astype(BF16)
    o_ref[...] = _dot(s, w_ref[...].astype(BF16)) + b_ref[...]


def _ada(c_all, w_ada, b_ada):
    rows = c_all.shape[0]
    n = w_ada.shape[1]
    tn = 1024
    return pl.pallas_call(
        _ada_kernel,
        grid=(n // tn,),
        in_specs=[pl.BlockSpec((rows, D_MODEL), lambda j: (0, 0)),
                  pl.BlockSpec((D_MODEL, tn), lambda j: (0, j)),
                  pl.BlockSpec((1, tn), lambda j: (0, j))],
        out_specs=pl.BlockSpec((rows, tn), lambda j: (0, j)),
        out_shape=jax.ShapeDtypeStruct((rows, n), F32),
        compiler_params=_cparams(("arbitrary",)),
        name="ada",
    )(c_all, w_ada, b_ada.reshape(1, n))


def _rms(x, gain):
    return x * lax.rsqrt(jnp.mean(x * x, axis=-1, keepdims=True) + EPS) * gain


def _per_seq(y, mod, fn):
    n_seq = mod.shape[0]
    ys = y.reshape(n_seq, y.shape[0] // n_seq, y.shape[1])
    return fn(ys, mod).reshape(y.shape)


def _mod_spec(tm, seq_rows, mod_row0):
    if tm >= seq_rows:
        n_seq = tm // seq_rows
        assert tm % seq_rows == 0 and mod_row0 % n_seq == 0
        return pl.BlockSpec((n_seq, 8, D_MODEL), lambda i: (mod_row0 // n_seq + i, 0, 0))
    assert seq_rows % tm == 0
    return pl.BlockSpec((1, 8, D_MODEL), lambda i: (mod_row0 + (i * tm) // seq_rows, 0, 0))


def _proj_kernel(x_ref, mod_ref, g_ref, wm_ref, wbh_ref,
                 a_ref, z_ref, bg_ref, q_ref, k_ref, v_ref, k16_ref, v16_ref):
    h = _per_seq(_rms(x_ref[...], g_ref[...]), mod_ref[...], lambda y, m: y * (1.0 + m[:, 1:2]) + m[:, 0:1])
    hh, hl = _split(h)
    p = _dot(hh, wm_ref[...])
    a_ref[...] = p[:, 0:CONV_CH]
    z_ref[...] = p[:, CONV_CH:CONV_CH + DN_WIDTH]
    o = CONV_CH + DN_WIDTH
    q_ref[...] = (p[:, o:o + SB_WIDTH] * (SB_DIM ** -0.5)).astype(BF16)
    k = p[:, o + SB_WIDTH:o + 2 * SB_WIDTH]
    v = p[:, o + 2 * SB_WIDTH:o + 3 * SB_WIDTH]
    k_ref[...] = k
    v_ref[...] = v
    k16_ref[...] = k.astype(BF16)
    v16_ref[...] = v.astype(BF16)
    o += 3 * SB_WIDTH
    bg_ref[...] = p[:, o:o + LANES] + (_dot(hl, wbh_ref[...]) + p[:, o + LANES:])


def _proj(x2d, mod8, g_pre, w_main, wb_hi, tm, seq_rows, mod_row0):
    n = x2d.shape[0]
    nm = w_main.shape[1]
    row = lambda i: (i, 0)
    const = lambda i: (0, 0)
    outs = [(CONV_CH, F32), (DN_WIDTH, F32), (LANES, F32), (SB_WIDTH, BF16), (SB_WIDTH, F32), (SB_WIDTH, F32),
            (SB_WIDTH, BF16), (SB_WIDTH, BF16)]
    return pl.pallas_call(
        _proj_kernel,
        grid=(n // tm,),
        in_specs=[pl.BlockSpec((tm, D_MODEL), row),
                  _mod_spec(tm, seq_rows, mod_row0),
                  pl.BlockSpec((1, D_MODEL), const),
                  pl.BlockSpec((D_MODEL, nm), const),
                  pl.BlockSpec((D_MODEL, LANES), const)],
        out_specs=[pl.BlockSpec((tm, w), row) for w, _ in outs],
        out_shape=[jax.ShapeDtypeStruct((n, w), dt) for w, dt in outs],
        compiler_params=_cparams(("arbitrary",)),
        name="proj",
    )(x2d, mod8, g_pre, w_main, wb_hi)


def _delta_kernel(a_ref, z_ref, bg_ref, hist0_ref, s0_ref, cw_ref, alog_ref, dtb_ref, on_ref,
                  o_ref, sfin_ref, hist_sc, s_sc, *, chunk, n_chunks):
    t_idx = pl.program_id(1)
    tt = chunk * n_chunks

    @pl.when(t_idx == 0)
    def _():
        hist_sc[...] = hist0_ref[0]
        s_sc[...] = s0_ref[0]

    x = a_ref[0]
    xx = jnp.concatenate([hist_sc[...], x], axis=0)
    cw = cw_ref[...]
    conv = x * cw[CONV_W - 1:CONV_W]
    for s in range(1, CONV_W):
        conv = conv + pltpu.roll(xx, s, 0)[8:] * cw[CONV_W - 1 - s:CONV_W - s]
    conv = _silu(conv)
    hist_sc[...] = x[tt - 8:tt]

    bg = bg_ref[0]
    lane = lax.broadcasted_iota(jnp.int32, (1, LANES), 1)
    g_lane = (lane >= DN_HEADS) & (lane < 2 * DN_HEADS)
    neg_a = jnp.where(g_lane, -jnp.exp(alog_ref[...]), 0.0)
    beta_all = jax.nn.sigmoid(bg)
    g_all = neg_a * _softplus(bg + dtb_ref[...])

    ri = lax.broadcasted_iota(jnp.int32, (chunk, chunk), 0)
    ci = lax.broadcasted_iota(jnp.int32, (chunk, chunk), 1)
    incl = ri >= ci
    strict = ri > ci
    tri = incl.astype(F32)
    eye = (ri == ci).astype(F32)
    ones_cc = jnp.ones((chunk, chunk), F32)
    onorm = on_ref[...]

    rows = lambda c: slice(c * chunk, (c + 1) * chunk)
    hs = range(DN_HEADS)
    prep = {}

    def prepare(chunk_ids):
        units = [(c, h) for c in chunk_ids for h in hs]
        gc_all = {c: _dot_small_int_lhs(tri, g_all[rows(c)]) for c in chunk_ids}
        yield
        diag = {c: jnp.concatenate([eye * gc_all[c][:, DN_HEADS + h:DN_HEADS + h + 1] for h in hs], axis=1)
                for c in chunk_ids}
        gc_rows = {c: _dot_small_int_lhs(ones_cc, diag[c]) for c in chunk_ids}
        yield
        q, k, v, beta, gcol, glast, eg, decay, kbeta = {}, {}, {}, {}, {}, {}, {}, {}, {}
        for (c, h) in units:
            lo = h * DN_DIM
            qq = conv[rows(c), lo:lo + DN_DIM]
            kk = conv[rows(c), DN_WIDTH + lo:DN_WIDTH + lo + DN_DIM]
            u_ = (c, h)
            q[u_] = qq * lax.rsqrt(jnp.sum(qq * qq, axis=-1, keepdims=True) + EPS) * (DN_DIM ** -0.5)
            k[u_] = kk * lax.rsqrt(jnp.sum(kk * kk, axis=-1, keepdims=True) + EPS)
            v[u_] = conv[rows(c), 2 * DN_WIDTH + lo:2 * DN_WIDTH + lo + DN_DIM]
            beta[u_] = beta_all[rows(c), h:h + 1]
            gcol[u_] = gc_all[c][:, DN_HEADS + h:DN_HEADS + h + 1]
            glast[u_] = gcol[u_][chunk - 1:chunk, :]
            eg[u_] = jnp.exp(gcol[u_])
            grow = gc_rows[c][:, h * chunk:(h + 1) * chunk]
            decay[u_] = jnp.where(incl, jnp.exp(jnp.minimum(gcol[u_] - grow, 0.0)), 0.0)
            kbeta[u_] = k[u_] * beta[u_]

        k16 = {u_: k[u_].astype(BF16) for u_ in units}
        kq = {u_: _dot(jnp.concatenate([kbeta[u_], q[u_]], axis=0).astype(BF16), k16[u_], _NT) for u_ in units}
        yield
        lower = {u_: jnp.where(strict, kq[u_][:chunk] * decay[u_], 0.0) for u_ in units}
        attn16 = {u_: jnp.where(incl, kq[u_][chunk:] * decay[u_], 0.0).astype(BF16) for u_ in units}

        sol = {u_: jnp.concatenate([v[u_] * beta[u_], kbeta[u_] * eg[u_]], axis=1) for u_ in units}
        lp = lower
        p = 1
        while p < chunk:
            lsp = {u_: _split(lp[u_]) for u_ in units}
            ssp = {u_: _split(sol[u_]) for u_ in units}
            lcat = {u_: jnp.concatenate([lsp[u_][0], lsp[u_][1], lsp[u_][0]], axis=1) for u_ in units}
            upd = {u_: _dot(lcat[u_], jnp.concatenate([ssp[u_][0], ssp[u_][0], ssp[u_][1]], axis=0))
                   for u_ in units}
            if 2 * p < chunk:
                lp = {u_: _dot(lcat[u_], jnp.concatenate([lsp[u_][0], lsp[u_][0], lsp[u_][1]], axis=0))
                      for u_ in units}
            yield
            sol = {u_: (sol[u_] - upd[u_]) if p == 1 else (sol[u_] + upd[u_]) for u_ in units}
            p *= 2

        for u_ in units:
            prep[u_] = dict(
                u=sol[u_][:, :DN_DIM],
                wq=jnp.concatenate([sol[u_][:, DN_DIM:], q[u_] * eg[u_]], axis=0).astype(BF16),
                kd=(k[u_] * jnp.exp(glast[u_] - gcol[u_])).astype(BF16),
                attn=attn16[u_], decay_all=jnp.exp(glast[u_]))

    def recur(chunk_ids):
        for c in chunk_ids:
            s_old = [s_sc[h] for h in hs]
            ws = [_dot(prep[(c, h)]['wq'], s_old[h].astype(BF16)) for h in hs]
            yield
            v_new = [(prep[(c, h)]['u'] - ws[h][:chunk]).astype(BF16) for h in hs]
            o_in = [_dot(prep[(c, h)]['attn'], v_new[h]) for h in hs]
            ds = [_dot(prep[(c, h)]['kd'], v_new[h], _TN) for h in hs]
            yield
            for h in hs:
                lo = h * DN_DIM
                s_sc[h] = s_old[h] * prep[(c, h)]['decay_all'] + ds[h]
                o = ws[h][chunk:] + o_in[h]
                zg = z_ref[0, rows(c), lo:lo + DN_DIM]
                o_ref[0, rows(c), lo:lo + DN_DIM] = (_rms(o, onorm) * _silu(zg)).astype(BF16)

    def run(*streams):
        live = list(streams)
        while live:
            for g_ in list(live):
                if next(g_, done_) is done_:
                    live.remove(g_)

    done_ = object()
    ids = list(range(n_chunks))
    first, second = (ids[:n_chunks // 2], ids[n_chunks // 2:]) if n_chunks >= 2 else (ids, [])
    run(prepare(first))
    run(recur(first), prepare(second))
    run(recur(second))

    @pl.when(t_idx == pl.num_programs(1) - 1)
    def _():
        sfin_ref[0] = s_sc[...]


def _delta(a_in, z, bg, hist8, s0, conv_w8, alog_row, dtb_row, onorm_a, chunk, n_chunks):
    b, t, _ = a_in.shape
    tt = chunk * n_chunks
    tile = lambda bi, ti: (bi, ti, 0)
    per_b3 = lambda bi, ti: (bi, 0, 0)
    per_b4 = lambda bi, ti: (bi, 0, 0, 0)
    const = lambda bi, ti: (0, 0)
    kern = functools.partial(_delta_kernel, chunk=chunk, n_chunks=n_chunks)
    return pl.pallas_call(
        kern,
        grid=(b, t // tt),
        in_specs=[pl.BlockSpec((1, tt, CONV_CH), tile),
                  pl.BlockSpec((1, tt, DN_WIDTH), tile),
                  pl.BlockSpec((1, tt, LANES), tile),
                  pl.BlockSpec((1, 8, CONV_CH), per_b3),
                  pl.BlockSpec((1, DN_HEADS, DN_DIM, DN_DIM), per_b4),
                  pl.BlockSpec((8, CONV_CH), const),
                  pl.BlockSpec((1, LANES), const),
                  pl.BlockSpec((1, LANES), const),
                  pl.BlockSpec((1, DN_DIM), const)],
        out_specs=[pl.BlockSpec((1, tt, DN_WIDTH), tile),
                   pl.BlockSpec((1, DN_HEADS, DN_DIM, DN_DIM), per_b4)],
        out_shape=[jax.ShapeDtypeStruct((b, t, DN_WIDTH), BF16),
                   jax.ShapeDtypeStruct((b, DN_HEADS, DN_DIM, DN_DIM), F32)],
        scratch_shapes=[pltpu.VMEM((8, CONV_CH), F32),
                        pltpu.VMEM((DN_HEADS, DN_DIM, DN_DIM), F32)],
        compiler_params=_cparams(("arbitrary", "arbitrary")),
        name="delta",
    )(a_in, z, bg, hist8, s0, conv_w8, alog_row, dtb_row, onorm_a)


def _sb_kernel(q_ref, *refs, bq, n_sub, n_pad, q_off):
    kwin = refs[:WINDOW_BLOCKS]
    vwin = refs[WINDOW_BLOCKS:2 * WINDOW_BLOCKS]
    k_hbm, v_hbm, on_ref, o_ref, kbuf, vbuf, qsel, acc, carry, sem = refs[2 * WINDOW_BLOCKS:]
    b = pl.program_id(0)
    i = pl.program_id(1)
    g = bq * n_sub
    qend_step = q_off + (i + 1) * g
    win_start = qend_step - WINDOW_BLOCKS * KEY_TILE
    n_pairs = SB_HEADS // 2
    heads = range(SB_HEADS)

    half_lane = lax.broadcasted_iota(jnp.int32, (bq, LANES), 1) < SB_DIM
    rj = lax.broadcasted_iota(jnp.int32, (2 * KEY_TILE, 2 * KEY_TILE), 0) % KEY_TILE
    cj = lax.broadcasted_iota(jnp.int32, (2 * KEY_TILE, 2 * KEY_TILE), 1)
    suffix2 = ((rj > cj) | (cj >= KEY_TILE)).astype(BF16)

    def window(blocks, off, p):
        parts = []
        for blk in range(WINDOW_BLOCKS):
            lo, hi = max(off, blk * KEY_TILE), min(off + ATTN_WINDOW, (blk + 1) * KEY_TILE)
            if lo < hi:
                ref = blocks[WINDOW_BLOCKS - 1 - blk]
                parts.append(ref[0, lo - blk * KEY_TILE:hi - blk * KEY_TILE, p * LANES:(p + 1) * LANES])
        return jnp.concatenate(parts, axis=0)

    n_t = ATTN_WINDOW // KEY_TILE
    row = lax.broadcasted_iota(jnp.int32, (bq, ATTN_WINDOW), 0)
    col = lax.broadcasted_iota(jnp.int32, (bq, ATTN_WINDOW), 1)
    causal = col < row + (ATTN_WINDOW - bq)
    subs = range(n_sub)
    offs = [WINDOW_BLOCKS * KEY_TILE - g + (s + 1) * bq - ATTN_WINDOW for s in subs]
    mask = [causal & (col >= n_pad - (win_start + offs[s])) for s in subs]
    zs, sp, cs_all = {}, {}, {}
    for s in subs:
        for p in range(n_pairs):
            qf = q_ref[0, s * bq:(s + 1) * bq, p * LANES:(p + 1) * LANES].astype(F32)
            qq = jnp.concatenate([jnp.where(half_lane, qf, 0.0), jnp.where(half_lane, 0.0, qf)], axis=0)
            zz = _dot(qq.astype(BF16), window(kwin, offs[s], p), _NT)
            zs[(s, 2 * p)], zs[(s, 2 * p + 1)] = zz[:bq], zz[bq:]
    for s in subs:
        pieces = []
        for h in heads:
            sp[(s, h)] = _softplus(zs[(s, h)])
            hi, lo = _split(jnp.where(mask[s], -sp[(s, h)], 0.0))
            for j in range(n_t):
                c0 = ATTN_WINDOW - (j + 1) * KEY_TILE
                pieces.append(jnp.concatenate([hi[:, c0:c0 + KEY_TILE], lo[:, c0:c0 + KEY_TILE]], axis=1))
        cs_all[s] = _dot(jnp.concatenate(pieces, axis=0), suffix2)
    done = []
    for s in subs:
        worst = jnp.full((bq, KEY_TILE), -jnp.inf, F32)
        a = {}
        for h in heads:
            run = None
            cols = []
            for j in range(n_t):
                r0 = (h * n_t + j) * bq
                cs = cs_all[s][r0:r0 + bq]
                cols.append(cs[:, :KEY_TILE] if run is None else cs[:, :KEY_TILE] + run)
                run = cs[:, KEY_TILE:] if run is None else run + cs[:, KEY_TILE:]
            within = jnp.concatenate(cols[::-1], axis=1)
            a[h] = jnp.where(mask[s], jnp.exp((zs[(s, h)] - sp[(s, h)]) + within), 0.0).astype(BF16)
            carry[s, h] = run
            worst = jnp.maximum(worst, run)
        for p in range(n_pairs):
            pv = _dot(jnp.concatenate([a[2 * p], a[2 * p + 1]], axis=0), window(vwin, offs[s], p))
            acc[s, p] = jnp.where(half_lane, pv[:bq], pv[bq:])
        done.append((jnp.max(worst) < EXP_ZERO_BELOW).astype(jnp.int32))

    rj1 = lax.broadcasted_iota(jnp.int32, (KEY_TILE, 2 * KEY_TILE), 0)
    cj1 = lax.broadcasted_iota(jnp.int32, (KEY_TILE, 2 * KEY_TILE), 1)
    suffix1 = ((rj1 > cj1) | (cj1 >= KEY_TILE)).astype(BF16)
    col1 = lax.broadcasted_iota(jnp.int32, (bq, KEY_TILE), 1)
    for s in subs:
        swept_from = win_start + offs[s]

        @pl.when((done[s] == 0) & (swept_from > n_pad))
        def _():
            for p in range(n_pairs):
                qf = q_ref[0, s * bq:(s + 1) * bq, p * LANES:(p + 1) * LANES].astype(F32)
                qsel[2 * p] = jnp.where(half_lane, qf, 0.0).astype(BF16)
                qsel[2 * p + 1] = jnp.where(half_lane, 0.0, qf).astype(BF16)

            def body(state):
                upper, _ = state
                start = jnp.maximum(upper - KEY_TILE, 0)
                copies = []
                for p in range(n_pairs):
                    for src, dst in ((k_hbm, kbuf), (v_hbm, vbuf)):
                        cp = pltpu.make_async_copy(
                            src.at[b, pl.ds(pl.multiple_of(start, 16), KEY_TILE), pl.ds(p * LANES, LANES)],
                            dst.at[p], sem)
                        cp.start()
                        copies.append(cp)
                for cp in copies:
                    cp.wait()
                kpos = start + col1
                m1 = (kpos < upper) & (kpos >= n_pad)
                zz = [_dot(qsel[h], kbuf[h // 2], _NT) for h in heads]
                spp = [_softplus(zz[h]) for h in heads]
                pcs = []
                for h in heads:
                    pcs.extend(_split(jnp.where(m1, -spp[h], 0.0)))
                cs1 = _dot(jnp.concatenate(pcs, axis=0), suffix1)
                worst = jnp.full((bq, KEY_TILE), -jnp.inf, F32)
                aa = []
                for h in heads:
                    c = cs1[2 * h * bq:(2 * h + 1) * bq] + cs1[(2 * h + 1) * bq:(2 * h + 2) * bq]
                    c_old = carry[s, h]
                    aa.append(jnp.where(m1, jnp.exp((zz[h] - spp[h]) + c[:, :KEY_TILE] + c_old), 0.0).astype(BF16))
                    c_new = c_old + c[:, KEY_TILE:]
                    carry[s, h] = c_new
                    worst = jnp.maximum(worst, c_new)
                for p in range(n_pairs):
                    pv0 = _dot(aa[2 * p], vbuf[p])
                    pv1 = _dot(aa[2 * p + 1], vbuf[p])
                    acc[s, p] = acc[s, p] + jnp.where(half_lane, pv0, pv1)
                return start, (jnp.max(worst) < EXP_ZERO_BELOW).astype(jnp.int32)

            lax.while_loop(lambda st: (st[0] > n_pad) & (st[1] == 0), body, (swept_from, jnp.int32(0)))

    onb = on_ref[...]
    for s in subs:
        for p in range(n_pairs):
            o = acc[s, p]
            sq = o * o
            s_lo = jnp.sum(jnp.where(half_lane, sq, 0.0), axis=-1, keepdims=True)
            s_hi = jnp.sum(jnp.where(half_lane, 0.0, sq), axis=-1, keepdims=True)
            ms = jnp.where(half_lane, s_lo, s_hi) * (1.0 / SB_DIM)
            o_ref[0, s * bq:(s + 1) * bq, p * LANES:(p + 1) * LANES] = (o * lax.rsqrt(ms + EPS) * onb).astype(BF16)


def _sb_attn(q16, k16p, v16p, onorm_b2, bq, n_sub, n_pad):
    b, tq, _ = q16.shape
    tkp = k16p.shape[1]
    q_off = tkp - tq
    g = bq * n_sub
    assert tq % g == 0 and g <= KEY_TILE and bq % 16 == 0
    assert all((q_off + (i + 1) * g) % KEY_TILE == 0 for i in range(tq // g))

    def kmap(back):
        def f(bi, i):
            last = (q_off + (i + 1) * g) // KEY_TILE - 1
            return (bi, jnp.maximum(last - back, 0), 0)
        return f

    qmap = lambda bi, i: (bi, i, 0)
    kern = functools.partial(_sb_kernel, bq=bq, n_sub=n_sub, n_pad=n_pad, q_off=q_off)
    kspec = [pl.BlockSpec((1, KEY_TILE, SB_WIDTH), kmap(back)) for back in range(WINDOW_BLOCKS)]
    return pl.pallas_call(
        kern,
        grid=(b, tq // g),
        in_specs=[pl.BlockSpec((1, g, SB_WIDTH), qmap)] + kspec + kspec
                 + [pl.BlockSpec(memory_space=pl.ANY), pl.BlockSpec(memory_space=pl.ANY),
                    pl.BlockSpec((1, LANES), lambda bi, i: (0, 0))],
        out_specs=pl.BlockSpec((1, g, SB_WIDTH), qmap),
        out_shape=jax.ShapeDtypeStruct((b, tq, SB_WIDTH), BF16),
        scratch_shapes=[pltpu.VMEM((SB_HEADS // 2, KEY_TILE, LANES), BF16),
                        pltpu.VMEM((SB_HEADS // 2, KEY_TILE, LANES), BF16),
                        pltpu.VMEM((SB_HEADS, bq, LANES), BF16),
                        pltpu.VMEM((n_sub, SB_HEADS // 2, bq, LANES), F32),
                        pltpu.VMEM((n_sub, SB_HEADS, bq, KEY_TILE), F32),
                        pltpu.SemaphoreType.DMA(())],
        compiler_params=_cparams(("arbitrary", "arbitrary")),
        name="sb_attn",
    )(q16, *([k16p] * WINDOW_BLOCKS), *([v16p] * WINDOW_BLOCKS), k16p, v16p, onorm_b2)


def _post_kernel(oa_ref, ob_ref, x_ref, mod_ref, gpm_ref, gpf_ref, wo_ref, wrc_ref, br_ref, cnt0_ref,
                 x1_ref, h2_ref, route_ref, cnt_ref):
    @pl.when(pl.program_id(0) == 0)
    def _():
        cnt_ref[...] = cnt0_ref[...]

    mod = mod_ref[...]
    mix = _dot(oa_ref[...], wo_ref[0:DN_WIDTH, :]) + _dot(ob_ref[...], wo_ref[DN_WIDTH:DN_WIDTH + SB_WIDTH, :])
    x1 = x_ref[...] + _per_seq(_rms(mix, gpm_ref[...]), mod, lambda y, m: y * m[:, 2:3])
    x1_ref[...] = x1
    h2 = _per_seq(_rms(x1, gpf_ref[...]), mod, lambda y, m: y * (1.0 + m[:, 4:5]) + m[:, 3:4])
    h2_ref[...] = h2
    hh, hl = _split(h2)
    hw = _dot(hh, wrc_ref[...])
    logits = hw[:, :LANES] + (_dot(hl, wrc_ref[:, :LANES]) + hw[:, LANES:]) + br_ref[...]
    lane = lax.broadcasted_iota(jnp.int32, logits.shape, 1).astype(F32)
    neg = -jnp.inf
    nl = float(LANES)
    lg = jnp.where(lane < N_GROUPS, logits, neg)
    gmax = jnp.max(lg, axis=-1, keepdims=True)
    grp = jnp.min(jnp.where(lg == gmax, lane, nl), axis=-1, keepdims=True)
    p_grp = 1.0 / jnp.sum(jnp.exp(lg - gmax), axis=-1, keepdims=True)
    first = N_GROUPS + grp * EXPERTS_PER_GROUP
    le = jnp.where((lane >= first) & (lane < first + EXPERTS_PER_GROUP), logits, neg)
    emax = jnp.max(le, axis=-1, keepdims=True)
    i1 = jnp.min(jnp.where(le == emax, lane, nl), axis=-1, keepdims=True)
    esum = jnp.sum(jnp.exp(le - emax), axis=-1, keepdims=True)
    le2 = jnp.where(lane == i1, neg, le)
    e2max = jnp.max(le2, axis=-1, keepdims=True)
    i2 = jnp.min(jnp.where(le2 == e2max, lane, nl), axis=-1, keepdims=True)
    p1 = 1.0 / esum
    p2 = jnp.exp(e2max - emax) / esum
    w1 = p_grp * p1 / (p1 + p2)
    w2 = p_grp * p2 / (p1 + p2)
    e1 = i1 - N_GROUPS
    e2 = i2 - N_GROUPS
    hot1 = (lane == e1).astype(F32)
    hot2 = (lane == e2).astype(F32)
    both = hot1 + hot2
    tm = logits.shape[0]
    sub = min(tm, RANK_SUBTILE)
    ti = lax.broadcasted_iota(jnp.int32, (sub, sub), 0)
    tj = lax.broadcasted_iota(jnp.int32, (sub, sub), 1)
    before = (ti > tj).astype(BF16)
    running = cnt_ref[...]
    earlier = []
    for r0 in range(0, tm, sub):
        part = both[r0:r0 + sub]
        earlier.append(_dot(before, part.astype(BF16)) + running)
        running = running + jnp.sum(part, axis=0, keepdims=True)
    earlier = jnp.concatenate(earlier, axis=0)
    rank1 = jnp.sum(hot1 * earlier, axis=-1, keepdims=True)
    rank2 = jnp.sum(hot2 * (earlier + hot1), axis=-1, keepdims=True)
    cnt_ref[...] = running
    out = jnp.where(lane == 0.0, e1, 0.0)
    out = jnp.where(lane == 1.0, e2, out)
    out = jnp.where(lane == 2.0, w1, out)
    out = jnp.where(lane == 3.0, w2, out)
    out = jnp.where(lane == 4.0, rank1, out)
    out = jnp.where(lane == 5.0, rank2, out)
    route_ref[...] = out


def _post(oa16, ob16, x2d, mod8, g_post_mix, g_pre_ffn, w_out16, wr_cat, b_r, cnt0, tm, seq_rows,
          mod_row0):
    n = x2d.shape[0]
    row = lambda i: (i, 0)
    const = lambda i: (0, 0)
    return pl.pallas_call(
        _post_kernel,
        grid=(n // tm,),
        in_specs=[pl.BlockSpec((tm, DN_WIDTH), row),
                  pl.BlockSpec((tm, SB_WIDTH), row),
                  pl.BlockSpec((tm, D_MODEL), row),
                  _mod_spec(tm, seq_rows, mod_row0),
                  pl.BlockSpec((1, D_MODEL), const),
                  pl.BlockSpec((1, D_MODEL), const),
                  pl.BlockSpec((D_MODEL, D_MODEL), const),
                  pl.BlockSpec((D_MODEL, 2 * LANES), const),
                  pl.BlockSpec((1, LANES), const),
                  pl.BlockSpec((1, LANES), const)],
        out_specs=[pl.BlockSpec((tm, D_MODEL), row),
                   pl.BlockSpec((tm, D_MODEL), row),
                   pl.BlockSpec((tm, LANES), row),
                   pl.BlockSpec((1, LANES), const)],
        out_shape=[jax.ShapeDtypeStruct((n, D_MODEL), F32),
                   jax.ShapeDtypeStruct((n, D_MODEL), F32),
                   jax.ShapeDtypeStruct((n, LANES), F32),
                   jax.ShapeDtypeStruct((1, LANES), F32)],
        compiler_params=_cparams(("arbitrary",)),
        name="post",
    )(oa16, ob16, x2d, mod8, g_post_mix, g_pre_ffn, w_out16, wr_cat, b_r, cnt0)


def _dispatch_kernel(seg_ref, dp_ref, ds_ref, hp_ref, hs_ref, xs_hbm, zbuf, stage, sem, stage_sems, *,
                     n_blocks, n_prompt_steps):
    i = pl.program_id(0)
    last_step = pl.num_programs(0) - 1

    @pl.when(i == 0)
    def _():
        zbuf[...] = jnp.zeros_like(zbuf)

        def zero_block(row0):
            return pltpu.make_async_copy(zbuf, xs_hbm.at[pl.ds(pl.multiple_of(row0, MOE_BLOCK), MOE_BLOCK), :], sem)

        for e in range(N_EXPERTS):
            @pl.when(seg_ref[e] > 0)
            def _():
                zero_block(seg_ref[N_EXPERTS + e] - MOE_BLOCK).start()
        for e in range(N_EXPERTS):
            @pl.when(seg_ref[e] > 0)
            def _():
                zero_block(seg_ref[N_EXPERTS + e] - MOE_BLOCK).wait()

        used = seg_ref[2 * N_EXPERTS - 1] // MOE_BLOCK

        def fill(b, c):
            cp = zero_block(b * MOE_BLOCK)
            cp.start()
            cp.wait()
            return c

        lax.fori_loop(used, n_blocks, fill, 0)

    def scatter(h_ref, dest_ref, sem_):
        for t in range(h_ref.shape[0]):
            for slot in range(2):
                pltpu.make_async_copy(h_ref.at[pl.ds(t, 1), :],
                                      xs_hbm.at[pl.ds(dest_ref[0, 0, 2 * t + slot], 1), :],
                                      sem_).start(priority=slot)

    def drain(h_ref, sem_):
        for slot in range(2):
            pltpu.make_async_copy(h_ref, xs_hbm.at[pl.ds(0, h_ref.shape[0]), :], sem_).wait()

    for buf in range(2):
        @pl.when((i < last_step) & (i % 2 == buf))
        def _():
            @pl.when(i >= 2)
            def _():
                drain(stage.at[buf], stage_sems.at[buf])
            stage[buf] = hp_ref[...]
            scatter(stage.at[buf], dp_ref, stage_sems.at[buf])

    @pl.when(i == last_step)
    def _():
        scatter(hs_ref, ds_ref, sem)
        drain(hs_ref, sem)
        for buf in range(min(2, n_prompt_steps)):
            drain(stage.at[buf], stage_sems.at[buf])


def _dispatch(seg, dest_p, dest_s, h2p, h2s, n_blocks, tm):
    n_p, n_s = h2p.shape[0], h2s.shape[0]
    steps_p = n_p // tm
    pmap3 = lambda i, sg: (jnp.minimum(i, steps_p - 1), 0, 0)
    pmap2 = lambda i, sg: (jnp.minimum(i, steps_p - 1), 0)
    grid_spec = pltpu.PrefetchScalarGridSpec(
        num_scalar_prefetch=1,
        grid=(steps_p + 1,),
        in_specs=[pl.BlockSpec((1, 1, 2 * tm), pmap3, memory_space=pltpu.SMEM),
                  pl.BlockSpec((1, 1, 2 * n_s), lambda i, sg: (0, 0, 0), memory_space=pltpu.SMEM),
                  pl.BlockSpec((tm, D_MODEL), pmap2),
                  pl.BlockSpec((n_s, D_MODEL), lambda i, sg: (0, 0))],
        out_specs=pl.BlockSpec(memory_space=pl.ANY),
        scratch_shapes=[pltpu.VMEM((MOE_BLOCK, D_MODEL), F32), pltpu.VMEM((2, tm, D_MODEL), F32),
                        pltpu.SemaphoreType.DMA(()), pltpu.SemaphoreType.DMA((2,))])
    return pl.pallas_call(
        functools.partial(_dispatch_kernel, n_blocks=n_blocks, n_prompt_steps=steps_p),
        grid_spec=grid_spec,
        out_shape=jax.ShapeDtypeStruct((n_blocks * MOE_BLOCK, D_MODEL), F32),
        compiler_params=_cparams(("arbitrary",)),
        name="dispatch",
    )(seg, dest_p.reshape(steps_p, 1, 2 * tm), dest_s.reshape(1, 1, 2 * n_s), h2p, h2s)


def _moe_kernel(blk_e_ref, nvalid_ref, next_e_ref, wslot_ref, x_ref, wg_hbm, wu_hbm, wd_hbm, y_ref,
                wg16, wu16, wd16, wg32, wu32, wd32, wsem):
    i = pl.program_id(0)
    e = blk_e_ref[i]
    used = nvalid_ref[i] > 0
    first_of_expert = used & ((i == 0) | (e != blk_e_ref[jnp.maximum(i - 1, 0)]))

    def fetch(expert, slot):
        return [pltpu.make_async_copy(src.at[expert], dst.at[slot], wsem.at[slot])
                for src, dst in ((wg_hbm, wg32), (wu_hbm, wu32), (wd_hbm, wd32))]

    for slot in range(2):
        @pl.when(first_of_expert & (wslot_ref[i] == slot))
        def _():
            @pl.when(i == 0)
            def _():
                for cp in fetch(e, slot):
                    cp.start()
            for cp in fetch(e, slot):
                cp.wait()
            wg16[...] = wg32[slot].astype(BF16)
            wu16[...] = wu32[slot].astype(BF16)
            wd16[...] = wd32[slot].astype(BF16)

            @pl.when(next_e_ref[i] < N_EXPERTS)
            def _():
                for cp in fetch(next_e_ref[i], 1 - slot):
                    cp.start()

    @pl.when(used)
    def _():
        xb = x_ref[...].astype(BF16)
        g = _dot(xb, wg16[...])
        u = _dot(xb, wu16[...])
        hmid = (_silu(g) * u).astype(BF16)
        y_ref[...] = _dot(hmid, wd16[...])

    @pl.when(nvalid_ref[i] == 0)
    def _():
        y_ref[...] = jnp.zeros_like(y_ref)


def _moe(blk_e, nvalid, next_e, wslot, x_sorted, w_gate, w_up, w_down):
    n_blocks = blk_e.shape[0]
    xmap = lambda i, be, nv, ne, ws: (jnp.where(nv[i] > 0, i, 0), 0)
    any_space = pl.BlockSpec(memory_space=pl.ANY)
    grid_spec = pltpu.PrefetchScalarGridSpec(
        num_scalar_prefetch=4,
        grid=(n_blocks,),
        in_specs=[pl.BlockSpec((MOE_BLOCK, D_MODEL), xmap), any_space, any_space, any_space],
        out_specs=pl.BlockSpec((MOE_BLOCK, D_MODEL), lambda i, be, nv, ne, ws: (i, 0)),
        scratch_shapes=[pltpu.VMEM((D_MODEL, D_EXPERT), BF16),
                        pltpu.VMEM((D_MODEL, D_EXPERT), BF16),
                        pltpu.VMEM((D_EXPERT, D_MODEL), BF16),
                        pltpu.VMEM((2, D_MODEL, D_EXPERT), F32),
                        pltpu.VMEM((2, D_MODEL, D_EXPERT), F32),
                        pltpu.VMEM((2, D_EXPERT, D_MODEL), F32),
                        pltpu.SemaphoreType.DMA((2,))])
    return pl.pallas_call(
        _moe_kernel,
        grid_spec=grid_spec,
        out_shape=jax.ShapeDtypeStruct((n_blocks * MOE_BLOCK, D_MODEL), F32),
        compiler_params=_cparams(("arbitrary",)),
        name="moe",
    )(blk_e, nvalid, next_e, wslot, x_sorted, w_gate, w_up, w_down)


def _combine_kernel(dcur_ref, dnext_ref, route_ref, x1_ref, mod_ref, g_ref, y_hbm, o_ref, ybuf, sems):
    i = pl.program_id(0)
    last = pl.num_programs(0) - 1
    tm = x1_ref.shape[0]

    def gather(dest_ref, buf):
        for t in range(tm):
            for slot in range(2):
                pltpu.make_async_copy(y_hbm.at[pl.ds(dest_ref[0, 0, 2 * t + slot], 1), :],
                                      ybuf.at[buf, slot, pl.ds(t, 1), :], sems.at[buf]).start(priority=slot)

    @pl.when(i == 0)
    def _():
        gather(dcur_ref, 0)

    for buf in range(2):
        @pl.when(i % 2 == buf)
        def _():
            @pl.when(i < last)
            def _():
                gather(dnext_ref, 1 - buf)

            for slot in range(2):
                pltpu.make_async_copy(y_hbm.at[pl.ds(0, tm), :], ybuf.at[buf, slot], sems.at[buf]).wait()
            route = route_ref[...]
            moe = ybuf[buf, 0] * route[:, 2:3] + ybuf[buf, 1] * route[:, 3:4]
            o_ref[...] = x1_ref[...] + _per_seq(_rms(moe, g_ref[...]), mod_ref[...], lambda y, m: y * m[:, 5:6])


def _combine(dest, route, y_sorted, x1, mod8, g_post_ffn, tm, seq_rows, mod_row0):
    n = x1.shape[0]
    steps = n // tm
    row = lambda i: (i, 0)
    dest3 = dest.reshape(steps, 1, 2 * tm)
    return pl.pallas_call(
        _combine_kernel,
        grid=(steps,),
        in_specs=[pl.BlockSpec((1, 1, 2 * tm), lambda i: (i, 0, 0), memory_space=pltpu.SMEM),
                  pl.BlockSpec((1, 1, 2 * tm), lambda i: (jnp.minimum(i + 1, steps - 1), 0, 0),
                               memory_space=pltpu.SMEM),
                  pl.BlockSpec((tm, LANES), row),
                  pl.BlockSpec((tm, D_MODEL), row),
                  _mod_spec(tm, seq_rows, mod_row0),
                  pl.BlockSpec((1, D_MODEL), lambda i: (0, 0)),
                  pl.BlockSpec(memory_space=pl.ANY)],
        out_specs=pl.BlockSpec((tm, D_MODEL), row),
        out_shape=jax.ShapeDtypeStruct((n, D_MODEL), F32),
        scratch_shapes=[pltpu.VMEM((2, 2, tm, D_MODEL), F32), pltpu.SemaphoreType.DMA((2,))],
        compiler_params=_cparams(("arbitrary",)),
        name="combine",
    )(dest3, dest3, route, x1, mod8, g_post_ffn, y_sorted)


def _segment_plan(counts_f, n_blocks):
    counts = counts_f[0, :N_EXPERTS].astype(jnp.int32)
    padded = (counts + MOE_BLOCK - 1) // MOE_BLOCK * MOE_BLOCK
    pad_end = jnp.cumsum(padded)
    pad_start = pad_end - padded
    blk_start = jnp.arange(n_blocks, dtype=jnp.int32) * MOE_BLOCK
    blk_e = jnp.minimum(jnp.sum((pad_end[None, :] <= blk_start[:, None]).astype(jnp.int32), axis=1),
                        N_EXPERTS - 1)
    onehot = blk_e[:, None] == jnp.arange(N_EXPERTS, dtype=jnp.int32)[None, :]
    c_blk = jnp.sum(jnp.where(onehot, counts[None, :], 0), axis=1)
    s_blk = jnp.sum(jnp.where(onehot, pad_start[None, :], 0), axis=1)
    nvalid = jnp.clip(c_blk - (blk_start - s_blk), 0, MOE_BLOCK).astype(jnp.int32)
    seg = jnp.concatenate([counts, pad_end]).astype(jnp.int32)
    ids = jnp.arange(N_EXPERTS, dtype=jnp.int32)
    later = (ids[None, :] > ids[:, None]) & (counts[None, :] > 0)
    next_e = jnp.min(jnp.where(later, ids[None, :], N_EXPERTS), axis=1)
    order = jnp.cumsum((counts > 0).astype(jnp.int32)) - 1
    pick = lambda table: jnp.sum(jnp.where(onehot, table[None, :], 0), axis=1).astype(jnp.int32)
    return blk_e.astype(jnp.int32), nvalid, pick(next_e), pick(order % 2), seg, pad_start


def _token_rows(route, pad_start):
    eid = route[:, 0:2].astype(jnp.int32)
    rank = route[:, 4:6].astype(jnp.int32)
    onehot = eid[:, :, None] == jnp.arange(N_EXPERTS, dtype=jnp.int32)[None, None, :]
    return rank + jnp.sum(jnp.where(onehot, pad_start[None, None, :], 0), axis=2)


def _layer(x_p, x_s, c_p, c_s, k_past, v_past, s0_s, conv_s, p):
    bp, tp, d = x_p.shape
    bs, ts, _ = x_s.shape
    n_p, n_s = bp * tp, bs * ts
    n_tok = n_p + n_s

    n_seq = bp + bs
    c_all = jnp.zeros((16, d), F32).at[:n_seq].set(jnp.concatenate([c_s, c_p], axis=0))
    mod = _ada(c_all, p['w_ada'], p['b_ada'])
    mod8 = jnp.pad(mod.reshape(16, 6, d), ((0, 0), (0, 2), (0, 0)))

    w_in = p['w_in']
    o_b = CONV_CH + DN_WIDTH
    o_q = o_b + 2 * DN_HEADS
    wb = jnp.pad(w_in[:, o_b:o_q], ((0, 0), (0, LANES - 2 * DN_HEADS)))
    wb_hi = wb.astype(BF16)
    wb_lo = (wb - wb_hi.astype(F32)).astype(BF16)
    w_main = jnp.concatenate([w_in[:, :o_b].astype(BF16), w_in[:, o_q:].astype(BF16), wb_hi, wb_lo], axis=1)
    g_pre_mix = p['g_pre_mix'].reshape(1, d)

    conv_w8 = jnp.pad(p['conv_w'], ((0, 8 - CONV_W), (0, 0)))
    pad_g = lambda a: jnp.pad(a.reshape(1, DN_HEADS), ((0, 0), (DN_HEADS, LANES - 2 * DN_HEADS)))
    alog_row, dtb_row = pad_g(p['a_log']), pad_g(p['dt_bias'])
    onorm_a = p['onorm_a'].reshape(1, DN_DIM)
    onorm_b2 = jnp.tile(p['onorm_b'].reshape(1, SB_DIM), (1, 2))

    w_out16 = p['w_out'].astype(BF16)
    wr = jnp.pad(jnp.concatenate([p['w_router_group'], p['w_router_expert']], axis=1),
                 ((0, 0), (0, LANES - N_GROUPS - N_EXPERTS)))
    wr_hi = wr.astype(BF16)
    wr_cat = jnp.concatenate([wr_hi, (wr - wr_hi.astype(F32)).astype(BF16)], axis=1)
    b_r = jnp.pad(jnp.concatenate([p['b_router_group'], p['b_router_expert']]).reshape(1, -1),
                  ((0, 0), (0, LANES - N_GROUPS - N_EXPERTS)))
    g_post_mix = p['g_post_mix'].reshape(1, d)
    g_pre_ffn = p['g_pre_ffn'].reshape(1, d)
    g_post_ffn = p['g_post_ffn'].reshape(1, d)

    def mixer(x, tm, tm_post, seq_rows, mod_row0, hist8, s0, k_old, v_old, chunk, n_chunks, bq, n_sub, cnt0):
        b, t, _ = x.shape
        x2d = x.reshape(b * t, d)
        a_in, z, bg, q16, kb, vb, k16, v16 = _proj(x2d, mod8, g_pre_mix, w_main, wb_hi, tm, seq_rows, mod_row0)
        r3 = lambda a: a.reshape(b, t, a.shape[-1])
        oa16, s_new = _delta(r3(a_in), r3(z), r3(bg), hist8, s0, conv_w8, alog_row, dtb_row, onorm_a,
                             chunk, n_chunks)
        k16, v16 = r3(k16), r3(v16)
        if k_old is not None:
            k16 = jnp.concatenate([k_old.reshape(b, -1, SB_WIDTH).astype(BF16), k16], axis=1)
            v16 = jnp.concatenate([v_old.reshape(b, -1, SB_WIDTH).astype(BF16), v16], axis=1)
        n_pad = (-k16.shape[1]) % KEY_TILE
        k16 = jnp.pad(k16, ((0, 0), (n_pad, 0), (0, 0)))
        v16 = jnp.pad(v16, ((0, 0), (n_pad, 0), (0, 0)))
        ob16 = _sb_attn(r3(q16), k16, v16, onorm_b2, bq, n_sub, n_pad)
        x1, h2, route, cnt = _post(oa16.reshape(b * t, DN_WIDTH), ob16.reshape(b * t, SB_WIDTH), x2d, mod8,
                                   g_post_mix, g_pre_ffn, w_out16, wr_cat, b_r, cnt0, tm_post, seq_rows,
                                   mod_row0)
        new_conv = r3(a_in)[:, t - (CONV_W - 1):, :]
        return (x1, h2, route, cnt, kb.reshape(b, t, SB_HEADS, SB_DIM), vb.reshape(b, t, SB_HEADS, SB_DIM),
                s_new, new_conv)

    zero_hist = jnp.zeros((bp, 8, CONV_CH), F32)
    zero_s = jnp.zeros((bp, DN_HEADS, DN_DIM, DN_DIM), F32)
    hist_s = jnp.pad(conv_s, ((0, 0), (8 - (CONV_W - 1), 0), (0, 0)))
    tm_p = min(ROW_TILE, tp)
    tm_dense = DENSE_TILE if n_p % DENSE_TILE == 0 else tm_p
    tm_router = ROUTER_TILE if n_p % ROUTER_TILE == 0 else tm_dense
    nc_p = max(1, min(8, tp // DELTA_BLOCK))
    x1p, h2p, rp, cnt_p, kp, vp, sp, cp = mixer(x_p, tm_dense, tm_router, tp, bs, zero_hist, zero_s, None, None,
                                                 min(DELTA_BLOCK, tp), nc_p, min(KEY_TILE // 2, tp), 2,
                                                 jnp.zeros((1, LANES), F32))
    x1s, h2s, rs, cnt, ks, vs, ss, cs = mixer(x_s, n_s, n_s, ts, 0, hist_s, s0_s, k_past, v_past,
                                               min(DELTA_BLOCK, ts), max(1, ts // DELTA_BLOCK),
                                               min(KEY_TILE, ts), 1, cnt_p)

    n_blocks = -(-2 * n_tok // MOE_BLOCK) + N_EXPERTS
    blk_e, nvalid, next_e, wslot, seg, pad_start = _segment_plan(cnt, n_blocks)
    dest_p = _token_rows(rp, pad_start)
    dest_s = _token_rows(rs, pad_start)
    x_sorted = _dispatch(seg, dest_p, dest_s, h2p, h2s, n_blocks, tm_p)
    y_sorted = _moe(blk_e, nvalid, next_e, wslot, x_sorted, p['w_gate'], p['w_up'], p['w_down'])
    y_p = _combine(dest_p, rp, y_sorted, x1p, mod8, g_post_ffn, tm_p, tp, bs).reshape(bp, tp, d)
    y_s = _combine(dest_s, rs, y_sorted, x1s, mod8, g_post_ffn, n_s, ts, 0).reshape(bs, ts, d)
    return y_p, y_s, kp, vp, sp, cp, ks, vs, ss, cs


def kernel(x_prompt, x_sample, c_prompt, c_sample, cache_k, cache_v, state_delta, state_conv, w_ada, b_ada, g_pre_mix, g_post_mix, g_pre_ffn, g_post_ffn, w_in, conv_w, a_log, dt_bias, onorm_a, onorm_b, w_out, w_router_group, b_router_group, w_router_expert, b_router_expert, w_gate, w_up, w_down):
    depth = w_in.shape[0]
    y_p, y_s = x_prompt, x_sample
    outs = [[] for _ in range(8)]
    for l in range(depth):
        p = dict(w_ada=w_ada[l], b_ada=b_ada[l], g_pre_mix=g_pre_mix[l], g_post_mix=g_post_mix[l],
                 g_pre_ffn=g_pre_ffn[l], g_post_ffn=g_post_ffn[l], w_in=w_in[l], conv_w=conv_w[l],
                 a_log=a_log[l], dt_bias=dt_bias[l], onorm_a=onorm_a[l], onorm_b=onorm_b[l],
                 w_out=w_out[l], w_router_group=w_router_group[l], b_router_group=b_router_group[l],
                 w_router_expert=w_router_expert[l], b_router_expert=b_router_expert[l],
                 w_gate=w_gate[l], w_up=w_up[l], w_down=w_down[l])
        res = _layer(y_p, y_s, c_prompt, c_sample, cache_k[l], cache_v[l], state_delta[l], state_conv[l], p)
        y_p, y_s = res[0], res[1]
        for lst, r in zip(outs, res[2:]):
            lst.append(r)
    return (y_p, y_s) + tuple(jnp.stack(o) for o in outs)
```

```python
import functools

import jax
import jax.numpy as jnp
from jax import lax
from jax.experimental import pallas as pl
from jax.experimental.pallas import tpu as pltpu

F32 = jnp.float32
BF16 = jnp.bfloat16

D_MODEL = 1024
DN_HEADS = 4
DN_DIM = 128
CONV_W = 4
DN_WIDTH = DN_HEADS * DN_DIM
CONV_CH = 3 * DN_WIDTH
DELTA_BLOCK = 64
SB_HEADS = 8
SB_DIM = 64
SB_WIDTH = SB_HEADS * SB_DIM
N_GROUPS = 4
EXPERTS_PER_GROUP = 8
N_EXPERTS = N_GROUPS * EXPERTS_PER_GROUP
D_EXPERT = D_MODEL // 2
MOE_BLOCK = 256
EPS = 1e-6

LANES = 128
KEY_TILE = 128
ATTN_WINDOW = 3 * KEY_TILE
WINDOW_BLOCKS = 4
EXP_ZERO_BELOW = -104.0
VMEM_LIMIT = 56 * 1024 * 1024
DENSE_TILE = 512
ROUTER_TILE = 1024
ROW_TILE = 512
RANK_SUBTILE = 256


def _cparams(sem):
    return pltpu.CompilerParams(dimension_semantics=sem, vmem_limit_bytes=VMEM_LIMIT)


def _split(a):
    hi = a.astype(BF16)
    lo = (a - hi.astype(F32)).astype(BF16)
    return hi, lo


def _dot(a, b, dims=(((1,), (0,)), ((), ()))):
    return lax.dot_general(a, b, dims, preferred_element_type=F32)


def _dot_small_int_lhs(a, b):
    a16 = a.astype(BF16)
    return _dot(jnp.concatenate([a16, a16], axis=1), jnp.concatenate(_split(b), axis=0))


_NT = (((1,), (1,)), ((), ()))
_TN = (((0,), (0,)), ((), ()))


def _silu(x):
    return x * jax.nn.sigmoid(x)


SOFTPLUS_LINEAR_ABOVE = 80.0


def _softplus(x):
    return jnp.where(x > SOFTPLUS_LINEAR_ABOVE, x, jnp.log(1.0 + jnp.exp(jnp.minimum(x, SOFTPLUS_LINEAR_ABOVE))))


def _ada_kernel(c_ref, w_ref, b_ref, o_ref):
    s = _silu(c_ref[...]).astype(BF16)
    o_ref[...] = _dot(s, w_ref[...].astype(BF16)) + b_ref[...]


def _ada(c_all, w_ada, b_ada):
    rows = c_all.shape[0]
    n = w_ada.shape[1]
    tn = 1024
    return pl.pallas_call(
        _ada_kernel,
        grid=(n // tn,),
        in_specs=[pl.BlockSpec((rows, D_MODEL), lambda j: (0, 0)),
                  pl.BlockSpec((D_MODEL, tn), lambda j: (0, j)),
                  pl.BlockSpec((1, tn), lambda j: (0, j))],
        out_specs=pl.BlockSpec((rows, tn), lambda j: (0, j)),
        out_shape=jax.ShapeDtypeStruct((rows, n), F32),
        compiler_params=_cparams(("arbitrary",)),
        name="ada",
    )(c_all, w_ada, b_ada.reshape(1, n))


def _rms(x, gain):
    return x * lax.rsqrt(jnp.mean(x * x, axis=-1, keepdims=True) + EPS) * gain


def _per_seq(y, mod, fn):
    n_seq = mod.shape[0]
    ys = y.reshape(n_seq, y.shape[0] // n_seq, y.shape[1])
    return fn(ys, mod).reshape(y.shape)


def _mod_spec(tm, seq_rows, mod_row0):
    if tm >= seq_rows:
        n_seq = tm // seq_rows
        assert tm % seq_rows == 0 and mod_row0 % n_seq == 0
        return pl.BlockSpec((n_seq, 8, D_MODEL), lambda i: (mod_row0 // n_seq + i, 0, 0))
    assert seq_rows % tm == 0
    return pl.BlockSpec((1, 8, D_MODEL), lambda i: (mod_row0 + (i * tm) // seq_rows, 0, 0))


def _proj_kernel(x_ref, mod_ref, g_ref, wm_ref, wbh_ref,
                 a_ref, z_ref, bg_ref, q_ref, k_ref, v_ref, k16_ref, v16_ref):
    h = _per_seq(_rms(x_ref[...], g_ref[...]), mod_ref[...], lambda y, m: y * (1.0 + m[:, 1:2]) + m[:, 0:1])
    hh, hl = _split(h)
    p = _dot(hh, wm_ref[...])
    a_ref[...] = p[:, 0:CONV_CH]
    z_ref[...] = p[:, CONV_CH:CONV_CH + DN_WIDTH]
    o = CONV_CH + DN_WIDTH
    q_ref[...] = (p[:, o:o + SB_WIDTH] * (SB_DIM ** -0.5)).astype(BF16)
    k = p[:, o + SB_WIDTH:o + 2 * SB_WIDTH]
    v = p[:, o + 2 * SB_WIDTH:o + 3 * SB_WIDTH]
    k_ref[...] = k
    v_ref[...] = v
    k16_ref[...] = k.astype(BF16)
    v16_ref[...] = v.astype(BF16)
    o += 3 * SB_WIDTH
    bg_ref[...] = p[:, o:o + LANES] + (_dot(hl, wbh_ref[...]) + p[:, o + LANES:])


def _proj(x2d, mod8, g_pre, w_main, wb_hi, tm, seq_rows, mod_row0):
    n = x2d.shape[0]
    nm = w_main.shape[1]
    row = lambda i: (i, 0)
    const = lambda i: (0, 0)
    outs = [(CONV_CH, F32), (DN_WIDTH, F32), (LANES, F32), (SB_WIDTH, BF16), (SB_WIDTH, F32), (SB_WIDTH, F32),
            (SB_WIDTH, BF16), (SB_WIDTH, BF16)]
    return pl.pallas_call(
        _proj_kernel,
        grid=(n // tm,),
        in_specs=[pl.BlockSpec((tm, D_MODEL), row),
                  _mod_spec(tm, seq_rows, mod_row0),
                  pl.BlockSpec((1, D_MODEL), const),
                  pl.BlockSpec((D_MODEL, nm), const),
                  pl.BlockSpec((D_MODEL, LANES), const)],
        out_specs=[pl.BlockSpec((tm, w), row) for w, _ in outs],
        out_shape=[jax.ShapeDtypeStruct((n, w), dt) for w, dt in outs],
        compiler_params=_cparams(("arbitrary",)),
        name="proj",
    )(x2d, mod8, g_pre, w_main, wb_hi)


def _delta_kernel(a_ref, z_ref, bg_ref, hist0_ref, s0_ref, cw_ref, alog_ref, dtb_ref, on_ref,
                  o_ref, sfin_ref, hist_sc, s_sc, *, chunk, n_chunks):
    t_idx = pl.program_id(1)
    tt = chunk * n_chunks

    @pl.when(t_idx == 0)
    def _():
        hist_sc[...] = hist0_ref[0]
        s_sc[...] = s0_ref[0]

    x = a_ref[0]
    xx = jnp.concatenate([hist_sc[...], x], axis=0)
    cw = cw_ref[...]
    conv = x * cw[CONV_W - 1:CONV_W]
    for s in range(1, CONV_W):
        conv = conv + pltpu.roll(xx, s, 0)[8:] * cw[CONV_W - 1 - s:CONV_W - s]
    conv = _silu(conv)
    hist_sc[...] = x[tt - 8:tt]

    bg = bg_ref[0]
    lane = lax.broadcasted_iota(jnp.int32, (1, LANES), 1)
    g_lane = (lane >= DN_HEADS) & (lane < 2 * DN_HEADS)
    neg_a = jnp.where(g_lane, -jnp.exp(alog_ref[...]), 0.0)
    beta_all = jax.nn.sigmoid(bg)
    g_all = neg_a * _softplus(bg + dtb_ref[...])

    ri = lax.broadcasted_iota(jnp.int32, (chunk, chunk), 0)
    ci = lax.broadcasted_iota(jnp.int32, (chunk, chunk), 1)
    incl = ri >= ci
    strict = ri > ci
    tri = incl.astype(F32)
    eye = (ri == ci).astype(F32)
    ones_cc = jnp.ones((chunk, chunk), F32)
    onorm = on_ref[...]

    rows = lambda c: slice(c * chunk, (c + 1) * chunk)
    hs = range(DN_HEADS)
    prep = {}

    def prepare(chunk_ids):
        units = [(c, h) for c in chunk_ids for h in hs]
        gc_all = {c: _dot_small_int_lhs(tri, g_all[rows(c)]) for c in chunk_ids}
        yield
        diag = {c: jnp.concatenate([eye * gc_all[c][:, DN_HEADS + h:DN_HEADS + h + 1] for h in hs], axis=1)
                for c in chunk_ids}
        gc_rows = {c: _dot_small_int_lhs(ones_cc, diag[c]) for c in chunk_ids}
        yield
        q, k, v, beta, gcol, glast, eg, decay, kbeta = {}, {}, {}, {}, {}, {}, {}, {}, {}
        for (c, h) in units:
            lo = h * DN_DIM
            qq = conv[rows(c), lo:lo + DN_DIM]
            kk = conv[rows(c), DN_WIDTH + lo:DN_WIDTH + lo + DN_DIM]
            u_ = (c, h)
            q[u_] = qq * lax.rsqrt(jnp.sum(qq * qq, axis=-1, keepdims=True) + EPS) * (DN_DIM ** -0.5)
            k[u_] = kk * lax.rsqrt(jnp.sum(kk * kk, axis=-1, keepdims=True) + EPS)
            v[u_] = conv[rows(c), 2 * DN_WIDTH + lo:2 * DN_WIDTH + lo + DN_DIM]
            beta[u_] = beta_all[rows(c), h:h + 1]
            gcol[u_] = gc_all[c][:, DN_HEADS + h:DN_HEADS + h + 1]
            glast[u_] = gcol[u_][chunk - 1:chunk, :]
            eg[u_] = jnp.exp(gcol[u_])
            grow = gc_rows[c][:, h * chunk:(h + 1) * chunk]
            decay[u_] = jnp.where(incl, jnp.exp(jnp.minimum(gcol[u_] - grow, 0.0)), 0.0)
            kbeta[u_] = k[u_] * beta[u_]

        k16 = {u_: k[u_].astype(BF16) for u_ in units}
        kq = {u_: _dot(jnp.concatenate([kbeta[u_], q[u_]], axis=0).astype(BF16), k16[u_], _NT) for u_ in units}
        yield
        lower = {u_: jnp.where(strict, kq[u_][:chunk] * decay[u_], 0.0) for u_ in units}
        attn16 = {u_: jnp.where(incl, kq[u_][chunk:] * decay[u_], 0.0).astype(BF16) for u_ in units}

        sol = {u_: jnp.concatenate([v[u_] * beta[u_], kbeta[u_] * eg[u_]], axis=1) for u_ in units}
        lp = lower
        p = 1
        while p < chunk:
            lsp = {u_: _split(lp[u_]) for u_ in units}
            ssp = {u_: _split(sol[u_]) for u_ in units}
            lcat = {u_: jnp.concatenate([lsp[u_][0], lsp[u_][1], lsp[u_][0]], axis=1) for u_ in units}
            upd = {u_: _dot(lcat[u_], jnp.concatenate([ssp[u_][0], ssp[u_][0], ssp[u_][1]], axis=0))
                   for u_ in units}
            if 2 * p < chunk:
                lp = {u_: _dot(lcat[u_], jnp.concatenate([lsp[u_][0], lsp[u_][0], lsp[u_][1]], axis=0))
                      for u_ in units}
            yield
            sol = {u_: (sol[u_] - upd[u_]) if p == 1 else (sol[u_] + upd[u_]) for u_ in units}
            p *= 2

        for u_ in units:
            prep[u_] = dict(
                u=sol[u_][:, :DN_DIM],
                wq=jnp.concatenate([sol[u_][:, DN_DIM:], q[u_] * eg[u_]], axis=0).astype(BF16),
                kd=(k[u_] * jnp.exp(glast[u_] - gcol[u_])).astype(BF16),
                attn=attn16[u_], decay_all=jnp.exp(glast[u_]))

    def recur(chunk_ids):
        for c in chunk_ids:
            s_old = [s_sc[h] for h in hs]
            ws = [_dot(prep[(c, h)]['wq'], s_old[h].astype(BF16)) for h in hs]
            yield
            v_new = [(prep[(c, h)]['u'] - ws[h][:chunk]).astype(BF16) for h in hs]
            o_in = [_dot(prep[(c, h)]['attn'], v_new[h]) for h in hs]
            ds = [_dot(prep[(c, h)]['kd'], v_new[h], _TN) for h in hs]
            yield
            for h in hs:
                lo = h * DN_DIM
                s_sc[h] = s_old[h] * prep[(c, h)]['decay_all'] + ds[h]
                o = ws[h][chunk:] + o_in[h]
                zg = z_ref[0, rows(c), lo:lo + DN_DIM]
                o_ref[0, rows(c), lo:lo + DN_DIM] = (_rms(o, onorm) * _silu(zg)).astype(BF16)

    def run(*streams):
        live = list(streams)
        while live:
            for g_ in list(live):
                if next(g_, done_) is done_:
                    live.remove(g_)

    done_ = object()
    ids = list(range(n_chunks))
    first, second = (ids[:n_chunks // 2], ids[n_chunks // 2:]) if n_chunks >= 2 else (ids, [])
    run(prepare(first))
    run(recur(first), prepare(second))
    run(recur(second))

    @pl.when(t_idx == pl.num_programs(1) - 1)
    def _():
        sfin_ref[0] = s_sc[...]


def _delta(a_in, z, bg, hist8, s0, conv_w8, alog_row, dtb_row, onorm_a, chunk, n_chunks):
    b, t, _ = a_in.shape
    tt = chunk * n_chunks
    tile = lambda bi, ti: (bi, ti, 0)
    per_b3 = lambda bi, ti: (bi, 0, 0)
    per_b4 = lambda bi, ti: (bi, 0, 0, 0)
    const = lambda bi, ti: (0, 0)
    kern = functools.partial(_delta_kernel, chunk=chunk, n_chunks=n_chunks)
    return pl.pallas_call(
        kern,
        grid=(b, t // tt),
        in_specs=[pl.BlockSpec((1, tt, CONV_CH), tile),
                  pl.BlockSpec((1, tt, DN_WIDTH), tile),
                  pl.BlockSpec((1, tt, LANES), tile),
                  pl.BlockSpec((1, 8, CONV_CH), per_b3),
                  pl.BlockSpec((1, DN_HEADS, DN_DIM, DN_DIM), per_b4),
                  pl.BlockSpec((8, CONV_CH), const),
                  pl.BlockSpec((1, LANES), const),
                  pl.BlockSpec((1, LANES), const),
                  pl.BlockSpec((1, DN_DIM), const)],
        out_specs=[pl.BlockSpec((1, tt, DN_WIDTH), tile),
                   pl.BlockSpec((1, DN_HEADS, DN_DIM, DN_DIM), per_b4)],
        out_shape=[jax.ShapeDtypeStruct((b, t, DN_WIDTH), BF16),
                   jax.ShapeDtypeStruct((b, DN_HEADS, DN_DIM, DN_DIM), F32)],
        scratch_shapes=[pltpu.VMEM((8, CONV_CH), F32),
                        pltpu.VMEM((DN_HEADS, DN_DIM, DN_DIM), F32)],
        compiler_params=_cparams(("arbitrary", "arbitrary")),
        name="delta",
    )(a_in, z, bg, hist8, s0, conv_w8, alog_row, dtb_row, onorm_a)


def _sb_kernel(q_ref, *refs, bq, n_sub, n_pad, q_off):
    kwin = refs[:WINDOW_BLOCKS]
    vwin = refs[WINDOW_BLOCKS:2 * WINDOW_BLOCKS]
    k_hbm, v_hbm, on_ref, o_ref, kbuf, vbuf, qsel, acc, carry, sem = refs[2 * WINDOW_BLOCKS:]
    b = pl.program_id(0)
    i = pl.program_id(1)
    g = bq * n_sub
    qend_step = q_off + (i + 1) * g
    win_start = qend_step - WINDOW_BLOCKS * KEY_TILE
    n_pairs = SB_HEADS // 2
    heads = range(SB_HEADS)

    half_lane = lax.broadcasted_iota(jnp.int32, (bq, LANES), 1) < SB_DIM
    rj = lax.broadcasted_iota(jnp.int32, (2 * KEY_TILE, 2 * KEY_TILE), 0) % KEY_TILE
    cj = lax.broadcasted_iota(jnp.int32, (2 * KEY_TILE, 2 * KEY_TILE), 1)
    suffix2 = ((rj > cj) | (cj >= KEY_TILE)).astype(BF16)

    def window(blocks, off, p):
        parts = []
        for blk in range(WINDOW_BLOCKS):
            lo, hi = max(off, blk * KEY_TILE), min(off + ATTN_WINDOW, (blk + 1) * KEY_TILE)
            if lo < hi:
                ref = blocks[WINDOW_BLOCKS - 1 - blk]
                parts.append(ref[0, lo - blk * KEY_TILE:hi - blk * KEY_TILE, p * LANES:(p + 1) * LANES])
        return jnp.concatenate(parts, axis=0)

    n_t = ATTN_WINDOW // KEY_TILE
    row = lax.broadcasted_iota(jnp.int32, (bq, ATTN_WINDOW), 0)
    col = lax.broadcasted_iota(jnp.int32, (bq, ATTN_WINDOW), 1)
    causal = col < row + (ATTN_WINDOW - bq)
    subs = range(n_sub)
    offs = [WINDOW_BLOCKS * KEY_TILE - g + (s + 1) * bq - ATTN_WINDOW for s in subs]
    mask = [causal & (col >= n_pad - (win_start + offs[s])) for s in subs]
    zs, sp, cs_all = {}, {}, {}
    for s in subs:
        for p in range(n_pairs):
            qf = q_ref[0, s * bq:(s + 1) * bq, p * LANES:(p + 1) * LANES].astype(F32)
            qq = jnp.concatenate([jnp.where(half_lane, qf, 0.0), jnp.where(half_lane, 0.0, qf)], axis=0)
            zz = _dot(qq.astype(BF16), window(kwin, offs[s], p), _NT)
            zs[(s, 2 * p)], zs[(s, 2 * p + 1)] = zz[:bq], zz[bq:]
    for s in subs:
        pieces = []
        for h in heads:
            sp[(s, h)] = _softplus(zs[(s, h)])
            hi, lo = _split(jnp.where(mask[s], -sp[(s, h)], 0.0))
            for j in range(n_t):
                c0 = ATTN_WINDOW - (j + 1) * KEY_TILE
                pieces.append(jnp.concatenate([hi[:, c0:c0 + KEY_TILE], lo[:, c0:c0 + KEY_TILE]], axis=1))
        cs_all[s] = _dot(jnp.concatenate(pieces, axis=0), suffix2)
    done = []
    for s in subs:
        worst = jnp.full((bq, KEY_TILE), -jnp.inf, F32)
        a = {}
        for h in heads:
            run = None
            cols = []
            for j in range(n_t):
                r0 = (h * n_t + j) * bq
                cs = cs_all[s][r0:r0 + bq]
                cols.append(cs[:, :KEY_TILE] if run is None else cs[:, :KEY_TILE] + run)
                run = cs[:, KEY_TILE:] if run is None else run + cs[:, KEY_TILE:]
            within = jnp.concatenate(cols[::-1], axis=1)
            a[h] = jnp.where(mask[s], jnp.exp((zs[(s, h)] - sp[(s, h)]) + within), 0.0).astype(BF16)
            carry[s, h] = run
            worst = jnp.maximum(worst, run)
        for p in range(n_pairs):
            pv = _dot(jnp.concatenate([a[2 * p], a[2 * p + 1]], axis=0), window(vwin, offs[s], p))
            acc[s, p] = jnp.where(half_lane, pv[:bq], pv[bq:])
        done.append((jnp.max(worst) < EXP_ZERO_BELOW).astype(jnp.int32))

    rj1 = lax.broadcasted_iota(jnp.int32, (KEY_TILE, 2 * KEY_TILE), 0)
    cj1 = lax.broadcasted_iota(jnp.int32, (KEY_TILE, 2 * KEY_TILE), 1)
    suffix1 = ((rj1 > cj1) | (cj1 >= KEY_TILE)).astype(BF16)
    col1 = lax.broadcasted_iota(jnp.int32, (bq, KEY_TILE), 1)
    for s in subs:
        swept_from = win_start + offs[s]

        @pl.when((done[s] == 0) & (swept_from > n_pad))
        def _():
            for p in range(n_pairs):
                qf = q_ref[0, s * bq:(s + 1) * bq, p * LANES:(p + 1) * LANES].astype(F32)
                qsel[2 * p] = jnp.where(half_lane, qf, 0.0).astype(BF16)
                qsel[2 * p + 1] = jnp.where(half_lane, 0.0, qf).astype(BF16)

            def body(state):
                upper, _ = state
                start = jnp.maximum(upper - KEY_TILE, 0)
                copies = []
                for p in range(n_pairs):
                    for src, dst in ((k_hbm, kbuf), (v_hbm, vbuf)):
                        cp = pltpu.make_async_copy(
                            src.at[b, pl.ds(pl.multiple_of(start, 16), KEY_TILE), pl.ds(p * LANES, LANES)],
                            dst.at[p], sem)
                        cp.start()
                        copies.append(cp)
                for cp in copies:
                    cp.wait()
                kpos = start + col1
                m1 = (kpos < upper) & (kpos >= n_pad)
                zz = [_dot(qsel[h], kbuf[h // 2], _NT) for h in heads]
                spp = [_softplus(zz[h]) for h in heads]
                pcs = []
                for h in heads:
                    pcs.extend(_split(jnp.where(m1, -spp[h], 0.0)))
                cs1 = _dot(jnp.concatenate(pcs, axis=0), suffix1)
                worst = jnp.full((bq, KEY_TILE), -jnp.inf, F32)
                aa = []
                for h in heads:
                    c = cs1[2 * h * bq:(2 * h + 1) * bq] + cs1[(2 * h + 1) * bq:(2 * h + 2) * bq]
                    c_old = carry[s, h]
                    aa.append(jnp.where(m1, jnp.exp((zz[h] - spp[h]) + c[:, :KEY_TILE] + c_old), 0.0).astype(BF16))
                    c_new = c_old + c[:, KEY_TILE:]
                    carry[s, h] = c_new
                    worst = jnp.maximum(worst, c_new)
                for p in range(n_pairs):
                    pv0 = _dot(aa[2 * p], vbuf[p])
                    pv1 = _dot(aa[2 * p + 1], vbuf[p])
                    acc[s, p] = acc[s, p] + jnp.where(half_lane, pv0, pv1)
                return start, (jnp.max(worst) < EXP_ZERO_BELOW).astype(jnp.int32)

            lax.while_loop(lambda st: (st[0] > n_pad) & (st[1] == 0), body, (swept_from, jnp.int32(0)))

    onb = on_ref[...]
    for s in subs:
        for p in range(n_pairs):
            o = acc[s, p]
            sq = o * o
            s_lo = jnp.sum(jnp.where(half_lane, sq, 0.0), axis=-1, keepdims=True)
            s_hi = jnp.sum(jnp.where(half_lane, 0.0, sq), axis=-1, keepdims=True)
            ms = jnp.where(half_lane, s_lo, s_hi) * (1.0 / SB_DIM)
            o_ref[0, s * bq:(s + 1) * bq, p * LANES:(p + 1) * LANES] = (o * lax.rsqrt(ms + EPS) * onb).astype(BF16)


def _sb_attn(q16, k16p, v16p, onorm_b2, bq, n_sub, n_pad):
    b, tq, _ = q16.shape
    tkp = k16p.shape[1]
    q_off = tkp - tq
    g = bq * n_sub
    assert tq % g == 0 and g <= KEY_TILE and bq % 16 == 0
    assert all((q_off + (i + 1) * g) % KEY_TILE == 0 for i in range(tq // g))

    def kmap(back):
        def f(bi, i):
            last = (q_off + (i + 1) * g) // KEY_TILE - 1
            return (bi, jnp.maximum(last - back, 0), 0)
        return f

    qmap = lambda bi, i: (bi, i, 0)
    kern = functools.partial(_sb_kernel, bq=bq, n_sub=n_sub, n_pad=n_pad, q_off=q_off)
    kspec = [pl.BlockSpec((1, KEY_TILE, SB_WIDTH), kmap(back)) for back in range(WINDOW_BLOCKS)]
    return pl.pallas_call(
        kern,
        grid=(b, tq // g),
        in_specs=[pl.BlockSpec((1, g, SB_WIDTH), qmap)] + kspec + kspec
                 + [pl.BlockSpec(memory_space=pl.ANY), pl.BlockSpec(memory_space=pl.ANY),
                    pl.BlockSpec((1, LANES), lambda bi, i: (0, 0))],
        out_specs=pl.BlockSpec((1, g, SB_WIDTH), qmap),
        out_shape=jax.ShapeDtypeStruct((b, tq, SB_WIDTH), BF16),
        scratch_shapes=[pltpu.VMEM((SB_HEADS // 2, KEY_TILE, LANES), BF16),
                        pltpu.VMEM((SB_HEADS // 2, KEY_TILE, LANES), BF16),
                        pltpu.VMEM((SB_HEADS, bq, LANES), BF16),
                        pltpu.VMEM((n_sub, SB_HEADS // 2, bq, LANES), F32),
                        pltpu.VMEM((n_sub, SB_HEADS, bq, KEY_TILE), F32),
                        pltpu.SemaphoreType.DMA(())],
        compiler_params=_cparams(("arbitrary", "arbitrary")),
        name="sb_attn",
    )(q16, *([k16p] * WINDOW_BLOCKS), *([v16p] * WINDOW_BLOCKS), k16p, v16p, onorm_b2)


def _post_kernel(oa_ref, ob_ref, x_ref, mod_ref, gpm_ref, gpf_ref, wo_ref, wrc_ref, br_ref, cnt0_ref,
                 x1_ref, h2_ref, route_ref, cnt_ref):
    @pl.when(pl.program_id(0) == 0)
    def _():
        cnt_ref[...] = cnt0_ref[...]

    mod = mod_ref[...]
    mix = _dot(oa_ref[...], wo_ref[0:DN_WIDTH, :]) + _dot(ob_ref[...], wo_ref[DN_WIDTH:DN_WIDTH + SB_WIDTH, :])
    x1 = x_ref[...] + _per_seq(_rms(mix, gpm_ref[...]), mod, lambda y, m: y * m[:, 2:3])
    x1_ref[...] = x1
    h2 = _per_seq(_rms(x1, gpf_ref[...]), mod, lambda y, m: y * (1.0 + m[:, 4:5]) + m[:, 3:4])
    h2_ref[...] = h2
    hh, hl = _split(h2)
    hw = _dot(hh, wrc_ref[...])
    logits = hw[:, :LANES] + (_dot(hl, wrc_ref[:, :LANES]) + hw[:, LANES:]) + br_ref[...]
    lane = lax.broadcasted_iota(jnp.int32, logits.shape, 1).astype(F32)
    neg = -jnp.inf
    nl = float(LANES)
    lg = jnp.where(lane < N_GROUPS, logits, neg)
    gmax = jnp.max(lg, axis=-1, keepdims=True)
    grp = jnp.min(jnp.where(lg == gmax, lane, nl), axis=-1, keepdims=True)
    p_grp = 1.0 / jnp.sum(jnp.exp(lg - gmax), axis=-1, keepdims=True)
    first = N_GROUPS + grp * EXPERTS_PER_GROUP
    le = jnp.where((lane >= first) & (lane < first + EXPERTS_PER_GROUP), logits, neg)
    emax = jnp.max(le, axis=-1, keepdims=True)
    i1 = jnp.min(jnp.where(le == emax, lane, nl), axis=-1, keepdims=True)
    esum = jnp.sum(jnp.exp(le - emax), axis=-1, keepdims=True)
    le2 = jnp.where(lane == i1, neg, le)
    e2max = jnp.max(le2, axis=-1, keepdims=True)
    i2 = jnp.min(jnp.where(le2 == e2max, lane, nl), axis=-1, keepdims=True)
    p1 = 1.0 / esum
    p2 = jnp.exp(e2max - emax) / esum
    w1 = p_grp * p1 / (p1 + p2)
    w2 = p_grp * p2 / (p1 + p2)
    e1 = i1 - N_GROUPS
    e2 = i2 - N_GROUPS
    hot1 = (lane == e1).astype(F32)
    hot2 = (lane == e2).astype(F32)
    both = hot1 + hot2
    tm = logits.shape[0]
    sub = min(tm, RANK_SUBTILE)
    ti = lax.broadcasted_iota(jnp.int32, (sub, sub), 0)
    tj = lax.broadcasted_iota(jnp.int32, (sub, sub), 1)
    before = (ti > tj).astype(BF16)
    running = cnt_ref[...]
    earlier = []
    for r0 in range(0, tm, sub):
        part = both[r0:r0 + sub]
        earlier.append(_dot(before, part.astype(BF16)) + running)
        running = running + jnp.sum(part, axis=0, keepdims=True)
    earlier = jnp.concatenate(earlier, axis=0)
    rank1 = jnp.sum(hot1 * earlier, axis=-1, keepdims=True)
    rank2 = jnp.sum(hot2 * (earlier + hot1), axis=-1, keepdims=True)
    cnt_ref[...] = running
    out = jnp.where(lane == 0.0, e1, 0.0)
    out = jnp.where(lane == 1.0, e2, out)
    out = jnp.where(lane == 2.0, w1, out)
    out = jnp.where(lane == 3.0, w2, out)
    out = jnp.where(lane == 4.0, rank1, out)
    out = jnp.where(lane == 5.0, rank2, out)
    route_ref[...] = out


def _post(oa16, ob16, x2d, mod8, g_post_mix, g_pre_ffn, w_out16, wr_cat, b_r, cnt0, tm, seq_rows,
          mod_row0):
    n = x2d.shape[0]
    row = lambda i: (i, 0)
    const = lambda i: (0, 0)
    return pl.pallas_call(
        _post_kernel,
        grid=(n // tm,),
        in_specs=[pl.BlockSpec((tm, DN_WIDTH), row),
                  pl.BlockSpec((tm, SB_WIDTH), row),
                  pl.BlockSpec((tm, D_MODEL), row),
                  _mod_spec(tm, seq_rows, mod_row0),
                  pl.BlockSpec((1, D_MODEL), const),
                  pl.BlockSpec((1, D_MODEL), const),
                  pl.BlockSpec((D_MODEL, D_MODEL), const),
                  pl.BlockSpec((D_MODEL, 2 * LANES), const),
                  pl.BlockSpec((1, LANES), const),
                  pl.BlockSpec((1, LANES), const)],
        out_specs=[pl.BlockSpec((tm, D_MODEL), row),
                   pl.BlockSpec((tm, D_MODEL), row),
                   pl.BlockSpec((tm, LANES), row),
                   pl.BlockSpec((1, LANES), const)],
        out_shape=[jax.ShapeDtypeStruct((n, D_MODEL), F32),
                   jax.ShapeDtypeStruct((n, D_MODEL), F32),
                   jax.ShapeDtypeStruct((n, LANES), F32),
                   jax.ShapeDtypeStruct((1, LANES), F32)],
        compiler_params=_cparams(("arbitrary",)),
        name="post",
    )(oa16, ob16, x2d, mod8, g_post_mix, g_pre_ffn, w_out16, wr_cat, b_r, cnt0)


def _dispatch_kernel(seg_ref, dp_ref, ds_ref, hp_ref, hs_ref, xs_hbm, zbuf, stage, sem, stage_sems, *,
                     n_blocks, n_prompt_steps):
    i = pl.program_id(0)
    last_step = pl.num_programs(0) - 1

    @pl.when(i == 0)
    def _():
        zbuf[...] = jnp.zeros_like(zbuf)

        def zero_block(row0):
            return pltpu.make_async_copy(zbuf, xs_hbm.at[pl.ds(pl.multiple_of(row0, MOE_BLOCK), MOE_BLOCK), :], sem)

        for e in range(N_EXPERTS):
            @pl.when(seg_ref[e] > 0)
            def _():
                zero_block(seg_ref[N_EXPERTS + e] - MOE_BLOCK).start()
        for e in range(N_EXPERTS):
            @pl.when(seg_ref[e] > 0)
            def _():
                zero_block(seg_ref[N_EXPERTS + e] - MOE_BLOCK).wait()

        used = seg_ref[2 * N_EXPERTS - 1] // MOE_BLOCK

        def fill(b, c):
            cp = zero_block(b * MOE_BLOCK)
            cp.start()
            cp.wait()
            return c

        lax.fori_loop(used, n_blocks, fill, 0)

    def scatter(h_ref, dest_ref, sem_):
        for t in range(h_ref.shape[0]):
            for slot in range(2):
                pltpu.make_async_copy(h_ref.at[pl.ds(t, 1), :],
                                      xs_hbm.at[pl.ds(dest_ref[0, 0, 2 * t + slot], 1), :],
                                      sem_).start(priority=slot)

    def drain(h_ref, sem_):
        for slot in range(2):
            pltpu.make_async_copy(h_ref, xs_hbm.at[pl.ds(0, h_ref.shape[0]), :], sem_).wait()

    for buf in range(2):
        @pl.when((i < last_step) & (i % 2 == buf))
        def _():
            @pl.when(i >= 2)
            def _():
                drain(stage.at[buf], stage_sems.at[buf])
            stage[buf] = hp_ref[...]
            scatter(stage.at[buf], dp_ref, stage_sems.at[buf])

    @pl.when(i == last_step)
    def _():
        scatter(hs_ref, ds_ref, sem)
        drain(hs_ref, sem)
        for buf in range(min(2, n_prompt_steps)):
            drain(stage.at[buf], stage_sems.at[buf])


def _dispatch(seg, dest_p, dest_s, h2p, h2s, n_blocks, tm):
    n_p, n_s = h2p.shape[0], h2s.shape[0]
    steps_p = n_p // tm
    pmap3 = lambda i, sg: (jnp.minimum(i, steps_p - 1), 0, 0)
    pmap2 = lambda i, sg: (jnp.minimum(i, steps_p - 1), 0)
    grid_spec = pltpu.PrefetchScalarGridSpec(
        num_scalar_prefetch=1,
        grid=(steps_p + 1,),
        in_specs=[pl.BlockSpec((1, 1, 2 * tm), pmap3, memory_space=pltpu.SMEM),
                  pl.BlockSpec((1, 1, 2 * n_s), lambda i, sg: (0, 0, 0), memory_space=pltpu.SMEM),
                  pl.BlockSpec((tm, D_MODEL), pmap2),
                  pl.BlockSpec((n_s, D_MODEL), lambda i, sg: (0, 0))],
        out_specs=pl.BlockSpec(memory_space=pl.ANY),
        scratch_shapes=[pltpu.VMEM((MOE_BLOCK, D_MODEL), F32), pltpu.VMEM((2, tm, D_MODEL), F32),
                        pltpu.SemaphoreType.DMA(()), pltpu.SemaphoreType.DMA((2,))])
    return pl.pallas_call(
        functools.partial(_dispatch_kernel, n_blocks=n_blocks, n_prompt_steps=steps_p),
        grid_spec=grid_spec,
        out_shape=jax.ShapeDtypeStruct((n_blocks * MOE_BLOCK, D_MODEL), F32),
        compiler_params=_cparams(("arbitrary",)),
        name="dispatch",
    )(seg, dest_p.reshape(steps_p, 1, 2 * tm), dest_s.reshape(1, 1, 2 * n_s), h2p, h2s)


def _moe_kernel(blk_e_ref, nvalid_ref, next_e_ref, wslot_ref, x_ref, wg_hbm, wu_hbm, wd_hbm, y_ref,
                wg16, wu16, wd16, wg32, wu32, wd32, wsem):
    i = pl.program_id(0)
    e = blk_e_ref[i]
    used = nvalid_ref[i] > 0
    first_of_expert = used & ((i == 0) | (e != blk_e_ref[jnp.maximum(i - 1, 0)]))

    def fetch(expert, slot):
        return [pltpu.make_async_copy(src.at[expert], dst.at[slot], wsem.at[slot])
                for src, dst in ((wg_hbm, wg32), (wu_hbm, wu32), (wd_hbm, wd32))]

    for slot in range(2):
        @pl.when(first_of_expert & (wslot_ref[i] == slot))
        def _():
            @pl.when(i == 0)
            def _():
                for cp in fetch(e, slot):
                    cp.start()
            for cp in fetch(e, slot):
                cp.wait()
            wg16[...] = wg32[slot].astype(BF16)
            wu16[...] = wu32[slot].astype(BF16)
            wd16[...] = wd32[slot].astype(BF16)

            @pl.when(next_e_ref[i] < N_EXPERTS)
            def _():
                for cp in fetch(next_e_ref[i], 1 - slot):
                    cp.start()

    @pl.when(used)
    def _():
        xb = x_ref[...].astype(BF16)
        g = _dot(xb, wg16[...])
        u = _dot(xb, wu16[...])
        hmid = (_silu(g) * u).astype(BF16)
        y_ref[...] = _dot(hmid, wd16[...])

    @pl.when(nvalid_ref[i] == 0)
    def _():
        y_ref[...] = jnp.zeros_like(y_ref)


def _moe(blk_e, nvalid, next_e, wslot, x_sorted, w_gate, w_up, w_down):
    n_blocks = blk_e.shape[0]
    xmap = lambda i, be, nv, ne, ws: (jnp.where(nv[i] > 0, i, 0), 0)
    any_space = pl.BlockSpec(memory_space=pl.ANY)
    grid_spec = pltpu.PrefetchScalarGridSpec(
        num_scalar_prefetch=4,
        grid=(n_blocks,),
        in_specs=[pl.BlockSpec((MOE_BLOCK, D_MODEL), xmap), any_space, any_space, any_space],
        out_specs=pl.BlockSpec((MOE_BLOCK, D_MODEL), lambda i, be, nv, ne, ws: (i, 0)),
        scratch_shapes=[pltpu.VMEM((D_MODEL, D_EXPERT), BF16),
                        pltpu.VMEM((D_MODEL, D_EXPERT), BF16),
                        pltpu.VMEM((D_EXPERT, D_MODEL), BF16),
                        pltpu.VMEM((2, D_MODEL, D_EXPERT), F32),
                        pltpu.VMEM((2, D_MODEL, D_EXPERT), F32),
                        pltpu.VMEM((2, D_EXPERT, D_MODEL), F32),
                        pltpu.SemaphoreType.DMA((2,))])
    return pl.pallas_call(
        _moe_kernel,
        grid_spec=grid_spec,
        out_shape=jax.ShapeDtypeStruct((n_blocks * MOE_BLOCK, D_MODEL), F32),
        compiler_params=_cparams(("arbitrary",)),
        name="moe",
    )(blk_e, nvalid, next_e, wslot, x_sorted, w_gate, w_up, w_down)


def _combine_kernel(dcur_ref, dnext_ref, route_ref, x1_ref, mod_ref, g_ref, y_hbm, o_ref, ybuf, sems):
    i = pl.program_id(0)
    last = pl.num_programs(0) - 1
    tm = x1_ref.shape[0]

    def gather(dest_ref, buf):
        for t in range(tm):
            for slot in range(2):
                pltpu.make_async_copy(y_hbm.at[pl.ds(dest_ref[0, 0, 2 * t + slot], 1), :],
                                      ybuf.at[buf, slot, pl.ds(t, 1), :], sems.at[buf]).start(priority=slot)

    @pl.when(i == 0)
    def _():
        gather(dcur_ref, 0)

    for buf in range(2):
        @pl.when(i % 2 == buf)
        def _():
            @pl.when(i < last)
            def _():
                gather(dnext_ref, 1 - buf)

            for slot in range(2):
                pltpu.make_async_copy(y_hbm.at[pl.ds(0, tm), :], ybuf.at[buf, slot], sems.at[buf]).wait()
            route = route_ref[...]
            moe = ybuf[buf, 0] * route[:, 2:3] + ybuf[buf, 1] * route[:, 3:4]
            o_ref[...] = x1_ref[...] + _per_seq(_rms(moe, g_ref[...]), mod_ref[...], lambda y, m: y * m[:, 5:6])


def _combine(dest, route, y_sorted, x1, mod8, g_post_ffn, tm, seq_rows, mod_row0):
    n = x1.shape[0]
    steps = n // tm
    row = lambda i: (i, 0)
    dest3 = dest.reshape(steps, 1, 2 * tm)
    return pl.pallas_call(
        _combine_kernel,
        grid=(steps,),
        in_specs=[pl.BlockSpec((1, 1, 2 * tm), lambda i: (i, 0, 0), memory_space=pltpu.SMEM),
                  pl.BlockSpec((1, 1, 2 * tm), lambda i: (jnp.minimum(i + 1, steps - 1), 0, 0),
                               memory_space=pltpu.SMEM),
                  pl.BlockSpec((tm, LANES), row),
                  pl.BlockSpec((tm, D_MODEL), row),
                  _mod_spec(tm, seq_rows, mod_row0),
                  pl.BlockSpec((1, D_MODEL), lambda i: (0, 0)),
                  pl.BlockSpec(memory_space=pl.ANY)],
        out_specs=pl.BlockSpec((tm, D_MODEL), row),
        out_shape=jax.ShapeDtypeStruct((n, D_MODEL), F32),
        scratch_shapes=[pltpu.VMEM((2, 2, tm, D_MODEL), F32), pltpu.SemaphoreType.DMA((2,))],
        compiler_params=_cparams(("arbitrary",)),
        name="combine",
    )(dest3, dest3, route, x1, mod8, g_post_ffn, y_sorted)


def _segment_plan(counts_f, n_blocks):
    counts = counts_f[0, :N_EXPERTS].astype(jnp.int32)
    padded = (counts + MOE_BLOCK - 1) // MOE_BLOCK * MOE_BLOCK
    pad_end = jnp.cumsum(padded)
    pad_start = pad_end - padded
    blk_start = jnp.arange(n_blocks, dtype=jnp.int32) * MOE_BLOCK
    blk_e = jnp.minimum(jnp.sum((pad_end[None, :] <= blk_start[:, None]).astype(jnp.int32), axis=1),
                        N_EXPERTS - 1)
    onehot = blk_e[:, None] == jnp.arange(N_EXPERTS, dtype=jnp.int32)[None, :]
    c_blk = jnp.sum(jnp.where(onehot, counts[None, :], 0), axis=1)
    s_blk = jnp.sum(jnp.where(onehot, pad_start[None, :], 0), axis=1)
    nvalid = jnp.clip(c_blk - (blk_start - s_blk), 0, MOE_BLOCK).astype(jnp.int32)
    seg = jnp.concatenate([counts, pad_end]).astype(jnp.int32)
    ids = jnp.arange(N_EXPERTS, dtype=jnp.int32)
    later = (ids[None, :] > ids[:, None]) & (counts[None, :] > 0)
    next_e = jnp.min(jnp.where(later, ids[None, :], N_EXPERTS), axis=1)
    order = jnp.cumsum((counts > 0).astype(jnp.int32)) - 1
    pick = lambda table: jnp.sum(jnp.where(onehot, table[None, :], 0), axis=1).astype(jnp.int32)
    return blk_e.astype(jnp.int32), nvalid, pick(next_e), pick(order % 2), seg, pad_start


def _token_rows(route, pad_start):
    eid = route[:, 0:2].astype(jnp.int32)
    rank = route[:, 4:6].astype(jnp.int32)
    onehot = eid[:, :, None] == jnp.arange(N_EXPERTS, dtype=jnp.int32)[None, None, :]
    return rank + jnp.sum(jnp.where(onehot, pad_start[None, None, :], 0), axis=2)


def _layer(x_p, x_s, c_p, c_s, k_past, v_past, s0_s, conv_s, p):
    bp, tp, d = x_p.shape
    bs, ts, _ = x_s.shape
    n_p, n_s = bp * tp, bs * ts
    n_tok = n_p + n_s

    n_seq = bp + bs
    c_all = jnp.zeros((16, d), F32).at[:n_seq].set(jnp.concatenate([c_s, c_p], axis=0))
    mod = _ada(c_all, p['w_ada'], p['b_ada'])
    mod8 = jnp.pad(mod.reshape(16, 6, d), ((0, 0), (0, 2), (0, 0)))

    w_in = p['w_in']
    o_b = CONV_CH + DN_WIDTH
    o_q = o_b + 2 * DN_HEADS
    wb = jnp.pad(w_in[:, o_b:o_q], ((0, 0), (0, LANES - 2 * DN_HEADS)))
    wb_hi = wb.astype(BF16)
    wb_lo = (wb - wb_hi.astype(F32)).astype(BF16)
    w_main = jnp.concatenate([w_in[:, :o_b].astype(BF16), w_in[:, o_q:].astype(BF16), wb_hi, wb_lo], axis=1)
    g_pre_mix = p['g_pre_mix'].reshape(1, d)

    conv_w8 = jnp.pad(p['conv_w'], ((0, 8 - CONV_W), (0, 0)))
    pad_g = lambda a: jnp.pad(a.reshape(1, DN_HEADS), ((0, 0), (DN_HEADS, LANES - 2 * DN_HEADS)))
    alog_row, dtb_row = pad_g(p['a_log']), pad_g(p['dt_bias'])
    onorm_a = p['onorm_a'].reshape(1, DN_DIM)
    onorm_b2 = jnp.tile(p['onorm_b'].reshape(1, SB_DIM), (1, 2))

    w_out16 = p['w_out'].astype(BF16)
    wr = jnp.pad(jnp.concatenate([p['w_router_group'], p['w_router_expert']], axis=1),
                 ((0, 0), (0, LANES - N_GROUPS - N_EXPERTS)))
    wr_hi = wr.astype(BF16)
    wr_cat = jnp.concatenate([wr_hi, (wr - wr_hi.astype(F32)).astype(BF16)], axis=1)
    b_r = jnp.pad(jnp.concatenate([p['b_router_group'], p['b_router_expert']]).reshape(1, -1),
                  ((0, 0), (0, LANES - N_GROUPS - N_EXPERTS)))
    g_post_mix = p['g_post_mix'].reshape(1, d)
    g_pre_ffn = p['g_pre_ffn'].reshape(1, d)
    g_post_ffn = p['g_post_ffn'].reshape(1, d)

    def mixer(x, tm, tm_post, seq_rows, mod_row0, hist8, s0, k_old, v_old, chunk, n_chunks, bq, n_sub, cnt0):
        b, t, _ = x.shape
        x2d = x.reshape(b * t, d)
        a_in, z, bg, q16, kb, vb, k16, v16 = _proj(x2d, mod8, g_pre_mix, w_main, wb_hi, tm, seq_rows, mod_row0)
        r3 = lambda a: a.reshape(b, t, a.shape[-1])
        oa16, s_new = _delta(r3(a_in), r3(z), r3(bg), hist8, s0, conv_w8, alog_row, dtb_row, onorm_a,
                             chunk, n_chunks)
        k16, v16 = r3(k16), r3(v16)
        if k_old is not None:
            k16 = jnp.concatenate([k_old.reshape(b, -1, SB_WIDTH).astype(BF16), k16], axis=1)
            v16 = jnp.concatenate([v_old.reshape(b, -1, SB_WIDTH).astype(BF16), v16], axis=1)
        n_pad = (-k16.shape[1]) % KEY_TILE
        k16 = jnp.pad(k16, ((0, 0), (n_pad, 0), (0, 0)))
        v16 = jnp.pad(v16, ((0, 0), (n_pad, 0), (0, 0)))
        ob16 = _sb_attn(r3(q16), k16, v16, onorm_b2, bq, n_sub, n_pad)
        x1, h2, route, cnt = _post(oa16.reshape(b * t, DN_WIDTH), ob16.reshape(b * t, SB_WIDTH), x2d, mod8,
                                   g_post_mix, g_pre_ffn, w_out16, wr_cat, b_r, cnt0, tm_post, seq_rows,
                                   mod_row0)
        new_conv = r3(a_in)[:, t - (CONV_W - 1):, :]
        return (x1, h2, route, cnt, kb.reshape(b, t, SB_HEADS, SB_DIM), vb.reshape(b, t, SB_HEADS, SB_DIM),
                s_new, new_conv)

    zero_hist = jnp.zeros((bp, 8, CONV_CH), F32)
    zero_s = jnp.zeros((bp, DN_HEADS, DN_DIM, DN_DIM), F32)
    hist_s = jnp.pad(conv_s, ((0, 0), (8 - (CONV_W - 1), 0), (0, 0)))
    tm_p = min(ROW_TILE, tp)
    tm_dense = DENSE_TILE if n_p % DENSE_TILE == 0 else tm_p
    tm_router = ROUTER_TILE if n_p % ROUTER_TILE == 0 else tm_dense
    nc_p = max(1, min(8, tp // DELTA_BLOCK))
    x1p, h2p, rp, cnt_p, kp, vp, sp, cp = mixer(x_p, tm_dense, tm_router, tp, bs, zero_hist, zero_s, None, None,
                                                 min(DELTA_BLOCK, tp), nc_p, min(KEY_TILE // 2, tp), 2,
                                                 jnp.zeros((1, LANES), F32))
    x1s, h2s, rs, cnt, ks, vs, ss, cs = mixer(x_s, n_s, n_s, ts, 0, hist_s, s0_s, k_past, v_past,
                                               min(DELTA_BLOCK, ts), max(1, ts // DELTA_BLOCK),
                                               min(KEY_TILE, ts), 1, cnt_p)

    n_blocks = -(-2 * n_tok // MOE_BLOCK) + N_EXPERTS
    blk_e, nvalid, next_e, wslot, seg, pad_start = _segment_plan(cnt, n_blocks)
    dest_p = _token_rows(rp, pad_start)
    dest_s = _token_rows(rs, pad_start)
    x_sorted = _dispatch(seg, dest_p, dest_s, h2p, h2s, n_blocks, tm_p)
    y_sorted = _moe(blk_e, nvalid, next_e, wslot, x_sorted, p['w_gate'], p['w_up'], p['w_down'])
    y_p = _combine(dest_p, rp, y_sorted, x1p, mod8, g_post_ffn, tm_p, tp, bs).reshape(bp, tp, d)
    y_s = _combine(dest_s, rs, y_sorted, x1s, mod8, g_post_ffn, n_s, ts, 0).reshape(bs, ts, d)
    return y_p, y_s, kp, vp, sp, cp, ks, vs, ss, cs


def kernel(x_prompt, x_sample, c_prompt, c_sample, cache_k, cache_v, state_delta, state_conv, w_ada, b_ada, g_pre_mix, g_post_mix, g_pre_ffn, g_post_ffn, w_in, conv_w, a_log, dt_bias, onorm_a, onorm_b, w_out, w_router_group, b_router_group, w_router_expert, b_router_expert, w_gate, w_up, w_down):
    depth = w_in.shape[0]
    y_p, y_s = x_prompt, x_sample
    outs = [[] for _ in range(8)]
    for l in range(depth):
        p = dict(w_ada=w_ada[l], b_ada=b_ada[l], g_pre_mix=g_pre_mix[l], g_post_mix=g_post_mix[l],
                 g_pre_ffn=g_pre_ffn[l], g_post_ffn=g_post_ffn[l], w_in=w_in[l], conv_w=conv_w[l],
                 a_log=a_log[l], dt_bias=dt_bias[l], onorm_a=onorm_a[l], onorm_b=onorm_b[l],
                 w_out=w_out[l], w_router_group=w_router_group[l], b_router_group=b_router_group[l],
                 w_router_expert=w_router_expert[l], b_router_expert=b_router_expert[l],
                 w_gate=w_gate[l], w_up=w_up[l], w_down=w_down[l])
        res = _layer(y_p, y_s, c_prompt, c_sample, cache_k[l], cache_v[l], state_delta[l], state_conv[l], p)
        y_p, y_s = res[0], res[1]
        for lst, r in zip(outs, res[2:]):
            lst.append(r)
    return (y_p, y_s) + tuple(jnp.stack(o) for o in outs)
```

```python
import functools

import jax
import jax.numpy as jnp
from jax import lax
from jax.experimental import pallas as pl
from jax.experimental.pallas import tpu as pltpu

F32 = jnp.float32
BF16 = jnp.bfloat16

D_MODEL = 1024
DN_HEADS = 4
DN_DIM = 128
CONV_W = 4
DN_WIDTH = DN_HEADS * DN_DIM
CONV_CH = 3 * DN_WIDTH
DELTA_BLOCK = 64
SB_HEADS = 8
SB_DIM = 64
SB_WIDTH = SB_HEADS * SB_DIM
N_GROUPS = 4
EXPERTS_PER_GROUP = 8
N_EXPERTS = N_GROUPS * EXPERTS_PER_GROUP
D_EXPERT = D_MODEL // 2
MOE_BLOCK = 256
EPS = 1e-6

LANES = 128
KEY_TILE = 128
ATTN_WINDOW = 3 * KEY_TILE
WINDOW_BLOCKS = 4
EXP_ZERO_BELOW = -104.0
VMEM_LIMIT = 56 * 1024 * 1024
DENSE_TILE = 512
ROUTER_TILE = 1024
ROW_TILE = 512
RANK_SUBTILE = 256


def _cparams(sem):
    return pltpu.CompilerParams(dimension_semantics=sem, vmem_limit_bytes=VMEM_LIMIT)


def _split(a):
    hi = a.astype(BF16)
    lo = (a - hi.astype(F32)).astype(BF16)
    return hi, lo


def _dot(a, b, dims=(((1,), (0,)), ((), ()))):
    return lax.dot_general(a, b, dims, preferred_element_type=F32)


def _dot_small_int_lhs(a, b):
    a16 = a.astype(BF16)
    return _dot(jnp.concatenate([a16, a16], axis=1), jnp.concatenate(_split(b), axis=0))


_NT = (((1,), (1,)), ((), ()))
_TN = (((0,), (0,)), ((), ()))


def _silu(x):
    return x * jax.nn.sigmoid(x)


SOFTPLUS_LINEAR_ABOVE = 80.0


def _softplus(x):
    return jnp.where(x > SOFTPLUS_LINEAR_ABOVE, x, jnp.log(1.0 + jnp.exp(jnp.minimum(x, SOFTPLUS_LINEAR_ABOVE))))


def _ada_kernel(c_ref, w_ref, b_ref, o_ref):
    s = _silu(c_ref[...]).astype(BF16)
    o_ref[...] = _dot(s, w_ref[...].astype(BF16)) + b_ref[...]


def _ada(c_all, w_ada, b_ada):
    rows = c_all.shape[0]
    n = w_ada.shape[1]
    tn = 1024
    return pl.pallas_call(
        _ada_kernel,
        grid=(n // tn,),
        in_specs=[pl.BlockSpec((rows, D_MODEL), lambda j: (0, 0)),
                  pl.BlockSpec((D_MODEL, tn), lambda j: (0, j)),
                  pl.BlockSpec((1, tn), lambda j: (0, j))],
        out_specs=pl.BlockSpec((rows, tn), lambda j: (0, j)),
        out_shape=jax.ShapeDtypeStruct((rows, n), F32),
        compiler_params=_cparams(("arbitrary",)),
        name="ada",
    )(c_all, w_ada, b_ada.reshape(1, n))


def _rms(x, gain):
    return x * lax.rsqrt(jnp.mean(x * x, axis=-1, keepdims=True) + EPS) * gain


def _per_seq(y, mod, fn):
    n_seq = mod.shape[0]
    ys = y.reshape(n_seq, y.shape[0] // n_seq, y.shape[1])
    return fn(ys, mod).reshape(y.shape)


def _mod_spec(tm, seq_rows, mod_row0):
    if tm >= seq_rows:
        n_seq = tm // seq_rows
        assert tm % seq_rows == 0 and mod_row0 % n_seq == 0
        return pl.BlockSpec((n_seq, 8, D_MODEL), lambda i: (mod_row0 // n_seq + i, 0, 0))
    assert seq_rows % tm == 0
    return pl.BlockSpec((1, 8, D_MODEL), lambda i: (mod_row0 + (i * tm) // seq_rows, 0, 0))


def _proj_kernel(x_ref, mod_ref, g_ref, wm_ref, wbh_ref,
                 a_ref, z_ref, bg_ref, q_ref, k_ref, v_ref, k16_ref, v16_ref):
    h = _per_seq(_rms(x_ref[...], g_ref[...]), mod_ref[...], lambda y, m: y * (1.0 + m[:, 1:2]) + m[:, 0:1])
    hh, hl = _split(h)
    p = _dot(hh, wm_ref[...])
    a_ref[...] = p[:, 0:CONV_CH]
    z_ref[...] = p[:, CONV_CH:CONV_CH + DN_WIDTH]
    o = CONV_CH + DN_WIDTH
    q_ref[...] = (p[:, o:o + SB_WIDTH] * (SB_DIM ** -0.5)).astype(BF16)
    k = p[:, o + SB_WIDTH:o + 2 * SB_WIDTH]
    v = p[:, o + 2 * SB_WIDTH:o + 3 * SB_WIDTH]
    k_ref[...] = k
    v_ref[...] = v
    k16_ref[...] = k.astype(BF16)
    v16_ref[...] = v.astype(BF16)
    o += 3 * SB_WIDTH
    bg_ref[...] = p[:, o:o + LANES] + (_dot(hl, wbh_ref[...]) + p[:, o + LANES:])


def _proj(x2d, mod8, g_pre, w_main, wb_hi, tm, seq_rows, mod_row0):
    n = x2d.shape[0]
    nm = w_main.shape[1]
    row = lambda i: (i, 0)
    const = lambda i: (0, 0)
    outs = [(CONV_CH, F32), (DN_WIDTH, F32), (LANES, F32), (SB_WIDTH, BF16), (SB_WIDTH, F32), (SB_WIDTH, F32),
            (SB_WIDTH, BF16), (SB_WIDTH, BF16)]
    return pl.pallas_call(
        _proj_kernel,
        grid=(n // tm,),
        in_specs=[pl.BlockSpec((tm, D_MODEL), row),
                  _mod_spec(tm, seq_rows, mod_row0),
                  pl.BlockSpec((1, D_MODEL), const),
                  pl.BlockSpec((D_MODEL, nm), const),
                  pl.BlockSpec((D_MODEL, LANES), const)],
        out_specs=[pl.BlockSpec((tm, w), row) for w, _ in outs],
        out_shape=[jax.ShapeDtypeStruct((n, w), dt) for w, dt in outs],
        compiler_params=_cparams(("arbitrary",)),
        name="proj",
    )(x2d, mod8, g_pre, w_main, wb_hi)


def _delta_kernel(a_ref, z_ref, bg_ref, hist0_ref, s0_ref, cw_ref, alog_ref, dtb_ref, on_ref,
                  o_ref, sfin_ref, hist_sc, s_sc, *, chunk, n_chunks):
    t_idx = pl.program_id(1)
    tt = chunk * n_chunks

    @pl.when(t_idx == 0)
    def _():
        hist_sc[...] = hist0_ref[0]
        s_sc[...] = s0_ref[0]

    x = a_ref[0]
    xx = jnp.concatenate([hist_sc[...], x], axis=0)
    cw = cw_ref[...]
    conv = x * cw[CONV_W - 1:CONV_W]
    for s in range(1, CONV_W):
        conv = conv + pltpu.roll(xx, s, 0)[8:] * cw[CONV_W - 1 - s:CONV_W - s]
    conv = _silu(conv)
    hist_sc[...] = x[tt - 8:tt]

    bg = bg_ref[0]
    lane = lax.broadcasted_iota(jnp.int32, (1, LANES), 1)
    g_lane = (lane >= DN_HEADS) & (lane < 2 * DN_HEADS)
    neg_a = jnp.where(g_lane, -jnp.exp(alog_ref[...]), 0.0)
    beta_all = jax.nn.sigmoid(bg)
    g_all = neg_a * _softplus(bg + dtb_ref[...])

    ri = lax.broadcasted_iota(jnp.int32, (chunk, chunk), 0)
    ci = lax.broadcasted_iota(jnp.int32, (chunk, chunk), 1)
    incl = ri >= ci
    strict = ri > ci
    tri = incl.astype(F32)
    eye = (ri == ci).astype(F32)
    ones_cc = jnp.ones((chunk, chunk), F32)
    onorm = on_ref[...]

    rows = lambda c: slice(c * chunk, (c + 1) * chunk)
    hs = range(DN_HEADS)
    prep = {}

    def prepare(chunk_ids):
        units = [(c, h) for c in chunk_ids for h in hs]
        gc_all = {c: _dot_small_int_lhs(tri, g_all[rows(c)]) for c in chunk_ids}
        yield
        diag = {c: jnp.concatenate([eye * gc_all[c][:, DN_HEADS + h:DN_HEADS + h + 1] for h in hs], axis=1)
                for c in chunk_ids}
        gc_rows = {c: _dot_small_int_lhs(ones_cc, diag[c]) for c in chunk_ids}
        yield
        q, k, v, beta, gcol, glast, eg, decay, kbeta = {}, {}, {}, {}, {}, {}, {}, {}, {}
        for (c, h) in units:
            lo = h * DN_DIM
            qq = conv[rows(c), lo:lo + DN_DIM]
            kk = conv[rows(c), DN_WIDTH + lo:DN_WIDTH + lo + DN_DIM]
            u_ = (c, h)
            q[u_] = qq * lax.rsqrt(jnp.sum(qq * qq, axis=-1, keepdims=True) + EPS) * (DN_DIM ** -0.5)
            k[u_] = kk * lax.rsqrt(jnp.sum(kk * kk, axis=-1, keepdims=True) + EPS)
            v[u_] = conv[rows(c), 2 * DN_WIDTH + lo:2 * DN_WIDTH + lo + DN_DIM]
            beta[u_] = beta_all[rows(c), h:h + 1]
            gcol[u_] = gc_all[c][:, DN_HEADS + h:DN_HEADS + h + 1]
            glast[u_] = gcol[u_][chunk - 1:chunk, :]
            eg[u_] = jnp.exp(gcol[u_])
            grow = gc_rows[c][:, h * chunk:(h + 1) * chunk]
            decay[u_] = jnp.where(incl, jnp.exp(jnp.minimum(gcol[u_] - grow, 0.0)), 0.0)
            kbeta[u_] = k[u_] * beta[u_]

        k16 = {u_: k[u_].astype(BF16) for u_ in units}
        kq = {u_: _dot(jnp.concatenate([kbeta[u_], q[u_]], axis=0).astype(BF16), k16[u_], _NT) for u_ in units}
        yield
        lower = {u_: jnp.where(strict, kq[u_][:chunk] * decay[u_], 0.0) for u_ in units}
        attn16 = {u_: jnp.where(incl, kq[u_][chunk:] * decay[u_], 0.0).astype(BF16) for u_ in units}

        sol = {u_: jnp.concatenate([v[u_] * beta[u_], kbeta[u_] * eg[u_]], axis=1) for u_ in units}
        lp = lower
        p = 1
        while p < chunk:
            lsp = {u_: _split(lp[u_]) for u_ in units}
            ssp = {u_: _split(sol[u_]) for u_ in units}
            lcat = {u_: jnp.concatenate([lsp[u_][0], lsp[u_][1], lsp[u_][0]], axis=1) for u_ in units}
            upd = {u_: _dot(lcat[u_], jnp.concatenate([ssp[u_][0], ssp[u_][0], ssp[u_][1]], axis=0))
                   for u_ in units}
            if 2 * p < chunk:
                lp = {u_: _dot(lcat[u_], jnp.concatenate([lsp[u_][0], lsp[u_][0], lsp[u_][1]], axis=0))
                      for u_ in units}
            yield
            sol = {u_: (sol[u_] - upd[u_]) if p == 1 else (sol[u_] + upd[u_]) for u_ in units}
            p *= 2

        for u_ in units:
            prep[u_] = dict(
                u=sol[u_][:, :DN_DIM],
                wq=jnp.concatenate([sol[u_][:, DN_DIM:], q[u_] * eg[u_]], axis=0).astype(BF16),
                kd=(k[u_] * jnp.exp(glast[u_] - gcol[u_])).astype(BF16),
                attn=attn16[u_], decay_all=jnp.exp(glast[u_]))

    def recur(chunk_ids):
        for c in chunk_ids:
            s_old = [s_sc[h] for h in hs]
            ws = [_dot(prep[(c, h)]['wq'], s_old[h].astype(BF16)) for h in hs]
            yield
            v_new = [(prep[(c, h)]['u'] - ws[h][:chunk]).astype(BF16) for h in hs]
            o_in = [_dot(prep[(c, h)]['attn'], v_new[h]) for h in hs]
            ds = [_dot(prep[(c, h)]['kd'], v_new[h], _TN) for h in hs]
            yield
            for h in hs:
                lo = h * DN_DIM
                s_sc[h] = s_old[h] * prep[(c, h)]['decay_all'] + ds[h]
                o = ws[h][chunk:] + o_in[h]
                zg = z_ref[0, rows(c), lo:lo + DN_DIM]
                o_ref[0, rows(c), lo:lo + DN_DIM] = (_rms(o, onorm) * _silu(zg)).astype(BF16)

    def run(*streams):
        live = list(streams)
        while live:
            for g_ in list(live):
                if next(g_, done_) is done_:
                    live.remove(g_)

    done_ = object()
    ids = list(range(n_chunks))
    first, second = (ids[:n_chunks // 2], ids[n_chunks // 2:]) if n_chunks >= 2 else (ids, [])
    run(prepare(first))
    run(recur(first), prepare(second))
    run(recur(second))

    @pl.when(t_idx == pl.num_programs(1) - 1)
    def _():
        sfin_ref[0] = s_sc[...]


def _delta(a_in, z, bg, hist8, s0, conv_w8, alog_row, dtb_row, onorm_a, chunk, n_chunks):
    b, t, _ = a_in.shape
    tt = chunk * n_chunks
    tile = lambda bi, ti: (bi, ti, 0)
    per_b3 = lambda bi, ti: (bi, 0, 0)
    per_b4 = lambda bi, ti: (bi, 0, 0, 0)
    const = lambda bi, ti: (0, 0)
    kern = functools.partial(_delta_kernel, chunk=chunk, n_chunks=n_chunks)
    return pl.pallas_call(
        kern,
        grid=(b, t // tt),
        in_specs=[pl.BlockSpec((1, tt, CONV_CH), tile),
                  pl.BlockSpec((1, tt, DN_WIDTH), tile),
                  pl.BlockSpec((1, tt, LANES), tile),
                  pl.BlockSpec((1, 8, CONV_CH), per_b3),
                  pl.BlockSpec((1, DN_HEADS, DN_DIM, DN_DIM), per_b4),
                  pl.BlockSpec((8, CONV_CH), const),
                  pl.BlockSpec((1, LANES), const),
                  pl.BlockSpec((1, LANES), const),
                  pl.BlockSpec((1, DN_DIM), const)],
        out_specs=[pl.BlockSpec((1, tt, DN_WIDTH), tile),
                   pl.BlockSpec((1, DN_HEADS, DN_DIM, DN_DIM), per_b4)],
        out_shape=[jax.ShapeDtypeStruct((b, t, DN_WIDTH), BF16),
                   jax.ShapeDtypeStruct((b, DN_HEADS, DN_DIM, DN_DIM), F32)],
        scratch_shapes=[pltpu.VMEM((8, CONV_CH), F32),
                        pltpu.VMEM((DN_HEADS, DN_DIM, DN_DIM), F32)],
        compiler_params=_cparams(("arbitrary", "arbitrary")),
        name="delta",
    )(a_in, z, bg, hist8, s0, conv_w8, alog_row, dtb_row, onorm_a)


def _sb_kernel(q_ref, *refs, bq, n_sub, n_pad, q_off):
    kwin = refs[:WINDOW_BLOCKS]
    vwin = refs[WINDOW_BLOCKS:2 * WINDOW_BLOCKS]
    k_hbm, v_hbm, on_ref, o_ref, kbuf, vbuf, qsel, acc, carry, done_ref, sem = refs[2 * WINDOW_BLOCKS:]
    b = pl.program_id(0)
    i = pl.program_id(1)
    g = bq * n_sub
    qend_step = q_off + (i + 1) * g
    win_start = qend_step - WINDOW_BLOCKS * KEY_TILE
    n_pairs = SB_HEADS // 2
    heads = range(SB_HEADS)

    half_lane = lax.broadcasted_iota(jnp.int32, (bq, LANES), 1) < SB_DIM
    rj = lax.broadcasted_iota(jnp.int32, (2 * KEY_TILE, 2 * KEY_TILE), 0) % KEY_TILE
    cj = lax.broadcasted_iota(jnp.int32, (2 * KEY_TILE, 2 * KEY_TILE), 1)
    suffix2 = ((rj > cj) | (cj >= KEY_TILE)).astype(BF16)

    def window(blocks, off, p):
        parts = []
        for blk in range(WINDOW_BLOCKS):
            lo, hi = max(off, blk * KEY_TILE), min(off + ATTN_WINDOW, (blk + 1) * KEY_TILE)
            if lo < hi:
                ref = blocks[WINDOW_BLOCKS - 1 - blk]
                parts.append(ref[0, lo - blk * KEY_TILE:hi - blk * KEY_TILE, p * LANES:(p + 1) * LANES])
        return jnp.concatenate(parts, axis=0)

    n_t = ATTN_WINDOW // KEY_TILE
    row = lax.broadcasted_iota(jnp.int32, (bq, ATTN_WINDOW), 0)
    col = lax.broadcasted_iota(jnp.int32, (bq, ATTN_WINDOW), 1)
    causal = col < row + (ATTN_WINDOW - bq)
    subs = range(n_sub)
    offs = [WINDOW_BLOCKS * KEY_TILE - g + (s + 1) * bq - ATTN_WINDOW for s in subs]
    old_cols = ATTN_WINDOW - KEY_TILE

    def sweep_window(all_valid):
        if all_valid:
            newest = causal[:, old_cols:]

            def keep(s, x):
                return jnp.concatenate([x[:, :old_cols], jnp.where(newest, x[:, old_cols:], 0.0)], axis=1)
        else:
            mask = [causal & (col >= n_pad - (win_start + offs[s])) for s in subs]

            def keep(s, x):
                return jnp.where(mask[s], x, 0.0)

        zs, sp, cs_all = {}, {}, {}
        for s in subs:
            for p in range(n_pairs):
                qf = q_ref[0, s * bq:(s + 1) * bq, p * LANES:(p + 1) * LANES].astype(F32)
                qq = jnp.concatenate([jnp.where(half_lane, qf, 0.0), jnp.where(half_lane, 0.0, qf)], axis=0)
                zz = _dot(qq.astype(BF16), window(kwin, offs[s], p), _NT)
                zs[(s, 2 * p)], zs[(s, 2 * p + 1)] = zz[:bq], zz[bq:]
        for s in subs:
            pieces = []
            for h in heads:
                sp[(s, h)] = _softplus(zs[(s, h)])
                hi, lo = _split(keep(s, -sp[(s, h)]))
                for j in range(n_t):
                    c0 = ATTN_WINDOW - (j + 1) * KEY_TILE
                    pieces.append(jnp.concatenate([hi[:, c0:c0 + KEY_TILE], lo[:, c0:c0 + KEY_TILE]], axis=1))
            cs_all[s] = _dot(jnp.concatenate(pieces, axis=0), suffix2)
        for s in subs:
            worst = jnp.full((bq, KEY_TILE), -jnp.inf, F32)
            a = {}
            for h in heads:
                run = None
                cols = []
                for j in range(n_t):
                    r0 = (h * n_t + j) * bq
                    cs = cs_all[s][r0:r0 + bq]
                    cols.append(cs[:, :KEY_TILE] if run is None else cs[:, :KEY_TILE] + run)
                    run = cs[:, KEY_TILE:] if run is None else run + cs[:, KEY_TILE:]
                within = jnp.concatenate(cols[::-1], axis=1)
                a[h] = keep(s, jnp.exp((zs[(s, h)] - sp[(s, h)]) + within)).astype(BF16)
                carry[s, h] = run
                worst = jnp.maximum(worst, run)
            for p in range(n_pairs):
                pv = _dot(jnp.concatenate([a[2 * p], a[2 * p + 1]], axis=0), window(vwin, offs[s], p))
                acc[s, p] = jnp.where(half_lane, pv[:bq], pv[bq:])
            done_ref[s] = (jnp.max(worst) < EXP_ZERO_BELOW).astype(jnp.int32)

    window_is_all_keys = win_start + min(offs) >= n_pad

    @pl.when(window_is_all_keys)
    def _():
        sweep_window(True)

    @pl.when(jnp.logical_not(window_is_all_keys))
    def _():
        sweep_window(False)

    done = [done_ref[s] for s in subs]

    rj1 = lax.broadcasted_iota(jnp.int32, (KEY_TILE, 2 * KEY_TILE), 0)
    cj1 = lax.broadcasted_iota(jnp.int32, (KEY_TILE, 2 * KEY_TILE), 1)
    suffix1 = ((rj1 > cj1) | (cj1 >= KEY_TILE)).astype(BF16)
    col1 = lax.broadcasted_iota(jnp.int32, (bq, KEY_TILE), 1)
    for s in subs:
        swept_from = win_start + offs[s]

        @pl.when((done[s] == 0) & (swept_from > n_pad))
        def _():
            for p in range(n_pairs):
                qf = q_ref[0, s * bq:(s + 1) * bq, p * LANES:(p + 1) * LANES].astype(F32)
                qsel[2 * p] = jnp.where(half_lane, qf, 0.0).astype(BF16)
                qsel[2 * p + 1] = jnp.where(half_lane, 0.0, qf).astype(BF16)

            def body(state):
                upper, _ = state
                start = jnp.maximum(upper - KEY_TILE, 0)
                copies = []
                for p in range(n_pairs):
                    for src, dst in ((k_hbm, kbuf), (v_hbm, vbuf)):
                        cp = pltpu.make_async_copy(
                            src.at[b, pl.ds(pl.multiple_of(start, 16), KEY_TILE), pl.ds(p * LANES, LANES)],
                            dst.at[p], sem)
                        cp.start()
                        copies.append(cp)
                for cp in copies:
                    cp.wait()
                kpos = start + col1
                m1 = (kpos < upper) & (kpos >= n_pad)
                zz = [_dot(qsel[h], kbuf[h // 2], _NT) for h in heads]
                spp = [_softplus(zz[h]) for h in heads]
                pcs = []
                for h in heads:
                    pcs.extend(_split(jnp.where(m1, -spp[h], 0.0)))
                cs1 = _dot(jnp.concatenate(pcs, axis=0), suffix1)
                worst = jnp.full((bq, KEY_TILE), -jnp.inf, F32)
                aa = []
                for h in heads:
                    c = cs1[2 * h * bq:(2 * h + 1) * bq] + cs1[(2 * h + 1) * bq:(2 * h + 2) * bq]
                    c_old = carry[s, h]
                    aa.append(jnp.where(m1, jnp.exp((zz[h] - spp[h]) + c[:, :KEY_TILE] + c_old), 0.0).astype(BF16))
                    c_new = c_old + c[:, KEY_TILE:]
                    carry[s, h] = c_new
                    worst = jnp.maximum(worst, c_new)
                for p in range(n_pairs):
                    pv0 = _dot(aa[2 * p], vbuf[p])
                    pv1 = _dot(aa[2 * p + 1], vbuf[p])
                    acc[s, p] = acc[s, p] + jnp.where(half_lane, pv0, pv1)
                return start, (jnp.max(worst) < EXP_ZERO_BELOW).astype(jnp.int32)

            lax.while_loop(lambda st: (st[0] > n_pad) & (st[1] == 0), body, (swept_from, jnp.int32(0)))

    onb = on_ref[...]
    for s in subs:
        for p in range(n_pairs):
            o = acc[s, p]
            sq = o * o
            s_lo = jnp.sum(jnp.where(half_lane, sq, 0.0), axis=-1, keepdims=True)
            s_hi = jnp.sum(jnp.where(half_lane, 0.0, sq), axis=-1, keepdims=True)
            ms = jnp.where(half_lane, s_lo, s_hi) * (1.0 / SB_DIM)
            o_ref[0, s * bq:(s + 1) * bq, p * LANES:(p + 1) * LANES] = (o * lax.rsqrt(ms + EPS) * onb).astype(BF16)


def _sb_attn(q16, k16p, v16p, onorm_b2, bq, n_sub, n_pad):
    b, tq, _ = q16.shape
    tkp = k16p.shape[1]
    q_off = tkp - tq
    g = bq * n_sub
    assert tq % g == 0 and g <= KEY_TILE and bq % 16 == 0
    assert all((q_off + (i + 1) * g) % KEY_TILE == 0 for i in range(tq // g))

    def kmap(back):
        def f(bi, i):
            last = (q_off + (i + 1) * g) // KEY_TILE - 1
            return (bi, jnp.maximum(last - back, 0), 0)
        return f

    qmap = lambda bi, i: (bi, i, 0)
    kern = functools.partial(_sb_kernel, bq=bq, n_sub=n_sub, n_pad=n_pad, q_off=q_off)
    kspec = [pl.BlockSpec((1, KEY_TILE, SB_WIDTH), kmap(back)) for back in range(WINDOW_BLOCKS)]
    return pl.pallas_call(
        kern,
        grid=(b, tq // g),
        in_specs=[pl.BlockSpec((1, g, SB_WIDTH), qmap)] + kspec + kspec
                 + [pl.BlockSpec(memory_space=pl.ANY), pl.BlockSpec(memory_space=pl.ANY),
                    pl.BlockSpec((1, LANES), lambda bi, i: (0, 0))],
        out_specs=pl.BlockSpec((1, g, SB_WIDTH), qmap),
        out_shape=jax.ShapeDtypeStruct((b, tq, SB_WIDTH), BF16),
        scratch_shapes=[pltpu.VMEM((SB_HEADS // 2, KEY_TILE, LANES), BF16),
                        pltpu.VMEM((SB_HEADS // 2, KEY_TILE, LANES), BF16),
                        pltpu.VMEM((SB_HEADS, bq, LANES), BF16),
                        pltpu.VMEM((n_sub, SB_HEADS // 2, bq, LANES), F32),
                        pltpu.VMEM((n_sub, SB_HEADS, bq, KEY_TILE), F32),
                        pltpu.SMEM((n_sub,), jnp.int32),
                        pltpu.SemaphoreType.DMA(())],
        compiler_params=_cparams(("arbitrary", "arbitrary")),
        name="sb_attn",
    )(q16, *([k16p] * WINDOW_BLOCKS), *([v16p] * WINDOW_BLOCKS), k16p, v16p, onorm_b2)


def _post_kernel(oa_ref, ob_ref, x_ref, mod_ref, gpm_ref, gpf_ref, wo_ref, wrc_ref, br_ref, cnt0_ref,
                 x1_ref, h2_ref, route_ref, cnt_ref):
    @pl.when(pl.program_id(0) == 0)
    def _():
        cnt_ref[...] = cnt0_ref[...]

    mod = mod_ref[...]
    mix = _dot(oa_ref[...], wo_ref[0:DN_WIDTH, :]) + _dot(ob_ref[...], wo_ref[DN_WIDTH:DN_WIDTH + SB_WIDTH, :])
    x1 = x_ref[...] + _per_seq(_rms(mix, gpm_ref[...]), mod, lambda y, m: y * m[:, 2:3])
    x1_ref[...] = x1
    h2 = _per_seq(_rms(x1, gpf_ref[...]), mod, lambda y, m: y * (1.0 + m[:, 4:5]) + m[:, 3:4])
    h2_ref[...] = h2
    hh, hl = _split(h2)
    hw = _dot(hh, wrc_ref[...])
    logits = hw[:, :LANES] + (_dot(hl, wrc_ref[:, :LANES]) + hw[:, LANES:]) + br_ref[...]
    lane = lax.broadcasted_iota(jnp.int32, logits.shape, 1).astype(F32)
    neg = -jnp.inf
    nl = float(LANES)
    lg = jnp.where(lane < N_GROUPS, logits, neg)
    gmax = jnp.max(lg, axis=-1, keepdims=True)
    grp = jnp.min(jnp.where(lg == gmax, lane, nl), axis=-1, keepdims=True)
    p_grp = 1.0 / jnp.sum(jnp.exp(lg - gmax), axis=-1, keepdims=True)
    first = N_GROUPS + grp * EXPERTS_PER_GROUP
    le = jnp.where((lane >= first) & (lane < first + EXPERTS_PER_GROUP), logits, neg)
    emax = jnp.max(le, axis=-1, keepdims=True)
    i1 = jnp.min(jnp.where(le == emax, lane, nl), axis=-1, keepdims=True)
    esum = jnp.sum(jnp.exp(le - emax), axis=-1, keepdims=True)
    le2 = jnp.where(lane == i1, neg, le)
    e2max = jnp.max(le2, axis=-1, keepdims=True)
    i2 = jnp.min(jnp.where(le2 == e2max, lane, nl), axis=-1, keepdims=True)
    p1 = 1.0 / esum
    p2 = jnp.exp(e2max - emax) / esum
    w1 = p_grp * p1 / (p1 + p2)
    w2 = p_grp * p2 / (p1 + p2)
    e1 = i1 - N_GROUPS
    e2 = i2 - N_GROUPS
    hot1 = (lane == e1).astype(F32)
    hot2 = (lane == e2).astype(F32)
    both = hot1 + hot2
    tm = logits.shape[0]
    sub = min(tm, RANK_SUBTILE)
    ti = lax.broadcasted_iota(jnp.int32, (sub, sub), 0)
    tj = lax.broadcasted_iota(jnp.int32, (sub, sub), 1)
    before = (ti > tj).astype(BF16)
    running = cnt_ref[...]
    earlier = []
    for r0 in range(0, tm, sub):
        part = both[r0:r0 + sub]
        earlier.append(_dot(before, part.astype(BF16)) + running)
        running = running + jnp.sum(part, axis=0, keepdims=True)
    earlier = jnp.concatenate(earlier, axis=0)
    rank1 = jnp.sum(hot1 * earlier, axis=-1, keepdims=True)
    rank2 = jnp.sum(hot2 * (earlier + hot1), axis=-1, keepdims=True)
    cnt_ref[...] = running
    out = jnp.where(lane == 0.0, e1, 0.0)
    out = jnp.where(lane == 1.0, e2, out)
    out = jnp.where(lane == 2.0, w1, out)
    out = jnp.where(lane == 3.0, w2, out)
    out = jnp.where(lane == 4.0, rank1, out)
    out = jnp.where(lane == 5.0, rank2, out)
    route_ref[...] = out


def _post(oa16, ob16, x2d, mod8, g_post_mix, g_pre_ffn, w_out16, wr_cat, b_r, cnt0, tm, seq_rows,
          mod_row0):
    n = x2d.shape[0]
    row = lambda i: (i, 0)
    const = lambda i: (0, 0)
    return pl.pallas_call(
        _post_kernel,
        grid=(n // tm,),
        in_specs=[pl.BlockSpec((tm, DN_WIDTH), row),
                  pl.BlockSpec((tm, SB_WIDTH), row),
                  pl.BlockSpec((tm, D_MODEL), row),
                  _mod_spec(tm, seq_rows, mod_row0),
                  pl.BlockSpec((1, D_MODEL), const),
                  pl.BlockSpec((1, D_MODEL), const),
                  pl.BlockSpec((D_MODEL, D_MODEL), const),
                  pl.BlockSpec((D_MODEL, 2 * LANES), const),
                  pl.BlockSpec((1, LANES), const),
                  pl.BlockSpec((1, LANES), const)],
        out_specs=[pl.BlockSpec((tm, D_MODEL), row),
                   pl.BlockSpec((tm, D_MODEL), row),
                   pl.BlockSpec((tm, LANES), row),
                   pl.BlockSpec((1, LANES), const)],
        out_shape=[jax.ShapeDtypeStruct((n, D_MODEL), F32),
                   jax.ShapeDtypeStruct((n, D_MODEL), F32),
                   jax.ShapeDtypeStruct((n, LANES), F32),
                   jax.ShapeDtypeStruct((1, LANES), F32)],
        compiler_params=_cparams(("arbitrary",)),
        name="post",
    )(oa16, ob16, x2d, mod8, g_post_mix, g_pre_ffn, w_out16, wr_cat, b_r, cnt0)


def _dispatch_kernel(seg_ref, dp_ref, ds_ref, hp_ref, hs_ref, xs_hbm, zbuf, stage, sem, stage_sems, *,
                     n_blocks, n_prompt_steps):
    i = pl.program_id(0)
    last_step = pl.num_programs(0) - 1

    @pl.when(i == 0)
    def _():
        zbuf[...] = jnp.zeros_like(zbuf)

        def zero_block(row0):
            return pltpu.make_async_copy(zbuf, xs_hbm.at[pl.ds(pl.multiple_of(row0, MOE_BLOCK), MOE_BLOCK), :], sem)

        for e in range(N_EXPERTS):
            @pl.when(seg_ref[e] > 0)
            def _():
                zero_block(seg_ref[N_EXPERTS + e] - MOE_BLOCK).start()
        for e in range(N_EXPERTS):
            @pl.when(seg_ref[e] > 0)
            def _():
                zero_block(seg_ref[N_EXPERTS + e] - MOE_BLOCK).wait()

        used = seg_ref[2 * N_EXPERTS - 1] // MOE_BLOCK

        def fill(b, c):
            cp = zero_block(b * MOE_BLOCK)
            cp.start()
            cp.wait()
            return c

        lax.fori_loop(used, n_blocks, fill, 0)

    def scatter(h_ref, dest_ref, sem_):
        for t in range(h_ref.shape[0]):
            for slot in range(2):
                pltpu.make_async_copy(h_ref.at[pl.ds(t, 1), :],
                                      xs_hbm.at[pl.ds(dest_ref[0, 0, 2 * t + slot], 1), :],
                                      sem_).start(priority=slot)

    def drain(h_ref, sem_):
        for slot in range(2):
            pltpu.make_async_copy(h_ref, xs_hbm.at[pl.ds(0, h_ref.shape[0]), :], sem_).wait()

    for buf in range(2):
        @pl.when((i < last_step) & (i % 2 == buf))
        def _():
            @pl.when(i >= 2)
            def _():
                drain(stage.at[buf], stage_sems.at[buf])
            stage[buf] = hp_ref[...]
            scatter(stage.at[buf], dp_ref, stage_sems.at[buf])

    @pl.when(i == last_step)
    def _():
        scatter(hs_ref, ds_ref, sem)
        drain(hs_ref, sem)
        for buf in range(min(2, n_prompt_steps)):
            drain(stage.at[buf], stage_sems.at[buf])


def _dispatch(seg, dest_p, dest_s, h2p, h2s, n_blocks, tm):
    n_p, n_s = h2p.shape[0], h2s.shape[0]
    steps_p = n_p // tm
    pmap3 = lambda i, sg: (jnp.minimum(i, steps_p - 1), 0, 0)
    pmap2 = lambda i, sg: (jnp.minimum(i, steps_p - 1), 0)
    grid_spec = pltpu.PrefetchScalarGridSpec(
        num_scalar_prefetch=1,
        grid=(steps_p + 1,),
        in_specs=[pl.BlockSpec((1, 1, 2 * tm), pmap3, memory_space=pltpu.SMEM),
                  pl.BlockSpec((1, 1, 2 * n_s), lambda i, sg: (0, 0, 0), memory_space=pltpu.SMEM),
                  pl.BlockSpec((tm, D_MODEL), pmap2),
                  pl.BlockSpec((n_s, D_MODEL), lambda i, sg: (0, 0))],
        out_specs=pl.BlockSpec(memory_space=pl.ANY),
        scratch_shapes=[pltpu.VMEM((MOE_BLOCK, D_MODEL), F32), pltpu.VMEM((2, tm, D_MODEL), F32),
                        pltpu.SemaphoreType.DMA(()), pltpu.SemaphoreType.DMA((2,))])
    return pl.pallas_call(
        functools.partial(_dispatch_kernel, n_blocks=n_blocks, n_prompt_steps=steps_p),
        grid_spec=grid_spec,
        out_shape=jax.ShapeDtypeStruct((n_blocks * MOE_BLOCK, D_MODEL), F32),
        compiler_params=_cparams(("arbitrary",)),
        name="dispatch",
    )(seg, dest_p.reshape(steps_p, 1, 2 * tm), dest_s.reshape(1, 1, 2 * n_s), h2p, h2s)


def _moe_kernel(blk_e_ref, nvalid_ref, next_e_ref, wslot_ref, x_ref, wg_hbm, wu_hbm, wd_hbm, y_ref,
                wg16, wu16, wd16, wg32, wu32, wd32, wsem):
    i = pl.program_id(0)
    e = blk_e_ref[i]
    used = nvalid_ref[i] > 0
    first_of_expert = used & ((i == 0) | (e != blk_e_ref[jnp.maximum(i - 1, 0)]))

    def fetch(expert, slot):
        return [pltpu.make_async_copy(src.at[expert], dst.at[slot], wsem.at[slot])
                for src, dst in ((wg_hbm, wg32), (wu_hbm, wu32), (wd_hbm, wd32))]

    for slot in range(2):
        @pl.when(first_of_expert & (wslot_ref[i] == slot))
        def _():
            @pl.when(i == 0)
            def _():
                for cp in fetch(e, slot):
                    cp.start()
            for cp in fetch(e, slot):
                cp.wait()
            wg16[...] = wg32[slot].astype(BF16)
            wu16[...] = wu32[slot].astype(BF16)
            wd16[...] = wd32[slot].astype(BF16)

            @pl.when(next_e_ref[i] < N_EXPERTS)
            def _():
                for cp in fetch(next_e_ref[i], 1 - slot):
                    cp.start()

    @pl.when(used)
    def _():
        xb = x_ref[...].astype(BF16)
        g = _dot(xb, wg16[...])
        u = _dot(xb, wu16[...])
        hmid = (_silu(g) * u).astype(BF16)
        y_ref[...] = _dot(hmid, wd16[...])

    @pl.when(nvalid_ref[i] == 0)
    def _():
        y_ref[...] = jnp.zeros_like(y_ref)


def _moe(blk_e, nvalid, next_e, wslot, x_sorted, w_gate, w_up, w_down):
    n_blocks = blk_e.shape[0]
    xmap = lambda i, be, nv, ne, ws: (jnp.where(nv[i] > 0, i, 0), 0)
    any_space = pl.BlockSpec(memory_space=pl.ANY)
    grid_spec = pltpu.PrefetchScalarGridSpec(
        num_scalar_prefetch=4,
        grid=(n_blocks,),
        in_specs=[pl.BlockSpec((MOE_BLOCK, D_MODEL), xmap), any_space, any_space, any_space],
        out_specs=pl.BlockSpec((MOE_BLOCK, D_MODEL), lambda i, be, nv, ne, ws: (i, 0)),
        scratch_shapes=[pltpu.VMEM((D_MODEL, D_EXPERT), BF16),
                        pltpu.VMEM((D_MODEL, D_EXPERT), BF16),
                        pltpu.VMEM((D_EXPERT, D_MODEL), BF16),
                        pltpu.VMEM((2, D_MODEL, D_EXPERT), F32),
                        pltpu.VMEM((2, D_MODEL, D_EXPERT), F32),
                        pltpu.VMEM((2, D_EXPERT, D_MODEL), F32),
                        pltpu.SemaphoreType.DMA((2,))])
    return pl.pallas_call(
        _moe_kernel,
        grid_spec=grid_spec,
        out_shape=jax.ShapeDtypeStruct((n_blocks * MOE_BLOCK, D_MODEL), F32),
        compiler_params=_cparams(("arbitrary",)),
        name="moe",
    )(blk_e, nvalid, next_e, wslot, x_sorted, w_gate, w_up, w_down)


def _combine_kernel(dcur_ref, dnext_ref, route_ref, x1_ref, mod_ref, g_ref, y_hbm, o_ref, ybuf, sems):
    i = pl.program_id(0)
    last = pl.num_programs(0) - 1
    tm = x1_ref.shape[0]

    def gather(dest_ref, buf):
        for t in range(tm):
            for slot in range(2):
                pltpu.make_async_copy(y_hbm.at[pl.ds(dest_ref[0, 0, 2 * t + slot], 1), :],
                                      ybuf.at[buf, slot, pl.ds(t, 1), :], sems.at[buf]).start(priority=slot)

    @pl.when(i == 0)
    def _():
        gather(dcur_ref, 0)

    for buf in range(2):
        @pl.when(i % 2 == buf)
        def _():
            @pl.when(i < last)
            def _():
                gather(dnext_ref, 1 - buf)

            for slot in range(2):
                pltpu.make_async_copy(y_hbm.at[pl.ds(0, tm), :], ybuf.at[buf, slot], sems.at[buf]).wait()
            route = route_ref[...]
            moe = ybuf[buf, 0] * route[:, 2:3] + ybuf[buf, 1] * route[:, 3:4]
            o_ref[...] = x1_ref[...] + _per_seq(_rms(moe, g_ref[...]), mod_ref[...], lambda y, m: y * m[:, 5:6])


def _combine(dest, route, y_sorted, x1, mod8, g_post_ffn, tm, seq_rows, mod_row0):
    n = x1.shape[0]
    steps = n // tm
    row = lambda i: (i, 0)
    dest3 = dest.reshape(steps, 1, 2 * tm)
    return pl.pallas_call(
        _combine_kernel,
        grid=(steps,),
        in_specs=[pl.BlockSpec((1, 1, 2 * tm), lambda i: (i, 0, 0), memory_space=pltpu.SMEM),
                  pl.BlockSpec((1, 1, 2 * tm), lambda i: (jnp.minimum(i + 1, steps - 1), 0, 0),
                               memory_space=pltpu.SMEM),
                  pl.BlockSpec((tm, LANES), row),
                  pl.BlockSpec((tm, D_MODEL), row),
                  _mod_spec(tm, seq_rows, mod_row0),
                  pl.BlockSpec((1, D_MODEL), lambda i: (0, 0)),
                  pl.BlockSpec(memory_space=pl.ANY)],
        out_specs=pl.BlockSpec((tm, D_MODEL), row),
        out_shape=jax.ShapeDtypeStruct((n, D_MODEL), F32),
        scratch_shapes=[pltpu.VMEM((2, 2, tm, D_MODEL), F32), pltpu.SemaphoreType.DMA((2,))],
        compiler_params=_cparams(("arbitrary",)),
        name="combine",
    )(dest3, dest3, route, x1, mod8, g_post_ffn, y_sorted)


def _segment_plan(counts_f, n_blocks):
    counts = counts_f[0, :N_EXPERTS].astype(jnp.int32)
    padded = (counts + MOE_BLOCK - 1) // MOE_BLOCK * MOE_BLOCK
    pad_end = jnp.cumsum(padded)
    pad_start = pad_end - padded
    blk_start = jnp.arange(n_blocks, dtype=jnp.int32) * MOE_BLOCK
    blk_e = jnp.minimum(jnp.sum((pad_end[None, :] <= blk_start[:, None]).astype(jnp.int32), axis=1),
                        N_EXPERTS - 1)
    onehot = blk_e[:, None] == jnp.arange(N_EXPERTS, dtype=jnp.int32)[None, :]
    c_blk = jnp.sum(jnp.where(onehot, counts[None, :], 0), axis=1)
    s_blk = jnp.sum(jnp.where(onehot, pad_start[None, :], 0), axis=1)
    nvalid = jnp.clip(c_blk - (blk_start - s_blk), 0, MOE_BLOCK).astype(jnp.int32)
    seg = jnp.concatenate([counts, pad_end]).astype(jnp.int32)
    ids = jnp.arange(N_EXPERTS, dtype=jnp.int32)
    later = (ids[None, :] > ids[:, None]) & (counts[None, :] > 0)
    next_e = jnp.min(jnp.where(later, ids[None, :], N_EXPERTS), axis=1)
    order = jnp.cumsum((counts > 0).astype(jnp.int32)) - 1
    pick = lambda table: jnp.sum(jnp.where(onehot, table[None, :], 0), axis=1).astype(jnp.int32)
    return blk_e.astype(jnp.int32), nvalid, pick(next_e), pick(order % 2), seg, pad_start


def _token_rows(route, pad_start):
    eid = route[:, 0:2].astype(jnp.int32)
    rank = route[:, 4:6].astype(jnp.int32)
    onehot = eid[:, :, None] == jnp.arange(N_EXPERTS, dtype=jnp.int32)[None, None, :]
    return rank + jnp.sum(jnp.where(onehot, pad_start[None, None, :], 0), axis=2)


def _layer(x_p, x_s, c_p, c_s, k_past, v_past, s0_s, conv_s, p):
    bp, tp, d = x_p.shape
    bs, ts, _ = x_s.shape
    n_p, n_s = bp * tp, bs * ts
    n_tok = n_p + n_s

    n_seq = bp + bs
    c_all = jnp.zeros((16, d), F32).at[:n_seq].set(jnp.concatenate([c_s, c_p], axis=0))
    mod = _ada(c_all, p['w_ada'], p['b_ada'])
    mod8 = jnp.pad(mod.reshape(16, 6, d), ((0, 0), (0, 2), (0, 0)))

    w_in = p['w_in']
    o_b = CONV_CH + DN_WIDTH
    o_q = o_b + 2 * DN_HEADS
    wb = jnp.pad(w_in[:, o_b:o_q], ((0, 0), (0, LANES - 2 * DN_HEADS)))
    wb_hi = wb.astype(BF16)
    wb_lo = (wb - wb_hi.astype(F32)).astype(BF16)
    w_main = jnp.concatenate([w_in[:, :o_b].astype(BF16), w_in[:, o_q:].astype(BF16), wb_hi, wb_lo], axis=1)
    g_pre_mix = p['g_pre_mix'].reshape(1, d)

    conv_w8 = jnp.pad(p['conv_w'], ((0, 8 - CONV_W), (0, 0)))
    pad_g = lambda a: jnp.pad(a.reshape(1, DN_HEADS), ((0, 0), (DN_HEADS, LANES - 2 * DN_HEADS)))
    alog_row, dtb_row = pad_g(p['a_log']), pad_g(p['dt_bias'])
    onorm_a = p['onorm_a'].reshape(1, DN_DIM)
    onorm_b2 = jnp.tile(p['onorm_b'].reshape(1, SB_DIM), (1, 2))

    w_out16 = p['w_out'].astype(BF16)
    wr = jnp.pad(jnp.concatenate([p['w_router_group'], p['w_router_expert']], axis=1),
                 ((0, 0), (0, LANES - N_GROUPS - N_EXPERTS)))
    wr_hi = wr.astype(BF16)
    wr_cat = jnp.concatenate([wr_hi, (wr - wr_hi.astype(F32)).astype(BF16)], axis=1)
    b_r = jnp.pad(jnp.concatenate([p['b_router_group'], p['b_router_expert']]).reshape(1, -1),
                  ((0, 0), (0, LANES - N_GROUPS - N_EXPERTS)))
    g_post_mix = p['g_post_mix'].reshape(1, d)
    g_pre_ffn = p['g_pre_ffn'].reshape(1, d)
    g_post_ffn = p['g_post_ffn'].reshape(1, d)

    def mixer(x, tm, tm_post, seq_rows, mod_row0, hist8, s0, k_old, v_old, chunk, n_chunks, bq, n_sub, cnt0):
        b, t, _ = x.shape
        x2d = x.reshape(b * t, d)
        a_in, z, bg, q16, kb, vb, k16, v16 = _proj(x2d, mod8, g_pre_mix, w_main, wb_hi, tm, seq_rows, mod_row0)
        r3 = lambda a: a.reshape(b, t, a.shape[-1])
        oa16, s_new = _delta(r3(a_in), r3(z), r3(bg), hist8, s0, conv_w8, alog_row, dtb_row, onorm_a,
                             chunk, n_chunks)
        k16, v16 = r3(k16), r3(v16)
        if k_old is not None:
            k16 = jnp.concatenate([k_old.reshape(b, -1, SB_WIDTH).astype(BF16), k16], axis=1)
            v16 = jnp.concatenate([v_old.reshape(b, -1, SB_WIDTH).astype(BF16), v16], axis=1)
        n_pad = (-k16.shape[1]) % KEY_TILE
        k16 = jnp.pad(k16, ((0, 0), (n_pad, 0), (0, 0)))
        v16 = jnp.pad(v16, ((0, 0), (n_pad, 0), (0, 0)))
        ob16 = _sb_attn(r3(q16), k16, v16, onorm_b2, bq, n_sub, n_pad)
        x1, h2, route, cnt = _post(oa16.reshape(b * t, DN_WIDTH), ob16.reshape(b * t, SB_WIDTH), x2d, mod8,
                                   g_post_mix, g_pre_ffn, w_out16, wr_cat, b_r, cnt0, tm_post, seq_rows,
                                   mod_row0)
        new_conv = r3(a_in)[:, t - (CONV_W - 1):, :]
        return (x1, h2, route, cnt, kb.reshape(b, t, SB_HEADS, SB_DIM), vb.reshape(b, t, SB_HEADS, SB_DIM),
                s_new, new_conv)

    zero_hist = jnp.zeros((bp, 8, CONV_CH), F32)
    zero_s = jnp.zeros((bp, DN_HEADS, DN_DIM, DN_DIM), F32)
    hist_s = jnp.pad(conv_s, ((0, 0), (8 - (CONV_W - 1), 0), (0, 0)))
    tm_p = min(ROW_TILE, tp)
    tm_dense = DENSE_TILE if n_p % DENSE_TILE == 0 else tm_p
    tm_router = ROUTER_TILE if n_p % ROUTER_TILE == 0 else tm_dense
    nc_p = max(1, min(8, tp // DELTA_BLOCK))
    x1p, h2p, rp, cnt_p, kp, vp, sp, cp = mixer(x_p, tm_dense, tm_router, tp, bs, zero_hist, zero_s, None, None,
                                                 min(DELTA_BLOCK, tp), nc_p, min(KEY_TILE // 2, tp), 2,
                                                 jnp.zeros((1, LANES), F32))
    x1s, h2s, rs, cnt, ks, vs, ss, cs = mixer(x_s, n_s, n_s, ts, 0, hist_s, s0_s, k_past, v_past,
                                               min(DELTA_BLOCK, ts), max(1, ts // DELTA_BLOCK),
                                               min(KEY_TILE, ts), 1, cnt_p)

    n_blocks = -(-2 * n_tok // MOE_BLOCK) + N_EXPERTS
    blk_e, nvalid, next_e, wslot, seg, pad_start = _segment_plan(cnt, n_blocks)
    dest_p = _token_rows(rp, pad_start)
    dest_s = _token_rows(rs, pad_start)
    x_sorted = _dispatch(seg, dest_p, dest_s, h2p, h2s, n_blocks, tm_p)
    y_sorted = _moe(blk_e, nvalid, next_e, wslot, x_sorted, p['w_gate'], p['w_up'], p['w_down'])
    y_p = _combine(dest_p, rp, y_sorted, x1p, mod8, g_post_ffn, tm_p, tp, bs).reshape(bp, tp, d)
    y_s = _combine(dest_s, rs, y_sorted, x1s, mod8, g_post_ffn, n_s, ts, 0).reshape(bs, ts, d)
    return y_p, y_s, kp, vp, sp, cp, ks, vs, ss, cs


def kernel(x_prompt, x_sample, c_prompt, c_sample, cache_k, cache_v, state_delta, state_conv, w_ada, b_ada, g_pre_mix, g_post_mix, g_pre_ffn, g_post_ffn, w_in, conv_w, a_log, dt_bias, onorm_a, onorm_b, w_out, w_router_group, b_router_group, w_router_expert, b_router_expert, w_gate, w_up, w_down):
    depth = w_in.shape[0]
    y_p, y_s = x_prompt, x_sample
    outs = [[] for _ in range(8)]
    for l in range(depth):
        p = dict(w_ada=w_ada[l], b_ada=b_ada[l], g_pre_mix=g_pre_mix[l], g_post_mix=g_post_mix[l],
                 g_pre_ffn=g_pre_ffn[l], g_post_ffn=g_post_ffn[l], w_in=w_in[l], conv_w=conv_w[l],
                 a_log=a_log[l], dt_bias=dt_bias[l], onorm_a=onorm_a[l], onorm_b=onorm_b[l],
                 w_out=w_out[l], w_router_group=w_router_group[l], b_router_group=b_router_group[l],
                 w_router_expert=w_router_expert[l], b_router_expert=b_router_expert[l],
                 w_gate=w_gate[l], w_up=w_up[l], w_down=w_down[l])
        res = _layer(y_p, y_s, c_prompt, c_sample, cache_k[l], cache_v[l], state_delta[l], state_conv[l], p)
        y_p, y_s = res[0], res[1]
        for lst, r in zip(outs, res[2:]):
            lst.append(r)
    return (y_p, y_s) + tuple(jnp.stack(o) for o in outs)
```
